```python
import jax, jax.numpy as jnp
from jax import lax
import numpy as np

D_MODEL = 2048
BATCH = 8
SEQ = 4096
DEPTH = 1

MEM_TOKENS = 256
HEAD_DIM = 64
N_Q_HEADS = 16
N_KV_HEADS = 4
Q_PER_KV = N_Q_HEADS // N_KV_HEADS
ATTN_WIDTH = N_Q_HEADS * HEAD_DIM
KV_WIDTH = N_KV_HEADS * HEAD_DIM
WINDOW = 128
BLOCK = 128
ROPE_THETA = 10000.0
CONV_WIDTH = 1024
CONV_K = 3
X_HEADS = 4
X_HEAD_DIM = 128
X_WIDTH = X_HEADS * X_HEAD_DIM
FFN_HIDDEN = -(-(8 * D_MODEL) // (3 * 256)) * 256
EPS = 1e-6
IN_SIZES = (ATTN_WIDTH, KV_WIDTH, KV_WIDTH, CONV_WIDTH, CONV_WIDTH, CONV_WIDTH, D_MODEL, D_MODEL)
IN_WIDTH = ATTN_WIDTH + 2 * KV_WIDTH + 3 * CONV_WIDTH + 2 * D_MODEL

kernel_name = "hybrid_gated_swa_shortconv_xattn_block"


def rms_norm(x, g):
    xf = x.astype(jnp.float32)
    y = xf * lax.rsqrt(jnp.mean(xf * xf, axis=-1, keepdims=True) + EPS)
    return (y * g.astype(jnp.float32)).astype(x.dtype)


def rope(x, positions):
    half = HEAD_DIM // 2
    inv_freq = ROPE_THETA ** (-jnp.arange(half, dtype=jnp.float32) / half)
    ang = positions.astype(jnp.float32)[:, None] * inv_freq[None, :]
    cos = jnp.cos(ang)[None, :, None, :]
    sin = jnp.sin(ang)[None, :, None, :]
    xf = x.astype(jnp.float32)
    x1, x2 = xf[..., :half], xf[..., half:]
    out = jnp.concatenate([x1 * cos - x2 * sin, x2 * cos + x1 * sin], axis=-1)
    return out.astype(x.dtype)


def _with_prev_block(t):
    pad = [(0, 0)] * t.ndim
    pad[1] = (1, 0)
    prev = jnp.pad(t, pad)[:, :-1]
    return jnp.concatenate([prev, t], axis=2)


def sliding_window_attention(q, k, v, sinks):
    b, t = q.shape[0], q.shape[1]
    nb = t // BLOCK
    qb = q.reshape(b, nb, BLOCK, N_KV_HEADS, Q_PER_KV, HEAD_DIM)
    kband = _with_prev_block(k.reshape(b, nb, BLOCK, N_KV_HEADS, HEAD_DIM))
    vband = _with_prev_block(v.reshape(b, nb, BLOCK, N_KV_HEADS, HEAD_DIM))
    scale = HEAD_DIM ** -0.5
    s = jnp.einsum('bnqhgd,bnkhd->bnhgqk', qb, kband).astype(jnp.float32) * scale
    blk = jnp.arange(nb)[:, None]
    q_pos = blk * BLOCK + jnp.arange(BLOCK)[None, :]
    k_pos = (blk - 1) * BLOCK + jnp.arange(2 * BLOCK)[None, :]
    diff = q_pos[:, :, None] - k_pos[:, None, :]
    valid = (diff >= 0) & (diff < WINDOW) & (k_pos[:, None, :] >= 0)
    s = jnp.where(valid[None, :, None, None, :, :], s, -jnp.inf)
    sink = sinks.astype(jnp.float32).reshape(N_KV_HEADS, Q_PER_KV)[None, None, :, :, None, None]
    m = jnp.maximum(jnp.max(s, axis=-1, keepdims=True), sink)
    p = jnp.exp(s - m)
    p = p / (jnp.sum(p, axis=-1, keepdims=True) + jnp.exp(sink - m))
    o = jnp.einsum('bnhgqk,bnkhd->bnqhgd', p.astype(v.dtype), vband)
    return o.reshape(b, t, ATTN_WIDTH)


def short_gated_conv(z, gate_b, gate_c, conv_w):
    t = z.shape[1]
    cz = gate_c * z
    zp = jnp.pad(cz, ((0, 0), (CONV_K - 1, 0), (0, 0)))
    y = conv_w[0] * zp[:, 0:t]
    for j in range(1, CONV_K):
        y = y + conv_w[j] * zp[:, j:j + t]
    return gate_b * y


def cross_attention(u, mem_n, w_xq, w_xkv, w_xo):
    b, t = u.shape[0], u.shape[1]
    q = (u @ w_xq).reshape(b, t, X_HEADS, X_HEAD_DIM)
    kv = mem_n @ w_xkv
    k = kv[..., :X_WIDTH].reshape(b, -1, X_HEADS, X_HEAD_DIM)
    v = kv[..., X_WIDTH:].reshape(b, -1, X_HEADS, X_HEAD_DIM)
    s = jnp.einsum('bthd,bmhd->bhtm', q, k).astype(jnp.float32) * (X_HEAD_DIM ** -0.5)
    p = jax.nn.softmax(s, axis=-1)
    o = jnp.einsum('bhtm,bmhd->bthd', p.astype(v.dtype), v).reshape(b, t, X_WIDTH)
    return o @ w_xo


def _fwd_setup_inputs(seed: int = 0) -> dict:
    key = jax.random.key(seed)
    ks = jax.random.split(key, 20)
    f32 = jnp.float32

    def w(k, shape, fan_in):
        return jax.random.normal(k, shape, f32) * (fan_in ** -0.5)

    def gain(k, shape):
        return 1.0 + 0.01 * jax.random.normal(k, shape, f32)

    return {
        "x": jax.random.normal(ks[0], (BATCH, SEQ, D_MODEL), f32),
        "mem": jax.random.normal(ks[1], (BATCH, MEM_TOKENS, D_MODEL), f32),
        "g_mix": gain(ks[2], (DEPTH, D_MODEL)),
        "w_in": w(ks[3], (DEPTH, D_MODEL, IN_WIDTH), D_MODEL),
        "conv_w": w(ks[4], (DEPTH, CONV_K, CONV_WIDTH), CONV_K),
        "attn_sinks": 0.5 * jax.random.normal(ks[5], (DEPTH, N_Q_HEADS), f32),
        "w_attn_proj": w(ks[6], (DEPTH, ATTN_WIDTH, D_MODEL), ATTN_WIDTH),
        "w_conv_proj": w(ks[7], (DEPTH, CONV_WIDTH, D_MODEL), CONV_WIDTH),
        "w_mix_out": w(ks[8], (DEPTH, D_MODEL, D_MODEL), D_MODEL),
        "g_xattn": gain(ks[9], (DEPTH, D_MODEL)),
        "g_mem": gain(ks[10], (DEPTH, D_MODEL)),
        "w_xq": w(ks[11], (DEPTH, D_MODEL, X_WIDTH), D_MODEL),
        "w_xkv": w(ks[12], (DEPTH, D_MODEL, 2 * X_WIDTH), D_MODEL),
        "w_xo": w(ks[13], (DEPTH, X_WIDTH, D_MODEL), X_WIDTH),
        "g_ffn": gain(ks[14], (DEPTH, D_MODEL)),
        "w_ffn_in": w(ks[15], (DEPTH, D_MODEL, 2 * FFN_HIDDEN), D_MODEL),
        "w_ffn_out": w(ks[16], (DEPTH, FFN_HIDDEN, D_MODEL), FFN_HIDDEN),
        "g_final": gain(ks[17], (D_MODEL,)),
    }


def _fwd_reference(x, mem, g_mix, w_in, conv_w, attn_sinks, w_attn_proj, w_conv_proj, w_mix_out,
              g_xattn, g_mem, w_xq, w_xkv, w_xo, g_ffn, w_ffn_in, w_ffn_out, g_final):
    b, t = x.shape[0], x.shape[1]
    positions = jnp.arange(t, dtype=jnp.int32)
    split_points = np.cumsum(IN_SIZES)[:-1].tolist()
    h = x
    for l in range(DEPTH):
        u = rms_norm(h, g_mix[l])
        proj = u @ w_in[l]
        q, k, v, z, gb, gc, gate_a, gate_c = jnp.split(proj, split_points, axis=-1)
        q = rope(q.reshape(b, t, N_Q_HEADS, HEAD_DIM), positions)
        k = rope(k.reshape(b, t, N_KV_HEADS, HEAD_DIM), positions)
        v = v.reshape(b, t, N_KV_HEADS, HEAD_DIM)
        y_attn = sliding_window_attention(q, k, v, attn_sinks[l]) @ w_attn_proj[l]
        y_conv = short_gated_conv(z, gb, gc, conv_w[l]) @ w_conv_proj[l]
        merged = jax.nn.sigmoid(gate_a) * y_attn + jax.nn.sigmoid(gate_c) * y_conv
        h = h + merged @ w_mix_out[l]
        u = rms_norm(h, g_xattn[l])
        mem_n = rms_norm(mem, g_mem[l])
        h = h + cross_attention(u, mem_n, w_xq[l], w_xkv[l], w_xo[l])
        u = rms_norm(h, g_ffn[l])
        hid = u @ w_ffn_in[l]
        h = h + (jax.nn.silu(hid[..., :FFN_HIDDEN]) * hid[..., FFN_HIDDEN:]) @ w_ffn_out[l]
    return rms_norm(h, g_final)


import jax as _jax
import jax.numpy as _jnp

TWIN_FORMAT = 'train_step'
FWD_PARAMS = ['x', 'mem', 'g_mix', 'w_in', 'conv_w', 'attn_sinks', 'w_attn_proj', 'w_conv_proj', 'w_mix_out', 'g_xattn', 'g_mem', 'w_xq', 'w_xkv', 'w_xo', 'g_ffn', 'w_ffn_in', 'w_ffn_out', 'g_final']
TWIN_WEIGHTS = ['g_mix', 'w_in', 'conv_w', 'attn_sinks', 'w_attn_proj', 'w_conv_proj', 'w_mix_out', 'g_xattn', 'g_mem', 'w_xq', 'w_xkv', 'w_xo', 'g_ffn', 'w_ffn_in', 'w_ffn_out', 'g_final']
TWIN_DIFF_INPUT = 'x'
TWIN_INPUTS = ['x', 'mem', 'g_mix', 'w_in', 'conv_w', 'attn_sinks', 'w_attn_proj', 'w_conv_proj', 'w_mix_out', 'g_xattn', 'g_mem', 'w_xq', 'w_xkv', 'w_xo', 'g_ffn', 'w_ffn_in', 'w_ffn_out', 'g_final', 'loss_target', 'm_g_mix', 'm_w_in', 'm_conv_w', 'm_attn_sinks', 'm_w_attn_proj', 'm_w_conv_proj', 'm_w_mix_out', 'm_g_xattn', 'm_g_mem', 'm_w_xq', 'm_w_xkv', 'm_w_xo', 'm_g_ffn', 'm_w_ffn_in', 'm_w_ffn_out', 'm_g_final', 'v_g_mix', 'v_w_in', 'v_conv_w', 'v_attn_sinks', 'v_w_attn_proj', 'v_w_conv_proj', 'v_w_mix_out', 'v_g_xattn', 'v_g_mem', 'v_w_xq', 'v_w_xkv', 'v_w_xo', 'v_g_ffn', 'v_w_ffn_in', 'v_w_ffn_out', 'v_g_final']
TWIN_OUTPUTS = ['loss', 'grad_x', 'grad_g_mix', 'grad_w_in', 'grad_conv_w', 'grad_attn_sinks', 'grad_w_attn_proj', 'grad_w_conv_proj', 'grad_w_mix_out', 'grad_g_xattn', 'grad_g_mem', 'grad_w_xq', 'grad_w_xkv', 'grad_w_xo', 'grad_g_ffn', 'grad_w_ffn_in', 'grad_w_ffn_out', 'grad_g_final', 'delta_g_mix', 'delta_w_in', 'delta_conv_w', 'delta_attn_sinks', 'delta_w_attn_proj', 'delta_w_conv_proj', 'delta_w_mix_out', 'delta_g_xattn', 'delta_g_mem', 'delta_w_xq', 'delta_w_xkv', 'delta_w_xo', 'delta_g_ffn', 'delta_w_ffn_in', 'delta_w_ffn_out', 'delta_g_final', 'new_m_g_mix', 'new_m_w_in', 'new_m_conv_w', 'new_m_attn_sinks', 'new_m_w_attn_proj', 'new_m_w_conv_proj', 'new_m_w_mix_out', 'new_m_g_xattn', 'new_m_g_mem', 'new_m_w_xq', 'new_m_w_xkv', 'new_m_w_xo', 'new_m_g_ffn', 'new_m_w_ffn_in', 'new_m_w_ffn_out', 'new_m_g_final', 'new_v_g_mix', 'new_v_w_in', 'new_v_conv_w', 'new_v_attn_sinks', 'new_v_w_attn_proj', 'new_v_w_conv_proj', 'new_v_w_mix_out', 'new_v_g_xattn', 'new_v_g_mem', 'new_v_w_xq', 'new_v_w_xkv', 'new_v_w_xo', 'new_v_g_ffn', 'new_v_w_ffn_in', 'new_v_w_ffn_out', 'new_v_g_final']
TWIN_LEAF_KINDS = {'loss': 'loss', 'grad_x': 'grad_x', 'grad_g_mix': 'grad_w', 'grad_w_in': 'grad_w', 'grad_conv_w': 'grad_w', 'grad_attn_sinks': 'grad_w', 'grad_w_attn_proj': 'grad_w', 'grad_w_conv_proj': 'grad_w', 'grad_w_mix_out': 'grad_w', 'grad_g_xattn': 'grad_w', 'grad_g_mem': 'grad_w', 'grad_w_xq': 'grad_w', 'grad_w_xkv': 'grad_w', 'grad_w_xo': 'grad_w', 'grad_g_ffn': 'grad_w', 'grad_w_ffn_in': 'grad_w', 'grad_w_ffn_out': 'grad_w', 'grad_g_final': 'grad_w', 'delta_g_mix': 'delta_w', 'delta_w_in': 'delta_w', 'delta_conv_w': 'delta_w', 'delta_attn_sinks': 'delta_w', 'delta_w_attn_proj': 'delta_w', 'delta_w_conv_proj': 'delta_w', 'delta_w_mix_out': 'delta_w', 'delta_g_xattn': 'delta_w', 'delta_g_mem': 'delta_w', 'delta_w_xq': 'delta_w', 'delta_w_xkv': 'delta_w', 'delta_w_xo': 'delta_w', 'delta_g_ffn': 'delta_w', 'delta_w_ffn_in': 'delta_w', 'delta_w_ffn_out': 'delta_w', 'delta_g_final': 'delta_w', 'new_m_g_mix': 'new_m', 'new_m_w_in': 'new_m', 'new_m_conv_w': 'new_m', 'new_m_attn_sinks': 'new_m', 'new_m_w_attn_proj': 'new_m', 'new_m_w_conv_proj': 'new_m', 'new_m_w_mix_out': 'new_m', 'new_m_g_xattn': 'new_m', 'new_m_g_mem': 'new_m', 'new_m_w_xq': 'new_m', 'new_m_w_xkv': 'new_m', 'new_m_w_xo': 'new_m', 'new_m_g_ffn': 'new_m', 'new_m_w_ffn_in': 'new_m', 'new_m_w_ffn_out': 'new_m', 'new_m_g_final': 'new_m', 'new_v_g_mix': 'new_v', 'new_v_w_in': 'new_v', 'new_v_conv_w': 'new_v', 'new_v_attn_sinks': 'new_v', 'new_v_w_attn_proj': 'new_v', 'new_v_w_conv_proj': 'new_v', 'new_v_w_mix_out': 'new_v', 'new_v_g_xattn': 'new_v', 'new_v_g_mem': 'new_v', 'new_v_w_xq': 'new_v', 'new_v_w_xkv': 'new_v', 'new_v_w_xo': 'new_v', 'new_v_g_ffn': 'new_v', 'new_v_w_ffn_in': 'new_v', 'new_v_w_ffn_out': 'new_v', 'new_v_g_final': 'new_v'}


def _forward(args):
    return _fwd_reference(*[args[k] for k in FWD_PARAMS])


def _output_shape():
    def fwd():
        inp = _fwd_setup_inputs(0)
        return _fwd_reference(*[inp[k] for k in FWD_PARAMS])
    out = _jax.eval_shape(fwd)
    return out.shape, out.dtype

N_MICROBATCH = 1
ADAM_LR = 0.001
ADAM_B1 = 0.9
ADAM_B2 = 0.999
ADAM_EPS = 1e-08
ADAM_WD = 0.01
ADAM_STEP = 10
PER_EXAMPLE_BATCH_AXIS = {'x': 0, 'mem': 0, 'loss_target': 0}
SHARED_INPUTS = []
_WEIGHT_DTYPES = {'g_mix': _jnp.float32, 'w_in': _jnp.float32, 'conv_w': _jnp.float32, 'attn_sinks': _jnp.float32, 'w_attn_proj': _jnp.float32, 'w_conv_proj': _jnp.float32, 'w_mix_out': _jnp.float32, 'g_xattn': _jnp.float32, 'g_mem': _jnp.float32, 'w_xq': _jnp.float32, 'w_xkv': _jnp.float32, 'w_xo': _jnp.float32, 'g_ffn': _jnp.float32, 'w_ffn_in': _jnp.float32, 'w_ffn_out': _jnp.float32, 'g_final': _jnp.float32}
MOMENT_SCALE = {'g_mix': 8.556976e-02, 'w_in': 4.078530e-02, 'conv_w': 6.664035e-02, 'attn_sinks': 1.384512e-02, 'w_attn_proj': 9.343166e-03, 'w_conv_proj': 4.658457e-02, 'w_mix_out': 4.732878e-02, 'g_xattn': 9.006894e-03, 'g_mem': 1.287129e-02, 'w_xq': 1.756741e-02, 'w_xkv': 1.781453e-02, 'w_xo': 8.888891e-03, 'g_ffn': 5.914176e-02, 'w_ffn_in': 2.507628e-02, 'w_ffn_out': 4.093456e-02, 'g_final': 1.599235e+01}


def _to_microbatches(a, axis):
    t = _jnp.moveaxis(a, axis, 0)
    t = t.reshape((N_MICROBATCH, t.shape[0] // N_MICROBATCH) + t.shape[1:])
    return _jnp.moveaxis(t, 1, axis + 1)


def setup_inputs(seed: int = 0) -> dict:
    inp = _fwd_setup_inputs(seed)
    key = _jax.random.fold_in(_jax.random.key(seed), 7919)
    shape, _ = _output_shape()
    out = dict(inp)
    out["loss_target"] = _jax.random.normal(_jax.random.fold_in(key, 0), shape, _jnp.float32)
    for i, name in enumerate(TWIN_WEIGHTS):
        w = inp[name].astype(_jnp.float32)
        if MOMENT_SCALE is None:
            s = _jnp.sqrt(_jnp.mean(_jnp.square(w)) + 1e-30)
        else:
            s = MOMENT_SCALE[name]
        km, kv = _jax.random.split(_jax.random.fold_in(key, i + 1))
        out[name] = w
        out["m_" + name] = s * _jax.random.normal(km, w.shape, _jnp.float32)
        out["v_" + name] = (s * s) * _jax.random.uniform(kv, w.shape, _jnp.float32, 0.5, 1.5)
    if N_MICROBATCH > 1:
        for name, axis in PER_EXAMPLE_BATCH_AXIS.items():
            out[name] = _to_microbatches(out[name], axis)
    return {'x': out['x'], 'mem': out['mem'], 'g_mix': out['g_mix'], 'w_in': out['w_in'], 'conv_w': out['conv_w'], 'attn_sinks': out['attn_sinks'], 'w_attn_proj': out['w_attn_proj'], 'w_conv_proj': out['w_conv_proj'], 'w_mix_out': out['w_mix_out'], 'g_xattn': out['g_xattn'], 'g_mem': out['g_mem'], 'w_xq': out['w_xq'], 'w_xkv': out['w_xkv'], 'w_xo': out['w_xo'], 'g_ffn': out['g_ffn'], 'w_ffn_in': out['w_ffn_in'], 'w_ffn_out': out['w_ffn_out'], 'g_final': out['g_final'], 'loss_target': out['loss_target'], 'm_g_mix': out['m_g_mix'], 'm_w_in': out['m_w_in'], 'm_conv_w': out['m_conv_w'], 'm_attn_sinks': out['m_attn_sinks'], 'm_w_attn_proj': out['m_w_attn_proj'], 'm_w_conv_proj': out['m_w_conv_proj'], 'm_w_mix_out': out['m_w_mix_out'], 'm_g_xattn': out['m_g_xattn'], 'm_g_mem': out['m_g_mem'], 'm_w_xq': out['m_w_xq'], 'm_w_xkv': out['m_w_xkv'], 'm_w_xo': out['m_w_xo'], 'm_g_ffn': out['m_g_ffn'], 'm_w_ffn_in': out['m_w_ffn_in'], 'm_w_ffn_out': out['m_w_ffn_out'], 'm_g_final': out['m_g_final'], 'v_g_mix': out['v_g_mix'], 'v_w_in': out['v_w_in'], 'v_conv_w': out['v_conv_w'], 'v_attn_sinks': out['v_attn_sinks'], 'v_w_attn_proj': out['v_w_attn_proj'], 'v_w_conv_proj': out['v_w_conv_proj'], 'v_w_mix_out': out['v_w_mix_out'], 'v_g_xattn': out['v_g_xattn'], 'v_g_mem': out['v_g_mem'], 'v_w_xq': out['v_w_xq'], 'v_w_xkv': out['v_w_xkv'], 'v_w_xo': out['v_w_xo'], 'v_g_ffn': out['v_g_ffn'], 'v_w_ffn_in': out['v_w_ffn_in'], 'v_w_ffn_out': out['v_w_ffn_out'], 'v_g_final': out['v_g_final']}


def _loss(weights, diff, rest, loss_target):
    with _jax.named_scope("forward"):
        args = {**rest, TWIN_DIFF_INPUT: diff, **{k: w.astype(_WEIGHT_DTYPES[k]) for k, w in weights.items()}}
        y = _forward(args)
    with _jax.named_scope("loss_head"):
        err = _jnp.square(y.astype(_jnp.float32) - loss_target)
        return 0.5 * _jnp.sum(_jnp.mean(err, axis=-1)) if err.ndim else 0.5 * err


def _adamw(w, g, m, v):
    m = ADAM_B1 * m + (1.0 - ADAM_B1) * g
    v = ADAM_B2 * v + (1.0 - ADAM_B2) * _jnp.square(g)
    m_hat = m / (1.0 - ADAM_B1 ** ADAM_STEP)
    v_hat = v / (1.0 - ADAM_B2 ** ADAM_STEP)
    delta = -ADAM_LR * (m_hat / (_jnp.sqrt(v_hat) + ADAM_EPS) + ADAM_WD * w)
    return delta, m, v


def reference(x, mem, g_mix, w_in, conv_w, attn_sinks, w_attn_proj, w_conv_proj, w_mix_out, g_xattn, g_mem, w_xq, w_xkv, w_xo, g_ffn, w_ffn_in, w_ffn_out, g_final, loss_target, m_g_mix, m_w_in, m_conv_w, m_attn_sinks, m_w_attn_proj, m_w_conv_proj, m_w_mix_out, m_g_xattn, m_g_mem, m_w_xq, m_w_xkv, m_w_xo, m_g_ffn, m_w_ffn_in, m_w_ffn_out, m_g_final, v_g_mix, v_w_in, v_conv_w, v_attn_sinks, v_w_attn_proj, v_w_conv_proj, v_w_mix_out, v_g_xattn, v_g_mem, v_w_xq, v_w_xkv, v_w_xo, v_g_ffn, v_w_ffn_in, v_w_ffn_out, v_g_final):
    given = dict(x=x, mem=mem, g_mix=g_mix, w_in=w_in, conv_w=conv_w, attn_sinks=attn_sinks, w_attn_proj=w_attn_proj, w_conv_proj=w_conv_proj, w_mix_out=w_mix_out, g_xattn=g_xattn, g_mem=g_mem, w_xq=w_xq, w_xkv=w_xkv, w_xo=w_xo, g_ffn=g_ffn, w_ffn_in=w_ffn_in, w_ffn_out=w_ffn_out, g_final=g_final, loss_target=loss_target, m_g_mix=m_g_mix, m_w_in=m_w_in, m_conv_w=m_conv_w, m_attn_sinks=m_attn_sinks, m_w_attn_proj=m_w_attn_proj, m_w_conv_proj=m_w_conv_proj, m_w_mix_out=m_w_mix_out, m_g_xattn=m_g_xattn, m_g_mem=m_g_mem, m_w_xq=m_w_xq, m_w_xkv=m_w_xkv, m_w_xo=m_w_xo, m_g_ffn=m_g_ffn, m_w_ffn_in=m_w_ffn_in, m_w_ffn_out=m_w_ffn_out, m_g_final=m_g_final, v_g_mix=v_g_mix, v_w_in=v_w_in, v_conv_w=v_conv_w, v_attn_sinks=v_attn_sinks, v_w_attn_proj=v_w_attn_proj, v_w_conv_proj=v_w_conv_proj, v_w_mix_out=v_w_mix_out, v_g_xattn=v_g_xattn, v_g_mem=v_g_mem, v_w_xq=v_w_xq, v_w_xkv=v_w_xkv, v_w_xo=v_w_xo, v_g_ffn=v_g_ffn, v_w_ffn_in=v_w_ffn_in, v_w_ffn_out=v_w_ffn_out, v_g_final=v_g_final)
    weights = {n: given[n] for n in TWIN_WEIGHTS}
    shared = {n: given[n] for n in SHARED_INPUTS}
    per_example = {n: given[n] for n in ['x', 'mem']}
    grad_fn = _jax.value_and_grad(_loss, argnums=(0, 1))

    def one_microbatch(ex, loss_target):
        ex = dict(ex)
        diff = ex.pop(TWIN_DIFF_INPUT)
        return grad_fn(weights, diff, {**shared, **ex}, loss_target)

    if N_MICROBATCH == 1:
        loss, (grad_w, grad_x) = one_microbatch(per_example, given["loss_target"])
    else:
        def body(carry, xs):
            loss_sum, grad_sum = carry
            l_k, (gw_k, gx_k) = one_microbatch(xs[0], xs[1])
            with _jax.named_scope("update"):
                return (loss_sum + l_k, _jax.tree.map(_jnp.add, grad_sum, gw_k)), gx_k

        init = (_jnp.zeros((), _jnp.float32), _jax.tree.map(_jnp.zeros_like, weights))
        (loss, grad_w), grad_x = _jax.lax.scan(body, init, (per_example, given["loss_target"]))
    with _jax.named_scope("update"):
        delta_w, new_m, new_v = {}, {}, {}
        for n in TWIN_WEIGHTS:
            delta_w[n], new_m[n], new_v[n] = _adamw(weights[n], grad_w[n], given["m_" + n], given["v_" + n])
    return (loss, grad_x, *[grad_w[n] for n in TWIN_WEIGHTS], *[delta_w[n] for n in TWIN_WEIGHTS],
            *[new_m[n] for n in TWIN_WEIGHTS], *[new_v[n] for n in TWIN_WEIGHTS])
```

```python
import functools
import math

import jax
import jax.numpy as jnp
from jax import lax
from jax.experimental import pallas as pl
from jax.experimental.pallas import tpu as pltpu

HEAD_DIM = 64
Q_PER_KV = 4
WINDOW = 128
X_HEAD_DIM = 128
ROPE_THETA = 10000.0
EPS = 1e-6
ADAM_LR = 0.001
ADAM_B1 = 0.9
ADAM_B2 = 0.999
ADAM_EPS = 1e-08
ADAM_WD = 0.01
ADAM_STEP = 10

N_DEV = 8
LANES = 128
SUBLANES = 8
VMEM_LIMIT_BYTES = 56 * 1024 * 1024
BF = jnp.bfloat16
F32 = jnp.float32
MESH = pl.DeviceIdType.MESH


def _cp(*sem):
    return pltpu.CompilerParams(dimension_semantics=sem, vmem_limit_bytes=VMEM_LIMIT_BYTES)


def _sigmoid(x):
    return 1.0 / (1.0 + jnp.exp(-x))


_DIMS = {
    "nn": (((1,), (0,)), ((), ())),
    "nt": (((1,), (1,)), ((), ())),
    "tn": (((0,), (0,)), ((), ())),
}


def _fit(dim, tile):
    tile = min(tile, dim)
    while dim % tile and tile > LANES:
        tile //= 2
    return tile


def _mm(a, b, *, mode, tm, tn, tk, out_dtype, name, residual=None):
    if mode == "nn":
        (m, k), (k2, n) = a.shape, b.shape
    elif mode == "nt":
        (m, k), (n, k2) = a.shape, b.shape
    else:
        (k, m), (k2, n) = a.shape, b.shape
    assert k == k2, (name, a.shape, b.shape)
    tm, tn, tk = _fit(m, tm), _fit(n, tn), _fit(k, tk)
    assert m % tm == 0 and n % tn == 0 and k % tk == 0, (name, m, n, k, tm, tn, tk)
    nk = k // tk
    if mode == "tn":
        a_spec = pl.BlockSpec((tk, tm), lambda i, j, kk: (kk, i))
    else:
        a_spec = pl.BlockSpec((tm, tk), lambda i, j, kk: (i, kk))
    if mode == "nt":
        b_spec = pl.BlockSpec((tn, tk), lambda i, j, kk: (j, kk))
    else:
        b_spec = pl.BlockSpec((tk, tn), lambda i, j, kk: (kk, j))
    o_spec = pl.BlockSpec((tm, tn), lambda i, j, kk: (i, j))
    dims = _DIMS[mode]
    has_res = residual is not None

    def body(*refs):
        if has_res:
            a_ref, b_ref, r_ref, o_ref = refs[:4]
        else:
            a_ref, b_ref, o_ref = refs[:3]
        part = lax.dot_general(a_ref[...].astype(BF), b_ref[...].astype(BF), dims, preferred_element_type=F32)

        def finish(acc):
            if has_res:
                acc = r_ref[...] + acc
            o_ref[...] = acc.astype(out_dtype)

        if nk == 1:
            finish(part)
        else:
            acc_ref = refs[-1]
            kk = pl.program_id(2)

            @pl.when(kk == 0)
            def _():
                acc_ref[...] = part

            @pl.when(kk > 0)
            def _():
                acc_ref[...] += part

            @pl.when(kk == nk - 1)
            def _():
                finish(acc_ref[...])

    in_specs = [a_spec, b_spec] + ([o_spec] if has_res else [])
    args = (a, b) + ((residual,) if has_res else ())
    return pl.pallas_call(
        body,
        name=name,
        grid=(m // tm, n // tn, nk),
        in_specs=in_specs,
        out_specs=o_spec,
        out_shape=jax.ShapeDtypeStruct((m, n), out_dtype),
        scratch_shapes=[pltpu.VMEM((tm, tn), F32)] if nk > 1 else [],
        compiler_params=_cp("parallel", "parallel", "arbitrary"),
    )(*args)


def _rms_fwd(h, g, *, name, tm=512):
    t, d = h.shape
    tm = min(tm, t)

    def body(h_ref, g_ref, u_ref):
        hv = h_ref[...]
        r = lax.rsqrt(jnp.mean(hv * hv, axis=-1, keepdims=True) + EPS)
        u_ref[...] = ((hv * r) * g_ref[...]).astype(BF)

    return pl.pallas_call(
        body,
        name=name,
        grid=(t // tm,),
        in_specs=[pl.BlockSpec((tm, d), lambda i: (i, 0)), pl.BlockSpec((1, d), lambda i: (0, 0))],
        out_specs=pl.BlockSpec((tm, d), lambda i: (i, 0)),
        out_shape=jax.ShapeDtypeStruct((t, d), BF),
        compiler_params=_cp("parallel"),
    )(h, g)


def _rms_bwd(du, h, g, dres, *, name, tm=256):
    t, d = h.shape
    tm = min(tm, t)
    want_dh = dres is not None

    def body(*refs):
        if want_dh:
            du_ref, h_ref, g_ref, dres_ref, dh_ref, dg_ref = refs
        else:
            du_ref, h_ref, g_ref, dg_ref = refs
        i = pl.program_id(0)
        hv = h_ref[...]
        duv = du_ref[...]
        r = lax.rsqrt(jnp.mean(hv * hv, axis=-1, keepdims=True) + EPS)
        nv = hv * r
        if want_dh:
            gy = duv * g_ref[...]
            dh_ref[...] = dres_ref[...] + r * (gy - nv * jnp.mean(nv * gy, axis=-1, keepdims=True))

        @pl.when(i == 0)
        def _():
            dg_ref[...] = jnp.zeros_like(dg_ref)

        dg_ref[...] += jnp.sum(duv * nv, axis=0, keepdims=True)

    row = pl.BlockSpec((tm, d), lambda i: (i, 0))
    vec = pl.BlockSpec((1, d), lambda i: (0, 0))
    if want_dh:
        in_specs, args = [row, row, vec, row], (du, h, g, dres)
        out_specs = [row, vec]
        out_shape = [jax.ShapeDtypeStruct((t, d), F32), jax.ShapeDtypeStruct((1, d), F32)]
    else:
        in_specs, args = [row, row, vec], (du, h, g)
        out_specs = [vec]
        out_shape = [jax.ShapeDtypeStruct((1, d), F32)]
    outs = pl.pallas_call(
        body,
        name=name,
        grid=(t // tm,),
        in_specs=in_specs,
        out_specs=out_specs,
        out_shape=out_shape,
        compiler_params=_cp("arbitrary"),
    )(*args)
    return (outs[0], outs[1]) if want_dh else (None, outs[0])


def _loss_head(h, target, g, *, name, tm=256):
    t, d = h.shape
    tm = min(tm, t)

    def body(h_ref, t_ref, g_ref, dh_ref, loss_ref, dg_ref):
        i = pl.program_id(0)
        hv = h_ref[...]
        gv = g_ref[...]
        r = lax.rsqrt(jnp.mean(hv * hv, axis=-1, keepdims=True) + EPS)
        nv = hv * r
        e = nv * gv - t_ref[...]
        per_tok = jnp.mean(e * e, axis=-1, keepdims=True)
        lp = 0.5 * jnp.sum(per_tok, axis=0, keepdims=True)
        dy = e * (1.0 / d)
        gy = dy * gv
        dh_ref[...] = r * (gy - nv * jnp.mean(nv * gy, axis=-1, keepdims=True))

        @pl.when(i == 0)
        def _():
            loss_ref[...] = jnp.zeros_like(loss_ref)
            dg_ref[...] = jnp.zeros_like(dg_ref)

        loss_ref[...] += jnp.broadcast_to(lp, loss_ref.shape)
        dg_ref[...] += jnp.sum(dy * nv, axis=0, keepdims=True)

    row = pl.BlockSpec((tm, d), lambda i: (i, 0))
    vec = pl.BlockSpec((1, d), lambda i: (0, 0))
    return pl.pallas_call(
        body,
        name=name,
        grid=(t // tm,),
        in_specs=[row, row, vec],
        out_specs=[row, pl.BlockSpec((SUBLANES, LANES), lambda i: (0, 0)), vec],
        out_shape=[
            jax.ShapeDtypeStruct((t, d), F32),
            jax.ShapeDtypeStruct((SUBLANES, LANES), F32),
            jax.ShapeDtypeStruct((1, d), F32),
        ],
        compiler_params=_cp("arbitrary"),
    )(h, target, g)


def _swiglu_fwd(hid, *, name, tm=512, tc=512):
    t, f2 = hid.shape
    f = f2 // 2
    tm, tc = min(tm, t), min(tc, f)
    nf = f // tc

    def body(a_ref, b_ref, o_ref):
        a = a_ref[...]
        o_ref[...] = ((a * _sigmoid(a)) * b_ref[...]).astype(BF)

    return pl.pallas_call(
        body,
        name=name,
        grid=(t // tm, nf),
        in_specs=[pl.BlockSpec((tm, tc), lambda i, j: (i, j)), pl.BlockSpec((tm, tc), lambda i, j: (i, nf + j))],
        out_specs=pl.BlockSpec((tm, tc), lambda i, j: (i, j)),
        out_shape=jax.ShapeDtypeStruct((t, f), BF),
        compiler_params=_cp("parallel", "parallel"),
    )(hid, hid)


def _swiglu_bwd(dact, hid, *, name, tm=512, tc=512):
    t, f2 = hid.shape
    f = f2 // 2
    tm, tc = min(tm, t), min(tc, f)
    nf = f // tc

    def body(d_ref, a_ref, b_ref, o_ref):
        j = pl.program_id(1)
        a = a_ref[...]
        dv = d_ref[...]
        sg = _sigmoid(a)

        @pl.when(j < nf)
        def _():
            o_ref[...] = (dv * b_ref[...] * (sg * (1.0 + a * (1.0 - sg)))).astype(BF)

        @pl.when(j >= nf)
        def _():
            o_ref[...] = (dv * (a * sg)).astype(BF)

    return pl.pallas_call(
        body,
        name=name,
        grid=(t // tm, 2 * nf),
        in_specs=[
            pl.BlockSpec((tm, tc), lambda i, j: (i, j % nf)),
            pl.BlockSpec((tm, tc), lambda i, j: (i, j % nf)),
            pl.BlockSpec((tm, tc), lambda i, j: (i, nf + j % nf)),
        ],
        out_specs=pl.BlockSpec((tm, tc), lambda i, j: (i, j)),
        out_shape=jax.ShapeDtypeStruct((t, f2), BF),
        compiler_params=_cp("parallel", "parallel"),
    )(dact, hid, hid)


def _gate_fwd(proj, ya, yc, *, ga0, gc0, name, tm=512, tc=512):
    t, d = ya.shape
    tm, tc = min(tm, t), math.gcd(tc, d, ga0, gc0)
    a0, c0 = ga0 // tc, gc0 // tc

    def body(ga_ref, gc_ref, ya_ref, yc_ref, o_ref):
        o_ref[...] = (_sigmoid(ga_ref[...]) * ya_ref[...] + _sigmoid(gc_ref[...]) * yc_ref[...]).astype(BF)

    blk = pl.BlockSpec((tm, tc), lambda i, j: (i, j))
    return pl.pallas_call(
        body,
        name=name,
        grid=(t // tm, d // tc),
        in_specs=[
            pl.BlockSpec((tm, tc), lambda i, j: (i, a0 + j)),
            pl.BlockSpec((tm, tc), lambda i, j: (i, c0 + j)),
            blk,
            blk,
        ],
        out_specs=blk,
        out_shape=jax.ShapeDtypeStruct((t, d), BF),
        compiler_params=_cp("parallel", "parallel"),
    )(proj, proj, ya, yc)


def _gate_bwd(dm, proj, ya, yc, *, ga0, gc0, name, tm=512, tc=512):
    t, d = ya.shape
    tm, tc = min(tm, t), math.gcd(tc, d, ga0, gc0)
    a0, c0 = ga0 // tc, gc0 // tc

    def body(dm_ref, ga_ref, gc_ref, ya_ref, yc_ref, dya_ref, dyc_ref, dga_ref, dgc_ref):
        dmv = dm_ref[...]
        sa = _sigmoid(ga_ref[...])
        sc = _sigmoid(gc_ref[...])
        dya_ref[...] = (dmv * sa).astype(BF)
        dyc_ref[...] = (dmv * sc).astype(BF)
        dga_ref[...] = (dmv * ya_ref[...] * (sa * (1.0 - sa))).astype(BF)
        dgc_ref[...] = (dmv * yc_ref[...] * (sc * (1.0 - sc))).astype(BF)

    blk = pl.BlockSpec((tm, tc), lambda i, j: (i, j))
    out = jax.ShapeDtypeStruct((t, d), BF)
    return pl.pallas_call(
        body,
        name=name,
        grid=(t // tm, d // tc),
        in_specs=[
            blk,
            pl.BlockSpec((tm, tc), lambda i, j: (i, a0 + j)),
            pl.BlockSpec((tm, tc), lambda i, j: (i, c0 + j)),
            blk,
            blk,
        ],
        out_specs=[blk, blk, blk, blk],
        out_shape=[out, out, out, out],
        compiler_params=_cp("parallel", "parallel"),
    )(dm, proj, proj, ya, yc)


def _conv_taps(cz, czp, i):
    czp = czp * (i > 0).astype(F32)
    h1 = czp[SUBLANES - 1:SUBLANES, :]
    h2 = czp[SUBLANES - 2:SUBLANES - 1, :]
    row = lax.broadcasted_iota(jnp.int32, cz.shape, 0)
    s1 = jnp.where(row == 0, h1, pltpu.roll(cz, 1, 0))
    s2 = jnp.where(row == 0, h2, jnp.where(row == 1, h1, pltpu.roll(cz, 2, 0)))
    return s1, s2


def _conv_fwd(proj, w8, *, z0, gb0, gc0, cw, name, tm=512, tc=512):
    t = proj.shape[0]
    tm, tc = min(tm, t), math.gcd(tc, cw, z0, gb0, gc0)
    zb, bb, cb = z0 // tc, gb0 // tc, gc0 // tc
    rb = tm // SUBLANES

    def body(z_ref, gb_ref, gc_ref, zp_ref, gcp_ref, w_ref, o_ref):
        i = pl.program_id(0)
        cz = gc_ref[...] * z_ref[...]
        s1, s2 = _conv_taps(cz, gcp_ref[...] * zp_ref[...], i)
        w = w_ref[...]
        y = w[0:1, :] * s2 + w[1:2, :] * s1 + w[2:3, :] * cz
        o_ref[...] = (gb_ref[...] * y).astype(BF)

    def cur(b0):
        return pl.BlockSpec((tm, tc), lambda i, j: (i, b0 + j))

    def prev(b0):
        return pl.BlockSpec((SUBLANES, tc), lambda i, j: (jnp.maximum(i * rb - 1, 0), b0 + j))

    return pl.pallas_call(
        body,
        name=name,
        grid=(t // tm, cw // tc),
        in_specs=[cur(zb), cur(bb), cur(cb), prev(zb), prev(cb), pl.BlockSpec((SUBLANES, tc), lambda i, j: (0, j))],
        out_specs=pl.BlockSpec((tm, tc), lambda i, j: (i, j)),
        out_shape=jax.ShapeDtypeStruct((t, cw), BF),
        compiler_params=_cp("parallel", "parallel"),
    )(proj, proj, proj, proj, proj, w8)


def _conv_bwd(proj, w8, dcy, *, z0, gb0, gc0, cw, name, tm=512, tc=512):
    t = proj.shape[0]
    tm, tc = min(tm, t), math.gcd(tc, cw, z0, gb0, gc0)
    zb, bb, cb = z0 // tc, gb0 // tc, gc0 // tc
    rb = tm // SUBLANES
    nt = t // tm

    def body(z_ref, gb_ref, gc_ref, zp_ref, gcp_ref, d_ref, dn_ref, gbn_ref, w_ref, dz_ref, dgb_ref, dgc_ref, dw_ref):
        i = pl.program_id(1)
        z = z_ref[...]
        gc = gc_ref[...]
        gb = gb_ref[...]
        cz = gc * z
        s1, s2 = _conv_taps(cz, gcp_ref[...] * zp_ref[...], i)
        w = w_ref[...]
        w0, w1, w2 = w[0:1, :], w[1:2, :], w[2:3, :]
        yc = w0 * s2 + w1 * s1 + w2 * cz
        dcyv = d_ref[...]
        dgb_ref[...] = (dcyv * yc).astype(BF)
        dyc = dcyv * gb
        dycn = dn_ref[...] * gbn_ref[...] * (i < nt - 1).astype(F32)
        n1, n2 = dycn[0:1, :], dycn[1:2, :]
        row = lax.broadcasted_iota(jnp.int32, cz.shape, 0)
        a1 = jnp.where(row == tm - 1, n1, pltpu.roll(dyc, tm - 1, 0))
        a2 = jnp.where(row == tm - 1, n2, jnp.where(row == tm - 2, n1, pltpu.roll(dyc, tm - 2, 0)))
        dcz = w2 * dyc + w1 * a1 + w0 * a2
        dz_ref[...] = (dcz * gc).astype(BF)
        dgc_ref[...] = (dcz * z).astype(BF)
        dw0 = jnp.sum(dyc * s2, axis=0, keepdims=True)
        dw1 = jnp.sum(dyc * s1, axis=0, keepdims=True)
        dw2 = jnp.sum(dyc * cz, axis=0, keepdims=True)
        r8 = lax.broadcasted_iota(jnp.int32, (SUBLANES, tc), 0)
        upd = jnp.where(r8 == 0, dw0, jnp.where(r8 == 1, dw1, jnp.where(r8 == 2, dw2, 0.0)))

        @pl.when(i == 0)
        def _():
            dw_ref[...] = jnp.zeros_like(dw_ref)

        dw_ref[...] += upd

    def cur(b0):
        return pl.BlockSpec((tm, tc), lambda j, i: (i, b0 + j))

    def prev(b0):
        return pl.BlockSpec((SUBLANES, tc), lambda j, i: (jnp.maximum(i * rb - 1, 0), b0 + j))

    def nxt(b0):
        return pl.BlockSpec((SUBLANES, tc), lambda j, i: (jnp.minimum((i + 1) * rb, t // SUBLANES - 1), b0 + j))

    blk = pl.BlockSpec((tm, tc), lambda j, i: (i, j))
    w_spec = pl.BlockSpec((SUBLANES, tc), lambda j, i: (0, j))
    out = jax.ShapeDtypeStruct((t, cw), BF)
    return pl.pallas_call(
        body,
        name=name,
        grid=(cw // tc, nt),
        in_specs=[cur(zb), cur(bb), cur(cb), prev(zb), prev(cb), blk, nxt(0), nxt(bb), w_spec],
        out_specs=[blk, blk, blk, w_spec],
        out_shape=[out, out, out, jax.ShapeDtypeStruct((SUBLANES, cw), F32)],
        compiler_params=_cp("parallel", "arbitrary"),
    )(proj, proj, proj, proj, proj, dcy, dcy, proj, w8)


def _rot_half(x):
    lane = lax.broadcasted_iota(jnp.int32, x.shape, 1)
    first = (lane % HEAD_DIM) < (HEAD_DIM // 2)
    return jnp.where(first, pltpu.roll(x, LANES - HEAD_DIM // 2, 1), pltpu.roll(x, HEAD_DIM // 2, 1))


def _rope(x, c, s):
    parts = []
    for a in range(x.shape[1] // LANES):
        xa = x[:, a * LANES:(a + 1) * LANES]
        parts.append(xa * c + _rot_half(xa) * s)
    return parts[0] if len(parts) == 1 else jnp.concatenate(parts, axis=1)


def _rope_bwd(dy, c, s):
    parts = []
    for a in range(dy.shape[1] // LANES):
        da = dy[:, a * LANES:(a + 1) * LANES]
        parts.append(da * c + _rot_half(da * s))
    return parts[0] if len(parts) == 1 else jnp.concatenate(parts, axis=1)


def _band_mask(i):
    b = WINDOW
    r = lax.broadcasted_iota(jnp.int32, (b, 2 * b), 0)
    c = lax.broadcasted_iota(jnp.int32, (b, 2 * b), 1)
    no_prev = jnp.where(i > 0, 0, 2 * b)
    return ((c < b) & (c > r + no_prev)) | ((c >= b) & ((c - b) <= r))


def _chunk(x, a):
    return x[:, a * LANES:(a + 1) * LANES]


def _kv_aligned(kp, kc, h):
    band = jnp.concatenate([_chunk(kp, h // 2), _chunk(kc, h // 2)], axis=0).astype(F32)
    swapped = pltpu.roll(band, HEAD_DIM, 1)
    return (band, swapped) if h % 2 == 0 else (swapped, band)


def _swa_fwd(proj, cosf, sinf, sinks, *, nq, name):
    t = proj.shape[0]
    nkv = nq // Q_PER_KV
    aw, kw, b = nq * HEAD_DIM, nkv * HEAD_DIM, WINDOW
    nb = t // b
    kblk = aw // kw
    scale = HEAD_DIM ** -0.5

    def body(sink_ref, q_ref, kc_ref, kp_ref, vc_ref, vp_ref, cc_ref, cp_ref, sc_ref, sp_ref, o_ref, qr_ref, kr_ref):
        i = pl.program_id(0)
        cc, sc, cpv, spv = cc_ref[...], sc_ref[...], cp_ref[...], sp_ref[...]
        qr = _rope(q_ref[...], cc, sc)
        kc = _rope(kc_ref[...], cc, sc)
        kp = _rope(kp_ref[...], cpv, spv)
        qr_ref[...] = qr.astype(BF)
        kr_ref[...] = kc.astype(BF)
        vc, vp = vc_ref[...], vp_ref[...]
        valid = _band_mask(i)
        lo = lax.broadcasted_iota(jnp.int32, (b, LANES), 1) < HEAD_DIM
        for a in range(nq // 2):
            h = (2 * a) // Q_PER_KV
            ks = [x.astype(BF) for x in _kv_aligned(kp, kc, h)]
            vs = [x.astype(BF) for x in _kv_aligned(vp, vc, h)]
            qa = _chunk(qr, a)
            o_par = []
            for par in range(2):
                hq = 2 * a + par
                qm = jnp.where(lo if par == 0 else ~lo, qa, 0.0).astype(BF)
                s = lax.dot_general(qm, ks[par], _DIMS["nt"], preferred_element_type=F32) * scale
                s = jnp.where(valid, s, -jnp.inf)
                sink = sink_ref[hq]
                m = jnp.maximum(jnp.max(s, axis=-1, keepdims=True), sink)
                p = jnp.exp(s - m)
                p = p / (jnp.sum(p, axis=-1, keepdims=True) + jnp.exp(sink - m))
                o_par.append(jnp.dot(p.astype(BF), vs[par], preferred_element_type=F32))
            o_ref[:, a * LANES:(a + 1) * LANES] = jnp.where(lo, o_par[0], o_par[1]).astype(BF)

    def prev_i(i):
        return jnp.maximum(i - 1, 0)

    tab_c = pl.BlockSpec((b, LANES), lambda i: (i, 0))
    tab_p = pl.BlockSpec((b, LANES), lambda i: (prev_i(i), 0))
    return pl.pallas_call(
        body,
        name=name,
        grid=(nb,),
        in_specs=[
            pl.BlockSpec(memory_space=pltpu.SMEM),
            pl.BlockSpec((b, aw), lambda i: (i, 0)),
            pl.BlockSpec((b, kw), lambda i: (i, kblk)),
            pl.BlockSpec((b, kw), lambda i: (prev_i(i), kblk)),
            pl.BlockSpec((b, kw), lambda i: (i, kblk + 1)),
            pl.BlockSpec((b, kw), lambda i: (prev_i(i), kblk + 1)),
            tab_c,
            tab_p,
            tab_c,
            tab_p,
        ],
        out_specs=[
            pl.BlockSpec((b, aw), lambda i: (i, 0)),
            pl.BlockSpec((b, aw), lambda i: (i, 0)),
            pl.BlockSpec((b, kw), lambda i: (i, 0)),
        ],
        out_shape=[
            jax.ShapeDtypeStruct((t, aw), BF),
            jax.ShapeDtypeStruct((t, aw), BF),
            jax.ShapeDtypeStruct((t, kw), BF),
        ],
        compiler_params=_cp("parallel"),
    )(sinks, proj, proj, proj, proj, proj, cosf, cosf, sinf, sinf)


def _swa_bwd(qr, kr, proj, do, cosf, sinf, sinks, *, nq, name):
    t = proj.shape[0]
    nkv = nq // Q_PER_KV
    aw, kw, b = nq * HEAD_DIM, nkv * HEAD_DIM, WINDOW
    nb = t // b
    kblk = aw // kw
    scale = HEAD_DIM ** -0.5

    def body(sink_ref, q_ref, kc_ref, kp_ref, vc_ref, vp_ref, do_ref, cc_ref, cp_ref, sc_ref, sp_ref,
             dq_ref, dk_ref, dv_ref, ds_ref, ck_ref, cv_ref, sacc_ref):
        i = pl.program_id(0)

        @pl.when(i == 0)
        def _():
            ck_ref[...] = jnp.zeros_like(ck_ref)
            cv_ref[...] = jnp.zeros_like(cv_ref)
            sacc_ref[...] = jnp.zeros_like(sacc_ref)

        @pl.when(i < nb)
        def _():
            q = q_ref[...]
            kc, kp = kc_ref[...], kp_ref[...]
            vc, vp = vc_ref[...], vp_ref[...]
            dov = do_ref[...]
            valid = _band_mask(i)
            lane = lax.broadcasted_iota(jnp.int32, (b, LANES), 1)
            lo = lane < HEAD_DIM
            cc, sc = cc_ref[...], sc_ref[...]
            nch = kw // LANES
            dk_ch = [jnp.zeros((2 * b, LANES), F32) for _ in range(nch)]
            dv_ch = [jnp.zeros((2 * b, LANES), F32) for _ in range(nch)]
            sacc = jnp.zeros((b, LANES), F32)
            for a in range(nq // 2):
                h = (2 * a) // Q_PER_KV
                ks = [x.astype(BF) for x in _kv_aligned(kp, kc, h)]
                vs = [x.astype(BF) for x in _kv_aligned(vp, vc, h)]
                qa = _chunk(q, a).astype(F32)
                doa = _chunk(dov, a).astype(F32)
                dq_par = []
                for par in range(2):
                    hq = 2 * a + par
                    mine = lo if par == 0 else ~lo
                    qm = jnp.where(mine, qa, 0.0).astype(BF)
                    dom = jnp.where(mine, doa, 0.0).astype(BF)
                    s = lax.dot_general(qm, ks[par], _DIMS["nt"], preferred_element_type=F32) * scale
                    s = jnp.where(valid, s, -jnp.inf)
                    sink = sink_ref[hq]
                    m = jnp.maximum(jnp.max(s, axis=-1, keepdims=True), sink)
                    e = jnp.exp(s - m)
                    es = jnp.exp(sink - m)
                    zinv = 1.0 / (jnp.sum(e, axis=-1, keepdims=True) + es)
                    p = e * zinv
                    dp = lax.dot_general(dom, vs[par], _DIMS["nt"], preferred_element_type=F32)
                    delta = jnp.sum(p * dp, axis=-1, keepdims=True)
                    dsv = (p * (dp - delta) * scale).astype(BF)
                    sacc = sacc + jnp.where(lane == hq, -(es * zinv) * delta, 0.0)
                    dq_par.append(jnp.dot(dsv, ks[par], preferred_element_type=F32))
                    dkh = lax.dot_general(dsv, qm, _DIMS["tn"], preferred_element_type=F32)
                    dvh = lax.dot_general(p.astype(BF), dom, _DIMS["tn"], preferred_element_type=F32)
                    if par != h % 2:
                        dkh = pltpu.roll(dkh, HEAD_DIM, 1)
                        dvh = pltpu.roll(dvh, HEAD_DIM, 1)
                    dk_ch[h // 2] = dk_ch[h // 2] + dkh
                    dv_ch[h // 2] = dv_ch[h // 2] + dvh
                dqa = jnp.where(lo, dq_par[0], dq_par[1])
                dq_ref[:, a * LANES:(a + 1) * LANES] = _rope_bwd(dqa, cc, sc).astype(BF)
            dk = dk_ch[0] if nch == 1 else jnp.concatenate(dk_ch, axis=1)
            dv = dv_ch[0] if nch == 1 else jnp.concatenate(dv_ch, axis=1)
            dk_ref[...] = _rope_bwd(ck_ref[...] + dk[:b, :], cp_ref[...], sp_ref[...]).astype(BF)
            dv_ref[...] = (cv_ref[...] + dv[:b, :]).astype(BF)
            ck_ref[...] = dk[b:, :]
            cv_ref[...] = dv[b:, :]
            sacc_ref[...] += sacc

        @pl.when(i == nb)
        def _():
            dk_ref[...] = _rope_bwd(ck_ref[...], cp_ref[...], sp_ref[...]).astype(BF)
            dv_ref[...] = cv_ref[...].astype(BF)
            ds_ref[...] = jnp.broadcast_to(jnp.sum(sacc_ref[...], axis=0, keepdims=True), ds_ref.shape)

    def cur_i(i):
        return jnp.minimum(i, nb - 1)

    def prev_i(i):
        return jnp.clip(i - 1, 0, nb - 1)

    tab_c = pl.BlockSpec((b, LANES), lambda i: (cur_i(i), 0))
    tab_p = pl.BlockSpec((b, LANES), lambda i: (prev_i(i), 0))
    return pl.pallas_call(
        body,
        name=name,
        grid=(nb + 1,),
        in_specs=[
            pl.BlockSpec(memory_space=pltpu.SMEM),
            pl.BlockSpec((b, aw), lambda i: (cur_i(i), 0)),
            pl.BlockSpec((b, kw), lambda i: (cur_i(i), 0)),
            pl.BlockSpec((b, kw), lambda i: (prev_i(i), 0)),
            pl.BlockSpec((b, kw), lambda i: (cur_i(i), kblk + 1)),
            pl.BlockSpec((b, kw), lambda i: (prev_i(i), kblk + 1)),
            pl.BlockSpec((b, aw), lambda i: (cur_i(i), 0)),
            tab_c,
            tab_p,
            tab_c,
            tab_p,
        ],
        out_specs=[
            pl.BlockSpec((b, aw), lambda i: (cur_i(i), 0)),
            pl.BlockSpec((b, kw), lambda i: (prev_i(i), 0)),
            pl.BlockSpec((b, kw), lambda i: (prev_i(i), 0)),
            pl.BlockSpec((SUBLANES, LANES), lambda i: (0, 0)),
        ],
        out_shape=[
            jax.ShapeDtypeStruct((t, aw), BF),
            jax.ShapeDtypeStruct((t, kw), BF),
            jax.ShapeDtypeStruct((t, kw), BF),
            jax.ShapeDtypeStruct((SUBLANES, LANES), F32),
        ],
        scratch_shapes=[pltpu.VMEM((b, kw), F32), pltpu.VMEM((b, kw), F32), pltpu.VMEM((b, LANES), F32)],
        compiler_params=_cp("arbitrary"),
    )(sinks, qr, kr, kr, proj, proj, do, cosf, cosf, sinf, sinf)


def _xattn_fwd(xq, kv, *, name, tq=512):
    t, xw = xq.shape
    mtok = kv.shape[0]
    tq = min(tq, t)
    nh = xw // X_HEAD_DIM
    scale = X_HEAD_DIM ** -0.5

    def body(q_ref, kv_ref, o_ref):
        q = q_ref[...]
        kvv = kv_ref[...]
        outs = []
        for h in range(nh):
            sl = slice(h * X_HEAD_DIM, (h + 1) * X_HEAD_DIM)
            k = kvv[:, sl]
            v = kvv[:, xw + h * X_HEAD_DIM: xw + (h + 1) * X_HEAD_DIM]
            s = lax.dot_general(q[:, sl], k, _DIMS["nt"], preferred_element_type=F32) * scale
            e = jnp.exp(s - jnp.max(s, axis=-1, keepdims=True))
            p = e / jnp.sum(e, axis=-1, keepdims=True)
            outs.append(jnp.dot(p.astype(BF), v, preferred_element_type=F32))
        o_ref[...] = jnp.concatenate(outs, axis=1).astype(BF)

    return pl.pallas_call(
        body,
        name=name,
        grid=(t // tq,),
        in_specs=[pl.BlockSpec((tq, xw), lambda i: (i, 0)), pl.BlockSpec((mtok, 2 * xw), lambda i: (0, 0))],
        out_specs=pl.BlockSpec((tq, xw), lambda i: (i, 0)),
        out_shape=jax.ShapeDtypeStruct((t, xw), BF),
        compiler_params=_cp("parallel"),
    )(xq, kv)


def _xattn_bwd(xq, kv, do, *, name, tq=512):
    t, xw = xq.shape
    mtok = kv.shape[0]
    tq = min(tq, t)
    nh = xw // X_HEAD_DIM
    scale = X_HEAD_DIM ** -0.5

    def body(q_ref, kv_ref, do_ref, dq_ref, dkv_ref):
        i = pl.program_id(0)
        q = q_ref[...]
        kvv = kv_ref[...]
        dov = do_ref[...]
        dqs, dks, dvs = [], [], []
        for h in range(nh):
            sl = slice(h * X_HEAD_DIM, (h + 1) * X_HEAD_DIM)
            k = kvv[:, sl]
            v = kvv[:, xw + h * X_HEAD_DIM: xw + (h + 1) * X_HEAD_DIM]
            qh, doh = q[:, sl], dov[:, sl]
            s = lax.dot_general(qh, k, _DIMS["nt"], preferred_element_type=F32) * scale
            e = jnp.exp(s - jnp.max(s, axis=-1, keepdims=True))
            p = e / jnp.sum(e, axis=-1, keepdims=True)
            dp = lax.dot_general(doh, v, _DIMS["nt"], preferred_element_type=F32)
            delta = jnp.sum(p * dp, axis=-1, keepdims=True)
            dsv = (p * (dp - delta) * scale).astype(BF)
            dqs.append(jnp.dot(dsv, k, preferred_element_type=F32))
            dks.append(lax.dot_general(dsv, qh, _DIMS["tn"], preferred_element_type=F32))
            dvs.append(lax.dot_general(p.astype(BF), doh, _DIMS["tn"], preferred_element_type=F32))
        dq_ref[...] = jnp.concatenate(dqs, axis=1).astype(BF)

        @pl.when(i == 0)
        def _():
            dkv_ref[...] = jnp.zeros_like(dkv_ref)

        dkv_ref[...] += jnp.concatenate(dks + dvs, axis=1)

    row = pl.BlockSpec((tq, xw), lambda i: (i, 0))
    full = pl.BlockSpec((mtok, 2 * xw), lambda i: (0, 0))
    return pl.pallas_call(
        body,
        name=name,
        grid=(t // tq,),
        in_specs=[row, full, row],
        out_specs=[row, full],
        out_shape=[jax.ShapeDtypeStruct((t, xw), BF), jax.ShapeDtypeStruct((mtok, 2 * xw), F32)],
        compiler_params=_cp("arbitrary"),
    )(xq, kv, do)


def _adam_math(w, g, m, v):
    m = ADAM_B1 * m + (1.0 - ADAM_B1) * g
    v = ADAM_B2 * v + (1.0 - ADAM_B2) * (g * g)
    m_hat = m / (1.0 - ADAM_B1 ** ADAM_STEP)
    v_hat = v / (1.0 - ADAM_B2 ** ADAM_STEP)
    delta = -ADAM_LR * (m_hat / (jnp.sqrt(v_hat) + ADAM_EPS) + ADAM_WD * w)
    return delta, m, v


def _row_tile(r, c, n_arrays, budget=24 * 1024 * 1024):
    step = 2 * SUBLANES
    cap = max(step, budget // (2 * n_arrays * c * 4))
    if r <= cap:
        return r
    best = None
    for tr in range(step, cap + 1, step):
        if r % tr == 0:
            best = tr
    assert best is not None, (r, c)
    return best


def _adamw_sum(parts, w, m, v, *, name):
    _, r, c = parts.shape
    tr = _row_tile(r, c, 11)

    def body(p_ref, w_ref, m_ref, v_ref, g_ref, d_ref, nm_ref, nv_ref):
        g = p_ref[0].astype(F32)
        for s in range(1, N_DEV):
            g = g + p_ref[s].astype(F32)
        g_ref[...] = g
        d_ref[...], nm_ref[...], nv_ref[...] = _adam_math(w_ref[...], g, m_ref[...], v_ref[...])

    blk = pl.BlockSpec((tr, c), lambda i: (i, 0))
    out = jax.ShapeDtypeStruct((r, c), F32)
    return pl.pallas_call(
        body,
        name=name,
        grid=(r // tr,),
        in_specs=[pl.BlockSpec((N_DEV, tr, c), lambda i: (0, i, 0)), blk, blk, blk],
        out_specs=[blk, blk, blk, blk],
        out_shape=[out, out, out, out],
        compiler_params=_cp("parallel"),
    )(parts, w, m, v)


def _adamw_small(w, g, m, v, *, name):
    def body(w_ref, g_ref, m_ref, v_ref, d_ref, nm_ref, nv_ref):
        d_ref[...], nm_ref[...], nv_ref[...] = _adam_math(w_ref[...], g_ref[...], m_ref[...], v_ref[...])

    out = jax.ShapeDtypeStruct(w.shape, F32)
    return pl.pallas_call(body, name=name, out_shape=[out, out, out])(w, g, m, v)


def _mesh_pos():
    x, y, c = lax.axis_index("x"), lax.axis_index("y"), lax.axis_index("c")
    return x, y, c


def _peer(x, y, c, mask):
    px = 1 - x if mask & 4 else x
    py = 1 - y if mask & 2 else y
    pc = 1 - c if mask & 1 else c
    return (px, py, pc), 4 * px + 2 * py + pc


def _all_gather(shards, *, name):
    n = len(shards)

    def body(*refs):
        ins, outs = refs[:n], refs[n:2 * n]
        send_sems, recv_sems, loc_sems = refs[2 * n:]
        x, y, c = _mesh_pos()
        me = 4 * x + 2 * y + c
        sends, recvs, locs = [], [], []
        for w in range(n):
            lc = pltpu.make_async_copy(ins[w], outs[w].at[me], loc_sems.at[w])
            lc.start()
            locs.append(lc)
            for k in range(N_DEV - 1):
                peer, pidx = _peer(x, y, c, k + 1)
                sem = w * (N_DEV - 1) + k
                cp = pltpu.make_async_remote_copy(
                    src_ref=ins[w], dst_ref=outs[w].at[me], send_sem=send_sems.at[sem], recv_sem=recv_sems.at[sem],
                    device_id=peer, device_id_type=MESH)
                cp.start()
                sends.append(cp)
                recvs.append(pltpu.make_async_remote_copy(
                    src_ref=ins[w], dst_ref=outs[w].at[pidx], send_sem=send_sems.at[sem], recv_sem=recv_sems.at[sem],
                    device_id=peer, device_id_type=MESH))
        for rc in recvs:
            rc.wait_recv()
        for cp in sends:
            cp.wait_send()
        for lc in locs:
            lc.wait()

    any_spec = pl.BlockSpec(memory_space=pl.ANY)
    return pl.pallas_call(
        body,
        name=name,
        in_specs=[any_spec] * n,
        out_specs=[any_spec] * n,
        out_shape=[jax.ShapeDtypeStruct((N_DEV,) + s.shape, s.dtype) for s in shards],
        scratch_shapes=[
            pltpu.SemaphoreType.DMA((n * (N_DEV - 1),)),
            pltpu.SemaphoreType.DMA((n * (N_DEV - 1),)),
            pltpu.SemaphoreType.DMA((n,)),
        ],
    )(*shards)


def _scatter_partials(stacked, *, name):
    n = len(stacked)

    def body(*refs):
        ins, outs = refs[:n], refs[n:2 * n]
        send_sems, recv_sems, loc_sems = refs[2 * n:]
        x, y, c = _mesh_pos()
        me = 4 * x + 2 * y + c
        sends, recvs, locs = [], [], []
        for w in range(n):
            lc = pltpu.make_async_copy(ins[w].at[me], outs[w].at[me], loc_sems.at[w])
            lc.start()
            locs.append(lc)
            for k in range(N_DEV - 1):
                peer, pidx = _peer(x, y, c, k + 1)
                sem = w * (N_DEV - 1) + k
                cp = pltpu.make_async_remote_copy(
                    src_ref=ins[w].at[pidx], dst_ref=outs[w].at[me], send_sem=send_sems.at[sem],
                    recv_sem=recv_sems.at[sem], device_id=peer, device_id_type=MESH)
                cp.start()
                sends.append(cp)
                recvs.append(pltpu.make_async_remote_copy(
                    src_ref=ins[w].at[pidx], dst_ref=outs[w].at[pidx], send_sem=send_sems.at[sem],
                    recv_sem=recv_sems.at[sem], device_id=peer, device_id_type=MESH))
        for rc in recvs:
            rc.wait_recv()
        for cp in sends:
            cp.wait_send()
        for lc in locs:
            lc.wait()

    any_spec = pl.BlockSpec(memory_space=pl.ANY)
    return pl.pallas_call(
        body,
        name=name,
        in_specs=[any_spec] * n,
        out_specs=[any_spec] * n,
        out_shape=[jax.ShapeDtypeStruct(s.shape, s.dtype) for s in stacked],
        scratch_shapes=[
            pltpu.SemaphoreType.DMA((n * (N_DEV - 1),)),
            pltpu.SemaphoreType.DMA((n * (N_DEV - 1),)),
            pltpu.SemaphoreType.DMA((n,)),
        ],
    )(*stacked)


def _all_reduce_small(pack, *, name):
    r, c = pack.shape

    def body(p_ref, o_ref, buf_ref, send_sems, recv_sems):
        x, y, c_ = _mesh_pos()
        me = 4 * x + 2 * y + c_
        sends, recvs = [], []
        for k in range(N_DEV - 1):
            peer, pidx = _peer(x, y, c_, k + 1)
            cp = pltpu.make_async_remote_copy(
                src_ref=p_ref, dst_ref=buf_ref.at[me], send_sem=send_sems.at[k], recv_sem=recv_sems.at[k],
                device_id=peer, device_id_type=MESH)
            cp.start()
            sends.append(cp)
            recvs.append(pltpu.make_async_remote_copy(
                src_ref=p_ref, dst_ref=buf_ref.at[pidx], send_sem=send_sems.at[k], recv_sem=recv_sems.at[k],
                device_id=peer, device_id_type=MESH))
        buf_ref[me] = p_ref[...]
        for rc in recvs:
            rc.wait_recv()
        for cp in sends:
            cp.wait_send()
        acc = buf_ref[0]
        for s in range(1, N_DEV):
            acc = acc + buf_ref[s]
        o_ref[...] = acc

    vmem = pl.BlockSpec(memory_space=pltpu.VMEM)
    return pl.pallas_call(
        body,
        name=name,
        in_specs=[vmem],
        out_specs=vmem,
        out_shape=jax.ShapeDtypeStruct((r, c), F32),
        scratch_shapes=[
            pltpu.VMEM((N_DEV, r, c), F32),
            pltpu.SemaphoreType.DMA((N_DEV - 1,)),
            pltpu.SemaphoreType.DMA((N_DEV - 1,)),
        ],
    )(pack)


def _rope_tables(t):
    half = HEAD_DIM // 2
    inv_freq = ROPE_THETA ** (-jnp.arange(half, dtype=F32) / half)
    ang = jnp.arange(t, dtype=jnp.int32).astype(F32)[:, None] * inv_freq[None, :]
    cos, sin = jnp.cos(ang), jnp.sin(ang)
    cosf = jnp.concatenate([cos, cos, cos, cos], axis=1)
    sinf = jnp.concatenate([-sin, sin, -sin, sin], axis=1)
    return cosf, sinf


def _local_step(x, mem, target, gains, sinks, conv_w8, wts):
    t, d = x.shape
    aw = wts["w_attn_proj"].shape[0]
    cw = wts["w_conv_proj"].shape[0]
    nq = aw // HEAD_DIM
    kw = aw // Q_PER_KV
    z0 = aw + 2 * kw
    gb0, gc0 = z0 + cw, z0 + 2 * cw
    ga0 = z0 + 3 * cw
    gcm0 = ga0 + d
    xw = wts["w_xq"].shape[1]
    cosf, sinf = _rope_tables(t)

    u1 = _rms_fwd(x, gains["g_mix"], name="rms_mix")
    proj = _mm(u1, wts["w_in"], mode="nn", tm=1024, tn=512, tk=2048, out_dtype=F32, name="mm_in")
    o_attn, q_rot, k_rot = _swa_fwd(proj, cosf, sinf, sinks, nq=nq, name="swa_fwd")
    y_attn = _mm(o_attn, wts["w_attn_proj"], mode="nn", tm=1024, tn=1024, tk=1024, out_dtype=F32, name="mm_attn_proj")
    cy = _conv_fwd(proj, conv_w8, z0=z0, gb0=gb0, gc0=gc0, cw=cw, name="conv_fwd")
    y_conv = _mm(cy, wts["w_conv_proj"], mode="nn", tm=1024, tn=1024, tk=1024, out_dtype=F32, name="mm_conv_proj")
    merged = _gate_fwd(proj, y_attn, y_conv, ga0=ga0, gc0=gcm0, name="gate_fwd")
    h1 = _mm(merged, wts["w_mix_out"], mode="nn", tm=1024, tn=1024, tk=2048, out_dtype=F32, name="mm_mix_out",
             residual=x)
    u2 = _rms_fwd(h1, gains["g_xattn"], name="rms_xattn")
    mem_n = _rms_fwd(mem, gains["g_mem"], name="rms_mem")
    xq = _mm(u2, wts["w_xq"], mode="nn", tm=1024, tn=512, tk=2048, out_dtype=BF, name="mm_xq")
    kv = _mm(mem_n, wts["w_xkv"], mode="nn", tm=256, tn=1024, tk=2048, out_dtype=BF, name="mm_xkv")
    o_x = _xattn_fwd(xq, kv, name="xattn_fwd")
    h2 = _mm(o_x, wts["w_xo"], mode="nn", tm=1024, tn=1024, tk=512, out_dtype=F32, name="mm_xo", residual=h1)
    u3 = _rms_fwd(h2, gains["g_ffn"], name="rms_ffn")
    hid = _mm(u3, wts["w_ffn_in"], mode="nn", tm=1024, tn=512, tk=2048, out_dtype=F32, name="mm_ffn_in")
    act = _swiglu_fwd(hid, name="swiglu_fwd")
    h3 = _mm(act, wts["w_ffn_out"], mode="nn", tm=1024, tn=1024, tk=512, out_dtype=F32, name="mm_ffn_out",
             residual=h2)

    dh3, loss_tile, dg_final = _loss_head(h3, target, gains["g_final"], name="loss_head")
    gw = {}
    gw["w_ffn_out"] = _mm(act, dh3, mode="tn", tm=512, tn=1024, tk=1024, out_dtype=BF, name="mm_dw_ffn_out")
    dact = _mm(dh3, wts["w_ffn_out"], mode="nt", tm=1024, tn=512, tk=2048, out_dtype=F32, name="mm_dact")
    dhid = _swiglu_bwd(dact, hid, name="swiglu_bwd")
    gw["w_ffn_in"] = _mm(u3, dhid, mode="tn", tm=1024, tn=512, tk=1024, out_dtype=BF, name="mm_dw_ffn_in")
    du3 = _mm(dhid, wts["w_ffn_in"], mode="nt", tm=1024, tn=1024, tk=512, out_dtype=F32, name="mm_du3")
    dh2, dg_ffn = _rms_bwd(du3, h2, gains["g_ffn"], dh3, name="rms_ffn_bwd")
    gw["w_xo"] = _mm(o_x, dh2, mode="tn", tm=512, tn=1024, tk=1024, out_dtype=BF, name="mm_dw_xo")
    do_x = _mm(dh2, wts["w_xo"], mode="nt", tm=1024, tn=512, tk=2048, out_dtype=BF, name="mm_do_x")
    dxq, dkv = _xattn_bwd(xq, kv, do_x, name="xattn_bwd")
    gw["w_xq"] = _mm(u2, dxq, mode="tn", tm=1024, tn=512, tk=1024, out_dtype=BF, name="mm_dw_xq")
    du2 = _mm(dxq, wts["w_xq"], mode="nt", tm=1024, tn=1024, tk=512, out_dtype=F32, name="mm_du2")
    gw["w_xkv"] = _mm(mem_n, dkv, mode="tn", tm=1024, tn=1024, tk=256, out_dtype=BF, name="mm_dw_xkv")
    dmem_n = _mm(dkv, wts["w_xkv"], mode="nt", tm=256, tn=1024, tk=1024, out_dtype=F32, name="mm_dmem")
    _, dg_mem = _rms_bwd(dmem_n, mem, gains["g_mem"], None, name="rms_mem_bwd")
    dh1, dg_xattn = _rms_bwd(du2, h1, gains["g_xattn"], dh2, name="rms_xattn_bwd")
    gw["w_mix_out"] = _mm(merged, dh1, mode="tn", tm=1024, tn=1024, tk=1024, out_dtype=BF, name="mm_dw_mix_out")
    dmerged = _mm(dh1, wts["w_mix_out"], mode="nt", tm=1024, tn=1024, tk=2048, out_dtype=F32, name="mm_dmerged")
    dya, dyc, dga, dgc = _gate_bwd(dmerged, proj, y_attn, y_conv, ga0=ga0, gc0=gcm0, name="gate_bwd")
    gw["w_attn_proj"] = _mm(o_attn, dya, mode="tn", tm=1024, tn=1024, tk=1024, out_dtype=BF, name="mm_dw_attn_proj")
    do_attn = _mm(dya, wts["w_attn_proj"], mode="nt", tm=1024, tn=1024, tk=2048, out_dtype=BF, name="mm_do_attn")
    gw["w_conv_proj"] = _mm(cy, dyc, mode="tn", tm=1024, tn=1024, tk=1024, out_dtype=BF, name="mm_dw_conv_proj")
    dcy = _mm(dyc, wts["w_conv_proj"], mode="nt", tm=1024, tn=1024, tk=2048, out_dtype=F32, name="mm_dcy")
    dz, dgb, dgcv, dconv_w8 = _conv_bwd(proj, conv_w8, dcy, z0=z0, gb0=gb0, gc0=gc0, cw=cw, name="conv_bwd")
    dq, dk, dv, dsink_tile = _swa_bwd(q_rot, k_rot, proj, do_attn, cosf, sinf, sinks, nq=nq, name="swa_bwd")
    dproj = jnp.concatenate([dq, dk, dv, dz, dgb, dgcv, dga, dgc], axis=1)
    gw["w_in"] = _mm(u1, dproj, mode="tn", tm=1024, tn=512, tk=1024, out_dtype=BF, name="mm_dw_in")
    du1 = _mm(dproj, wts["w_in"], mode="nt", tm=1024, tn=1024, tk=512, out_dtype=F32, name="mm_du1")
    grad_x, dg_mix = _rms_bwd(du1, x, gains["g_mix"], dh1, name="rms_mix_bwd")

    small = {
        "g_mix": dg_mix, "g_xattn": dg_xattn, "g_mem": dg_mem, "g_ffn": dg_ffn, "g_final": dg_final,
        "attn_sinks": dsink_tile[0:1, :nq], "conv_w8": dconv_w8,
    }
    return loss_tile, grad_x, gw, small


_COL_SHARDED = ("w_in", "w_attn_proj", "w_conv_proj", "w_xo", "w_ffn_in")
_ROW_SHARDED = ("w_mix_out", "w_xq", "w_xkv", "w_ffn_out")
_BIG = _COL_SHARDED + _ROW_SHARDED
_GAINS = ("g_mix", "g_xattn", "g_mem", "g_ffn", "g_final")
_WEIGHTS = ("g_mix", "w_in", "conv_w", "attn_sinks", "w_attn_proj", "w_conv_proj", "w_mix_out", "g_xattn", "g_mem",
            "w_xq", "w_xkv", "w_xo", "g_ffn", "w_ffn_in", "w_ffn_out", "g_final")


def _unstack(g, col_sharded):
    n, r, c = g.shape
    if col_sharded:
        return jnp.transpose(g, (1, 0, 2)).reshape(r, n * c)
    return g.reshape(n * r, c)


def _stack(w, col_sharded):
    r, c = w.shape
    if col_sharded:
        return jnp.transpose(w.reshape(r, N_DEV, c // N_DEV), (1, 0, 2))
    return w.reshape(N_DEV, r // N_DEV, c)


def kernel(x, mem, g_mix, w_in, conv_w, attn_sinks, w_attn_proj, w_conv_proj, w_mix_out, g_xattn, g_mem, w_xq, w_xkv, w_xo, g_ffn, w_ffn_in, w_ffn_out, g_final, loss_target, m_g_mix, m_w_in, m_conv_w, m_attn_sinks, m_w_attn_proj, m_w_conv_proj, m_w_mix_out, m_g_xattn, m_g_mem, m_w_xq, m_w_xkv, m_w_xo, m_g_ffn, m_w_ffn_in, m_w_ffn_out, m_g_final, v_g_mix, v_w_in, v_conv_w, v_attn_sinks, v_w_attn_proj, v_w_conv_proj, v_w_mix_out, v_g_xattn, v_g_mem, v_w_xq, v_w_xkv, v_w_xo, v_g_ffn, v_w_ffn_in, v_w_ffn_out, v_g_final):
    w_ = dict(g_mix=g_mix, w_in=w_in, conv_w=conv_w, attn_sinks=attn_sinks, w_attn_proj=w_attn_proj,
              w_conv_proj=w_conv_proj, w_mix_out=w_mix_out, g_xattn=g_xattn, g_mem=g_mem, w_xq=w_xq, w_xkv=w_xkv,
              w_xo=w_xo, g_ffn=g_ffn, w_ffn_in=w_ffn_in, w_ffn_out=w_ffn_out, g_final=g_final)
    m_ = dict(g_mix=m_g_mix, w_in=m_w_in, conv_w=m_conv_w, attn_sinks=m_attn_sinks, w_attn_proj=m_w_attn_proj,
              w_conv_proj=m_w_conv_proj, w_mix_out=m_w_mix_out, g_xattn=m_g_xattn, g_mem=m_g_mem, w_xq=m_w_xq,
              w_xkv=m_w_xkv, w_xo=m_w_xo, g_ffn=m_g_ffn, w_ffn_in=m_w_ffn_in, w_ffn_out=m_w_ffn_out,
              g_final=m_g_final)
    v_ = dict(g_mix=v_g_mix, w_in=v_w_in, conv_w=v_conv_w, attn_sinks=v_attn_sinks, w_attn_proj=v_w_attn_proj,
              w_conv_proj=v_w_conv_proj, w_mix_out=v_w_mix_out, g_xattn=v_g_xattn, g_mem=v_g_mem, w_xq=v_w_xq,
              w_xkv=v_w_xkv, w_xo=v_w_xo, g_ffn=v_g_ffn, w_ffn_in=v_w_ffn_in, w_ffn_out=v_w_ffn_out,
              g_final=v_g_final)
    t, d = x.shape[1], x.shape[2]
    nq = attn_sinks.shape[-1]
    cw_shard = conv_w.shape[-1]
    cw = cw_shard * N_DEV

    def two_d(a):
        return a.reshape(a.shape[-2], a.shape[-1]) if a.ndim == 3 else a.reshape(1, a.shape[-1])

    shards = [two_d(w_[n]).astype(BF) for n in _BIG]
    conv_pad = jnp.zeros((SUBLANES, cw_shard), F32).at[:3].set(two_d(conv_w))
    gathered = _all_gather(shards + [conv_pad], name="gather_weights")
    wts = {n: _unstack(g, n in _COL_SHARDED) for n, g in zip(_BIG, gathered[:-1])}
    conv_w8 = _unstack(gathered[-1], True)

    gains = {n: two_d(w_[n]) for n in _GAINS}
    loss_tile, grad_x, gw, small = _local_step(
        x[0], mem[0], loss_target[0], gains, attn_sinks.reshape(nq), conv_w8, wts)

    stacked = [_stack(gw[n], n in _COL_SHARDED) for n in _BIG]
    parts = _scatter_partials(stacked, name="scatter_grads")
    grads, deltas, new_m, new_v = {}, {}, {}, {}
    for n, p in zip(_BIG, parts):
        shape = w_[n].shape
        g, dl, nm, nv = _adamw_sum(p, two_d(w_[n]), two_d(m_[n]), two_d(v_[n]), name="adamw_" + n)
        grads[n], deltas[n], new_m[n], new_v[n] = (a.reshape(shape) for a in (g, dl, nm, nv))

    width = max(d, cw)

    def padw(a):
        return jnp.pad(a, ((0, 0), (0, width - a.shape[1])))

    rows = [padw(small[n]) for n in _GAINS] + [padw(small["attn_sinks"]), padw(small["conv_w8"][:3]),
                                               padw(loss_tile[0:1, :]), jnp.zeros((6, width), F32)]
    red = _all_reduce_small(jnp.concatenate(rows, axis=0), name="reduce_small")
    loss = red[9, 0]
    small_g = {n: red[i:i + 1, :d] for i, n in enumerate(_GAINS)}
    small_g["attn_sinks"] = red[5:6, :nq]
    me = 4 * lax.axis_index("x") + 2 * lax.axis_index("y") + lax.axis_index("c")
    small_g["conv_w"] = lax.dynamic_slice(red, (6, me * cw_shard), (3, cw_shard))
    for n in _GAINS + ("attn_sinks", "conv_w"):
        shape = w_[n].shape
        g = small_g[n]
        dl, nm, nv = _adamw_small(two_d(w_[n]), g, two_d(m_[n]), two_d(v_[n]), name="adamw_" + n)
        grads[n], deltas[n], new_m[n], new_v[n] = (a.reshape(shape) for a in (g, dl, nm, nv))

    return (loss, grad_x[None], *[grads[n] for n in _WEIGHTS], *[deltas[n] for n in _WEIGHTS],
            *[new_m[n] for n in _WEIGHTS], *[new_v[n] for n in _WEIGHTS])
```

```python
import functools
import math

import jax
import jax.numpy as jnp
from jax import lax
from jax.experimental import pallas as pl
from jax.experimental.pallas import tpu as pltpu

HEAD_DIM = 64
Q_PER_KV = 4
WINDOW = 128
X_HEAD_DIM = 128
ROPE_THETA = 10000.0
EPS = 1e-6
ADAM_LR = 0.001
ADAM_B1 = 0.9
ADAM_B2 = 0.999
ADAM_EPS = 1e-08
ADAM_WD = 0.01
ADAM_STEP = 10

N_DEV = 8
LANES = 128
SUBLANES = 8
VMEM_LIMIT_BYTES = 56 * 1024 * 1024
BF = jnp.bfloat16
F32 = jnp.float32
MESH = pl.DeviceIdType.MESH


def _cp(*sem):
    return pltpu.CompilerParams(dimension_semantics=sem, vmem_limit_bytes=VMEM_LIMIT_BYTES)


def _sigmoid(x):
    return 1.0 / (1.0 + jnp.exp(-x))


_DIMS = {
    "nn": (((1,), (0,)), ((), ())),
    "nt": (((1,), (1,)), ((), ())),
    "tn": (((0,), (0,)), ((), ())),
}


def _fit(dim, tile):
    tile = min(tile, dim)
    while dim % tile and tile > LANES:
        tile //= 2
    return tile


def _mm(a, b, *, mode, tm, tn, tk, out_dtype, name, residual=None):
    if mode == "nn":
        (m, k), (k2, n) = a.shape, b.shape
    elif mode == "nt":
        (m, k), (n, k2) = a.shape, b.shape
    else:
        (k, m), (k2, n) = a.shape, b.shape
    assert k == k2, (name, a.shape, b.shape)
    tm, tn, tk = _fit(m, tm), _fit(n, tn), _fit(k, tk)
    assert m % tm == 0 and n % tn == 0 and k % tk == 0, (name, m, n, k, tm, tn, tk)
    nk = k // tk
    if mode == "tn":
        a_spec = pl.BlockSpec((tk, tm), lambda i, j, kk: (kk, i))
    else:
        a_spec = pl.BlockSpec((tm, tk), lambda i, j, kk: (i, kk))
    if mode == "nt":
        b_spec = pl.BlockSpec((tn, tk), lambda i, j, kk: (j, kk))
    else:
        b_spec = pl.BlockSpec((tk, tn), lambda i, j, kk: (kk, j))
    o_spec = pl.BlockSpec((tm, tn), lambda i, j, kk: (i, j))
    dims = _DIMS[mode]
    has_res = residual is not None

    def body(*refs):
        if has_res:
            a_ref, b_ref, r_ref, o_ref = refs[:4]
        else:
            a_ref, b_ref, o_ref = refs[:3]
        part = lax.dot_general(a_ref[...].astype(BF), b_ref[...].astype(BF), dims, preferred_element_type=F32)

        def finish(acc):
            if has_res:
                acc = r_ref[...] + acc
            o_ref[...] = acc.astype(out_dtype)

        if nk == 1:
            finish(part)
        else:
            acc_ref = refs[-1]
            kk = pl.program_id(2)

            @pl.when(kk == 0)
            def _():
                acc_ref[...] = part

            @pl.when(kk > 0)
            def _():
                acc_ref[...] += part

            @pl.when(kk == nk - 1)
            def _():
                finish(acc_ref[...])

    in_specs = [a_spec, b_spec] + ([o_spec] if has_res else [])
    args = (a, b) + ((residual,) if has_res else ())
    return pl.pallas_call(
        body,
        name=name,
        grid=(m // tm, n // tn, nk),
        in_specs=in_specs,
        out_specs=o_spec,
        out_shape=jax.ShapeDtypeStruct((m, n), out_dtype),
        scratch_shapes=[pltpu.VMEM((tm, tn), F32)] if nk > 1 else [],
        compiler_params=_cp("parallel", "parallel", "arbitrary"),
    )(*args)


def _rms_fwd(h, g, *, name, tm=512):
    t, d = h.shape
    tm = min(tm, t)

    def body(h_ref, g_ref, u_ref):
        hv = h_ref[...]
        r = lax.rsqrt(jnp.mean(hv * hv, axis=-1, keepdims=True) + EPS)
        u_ref[...] = ((hv * r) * g_ref[...]).astype(BF)

    return pl.pallas_call(
        body,
        name=name,
        grid=(t // tm,),
        in_specs=[pl.BlockSpec((tm, d), lambda i: (i, 0)), pl.BlockSpec((1, d), lambda i: (0, 0))],
        out_specs=pl.BlockSpec((tm, d), lambda i: (i, 0)),
        out_shape=jax.ShapeDtypeStruct((t, d), BF),
        compiler_params=_cp("parallel"),
    )(h, g)


def _rms_bwd(du, h, g, dres, *, name, tm=256):
    t, d = h.shape
    tm = min(tm, t)
    want_dh = dres is not None

    def body(*refs):
        if want_dh:
            du_ref, h_ref, g_ref, dres_ref, dh_ref, dg_ref = refs
        else:
            du_ref, h_ref, g_ref, dg_ref = refs
        i = pl.program_id(0)
        hv = h_ref[...]
        duv = du_ref[...]
        r = lax.rsqrt(jnp.mean(hv * hv, axis=-1, keepdims=True) + EPS)
        nv = hv * r
        if want_dh:
            gy = duv * g_ref[...]
            dh_ref[...] = dres_ref[...] + r * (gy - nv * jnp.mean(nv * gy, axis=-1, keepdims=True))

        @pl.when(i == 0)
        def _():
            dg_ref[...] = jnp.zeros_like(dg_ref)

        dg_ref[...] += jnp.sum(duv * nv, axis=0, keepdims=True)

    row = pl.BlockSpec((tm, d), lambda i: (i, 0))
    vec = pl.BlockSpec((1, d), lambda i: (0, 0))
    if want_dh:
        in_specs, args = [row, row, vec, row], (du, h, g, dres)
        out_specs = [row, vec]
        out_shape = [jax.ShapeDtypeStruct((t, d), F32), jax.ShapeDtypeStruct((1, d), F32)]
    else:
        in_specs, args = [row, row, vec], (du, h, g)
        out_specs = [vec]
        out_shape = [jax.ShapeDtypeStruct((1, d), F32)]
    outs = pl.pallas_call(
        body,
        name=name,
        grid=(t // tm,),
        in_specs=in_specs,
        out_specs=out_specs,
        out_shape=out_shape,
        compiler_params=_cp("arbitrary"),
    )(*args)
    return (outs[0], outs[1]) if want_dh else (None, outs[0])


def _loss_head(h, target, g, *, name, tm=256):
    t, d = h.shape
    tm = min(tm, t)

    def body(h_ref, t_ref, g_ref, dh_ref, loss_ref, dg_ref):
        i = pl.program_id(0)
        hv = h_ref[...]
        gv = g_ref[...]
        r = lax.rsqrt(jnp.mean(hv * hv, axis=-1, keepdims=True) + EPS)
        nv = hv * r
        e = nv * gv - t_ref[...]
        per_tok = jnp.mean(e * e, axis=-1, keepdims=True)
        lp = 0.5 * jnp.sum(per_tok, axis=0, keepdims=True)
        dy = e * (1.0 / d)
        gy = dy * gv
        dh_ref[...] = r * (gy - nv * jnp.mean(nv * gy, axis=-1, keepdims=True))

        @pl.when(i == 0)
        def _():
            loss_ref[...] = jnp.zeros_like(loss_ref)
            dg_ref[...] = jnp.zeros_like(dg_ref)

        loss_ref[...] += jnp.broadcast_to(lp, loss_ref.shape)
        dg_ref[...] += jnp.sum(dy * nv, axis=0, keepdims=True)

    row = pl.BlockSpec((tm, d), lambda i: (i, 0))
    vec = pl.BlockSpec((1, d), lambda i: (0, 0))
    return pl.pallas_call(
        body,
        name=name,
        grid=(t // tm,),
        in_specs=[row, row, vec],
        out_specs=[row, pl.BlockSpec((SUBLANES, LANES), lambda i: (0, 0)), vec],
        out_shape=[
            jax.ShapeDtypeStruct((t, d), F32),
            jax.ShapeDtypeStruct((SUBLANES, LANES), F32),
            jax.ShapeDtypeStruct((1, d), F32),
        ],
        compiler_params=_cp("arbitrary"),
    )(h, target, g)


def _swiglu_fwd(hid, *, name, tm=512, tc=512):
    t, f2 = hid.shape
    f = f2 // 2
    tm, tc = min(tm, t), min(tc, f)
    nf = f // tc

    def body(a_ref, b_ref, o_ref):
        a = a_ref[...]
        o_ref[...] = ((a * _sigmoid(a)) * b_ref[...]).astype(BF)

    return pl.pallas_call(
        body,
        name=name,
        grid=(t // tm, nf),
        in_specs=[pl.BlockSpec((tm, tc), lambda i, j: (i, j)), pl.BlockSpec((tm, tc), lambda i, j: (i, nf + j))],
        out_specs=pl.BlockSpec((tm, tc), lambda i, j: (i, j)),
        out_shape=jax.ShapeDtypeStruct((t, f), BF),
        compiler_params=_cp("parallel", "parallel"),
    )(hid, hid)


def _swiglu_bwd(dact, hid, *, name, tm=512, tc=512):
    t, f2 = hid.shape
    f = f2 // 2
    tm, tc = min(tm, t), min(tc, f)
    nf = f // tc

    def body(d_ref, a_ref, b_ref, o_ref):
        j = pl.program_id(1)
        a = a_ref[...]
        dv = d_ref[...]
        sg = _sigmoid(a)

        @pl.when(j < nf)
        def _():
            o_ref[...] = (dv * b_ref[...] * (sg * (1.0 + a * (1.0 - sg)))).astype(BF)

        @pl.when(j >= nf)
        def _():
            o_ref[...] = (dv * (a * sg)).astype(BF)

    return pl.pallas_call(
        body,
        name=name,
        grid=(t // tm, 2 * nf),
        in_specs=[
            pl.BlockSpec((tm, tc), lambda i, j: (i, j % nf)),
            pl.BlockSpec((tm, tc), lambda i, j: (i, j % nf)),
            pl.BlockSpec((tm, tc), lambda i, j: (i, nf + j % nf)),
        ],
        out_specs=pl.BlockSpec((tm, tc), lambda i, j: (i, j)),
        out_shape=jax.ShapeDtypeStruct((t, f2), BF),
        compiler_params=_cp("parallel", "parallel"),
    )(dact, hid, hid)


def _gate_fwd(proj, ya, yc, *, ga0, gc0, name, tm=512, tc=512):
    t, d = ya.shape
    tm, tc = min(tm, t), math.gcd(tc, d, ga0, gc0)
    a0, c0 = ga0 // tc, gc0 // tc

    def body(ga_ref, gc_ref, ya_ref, yc_ref, o_ref):
        o_ref[...] = (_sigmoid(ga_ref[...]) * ya_ref[...] + _sigmoid(gc_ref[...]) * yc_ref[...]).astype(BF)

    blk = pl.BlockSpec((tm, tc), lambda i, j: (i, j))
    return pl.pallas_call(
        body,
        name=name,
        grid=(t // tm, d // tc),
        in_specs=[
            pl.BlockSpec((tm, tc), lambda i, j: (i, a0 + j)),
            pl.BlockSpec((tm, tc), lambda i, j: (i, c0 + j)),
            blk,
            blk,
        ],
        out_specs=blk,
        out_shape=jax.ShapeDtypeStruct((t, d), BF),
        compiler_params=_cp("parallel", "parallel"),
    )(proj, proj, ya, yc)


def _gate_bwd(dm, proj, ya, yc, *, ga0, gc0, name, tm=512, tc=512):
    t, d = ya.shape
    tm, tc = min(tm, t), math.gcd(tc, d, ga0, gc0)
    a0, c0 = ga0 // tc, gc0 // tc

    def body(dm_ref, ga_ref, gc_ref, ya_ref, yc_ref, dya_ref, dyc_ref, dga_ref, dgc_ref):
        dmv = dm_ref[...]
        sa = _sigmoid(ga_ref[...])
        sc = _sigmoid(gc_ref[...])
        dya_ref[...] = (dmv * sa).astype(BF)
        dyc_ref[...] = (dmv * sc).astype(BF)
        dga_ref[...] = (dmv * ya_ref[...] * (sa * (1.0 - sa))).astype(BF)
        dgc_ref[...] = (dmv * yc_ref[...] * (sc * (1.0 - sc))).astype(BF)

    blk = pl.BlockSpec((tm, tc), lambda i, j: (i, j))
    out = jax.ShapeDtypeStruct((t, d), BF)
    return pl.pallas_call(
        body,
        name=name,
        grid=(t // tm, d // tc),
        in_specs=[
            blk,
            pl.BlockSpec((tm, tc), lambda i, j: (i, a0 + j)),
            pl.BlockSpec((tm, tc), lambda i, j: (i, c0 + j)),
            blk,
            blk,
        ],
        out_specs=[blk, blk, blk, blk],
        out_shape=[out, out, out, out],
        compiler_params=_cp("parallel", "parallel"),
    )(dm, proj, proj, ya, yc)


def _conv_taps(cz, czp, i):
    czp = czp * (i > 0).astype(F32)
    h1 = czp[SUBLANES - 1:SUBLANES, :]
    h2 = czp[SUBLANES - 2:SUBLANES - 1, :]
    row = lax.broadcasted_iota(jnp.int32, cz.shape, 0)
    s1 = jnp.where(row == 0, h1, pltpu.roll(cz, 1, 0))
    s2 = jnp.where(row == 0, h2, jnp.where(row == 1, h1, pltpu.roll(cz, 2, 0)))
    return s1, s2


def _conv_fwd(proj, w8, *, z0, gb0, gc0, cw, name, tm=512, tc=512):
    t = proj.shape[0]
    tm, tc = min(tm, t), math.gcd(tc, cw, z0, gb0, gc0)
    zb, bb, cb = z0 // tc, gb0 // tc, gc0 // tc
    rb = tm // SUBLANES

    def body(z_ref, gb_ref, gc_ref, zp_ref, gcp_ref, w_ref, o_ref):
        i = pl.program_id(0)
        cz = gc_ref[...] * z_ref[...]
        s1, s2 = _conv_taps(cz, gcp_ref[...] * zp_ref[...], i)
        w = w_ref[...]
        y = w[0:1, :] * s2 + w[1:2, :] * s1 + w[2:3, :] * cz
        o_ref[...] = (gb_ref[...] * y).astype(BF)

    def cur(b0):
        return pl.BlockSpec((tm, tc), lambda i, j: (i, b0 + j))

    def prev(b0):
        return pl.BlockSpec((SUBLANES, tc), lambda i, j: (jnp.maximum(i * rb - 1, 0), b0 + j))

    return pl.pallas_call(
        body,
        name=name,
        grid=(t // tm, cw // tc),
        in_specs=[cur(zb), cur(bb), cur(cb), prev(zb), prev(cb), pl.BlockSpec((SUBLANES, tc), lambda i, j: (0, j))],
        out_specs=pl.BlockSpec((tm, tc), lambda i, j: (i, j)),
        out_shape=jax.ShapeDtypeStruct((t, cw), BF),
        compiler_params=_cp("parallel", "parallel"),
    )(proj, proj, proj, proj, proj, w8)


def _conv_bwd(proj, w8, dcy, *, z0, gb0, gc0, cw, name, tm=512, tc=512):
    t = proj.shape[0]
    tm, tc = min(tm, t), math.gcd(tc, cw, z0, gb0, gc0)
    zb, bb, cb = z0 // tc, gb0 // tc, gc0 // tc
    rb = tm // SUBLANES
    nt = t // tm

    def body(z_ref, gb_ref, gc_ref, zp_ref, gcp_ref, d_ref, dn_ref, gbn_ref, w_ref, dz_ref, dgb_ref, dgc_ref, dw_ref):
        i = pl.program_id(1)
        z = z_ref[...]
        gc = gc_ref[...]
        gb = gb_ref[...]
        cz = gc * z
        s1, s2 = _conv_taps(cz, gcp_ref[...] * zp_ref[...], i)
        w = w_ref[...]
        w0, w1, w2 = w[0:1, :], w[1:2, :], w[2:3, :]
        yc = w0 * s2 + w1 * s1 + w2 * cz
        dcyv = d_ref[...]
        dgb_ref[...] = (dcyv * yc).astype(BF)
        dyc = dcyv * gb
        dycn = dn_ref[...] * gbn_ref[...] * (i < nt - 1).astype(F32)
        n1, n2 = dycn[0:1, :], dycn[1:2, :]
        row = lax.broadcasted_iota(jnp.int32, cz.shape, 0)
        a1 = jnp.where(row == tm - 1, n1, pltpu.roll(dyc, tm - 1, 0))
        a2 = jnp.where(row == tm - 1, n2, jnp.where(row == tm - 2, n1, pltpu.roll(dyc, tm - 2, 0)))
        dcz = w2 * dyc + w1 * a1 + w0 * a2
        dz_ref[...] = (dcz * gc).astype(BF)
        dgc_ref[...] = (dcz * z).astype(BF)
        dw0 = jnp.sum(dyc * s2, axis=0, keepdims=True)
        dw1 = jnp.sum(dyc * s1, axis=0, keepdims=True)
        dw2 = jnp.sum(dyc * cz, axis=0, keepdims=True)
        r8 = lax.broadcasted_iota(jnp.int32, (SUBLANES, tc), 0)
        upd = jnp.where(r8 == 0, dw0, jnp.where(r8 == 1, dw1, jnp.where(r8 == 2, dw2, 0.0)))

        @pl.when(i == 0)
        def _():
            dw_ref[...] = jnp.zeros_like(dw_ref)

        dw_ref[...] += upd

    def cur(b0):
        return pl.BlockSpec((tm, tc), lambda j, i: (i, b0 + j))

    def prev(b0):
        return pl.BlockSpec((SUBLANES, tc), lambda j, i: (jnp.maximum(i * rb - 1, 0), b0 + j))

    def nxt(b0):
        return pl.BlockSpec((SUBLANES, tc), lambda j, i: (jnp.minimum((i + 1) * rb, t // SUBLANES - 1), b0 + j))

    blk = pl.BlockSpec((tm, tc), lambda j, i: (i, j))
    w_spec = pl.BlockSpec((SUBLANES, tc), lambda j, i: (0, j))
    out = jax.ShapeDtypeStruct((t, cw), BF)
    return pl.pallas_call(
        body,
        name=name,
        grid=(cw // tc, nt),
        in_specs=[cur(zb), cur(bb), cur(cb), prev(zb), prev(cb), blk, nxt(0), nxt(bb), w_spec],
        out_specs=[blk, blk, blk, w_spec],
        out_shape=[out, out, out, jax.ShapeDtypeStruct((SUBLANES, cw), F32)],
        compiler_params=_cp("parallel", "arbitrary"),
    )(proj, proj, proj, proj, proj, dcy, dcy, proj, w8)


def _rot_half(x):
    lane = lax.broadcasted_iota(jnp.int32, x.shape, 1)
    first = (lane % HEAD_DIM) < (HEAD_DIM // 2)
    return jnp.where(first, pltpu.roll(x, LANES - HEAD_DIM // 2, 1), pltpu.roll(x, HEAD_DIM // 2, 1))


def _rope(x, c, s):
    parts = []
    for a in range(x.shape[1] // LANES):
        xa = x[:, a * LANES:(a + 1) * LANES]
        parts.append(xa * c + _rot_half(xa) * s)
    return parts[0] if len(parts) == 1 else jnp.concatenate(parts, axis=1)


def _rope_bwd(dy, c, s):
    parts = []
    for a in range(dy.shape[1] // LANES):
        da = dy[:, a * LANES:(a + 1) * LANES]
        parts.append(da * c + _rot_half(da * s))
    return parts[0] if len(parts) == 1 else jnp.concatenate(parts, axis=1)


def _band_mask(i):
    b = WINDOW
    r = lax.broadcasted_iota(jnp.int32, (b, 2 * b), 0)
    c = lax.broadcasted_iota(jnp.int32, (b, 2 * b), 1)
    no_prev = jnp.where(i > 0, 0, 2 * b)
    return ((c < b) & (c > r + no_prev)) | ((c >= b) & ((c - b) <= r))


def _chunk(x, a):
    return x[:, a * LANES:(a + 1) * LANES]


def _kv_aligned(kp, kc, h):
    band = jnp.concatenate([_chunk(kp, h // 2), _chunk(kc, h // 2)], axis=0).astype(F32)
    swapped = pltpu.roll(band, HEAD_DIM, 1)
    return (band, swapped) if h % 2 == 0 else (swapped, band)


def _swa_fwd(proj, cosf, sinf, sinks, *, nq, name):
    t = proj.shape[0]
    nkv = nq // Q_PER_KV
    aw, kw, b = nq * HEAD_DIM, nkv * HEAD_DIM, WINDOW
    nb = t // b
    kblk = aw // kw
    scale = HEAD_DIM ** -0.5

    def body(sink_ref, q_ref, kc_ref, kp_ref, vc_ref, vp_ref, cc_ref, cp_ref, sc_ref, sp_ref, o_ref, qr_ref, kr_ref):
        i = pl.program_id(0)
        cc, sc, cpv, spv = cc_ref[...], sc_ref[...], cp_ref[...], sp_ref[...]
        qr = _rope(q_ref[...], cc, sc)
        kc = _rope(kc_ref[...], cc, sc)
        kp = _rope(kp_ref[...], cpv, spv)
        qr_ref[...] = qr.astype(BF)
        kr_ref[...] = kc.astype(BF)
        vc, vp = vc_ref[...], vp_ref[...]
        valid = _band_mask(i)
        lo = lax.broadcasted_iota(jnp.int32, (b, LANES), 1) < HEAD_DIM
        for a in range(nq // 2):
            h = (2 * a) // Q_PER_KV
            ks = [x.astype(BF) for x in _kv_aligned(kp, kc, h)]
            vs = [x.astype(BF) for x in _kv_aligned(vp, vc, h)]
            qa = _chunk(qr, a)
            o_par = []
            for par in range(2):
                hq = 2 * a + par
                qm = jnp.where(lo if par == 0 else ~lo, qa, 0.0).astype(BF)
                s = lax.dot_general(qm, ks[par], _DIMS["nt"], preferred_element_type=F32) * scale
                s = jnp.where(valid, s, -jnp.inf)
                sink = sink_ref[hq]
                m = jnp.maximum(jnp.max(s, axis=-1, keepdims=True), sink)
                p = jnp.exp(s - m)
                p = p / (jnp.sum(p, axis=-1, keepdims=True) + jnp.exp(sink - m))
                o_par.append(jnp.dot(p.astype(BF), vs[par], preferred_element_type=F32))
            o_ref[:, a * LANES:(a + 1) * LANES] = jnp.where(lo, o_par[0], o_par[1]).astype(BF)

    def prev_i(i):
        return jnp.maximum(i - 1, 0)

    tab_c = pl.BlockSpec((b, LANES), lambda i: (i, 0))
    tab_p = pl.BlockSpec((b, LANES), lambda i: (prev_i(i), 0))
    return pl.pallas_call(
        body,
        name=name,
        grid=(nb,),
        in_specs=[
            pl.BlockSpec(memory_space=pltpu.SMEM),
            pl.BlockSpec((b, aw), lambda i: (i, 0)),
            pl.BlockSpec((b, kw), lambda i: (i, kblk)),
            pl.BlockSpec((b, kw), lambda i: (prev_i(i), kblk)),
            pl.BlockSpec((b, kw), lambda i: (i, kblk + 1)),
            pl.BlockSpec((b, kw), lambda i: (prev_i(i), kblk + 1)),
            tab_c,
            tab_p,
            tab_c,
            tab_p,
        ],
        out_specs=[
            pl.BlockSpec((b, aw), lambda i: (i, 0)),
            pl.BlockSpec((b, aw), lambda i: (i, 0)),
            pl.BlockSpec((b, kw), lambda i: (i, 0)),
        ],
        out_shape=[
            jax.ShapeDtypeStruct((t, aw), BF),
            jax.ShapeDtypeStruct((t, aw), BF),
            jax.ShapeDtypeStruct((t, kw), BF),
        ],
        compiler_params=_cp("parallel"),
    )(sinks, proj, proj, proj, proj, proj, cosf, cosf, sinf, sinf)


def _swa_bwd(qr, kr, proj, do, cosf, sinf, sinks, *, nq, name):
    t = proj.shape[0]
    nkv = nq // Q_PER_KV
    aw, kw, b = nq * HEAD_DIM, nkv * HEAD_DIM, WINDOW
    nb = t // b
    kblk = aw // kw
    scale = HEAD_DIM ** -0.5

    def body(sink_ref, q_ref, kc_ref, kp_ref, vc_ref, vp_ref, do_ref, cc_ref, cp_ref, sc_ref, sp_ref,
             dq_ref, dk_ref, dv_ref, ds_ref, ck_ref, cv_ref, sacc_ref):
        i = pl.program_id(0)

        @pl.when(i == 0)
        def _():
            ck_ref[...] = jnp.zeros_like(ck_ref)
            cv_ref[...] = jnp.zeros_like(cv_ref)
            sacc_ref[...] = jnp.zeros_like(sacc_ref)

        @pl.when(i < nb)
        def _():
            q = q_ref[...]
            kc, kp = kc_ref[...], kp_ref[...]
            vc, vp = vc_ref[...], vp_ref[...]
            dov = do_ref[...]
            valid = _band_mask(i)
            lane = lax.broadcasted_iota(jnp.int32, (b, LANES), 1)
            lo = lane < HEAD_DIM
            cc, sc = cc_ref[...], sc_ref[...]
            nch = kw // LANES
            dk_ch = [jnp.zeros((2 * b, LANES), F32) for _ in range(nch)]
            dv_ch = [jnp.zeros((2 * b, LANES), F32) for _ in range(nch)]
            sacc = jnp.zeros((b, LANES), F32)
            for a in range(nq // 2):
                h = (2 * a) // Q_PER_KV
                ks = [x.astype(BF) for x in _kv_aligned(kp, kc, h)]
                vs = [x.astype(BF) for x in _kv_aligned(vp, vc, h)]
                qa = _chunk(q, a).astype(F32)
                doa = _chunk(dov, a).astype(F32)
                dq_par = []
                for par in range(2):
                    hq = 2 * a + par
                    mine = lo if par == 0 else ~lo
                    qm = jnp.where(mine, qa, 0.0).astype(BF)
                    dom = jnp.where(mine, doa, 0.0).astype(BF)
                    s = lax.dot_general(qm, ks[par], _DIMS["nt"], preferred_element_type=F32) * scale
                    s = jnp.where(valid, s, -jnp.inf)
                    sink = sink_ref[hq]
                    m = jnp.maximum(jnp.max(s, axis=-1, keepdims=True), sink)
                    e = jnp.exp(s - m)
                    es = jnp.exp(sink - m)
                    zinv = 1.0 / (jnp.sum(e, axis=-1, keepdims=True) + es)
                    p = e * zinv
                    dp = lax.dot_general(dom, vs[par], _DIMS["nt"], preferred_element_type=F32)
                    delta = jnp.sum(p * dp, axis=-1, keepdims=True)
                    dsv = (p * (dp - delta) * scale).astype(BF)
                    sacc = sacc + jnp.where(lane == hq, -(es * zinv) * delta, 0.0)
                    dq_par.append(jnp.dot(dsv, ks[par], preferred_element_type=F32))
                    dkh = lax.dot_general(dsv, qm, _DIMS["tn"], preferred_element_type=F32)
                    dvh = lax.dot_general(p.astype(BF), dom, _DIMS["tn"], preferred_element_type=F32)
                    if par != h % 2:
                        dkh = pltpu.roll(dkh, HEAD_DIM, 1)
                        dvh = pltpu.roll(dvh, HEAD_DIM, 1)
                    dk_ch[h // 2] = dk_ch[h // 2] + dkh
                    dv_ch[h // 2] = dv_ch[h // 2] + dvh
                dqa = jnp.where(lo, dq_par[0], dq_par[1])
                dq_ref[:, a * LANES:(a + 1) * LANES] = _rope_bwd(dqa, cc, sc).astype(BF)
            dk = dk_ch[0] if nch == 1 else jnp.concatenate(dk_ch, axis=1)
            dv = dv_ch[0] if nch == 1 else jnp.concatenate(dv_ch, axis=1)
            dk_ref[...] = _rope_bwd(ck_ref[...] + dk[:b, :], cp_ref[...], sp_ref[...]).astype(BF)
            dv_ref[...] = (cv_ref[...] + dv[:b, :]).astype(BF)
            ck_ref[...] = dk[b:, :]
            cv_ref[...] = dv[b:, :]
            sacc_ref[...] += sacc

        @pl.when(i == nb)
        def _():
            dk_ref[...] = _rope_bwd(ck_ref[...], cp_ref[...], sp_ref[...]).astype(BF)
            dv_ref[...] = cv_ref[...].astype(BF)
            ds_ref[...] = jnp.broadcast_to(jnp.sum(sacc_ref[...], axis=0, keepdims=True), ds_ref.shape)

    def cur_i(i):
        return jnp.minimum(i, nb - 1)

    def prev_i(i):
        return jnp.clip(i - 1, 0, nb - 1)

    tab_c = pl.BlockSpec((b, LANES), lambda i: (cur_i(i), 0))
    tab_p = pl.BlockSpec((b, LANES), lambda i: (prev_i(i), 0))
    return pl.pallas_call(
        body,
        name=name,
        grid=(nb + 1,),
        in_specs=[
            pl.BlockSpec(memory_space=pltpu.SMEM),
            pl.BlockSpec((b, aw), lambda i: (cur_i(i), 0)),
            pl.BlockSpec((b, kw), lambda i: (cur_i(i), 0)),
            pl.BlockSpec((b, kw), lambda i: (prev_i(i), 0)),
            pl.BlockSpec((b, kw), lambda i: (cur_i(i), kblk + 1)),
            pl.BlockSpec((b, kw), lambda i: (prev_i(i), kblk + 1)),
            pl.BlockSpec((b, aw), lambda i: (cur_i(i), 0)),
            tab_c,
            tab_p,
            tab_c,
            tab_p,
        ],
        out_specs=[
            pl.BlockSpec((b, aw), lambda i: (cur_i(i), 0)),
            pl.BlockSpec((b, kw), lambda i: (prev_i(i), 0)),
            pl.BlockSpec((b, kw), lambda i: (prev_i(i), 0)),
            pl.BlockSpec((SUBLANES, LANES), lambda i: (0, 0)),
        ],
        out_shape=[
            jax.ShapeDtypeStruct((t, aw), BF),
            jax.ShapeDtypeStruct((t, kw), BF),
            jax.ShapeDtypeStruct((t, kw), BF),
            jax.ShapeDtypeStruct((SUBLANES, LANES), F32),
        ],
        scratch_shapes=[pltpu.VMEM((b, kw), F32), pltpu.VMEM((b, kw), F32), pltpu.VMEM((b, LANES), F32)],
        compiler_params=_cp("arbitrary"),
    )(sinks, qr, kr, kr, proj, proj, do, cosf, cosf, sinf, sinf)


def _xattn_fwd(xq, kv, *, name, tq=512):
    t, xw = xq.shape
    mtok = kv.shape[0]
    tq = min(tq, t)
    nh = xw // X_HEAD_DIM
    scale = X_HEAD_DIM ** -0.5

    def body(q_ref, kv_ref, o_ref):
        q = q_ref[...]
        kvv = kv_ref[...]
        outs = []
        for h in range(nh):
            sl = slice(h * X_HEAD_DIM, (h + 1) * X_HEAD_DIM)
            k = kvv[:, sl]
            v = kvv[:, xw + h * X_HEAD_DIM: xw + (h + 1) * X_HEAD_DIM]
            s = lax.dot_general(q[:, sl], k, _DIMS["nt"], preferred_element_type=F32) * scale
            e = jnp.exp(s - jnp.max(s, axis=-1, keepdims=True))
            p = e / jnp.sum(e, axis=-1, keepdims=True)
            outs.append(jnp.dot(p.astype(BF), v, preferred_element_type=F32))
        o_ref[...] = jnp.concatenate(outs, axis=1).astype(BF)

    return pl.pallas_call(
        body,
        name=name,
        grid=(t // tq,),
        in_specs=[pl.BlockSpec((tq, xw), lambda i: (i, 0)), pl.BlockSpec((mtok, 2 * xw), lambda i: (0, 0))],
        out_specs=pl.BlockSpec((tq, xw), lambda i: (i, 0)),
        out_shape=jax.ShapeDtypeStruct((t, xw), BF),
        compiler_params=_cp("parallel"),
    )(xq, kv)


def _xattn_bwd(xq, kv, do, *, name, tq=512):
    t, xw = xq.shape
    mtok = kv.shape[0]
    tq = min(tq, t)
    nh = xw // X_HEAD_DIM
    scale = X_HEAD_DIM ** -0.5

    def body(q_ref, kv_ref, do_ref, dq_ref, dkv_ref):
        i = pl.program_id(0)
        q = q_ref[...]
        kvv = kv_ref[...]
        dov = do_ref[...]
        dqs, dks, dvs = [], [], []
        for h in range(nh):
            sl = slice(h * X_HEAD_DIM, (h + 1) * X_HEAD_DIM)
            k = kvv[:, sl]
            v = kvv[:, xw + h * X_HEAD_DIM: xw + (h + 1) * X_HEAD_DIM]
            qh, doh = q[:, sl], dov[:, sl]
            s = lax.dot_general(qh, k, _DIMS["nt"], preferred_element_type=F32) * scale
            e = jnp.exp(s - jnp.max(s, axis=-1, keepdims=True))
            p = e / jnp.sum(e, axis=-1, keepdims=True)
            dp = lax.dot_general(doh, v, _DIMS["nt"], preferred_element_type=F32)
            delta = jnp.sum(p * dp, axis=-1, keepdims=True)
            dsv = (p * (dp - delta) * scale).astype(BF)
            dqs.append(jnp.dot(dsv, k, preferred_element_type=F32))
            dks.append(lax.dot_general(dsv, qh, _DIMS["tn"], preferred_element_type=F32))
            dvs.append(lax.dot_general(p.astype(BF), doh, _DIMS["tn"], preferred_element_type=F32))
        dq_ref[...] = jnp.concatenate(dqs, axis=1).astype(BF)

        @pl.when(i == 0)
        def _():
            dkv_ref[...] = jnp.zeros_like(dkv_ref)

        dkv_ref[...] += jnp.concatenate(dks + dvs, axis=1)

    row = pl.BlockSpec((tq, xw), lambda i: (i, 0))
    full = pl.BlockSpec((mtok, 2 * xw), lambda i: (0, 0))
    return pl.pallas_call(
        body,
        name=name,
        grid=(t // tq,),
        in_specs=[row, full, row],
        out_specs=[row, full],
        out_shape=[jax.ShapeDtypeStruct((t, xw), BF), jax.ShapeDtypeStruct((mtok, 2 * xw), F32)],
        compiler_params=_cp("arbitrary"),
    )(xq, kv, do)


def _adam_math(w, g, m, v):
    m = ADAM_B1 * m + (1.0 - ADAM_B1) * g
    v = ADAM_B2 * v + (1.0 - ADAM_B2) * (g * g)
    m_hat = m / (1.0 - ADAM_B1 ** ADAM_STEP)
    v_hat = v / (1.0 - ADAM_B2 ** ADAM_STEP)
    delta = -ADAM_LR * (m_hat / (jnp.sqrt(v_hat) + ADAM_EPS) + ADAM_WD * w)
    return delta, m, v


def _row_tile(r, c, n_arrays, budget=24 * 1024 * 1024):
    step = 2 * SUBLANES
    cap = max(step, budget // (2 * n_arrays * c * 4))
    if r <= cap:
        return r
    best = None
    for tr in range(step, cap + 1, step):
        if r % tr == 0:
            best = tr
    assert best is not None, (r, c)
    return best


def _adamw_sum(parts, w, m, v, *, name):
    _, r, c = parts.shape
    tr = _row_tile(r, c, 11)

    def body(p_ref, w_ref, m_ref, v_ref, g_ref, d_ref, nm_ref, nv_ref):
        g = p_ref[0].astype(F32)
        for s in range(1, N_DEV):
            g = g + p_ref[s].astype(F32)
        g_ref[...] = g
        d_ref[...], nm_ref[...], nv_ref[...] = _adam_math(w_ref[...], g, m_ref[...], v_ref[...])

    blk = pl.BlockSpec((tr, c), lambda i: (i, 0))
    out = jax.ShapeDtypeStruct((r, c), F32)
    return pl.pallas_call(
        body,
        name=name,
        grid=(r // tr,),
        in_specs=[pl.BlockSpec((N_DEV, tr, c), lambda i: (0, i, 0)), blk, blk, blk],
        out_specs=[blk, blk, blk, blk],
        out_shape=[out, out, out, out],
        compiler_params=_cp("parallel"),
    )(parts, w, m, v)


def _adamw_small(w, g, m, v, *, name):
    def body(w_ref, g_ref, m_ref, v_ref, d_ref, nm_ref, nv_ref):
        d_ref[...], nm_ref[...], nv_ref[...] = _adam_math(w_ref[...], g_ref[...], m_ref[...], v_ref[...])

    out = jax.ShapeDtypeStruct(w.shape, F32)
    return pl.pallas_call(body, name=name, out_shape=[out, out, out])(w, g, m, v)


def _mesh_pos():
    x, y, c = lax.axis_index("x"), lax.axis_index("y"), lax.axis_index("c")
    return x, y, c


def _peer(x, y, c, mask):
    px = 1 - x if mask & 4 else x
    py = 1 - y if mask & 2 else y
    pc = 1 - c if mask & 1 else c
    return (px, py, pc), 4 * px + 2 * py + pc


_HBM = pl.BlockSpec(memory_space=pltpu.HBM)
_SEM = pl.BlockSpec(memory_space=pltpu.SEMAPHORE)
_EFFECT = pltpu.SideEffectType.DATAFLOW_SIDE_EFFECTING


def _me():
    return 4 * lax.axis_index("x") + 2 * lax.axis_index("y") + lax.axis_index("c")


def _landing(own, me):
    land = lax.empty((N_DEV,) + own.shape, own.dtype)
    return lax.dynamic_update_slice(land, own[None], (me, 0, 0))


def _copy(src, land, send_sem, recv_sem, sem0, x, y, c, k, scatter, arriving):
    me = 4 * x + 2 * y + c
    peer, pidx = _peer(x, y, c, k + 1)
    return pltpu.make_async_remote_copy(
        src_ref=src.at[pidx] if scatter else src,
        dst_ref=land.at[pidx if arriving else me],
        send_sem=send_sem.at[sem0 + k], recv_sem=recv_sem.at[sem0 + k], device_id=peer, device_id_type=MESH)


def _exchange_start(groups, *, scatter, name):
    flat = [p for g in groups for p in g]
    n, ng = len(flat), len(groups)

    def body(*refs):
        srcs, lands = refs[:n], refs[n:2 * n]
        sems = refs[2 * n:2 * n + 2 * ng]
        token = refs[-1]
        x, y, c = _mesh_pos()
        w = 0
        for gi, g in enumerate(groups):
            for wi in range(len(g)):
                for k in range(N_DEV - 1):
                    _copy(srcs[w], lands[w], sems[2 * gi], sems[2 * gi + 1], wi * (N_DEV - 1),
                          x, y, c, k, scatter, False).start()
                w += 1
        token[...] = jnp.zeros_like(token)

    sem_shapes = []
    for g in groups:
        sem_shapes += [pltpu.SemaphoreType.DMA((len(g) * (N_DEV - 1),))] * 2
    args = [pltpu.with_memory_space_constraint(s, pltpu.HBM) for s, _ in flat]
    args += [pltpu.with_memory_space_constraint(l, pltpu.HBM) for _, l in flat]
    outs = pl.pallas_call(
        body,
        name=name,
        in_specs=[_HBM] * (2 * n),
        out_specs=[_SEM] * (2 * ng) + [_HBM] * (2 * n) + [pl.BlockSpec(memory_space=pltpu.VMEM)],
        out_shape=sem_shapes + [pltpu.HBM(a.shape, a.dtype) for a in args]
        + [jax.ShapeDtypeStruct((SUBLANES, LANES), F32)],
        input_output_aliases={i: 2 * ng + i for i in range(2 * n)},
        compiler_params=pltpu.CompilerParams(has_side_effects=_EFFECT),
    )(*args)
    sems, thru, token = outs[:2 * ng], outs[2 * ng:2 * ng + 2 * n], outs[-1]
    res, w = [], 0
    for gi, g in enumerate(groups):
        m = len(g)
        res.append((sems[2 * gi], sems[2 * gi + 1], list(thru[w:w + m]), list(thru[n + w:n + w + m])))
        w += m
    return res, token


def _exchange_wait(group, after, *, scatter, name):
    send_sems, recv_sems, srcs_in, lands_in = group
    n = len(srcs_in)

    def body(*refs):
        srcs, lands = refs[:n], refs[n:2 * n]
        send_sem, recv_sem = refs[2 * n], refs[2 * n + 1]
        x, y, c = _mesh_pos()
        for w in range(n):
            for k in range(N_DEV - 1):
                cp = _copy(srcs[w], lands[w], send_sem, recv_sem, w * (N_DEV - 1), x, y, c, k, scatter, True)
                cp.wait_send()
                cp.wait_recv()

    outs = pl.pallas_call(
        body,
        name=name,
        in_specs=[_HBM] * (2 * n) + [_SEM, _SEM, pl.BlockSpec(memory_space=pl.ANY)],
        out_specs=[_HBM] * (2 * n),
        out_shape=[pltpu.HBM(a.shape, a.dtype) for a in srcs_in + lands_in],
        input_output_aliases={i: i for i in range(2 * n)},
        compiler_params=pltpu.CompilerParams(has_side_effects=_EFFECT),
    )(*srcs_in, *lands_in, send_sems, recv_sems, after)
    return list(outs[n:])


def _tie(a, token):
    a, _ = lax.optimization_barrier((a, token))
    return a


def _all_reduce_small(parts, rows, width, *, name):
    n = len(parts)

    def body(*refs):
        ins = refs[:n]
        o_ref, pack_ref, buf_ref, send_sems, recv_sems = refs[n:]
        x, y, c_ = _mesh_pos()
        me = 4 * x + 2 * y + c_
        pack_ref[...] = jnp.zeros_like(pack_ref)
        for ref, (arr, r0, nr) in zip(ins, parts):
            pack_ref[r0:r0 + nr, 0:arr.shape[1]] = ref[0:nr, :]
        sends, recvs = [], []
        for k in range(N_DEV - 1):
            peer, pidx = _peer(x, y, c_, k + 1)
            cp = pltpu.make_async_remote_copy(
                src_ref=pack_ref, dst_ref=buf_ref.at[me], send_sem=send_sems.at[k], recv_sem=recv_sems.at[k],
                device_id=peer, device_id_type=MESH)
            cp.start()
            sends.append(cp)
            recvs.append(pltpu.make_async_remote_copy(
                src_ref=pack_ref, dst_ref=buf_ref.at[pidx], send_sem=send_sems.at[k], recv_sem=recv_sems.at[k],
                device_id=peer, device_id_type=MESH))
        buf_ref[me] = pack_ref[...]
        for rc in recvs:
            rc.wait_recv()
        for cp in sends:
            cp.wait_send()
        acc = buf_ref[0]
        for s in range(1, N_DEV):
            acc = acc + buf_ref[s]
        o_ref[...] = acc

    vmem = pl.BlockSpec(memory_space=pltpu.VMEM)
    return pl.pallas_call(
        body,
        name=name,
        in_specs=[vmem] * n,
        out_specs=vmem,
        out_shape=jax.ShapeDtypeStruct((rows, width), F32),
        scratch_shapes=[
            pltpu.VMEM((rows, width), F32),
            pltpu.VMEM((N_DEV, rows, width), F32),
            pltpu.SemaphoreType.DMA((N_DEV - 1,)),
            pltpu.SemaphoreType.DMA((N_DEV - 1,)),
        ],
    )(*[p[0] for p in parts])


def _rope_tables(t):
    half = HEAD_DIM // 2
    inv_freq = ROPE_THETA ** (-jnp.arange(half, dtype=F32) / half)
    ang = jnp.arange(t, dtype=jnp.int32).astype(F32)[:, None] * inv_freq[None, :]
    cos, sin = jnp.cos(ang), jnp.sin(ang)
    cosf = jnp.concatenate([cos, cos, cos, cos], axis=1)
    sinf = jnp.concatenate([-sin, sin, -sin, sin], axis=1)
    return cosf, sinf


def _local_step(x, mem, target, gains, sinks, aw, cw, get_w, put_g):
    t, d = x.shape
    nq = aw // HEAD_DIM
    kw = aw // Q_PER_KV
    z0 = aw + 2 * kw
    gb0, gc0 = z0 + cw, z0 + 2 * cw
    ga0 = z0 + 3 * cw
    gcm0 = ga0 + d
    cosf, sinf = _rope_tables(t)

    u1 = _rms_fwd(x, gains["g_mix"], name="rms_mix")
    mem_n = _rms_fwd(mem, gains["g_mem"], name="rms_mem")
    w_in = get_w("w_in", mem_n)
    proj = _mm(u1, w_in, mode="nn", tm=1024, tn=512, tk=2048, out_dtype=F32, name="mm_in")
    o_attn, q_rot, k_rot = _swa_fwd(proj, cosf, sinf, sinks, nq=nq, name="swa_fwd")
    conv_w8 = get_w("conv_w8", o_attn)
    w_attn_proj, w_conv_proj, w_mix_out = (get_w(n, o_attn) for n in ("w_attn_proj", "w_conv_proj", "w_mix_out"))
    w_xq, w_xkv, w_xo = (get_w(n, o_attn) for n in ("w_xq", "w_xkv", "w_xo"))
    y_attn = _mm(o_attn, w_attn_proj, mode="nn", tm=1024, tn=1024, tk=1024, out_dtype=F32, name="mm_attn_proj")
    cy = _conv_fwd(proj, conv_w8, z0=z0, gb0=gb0, gc0=gc0, cw=cw, name="conv_fwd")
    y_conv = _mm(cy, w_conv_proj, mode="nn", tm=1024, tn=1024, tk=1024, out_dtype=F32, name="mm_conv_proj")
    merged = _gate_fwd(proj, y_attn, y_conv, ga0=ga0, gc0=gcm0, name="gate_fwd")
    h1 = _mm(merged, w_mix_out, mode="nn", tm=1024, tn=1024, tk=2048, out_dtype=F32, name="mm_mix_out", residual=x)
    u2 = _rms_fwd(h1, gains["g_xattn"], name="rms_xattn")
    xq = _mm(u2, w_xq, mode="nn", tm=1024, tn=512, tk=2048, out_dtype=BF, name="mm_xq")
    kv = _mm(mem_n, w_xkv, mode="nn", tm=256, tn=1024, tk=2048, out_dtype=BF, name="mm_xkv")
    o_x = _xattn_fwd(xq, kv, name="xattn_fwd")
    h2 = _mm(o_x, w_xo, mode="nn", tm=1024, tn=1024, tk=512, out_dtype=F32, name="mm_xo", residual=h1)
    u3 = _rms_fwd(h2, gains["g_ffn"], name="rms_ffn")
    w_ffn_in = get_w("w_ffn_in", u3)
    hid = _mm(u3, w_ffn_in, mode="nn", tm=1024, tn=512, tk=2048, out_dtype=F32, name="mm_ffn_in")
    act = _swiglu_fwd(hid, name="swiglu_fwd")
    w_ffn_out = get_w("w_ffn_out", act)
    h3 = _mm(act, w_ffn_out, mode="nn", tm=1024, tn=1024, tk=512, out_dtype=F32, name="mm_ffn_out", residual=h2)

    dh3, loss_tile, dg_final = _loss_head(h3, target, gains["g_final"], name="loss_head")
    dh3 = put_g("w_ffn_out", _mm(act, dh3, mode="tn", tm=512, tn=1024, tk=1024, out_dtype=BF, name="mm_dw_ffn_out"),
                dh3)
    dact = _mm(dh3, w_ffn_out, mode="nt", tm=1024, tn=512, tk=2048, out_dtype=F32, name="mm_dact")
    dhid = _swiglu_bwd(dact, hid, name="swiglu_bwd")
    dhid = put_g("w_ffn_in", _mm(u3, dhid, mode="tn", tm=1024, tn=512, tk=1024, out_dtype=BF, name="mm_dw_ffn_in"),
                 dhid)
    du3 = _mm(dhid, w_ffn_in, mode="nt", tm=1024, tn=1024, tk=512, out_dtype=F32, name="mm_du3")
    dh2, dg_ffn = _rms_bwd(du3, h2, gains["g_ffn"], dh3, name="rms_ffn_bwd")
    dh2 = put_g("w_xo", _mm(o_x, dh2, mode="tn", tm=512, tn=1024, tk=1024, out_dtype=BF, name="mm_dw_xo"), dh2)
    do_x = _mm(dh2, w_xo, mode="nt", tm=1024, tn=512, tk=2048, out_dtype=BF, name="mm_do_x")
    dxq, dkv = _xattn_bwd(xq, kv, do_x, name="xattn_bwd")
    dxq = put_g("w_xq", _mm(u2, dxq, mode="tn", tm=1024, tn=512, tk=1024, out_dtype=BF, name="mm_dw_xq"), dxq)
    du2 = _mm(dxq, w_xq, mode="nt", tm=1024, tn=1024, tk=512, out_dtype=F32, name="mm_du2")
    dkv = put_g("w_xkv", _mm(mem_n, dkv, mode="tn", tm=1024, tn=1024, tk=256, out_dtype=BF, name="mm_dw_xkv"), dkv)
    dmem_n = _mm(dkv, w_xkv, mode="nt", tm=256, tn=1024, tk=1024, out_dtype=F32, name="mm_dmem")
    _, dg_mem = _rms_bwd(dmem_n, mem, gains["g_mem"], None, name="rms_mem_bwd")
    dh1, dg_xattn = _rms_bwd(du2, h1, gains["g_xattn"], dh2, name="rms_xattn_bwd")
    dh1 = put_g("w_mix_out",
                _mm(merged, dh1, mode="tn", tm=1024, tn=1024, tk=1024, out_dtype=BF, name="mm_dw_mix_out"), dh1)
    dmerged = _mm(dh1, w_mix_out, mode="nt", tm=1024, tn=1024, tk=2048, out_dtype=F32, name="mm_dmerged")
    dya, dyc, dga, dgc = _gate_bwd(dmerged, proj, y_attn, y_conv, ga0=ga0, gc0=gcm0, name="gate_bwd")
    dya = put_g("w_attn_proj",
                _mm(o_attn, dya, mode="tn", tm=1024, tn=1024, tk=1024, out_dtype=BF, name="mm_dw_attn_proj"), dya)
    do_attn = _mm(dya, w_attn_proj, mode="nt", tm=1024, tn=1024, tk=2048, out_dtype=BF, name="mm_do_attn")
    dyc = put_g("w_conv_proj",
                _mm(cy, dyc, mode="tn", tm=1024, tn=1024, tk=1024, out_dtype=BF, name="mm_dw_conv_proj"), dyc)
    dcy = _mm(dyc, w_conv_proj, mode="nt", tm=1024, tn=1024, tk=2048, out_dtype=F32, name="mm_dcy")
    dz, dgb, dgcv, dconv_w8 = _conv_bwd(proj, conv_w8, dcy, z0=z0, gb0=gb0, gc0=gc0, cw=cw, name="conv_bwd")
    dq, dk, dv, dsink_tile = _swa_bwd(q_rot, k_rot, proj, do_attn, cosf, sinf, sinks, nq=nq, name="swa_bwd")
    dproj = jnp.concatenate([dq, dk, dv, dz, dgb, dgcv, dga, dgc], axis=1)
    dproj = put_g("w_in", _mm(u1, dproj, mode="tn", tm=1024, tn=512, tk=1024, out_dtype=BF, name="mm_dw_in"), dproj)
    du1 = _mm(dproj, w_in, mode="nt", tm=1024, tn=1024, tk=512, out_dtype=F32, name="mm_du1")
    grad_x, dg_mix = _rms_bwd(du1, x, gains["g_mix"], dh1, name="rms_mix_bwd")

    small = {
        "g_mix": dg_mix, "g_xattn": dg_xattn, "g_mem": dg_mem, "g_ffn": dg_ffn, "g_final": dg_final,
        "attn_sinks": dsink_tile, "conv_w8": dconv_w8, "loss": loss_tile,
    }
    return grad_x, small


_COL_SHARDED = ("w_in", "w_attn_proj", "w_conv_proj", "w_xo", "w_ffn_in")
_ROW_SHARDED = ("w_mix_out", "w_xq", "w_xkv", "w_ffn_out")
_BIG = _COL_SHARDED + _ROW_SHARDED
_GAINS = ("g_mix", "g_xattn", "g_mem", "g_ffn", "g_final")
_GATHER_GROUPS = (("w_in",), ("conv_w8", "w_attn_proj", "w_conv_proj", "w_mix_out", "w_xq", "w_xkv", "w_xo"),
                  ("w_ffn_in",), ("w_ffn_out",))
_SCATTER_GROUPS = (("w_ffn_out",), ("w_ffn_in",), ("w_xo", "w_xq", "w_xkv"),
                   ("w_mix_out", "w_attn_proj", "w_conv_proj"), ("w_in",))
_WEIGHTS = ("g_mix", "w_in", "conv_w", "attn_sinks", "w_attn_proj", "w_conv_proj", "w_mix_out", "g_xattn", "g_mem",
            "w_xq", "w_xkv", "w_xo", "g_ffn", "w_ffn_in", "w_ffn_out", "g_final")


def _unstack(g, col_sharded):
    n, r, c = g.shape
    if col_sharded:
        return jnp.transpose(g, (1, 0, 2)).reshape(r, n * c)
    return g.reshape(n * r, c)


def _stack(w, col_sharded):
    r, c = w.shape
    if col_sharded:
        return jnp.transpose(w.reshape(r, N_DEV, c // N_DEV), (1, 0, 2))
    return w.reshape(N_DEV, r // N_DEV, c)


def kernel(x, mem, g_mix, w_in, conv_w, attn_sinks, w_attn_proj, w_conv_proj, w_mix_out, g_xattn, g_mem, w_xq, w_xkv, w_xo, g_ffn, w_ffn_in, w_ffn_out, g_final, loss_target, m_g_mix, m_w_in, m_conv_w, m_attn_sinks, m_w_attn_proj, m_w_conv_proj, m_w_mix_out, m_g_xattn, m_g_mem, m_w_xq, m_w_xkv, m_w_xo, m_g_ffn, m_w_ffn_in, m_w_ffn_out, m_g_final, v_g_mix, v_w_in, v_conv_w, v_attn_sinks, v_w_attn_proj, v_w_conv_proj, v_w_mix_out, v_g_xattn, v_g_mem, v_w_xq, v_w_xkv, v_w_xo, v_g_ffn, v_w_ffn_in, v_w_ffn_out, v_g_final):
    w_ = dict(g_mix=g_mix, w_in=w_in, conv_w=conv_w, attn_sinks=attn_sinks, w_attn_proj=w_attn_proj,
              w_conv_proj=w_conv_proj, w_mix_out=w_mix_out, g_xattn=g_xattn, g_mem=g_mem, w_xq=w_xq, w_xkv=w_xkv,
              w_xo=w_xo, g_ffn=g_ffn, w_ffn_in=w_ffn_in, w_ffn_out=w_ffn_out, g_final=g_final)
    m_ = dict(g_mix=m_g_mix, w_in=m_w_in, conv_w=m_conv_w, attn_sinks=m_attn_sinks, w_attn_proj=m_w_attn_proj,
              w_conv_proj=m_w_conv_proj, w_mix_out=m_w_mix_out, g_xattn=m_g_xattn, g_mem=m_g_mem, w_xq=m_w_xq,
              w_xkv=m_w_xkv, w_xo=m_w_xo, g_ffn=m_g_ffn, w_ffn_in=m_w_ffn_in, w_ffn_out=m_w_ffn_out,
              g_final=m_g_final)
    v_ = dict(g_mix=v_g_mix, w_in=v_w_in, conv_w=v_conv_w, attn_sinks=v_attn_sinks, w_attn_proj=v_w_attn_proj,
              w_conv_proj=v_w_conv_proj, w_mix_out=v_w_mix_out, g_xattn=v_g_xattn, g_mem=v_g_mem, w_xq=v_w_xq,
              w_xkv=v_w_xkv, w_xo=v_w_xo, g_ffn=v_g_ffn, w_ffn_in=v_w_ffn_in, w_ffn_out=v_w_ffn_out,
              g_final=v_g_final)
    t, d = x.shape[1], x.shape[2]
    nq = attn_sinks.shape[-1]
    cw_shard = conv_w.shape[-1]
    cw = cw_shard * N_DEV

    def two_d(a):
        return a.reshape(a.shape[-2], a.shape[-1]) if a.ndim == 3 else a.reshape(1, a.shape[-1])

    me = _me()
    col = set(_COL_SHARDED) | {"conv_w8"}

    shards = {n: two_d(w_[n]).astype(BF) for n in _BIG}
    shards["conv_w8"] = jnp.zeros((SUBLANES, cw_shard), F32).at[:3].set(two_d(conv_w))
    gathers, token = _exchange_start(
        [[(shards[n], _landing(shards[n], me)) for n in g] for g in _GATHER_GROUPS], scatter=False,
        name="gather_start")
    full = {}

    def get_w(name, after):
        if name not in full:
            gi = [name in g for g in _GATHER_GROUPS].index(True)
            lands = _exchange_wait(gathers[gi], after, scatter=False, name="gather_wait_%d" % gi)
            for n, land in zip(_GATHER_GROUPS[gi], lands):
                full[n] = _unstack(land, n in col)
        return full[name]

    pending, scatters = {}, []

    def put_g(name, dw, nxt):
        pending[name] = _stack(dw, name in col)
        gi = [name in g for g in _SCATTER_GROUPS].index(True)
        group = _SCATTER_GROUPS[gi]
        if not all(n in pending for n in group):
            return nxt
        pairs = [(pending[n], _landing(lax.dynamic_index_in_dim(pending[n], me, 0, keepdims=False), me))
                 for n in group]
        started, tok = _exchange_start([pairs], scatter=True, name="scatter_start_%d" % gi)
        scatters.append((gi, started[0]))
        return _tie(nxt, tok)

    gains = {n: two_d(w_[n]) for n in _GAINS}
    grad_x, small = _local_step(
        _tie(x[0], token), mem[0], loss_target[0], gains, attn_sinks.reshape(nq), w_attn_proj.shape[-2], cw,
        get_w, put_g)

    grads, deltas, new_m, new_v = {}, {}, {}, {}
    after = grad_x
    for gi, started in scatters:
        parts = _exchange_wait(started, after, scatter=True, name="scatter_wait_%d" % gi)
        for n, p in zip(_SCATTER_GROUPS[gi], parts):
            shape = w_[n].shape
            g, dl, nm, nv = _adamw_sum(p, two_d(w_[n]), two_d(m_[n]), two_d(v_[n]), name="adamw_" + n)
            grads[n], deltas[n], new_m[n], new_v[n] = (a.reshape(shape) for a in (g, dl, nm, nv))
            after = g

    parts = [(small[n], i, 1) for i, n in enumerate(_GAINS)]
    parts += [(small["attn_sinks"], 5, 1), (small["loss"], 6, 1), (small["conv_w8"], 8, 3)]
    red = _all_reduce_small(parts, 2 * SUBLANES, max(d, cw), name="reduce_small")
    loss = red[6, 0]
    small_g = {n: red[i:i + 1, :d] for i, n in enumerate(_GAINS)}
    small_g["attn_sinks"] = red[5:6, :nq]
    small_g["conv_w"] = lax.dynamic_slice(red, (8, me * cw_shard), (3, cw_shard))
    for n in _GAINS + ("attn_sinks", "conv_w"):
        shape = w_[n].shape
        g = small_g[n]
        dl, nm, nv = _adamw_small(two_d(w_[n]), g, two_d(m_[n]), two_d(v_[n]), name="adamw_" + n)
        grads[n], deltas[n], new_m[n], new_v[n] = (a.reshape(shape) for a in (g, dl, nm, nv))

    return (loss, grad_x[None], *[grads[n] for n in _WEIGHTS], *[deltas[n] for n in _WEIGHTS],
            *[new_m[n] for n in _WEIGHTS], *[new_v[n] for n in _WEIGHTS])
```

```python
import functools
import math

import jax
import jax.numpy as jnp
from jax import lax
from jax.experimental import pallas as pl
from jax.experimental.pallas import tpu as pltpu

HEAD_DIM = 64
Q_PER_KV = 4
WINDOW = 128
X_HEAD_DIM = 128
ROPE_THETA = 10000.0
EPS = 1e-6
ADAM_LR = 0.001
ADAM_B1 = 0.9
ADAM_B2 = 0.999
ADAM_EPS = 1e-08
ADAM_WD = 0.01
ADAM_STEP = 10

N_DEV = 8
LANES = 128
SUBLANES = 8
VMEM_LIMIT_BYTES = 56 * 1024 * 1024
BF = jnp.bfloat16
F32 = jnp.float32
MESH = pl.DeviceIdType.MESH


def _cp(*sem):
    return pltpu.CompilerParams(dimension_semantics=sem, vmem_limit_bytes=VMEM_LIMIT_BYTES)


def _sigmoid(x):
    return 1.0 / (1.0 + jnp.exp(-x))


_DIMS = {
    "nn": (((1,), (0,)), ((), ())),
    "nt": (((1,), (1,)), ((), ())),
    "tn": (((0,), (0,)), ((), ())),
}


def _fit(dim, tile):
    tile = min(tile, dim)
    while dim % tile and tile > LANES:
        tile //= 2
    return tile


def _mm(a, b, *, mode, tm, tn, tk, out_dtype, name, residual=None, dep=None):
    if mode == "nn":
        (m, k), (k2, n) = a.shape, b.shape
    elif mode == "nt":
        (m, k), (n, k2) = a.shape, b.shape
    else:
        (k, m), (k2, n) = a.shape, b.shape
    assert k == k2, (name, a.shape, b.shape)
    tm, tn, tk = _fit(m, tm), _fit(n, tn), _fit(k, tk)
    assert m % tm == 0 and n % tn == 0 and k % tk == 0, (name, m, n, k, tm, tn, tk)
    nk = k // tk
    if mode == "tn":
        a_spec = pl.BlockSpec((tk, tm), lambda i, j, kk: (kk, i))
    else:
        a_spec = pl.BlockSpec((tm, tk), lambda i, j, kk: (i, kk))
    if mode == "nt":
        b_spec = pl.BlockSpec((tn, tk), lambda i, j, kk: (j, kk))
    else:
        b_spec = pl.BlockSpec((tk, tn), lambda i, j, kk: (kk, j))
    o_spec = pl.BlockSpec((tm, tn), lambda i, j, kk: (i, j))
    dims = _DIMS[mode]
    has_res = residual is not None
    n_in = 2 + has_res + (dep is not None)

    def body(*refs):
        a_ref, b_ref, r_ref, o_ref = refs[0], refs[1], refs[2], refs[n_in]
        part = lax.dot_general(a_ref[...].astype(BF), b_ref[...].astype(BF), dims, preferred_element_type=F32)

        def finish(acc):
            if has_res:
                acc = r_ref[...] + acc
            o_ref[...] = acc.astype(out_dtype)

        if nk == 1:
            finish(part)
        else:
            acc_ref = refs[-1]
            kk = pl.program_id(2)

            @pl.when(kk == 0)
            def _():
                acc_ref[...] = part

            @pl.when(kk > 0)
            def _():
                acc_ref[...] += part

            @pl.when(kk == nk - 1)
            def _():
                finish(acc_ref[...])

    in_specs = [a_spec, b_spec] + ([o_spec] if has_res else [])
    args = (a, b) + ((residual,) if has_res else ())
    if dep is not None:
        in_specs.append(pl.BlockSpec(memory_space=pl.ANY))
        args += (dep,)
    return pl.pallas_call(
        body,
        name=name,
        grid=(m // tm, n // tn, nk),
        in_specs=in_specs,
        out_specs=o_spec,
        out_shape=jax.ShapeDtypeStruct((m, n), out_dtype),
        scratch_shapes=[pltpu.VMEM((tm, tn), F32)] if nk > 1 else [],
        compiler_params=_cp("parallel", "parallel", "arbitrary"),
    )(*args)


def _rms_fwd(h, g, *, name, tm=512, dep=None):
    t, d = h.shape
    tm = min(tm, t)

    def body(*refs):
        h_ref, g_ref, u_ref = refs[0], refs[1], refs[-1]
        hv = h_ref[...]
        r = lax.rsqrt(jnp.mean(hv * hv, axis=-1, keepdims=True) + EPS)
        u_ref[...] = ((hv * r) * g_ref[...]).astype(BF)

    in_specs = [pl.BlockSpec((tm, d), lambda i: (i, 0)), pl.BlockSpec((1, d), lambda i: (0, 0))]
    args = (h, g)
    if dep is not None:
        in_specs.append(pl.BlockSpec(memory_space=pl.ANY))
        args += (dep,)
    return pl.pallas_call(
        body,
        name=name,
        grid=(t // tm,),
        in_specs=in_specs,
        out_specs=pl.BlockSpec((tm, d), lambda i: (i, 0)),
        out_shape=jax.ShapeDtypeStruct((t, d), BF),
        compiler_params=_cp("parallel"),
    )(*args)


def _rms_bwd(du, h, g, dres, *, name, tm=256):
    t, d = h.shape
    tm = min(tm, t)
    want_dh = dres is not None

    def body(*refs):
        if want_dh:
            du_ref, h_ref, g_ref, dres_ref, dh_ref, dg_ref = refs
        else:
            du_ref, h_ref, g_ref, dg_ref = refs
        i = pl.program_id(0)
        hv = h_ref[...]
        duv = du_ref[...]
        r = lax.rsqrt(jnp.mean(hv * hv, axis=-1, keepdims=True) + EPS)
        nv = hv * r
        if want_dh:
            gy = duv * g_ref[...]
            dh_ref[...] = dres_ref[...] + r * (gy - nv * jnp.mean(nv * gy, axis=-1, keepdims=True))

        @pl.when(i == 0)
        def _():
            dg_ref[...] = jnp.zeros_like(dg_ref)

        dg_ref[...] += jnp.sum(duv * nv, axis=0, keepdims=True)

    row = pl.BlockSpec((tm, d), lambda i: (i, 0))
    vec = pl.BlockSpec((1, d), lambda i: (0, 0))
    if want_dh:
        in_specs, args = [row, row, vec, row], (du, h, g, dres)
        out_specs = [row, vec]
        out_shape = [jax.ShapeDtypeStruct((t, d), F32), jax.ShapeDtypeStruct((1, d), F32)]
    else:
        in_specs, args = [row, row, vec], (du, h, g)
        out_specs = [vec]
        out_shape = [jax.ShapeDtypeStruct((1, d), F32)]
    outs = pl.pallas_call(
        body,
        name=name,
        grid=(t // tm,),
        in_specs=in_specs,
        out_specs=out_specs,
        out_shape=out_shape,
        compiler_params=_cp("arbitrary"),
    )(*args)
    return (outs[0], outs[1]) if want_dh else (None, outs[0])


def _loss_head(h, target, g, *, name, tm=256):
    t, d = h.shape
    tm = min(tm, t)

    def body(h_ref, t_ref, g_ref, dh_ref, loss_ref, dg_ref):
        i = pl.program_id(0)
        hv = h_ref[...]
        gv = g_ref[...]
        r = lax.rsqrt(jnp.mean(hv * hv, axis=-1, keepdims=True) + EPS)
        nv = hv * r
        e = nv * gv - t_ref[...]
        per_tok = jnp.mean(e * e, axis=-1, keepdims=True)
        lp = 0.5 * jnp.sum(per_tok, axis=0, keepdims=True)
        dy = e * (1.0 / d)
        gy = dy * gv
        dh_ref[...] = r * (gy - nv * jnp.mean(nv * gy, axis=-1, keepdims=True))

        @pl.when(i == 0)
        def _():
            loss_ref[...] = jnp.zeros_like(loss_ref)
            dg_ref[...] = jnp.zeros_like(dg_ref)

        loss_ref[...] += jnp.broadcast_to(lp, loss_ref.shape)
        dg_ref[...] += jnp.sum(dy * nv, axis=0, keepdims=True)

    row = pl.BlockSpec((tm, d), lambda i: (i, 0))
    vec = pl.BlockSpec((1, d), lambda i: (0, 0))
    return pl.pallas_call(
        body,
        name=name,
        grid=(t // tm,),
        in_specs=[row, row, vec],
        out_specs=[row, pl.BlockSpec((SUBLANES, LANES), lambda i: (0, 0)), vec],
        out_shape=[
            jax.ShapeDtypeStruct((t, d), F32),
            jax.ShapeDtypeStruct((SUBLANES, LANES), F32),
            jax.ShapeDtypeStruct((1, d), F32),
        ],
        compiler_params=_cp("arbitrary"),
    )(h, target, g)


def _swiglu_fwd(hid, *, name, tm=512, tc=512):
    t, f2 = hid.shape
    f = f2 // 2
    tm, tc = min(tm, t), min(tc, f)
    nf = f // tc

    def body(a_ref, b_ref, o_ref):
        a = a_ref[...]
        o_ref[...] = ((a * _sigmoid(a)) * b_ref[...]).astype(BF)

    return pl.pallas_call(
        body,
        name=name,
        grid=(t // tm, nf),
        in_specs=[pl.BlockSpec((tm, tc), lambda i, j: (i, j)), pl.BlockSpec((tm, tc), lambda i, j: (i, nf + j))],
        out_specs=pl.BlockSpec((tm, tc), lambda i, j: (i, j)),
        out_shape=jax.ShapeDtypeStruct((t, f), BF),
        compiler_params=_cp("parallel", "parallel"),
    )(hid, hid)


def _swiglu_bwd(dact, hid, *, name, tm=512, tc=512):
    t, f2 = hid.shape
    f = f2 // 2
    tm, tc = min(tm, t), min(tc, f)
    nf = f // tc

    def body(d_ref, a_ref, b_ref, o_ref):
        j = pl.program_id(1)
        a = a_ref[...]
        dv = d_ref[...]
        sg = _sigmoid(a)

        @pl.when(j < nf)
        def _():
            o_ref[...] = (dv * b_ref[...] * (sg * (1.0 + a * (1.0 - sg)))).astype(BF)

        @pl.when(j >= nf)
        def _():
            o_ref[...] = (dv * (a * sg)).astype(BF)

    return pl.pallas_call(
        body,
        name=name,
        grid=(t // tm, 2 * nf),
        in_specs=[
            pl.BlockSpec((tm, tc), lambda i, j: (i, j % nf)),
            pl.BlockSpec((tm, tc), lambda i, j: (i, j % nf)),
            pl.BlockSpec((tm, tc), lambda i, j: (i, nf + j % nf)),
        ],
        out_specs=pl.BlockSpec((tm, tc), lambda i, j: (i, j)),
        out_shape=jax.ShapeDtypeStruct((t, f2), BF),
        compiler_params=_cp("parallel", "parallel"),
    )(dact, hid, hid)


def _gate_fwd(proj, ya, yc, *, ga0, gc0, name, tm=512, tc=512):
    t, d = ya.shape
    tm, tc = min(tm, t), math.gcd(tc, d, ga0, gc0)
    a0, c0 = ga0 // tc, gc0 // tc

    def body(ga_ref, gc_ref, ya_ref, yc_ref, o_ref):
        o_ref[...] = (_sigmoid(ga_ref[...]) * ya_ref[...] + _sigmoid(gc_ref[...]) * yc_ref[...]).astype(BF)

    blk = pl.BlockSpec((tm, tc), lambda i, j: (i, j))
    return pl.pallas_call(
        body,
        name=name,
        grid=(t // tm, d // tc),
        in_specs=[
            pl.BlockSpec((tm, tc), lambda i, j: (i, a0 + j)),
            pl.BlockSpec((tm, tc), lambda i, j: (i, c0 + j)),
            blk,
            blk,
        ],
        out_specs=blk,
        out_shape=jax.ShapeDtypeStruct((t, d), BF),
        compiler_params=_cp("parallel", "parallel"),
    )(proj, proj, ya, yc)


def _gate_bwd(dm, proj, ya, yc, *, ga0, gc0, name, tm=512, tc=512):
    t, d = ya.shape
    tm, tc = min(tm, t), math.gcd(tc, d, ga0, gc0)
    a0, c0 = ga0 // tc, gc0 // tc

    def body(dm_ref, ga_ref, gc_ref, ya_ref, yc_ref, dya_ref, dyc_ref, dga_ref, dgc_ref):
        dmv = dm_ref[...]
        sa = _sigmoid(ga_ref[...])
        sc = _sigmoid(gc_ref[...])
        dya_ref[...] = (dmv * sa).astype(BF)
        dyc_ref[...] = (dmv * sc).astype(BF)
        dga_ref[...] = (dmv * ya_ref[...] * (sa * (1.0 - sa))).astype(BF)
        dgc_ref[...] = (dmv * yc_ref[...] * (sc * (1.0 - sc))).astype(BF)

    blk = pl.BlockSpec((tm, tc), lambda i, j: (i, j))
    out = jax.ShapeDtypeStruct((t, d), BF)
    return pl.pallas_call(
        body,
        name=name,
        grid=(t // tm, d // tc),
        in_specs=[
            blk,
            pl.BlockSpec((tm, tc), lambda i, j: (i, a0 + j)),
            pl.BlockSpec((tm, tc), lambda i, j: (i, c0 + j)),
            blk,
            blk,
        ],
        out_specs=[blk, blk, blk, blk],
        out_shape=[out, out, out, out],
        compiler_params=_cp("parallel", "parallel"),
    )(dm, proj, proj, ya, yc)


def _conv_taps(cz, czp, i):
    czp = czp * (i > 0).astype(F32)
    h1 = czp[SUBLANES - 1:SUBLANES, :]
    h2 = czp[SUBLANES - 2:SUBLANES - 1, :]
    row = lax.broadcasted_iota(jnp.int32, cz.shape, 0)
    s1 = jnp.where(row == 0, h1, pltpu.roll(cz, 1, 0))
    s2 = jnp.where(row == 0, h2, jnp.where(row == 1, h1, pltpu.roll(cz, 2, 0)))
    return s1, s2


def _conv_fwd(proj, w8, *, z0, gb0, gc0, cw, name, tm=512, tc=512):
    t = proj.shape[0]
    tm, tc = min(tm, t), math.gcd(tc, cw, z0, gb0, gc0)
    zb, bb, cb = z0 // tc, gb0 // tc, gc0 // tc
    rb = tm // SUBLANES

    def body(z_ref, gb_ref, gc_ref, zp_ref, gcp_ref, w_ref, o_ref):
        i = pl.program_id(0)
        cz = gc_ref[...] * z_ref[...]
        s1, s2 = _conv_taps(cz, gcp_ref[...] * zp_ref[...], i)
        w = w_ref[...]
        y = w[0:1, :] * s2 + w[1:2, :] * s1 + w[2:3, :] * cz
        o_ref[...] = (gb_ref[...] * y).astype(BF)

    def cur(b0):
        return pl.BlockSpec((tm, tc), lambda i, j: (i, b0 + j))

    def prev(b0):
        return pl.BlockSpec((SUBLANES, tc), lambda i, j: (jnp.maximum(i * rb - 1, 0), b0 + j))

    return pl.pallas_call(
        body,
        name=name,
        grid=(t // tm, cw // tc),
        in_specs=[cur(zb), cur(bb), cur(cb), prev(zb), prev(cb), pl.BlockSpec((SUBLANES, tc), lambda i, j: (0, j))],
        out_specs=pl.BlockSpec((tm, tc), lambda i, j: (i, j)),
        out_shape=jax.ShapeDtypeStruct((t, cw), BF),
        compiler_params=_cp("parallel", "parallel"),
    )(proj, proj, proj, proj, proj, w8)


def _conv_bwd(proj, w8, dcy, *, z0, gb0, gc0, cw, name, tm=512, tc=512):
    t = proj.shape[0]
    tm, tc = min(tm, t), math.gcd(tc, cw, z0, gb0, gc0)
    zb, bb, cb = z0 // tc, gb0 // tc, gc0 // tc
    rb = tm // SUBLANES
    nt = t // tm

    def body(z_ref, gb_ref, gc_ref, zp_ref, gcp_ref, d_ref, dn_ref, gbn_ref, w_ref, dz_ref, dgb_ref, dgc_ref, dw_ref):
        i = pl.program_id(1)
        z = z_ref[...]
        gc = gc_ref[...]
        gb = gb_ref[...]
        cz = gc * z
        s1, s2 = _conv_taps(cz, gcp_ref[...] * zp_ref[...], i)
        w = w_ref[...]
        w0, w1, w2 = w[0:1, :], w[1:2, :], w[2:3, :]
        yc = w0 * s2 + w1 * s1 + w2 * cz
        dcyv = d_ref[...]
        dgb_ref[...] = (dcyv * yc).astype(BF)
        dyc = dcyv * gb
        dycn = dn_ref[...] * gbn_ref[...] * (i < nt - 1).astype(F32)
        n1, n2 = dycn[0:1, :], dycn[1:2, :]
        row = lax.broadcasted_iota(jnp.int32, cz.shape, 0)
        a1 = jnp.where(row == tm - 1, n1, pltpu.roll(dyc, tm - 1, 0))
        a2 = jnp.where(row == tm - 1, n2, jnp.where(row == tm - 2, n1, pltpu.roll(dyc, tm - 2, 0)))
        dcz = w2 * dyc + w1 * a1 + w0 * a2
        dz_ref[...] = (dcz * gc).astype(BF)
        dgc_ref[...] = (dcz * z).astype(BF)
        dw0 = jnp.sum(dyc * s2, axis=0, keepdims=True)
        dw1 = jnp.sum(dyc * s1, axis=0, keepdims=True)
        dw2 = jnp.sum(dyc * cz, axis=0, keepdims=True)
        r8 = lax.broadcasted_iota(jnp.int32, (SUBLANES, tc), 0)
        upd = jnp.where(r8 == 0, dw0, jnp.where(r8 == 1, dw1, jnp.where(r8 == 2, dw2, 0.0)))

        @pl.when(i == 0)
        def _():
            dw_ref[...] = jnp.zeros_like(dw_ref)

        dw_ref[...] += upd

    def cur(b0):
        return pl.BlockSpec((tm, tc), lambda j, i: (i, b0 + j))

    def prev(b0):
        return pl.BlockSpec((SUBLANES, tc), lambda j, i: (jnp.maximum(i * rb - 1, 0), b0 + j))

    def nxt(b0):
        return pl.BlockSpec((SUBLANES, tc), lambda j, i: (jnp.minimum((i + 1) * rb, t // SUBLANES - 1), b0 + j))

    blk = pl.BlockSpec((tm, tc), lambda j, i: (i, j))
    w_spec = pl.BlockSpec((SUBLANES, tc), lambda j, i: (0, j))
    out = jax.ShapeDtypeStruct((t, cw), BF)
    return pl.pallas_call(
        body,
        name=name,
        grid=(cw // tc, nt),
        in_specs=[cur(zb), cur(bb), cur(cb), prev(zb), prev(cb), blk, nxt(0), nxt(bb), w_spec],
        out_specs=[blk, blk, blk, w_spec],
        out_shape=[out, out, out, jax.ShapeDtypeStruct((SUBLANES, cw), F32)],
        compiler_params=_cp("parallel", "arbitrary"),
    )(proj, proj, proj, proj, proj, dcy, dcy, proj, w8)


def _rot_half(x):
    lane = lax.broadcasted_iota(jnp.int32, x.shape, 1)
    first = (lane % HEAD_DIM) < (HEAD_DIM // 2)
    return jnp.where(first, pltpu.roll(x, LANES - HEAD_DIM // 2, 1), pltpu.roll(x, HEAD_DIM // 2, 1))


def _rope(x, c, s):
    parts = []
    for a in range(x.shape[1] // LANES):
        xa = x[:, a * LANES:(a + 1) * LANES]
        parts.append(xa * c + _rot_half(xa) * s)
    return parts[0] if len(parts) == 1 else jnp.concatenate(parts, axis=1)


def _rope_bwd(dy, c, s):
    parts = []
    for a in range(dy.shape[1] // LANES):
        da = dy[:, a * LANES:(a + 1) * LANES]
        parts.append(da * c + _rot_half(da * s))
    return parts[0] if len(parts) == 1 else jnp.concatenate(parts, axis=1)


def _band_mask(i):
    b = WINDOW
    r = lax.broadcasted_iota(jnp.int32, (b, 2 * b), 0)
    c = lax.broadcasted_iota(jnp.int32, (b, 2 * b), 1)
    no_prev = jnp.where(i > 0, 0, 2 * b)
    return ((c < b) & (c > r + no_prev)) | ((c >= b) & ((c - b) <= r))


def _chunk(x, a):
    return x[:, a * LANES:(a + 1) * LANES]


def _kv_aligned(kp, kc, h):
    band = jnp.concatenate([_chunk(kp, h // 2), _chunk(kc, h // 2)], axis=0).astype(F32)
    swapped = pltpu.roll(band, HEAD_DIM, 1)
    return (band, swapped) if h % 2 == 0 else (swapped, band)


def _swa_fwd(proj, cosf, sinf, sinks, *, nq, name):
    t = proj.shape[0]
    nkv = nq // Q_PER_KV
    aw, kw, b = nq * HEAD_DIM, nkv * HEAD_DIM, WINDOW
    nb = t // b
    kblk = aw // kw
    scale = HEAD_DIM ** -0.5

    def body(sink_ref, q_ref, kc_ref, kp_ref, vc_ref, vp_ref, cc_ref, cp_ref, sc_ref, sp_ref, o_ref, qr_ref, kr_ref):
        i = pl.program_id(0)
        cc, sc, cpv, spv = cc_ref[...], sc_ref[...], cp_ref[...], sp_ref[...]
        qr = _rope(q_ref[...], cc, sc)
        kc = _rope(kc_ref[...], cc, sc)
        kp = _rope(kp_ref[...], cpv, spv)
        qr_ref[...] = qr.astype(BF)
        kr_ref[...] = kc.astype(BF)
        vc, vp = vc_ref[...], vp_ref[...]
        valid = _band_mask(i)
        lo = lax.broadcasted_iota(jnp.int32, (b, LANES), 1) < HEAD_DIM
        for a in range(nq // 2):
            h = (2 * a) // Q_PER_KV
            ks = [x.astype(BF) for x in _kv_aligned(kp, kc, h)]
            vs = [x.astype(BF) for x in _kv_aligned(vp, vc, h)]
            qa = _chunk(qr, a)
            o_par = []
            for par in range(2):
                hq = 2 * a + par
                qm = jnp.where(lo if par == 0 else ~lo, qa, 0.0).astype(BF)
                s = lax.dot_general(qm, ks[par], _DIMS["nt"], preferred_element_type=F32) * scale
                s = jnp.where(valid, s, -jnp.inf)
                sink = sink_ref[hq]
                m = jnp.maximum(jnp.max(s, axis=-1, keepdims=True), sink)
                p = jnp.exp(s - m)
                p = p / (jnp.sum(p, axis=-1, keepdims=True) + jnp.exp(sink - m))
                o_par.append(jnp.dot(p.astype(BF), vs[par], preferred_element_type=F32))
            o_ref[:, a * LANES:(a + 1) * LANES] = jnp.where(lo, o_par[0], o_par[1]).astype(BF)

    def prev_i(i):
        return jnp.maximum(i - 1, 0)

    tab_c = pl.BlockSpec((b, LANES), lambda i: (i, 0))
    tab_p = pl.BlockSpec((b, LANES), lambda i: (prev_i(i), 0))
    return pl.pallas_call(
        body,
        name=name,
        grid=(nb,),
        in_specs=[
            pl.BlockSpec(memory_space=pltpu.SMEM),
            pl.BlockSpec((b, aw), lambda i: (i, 0)),
            pl.BlockSpec((b, kw), lambda i: (i, kblk)),
            pl.BlockSpec((b, kw), lambda i: (prev_i(i), kblk)),
            pl.BlockSpec((b, kw), lambda i: (i, kblk + 1)),
            pl.BlockSpec((b, kw), lambda i: (prev_i(i), kblk + 1)),
            tab_c,
            tab_p,
            tab_c,
            tab_p,
        ],
        out_specs=[
            pl.BlockSpec((b, aw), lambda i: (i, 0)),
            pl.BlockSpec((b, aw), lambda i: (i, 0)),
            pl.BlockSpec((b, kw), lambda i: (i, 0)),
        ],
        out_shape=[
            jax.ShapeDtypeStruct((t, aw), BF),
            jax.ShapeDtypeStruct((t, aw), BF),
            jax.ShapeDtypeStruct((t, kw), BF),
        ],
        compiler_params=_cp("parallel"),
    )(sinks, proj, proj, proj, proj, proj, cosf, cosf, sinf, sinf)


def _swa_bwd(qr, kr, proj, do, cosf, sinf, sinks, *, nq, name):
    t = proj.shape[0]
    nkv = nq // Q_PER_KV
    aw, kw, b = nq * HEAD_DIM, nkv * HEAD_DIM, WINDOW
    nb = t // b
    kblk = aw // kw
    scale = HEAD_DIM ** -0.5

    def body(sink_ref, q_ref, kc_ref, kp_ref, vc_ref, vp_ref, do_ref, cc_ref, cp_ref, sc_ref, sp_ref,
             dq_ref, dk_ref, dv_ref, ds_ref, ck_ref, cv_ref, sacc_ref):
        i = pl.program_id(0)

        @pl.when(i == 0)
        def _():
            ck_ref[...] = jnp.zeros_like(ck_ref)
            cv_ref[...] = jnp.zeros_like(cv_ref)
            sacc_ref[...] = jnp.zeros_like(sacc_ref)

        @pl.when(i < nb)
        def _():
            q = q_ref[...]
            kc, kp = kc_ref[...], kp_ref[...]
            vc, vp = vc_ref[...], vp_ref[...]
            dov = do_ref[...]
            valid = _band_mask(i)
            lane = lax.broadcasted_iota(jnp.int32, (b, LANES), 1)
            lo = lane < HEAD_DIM
            cc, sc = cc_ref[...], sc_ref[...]
            nch = kw // LANES
            dk_ch = [jnp.zeros((2 * b, LANES), F32) for _ in range(nch)]
            dv_ch = [jnp.zeros((2 * b, LANES), F32) for _ in range(nch)]
            sacc = jnp.zeros((b, LANES), F32)
            for a in range(nq // 2):
                h = (2 * a) // Q_PER_KV
                ks = [x.astype(BF) for x in _kv_aligned(kp, kc, h)]
                vs = [x.astype(BF) for x in _kv_aligned(vp, vc, h)]
                qa = _chunk(q, a).astype(F32)
                doa = _chunk(dov, a).astype(F32)
                dq_par = []
                for par in range(2):
                    hq = 2 * a + par
                    mine = lo if par == 0 else ~lo
                    qm = jnp.where(mine, qa, 0.0).astype(BF)
                    dom = jnp.where(mine, doa, 0.0).astype(BF)
                    s = lax.dot_general(qm, ks[par], _DIMS["nt"], preferred_element_type=F32) * scale
                    s = jnp.where(valid, s, -jnp.inf)
                    sink = sink_ref[hq]
                    m = jnp.maximum(jnp.max(s, axis=-1, keepdims=True), sink)
                    e = jnp.exp(s - m)
                    es = jnp.exp(sink - m)
                    zinv = 1.0 / (jnp.sum(e, axis=-1, keepdims=True) + es)
                    p = e * zinv
                    dp = lax.dot_general(dom, vs[par], _DIMS["nt"], preferred_element_type=F32)
                    delta = jnp.sum(p * dp, axis=-1, keepdims=True)
                    dsv = (p * (dp - delta) * scale).astype(BF)
                    sacc = sacc + jnp.where(lane == hq, -(es * zinv) * delta, 0.0)
                    dq_par.append(jnp.dot(dsv, ks[par], preferred_element_type=F32))
                    dkh = lax.dot_general(dsv, qm, _DIMS["tn"], preferred_element_type=F32)
                    dvh = lax.dot_general(p.astype(BF), dom, _DIMS["tn"], preferred_element_type=F32)
                    if par != h % 2:
                        dkh = pltpu.roll(dkh, HEAD_DIM, 1)
                        dvh = pltpu.roll(dvh, HEAD_DIM, 1)
                    dk_ch[h // 2] = dk_ch[h // 2] + dkh
                    dv_ch[h // 2] = dv_ch[h // 2] + dvh
                dqa = jnp.where(lo, dq_par[0], dq_par[1])
                dq_ref[:, a * LANES:(a + 1) * LANES] = _rope_bwd(dqa, cc, sc).astype(BF)
            dk = dk_ch[0] if nch == 1 else jnp.concatenate(dk_ch, axis=1)
            dv = dv_ch[0] if nch == 1 else jnp.concatenate(dv_ch, axis=1)
            dk_ref[...] = _rope_bwd(ck_ref[...] + dk[:b, :], cp_ref[...], sp_ref[...]).astype(BF)
            dv_ref[...] = (cv_ref[...] + dv[:b, :]).astype(BF)
            ck_ref[...] = dk[b:, :]
            cv_ref[...] = dv[b:, :]
            sacc_ref[...] += sacc

        @pl.when(i == nb)
        def _():
            dk_ref[...] = _rope_bwd(ck_ref[...], cp_ref[...], sp_ref[...]).astype(BF)
            dv_ref[...] = cv_ref[...].astype(BF)
            ds_ref[...] = jnp.broadcast_to(jnp.sum(sacc_ref[...], axis=0, keepdims=True), ds_ref.shape)

    def cur_i(i):
        return jnp.minimum(i, nb - 1)

    def prev_i(i):
        return jnp.clip(i - 1, 0, nb - 1)

    tab_c = pl.BlockSpec((b, LANES), lambda i: (cur_i(i), 0))
    tab_p = pl.BlockSpec((b, LANES), lambda i: (prev_i(i), 0))
    return pl.pallas_call(
        body,
        name=name,
        grid=(nb + 1,),
        in_specs=[
            pl.BlockSpec(memory_space=pltpu.SMEM),
            pl.BlockSpec((b, aw), lambda i: (cur_i(i), 0)),
            pl.BlockSpec((b, kw), lambda i: (cur_i(i), 0)),
            pl.BlockSpec((b, kw), lambda i: (prev_i(i), 0)),
            pl.BlockSpec((b, kw), lambda i: (cur_i(i), kblk + 1)),
            pl.BlockSpec((b, kw), lambda i: (prev_i(i), kblk + 1)),
            pl.BlockSpec((b, aw), lambda i: (cur_i(i), 0)),
            tab_c,
            tab_p,
            tab_c,
            tab_p,
        ],
        out_specs=[
            pl.BlockSpec((b, aw), lambda i: (cur_i(i), 0)),
            pl.BlockSpec((b, kw), lambda i: (prev_i(i), 0)),
            pl.BlockSpec((b, kw), lambda i: (prev_i(i), 0)),
            pl.BlockSpec((SUBLANES, LANES), lambda i: (0, 0)),
        ],
        out_shape=[
            jax.ShapeDtypeStruct((t, aw), BF),
            jax.ShapeDtypeStruct((t, kw), BF),
            jax.ShapeDtypeStruct((t, kw), BF),
            jax.ShapeDtypeStruct((SUBLANES, LANES), F32),
        ],
        scratch_shapes=[pltpu.VMEM((b, kw), F32), pltpu.VMEM((b, kw), F32), pltpu.VMEM((b, LANES), F32)],
        compiler_params=_cp("arbitrary"),
    )(sinks, qr, kr, kr, proj, proj, do, cosf, cosf, sinf, sinf)


def _xattn_fwd(xq, kv, *, name, tq=512):
    t, xw = xq.shape
    mtok = kv.shape[0]
    tq = min(tq, t)
    nh = xw // X_HEAD_DIM
    scale = X_HEAD_DIM ** -0.5

    def body(q_ref, kv_ref, o_ref):
        q = q_ref[...]
        kvv = kv_ref[...]
        outs = []
        for h in range(nh):
            sl = slice(h * X_HEAD_DIM, (h + 1) * X_HEAD_DIM)
            k = kvv[:, sl]
            v = kvv[:, xw + h * X_HEAD_DIM: xw + (h + 1) * X_HEAD_DIM]
            s = lax.dot_general(q[:, sl], k, _DIMS["nt"], preferred_element_type=F32) * scale
            e = jnp.exp(s - jnp.max(s, axis=-1, keepdims=True))
            p = e / jnp.sum(e, axis=-1, keepdims=True)
            outs.append(jnp.dot(p.astype(BF), v, preferred_element_type=F32))
        o_ref[...] = jnp.concatenate(outs, axis=1).astype(BF)

    return pl.pallas_call(
        body,
        name=name,
        grid=(t // tq,),
        in_specs=[pl.BlockSpec((tq, xw), lambda i: (i, 0)), pl.BlockSpec((mtok, 2 * xw), lambda i: (0, 0))],
        out_specs=pl.BlockSpec((tq, xw), lambda i: (i, 0)),
        out_shape=jax.ShapeDtypeStruct((t, xw), BF),
        compiler_params=_cp("parallel"),
    )(xq, kv)


def _xattn_bwd(xq, kv, do, *, name, tq=512):
    t, xw = xq.shape
    mtok = kv.shape[0]
    tq = min(tq, t)
    nh = xw // X_HEAD_DIM
    scale = X_HEAD_DIM ** -0.5

    def body(q_ref, kv_ref, do_ref, dq_ref, dkv_ref):
        i = pl.program_id(0)
        q = q_ref[...]
        kvv = kv_ref[...]
        dov = do_ref[...]
        dqs, dks, dvs = [], [], []
        for h in range(nh):
            sl = slice(h * X_HEAD_DIM, (h + 1) * X_HEAD_DIM)
            k = kvv[:, sl]
            v = kvv[:, xw + h * X_HEAD_DIM: xw + (h + 1) * X_HEAD_DIM]
            qh, doh = q[:, sl], dov[:, sl]
            s = lax.dot_general(qh, k, _DIMS["nt"], preferred_element_type=F32) * scale
            e = jnp.exp(s - jnp.max(s, axis=-1, keepdims=True))
            p = e / jnp.sum(e, axis=-1, keepdims=True)
            dp = lax.dot_general(doh, v, _DIMS["nt"], preferred_element_type=F32)
            delta = jnp.sum(p * dp, axis=-1, keepdims=True)
            dsv = (p * (dp - delta) * scale).astype(BF)
            dqs.append(jnp.dot(dsv, k, preferred_element_type=F32))
            dks.append(lax.dot_general(dsv, qh, _DIMS["tn"], preferred_element_type=F32))
            dvs.append(lax.dot_general(p.astype(BF), doh, _DIMS["tn"], preferred_element_type=F32))
        dq_ref[...] = jnp.concatenate(dqs, axis=1).astype(BF)

        @pl.when(i == 0)
        def _():
            dkv_ref[...] = jnp.zeros_like(dkv_ref)

        dkv_ref[...] += jnp.concatenate(dks + dvs, axis=1)

    row = pl.BlockSpec((tq, xw), lambda i: (i, 0))
    full = pl.BlockSpec((mtok, 2 * xw), lambda i: (0, 0))
    return pl.pallas_call(
        body,
        name=name,
        grid=(t // tq,),
        in_specs=[row, full, row],
        out_specs=[row, full],
        out_shape=[jax.ShapeDtypeStruct((t, xw), BF), jax.ShapeDtypeStruct((mtok, 2 * xw), F32)],
        compiler_params=_cp("arbitrary"),
    )(xq, kv, do)


def _adam_math(w, g, m, v):
    m = ADAM_B1 * m + (1.0 - ADAM_B1) * g
    v = ADAM_B2 * v + (1.0 - ADAM_B2) * (g * g)
    m_hat = m / (1.0 - ADAM_B1 ** ADAM_STEP)
    v_hat = v / (1.0 - ADAM_B2 ** ADAM_STEP)
    delta = -ADAM_LR * (m_hat / (jnp.sqrt(v_hat) + ADAM_EPS) + ADAM_WD * w)
    return delta, m, v


def _row_tile(r, c, n_arrays, budget=24 * 1024 * 1024):
    step = 2 * SUBLANES
    cap = max(step, budget // (2 * n_arrays * c * 4))
    if r <= cap:
        return r
    best = None
    for tr in range(step, cap + 1, step):
        if r % tr == 0:
            best = tr
    assert best is not None, (r, c)
    return best


def _adamw_sum(parts, w, m, v, *, name):
    _, r, c = parts.shape
    tr = _row_tile(r, c, 11)

    def body(p_ref, w_ref, m_ref, v_ref, g_ref, d_ref, nm_ref, nv_ref):
        g = p_ref[0].astype(F32)
        for s in range(1, N_DEV):
            g = g + p_ref[s].astype(F32)
        g_ref[...] = g
        d_ref[...], nm_ref[...], nv_ref[...] = _adam_math(w_ref[...], g, m_ref[...], v_ref[...])

    blk = pl.BlockSpec((tr, c), lambda i: (i, 0))
    out = jax.ShapeDtypeStruct((r, c), F32)
    return pl.pallas_call(
        body,
        name=name,
        grid=(r // tr,),
        in_specs=[pl.BlockSpec((N_DEV, tr, c), lambda i: (0, i, 0)), blk, blk, blk],
        out_specs=[blk, blk, blk, blk],
        out_shape=[out, out, out, out],
        compiler_params=_cp("parallel"),
    )(parts, w, m, v)


def _adamw_small(w, g, m, v, *, name):
    def body(w_ref, g_ref, m_ref, v_ref, d_ref, nm_ref, nv_ref):
        d_ref[...], nm_ref[...], nv_ref[...] = _adam_math(w_ref[...], g_ref[...], m_ref[...], v_ref[...])

    out = jax.ShapeDtypeStruct(w.shape, F32)
    return pl.pallas_call(body, name=name, out_shape=[out, out, out])(w, g, m, v)


def _mesh_pos():
    x, y, c = lax.axis_index("x"), lax.axis_index("y"), lax.axis_index("c")
    return x, y, c


def _peer(x, y, c, mask):
    px = 1 - x if mask & 4 else x
    py = 1 - y if mask & 2 else y
    pc = 1 - c if mask & 1 else c
    return (px, py, pc), 4 * px + 2 * py + pc


_HBM = pl.BlockSpec(memory_space=pltpu.HBM)
_SEM = pl.BlockSpec(memory_space=pltpu.SEMAPHORE)
_EFFECT = pltpu.SideEffectType.DATAFLOW_SIDE_EFFECTING


def _me():
    return 4 * lax.axis_index("x") + 2 * lax.axis_index("y") + lax.axis_index("c")


def _landing(own, me):
    land = lax.empty((N_DEV,) + own.shape, own.dtype)
    return lax.dynamic_update_slice(land, own[None], (me, 0, 0))


def _copy(src, land, send_sem, recv_sem, sem0, x, y, c, k, scatter, arriving):
    me = 4 * x + 2 * y + c
    peer, pidx = _peer(x, y, c, k + 1)
    return pltpu.make_async_remote_copy(
        src_ref=src.at[pidx] if scatter else src,
        dst_ref=land.at[pidx if arriving else me],
        send_sem=send_sem.at[sem0 + k], recv_sem=recv_sem.at[sem0 + k], device_id=peer, device_id_type=MESH)


def _exchange_start(groups, *, scatter, name):
    flat = [p for g in groups for p in g]
    n, ng = len(flat), len(groups)

    def body(*refs):
        srcs, lands = refs[:n], refs[n:2 * n]
        sems = refs[2 * n:2 * n + 2 * ng]
        token = refs[-1]
        x, y, c = _mesh_pos()
        w = 0
        for gi, g in enumerate(groups):
            for wi in range(len(g)):
                for k in range(N_DEV - 1):
                    _copy(srcs[w], lands[w], sems[2 * gi], sems[2 * gi + 1], wi * (N_DEV - 1),
                          x, y, c, k, scatter, False).start()
                w += 1
        token[...] = jnp.zeros_like(token)

    sem_shapes = []
    for g in groups:
        sem_shapes += [pltpu.SemaphoreType.DMA((len(g) * (N_DEV - 1),))] * 2
    args = [pltpu.with_memory_space_constraint(s, pltpu.HBM) for s, _ in flat]
    args += [pltpu.with_memory_space_constraint(l, pltpu.HBM) for _, l in flat]
    outs = pl.pallas_call(
        body,
        name=name,
        in_specs=[_HBM] * (2 * n),
        out_specs=[_SEM] * (2 * ng) + [_HBM] * (2 * n) + [pl.BlockSpec(memory_space=pltpu.VMEM)],
        out_shape=sem_shapes + [pltpu.HBM(a.shape, a.dtype) for a in args]
        + [jax.ShapeDtypeStruct((SUBLANES, LANES), F32)],
        input_output_aliases={i: 2 * ng + i for i in range(2 * n)},
        compiler_params=pltpu.CompilerParams(has_side_effects=_EFFECT),
    )(*args)
    sems, thru, token = outs[:2 * ng], outs[2 * ng:2 * ng + 2 * n], outs[-1]
    res, w = [], 0
    for gi, g in enumerate(groups):
        m = len(g)
        res.append((sems[2 * gi], sems[2 * gi + 1], list(thru[w:w + m]), list(thru[n + w:n + w + m])))
        w += m
    return res, token


def _exchange_wait(group, after, *, scatter, name):
    send_sems, recv_sems, srcs_in, lands_in = group
    n = len(srcs_in)

    def body(*refs):
        srcs, lands = refs[:n], refs[n:2 * n]
        send_sem, recv_sem = refs[2 * n], refs[2 * n + 1]
        x, y, c = _mesh_pos()
        for w in range(n):
            for k in range(N_DEV - 1):
                cp = _copy(srcs[w], lands[w], send_sem, recv_sem, w * (N_DEV - 1), x, y, c, k, scatter, True)
                cp.wait_send()
                cp.wait_recv()

    outs = pl.pallas_call(
        body,
        name=name,
        in_specs=[_HBM] * (2 * n) + [_SEM, _SEM, pl.BlockSpec(memory_space=pl.ANY)],
        out_specs=[_HBM] * (2 * n),
        out_shape=[pltpu.HBM(a.shape, a.dtype) for a in srcs_in + lands_in],
        input_output_aliases={i: i for i in range(2 * n)},
        compiler_params=pltpu.CompilerParams(has_side_effects=_EFFECT),
    )(*srcs_in, *lands_in, send_sems, recv_sems, after)
    return list(outs[n:])


def _all_reduce_small(parts, rows, width, *, name):
    n = len(parts)

    def body(*refs):
        ins = refs[:n]
        o_ref, pack_ref, buf_ref, send_sems, recv_sems = refs[n:]
        x, y, c_ = _mesh_pos()
        me = 4 * x + 2 * y + c_
        pack_ref[...] = jnp.zeros_like(pack_ref)
        for ref, (arr, r0, nr) in zip(ins, parts):
            pack_ref[r0:r0 + nr, 0:arr.shape[1]] = ref[0:nr, :]
        sends, recvs = [], []
        for k in range(N_DEV - 1):
            peer, pidx = _peer(x, y, c_, k + 1)
            cp = pltpu.make_async_remote_copy(
                src_ref=pack_ref, dst_ref=buf_ref.at[me], send_sem=send_sems.at[k], recv_sem=recv_sems.at[k],
                device_id=peer, device_id_type=MESH)
            cp.start()
            sends.append(cp)
            recvs.append(pltpu.make_async_remote_copy(
                src_ref=pack_ref, dst_ref=buf_ref.at[pidx], send_sem=send_sems.at[k], recv_sem=recv_sems.at[k],
                device_id=peer, device_id_type=MESH))
        buf_ref[me] = pack_ref[...]
        for rc in recvs:
            rc.wait_recv()
        for cp in sends:
            cp.wait_send()
        acc = buf_ref[0]
        for s in range(1, N_DEV):
            acc = acc + buf_ref[s]
        o_ref[...] = acc

    vmem = pl.BlockSpec(memory_space=pltpu.VMEM)
    return pl.pallas_call(
        body,
        name=name,
        in_specs=[vmem] * n,
        out_specs=vmem,
        out_shape=jax.ShapeDtypeStruct((rows, width), F32),
        scratch_shapes=[
            pltpu.VMEM((rows, width), F32),
            pltpu.VMEM((N_DEV, rows, width), F32),
            pltpu.SemaphoreType.DMA((N_DEV - 1,)),
            pltpu.SemaphoreType.DMA((N_DEV - 1,)),
        ],
    )(*[p[0] for p in parts])


def _rope_tables(t):
    half = HEAD_DIM // 2
    inv_freq = ROPE_THETA ** (-jnp.arange(half, dtype=F32) / half)
    ang = jnp.arange(t, dtype=jnp.int32).astype(F32)[:, None] * inv_freq[None, :]
    cos, sin = jnp.cos(ang), jnp.sin(ang)
    cosf = jnp.concatenate([cos, cos, cos, cos], axis=1)
    sinf = jnp.concatenate([-sin, sin, -sin, sin], axis=1)
    return cosf, sinf


def _local_step(x, mem, target, gains, sinks, aw, cw, get_w, put_g, dep0=None):
    t, d = x.shape
    nq = aw // HEAD_DIM
    kw = aw // Q_PER_KV
    z0 = aw + 2 * kw
    gb0, gc0 = z0 + cw, z0 + 2 * cw
    ga0 = z0 + 3 * cw
    gcm0 = ga0 + d
    cosf, sinf = _rope_tables(t)

    u1 = _rms_fwd(x, gains["g_mix"], name="rms_mix", dep=dep0)
    mem_n = _rms_fwd(mem, gains["g_mem"], name="rms_mem", dep=dep0)
    w_in = get_w("w_in", mem_n)
    proj = _mm(u1, w_in, mode="nn", tm=1024, tn=512, tk=2048, out_dtype=F32, name="mm_in")
    o_attn, q_rot, k_rot = _swa_fwd(proj, cosf, sinf, sinks, nq=nq, name="swa_fwd")
    conv_w8 = get_w("conv_w8", o_attn)
    w_attn_proj, w_conv_proj, w_mix_out = (get_w(n, o_attn) for n in ("w_attn_proj", "w_conv_proj", "w_mix_out"))
    w_xq, w_xkv, w_xo = (get_w(n, o_attn) for n in ("w_xq", "w_xkv", "w_xo"))
    y_attn = _mm(o_attn, w_attn_proj, mode="nn", tm=1024, tn=1024, tk=1024, out_dtype=F32, name="mm_attn_proj")
    cy = _conv_fwd(proj, conv_w8, z0=z0, gb0=gb0, gc0=gc0, cw=cw, name="conv_fwd")
    y_conv = _mm(cy, w_conv_proj, mode="nn", tm=1024, tn=1024, tk=1024, out_dtype=F32, name="mm_conv_proj")
    merged = _gate_fwd(proj, y_attn, y_conv, ga0=ga0, gc0=gcm0, name="gate_fwd")
    h1 = _mm(merged, w_mix_out, mode="nn", tm=1024, tn=1024, tk=2048, out_dtype=F32, name="mm_mix_out", residual=x)
    u2 = _rms_fwd(h1, gains["g_xattn"], name="rms_xattn")
    xq = _mm(u2, w_xq, mode="nn", tm=1024, tn=512, tk=2048, out_dtype=BF, name="mm_xq")
    kv = _mm(mem_n, w_xkv, mode="nn", tm=256, tn=1024, tk=2048, out_dtype=BF, name="mm_xkv")
    o_x = _xattn_fwd(xq, kv, name="xattn_fwd")
    h2 = _mm(o_x, w_xo, mode="nn", tm=1024, tn=1024, tk=512, out_dtype=F32, name="mm_xo", residual=h1)
    u3 = _rms_fwd(h2, gains["g_ffn"], name="rms_ffn")
    w_ffn_in = get_w("w_ffn_in", u3)
    hid = _mm(u3, w_ffn_in, mode="nn", tm=1024, tn=512, tk=2048, out_dtype=F32, name="mm_ffn_in")
    act = _swiglu_fwd(hid, name="swiglu_fwd")
    w_ffn_out = get_w("w_ffn_out", act)
    h3 = _mm(act, w_ffn_out, mode="nn", tm=1024, tn=1024, tk=512, out_dtype=F32, name="mm_ffn_out", residual=h2)

    dh3, loss_tile, dg_final = _loss_head(h3, target, gains["g_final"], name="loss_head")
    tok = put_g("w_ffn_out", _mm(act, dh3, mode="tn", tm=512, tn=1024, tk=1024, out_dtype=BF, name="mm_dw_ffn_out"))
    dact = _mm(dh3, w_ffn_out, mode="nt", tm=1024, tn=512, tk=2048, out_dtype=F32, name="mm_dact", dep=tok)
    dhid = _swiglu_bwd(dact, hid, name="swiglu_bwd")
    tok = put_g("w_ffn_in", _mm(u3, dhid, mode="tn", tm=1024, tn=512, tk=1024, out_dtype=BF, name="mm_dw_ffn_in"))
    du3 = _mm(dhid, w_ffn_in, mode="nt", tm=1024, tn=1024, tk=512, out_dtype=F32, name="mm_du3", dep=tok)
    dh2, dg_ffn = _rms_bwd(du3, h2, gains["g_ffn"], dh3, name="rms_ffn_bwd")
    put_g("w_xo", _mm(o_x, dh2, mode="tn", tm=512, tn=1024, tk=1024, out_dtype=BF, name="mm_dw_xo"))
    do_x = _mm(dh2, w_xo, mode="nt", tm=1024, tn=512, tk=2048, out_dtype=BF, name="mm_do_x")
    dxq, dkv = _xattn_bwd(xq, kv, do_x, name="xattn_bwd")
    put_g("w_xkv", _mm(mem_n, dkv, mode="tn", tm=1024, tn=1024, tk=256, out_dtype=BF, name="mm_dw_xkv"))
    tok = put_g("w_xq", _mm(u2, dxq, mode="tn", tm=1024, tn=512, tk=1024, out_dtype=BF, name="mm_dw_xq"))
    du2 = _mm(dxq, w_xq, mode="nt", tm=1024, tn=1024, tk=512, out_dtype=F32, name="mm_du2", dep=tok)
    dmem_n = _mm(dkv, w_xkv, mode="nt", tm=256, tn=1024, tk=1024, out_dtype=F32, name="mm_dmem")
    _, dg_mem = _rms_bwd(dmem_n, mem, gains["g_mem"], None, name="rms_mem_bwd")
    dh1, dg_xattn = _rms_bwd(du2, h1, gains["g_xattn"], dh2, name="rms_xattn_bwd")
    put_g("w_mix_out", _mm(merged, dh1, mode="tn", tm=1024, tn=1024, tk=1024, out_dtype=BF, name="mm_dw_mix_out"))
    dmerged = _mm(dh1, w_mix_out, mode="nt", tm=1024, tn=1024, tk=2048, out_dtype=F32, name="mm_dmerged")
    dya, dyc, dga, dgc = _gate_bwd(dmerged, proj, y_attn, y_conv, ga0=ga0, gc0=gcm0, name="gate_bwd")
    put_g("w_attn_proj", _mm(o_attn, dya, mode="tn", tm=1024, tn=1024, tk=1024, out_dtype=BF, name="mm_dw_attn_proj"))
    do_attn = _mm(dya, w_attn_proj, mode="nt", tm=1024, tn=1024, tk=2048, out_dtype=BF, name="mm_do_attn")
    tok = put_g("w_conv_proj",
                _mm(cy, dyc, mode="tn", tm=1024, tn=1024, tk=1024, out_dtype=BF, name="mm_dw_conv_proj"))
    dcy = _mm(dyc, w_conv_proj, mode="nt", tm=1024, tn=1024, tk=2048, out_dtype=F32, name="mm_dcy", dep=tok)
    dz, dgb, dgcv, dconv_w8 = _conv_bwd(proj, conv_w8, dcy, z0=z0, gb0=gb0, gc0=gc0, cw=cw, name="conv_bwd")
    dq, dk, dv, dsink_tile = _swa_bwd(q_rot, k_rot, proj, do_attn, cosf, sinf, sinks, nq=nq, name="swa_bwd")
    dproj = jnp.concatenate([dq, dk, dv, dz, dgb, dgcv, dga, dgc], axis=1)
    tok = put_g("w_in", _mm(u1, dproj, mode="tn", tm=1024, tn=512, tk=1024, out_dtype=BF, name="mm_dw_in"))
    du1 = _mm(dproj, w_in, mode="nt", tm=1024, tn=1024, tk=512, out_dtype=F32, name="mm_du1", dep=tok)
    grad_x, dg_mix = _rms_bwd(du1, x, gains["g_mix"], dh1, name="rms_mix_bwd")

    small = {
        "g_mix": dg_mix, "g_xattn": dg_xattn, "g_mem": dg_mem, "g_ffn": dg_ffn, "g_final": dg_final,
        "attn_sinks": dsink_tile, "conv_w8": dconv_w8, "loss": loss_tile,
    }
    return grad_x, small


_COL_SHARDED = ("w_in", "w_attn_proj", "w_conv_proj", "w_xo", "w_ffn_in")
_ROW_SHARDED = ("w_mix_out", "w_xq", "w_xkv", "w_ffn_out")
_BIG = _COL_SHARDED + _ROW_SHARDED
_GAINS = ("g_mix", "g_xattn", "g_mem", "g_ffn", "g_final")
_GATHER_GROUPS = (("w_in",), ("conv_w8", "w_attn_proj", "w_conv_proj", "w_mix_out", "w_xq", "w_xkv", "w_xo"),
                  ("w_ffn_in",), ("w_ffn_out",))
_SCATTER_GROUPS = (("w_ffn_out",), ("w_ffn_in",), ("w_xo", "w_xq", "w_xkv"),
                   ("w_mix_out", "w_attn_proj", "w_conv_proj"), ("w_in",))
_WEIGHTS = ("g_mix", "w_in", "conv_w", "attn_sinks", "w_attn_proj", "w_conv_proj", "w_mix_out", "g_xattn", "g_mem",
            "w_xq", "w_xkv", "w_xo", "g_ffn", "w_ffn_in", "w_ffn_out", "g_final")


def _unstack(g, col_sharded):
    n, r, c = g.shape
    if col_sharded:
        return jnp.transpose(g, (1, 0, 2)).reshape(r, n * c)
    return g.reshape(n * r, c)


def _stack(w, col_sharded):
    r, c = w.shape
    if col_sharded:
        return jnp.transpose(w.reshape(r, N_DEV, c // N_DEV), (1, 0, 2))
    return w.reshape(N_DEV, r // N_DEV, c)


def kernel(x, mem, g_mix, w_in, conv_w, attn_sinks, w_attn_proj, w_conv_proj, w_mix_out, g_xattn, g_mem, w_xq, w_xkv, w_xo, g_ffn, w_ffn_in, w_ffn_out, g_final, loss_target, m_g_mix, m_w_in, m_conv_w, m_attn_sinks, m_w_attn_proj, m_w_conv_proj, m_w_mix_out, m_g_xattn, m_g_mem, m_w_xq, m_w_xkv, m_w_xo, m_g_ffn, m_w_ffn_in, m_w_ffn_out, m_g_final, v_g_mix, v_w_in, v_conv_w, v_attn_sinks, v_w_attn_proj, v_w_conv_proj, v_w_mix_out, v_g_xattn, v_g_mem, v_w_xq, v_w_xkv, v_w_xo, v_g_ffn, v_w_ffn_in, v_w_ffn_out, v_g_final):
    w_ = dict(g_mix=g_mix, w_in=w_in, conv_w=conv_w, attn_sinks=attn_sinks, w_attn_proj=w_attn_proj,
              w_conv_proj=w_conv_proj, w_mix_out=w_mix_out, g_xattn=g_xattn, g_mem=g_mem, w_xq=w_xq, w_xkv=w_xkv,
              w_xo=w_xo, g_ffn=g_ffn, w_ffn_in=w_ffn_in, w_ffn_out=w_ffn_out, g_final=g_final)
    m_ = dict(g_mix=m_g_mix, w_in=m_w_in, conv_w=m_conv_w, attn_sinks=m_attn_sinks, w_attn_proj=m_w_attn_proj,
              w_conv_proj=m_w_conv_proj, w_mix_out=m_w_mix_out, g_xattn=m_g_xattn, g_mem=m_g_mem, w_xq=m_w_xq,
              w_xkv=m_w_xkv, w_xo=m_w_xo, g_ffn=m_g_ffn, w_ffn_in=m_w_ffn_in, w_ffn_out=m_w_ffn_out,
              g_final=m_g_final)
    v_ = dict(g_mix=v_g_mix, w_in=v_w_in, conv_w=v_conv_w, attn_sinks=v_attn_sinks, w_attn_proj=v_w_attn_proj,
              w_conv_proj=v_w_conv_proj, w_mix_out=v_w_mix_out, g_xattn=v_g_xattn, g_mem=v_g_mem, w_xq=v_w_xq,
              w_xkv=v_w_xkv, w_xo=v_w_xo, g_ffn=v_g_ffn, w_ffn_in=v_w_ffn_in, w_ffn_out=v_w_ffn_out,
              g_final=v_g_final)
    t, d = x.shape[1], x.shape[2]
    nq = attn_sinks.shape[-1]
    cw_shard = conv_w.shape[-1]
    cw = cw_shard * N_DEV

    def two_d(a):
        return a.reshape(a.shape[-2], a.shape[-1]) if a.ndim == 3 else a.reshape(1, a.shape[-1])

    me = _me()
    col = set(_COL_SHARDED) | {"conv_w8"}

    shards = {n: two_d(w_[n]).astype(BF) for n in _BIG}
    shards["conv_w8"] = jnp.zeros((SUBLANES, cw_shard), F32).at[:3].set(two_d(conv_w))
    gathers, token = _exchange_start(
        [[(shards[n], _landing(shards[n], me)) for n in g] for g in _GATHER_GROUPS], scatter=False,
        name="gather_start")
    full = {}

    def get_w(name, after):
        if name not in full:
            gi = [name in g for g in _GATHER_GROUPS].index(True)
            lands = _exchange_wait(gathers[gi], after, scatter=False, name="gather_wait_%d" % gi)
            for n, land in zip(_GATHER_GROUPS[gi], lands):
                full[n] = _unstack(land, n in col)
        return full[name]

    pending, scatters = {}, []

    def put_g(name, dw):
        pending[name] = _stack(dw, name in col)
        gi = [name in g for g in _SCATTER_GROUPS].index(True)
        group = _SCATTER_GROUPS[gi]
        if not all(n in pending for n in group):
            return None
        pairs = [(pending[n], _landing(lax.dynamic_index_in_dim(pending[n], me, 0, keepdims=False), me))
                 for n in group]
        started, tok = _exchange_start([pairs], scatter=True, name="scatter_start_%d" % gi)
        scatters.append((gi, started[0]))
        return tok

    gains = {n: two_d(w_[n]) for n in _GAINS}
    grad_x, small = _local_step(
        x[0], mem[0], loss_target[0], gains, attn_sinks.reshape(nq), w_attn_proj.shape[-2], cw, get_w, put_g,
        dep0=token)

    grads, deltas, new_m, new_v = {}, {}, {}, {}
    after = grad_x
    for gi, started in scatters:
        parts = _exchange_wait(started, after, scatter=True, name="scatter_wait_%d" % gi)
        for n, p in zip(_SCATTER_GROUPS[gi], parts):
            shape = w_[n].shape
            g, dl, nm, nv = _adamw_sum(p, two_d(w_[n]), two_d(m_[n]), two_d(v_[n]), name="adamw_" + n)
            grads[n], deltas[n], new_m[n], new_v[n] = (a.reshape(shape) for a in (g, dl, nm, nv))
            after = g

    parts = [(small[n], i, 1) for i, n in enumerate(_GAINS)]
    parts += [(small["attn_sinks"], 5, 1), (small["loss"], 6, 1), (small["conv_w8"], 8, 3)]
    red = _all_reduce_small(parts, 2 * SUBLANES, max(d, cw), name="reduce_small")
    loss = red[6, 0]
    small_g = {n: red[i:i + 1, :d] for i, n in enumerate(_GAINS)}
    small_g["attn_sinks"] = red[5:6, :nq]
    small_g["conv_w"] = lax.dynamic_slice(red, (8, me * cw_shard), (3, cw_shard))
    for n in _GAINS + ("attn_sinks", "conv_w"):
        shape = w_[n].shape
        g = small_g[n]
        dl, nm, nv = _adamw_small(two_d(w_[n]), g, two_d(m_[n]), two_d(v_[n]), name="adamw_" + n)
        grads[n], deltas[n], new_m[n], new_v[n] = (a.reshape(shape) for a in (g, dl, nm, nv))

    return (loss, grad_x[None], *[grads[n] for n in _WEIGHTS], *[deltas[n] for n in _WEIGHTS],
            *[new_m[n] for n in _WEIGHTS], *[new_v[n] for n in _WEIGHTS])
```

```python
import functools
import math

import jax
import jax.numpy as jnp
from jax import lax
from jax.experimental import pallas as pl
from jax.experimental.pallas import tpu as pltpu

HEAD_DIM = 64
Q_PER_KV = 4
WINDOW = 128
X_HEAD_DIM = 128
ROPE_THETA = 10000.0
EPS = 1e-6
ADAM_LR = 0.001
ADAM_B1 = 0.9
ADAM_B2 = 0.999
ADAM_EPS = 1e-08
ADAM_WD = 0.01
ADAM_STEP = 10

N_DEV = 8
LANES = 128
SUBLANES = 8
VMEM_LIMIT_BYTES = 56 * 1024 * 1024
BF = jnp.bfloat16
F32 = jnp.float32
MESH = pl.DeviceIdType.MESH


def _cp(*sem):
    return pltpu.CompilerParams(dimension_semantics=sem, vmem_limit_bytes=VMEM_LIMIT_BYTES)


def _sigmoid(x):
    return 1.0 / (1.0 + jnp.exp(-x))


_DIMS = {
    "nn": (((1,), (0,)), ((), ())),
    "nt": (((1,), (1,)), ((), ())),
    "tn": (((0,), (0,)), ((), ())),
}


def _fit(dim, tile):
    tile = min(tile, dim)
    while dim % tile and tile > LANES:
        tile //= 2
    return tile


def _mm(a, b, *, mode, tm, tn, tk, out_dtype, name, residual=None, dep=None):
    if mode == "nn":
        (m, k), (k2, n) = a.shape, b.shape
    elif mode == "nt":
        (m, k), (n, k2) = a.shape, b.shape
    else:
        (k, m), (k2, n) = a.shape, b.shape
    assert k == k2, (name, a.shape, b.shape)
    tm, tn, tk = _fit(m, tm), _fit(n, tn), _fit(k, tk)
    assert m % tm == 0 and n % tn == 0 and k % tk == 0, (name, m, n, k, tm, tn, tk)
    nk = k // tk
    if mode == "tn":
        a_spec = pl.BlockSpec((tk, tm), lambda i, j, kk: (kk, i))
    else:
        a_spec = pl.BlockSpec((tm, tk), lambda i, j, kk: (i, kk))
    if mode == "nt":
        b_spec = pl.BlockSpec((tn, tk), lambda i, j, kk: (j, kk))
    else:
        b_spec = pl.BlockSpec((tk, tn), lambda i, j, kk: (kk, j))
    o_spec = pl.BlockSpec((tm, tn), lambda i, j, kk: (i, j))
    dims = _DIMS[mode]
    has_res = residual is not None
    n_in = 2 + has_res + (dep is not None)

    def body(*refs):
        a_ref, b_ref, r_ref, o_ref = refs[0], refs[1], refs[2], refs[n_in]
        part = lax.dot_general(a_ref[...].astype(BF), b_ref[...].astype(BF), dims, preferred_element_type=F32)

        def finish(acc):
            if has_res:
                acc = r_ref[...] + acc
            o_ref[...] = acc.astype(out_dtype)

        if nk == 1:
            finish(part)
        else:
            acc_ref = refs[-1]
            kk = pl.program_id(2)

            @pl.when(kk == 0)
            def _():
                acc_ref[...] = part

            @pl.when(kk > 0)
            def _():
                acc_ref[...] += part

            @pl.when(kk == nk - 1)
            def _():
                finish(acc_ref[...])

    in_specs = [a_spec, b_spec] + ([o_spec] if has_res else [])
    args = (a, b) + ((residual,) if has_res else ())
    if dep is not None:
        in_specs.append(pl.BlockSpec(memory_space=pl.ANY))
        args += (dep,)
    return pl.pallas_call(
        body,
        name=name,
        grid=(m // tm, n // tn, nk),
        in_specs=in_specs,
        out_specs=o_spec,
        out_shape=jax.ShapeDtypeStruct((m, n), out_dtype),
        scratch_shapes=[pltpu.VMEM((tm, tn), F32)] if nk > 1 else [],
        compiler_params=_cp("parallel", "parallel", "arbitrary"),
    )(*args)


def _rms_fwd(h, g, *, name, tm=512, dep=None):
    t, d = h.shape
    tm = min(tm, t)

    def body(*refs):
        h_ref, g_ref, u_ref = refs[0], refs[1], refs[-1]
        hv = h_ref[...]
        r = lax.rsqrt(jnp.mean(hv * hv, axis=-1, keepdims=True) + EPS)
        u_ref[...] = ((hv * r) * g_ref[...]).astype(BF)

    in_specs = [pl.BlockSpec((tm, d), lambda i: (i, 0)), pl.BlockSpec((1, d), lambda i: (0, 0))]
    args = (h, g)
    if dep is not None:
        in_specs.append(pl.BlockSpec(memory_space=pl.ANY))
        args += (dep,)
    return pl.pallas_call(
        body,
        name=name,
        grid=(t // tm,),
        in_specs=in_specs,
        out_specs=pl.BlockSpec((tm, d), lambda i: (i, 0)),
        out_shape=jax.ShapeDtypeStruct((t, d), BF),
        compiler_params=_cp("parallel"),
    )(*args)


def _rms_bwd(du, h, g, dres, *, name, tm=256):
    t, d = h.shape
    tm = min(tm, t)
    want_dh = dres is not None

    def body(*refs):
        if want_dh:
            du_ref, h_ref, g_ref, dres_ref, dh_ref, dhb_ref, dg_ref = refs
        else:
            du_ref, h_ref, g_ref, dg_ref = refs
        i = pl.program_id(0)
        hv = h_ref[...]
        duv = du_ref[...]
        r = lax.rsqrt(jnp.mean(hv * hv, axis=-1, keepdims=True) + EPS)
        nv = hv * r
        if want_dh:
            gy = duv * g_ref[...]
            dh = dres_ref[...] + r * (gy - nv * jnp.mean(nv * gy, axis=-1, keepdims=True))
            dh_ref[...] = dh
            dhb_ref[...] = dh.astype(BF)

        @pl.when(i == 0)
        def _():
            dg_ref[...] = jnp.zeros_like(dg_ref)

        dg_ref[...] += jnp.sum(duv * nv, axis=0, keepdims=True)

    row = pl.BlockSpec((tm, d), lambda i: (i, 0))
    vec = pl.BlockSpec((1, d), lambda i: (0, 0))
    if want_dh:
        in_specs, args = [row, row, vec, row], (du, h, g, dres)
        out_specs = [row, row, vec]
        out_shape = [jax.ShapeDtypeStruct((t, d), F32), jax.ShapeDtypeStruct((t, d), BF),
                     jax.ShapeDtypeStruct((1, d), F32)]
    else:
        in_specs, args = [row, row, vec], (du, h, g)
        out_specs = [vec]
        out_shape = [jax.ShapeDtypeStruct((1, d), F32)]
    outs = pl.pallas_call(
        body,
        name=name,
        grid=(t // tm,),
        in_specs=in_specs,
        out_specs=out_specs,
        out_shape=out_shape,
        compiler_params=_cp("arbitrary"),
    )(*args)
    return (outs[0], outs[1], outs[2]) if want_dh else (None, None, outs[0])


def _loss_head(h, target, g, *, name, tm=256):
    t, d = h.shape
    tm = min(tm, t)

    def body(h_ref, t_ref, g_ref, dh_ref, dhb_ref, loss_ref, dg_ref):
        i = pl.program_id(0)
        hv = h_ref[...]
        gv = g_ref[...]
        r = lax.rsqrt(jnp.mean(hv * hv, axis=-1, keepdims=True) + EPS)
        nv = hv * r
        e = nv * gv - t_ref[...]
        per_tok = jnp.mean(e * e, axis=-1, keepdims=True)
        lp = 0.5 * jnp.sum(per_tok, axis=0, keepdims=True)
        dy = e * (1.0 / d)
        gy = dy * gv
        dh = r * (gy - nv * jnp.mean(nv * gy, axis=-1, keepdims=True))
        dh_ref[...] = dh
        dhb_ref[...] = dh.astype(BF)

        @pl.when(i == 0)
        def _():
            loss_ref[...] = jnp.zeros_like(loss_ref)
            dg_ref[...] = jnp.zeros_like(dg_ref)

        loss_ref[...] += jnp.broadcast_to(lp, loss_ref.shape)
        dg_ref[...] += jnp.sum(dy * nv, axis=0, keepdims=True)

    row = pl.BlockSpec((tm, d), lambda i: (i, 0))
    vec = pl.BlockSpec((1, d), lambda i: (0, 0))
    return pl.pallas_call(
        body,
        name=name,
        grid=(t // tm,),
        in_specs=[row, row, vec],
        out_specs=[row, row, pl.BlockSpec((SUBLANES, LANES), lambda i: (0, 0)), vec],
        out_shape=[
            jax.ShapeDtypeStruct((t, d), F32),
            jax.ShapeDtypeStruct((t, d), BF),
            jax.ShapeDtypeStruct((SUBLANES, LANES), F32),
            jax.ShapeDtypeStruct((1, d), F32),
        ],
        compiler_params=_cp("arbitrary"),
    )(h, target, g)


def _swiglu_fwd(hid, *, name, tm=512, tc=512):
    t, f2 = hid.shape
    f = f2 // 2
    tm, tc = min(tm, t), min(tc, f)
    nf = f // tc

    def body(a_ref, b_ref, o_ref):
        a = a_ref[...]
        o_ref[...] = ((a * _sigmoid(a)) * b_ref[...]).astype(BF)

    return pl.pallas_call(
        body,
        name=name,
        grid=(t // tm, nf),
        in_specs=[pl.BlockSpec((tm, tc), lambda i, j: (i, j)), pl.BlockSpec((tm, tc), lambda i, j: (i, nf + j))],
        out_specs=pl.BlockSpec((tm, tc), lambda i, j: (i, j)),
        out_shape=jax.ShapeDtypeStruct((t, f), BF),
        compiler_params=_cp("parallel", "parallel"),
    )(hid, hid)


def _swiglu_bwd(dact, hid, *, name, tm=512, tc=512):
    t, f2 = hid.shape
    f = f2 // 2
    tm, tc = min(tm, t), min(tc, f)
    nf = f // tc

    def body(d_ref, a_ref, b_ref, o_ref):
        j = pl.program_id(1)
        a = a_ref[...]
        dv = d_ref[...]
        sg = _sigmoid(a)

        @pl.when(j < nf)
        def _():
            o_ref[...] = (dv * b_ref[...] * (sg * (1.0 + a * (1.0 - sg)))).astype(BF)

        @pl.when(j >= nf)
        def _():
            o_ref[...] = (dv * (a * sg)).astype(BF)

    return pl.pallas_call(
        body,
        name=name,
        grid=(t // tm, 2 * nf),
        in_specs=[
            pl.BlockSpec((tm, tc), lambda i, j: (i, j % nf)),
            pl.BlockSpec((tm, tc), lambda i, j: (i, j % nf)),
            pl.BlockSpec((tm, tc), lambda i, j: (i, nf + j % nf)),
        ],
        out_specs=pl.BlockSpec((tm, tc), lambda i, j: (i, j)),
        out_shape=jax.ShapeDtypeStruct((t, f2), BF),
        compiler_params=_cp("parallel", "parallel"),
    )(dact, hid, hid)


def _gate_fwd(proj, ya, yc, *, ga0, gc0, name, tm=512, tc=512):
    t, d = ya.shape
    tm, tc = min(tm, t), math.gcd(tc, d, ga0, gc0)
    a0, c0 = ga0 // tc, gc0 // tc

    def body(ga_ref, gc_ref, ya_ref, yc_ref, o_ref):
        o_ref[...] = (_sigmoid(ga_ref[...]) * ya_ref[...] + _sigmoid(gc_ref[...]) * yc_ref[...]).astype(BF)

    blk = pl.BlockSpec((tm, tc), lambda i, j: (i, j))
    return pl.pallas_call(
        body,
        name=name,
        grid=(t // tm, d // tc),
        in_specs=[
            pl.BlockSpec((tm, tc), lambda i, j: (i, a0 + j)),
            pl.BlockSpec((tm, tc), lambda i, j: (i, c0 + j)),
            blk,
            blk,
        ],
        out_specs=blk,
        out_shape=jax.ShapeDtypeStruct((t, d), BF),
        compiler_params=_cp("parallel", "parallel"),
    )(proj, proj, ya, yc)


def _gate_bwd(dm, proj, ya, yc, *, ga0, gc0, name, tm=512, tc=512):
    t, d = ya.shape
    tm, tc = min(tm, t), math.gcd(tc, d, ga0, gc0)
    a0, c0 = ga0 // tc, gc0 // tc

    def body(dm_ref, ga_ref, gc_ref, ya_ref, yc_ref, dya_ref, dyc_ref, dga_ref, dgc_ref):
        dmv = dm_ref[...]
        sa = _sigmoid(ga_ref[...])
        sc = _sigmoid(gc_ref[...])
        dya_ref[...] = (dmv * sa).astype(BF)
        dyc_ref[...] = (dmv * sc).astype(BF)
        dga_ref[...] = (dmv * ya_ref[...] * (sa * (1.0 - sa))).astype(BF)
        dgc_ref[...] = (dmv * yc_ref[...] * (sc * (1.0 - sc))).astype(BF)

    blk = pl.BlockSpec((tm, tc), lambda i, j: (i, j))
    out = jax.ShapeDtypeStruct((t, d), BF)
    return pl.pallas_call(
        body,
        name=name,
        grid=(t // tm, d // tc),
        in_specs=[
            blk,
            pl.BlockSpec((tm, tc), lambda i, j: (i, a0 + j)),
            pl.BlockSpec((tm, tc), lambda i, j: (i, c0 + j)),
            blk,
            blk,
        ],
        out_specs=[blk, blk, blk, blk],
        out_shape=[out, out, out, out],
        compiler_params=_cp("parallel", "parallel"),
    )(dm, proj, proj, ya, yc)


def _conv_taps(cz, czp, i):
    czp = czp * (i > 0).astype(F32)
    h1 = czp[SUBLANES - 1:SUBLANES, :]
    h2 = czp[SUBLANES - 2:SUBLANES - 1, :]
    row = lax.broadcasted_iota(jnp.int32, cz.shape, 0)
    s1 = jnp.where(row == 0, h1, pltpu.roll(cz, 1, 0))
    s2 = jnp.where(row == 0, h2, jnp.where(row == 1, h1, pltpu.roll(cz, 2, 0)))
    return s1, s2


def _conv_fwd(proj, w8, *, z0, gb0, gc0, cw, name, tm=512, tc=512):
    t = proj.shape[0]
    tm, tc = min(tm, t), math.gcd(tc, cw, z0, gb0, gc0)
    zb, bb, cb = z0 // tc, gb0 // tc, gc0 // tc
    rb = tm // SUBLANES

    def body(z_ref, gb_ref, gc_ref, zp_ref, gcp_ref, w_ref, o_ref):
        i = pl.program_id(0)
        cz = gc_ref[...] * z_ref[...]
        s1, s2 = _conv_taps(cz, gcp_ref[...] * zp_ref[...], i)
        w = w_ref[...]
        y = w[0:1, :] * s2 + w[1:2, :] * s1 + w[2:3, :] * cz
        o_ref[...] = (gb_ref[...] * y).astype(BF)

    def cur(b0):
        return pl.BlockSpec((tm, tc), lambda i, j: (i, b0 + j))

    def prev(b0):
        return pl.BlockSpec((SUBLANES, tc), lambda i, j: (jnp.maximum(i * rb - 1, 0), b0 + j))

    return pl.pallas_call(
        body,
        name=name,
        grid=(t // tm, cw // tc),
        in_specs=[cur(zb), cur(bb), cur(cb), prev(zb), prev(cb), pl.BlockSpec((SUBLANES, tc), lambda i, j: (0, j))],
        out_specs=pl.BlockSpec((tm, tc), lambda i, j: (i, j)),
        out_shape=jax.ShapeDtypeStruct((t, cw), BF),
        compiler_params=_cp("parallel", "parallel"),
    )(proj, proj, proj, proj, proj, w8)


def _conv_bwd(proj, w8, dcy, *, z0, gb0, gc0, cw, name, tm=512, tc=512):
    t = proj.shape[0]
    tm, tc = min(tm, t), math.gcd(tc, cw, z0, gb0, gc0)
    zb, bb, cb = z0 // tc, gb0 // tc, gc0 // tc
    rb = tm // SUBLANES
    nt = t // tm

    def body(z_ref, gb_ref, gc_ref, zp_ref, gcp_ref, d_ref, dn_ref, gbn_ref, w_ref, dz_ref, dgb_ref, dgc_ref, dw_ref):
        i = pl.program_id(1)
        z = z_ref[...]
        gc = gc_ref[...]
        gb = gb_ref[...]
        cz = gc * z
        s1, s2 = _conv_taps(cz, gcp_ref[...] * zp_ref[...], i)
        w = w_ref[...]
        w0, w1, w2 = w[0:1, :], w[1:2, :], w[2:3, :]
        yc = w0 * s2 + w1 * s1 + w2 * cz
        dcyv = d_ref[...]
        dgb_ref[...] = (dcyv * yc).astype(BF)
        dyc = dcyv * gb
        dycn = dn_ref[...] * gbn_ref[...] * (i < nt - 1).astype(F32)
        n1, n2 = dycn[0:1, :], dycn[1:2, :]
        row = lax.broadcasted_iota(jnp.int32, cz.shape, 0)
        a1 = jnp.where(row == tm - 1, n1, pltpu.roll(dyc, tm - 1, 0))
        a2 = jnp.where(row == tm - 1, n2, jnp.where(row == tm - 2, n1, pltpu.roll(dyc, tm - 2, 0)))
        dcz = w2 * dyc + w1 * a1 + w0 * a2
        dz_ref[...] = (dcz * gc).astype(BF)
        dgc_ref[...] = (dcz * z).astype(BF)
        dw0 = jnp.sum(dyc * s2, axis=0, keepdims=True)
        dw1 = jnp.sum(dyc * s1, axis=0, keepdims=True)
        dw2 = jnp.sum(dyc * cz, axis=0, keepdims=True)
        r8 = lax.broadcasted_iota(jnp.int32, (SUBLANES, tc), 0)
        upd = jnp.where(r8 == 0, dw0, jnp.where(r8 == 1, dw1, jnp.where(r8 == 2, dw2, 0.0)))

        @pl.when(i == 0)
        def _():
            dw_ref[...] = jnp.zeros_like(dw_ref)

        dw_ref[...] += upd

    def cur(b0):
        return pl.BlockSpec((tm, tc), lambda j, i: (i, b0 + j))

    def prev(b0):
        return pl.BlockSpec((SUBLANES, tc), lambda j, i: (jnp.maximum(i * rb - 1, 0), b0 + j))

    def nxt(b0):
        return pl.BlockSpec((SUBLANES, tc), lambda j, i: (jnp.minimum((i + 1) * rb, t // SUBLANES - 1), b0 + j))

    blk = pl.BlockSpec((tm, tc), lambda j, i: (i, j))
    w_spec = pl.BlockSpec((SUBLANES, tc), lambda j, i: (0, j))
    out = jax.ShapeDtypeStruct((t, cw), BF)
    return pl.pallas_call(
        body,
        name=name,
        grid=(cw // tc, nt),
        in_specs=[cur(zb), cur(bb), cur(cb), prev(zb), prev(cb), blk, nxt(0), nxt(bb), w_spec],
        out_specs=[blk, blk, blk, w_spec],
        out_shape=[out, out, out, jax.ShapeDtypeStruct((SUBLANES, cw), F32)],
        compiler_params=_cp("parallel", "arbitrary"),
    )(proj, proj, proj, proj, proj, dcy, dcy, proj, w8)


def _rot_half(x):
    lane = lax.broadcasted_iota(jnp.int32, x.shape, 1)
    first = (lane % HEAD_DIM) < (HEAD_DIM // 2)
    return jnp.where(first, pltpu.roll(x, LANES - HEAD_DIM // 2, 1), pltpu.roll(x, HEAD_DIM // 2, 1))


def _rope(x, c, s):
    parts = []
    for a in range(x.shape[1] // LANES):
        xa = x[:, a * LANES:(a + 1) * LANES]
        parts.append(xa * c + _rot_half(xa) * s)
    return parts[0] if len(parts) == 1 else jnp.concatenate(parts, axis=1)


def _rope_bwd(dy, c, s):
    parts = []
    for a in range(dy.shape[1] // LANES):
        da = dy[:, a * LANES:(a + 1) * LANES]
        parts.append(da * c + _rot_half(da * s))
    return parts[0] if len(parts) == 1 else jnp.concatenate(parts, axis=1)


def _band_mask(i):
    b = WINDOW
    r = lax.broadcasted_iota(jnp.int32, (b, 2 * b), 0)
    c = lax.broadcasted_iota(jnp.int32, (b, 2 * b), 1)
    no_prev = jnp.where(i > 0, 0, 2 * b)
    return ((c < b) & (c > r + no_prev)) | ((c >= b) & ((c - b) <= r))


def _chunk(x, a):
    return x[:, a * LANES:(a + 1) * LANES]


def _kv_aligned(kp, kc, h):
    band = jnp.concatenate([_chunk(kp, h // 2), _chunk(kc, h // 2)], axis=0).astype(F32)
    swapped = pltpu.roll(band, HEAD_DIM, 1)
    return (band, swapped) if h % 2 == 0 else (swapped, band)


def _swa_fwd(proj, cosf, sinf, sinks, *, nq, name):
    t = proj.shape[0]
    nkv = nq // Q_PER_KV
    aw, kw, b = nq * HEAD_DIM, nkv * HEAD_DIM, WINDOW
    nb = t // b
    kblk = aw // kw
    scale = HEAD_DIM ** -0.5

    def body(sink_ref, q_ref, kc_ref, kp_ref, vc_ref, vp_ref, cc_ref, cp_ref, sc_ref, sp_ref, o_ref, qr_ref, kr_ref):
        i = pl.program_id(0)
        cc, sc, cpv, spv = cc_ref[...], sc_ref[...], cp_ref[...], sp_ref[...]
        qr = _rope(q_ref[...], cc, sc)
        kc = _rope(kc_ref[...], cc, sc)
        kp = _rope(kp_ref[...], cpv, spv)
        qr_ref[...] = qr.astype(BF)
        kr_ref[...] = kc.astype(BF)
        vc, vp = vc_ref[...], vp_ref[...]
        valid = _band_mask(i)
        lo = lax.broadcasted_iota(jnp.int32, (b, LANES), 1) < HEAD_DIM
        for a in range(nq // 2):
            h = (2 * a) // Q_PER_KV
            ks = [x.astype(BF) for x in _kv_aligned(kp, kc, h)]
            vs = [x.astype(BF) for x in _kv_aligned(vp, vc, h)]
            qa = _chunk(qr, a)
            o_par = []
            for par in range(2):
                hq = 2 * a + par
                qm = jnp.where(lo if par == 0 else ~lo, qa, 0.0).astype(BF)
                s = lax.dot_general(qm, ks[par], _DIMS["nt"], preferred_element_type=F32) * scale
                s = jnp.where(valid, s, -jnp.inf)
                sink = sink_ref[hq]
                m = jnp.maximum(jnp.max(s, axis=-1, keepdims=True), sink)
                p = jnp.exp(s - m)
                p = p / (jnp.sum(p, axis=-1, keepdims=True) + jnp.exp(sink - m))
                o_par.append(jnp.dot(p.astype(BF), vs[par], preferred_element_type=F32))
            o_ref[:, a * LANES:(a + 1) * LANES] = jnp.where(lo, o_par[0], o_par[1]).astype(BF)

    def prev_i(i):
        return jnp.maximum(i - 1, 0)

    tab_c = pl.BlockSpec((b, LANES), lambda i: (i, 0))
    tab_p = pl.BlockSpec((b, LANES), lambda i: (prev_i(i), 0))
    return pl.pallas_call(
        body,
        name=name,
        grid=(nb,),
        in_specs=[
            pl.BlockSpec(memory_space=pltpu.SMEM),
            pl.BlockSpec((b, aw), lambda i: (i, 0)),
            pl.BlockSpec((b, kw), lambda i: (i, kblk)),
            pl.BlockSpec((b, kw), lambda i: (prev_i(i), kblk)),
            pl.BlockSpec((b, kw), lambda i: (i, kblk + 1)),
            pl.BlockSpec((b, kw), lambda i: (prev_i(i), kblk + 1)),
            tab_c,
            tab_p,
            tab_c,
            tab_p,
        ],
        out_specs=[
            pl.BlockSpec((b, aw), lambda i: (i, 0)),
            pl.BlockSpec((b, aw), lambda i: (i, 0)),
            pl.BlockSpec((b, kw), lambda i: (i, 0)),
        ],
        out_shape=[
            jax.ShapeDtypeStruct((t, aw), BF),
            jax.ShapeDtypeStruct((t, aw), BF),
            jax.ShapeDtypeStruct((t, kw), BF),
        ],
        compiler_params=_cp("parallel"),
    )(sinks, proj, proj, proj, proj, proj, cosf, cosf, sinf, sinf)


def _swa_bwd(qr, kr, proj, do, cosf, sinf, sinks, *, nq, name):
    t = proj.shape[0]
    nkv = nq // Q_PER_KV
    aw, kw, b = nq * HEAD_DIM, nkv * HEAD_DIM, WINDOW
    nb = t // b
    kblk = aw // kw
    scale = HEAD_DIM ** -0.5

    def body(sink_ref, q_ref, kc_ref, kp_ref, vc_ref, vp_ref, do_ref, cc_ref, cp_ref, sc_ref, sp_ref,
             dq_ref, dk_ref, dv_ref, ds_ref, ck_ref, cv_ref, sacc_ref):
        i = pl.program_id(0)

        @pl.when(i == 0)
        def _():
            ck_ref[...] = jnp.zeros_like(ck_ref)
            cv_ref[...] = jnp.zeros_like(cv_ref)
            sacc_ref[...] = jnp.zeros_like(sacc_ref)

        @pl.when(i < nb)
        def _():
            q = q_ref[...]
            kc, kp = kc_ref[...], kp_ref[...]
            vc, vp = vc_ref[...], vp_ref[...]
            dov = do_ref[...]
            valid = _band_mask(i)
            lane = lax.broadcasted_iota(jnp.int32, (b, LANES), 1)
            lo = lane < HEAD_DIM
            cc, sc = cc_ref[...], sc_ref[...]
            nch = kw // LANES
            dk_ch = [jnp.zeros((2 * b, LANES), F32) for _ in range(nch)]
            dv_ch = [jnp.zeros((2 * b, LANES), F32) for _ in range(nch)]
            sacc = jnp.zeros((b, LANES), F32)
            for a in range(nq // 2):
                h = (2 * a) // Q_PER_KV
                ks = [x.astype(BF) for x in _kv_aligned(kp, kc, h)]
                vs = [x.astype(BF) for x in _kv_aligned(vp, vc, h)]
                qa = _chunk(q, a).astype(F32)
                doa = _chunk(dov, a).astype(F32)
                dq_par = []
                for par in range(2):
                    hq = 2 * a + par
                    mine = lo if par == 0 else ~lo
                    qm = jnp.where(mine, qa, 0.0).astype(BF)
                    dom = jnp.where(mine, doa, 0.0).astype(BF)
                    s = lax.dot_general(qm, ks[par], _DIMS["nt"], preferred_element_type=F32) * scale
                    s = jnp.where(valid, s, -jnp.inf)
                    sink = sink_ref[hq]
                    m = jnp.maximum(jnp.max(s, axis=-1, keepdims=True), sink)
                    e = jnp.exp(s - m)
                    es = jnp.exp(sink - m)
                    zinv = 1.0 / (jnp.sum(e, axis=-1, keepdims=True) + es)
                    p = e * zinv
                    dp = lax.dot_general(dom, vs[par], _DIMS["nt"], preferred_element_type=F32)
                    delta = jnp.sum(p * dp, axis=-1, keepdims=True)
                    dsv = (p * (dp - delta) * scale).astype(BF)
                    sacc = sacc + jnp.where(lane == hq, -(es * zinv) * delta, 0.0)
                    dq_par.append(jnp.dot(dsv, ks[par], preferred_element_type=F32))
                    dkh = lax.dot_general(dsv, qm, _DIMS["tn"], preferred_element_type=F32)
                    dvh = lax.dot_general(p.astype(BF), dom, _DIMS["tn"], preferred_element_type=F32)
                    if par != h % 2:
                        dkh = pltpu.roll(dkh, HEAD_DIM, 1)
                        dvh = pltpu.roll(dvh, HEAD_DIM, 1)
                    dk_ch[h // 2] = dk_ch[h // 2] + dkh
                    dv_ch[h // 2] = dv_ch[h // 2] + dvh
                dqa = jnp.where(lo, dq_par[0], dq_par[1])
                dq_ref[:, a * LANES:(a + 1) * LANES] = _rope_bwd(dqa, cc, sc).astype(BF)
            dk = dk_ch[0] if nch == 1 else jnp.concatenate(dk_ch, axis=1)
            dv = dv_ch[0] if nch == 1 else jnp.concatenate(dv_ch, axis=1)
            dk_ref[...] = _rope_bwd(ck_ref[...] + dk[:b, :], cp_ref[...], sp_ref[...]).astype(BF)
            dv_ref[...] = (cv_ref[...] + dv[:b, :]).astype(BF)
            ck_ref[...] = dk[b:, :]
            cv_ref[...] = dv[b:, :]
            sacc_ref[...] += sacc

        @pl.when(i == nb)
        def _():
            dk_ref[...] = _rope_bwd(ck_ref[...], cp_ref[...], sp_ref[...]).astype(BF)
            dv_ref[...] = cv_ref[...].astype(BF)
            ds_ref[...] = jnp.broadcast_to(jnp.sum(sacc_ref[...], axis=0, keepdims=True), ds_ref.shape)

    def cur_i(i):
        return jnp.minimum(i, nb - 1)

    def prev_i(i):
        return jnp.clip(i - 1, 0, nb - 1)

    tab_c = pl.BlockSpec((b, LANES), lambda i: (cur_i(i), 0))
    tab_p = pl.BlockSpec((b, LANES), lambda i: (prev_i(i), 0))
    return pl.pallas_call(
        body,
        name=name,
        grid=(nb + 1,),
        in_specs=[
            pl.BlockSpec(memory_space=pltpu.SMEM),
            pl.BlockSpec((b, aw), lambda i: (cur_i(i), 0)),
            pl.BlockSpec((b, kw), lambda i: (cur_i(i), 0)),
            pl.BlockSpec((b, kw), lambda i: (prev_i(i), 0)),
            pl.BlockSpec((b, kw), lambda i: (cur_i(i), kblk + 1)),
            pl.BlockSpec((b, kw), lambda i: (prev_i(i), kblk + 1)),
            pl.BlockSpec((b, aw), lambda i: (cur_i(i), 0)),
            tab_c,
            tab_p,
            tab_c,
            tab_p,
        ],
        out_specs=[
            pl.BlockSpec((b, aw), lambda i: (cur_i(i), 0)),
            pl.BlockSpec((b, kw), lambda i: (prev_i(i), 0)),
            pl.BlockSpec((b, kw), lambda i: (prev_i(i), 0)),
            pl.BlockSpec((SUBLANES, LANES), lambda i: (0, 0)),
        ],
        out_shape=[
            jax.ShapeDtypeStruct((t, aw), BF),
            jax.ShapeDtypeStruct((t, kw), BF),
            jax.ShapeDtypeStruct((t, kw), BF),
            jax.ShapeDtypeStruct((SUBLANES, LANES), F32),
        ],
        scratch_shapes=[pltpu.VMEM((b, kw), F32), pltpu.VMEM((b, kw), F32), pltpu.VMEM((b, LANES), F32)],
        compiler_params=_cp("arbitrary"),
    )(sinks, qr, kr, kr, proj, proj, do, cosf, cosf, sinf, sinf)


def _xattn_fwd(xq, kv, *, name, tq=512):
    t, xw = xq.shape
    mtok = kv.shape[0]
    tq = min(tq, t)
    nh = xw // X_HEAD_DIM
    scale = X_HEAD_DIM ** -0.5

    def body(q_ref, kv_ref, o_ref):
        q = q_ref[...]
        kvv = kv_ref[...]
        outs = []
        for h in range(nh):
            sl = slice(h * X_HEAD_DIM, (h + 1) * X_HEAD_DIM)
            k = kvv[:, sl]
            v = kvv[:, xw + h * X_HEAD_DIM: xw + (h + 1) * X_HEAD_DIM]
            s = lax.dot_general(q[:, sl], k, _DIMS["nt"], preferred_element_type=F32) * scale
            e = jnp.exp(s - jnp.max(s, axis=-1, keepdims=True))
            p = e / jnp.sum(e, axis=-1, keepdims=True)
            outs.append(jnp.dot(p.astype(BF), v, preferred_element_type=F32))
        o_ref[...] = jnp.concatenate(outs, axis=1).astype(BF)

    return pl.pallas_call(
        body,
        name=name,
        grid=(t // tq,),
        in_specs=[pl.BlockSpec((tq, xw), lambda i: (i, 0)), pl.BlockSpec((mtok, 2 * xw), lambda i: (0, 0))],
        out_specs=pl.BlockSpec((tq, xw), lambda i: (i, 0)),
        out_shape=jax.ShapeDtypeStruct((t, xw), BF),
        compiler_params=_cp("parallel"),
    )(xq, kv)


def _xattn_bwd(xq, kv, do, *, name, tq=512):
    t, xw = xq.shape
    mtok = kv.shape[0]
    tq = min(tq, t)
    nh = xw // X_HEAD_DIM
    scale = X_HEAD_DIM ** -0.5

    def body(q_ref, kv_ref, do_ref, dq_ref, dkv_ref):
        i = pl.program_id(0)
        q = q_ref[...]
        kvv = kv_ref[...]
        dov = do_ref[...]
        dqs, dks, dvs = [], [], []
        for h in range(nh):
            sl = slice(h * X_HEAD_DIM, (h + 1) * X_HEAD_DIM)
            k = kvv[:, sl]
            v = kvv[:, xw + h * X_HEAD_DIM: xw + (h + 1) * X_HEAD_DIM]
            qh, doh = q[:, sl], dov[:, sl]
            s = lax.dot_general(qh, k, _DIMS["nt"], preferred_element_type=F32) * scale
            e = jnp.exp(s - jnp.max(s, axis=-1, keepdims=True))
            p = e / jnp.sum(e, axis=-1, keepdims=True)
            dp = lax.dot_general(doh, v, _DIMS["nt"], preferred_element_type=F32)
            delta = jnp.sum(p * dp, axis=-1, keepdims=True)
            dsv = (p * (dp - delta) * scale).astype(BF)
            dqs.append(jnp.dot(dsv, k, preferred_element_type=F32))
            dks.append(lax.dot_general(dsv, qh, _DIMS["tn"], preferred_element_type=F32))
            dvs.append(lax.dot_general(p.astype(BF), doh, _DIMS["tn"], preferred_element_type=F32))
        dq_ref[...] = jnp.concatenate(dqs, axis=1).astype(BF)

        @pl.when(i == 0)
        def _():
            dkv_ref[...] = jnp.zeros_like(dkv_ref)

        dkv_ref[...] += jnp.concatenate(dks + dvs, axis=1)

    row = pl.BlockSpec((tq, xw), lambda i: (i, 0))
    full = pl.BlockSpec((mtok, 2 * xw), lambda i: (0, 0))
    return pl.pallas_call(
        body,
        name=name,
        grid=(t // tq,),
        in_specs=[row, full, row],
        out_specs=[row, full],
        out_shape=[jax.ShapeDtypeStruct((t, xw), BF), jax.ShapeDtypeStruct((mtok, 2 * xw), F32)],
        compiler_params=_cp("arbitrary"),
    )(xq, kv, do)


def _adam_math(w, g, m, v):
    m = ADAM_B1 * m + (1.0 - ADAM_B1) * g
    v = ADAM_B2 * v + (1.0 - ADAM_B2) * (g * g)
    m_hat = m / (1.0 - ADAM_B1 ** ADAM_STEP)
    v_hat = v / (1.0 - ADAM_B2 ** ADAM_STEP)
    delta = -ADAM_LR * (m_hat / (jnp.sqrt(v_hat) + ADAM_EPS) + ADAM_WD * w)
    return delta, m, v


def _row_tile(r, c, n_arrays, budget=24 * 1024 * 1024):
    step = 2 * SUBLANES
    cap = max(step, budget // (2 * n_arrays * c * 4))
    if r <= cap:
        return r
    best = None
    for tr in range(step, cap + 1, step):
        if r % tr == 0:
            best = tr
    assert best is not None, (r, c)
    return best


def _adamw_sum(parts, w, m, v, *, name):
    _, r, c = parts.shape
    tr = _row_tile(r, c, 11)

    def body(p_ref, w_ref, m_ref, v_ref, g_ref, d_ref, nm_ref, nv_ref):
        g = p_ref[0].astype(F32)
        for s in range(1, N_DEV):
            g = g + p_ref[s].astype(F32)
        g_ref[...] = g
        d_ref[...], nm_ref[...], nv_ref[...] = _adam_math(w_ref[...], g, m_ref[...], v_ref[...])

    blk = pl.BlockSpec((tr, c), lambda i: (i, 0))
    out = jax.ShapeDtypeStruct((r, c), F32)
    return pl.pallas_call(
        body,
        name=name,
        grid=(r // tr,),
        in_specs=[pl.BlockSpec((N_DEV, tr, c), lambda i: (0, i, 0)), blk, blk, blk],
        out_specs=[blk, blk, blk, blk],
        out_shape=[out, out, out, out],
        compiler_params=_cp("parallel"),
    )(parts, w, m, v)


def _adamw_small(w, g, m, v, *, name):
    def body(w_ref, g_ref, m_ref, v_ref, d_ref, nm_ref, nv_ref):
        d_ref[...], nm_ref[...], nv_ref[...] = _adam_math(w_ref[...], g_ref[...], m_ref[...], v_ref[...])

    out = jax.ShapeDtypeStruct(w.shape, F32)
    return pl.pallas_call(body, name=name, out_shape=[out, out, out])(w, g, m, v)


def _mesh_pos():
    x, y, c = lax.axis_index("x"), lax.axis_index("y"), lax.axis_index("c")
    return x, y, c


def _peer(x, y, c, mask):
    px = 1 - x if mask & 4 else x
    py = 1 - y if mask & 2 else y
    pc = 1 - c if mask & 1 else c
    return (px, py, pc), 4 * px + 2 * py + pc


_HBM = pl.BlockSpec(memory_space=pltpu.HBM)
_SEM = pl.BlockSpec(memory_space=pltpu.SEMAPHORE)
_EFFECT = pltpu.SideEffectType.DATAFLOW_SIDE_EFFECTING


def _me():
    return 4 * lax.axis_index("x") + 2 * lax.axis_index("y") + lax.axis_index("c")


def _landing(own, me):
    land = lax.empty((N_DEV,) + own.shape, own.dtype)
    return lax.dynamic_update_slice(land, own[None], (me, 0, 0))


def _copy(src, land, send_sem, recv_sem, sem0, x, y, c, k, scatter, arriving):
    me = 4 * x + 2 * y + c
    peer, pidx = _peer(x, y, c, k + 1)
    return pltpu.make_async_remote_copy(
        src_ref=src.at[pidx] if scatter else src,
        dst_ref=land.at[pidx if arriving else me],
        send_sem=send_sem.at[sem0 + k], recv_sem=recv_sem.at[sem0 + k], device_id=peer, device_id_type=MESH)


def _exchange_start(groups, *, scatter, name):
    flat = [p for g in groups for p in g]
    n, ng = len(flat), len(groups)

    def body(*refs):
        srcs, lands = refs[:n], refs[n:2 * n]
        sems = refs[2 * n:2 * n + 2 * ng]
        token = refs[-1]
        x, y, c = _mesh_pos()
        w = 0
        for gi, g in enumerate(groups):
            for wi in range(len(g)):
                for k in range(N_DEV - 1):
                    _copy(srcs[w], lands[w], sems[2 * gi], sems[2 * gi + 1], wi * (N_DEV - 1),
                          x, y, c, k, scatter, False).start()
                w += 1
        token[...] = jnp.zeros_like(token)

    sem_shapes = []
    for g in groups:
        sem_shapes += [pltpu.SemaphoreType.DMA((len(g) * (N_DEV - 1),))] * 2
    args = [pltpu.with_memory_space_constraint(s, pltpu.HBM) for s, _ in flat]
    args += [pltpu.with_memory_space_constraint(l, pltpu.HBM) for _, l in flat]
    outs = pl.pallas_call(
        body,
        name=name,
        in_specs=[_HBM] * (2 * n),
        out_specs=[_SEM] * (2 * ng) + [_HBM] * (2 * n) + [pl.BlockSpec(memory_space=pltpu.VMEM)],
        out_shape=sem_shapes + [pltpu.HBM(a.shape, a.dtype) for a in args]
        + [jax.ShapeDtypeStruct((SUBLANES, LANES), F32)],
        input_output_aliases={i: 2 * ng + i for i in range(2 * n)},
        compiler_params=pltpu.CompilerParams(has_side_effects=_EFFECT),
    )(*args)
    sems, thru, token = outs[:2 * ng], outs[2 * ng:2 * ng + 2 * n], outs[-1]
    res, w = [], 0
    for gi, g in enumerate(groups):
        m = len(g)
        res.append((sems[2 * gi], sems[2 * gi + 1], list(thru[w:w + m]), list(thru[n + w:n + w + m])))
        w += m
    return res, token


def _exchange_wait(group, after, *, scatter, name):
    send_sems, recv_sems, srcs_in, lands_in = group
    n = len(srcs_in)

    def body(*refs):
        srcs, lands = refs[:n], refs[n:2 * n]
        send_sem, recv_sem = refs[2 * n], refs[2 * n + 1]
        x, y, c = _mesh_pos()
        for w in range(n):
            for k in range(N_DEV - 1):
                cp = _copy(srcs[w], lands[w], send_sem, recv_sem, w * (N_DEV - 1), x, y, c, k, scatter, True)
                cp.wait_send()
                cp.wait_recv()

    outs = pl.pallas_call(
        body,
        name=name,
        in_specs=[_HBM] * (2 * n) + [_SEM, _SEM, pl.BlockSpec(memory_space=pl.ANY)],
        out_specs=[_HBM] * (2 * n),
        out_shape=[pltpu.HBM(a.shape, a.dtype) for a in srcs_in + lands_in],
        input_output_aliases={i: i for i in range(2 * n)},
        compiler_params=pltpu.CompilerParams(has_side_effects=_EFFECT),
    )(*srcs_in, *lands_in, send_sems, recv_sems, after)
    return list(outs[n:])


def _all_reduce_small(parts, rows, width, *, name):
    n = len(parts)

    def body(*refs):
        ins = refs[:n]
        o_ref, pack_ref, buf_ref, send_sems, recv_sems = refs[n:]
        x, y, c_ = _mesh_pos()
        me = 4 * x + 2 * y + c_
        pack_ref[...] = jnp.zeros_like(pack_ref)
        for ref, (arr, r0, nr) in zip(ins, parts):
            pack_ref[r0:r0 + nr, 0:arr.shape[1]] = ref[0:nr, :]
        sends, recvs = [], []
        for k in range(N_DEV - 1):
            peer, pidx = _peer(x, y, c_, k + 1)
            cp = pltpu.make_async_remote_copy(
                src_ref=pack_ref, dst_ref=buf_ref.at[me], send_sem=send_sems.at[k], recv_sem=recv_sems.at[k],
                device_id=peer, device_id_type=MESH)
            cp.start()
            sends.append(cp)
            recvs.append(pltpu.make_async_remote_copy(
                src_ref=pack_ref, dst_ref=buf_ref.at[pidx], send_sem=send_sems.at[k], recv_sem=recv_sems.at[k],
                device_id=peer, device_id_type=MESH))
        buf_ref[me] = pack_ref[...]
        for rc in recvs:
            rc.wait_recv()
        for cp in sends:
            cp.wait_send()
        acc = buf_ref[0]
        for s in range(1, N_DEV):
            acc = acc + buf_ref[s]
        o_ref[...] = acc

    vmem = pl.BlockSpec(memory_space=pltpu.VMEM)
    return pl.pallas_call(
        body,
        name=name,
        in_specs=[vmem] * n,
        out_specs=vmem,
        out_shape=jax.ShapeDtypeStruct((rows, width), F32),
        scratch_shapes=[
            pltpu.VMEM((rows, width), F32),
            pltpu.VMEM((N_DEV, rows, width), F32),
            pltpu.SemaphoreType.DMA((N_DEV - 1,)),
            pltpu.SemaphoreType.DMA((N_DEV - 1,)),
        ],
    )(*[p[0] for p in parts])


def _rope_tables(t):
    half = HEAD_DIM // 2
    inv_freq = ROPE_THETA ** (-jnp.arange(half, dtype=F32) / half)
    ang = jnp.arange(t, dtype=jnp.int32).astype(F32)[:, None] * inv_freq[None, :]
    cos, sin = jnp.cos(ang), jnp.sin(ang)
    cosf = jnp.concatenate([cos, cos, cos, cos], axis=1)
    sinf = jnp.concatenate([-sin, sin, -sin, sin], axis=1)
    return cosf, sinf


def _local_step(x, mem, target, gains, sinks, aw, cw, get_w, put_g, dep0=None):
    t, d = x.shape
    nq = aw // HEAD_DIM
    kw = aw // Q_PER_KV
    z0 = aw + 2 * kw
    gb0, gc0 = z0 + cw, z0 + 2 * cw
    ga0 = z0 + 3 * cw
    gcm0 = ga0 + d
    cosf, sinf = _rope_tables(t)

    u1 = _rms_fwd(x, gains["g_mix"], name="rms_mix", dep=dep0)
    mem_n = _rms_fwd(mem, gains["g_mem"], name="rms_mem", dep=dep0)
    w_in = get_w("w_in", mem_n)
    proj = _mm(u1, w_in, mode="nn", tm=1024, tn=512, tk=2048, out_dtype=F32, name="mm_in")
    o_attn, q_rot, k_rot = _swa_fwd(proj, cosf, sinf, sinks, nq=nq, name="swa_fwd")
    conv_w8 = get_w("conv_w8", o_attn)
    w_attn_proj, w_conv_proj, w_mix_out = (get_w(n, o_attn) for n in ("w_attn_proj", "w_conv_proj", "w_mix_out"))
    w_xq, w_xkv, w_xo = (get_w(n, o_attn) for n in ("w_xq", "w_xkv", "w_xo"))
    y_attn = _mm(o_attn, w_attn_proj, mode="nn", tm=1024, tn=1024, tk=1024, out_dtype=F32, name="mm_attn_proj")
    cy = _conv_fwd(proj, conv_w8, z0=z0, gb0=gb0, gc0=gc0, cw=cw, name="conv_fwd")
    y_conv = _mm(cy, w_conv_proj, mode="nn", tm=1024, tn=1024, tk=1024, out_dtype=F32, name="mm_conv_proj")
    merged = _gate_fwd(proj, y_attn, y_conv, ga0=ga0, gc0=gcm0, name="gate_fwd")
    h1 = _mm(merged, w_mix_out, mode="nn", tm=1024, tn=1024, tk=2048, out_dtype=F32, name="mm_mix_out", residual=x)
    u2 = _rms_fwd(h1, gains["g_xattn"], name="rms_xattn")
    xq = _mm(u2, w_xq, mode="nn", tm=1024, tn=512, tk=2048, out_dtype=BF, name="mm_xq")
    kv = _mm(mem_n, w_xkv, mode="nn", tm=256, tn=1024, tk=2048, out_dtype=BF, name="mm_xkv")
    o_x = _xattn_fwd(xq, kv, name="xattn_fwd")
    h2 = _mm(o_x, w_xo, mode="nn", tm=1024, tn=1024, tk=512, out_dtype=F32, name="mm_xo", residual=h1)
    u3 = _rms_fwd(h2, gains["g_ffn"], name="rms_ffn")
    w_ffn_in = get_w("w_ffn_in", u3)
    hid = _mm(u3, w_ffn_in, mode="nn", tm=1024, tn=512, tk=2048, out_dtype=F32, name="mm_ffn_in")
    act = _swiglu_fwd(hid, name="swiglu_fwd")
    w_ffn_out = get_w("w_ffn_out", act)
    h3 = _mm(act, w_ffn_out, mode="nn", tm=512, tn=1024, tk=8192, out_dtype=F32, name="mm_ffn_out", residual=h2)

    tt = 8192
    dh3, dh3b, loss_tile, dg_final = _loss_head(h3, target, gains["g_final"], name="loss_head")
    tok = put_g("w_ffn_out", _mm(act, dh3b, mode="tn", tm=512, tn=1024, tk=tt, out_dtype=BF, name="mm_dw_ffn_out"))
    dact = _mm(dh3b, w_ffn_out, mode="nt", tm=1024, tn=512, tk=2048, out_dtype=F32, name="mm_dact", dep=tok)
    dhid = _swiglu_bwd(dact, hid, name="swiglu_bwd")
    tok = put_g("w_ffn_in", _mm(u3, dhid, mode="tn", tm=1024, tn=512, tk=tt, out_dtype=BF, name="mm_dw_ffn_in"))
    du3 = _mm(dhid, w_ffn_in, mode="nt", tm=1024, tn=1024, tk=2816, out_dtype=F32, name="mm_du3", dep=tok)
    dh2, dh2b, dg_ffn = _rms_bwd(du3, h2, gains["g_ffn"], dh3, name="rms_ffn_bwd")
    put_g("w_xo", _mm(o_x, dh2b, mode="tn", tm=512, tn=1024, tk=tt, out_dtype=BF, name="mm_dw_xo"))
    do_x = _mm(dh2b, w_xo, mode="nt", tm=1024, tn=512, tk=2048, out_dtype=BF, name="mm_do_x")
    dxq, dkv = _xattn_bwd(xq, kv, do_x, name="xattn_bwd")
    put_g("w_xkv", _mm(mem_n, dkv, mode="tn", tm=1024, tn=1024, tk=256, out_dtype=BF, name="mm_dw_xkv"))
    tok = put_g("w_xq", _mm(u2, dxq, mode="tn", tm=1024, tn=512, tk=tt, out_dtype=BF, name="mm_dw_xq"))
    du2 = _mm(dxq, w_xq, mode="nt", tm=1024, tn=1024, tk=512, out_dtype=F32, name="mm_du2", dep=tok)
    dmem_n = _mm(dkv, w_xkv, mode="nt", tm=256, tn=1024, tk=1024, out_dtype=F32, name="mm_dmem")
    _, _, dg_mem = _rms_bwd(dmem_n, mem, gains["g_mem"], None, name="rms_mem_bwd")
    dh1, dh1b, dg_xattn = _rms_bwd(du2, h1, gains["g_xattn"], dh2, name="rms_xattn_bwd")
    put_g("w_mix_out", _mm(merged, dh1b, mode="tn", tm=1024, tn=1024, tk=tt, out_dtype=BF, name="mm_dw_mix_out"))
    dmerged = _mm(dh1b, w_mix_out, mode="nt", tm=1024, tn=1024, tk=2048, out_dtype=F32, name="mm_dmerged")
    dya, dyc, dga, dgc = _gate_bwd(dmerged, proj, y_attn, y_conv, ga0=ga0, gc0=gcm0, name="gate_bwd")
    put_g("w_attn_proj", _mm(o_attn, dya, mode="tn", tm=1024, tn=1024, tk=tt, out_dtype=BF, name="mm_dw_attn_proj"))
    do_attn = _mm(dya, w_attn_proj, mode="nt", tm=1024, tn=1024, tk=2048, out_dtype=BF, name="mm_do_attn")
    tok = put_g("w_conv_proj", _mm(cy, dyc, mode="tn", tm=1024, tn=1024, tk=tt, out_dtype=BF, name="mm_dw_conv_proj"))
    dcy = _mm(dyc, w_conv_proj, mode="nt", tm=1024, tn=1024, tk=2048, out_dtype=F32, name="mm_dcy", dep=tok)
    dz, dgb, dgcv, dconv_w8 = _conv_bwd(proj, conv_w8, dcy, z0=z0, gb0=gb0, gc0=gc0, cw=cw, name="conv_bwd")
    dq, dk, dv, dsink_tile = _swa_bwd(q_rot, k_rot, proj, do_attn, cosf, sinf, sinks, nq=nq, name="swa_bwd")
    dproj = jnp.concatenate([dq, dk, dv, dz, dgb, dgcv, dga, dgc], axis=1)
    tok = put_g("w_in", _mm(u1, dproj, mode="tn", tm=1024, tn=512, tk=tt, out_dtype=BF, name="mm_dw_in"))
    du1 = _mm(dproj, w_in, mode="nt", tm=512, tn=1024, tk=4352, out_dtype=F32, name="mm_du1", dep=tok)
    grad_x, _, dg_mix = _rms_bwd(du1, x, gains["g_mix"], dh1, name="rms_mix_bwd")

    small = {
        "g_mix": dg_mix, "g_xattn": dg_xattn, "g_mem": dg_mem, "g_ffn": dg_ffn, "g_final": dg_final,
        "attn_sinks": dsink_tile, "conv_w8": dconv_w8, "loss": loss_tile,
    }
    return grad_x, small


_COL_SHARDED = ("w_in", "w_attn_proj", "w_conv_proj", "w_xo", "w_ffn_in")
_ROW_SHARDED = ("w_mix_out", "w_xq", "w_xkv", "w_ffn_out")
_BIG = _COL_SHARDED + _ROW_SHARDED
_GAINS = ("g_mix", "g_xattn", "g_mem", "g_ffn", "g_final")
_GATHER_GROUPS = (("w_in",), ("conv_w8", "w_attn_proj", "w_conv_proj", "w_mix_out", "w_xq", "w_xkv", "w_xo"),
                  ("w_ffn_in",), ("w_ffn_out",))
_SCATTER_GROUPS = (("w_ffn_out",), ("w_ffn_in",), ("w_xo", "w_xq", "w_xkv"),
                   ("w_mix_out", "w_attn_proj", "w_conv_proj"), ("w_in",))
_WEIGHTS = ("g_mix", "w_in", "conv_w", "attn_sinks", "w_attn_proj", "w_conv_proj", "w_mix_out", "g_xattn", "g_mem",
            "w_xq", "w_xkv", "w_xo", "g_ffn", "w_ffn_in", "w_ffn_out", "g_final")


def _unstack(g, col_sharded):
    n, r, c = g.shape
    if col_sharded:
        return jnp.transpose(g, (1, 0, 2)).reshape(r, n * c)
    return g.reshape(n * r, c)


def _stack(w, col_sharded):
    r, c = w.shape
    if col_sharded:
        return jnp.transpose(w.reshape(r, N_DEV, c // N_DEV), (1, 0, 2))
    return w.reshape(N_DEV, r // N_DEV, c)


def kernel(x, mem, g_mix, w_in, conv_w, attn_sinks, w_attn_proj, w_conv_proj, w_mix_out, g_xattn, g_mem, w_xq, w_xkv, w_xo, g_ffn, w_ffn_in, w_ffn_out, g_final, loss_target, m_g_mix, m_w_in, m_conv_w, m_attn_sinks, m_w_attn_proj, m_w_conv_proj, m_w_mix_out, m_g_xattn, m_g_mem, m_w_xq, m_w_xkv, m_w_xo, m_g_ffn, m_w_ffn_in, m_w_ffn_out, m_g_final, v_g_mix, v_w_in, v_conv_w, v_attn_sinks, v_w_attn_proj, v_w_conv_proj, v_w_mix_out, v_g_xattn, v_g_mem, v_w_xq, v_w_xkv, v_w_xo, v_g_ffn, v_w_ffn_in, v_w_ffn_out, v_g_final):
    w_ = dict(g_mix=g_mix, w_in=w_in, conv_w=conv_w, attn_sinks=attn_sinks, w_attn_proj=w_attn_proj,
              w_conv_proj=w_conv_proj, w_mix_out=w_mix_out, g_xattn=g_xattn, g_mem=g_mem, w_xq=w_xq, w_xkv=w_xkv,
              w_xo=w_xo, g_ffn=g_ffn, w_ffn_in=w_ffn_in, w_ffn_out=w_ffn_out, g_final=g_final)
    m_ = dict(g_mix=m_g_mix, w_in=m_w_in, conv_w=m_conv_w, attn_sinks=m_attn_sinks, w_attn_proj=m_w_attn_proj,
              w_conv_proj=m_w_conv_proj, w_mix_out=m_w_mix_out, g_xattn=m_g_xattn, g_mem=m_g_mem, w_xq=m_w_xq,
              w_xkv=m_w_xkv, w_xo=m_w_xo, g_ffn=m_g_ffn, w_ffn_in=m_w_ffn_in, w_ffn_out=m_w_ffn_out,
              g_final=m_g_final)
    v_ = dict(g_mix=v_g_mix, w_in=v_w_in, conv_w=v_conv_w, attn_sinks=v_attn_sinks, w_attn_proj=v_w_attn_proj,
              w_conv_proj=v_w_conv_proj, w_mix_out=v_w_mix_out, g_xattn=v_g_xattn, g_mem=v_g_mem, w_xq=v_w_xq,
              w_xkv=v_w_xkv, w_xo=v_w_xo, g_ffn=v_g_ffn, w_ffn_in=v_w_ffn_in, w_ffn_out=v_w_ffn_out,
              g_final=v_g_final)
    t, d = x.shape[1], x.shape[2]
    nq = attn_sinks.shape[-1]
    cw_shard = conv_w.shape[-1]
    cw = cw_shard * N_DEV

    def two_d(a):
        return a.reshape(a.shape[-2], a.shape[-1]) if a.ndim == 3 else a.reshape(1, a.shape[-1])

    me = _me()
    col = set(_COL_SHARDED) | {"conv_w8"}

    shards = {n: two_d(w_[n]).astype(BF) for n in _BIG}
    shards["conv_w8"] = jnp.zeros((SUBLANES, cw_shard), F32).at[:3].set(two_d(conv_w))
    gathers, token = _exchange_start(
        [[(shards[n], _landing(shards[n], me)) for n in g] for g in _GATHER_GROUPS], scatter=False,
        name="gather_start")
    full = {}

    def get_w(name, after):
        if name not in full:
            gi = [name in g for g in _GATHER_GROUPS].index(True)
            lands = _exchange_wait(gathers[gi], after, scatter=False, name="gather_wait_%d" % gi)
            for n, land in zip(_GATHER_GROUPS[gi], lands):
                full[n] = _unstack(land, n in col)
        return full[name]

    pending, scatters = {}, []

    def put_g(name, dw):
        pending[name] = _stack(dw, name in col)
        gi = [name in g for g in _SCATTER_GROUPS].index(True)
        group = _SCATTER_GROUPS[gi]
        if not all(n in pending for n in group):
            return None
        pairs = [(pending[n], _landing(lax.dynamic_index_in_dim(pending[n], me, 0, keepdims=False), me))
                 for n in group]
        started, tok = _exchange_start([pairs], scatter=True, name="scatter_start_%d" % gi)
        scatters.append((gi, started[0]))
        return tok

    gains = {n: two_d(w_[n]) for n in _GAINS}
    grad_x, small = _local_step(
        x[0], mem[0], loss_target[0], gains, attn_sinks.reshape(nq), w_attn_proj.shape[-2], cw, get_w, put_g,
        dep0=token)

    grads, deltas, new_m, new_v = {}, {}, {}, {}
    after = grad_x
    for gi, started in scatters:
        parts = _exchange_wait(started, after, scatter=True, name="scatter_wait_%d" % gi)
        for n, p in zip(_SCATTER_GROUPS[gi], parts):
            shape = w_[n].shape
            g, dl, nm, nv = _adamw_sum(p, two_d(w_[n]), two_d(m_[n]), two_d(v_[n]), name="adamw_" + n)
            grads[n], deltas[n], new_m[n], new_v[n] = (a.reshape(shape) for a in (g, dl, nm, nv))
            after = g

    parts = [(small[n], i, 1) for i, n in enumerate(_GAINS)]
    parts += [(small["attn_sinks"], 5, 1), (small["loss"], 6, 1), (small["conv_w8"], 8, 3)]
    red = _all_reduce_small(parts, 2 * SUBLANES, max(d, cw), name="reduce_small")
    loss = red[6, 0]
    small_g = {n: red[i:i + 1, :d] for i, n in enumerate(_GAINS)}
    small_g["attn_sinks"] = red[5:6, :nq]
    small_g["conv_w"] = lax.dynamic_slice(red, (8, me * cw_shard), (3, cw_shard))
    for n in _GAINS + ("attn_sinks", "conv_w"):
        shape = w_[n].shape
        g = small_g[n]
        dl, nm, nv = _adamw_small(two_d(w_[n]), g, two_d(m_[n]), two_d(v_[n]), name="adamw_" + n)
        grads[n], deltas[n], new_m[n], new_v[n] = (a.reshape(shape) for a in (g, dl, nm, nv))

    return (loss, grad_x[None], *[grads[n] for n in _WEIGHTS], *[deltas[n] for n in _WEIGHTS],
            *[new_m[n] for n in _WEIGHTS], *[new_v[n] for n in _WEIGHTS])
```

```python
import functools
import math

import jax
import jax.numpy as jnp
from jax import lax
from jax.experimental import pallas as pl
from jax.experimental.pallas import tpu as pltpu

HEAD_DIM = 64
Q_PER_KV = 4
WINDOW = 128
X_HEAD_DIM = 128
ROPE_THETA = 10000.0
EPS = 1e-6
ADAM_LR = 0.001
ADAM_B1 = 0.9
ADAM_B2 = 0.999
ADAM_EPS = 1e-08
ADAM_WD = 0.01
ADAM_STEP = 10

N_DEV = 8
LANES = 128
SUBLANES = 8
VMEM_LIMIT_BYTES = 56 * 1024 * 1024
BF = jnp.bfloat16
F32 = jnp.float32
MESH = pl.DeviceIdType.MESH


def _cp(*sem):
    return pltpu.CompilerParams(dimension_semantics=sem, vmem_limit_bytes=VMEM_LIMIT_BYTES)


def _sigmoid(x):
    return 1.0 / (1.0 + jnp.exp(-x))


_DIMS = {
    "nn": (((1,), (0,)), ((), ())),
    "nt": (((1,), (1,)), ((), ())),
    "tn": (((0,), (0,)), ((), ())),
}


def _fit(dim, tile):
    if dim <= tile:
        return dim
    for t in range(tile // LANES * LANES, 0, -LANES):
        if dim % t == 0:
            return t
    return dim


def _mm(a, b, *, mode, tm, tn, tk, out_dtype, name, residual=None, dep=None, a_planes=1, b_planes=1,
        stacked=False):
    if a_planes > 1:
        assert mode == "nt"
        (_, m, kp), (n, k) = a.shape, b.shape
        assert kp * a_planes == k
    elif b_planes > 1:
        assert mode == "tn"
        (k, m), (_, k2, np_) = a.shape, b.shape
        n = np_ * b_planes
        assert k == k2
    elif mode == "nn":
        (m, k), (k2, n) = a.shape, b.shape
        assert k == k2, (name, a.shape, b.shape)
    elif mode == "nt":
        (m, k), (n, k2) = a.shape, b.shape
        assert k == k2, (name, a.shape, b.shape)
    else:
        (k, m), (k2, n) = a.shape, b.shape
        assert k == k2, (name, a.shape, b.shape)
    tm, tn, tk = _fit(m, tm), _fit(n // b_planes, tn), _fit(k // a_planes, tk)
    assert m % tm == 0 and (n // b_planes) % tn == 0 and (k // a_planes) % tk == 0, (name, m, n, k, tm, tn, tk)
    nk = k // tk
    nkp, njp = nk // a_planes, n // tn // b_planes
    if a_planes > 1:
        a_spec = pl.BlockSpec((None, tm, tk), lambda i, j, kk: (kk // nkp, i, kk % nkp))
    elif mode == "tn":
        a_spec = pl.BlockSpec((tk, tm), lambda i, j, kk: (kk, i))
    else:
        a_spec = pl.BlockSpec((tm, tk), lambda i, j, kk: (i, kk))
    if b_planes > 1:
        b_spec = pl.BlockSpec((None, tk, tn), lambda i, j, kk: (j // njp, kk, j % njp))
    elif mode == "nt":
        b_spec = pl.BlockSpec((tn, tk), lambda i, j, kk: (j, kk))
    else:
        b_spec = pl.BlockSpec((tk, tn), lambda i, j, kk: (kk, j))
    if stacked:
        assert residual is None
        o_spec = pl.BlockSpec((None, tm, tn), lambda i, j, kk: (j, i, 0))
        out_shape = jax.ShapeDtypeStruct((n // tn, m, tn), out_dtype)
    else:
        o_spec = pl.BlockSpec((tm, tn), lambda i, j, kk: (i, j))
        out_shape = jax.ShapeDtypeStruct((m, n), out_dtype)
    dims = _DIMS[mode]
    has_res = residual is not None
    n_in = 2 + has_res + (dep is not None)

    def body(*refs):
        a_ref, b_ref, r_ref, o_ref = refs[0], refs[1], refs[2], refs[n_in]
        part = lax.dot_general(a_ref[...].astype(BF), b_ref[...].astype(BF), dims, preferred_element_type=F32)

        def finish(acc):
            if has_res:
                acc = r_ref[...] + acc
            o_ref[...] = acc.astype(out_dtype)

        if nk == 1:
            finish(part)
        else:
            acc_ref = refs[-1]
            kk = pl.program_id(2)

            @pl.when(kk == 0)
            def _():
                acc_ref[...] = part

            @pl.when(kk > 0)
            def _():
                acc_ref[...] += part

            @pl.when(kk == nk - 1)
            def _():
                finish(acc_ref[...])

    in_specs = [a_spec, b_spec] + ([o_spec] if has_res else [])
    args = (a, b) + ((residual,) if has_res else ())
    if dep is not None:
        in_specs.append(pl.BlockSpec(memory_space=pl.ANY))
        args += (dep,)
    return pl.pallas_call(
        body,
        name=name,
        grid=(m // tm, n // tn, nk),
        in_specs=in_specs,
        out_specs=o_spec,
        out_shape=out_shape,
        scratch_shapes=[pltpu.VMEM((tm, tn), F32)] if nk > 1 else [],
        compiler_params=_cp("parallel", "parallel", "arbitrary"),
    )(*args)


def _rms_fwd(h, g, *, name, tm=512, dep=None):
    t, d = h.shape
    tm = min(tm, t)

    def body(*refs):
        h_ref, g_ref, u_ref = refs[0], refs[1], refs[-1]
        hv = h_ref[...]
        r = lax.rsqrt(jnp.mean(hv * hv, axis=-1, keepdims=True) + EPS)
        u_ref[...] = ((hv * r) * g_ref[...]).astype(BF)

    in_specs = [pl.BlockSpec((tm, d), lambda i: (i, 0)), pl.BlockSpec((1, d), lambda i: (0, 0))]
    args = (h, g)
    if dep is not None:
        in_specs.append(pl.BlockSpec(memory_space=pl.ANY))
        args += (dep,)
    return pl.pallas_call(
        body,
        name=name,
        grid=(t // tm,),
        in_specs=in_specs,
        out_specs=pl.BlockSpec((tm, d), lambda i: (i, 0)),
        out_shape=jax.ShapeDtypeStruct((t, d), BF),
        compiler_params=_cp("parallel"),
    )(*args)


def _rms_bwd(du, h, g, dres, *, name, tm=256):
    t, d = h.shape
    tm = min(tm, t)
    want_dh = dres is not None

    def body(*refs):
        if want_dh:
            du_ref, h_ref, g_ref, dres_ref, dh_ref, dhb_ref, dg_ref = refs
        else:
            du_ref, h_ref, g_ref, dg_ref = refs
        i = pl.program_id(0)
        hv = h_ref[...]
        duv = du_ref[...]
        r = lax.rsqrt(jnp.mean(hv * hv, axis=-1, keepdims=True) + EPS)
        nv = hv * r
        if want_dh:
            gy = duv * g_ref[...]
            dh = dres_ref[...] + r * (gy - nv * jnp.mean(nv * gy, axis=-1, keepdims=True))
            dh_ref[...] = dh
            dhb_ref[...] = dh.astype(BF)

        @pl.when(i == 0)
        def _():
            dg_ref[...] = jnp.zeros_like(dg_ref)

        dg_ref[...] += jnp.sum(duv * nv, axis=0, keepdims=True)

    row = pl.BlockSpec((tm, d), lambda i: (i, 0))
    vec = pl.BlockSpec((1, d), lambda i: (0, 0))
    if want_dh:
        in_specs, args = [row, row, vec, row], (du, h, g, dres)
        out_specs = [row, row, vec]
        out_shape = [jax.ShapeDtypeStruct((t, d), F32), jax.ShapeDtypeStruct((t, d), BF),
                     jax.ShapeDtypeStruct((1, d), F32)]
    else:
        in_specs, args = [row, row, vec], (du, h, g)
        out_specs = [vec]
        out_shape = [jax.ShapeDtypeStruct((1, d), F32)]
    outs = pl.pallas_call(
        body,
        name=name,
        grid=(t // tm,),
        in_specs=in_specs,
        out_specs=out_specs,
        out_shape=out_shape,
        compiler_params=_cp("arbitrary"),
    )(*args)
    return (outs[0], outs[1], outs[2]) if want_dh else (None, None, outs[0])


def _loss_head(h, target, g, *, name, tm=256):
    t, d = h.shape
    tm = min(tm, t)

    def body(h_ref, t_ref, g_ref, dh_ref, dhb_ref, loss_ref, dg_ref):
        i = pl.program_id(0)
        hv = h_ref[...]
        gv = g_ref[...]
        r = lax.rsqrt(jnp.mean(hv * hv, axis=-1, keepdims=True) + EPS)
        nv = hv * r
        e = nv * gv - t_ref[...]
        per_tok = jnp.mean(e * e, axis=-1, keepdims=True)
        lp = 0.5 * jnp.sum(per_tok, axis=0, keepdims=True)
        dy = e * (1.0 / d)
        gy = dy * gv
        dh = r * (gy - nv * jnp.mean(nv * gy, axis=-1, keepdims=True))
        dh_ref[...] = dh
        dhb_ref[...] = dh.astype(BF)

        @pl.when(i == 0)
        def _():
            loss_ref[...] = jnp.zeros_like(loss_ref)
            dg_ref[...] = jnp.zeros_like(dg_ref)

        loss_ref[...] += jnp.broadcast_to(lp, loss_ref.shape)
        dg_ref[...] += jnp.sum(dy * nv, axis=0, keepdims=True)

    row = pl.BlockSpec((tm, d), lambda i: (i, 0))
    vec = pl.BlockSpec((1, d), lambda i: (0, 0))
    return pl.pallas_call(
        body,
        name=name,
        grid=(t // tm,),
        in_specs=[row, row, vec],
        out_specs=[row, row, pl.BlockSpec((SUBLANES, LANES), lambda i: (0, 0)), vec],
        out_shape=[
            jax.ShapeDtypeStruct((t, d), F32),
            jax.ShapeDtypeStruct((t, d), BF),
            jax.ShapeDtypeStruct((SUBLANES, LANES), F32),
            jax.ShapeDtypeStruct((1, d), F32),
        ],
        compiler_params=_cp("arbitrary"),
    )(h, target, g)


def _ffn_in_fwd(u, w, *, name, tm=1024, tn=512):
    t, d = u.shape
    f = w.shape[1] // 2
    tm, tn = _fit(t, tm), _fit(f, tn)
    nf = f // tn

    def body(u_ref, wa_ref, wb_ref, hid_ref, act_ref):
        uv = u_ref[...]
        a = jnp.dot(uv, wa_ref[...], preferred_element_type=F32)
        b = jnp.dot(uv, wb_ref[...], preferred_element_type=F32)
        hid_ref[0] = a.astype(BF)
        hid_ref[1] = b.astype(BF)
        act_ref[...] = ((a * _sigmoid(a)) * b).astype(BF)

    return pl.pallas_call(
        body,
        name=name,
        grid=(t // tm, nf),
        in_specs=[
            pl.BlockSpec((tm, d), lambda i, j: (i, 0)),
            pl.BlockSpec((d, tn), lambda i, j: (0, j)),
            pl.BlockSpec((d, tn), lambda i, j: (0, nf + j)),
        ],
        out_specs=[pl.BlockSpec((2, tm, tn), lambda i, j: (0, i, j)), pl.BlockSpec((tm, tn), lambda i, j: (i, j))],
        out_shape=[jax.ShapeDtypeStruct((2, t, f), BF), jax.ShapeDtypeStruct((t, f), BF)],
        compiler_params=_cp("parallel", "parallel"),
    )(u, w, w)


def _ffn_out_bwd(dh, w_out, hid2, *, name, tm=1024, tn=512, dep=None):
    t, d = dh.shape
    f = w_out.shape[0]
    tm, tn = _fit(t, tm), _fit(f, tn)

    def body(*refs):
        dh_ref, w_ref, hid_ref, o_ref = refs[0], refs[1], refs[2], refs[-1]
        dact = lax.dot_general(dh_ref[...], w_ref[...], _DIMS["nt"], preferred_element_type=F32)
        a = hid_ref[0].astype(F32)
        b = hid_ref[1].astype(F32)
        sg = _sigmoid(a)
        o_ref[0] = (dact * b * (sg * (1.0 + a * (1.0 - sg)))).astype(BF)
        o_ref[1] = (dact * (a * sg)).astype(BF)

    pair = pl.BlockSpec((2, tm, tn), lambda i, j: (0, i, j))
    in_specs = [pl.BlockSpec((tm, d), lambda i, j: (i, 0)), pl.BlockSpec((tn, d), lambda i, j: (j, 0)), pair]
    args = (dh, w_out, hid2)
    if dep is not None:
        in_specs.append(pl.BlockSpec(memory_space=pl.ANY))
        args += (dep,)
    return pl.pallas_call(
        body,
        name=name,
        grid=(t // tm, f // tn),
        in_specs=in_specs,
        out_specs=pair,
        out_shape=jax.ShapeDtypeStruct((2, t, f), BF),
        compiler_params=_cp("parallel", "parallel"),
    )(*args)


def _gate_fwd(proj, ya, yc, *, ga0, gc0, name, tm=512, tc=512):
    t, d = ya.shape
    tm, tc = min(tm, t), math.gcd(tc, d, ga0, gc0)
    a0, c0 = ga0 // tc, gc0 // tc

    def body(ga_ref, gc_ref, ya_ref, yc_ref, o_ref):
        o_ref[...] = (_sigmoid(ga_ref[...]) * ya_ref[...] + _sigmoid(gc_ref[...]) * yc_ref[...]).astype(BF)

    blk = pl.BlockSpec((tm, tc), lambda i, j: (i, j))
    return pl.pallas_call(
        body,
        name=name,
        grid=(t // tm, d // tc),
        in_specs=[
            pl.BlockSpec((tm, tc), lambda i, j: (i, a0 + j)),
            pl.BlockSpec((tm, tc), lambda i, j: (i, c0 + j)),
            blk,
            blk,
        ],
        out_specs=blk,
        out_shape=jax.ShapeDtypeStruct((t, d), BF),
        compiler_params=_cp("parallel", "parallel"),
    )(proj, proj, ya, yc)


def _gate_bwd(dm, proj, ya, yc, *, ga0, gc0, name, tm=512, tc=512):
    t, d = ya.shape
    tm, tc = min(tm, t), math.gcd(tc, d, ga0, gc0)
    a0, c0 = ga0 // tc, gc0 // tc

    def body(dm_ref, ga_ref, gc_ref, ya_ref, yc_ref, dya_ref, dyc_ref, dga_ref, dgc_ref):
        dmv = dm_ref[...]
        sa = _sigmoid(ga_ref[...])
        sc = _sigmoid(gc_ref[...])
        dya_ref[...] = (dmv * sa).astype(BF)
        dyc_ref[...] = (dmv * sc).astype(BF)
        dga_ref[...] = (dmv * ya_ref[...] * (sa * (1.0 - sa))).astype(BF)
        dgc_ref[...] = (dmv * yc_ref[...] * (sc * (1.0 - sc))).astype(BF)

    blk = pl.BlockSpec((tm, tc), lambda i, j: (i, j))
    out = jax.ShapeDtypeStruct((t, d), BF)
    return pl.pallas_call(
        body,
        name=name,
        grid=(t // tm, d // tc),
        in_specs=[
            blk,
            pl.BlockSpec((tm, tc), lambda i, j: (i, a0 + j)),
            pl.BlockSpec((tm, tc), lambda i, j: (i, c0 + j)),
            blk,
            blk,
        ],
        out_specs=[blk, blk, blk, blk],
        out_shape=[out, out, out, out],
        compiler_params=_cp("parallel", "parallel"),
    )(dm, proj, proj, ya, yc)


def _conv_taps(cz, czp, i):
    czp = czp * (i > 0).astype(F32)
    h1 = czp[SUBLANES - 1:SUBLANES, :]
    h2 = czp[SUBLANES - 2:SUBLANES - 1, :]
    row = lax.broadcasted_iota(jnp.int32, cz.shape, 0)
    s1 = jnp.where(row == 0, h1, pltpu.roll(cz, 1, 0))
    s2 = jnp.where(row == 0, h2, jnp.where(row == 1, h1, pltpu.roll(cz, 2, 0)))
    return s1, s2


def _conv_fwd(proj, w8, *, z0, gb0, gc0, cw, name, tm=512, tc=512):
    t = proj.shape[0]
    tm, tc = min(tm, t), math.gcd(tc, cw, z0, gb0, gc0)
    zb, bb, cb = z0 // tc, gb0 // tc, gc0 // tc
    rb = tm // SUBLANES

    def body(z_ref, gb_ref, gc_ref, zp_ref, gcp_ref, w_ref, o_ref):
        i = pl.program_id(0)
        cz = gc_ref[...] * z_ref[...]
        s1, s2 = _conv_taps(cz, gcp_ref[...] * zp_ref[...], i)
        w = w_ref[...]
        y = w[0:1, :] * s2 + w[1:2, :] * s1 + w[2:3, :] * cz
        o_ref[...] = (gb_ref[...] * y).astype(BF)

    def cur(b0):
        return pl.BlockSpec((tm, tc), lambda i, j: (i, b0 + j))

    def prev(b0):
        return pl.BlockSpec((SUBLANES, tc), lambda i, j: (jnp.maximum(i * rb - 1, 0), b0 + j))

    return pl.pallas_call(
        body,
        name=name,
        grid=(t // tm, cw // tc),
        in_specs=[cur(zb), cur(bb), cur(cb), prev(zb), prev(cb), pl.BlockSpec((SUBLANES, tc), lambda i, j: (0, j))],
        out_specs=pl.BlockSpec((tm, tc), lambda i, j: (i, j)),
        out_shape=jax.ShapeDtypeStruct((t, cw), BF),
        compiler_params=_cp("parallel", "parallel"),
    )(proj, proj, proj, proj, proj, w8)


def _conv_bwd(proj, w8, dcy, *, z0, gb0, gc0, cw, name, tm=512, tc=512):
    t = proj.shape[0]
    tm, tc = min(tm, t), math.gcd(tc, cw, z0, gb0, gc0)
    zb, bb, cb = z0 // tc, gb0 // tc, gc0 // tc
    rb = tm // SUBLANES
    nt = t // tm

    def body(z_ref, gb_ref, gc_ref, zp_ref, gcp_ref, d_ref, dn_ref, gbn_ref, w_ref, dz_ref, dgb_ref, dgc_ref, dw_ref):
        i = pl.program_id(1)
        z = z_ref[...]
        gc = gc_ref[...]
        gb = gb_ref[...]
        cz = gc * z
        s1, s2 = _conv_taps(cz, gcp_ref[...] * zp_ref[...], i)
        w = w_ref[...]
        w0, w1, w2 = w[0:1, :], w[1:2, :], w[2:3, :]
        yc = w0 * s2 + w1 * s1 + w2 * cz
        dcyv = d_ref[...]
        dgb_ref[...] = (dcyv * yc).astype(BF)
        dyc = dcyv * gb
        dycn = dn_ref[...] * gbn_ref[...] * (i < nt - 1).astype(F32)
        n1, n2 = dycn[0:1, :], dycn[1:2, :]
        row = lax.broadcasted_iota(jnp.int32, cz.shape, 0)
        a1 = jnp.where(row == tm - 1, n1, pltpu.roll(dyc, tm - 1, 0))
        a2 = jnp.where(row == tm - 1, n2, jnp.where(row == tm - 2, n1, pltpu.roll(dyc, tm - 2, 0)))
        dcz = w2 * dyc + w1 * a1 + w0 * a2
        dz_ref[...] = (dcz * gc).astype(BF)
        dgc_ref[...] = (dcz * z).astype(BF)
        dw0 = jnp.sum(dyc * s2, axis=0, keepdims=True)
        dw1 = jnp.sum(dyc * s1, axis=0, keepdims=True)
        dw2 = jnp.sum(dyc * cz, axis=0, keepdims=True)
        r8 = lax.broadcasted_iota(jnp.int32, (SUBLANES, tc), 0)
        upd = jnp.where(r8 == 0, dw0, jnp.where(r8 == 1, dw1, jnp.where(r8 == 2, dw2, 0.0)))

        @pl.when(i == 0)
        def _():
            dw_ref[...] = jnp.zeros_like(dw_ref)

        dw_ref[...] += upd

    def cur(b0):
        return pl.BlockSpec((tm, tc), lambda j, i: (i, b0 + j))

    def prev(b0):
        return pl.BlockSpec((SUBLANES, tc), lambda j, i: (jnp.maximum(i * rb - 1, 0), b0 + j))

    def nxt(b0):
        return pl.BlockSpec((SUBLANES, tc), lambda j, i: (jnp.minimum((i + 1) * rb, t // SUBLANES - 1), b0 + j))

    blk = pl.BlockSpec((tm, tc), lambda j, i: (i, j))
    w_spec = pl.BlockSpec((SUBLANES, tc), lambda j, i: (0, j))
    out = jax.ShapeDtypeStruct((t, cw), BF)
    return pl.pallas_call(
        body,
        name=name,
        grid=(cw // tc, nt),
        in_specs=[cur(zb), cur(bb), cur(cb), prev(zb), prev(cb), blk, nxt(0), nxt(bb), w_spec],
        out_specs=[blk, blk, blk, w_spec],
        out_shape=[out, out, out, jax.ShapeDtypeStruct((SUBLANES, cw), F32)],
        compiler_params=_cp("parallel", "arbitrary"),
    )(proj, proj, proj, proj, proj, dcy, dcy, proj, w8)


def _rot_half(x):
    lane = lax.broadcasted_iota(jnp.int32, x.shape, 1)
    first = (lane % HEAD_DIM) < (HEAD_DIM // 2)
    return jnp.where(first, pltpu.roll(x, LANES - HEAD_DIM // 2, 1), pltpu.roll(x, HEAD_DIM // 2, 1))


def _rope(x, c, s):
    parts = []
    for a in range(x.shape[1] // LANES):
        xa = x[:, a * LANES:(a + 1) * LANES]
        parts.append(xa * c + _rot_half(xa) * s)
    return parts[0] if len(parts) == 1 else jnp.concatenate(parts, axis=1)


def _rope_bwd(dy, c, s):
    parts = []
    for a in range(dy.shape[1] // LANES):
        da = dy[:, a * LANES:(a + 1) * LANES]
        parts.append(da * c + _rot_half(da * s))
    return parts[0] if len(parts) == 1 else jnp.concatenate(parts, axis=1)


def _band_mask(i):
    b = WINDOW
    r = lax.broadcasted_iota(jnp.int32, (b, 2 * b), 0)
    c = lax.broadcasted_iota(jnp.int32, (b, 2 * b), 1)
    no_prev = jnp.where(i > 0, 0, 2 * b)
    return ((c < b) & (c > r + no_prev)) | ((c >= b) & ((c - b) <= r))


def _chunk(x, a):
    return x[:, a * LANES:(a + 1) * LANES]


def _kv_aligned(kp, kc, h):
    band = jnp.concatenate([_chunk(kp, h // 2), _chunk(kc, h // 2)], axis=0).astype(F32)
    swapped = pltpu.roll(band, HEAD_DIM, 1)
    return (band, swapped) if h % 2 == 0 else (swapped, band)


def _swa_fwd(proj, cosf, sinf, sinks, *, nq, name):
    t = proj.shape[0]
    nkv = nq // Q_PER_KV
    aw, kw, b = nq * HEAD_DIM, nkv * HEAD_DIM, WINDOW
    nb = t // b
    kblk = aw // kw
    scale = HEAD_DIM ** -0.5

    def body(sink_ref, q_ref, kc_ref, kp_ref, vc_ref, vp_ref, cc_ref, cp_ref, sc_ref, sp_ref, o_ref, qr_ref, kr_ref):
        i = pl.program_id(0)
        cc, sc, cpv, spv = cc_ref[...], sc_ref[...], cp_ref[...], sp_ref[...]
        qr = _rope(q_ref[...], cc, sc)
        kc = _rope(kc_ref[...], cc, sc)
        kp = _rope(kp_ref[...], cpv, spv)
        qr_ref[...] = qr.astype(BF)
        kr_ref[...] = kc.astype(BF)
        vc, vp = vc_ref[...], vp_ref[...]
        valid = _band_mask(i)
        lo = lax.broadcasted_iota(jnp.int32, (b, LANES), 1) < HEAD_DIM
        for a in range(nq // 2):
            h = (2 * a) // Q_PER_KV
            ks = [x.astype(BF) for x in _kv_aligned(kp, kc, h)]
            vs = [x.astype(BF) for x in _kv_aligned(vp, vc, h)]
            qa = _chunk(qr, a)
            o_par = []
            for par in range(2):
                hq = 2 * a + par
                qm = jnp.where(lo if par == 0 else ~lo, qa, 0.0).astype(BF)
                s = lax.dot_general(qm, ks[par], _DIMS["nt"], preferred_element_type=F32) * scale
                s = jnp.where(valid, s, -jnp.inf)
                sink = sink_ref[hq]
                m = jnp.maximum(jnp.max(s, axis=-1, keepdims=True), sink)
                p = jnp.exp(s - m)
                p = p / (jnp.sum(p, axis=-1, keepdims=True) + jnp.exp(sink - m))
                o_par.append(jnp.dot(p.astype(BF), vs[par], preferred_element_type=F32))
            o_ref[:, a * LANES:(a + 1) * LANES] = jnp.where(lo, o_par[0], o_par[1]).astype(BF)

    def prev_i(i):
        return jnp.maximum(i - 1, 0)

    tab_c = pl.BlockSpec((b, LANES), lambda i: (i, 0))
    tab_p = pl.BlockSpec((b, LANES), lambda i: (prev_i(i), 0))
    return pl.pallas_call(
        body,
        name=name,
        grid=(nb,),
        in_specs=[
            pl.BlockSpec(memory_space=pltpu.SMEM),
            pl.BlockSpec((b, aw), lambda i: (i, 0)),
            pl.BlockSpec((b, kw), lambda i: (i, kblk)),
            pl.BlockSpec((b, kw), lambda i: (prev_i(i), kblk)),
            pl.BlockSpec((b, kw), lambda i: (i, kblk + 1)),
            pl.BlockSpec((b, kw), lambda i: (prev_i(i), kblk + 1)),
            tab_c,
            tab_p,
            tab_c,
            tab_p,
        ],
        out_specs=[
            pl.BlockSpec((b, aw), lambda i: (i, 0)),
            pl.BlockSpec((b, aw), lambda i: (i, 0)),
            pl.BlockSpec((b, kw), lambda i: (i, 0)),
        ],
        out_shape=[
            jax.ShapeDtypeStruct((t, aw), BF),
            jax.ShapeDtypeStruct((t, aw), BF),
            jax.ShapeDtypeStruct((t, kw), BF),
        ],
        compiler_params=_cp("parallel"),
    )(sinks, proj, proj, proj, proj, proj, cosf, cosf, sinf, sinf)


def _swa_bwd(qr, kr, proj, do, cosf, sinf, sinks, *, nq, name):
    t = proj.shape[0]
    nkv = nq // Q_PER_KV
    aw, kw, b = nq * HEAD_DIM, nkv * HEAD_DIM, WINDOW
    nb = t // b
    kblk = aw // kw
    scale = HEAD_DIM ** -0.5

    def body(sink_ref, q_ref, kc_ref, kp_ref, vc_ref, vp_ref, do_ref, cc_ref, cp_ref, sc_ref, sp_ref,
             dq_ref, dk_ref, dv_ref, ds_ref, ck_ref, cv_ref, sacc_ref):
        i = pl.program_id(0)

        @pl.when(i == 0)
        def _():
            ck_ref[...] = jnp.zeros_like(ck_ref)
            cv_ref[...] = jnp.zeros_like(cv_ref)
            sacc_ref[...] = jnp.zeros_like(sacc_ref)

        @pl.when(i < nb)
        def _():
            q = q_ref[...]
            kc, kp = kc_ref[...], kp_ref[...]
            vc, vp = vc_ref[...], vp_ref[...]
            dov = do_ref[...]
            valid = _band_mask(i)
            lane = lax.broadcasted_iota(jnp.int32, (b, LANES), 1)
            lo = lane < HEAD_DIM
            cc, sc = cc_ref[...], sc_ref[...]
            nch = kw // LANES
            dk_ch = [jnp.zeros((2 * b, LANES), F32) for _ in range(nch)]
            dv_ch = [jnp.zeros((2 * b, LANES), F32) for _ in range(nch)]
            sacc = jnp.zeros((b, LANES), F32)
            for a in range(nq // 2):
                h = (2 * a) // Q_PER_KV
                ks = [x.astype(BF) for x in _kv_aligned(kp, kc, h)]
                vs = [x.astype(BF) for x in _kv_aligned(vp, vc, h)]
                qa = _chunk(q, a).astype(F32)
                doa = _chunk(dov, a).astype(F32)
                dq_par = []
                for par in range(2):
                    hq = 2 * a + par
                    mine = lo if par == 0 else ~lo
                    qm = jnp.where(mine, qa, 0.0).astype(BF)
                    dom = jnp.where(mine, doa, 0.0).astype(BF)
                    s = lax.dot_general(qm, ks[par], _DIMS["nt"], preferred_element_type=F32) * scale
                    s = jnp.where(valid, s, -jnp.inf)
                    sink = sink_ref[hq]
                    m = jnp.maximum(jnp.max(s, axis=-1, keepdims=True), sink)
                    e = jnp.exp(s - m)
                    es = jnp.exp(sink - m)
                    zinv = 1.0 / (jnp.sum(e, axis=-1, keepdims=True) + es)
                    p = e * zinv
                    dp = lax.dot_general(dom, vs[par], _DIMS["nt"], preferred_element_type=F32)
                    delta = jnp.sum(p * dp, axis=-1, keepdims=True)
                    dsv = (p * (dp - delta) * scale).astype(BF)
                    sacc = sacc + jnp.where(lane == hq, -(es * zinv) * delta, 0.0)
                    dq_par.append(jnp.dot(dsv, ks[par], preferred_element_type=F32))
                    dkh = lax.dot_general(dsv, qm, _DIMS["tn"], preferred_element_type=F32)
                    dvh = lax.dot_general(p.astype(BF), dom, _DIMS["tn"], preferred_element_type=F32)
                    if par != h % 2:
                        dkh = pltpu.roll(dkh, HEAD_DIM, 1)
                        dvh = pltpu.roll(dvh, HEAD_DIM, 1)
                    dk_ch[h // 2] = dk_ch[h // 2] + dkh
                    dv_ch[h // 2] = dv_ch[h // 2] + dvh
                dqa = jnp.where(lo, dq_par[0], dq_par[1])
                dq_ref[:, a * LANES:(a + 1) * LANES] = _rope_bwd(dqa, cc, sc).astype(BF)
            dk = dk_ch[0] if nch == 1 else jnp.concatenate(dk_ch, axis=1)
            dv = dv_ch[0] if nch == 1 else jnp.concatenate(dv_ch, axis=1)
            dk_ref[...] = _rope_bwd(ck_ref[...] + dk[:b, :], cp_ref[...], sp_ref[...]).astype(BF)
            dv_ref[...] = (cv_ref[...] + dv[:b, :]).astype(BF)
            ck_ref[...] = dk[b:, :]
            cv_ref[...] = dv[b:, :]
            sacc_ref[...] += sacc

        @pl.when(i == nb)
        def _():
            dk_ref[...] = _rope_bwd(ck_ref[...], cp_ref[...], sp_ref[...]).astype(BF)
            dv_ref[...] = cv_ref[...].astype(BF)
            ds_ref[...] = jnp.broadcast_to(jnp.sum(sacc_ref[...], axis=0, keepdims=True), ds_ref.shape)

    def cur_i(i):
        return jnp.minimum(i, nb - 1)

    def prev_i(i):
        return jnp.clip(i - 1, 0, nb - 1)

    tab_c = pl.BlockSpec((b, LANES), lambda i: (cur_i(i), 0))
    tab_p = pl.BlockSpec((b, LANES), lambda i: (prev_i(i), 0))
    return pl.pallas_call(
        body,
        name=name,
        grid=(nb + 1,),
        in_specs=[
            pl.BlockSpec(memory_space=pltpu.SMEM),
            pl.BlockSpec((b, aw), lambda i: (cur_i(i), 0)),
            pl.BlockSpec((b, kw), lambda i: (cur_i(i), 0)),
            pl.BlockSpec((b, kw), lambda i: (prev_i(i), 0)),
            pl.BlockSpec((b, kw), lambda i: (cur_i(i), kblk + 1)),
            pl.BlockSpec((b, kw), lambda i: (prev_i(i), kblk + 1)),
            pl.BlockSpec((b, aw), lambda i: (cur_i(i), 0)),
            tab_c,
            tab_p,
            tab_c,
            tab_p,
        ],
        out_specs=[
            pl.BlockSpec((b, aw), lambda i: (cur_i(i), 0)),
            pl.BlockSpec((b, kw), lambda i: (prev_i(i), 0)),
            pl.BlockSpec((b, kw), lambda i: (prev_i(i), 0)),
            pl.BlockSpec((SUBLANES, LANES), lambda i: (0, 0)),
        ],
        out_shape=[
            jax.ShapeDtypeStruct((t, aw), BF),
            jax.ShapeDtypeStruct((t, kw), BF),
            jax.ShapeDtypeStruct((t, kw), BF),
            jax.ShapeDtypeStruct((SUBLANES, LANES), F32),
        ],
        scratch_shapes=[pltpu.VMEM((b, kw), F32), pltpu.VMEM((b, kw), F32), pltpu.VMEM((b, LANES), F32)],
        compiler_params=_cp("arbitrary"),
    )(sinks, qr, kr, kr, proj, proj, do, cosf, cosf, sinf, sinf)


def _xattn_fwd(xq, kv, *, name, tq=512):
    t, xw = xq.shape
    mtok = kv.shape[0]
    tq = min(tq, t)
    nh = xw // X_HEAD_DIM
    scale = X_HEAD_DIM ** -0.5

    def body(q_ref, kv_ref, o_ref):
        q = q_ref[...]
        kvv = kv_ref[...]
        outs = []
        for h in range(nh):
            sl = slice(h * X_HEAD_DIM, (h + 1) * X_HEAD_DIM)
            k = kvv[:, sl]
            v = kvv[:, xw + h * X_HEAD_DIM: xw + (h + 1) * X_HEAD_DIM]
            s = lax.dot_general(q[:, sl], k, _DIMS["nt"], preferred_element_type=F32) * scale
            e = jnp.exp(s - jnp.max(s, axis=-1, keepdims=True))
            p = e / jnp.sum(e, axis=-1, keepdims=True)
            outs.append(jnp.dot(p.astype(BF), v, preferred_element_type=F32))
        o_ref[...] = jnp.concatenate(outs, axis=1).astype(BF)

    return pl.pallas_call(
        body,
        name=name,
        grid=(t // tq,),
        in_specs=[pl.BlockSpec((tq, xw), lambda i: (i, 0)), pl.BlockSpec((mtok, 2 * xw), lambda i: (0, 0))],
        out_specs=pl.BlockSpec((tq, xw), lambda i: (i, 0)),
        out_shape=jax.ShapeDtypeStruct((t, xw), BF),
        compiler_params=_cp("parallel"),
    )(xq, kv)


def _xattn_bwd(xq, kv, do, *, name, tq=512):
    t, xw = xq.shape
    mtok = kv.shape[0]
    tq = min(tq, t)
    nh = xw // X_HEAD_DIM
    scale = X_HEAD_DIM ** -0.5

    def body(q_ref, kv_ref, do_ref, dq_ref, dkv_ref):
        i = pl.program_id(0)
        q = q_ref[...]
        kvv = kv_ref[...]
        dov = do_ref[...]
        dqs, dks, dvs = [], [], []
        for h in range(nh):
            sl = slice(h * X_HEAD_DIM, (h + 1) * X_HEAD_DIM)
            k = kvv[:, sl]
            v = kvv[:, xw + h * X_HEAD_DIM: xw + (h + 1) * X_HEAD_DIM]
            qh, doh = q[:, sl], dov[:, sl]
            s = lax.dot_general(qh, k, _DIMS["nt"], preferred_element_type=F32) * scale
            e = jnp.exp(s - jnp.max(s, axis=-1, keepdims=True))
            p = e / jnp.sum(e, axis=-1, keepdims=True)
            dp = lax.dot_general(doh, v, _DIMS["nt"], preferred_element_type=F32)
            delta = jnp.sum(p * dp, axis=-1, keepdims=True)
            dsv = (p * (dp - delta) * scale).astype(BF)
            dqs.append(jnp.dot(dsv, k, preferred_element_type=F32))
            dks.append(lax.dot_general(dsv, qh, _DIMS["tn"], preferred_element_type=F32))
            dvs.append(lax.dot_general(p.astype(BF), doh, _DIMS["tn"], preferred_element_type=F32))
        dq_ref[...] = jnp.concatenate(dqs, axis=1).astype(BF)

        @pl.when(i == 0)
        def _():
            dkv_ref[...] = jnp.zeros_like(dkv_ref)

        dkv_ref[...] += jnp.concatenate(dks + dvs, axis=1)

    row = pl.BlockSpec((tq, xw), lambda i: (i, 0))
    full = pl.BlockSpec((mtok, 2 * xw), lambda i: (0, 0))
    return pl.pallas_call(
        body,
        name=name,
        grid=(t // tq,),
        in_specs=[row, full, row],
        out_specs=[row, full],
        out_shape=[jax.ShapeDtypeStruct((t, xw), BF), jax.ShapeDtypeStruct((mtok, 2 * xw), F32)],
        compiler_params=_cp("arbitrary"),
    )(xq, kv, do)


def _adam_math(w, g, m, v):
    m = ADAM_B1 * m + (1.0 - ADAM_B1) * g
    v = ADAM_B2 * v + (1.0 - ADAM_B2) * (g * g)
    m_hat = m / (1.0 - ADAM_B1 ** ADAM_STEP)
    v_hat = v / (1.0 - ADAM_B2 ** ADAM_STEP)
    delta = -ADAM_LR * (m_hat / (jnp.sqrt(v_hat) + ADAM_EPS) + ADAM_WD * w)
    return delta, m, v


def _row_tile(r, c, n_arrays, budget=24 * 1024 * 1024):
    step = 2 * SUBLANES
    cap = max(step, budget // (2 * n_arrays * c * 4))
    if r <= cap:
        return r
    best = None
    for tr in range(step, cap + 1, step):
        if r % tr == 0:
            best = tr
    assert best is not None, (r, c)
    return best


def _adamw_sum(parts, own, me, w, m, v, *, name):
    _, r, c = parts.shape
    tr = _row_tile(r, c, 12)

    def body(me_ref, p_ref, own_ref, w_ref, m_ref, v_ref, g_ref, d_ref, nm_ref, nv_ref):
        mine = jnp.full((tr, c), me_ref[0], jnp.int32)
        g = None
        for s in range(N_DEV):
            term = jnp.where(mine == s, own_ref[...], p_ref[s]).astype(F32)
            g = term if g is None else g + term
        g_ref[...] = g
        d_ref[...], nm_ref[...], nv_ref[...] = _adam_math(w_ref[...], g, m_ref[...], v_ref[...])

    blk = pl.BlockSpec((tr, c), lambda i, me_ref: (i, 0))
    out = jax.ShapeDtypeStruct((r, c), F32)
    return pl.pallas_call(
        body,
        name=name,
        grid_spec=pltpu.PrefetchScalarGridSpec(
            num_scalar_prefetch=1,
            grid=(r // tr,),
            in_specs=[
                pl.BlockSpec((N_DEV, tr, c), lambda i, me_ref: (0, i, 0)),
                pl.BlockSpec((None, tr, c), lambda i, me_ref: (me_ref[0], i, 0)),
                blk, blk, blk,
            ],
            out_specs=[blk, blk, blk, blk],
        ),
        out_shape=[out, out, out, out],
        compiler_params=_cp("parallel"),
    )(me, parts, own, w, m, v)


def _adamw_small(w, g, m, v, *, name):
    def body(w_ref, g_ref, m_ref, v_ref, d_ref, nm_ref, nv_ref):
        d_ref[...], nm_ref[...], nv_ref[...] = _adam_math(w_ref[...], g_ref[...], m_ref[...], v_ref[...])

    out = jax.ShapeDtypeStruct(w.shape, F32)
    return pl.pallas_call(body, name=name, out_shape=[out, out, out])(w, g, m, v)


def _mesh_pos():
    x, y, c = lax.axis_index("x"), lax.axis_index("y"), lax.axis_index("c")
    return x, y, c


def _peer(x, y, c, mask):
    px = 1 - x if mask & 4 else x
    py = 1 - y if mask & 2 else y
    pc = 1 - c if mask & 1 else c
    return (px, py, pc), 4 * px + 2 * py + pc


_HBM = pl.BlockSpec(memory_space=pltpu.HBM)
_SEM = pl.BlockSpec(memory_space=pltpu.SEMAPHORE)
_EFFECT = pltpu.SideEffectType.DATAFLOW_SIDE_EFFECTING


def _me():
    return 4 * lax.axis_index("x") + 2 * lax.axis_index("y") + lax.axis_index("c")


def _landing(own, me):
    land = lax.empty((N_DEV,) + own.shape, own.dtype)
    return lax.dynamic_update_slice(land, own[None], (me, 0, 0))


def _copy(src, land, send_sem, recv_sem, sem0, x, y, c, k, scatter, arriving):
    me = 4 * x + 2 * y + c
    peer, pidx = _peer(x, y, c, k + 1)
    return pltpu.make_async_remote_copy(
        src_ref=src.at[pidx] if scatter else src,
        dst_ref=land.at[pidx if arriving else me],
        send_sem=send_sem.at[sem0 + k], recv_sem=recv_sem.at[sem0 + k], device_id=peer, device_id_type=MESH)


def _exchange_start(groups, *, scatter, name):
    flat = [p for g in groups for p in g]
    n, ng = len(flat), len(groups)

    def body(*refs):
        srcs, lands = refs[:n], refs[n:2 * n]
        sems = refs[2 * n:2 * n + 2 * ng]
        token = refs[-1]
        x, y, c = _mesh_pos()
        w = 0
        for gi, g in enumerate(groups):
            for wi in range(len(g)):
                for k in range(N_DEV - 1):
                    _copy(srcs[w], lands[w], sems[2 * gi], sems[2 * gi + 1], wi * (N_DEV - 1),
                          x, y, c, k, scatter, False).start()
                w += 1
        token[...] = jnp.zeros_like(token)

    sem_shapes = []
    for g in groups:
        sem_shapes += [pltpu.SemaphoreType.DMA((len(g) * (N_DEV - 1),))] * 2
    args = [pltpu.with_memory_space_constraint(s, pltpu.HBM) for s, _ in flat]
    args += [pltpu.with_memory_space_constraint(l, pltpu.HBM) for _, l in flat]
    outs = pl.pallas_call(
        body,
        name=name,
        in_specs=[_HBM] * (2 * n),
        out_specs=[_SEM] * (2 * ng) + [_HBM] * (2 * n) + [pl.BlockSpec(memory_space=pltpu.VMEM)],
        out_shape=sem_shapes + [pltpu.HBM(a.shape, a.dtype) for a in args]
        + [jax.ShapeDtypeStruct((SUBLANES, LANES), F32)],
        input_output_aliases={i: 2 * ng + i for i in range(2 * n)},
        compiler_params=pltpu.CompilerParams(has_side_effects=_EFFECT),
    )(*args)
    sems, thru, token = outs[:2 * ng], outs[2 * ng:2 * ng + 2 * n], outs[-1]
    res, w = [], 0
    for gi, g in enumerate(groups):
        m = len(g)
        res.append((sems[2 * gi], sems[2 * gi + 1], list(thru[w:w + m]), list(thru[n + w:n + w + m])))
        w += m
    return res, token


def _exchange_wait(group, after, *, scatter, name):
    send_sems, recv_sems, srcs_in, lands_in = group
    n = len(srcs_in)

    def body(*refs):
        srcs, lands = refs[:n], refs[n:2 * n]
        send_sem, recv_sem = refs[2 * n], refs[2 * n + 1]
        x, y, c = _mesh_pos()
        for w in range(n):
            for k in range(N_DEV - 1):
                cp = _copy(srcs[w], lands[w], send_sem, recv_sem, w * (N_DEV - 1), x, y, c, k, scatter, True)
                cp.wait_send()
                cp.wait_recv()

    outs = pl.pallas_call(
        body,
        name=name,
        in_specs=[_HBM] * (2 * n) + [_SEM, _SEM, pl.BlockSpec(memory_space=pl.ANY)],
        out_specs=[_HBM] * (2 * n),
        out_shape=[pltpu.HBM(a.shape, a.dtype) for a in srcs_in + lands_in],
        input_output_aliases={i: i for i in range(2 * n)},
        compiler_params=pltpu.CompilerParams(has_side_effects=_EFFECT),
    )(*srcs_in, *lands_in, send_sems, recv_sems, after)
    return list(outs[:n]), list(outs[n:])


def _all_reduce_small(parts, rows, width, *, name):
    n = len(parts)

    def body(*refs):
        ins = refs[:n]
        o_ref, pack_ref, buf_ref, send_sems, recv_sems = refs[n:]
        x, y, c_ = _mesh_pos()
        me = 4 * x + 2 * y + c_
        pack_ref[...] = jnp.zeros_like(pack_ref)
        for ref, (arr, r0, nr) in zip(ins, parts):
            pack_ref[r0:r0 + nr, 0:arr.shape[1]] = ref[0:nr, :]
        sends, recvs = [], []
        for k in range(N_DEV - 1):
            peer, pidx = _peer(x, y, c_, k + 1)
            cp = pltpu.make_async_remote_copy(
                src_ref=pack_ref, dst_ref=buf_ref.at[me], send_sem=send_sems.at[k], recv_sem=recv_sems.at[k],
                device_id=peer, device_id_type=MESH)
            cp.start()
            sends.append(cp)
            recvs.append(pltpu.make_async_remote_copy(
                src_ref=pack_ref, dst_ref=buf_ref.at[pidx], send_sem=send_sems.at[k], recv_sem=recv_sems.at[k],
                device_id=peer, device_id_type=MESH))
        buf_ref[me] = pack_ref[...]
        for rc in recvs:
            rc.wait_recv()
        for cp in sends:
            cp.wait_send()
        acc = buf_ref[0]
        for s in range(1, N_DEV):
            acc = acc + buf_ref[s]
        o_ref[...] = acc

    vmem = pl.BlockSpec(memory_space=pltpu.VMEM)
    return pl.pallas_call(
        body,
        name=name,
        in_specs=[vmem] * n,
        out_specs=vmem,
        out_shape=jax.ShapeDtypeStruct((rows, width), F32),
        scratch_shapes=[
            pltpu.VMEM((rows, width), F32),
            pltpu.VMEM((N_DEV, rows, width), F32),
            pltpu.SemaphoreType.DMA((N_DEV - 1,)),
            pltpu.SemaphoreType.DMA((N_DEV - 1,)),
        ],
    )(*[p[0] for p in parts])


def _rope_tables(t):
    half = HEAD_DIM // 2
    inv_freq = ROPE_THETA ** (-jnp.arange(half, dtype=F32) / half)
    ang = jnp.arange(t, dtype=jnp.int32).astype(F32)[:, None] * inv_freq[None, :]
    cos, sin = jnp.cos(ang), jnp.sin(ang)
    cosf = jnp.concatenate([cos, cos, cos, cos], axis=1)
    sinf = jnp.concatenate([-sin, sin, -sin, sin], axis=1)
    return cosf, sinf


def _local_step(x, mem, target, gains, sinks, aw, cw, get_w, put_g, dep0=None):
    t, d = x.shape
    nq = aw // HEAD_DIM
    kw = aw // Q_PER_KV
    z0 = aw + 2 * kw
    gb0, gc0 = z0 + cw, z0 + 2 * cw
    ga0 = z0 + 3 * cw
    gcm0 = ga0 + d
    cosf, sinf = _rope_tables(t)

    u1 = _rms_fwd(x, gains["g_mix"], name="rms_mix", dep=dep0)
    mem_n = _rms_fwd(mem, gains["g_mem"], name="rms_mem", dep=dep0)
    w_in = get_w("w_in", mem_n)
    proj = _mm(u1, w_in, mode="nn", tm=1024, tn=512, tk=2048, out_dtype=F32, name="mm_in")
    o_attn, q_rot, k_rot = _swa_fwd(proj, cosf, sinf, sinks, nq=nq, name="swa_fwd")
    conv_w8 = get_w("conv_w8", o_attn)
    w_attn_proj, w_conv_proj, w_mix_out = (get_w(n, o_attn) for n in ("w_attn_proj", "w_conv_proj", "w_mix_out"))
    w_xq, w_xkv, w_xo = (get_w(n, o_attn) for n in ("w_xq", "w_xkv", "w_xo"))
    y_attn = _mm(o_attn, w_attn_proj, mode="nn", tm=1024, tn=1024, tk=1024, out_dtype=F32, name="mm_attn_proj")
    cy = _conv_fwd(proj, conv_w8, z0=z0, gb0=gb0, gc0=gc0, cw=cw, name="conv_fwd")
    y_conv = _mm(cy, w_conv_proj, mode="nn", tm=1024, tn=1024, tk=1024, out_dtype=F32, name="mm_conv_proj")
    merged = _gate_fwd(proj, y_attn, y_conv, ga0=ga0, gc0=gcm0, name="gate_fwd")
    h1 = _mm(merged, w_mix_out, mode="nn", tm=1024, tn=1024, tk=2048, out_dtype=F32, name="mm_mix_out", residual=x)
    u2 = _rms_fwd(h1, gains["g_xattn"], name="rms_xattn")
    xq = _mm(u2, w_xq, mode="nn", tm=1024, tn=512, tk=2048, out_dtype=BF, name="mm_xq")
    kv = _mm(mem_n, w_xkv, mode="nn", tm=256, tn=1024, tk=2048, out_dtype=BF, name="mm_xkv")
    o_x = _xattn_fwd(xq, kv, name="xattn_fwd")
    h2 = _mm(o_x, w_xo, mode="nn", tm=1024, tn=1024, tk=512, out_dtype=F32, name="mm_xo", residual=h1)
    u3 = _rms_fwd(h2, gains["g_ffn"], name="rms_ffn")
    w_ffn_in = get_w("w_ffn_in", u3)
    hid2, act = _ffn_in_fwd(u3, w_ffn_in, name="mm_ffn_in")
    w_ffn_out = get_w("w_ffn_out", act)
    h3 = _mm(act, w_ffn_out, mode="nn", tm=512, tn=1024, tk=8192, out_dtype=F32, name="mm_ffn_out", residual=h2)

    tt = 8192
    dh3, dh3b, loss_tile, dg_final = _loss_head(h3, target, gains["g_final"], name="loss_head")
    tok = put_g("w_ffn_out", _mm(act, dh3b, mode="tn", tm=512, tn=1024, tk=tt, out_dtype=BF, name="mm_dw_ffn_out"))
    dhid2 = _ffn_out_bwd(dh3b, w_ffn_out, hid2, name="mm_dact", dep=tok)
    f2 = w_ffn_in.shape[1]
    tok = put_g("w_ffn_in", _mm(u3, dhid2, mode="tn", tm=1024, tn=f2 // N_DEV, tk=tt, out_dtype=BF,
                                name="mm_dw_ffn_in", b_planes=2, stacked=True), stacked=True)
    du3 = _mm(dhid2, w_ffn_in, mode="nt", tm=1024, tn=1024, tk=2816, out_dtype=F32, name="mm_du3", dep=tok,
              a_planes=2)
    dh2, dh2b, dg_ffn = _rms_bwd(du3, h2, gains["g_ffn"], dh3, name="rms_ffn_bwd")
    put_g("w_xo", _mm(o_x, dh2b, mode="tn", tm=512, tn=d // N_DEV, tk=tt, out_dtype=BF, name="mm_dw_xo",
                      stacked=True), stacked=True)
    do_x = _mm(dh2b, w_xo, mode="nt", tm=1024, tn=512, tk=2048, out_dtype=BF, name="mm_do_x")
    dxq, dkv = _xattn_bwd(xq, kv, do_x, name="xattn_bwd")
    put_g("w_xkv", _mm(mem_n, dkv, mode="tn", tm=1024, tn=1024, tk=256, out_dtype=BF, name="mm_dw_xkv"))
    tok = put_g("w_xq", _mm(u2, dxq, mode="tn", tm=1024, tn=512, tk=tt, out_dtype=BF, name="mm_dw_xq"))
    du2 = _mm(dxq, w_xq, mode="nt", tm=1024, tn=1024, tk=512, out_dtype=F32, name="mm_du2", dep=tok)
    dmem_n = _mm(dkv, w_xkv, mode="nt", tm=256, tn=1024, tk=1024, out_dtype=F32, name="mm_dmem")
    _, _, dg_mem = _rms_bwd(dmem_n, mem, gains["g_mem"], None, name="rms_mem_bwd")
    dh1, dh1b, dg_xattn = _rms_bwd(du2, h1, gains["g_xattn"], dh2, name="rms_xattn_bwd")
    put_g("w_mix_out", _mm(merged, dh1b, mode="tn", tm=1024, tn=1024, tk=tt, out_dtype=BF, name="mm_dw_mix_out"))
    dmerged = _mm(dh1b, w_mix_out, mode="nt", tm=1024, tn=1024, tk=2048, out_dtype=F32, name="mm_dmerged")
    dya, dyc, dga, dgc = _gate_bwd(dmerged, proj, y_attn, y_conv, ga0=ga0, gc0=gcm0, name="gate_bwd")
    put_g("w_attn_proj", _mm(o_attn, dya, mode="tn", tm=1024, tn=d // N_DEV, tk=tt, out_dtype=BF,
                             name="mm_dw_attn_proj", stacked=True), stacked=True)
    do_attn = _mm(dya, w_attn_proj, mode="nt", tm=1024, tn=1024, tk=2048, out_dtype=BF, name="mm_do_attn")
    tok = put_g("w_conv_proj", _mm(cy, dyc, mode="tn", tm=1024, tn=d // N_DEV, tk=tt, out_dtype=BF,
                                   name="mm_dw_conv_proj", stacked=True), stacked=True)
    dcy = _mm(dyc, w_conv_proj, mode="nt", tm=1024, tn=1024, tk=2048, out_dtype=F32, name="mm_dcy", dep=tok)
    dz, dgb, dgcv, dconv_w8 = _conv_bwd(proj, conv_w8, dcy, z0=z0, gb0=gb0, gc0=gc0, cw=cw, name="conv_bwd")
    dq, dk, dv, dsink_tile = _swa_bwd(q_rot, k_rot, proj, do_attn, cosf, sinf, sinks, nq=nq, name="swa_bwd")
    dproj = jnp.concatenate([dq, dk, dv, dz, dgb, dgcv, dga, dgc], axis=1)
    tok = put_g("w_in", _mm(u1, dproj, mode="tn", tm=1024, tn=512, tk=tt, out_dtype=BF, name="mm_dw_in"))
    du1 = _mm(dproj, w_in, mode="nt", tm=512, tn=1024, tk=4352, out_dtype=F32, name="mm_du1", dep=tok)
    grad_x, _, dg_mix = _rms_bwd(du1, x, gains["g_mix"], dh1, name="rms_mix_bwd")

    small = {
        "g_mix": dg_mix, "g_xattn": dg_xattn, "g_mem": dg_mem, "g_ffn": dg_ffn, "g_final": dg_final,
        "attn_sinks": dsink_tile, "conv_w8": dconv_w8, "loss": loss_tile,
    }
    return grad_x, small


_COL_SHARDED = ("w_in", "w_attn_proj", "w_conv_proj", "w_xo", "w_ffn_in")
_ROW_SHARDED = ("w_mix_out", "w_xq", "w_xkv", "w_ffn_out")
_BIG = _COL_SHARDED + _ROW_SHARDED
_GAINS = ("g_mix", "g_xattn", "g_mem", "g_ffn", "g_final")
_GATHER_GROUPS = (("w_in",), ("conv_w8", "w_attn_proj", "w_conv_proj", "w_mix_out", "w_xq", "w_xkv", "w_xo"),
                  ("w_ffn_in",), ("w_ffn_out",))
_SCATTER_GROUPS = (("w_ffn_out",), ("w_ffn_in",), ("w_xo", "w_xq", "w_xkv"),
                   ("w_mix_out", "w_attn_proj", "w_conv_proj"), ("w_in",))
_WEIGHTS = ("g_mix", "w_in", "conv_w", "attn_sinks", "w_attn_proj", "w_conv_proj", "w_mix_out", "g_xattn", "g_mem",
            "w_xq", "w_xkv", "w_xo", "g_ffn", "w_ffn_in", "w_ffn_out", "g_final")


def _unstack(g, col_sharded):
    n, r, c = g.shape
    if col_sharded:
        return jnp.transpose(g, (1, 0, 2)).reshape(r, n * c)
    return g.reshape(n * r, c)


def _stack(w, col_sharded):
    r, c = w.shape
    if col_sharded:
        return jnp.transpose(w.reshape(r, N_DEV, c // N_DEV), (1, 0, 2))
    return w.reshape(N_DEV, r // N_DEV, c)


def kernel(x, mem, g_mix, w_in, conv_w, attn_sinks, w_attn_proj, w_conv_proj, w_mix_out, g_xattn, g_mem, w_xq, w_xkv, w_xo, g_ffn, w_ffn_in, w_ffn_out, g_final, loss_target, m_g_mix, m_w_in, m_conv_w, m_attn_sinks, m_w_attn_proj, m_w_conv_proj, m_w_mix_out, m_g_xattn, m_g_mem, m_w_xq, m_w_xkv, m_w_xo, m_g_ffn, m_w_ffn_in, m_w_ffn_out, m_g_final, v_g_mix, v_w_in, v_conv_w, v_attn_sinks, v_w_attn_proj, v_w_conv_proj, v_w_mix_out, v_g_xattn, v_g_mem, v_w_xq, v_w_xkv, v_w_xo, v_g_ffn, v_w_ffn_in, v_w_ffn_out, v_g_final):
    w_ = dict(g_mix=g_mix, w_in=w_in, conv_w=conv_w, attn_sinks=attn_sinks, w_attn_proj=w_attn_proj,
              w_conv_proj=w_conv_proj, w_mix_out=w_mix_out, g_xattn=g_xattn, g_mem=g_mem, w_xq=w_xq, w_xkv=w_xkv,
              w_xo=w_xo, g_ffn=g_ffn, w_ffn_in=w_ffn_in, w_ffn_out=w_ffn_out, g_final=g_final)
    m_ = dict(g_mix=m_g_mix, w_in=m_w_in, conv_w=m_conv_w, attn_sinks=m_attn_sinks, w_attn_proj=m_w_attn_proj,
              w_conv_proj=m_w_conv_proj, w_mix_out=m_w_mix_out, g_xattn=m_g_xattn, g_mem=m_g_mem, w_xq=m_w_xq,
              w_xkv=m_w_xkv, w_xo=m_w_xo, g_ffn=m_g_ffn, w_ffn_in=m_w_ffn_in, w_ffn_out=m_w_ffn_out,
              g_final=m_g_final)
    v_ = dict(g_mix=v_g_mix, w_in=v_w_in, conv_w=v_conv_w, attn_sinks=v_attn_sinks, w_attn_proj=v_w_attn_proj,
              w_conv_proj=v_w_conv_proj, w_mix_out=v_w_mix_out, g_xattn=v_g_xattn, g_mem=v_g_mem, w_xq=v_w_xq,
              w_xkv=v_w_xkv, w_xo=v_w_xo, g_ffn=v_g_ffn, w_ffn_in=v_w_ffn_in, w_ffn_out=v_w_ffn_out,
              g_final=v_g_final)
    t, d = x.shape[1], x.shape[2]
    nq = attn_sinks.shape[-1]
    cw_shard = conv_w.shape[-1]
    cw = cw_shard * N_DEV

    def two_d(a):
        return a.reshape(a.shape[-2], a.shape[-1]) if a.ndim == 3 else a.reshape(1, a.shape[-1])

    me = _me()
    col = set(_COL_SHARDED) | {"conv_w8"}

    shards = {n: two_d(w_[n]).astype(BF) for n in _BIG}
    shards["conv_w8"] = jnp.zeros((SUBLANES, cw_shard), F32).at[:3].set(two_d(conv_w))
    gathers, token = _exchange_start(
        [[(shards[n], _landing(shards[n], me)) for n in g] for g in _GATHER_GROUPS], scatter=False,
        name="gather_start")
    full = {}

    def get_w(name, after):
        if name not in full:
            gi = [name in g for g in _GATHER_GROUPS].index(True)
            _, lands = _exchange_wait(gathers[gi], after, scatter=False, name="gather_wait_%d" % gi)
            for n, land in zip(_GATHER_GROUPS[gi], lands):
                full[n] = _unstack(land, n in col)
        return full[name]

    pending, scatters = {}, []

    def put_g(name, dw, stacked=False):
        pending[name] = dw if stacked else _stack(dw, name in col)
        gi = [name in g for g in _SCATTER_GROUPS].index(True)
        group = _SCATTER_GROUPS[gi]
        if not all(n in pending for n in group):
            return None
        pairs = [(pending[n], lax.empty(pending[n].shape, pending[n].dtype)) for n in group]
        started, tok = _exchange_start([pairs], scatter=True, name="scatter_start_%d" % gi)
        scatters.append((gi, started[0]))
        return tok

    gains = {n: two_d(w_[n]) for n in _GAINS}
    grad_x, small = _local_step(
        x[0], mem[0], loss_target[0], gains, attn_sinks.reshape(nq), w_attn_proj.shape[-2], cw, get_w, put_g,
        dep0=token)

    grads, deltas, new_m, new_v = {}, {}, {}, {}
    after = grad_x
    me1 = me.reshape(1).astype(jnp.int32)
    for gi, started in scatters:
        mine, parts = _exchange_wait(started, after, scatter=True, name="scatter_wait_%d" % gi)
        for n, own, p in zip(_SCATTER_GROUPS[gi], mine, parts):
            shape = w_[n].shape
            g, dl, nm, nv = _adamw_sum(p, own, me1, two_d(w_[n]), two_d(m_[n]), two_d(v_[n]), name="adamw_" + n)
            grads[n], deltas[n], new_m[n], new_v[n] = (a.reshape(shape) for a in (g, dl, nm, nv))
            after = g

    parts = [(small[n], i, 1) for i, n in enumerate(_GAINS)]
    parts += [(small["attn_sinks"], 5, 1), (small["loss"], 6, 1), (small["conv_w8"], 8, 3)]
    red = _all_reduce_small(parts, 2 * SUBLANES, max(d, cw), name="reduce_small")
    loss = red[6, 0]
    small_g = {n: red[i:i + 1, :d] for i, n in enumerate(_GAINS)}
    small_g["attn_sinks"] = red[5:6, :nq]
    small_g["conv_w"] = lax.dynamic_slice(red, (8, me * cw_shard), (3, cw_shard))
    for n in _GAINS + ("attn_sinks", "conv_w"):
        shape = w_[n].shape
        g = small_g[n]
        dl, nm, nv = _adamw_small(two_d(w_[n]), g, two_d(m_[n]), two_d(v_[n]), name="adamw_" + n)
        grads[n], deltas[n], new_m[n], new_v[n] = (a.reshape(shape) for a in (g, dl, nm, nv))

    return (loss, grad_x[None], *[grads[n] for n in _WEIGHTS], *[deltas[n] for n in _WEIGHTS],
            *[new_m[n] for n in _WEIGHTS], *[new_v[n] for n in _WEIGHTS])
```

```python
import functools
import math

import jax
import jax.numpy as jnp
from jax import lax
from jax.experimental import pallas as pl
from jax.experimental.pallas import tpu as pltpu

HEAD_DIM = 64
Q_PER_KV = 4
WINDOW = 128
X_HEAD_DIM = 128
ROPE_THETA = 10000.0
EPS = 1e-6
ADAM_LR = 0.001
ADAM_B1 = 0.9
ADAM_B2 = 0.999
ADAM_EPS = 1e-08
ADAM_WD = 0.01
ADAM_STEP = 10

N_DEV = 8
LANES = 128
SUBLANES = 8
VMEM_LIMIT_BYTES = 56 * 1024 * 1024
BF = jnp.bfloat16
F32 = jnp.float32
MESH = pl.DeviceIdType.MESH


def _cp(*sem):
    return pltpu.CompilerParams(dimension_semantics=sem, vmem_limit_bytes=VMEM_LIMIT_BYTES)


def _sigmoid(x):
    return 1.0 / (1.0 + jnp.exp(-x))


_DIMS = {
    "nn": (((1,), (0,)), ((), ())),
    "nt": (((1,), (1,)), ((), ())),
    "tn": (((0,), (0,)), ((), ())),
}


def _fit(dim, tile):
    if dim <= tile:
        return dim
    for t in range(tile // LANES * LANES, 0, -LANES):
        if dim % t == 0:
            return t
    return dim


def _mm(a, b, *, mode, tm, tn, tk, out_dtype, name, residual=None, dep=None, a_planes=1, b_planes=1,
        stacked=False):
    if a_planes > 1:
        assert mode == "nt"
        (_, m, kp), (n, k) = a.shape, b.shape
        assert kp * a_planes == k
    elif b_planes > 1:
        assert mode == "tn"
        (k, m), (_, k2, np_) = a.shape, b.shape
        n = np_ * b_planes
        assert k == k2
    elif mode == "nn":
        (m, k), (k2, n) = a.shape, b.shape
        assert k == k2, (name, a.shape, b.shape)
    elif mode == "nt":
        (m, k), (n, k2) = a.shape, b.shape
        assert k == k2, (name, a.shape, b.shape)
    else:
        (k, m), (k2, n) = a.shape, b.shape
        assert k == k2, (name, a.shape, b.shape)
    tm, tn, tk = _fit(m, tm), _fit(n // b_planes, tn), _fit(k // a_planes, tk)
    assert m % tm == 0 and (n // b_planes) % tn == 0 and (k // a_planes) % tk == 0, (name, m, n, k, tm, tn, tk)
    nk = k // tk
    nkp, njp = nk // a_planes, n // tn // b_planes
    if a_planes > 1:
        a_spec = pl.BlockSpec((None, tm, tk), lambda i, j, kk: (kk // nkp, i, kk % nkp))
    elif mode == "tn":
        a_spec = pl.BlockSpec((tk, tm), lambda i, j, kk: (kk, i))
    else:
        a_spec = pl.BlockSpec((tm, tk), lambda i, j, kk: (i, kk))
    if b_planes > 1:
        b_spec = pl.BlockSpec((None, tk, tn), lambda i, j, kk: (j // njp, kk, j % njp))
    elif mode == "nt":
        b_spec = pl.BlockSpec((tn, tk), lambda i, j, kk: (j, kk))
    else:
        b_spec = pl.BlockSpec((tk, tn), lambda i, j, kk: (kk, j))
    if stacked:
        assert residual is None
        o_spec = pl.BlockSpec((None, tm, tn), lambda i, j, kk: (j, i, 0))
        out_shape = jax.ShapeDtypeStruct((n // tn, m, tn), out_dtype)
    else:
        o_spec = pl.BlockSpec((tm, tn), lambda i, j, kk: (i, j))
        out_shape = jax.ShapeDtypeStruct((m, n), out_dtype)
    dims = _DIMS[mode]
    has_res = residual is not None
    n_in = 2 + has_res + (dep is not None)

    def body(*refs):
        a_ref, b_ref, r_ref, o_ref = refs[0], refs[1], refs[2], refs[n_in]
        part = lax.dot_general(a_ref[...].astype(BF), b_ref[...].astype(BF), dims, preferred_element_type=F32)

        def finish(acc):
            if has_res:
                acc = r_ref[...] + acc
            o_ref[...] = acc.astype(out_dtype)

        if nk == 1:
            finish(part)
        else:
            acc_ref = refs[-1]
            kk = pl.program_id(2)

            @pl.when(kk == 0)
            def _():
                acc_ref[...] = part

            @pl.when(kk > 0)
            def _():
                acc_ref[...] += part

            @pl.when(kk == nk - 1)
            def _():
                finish(acc_ref[...])

    in_specs = [a_spec, b_spec] + ([o_spec] if has_res else [])
    args = (a, b) + ((residual,) if has_res else ())
    if dep is not None:
        in_specs.append(pl.BlockSpec(memory_space=pl.ANY))
        args += (dep,)
    return pl.pallas_call(
        body,
        name=name,
        grid=(m // tm, n // tn, nk),
        in_specs=in_specs,
        out_specs=o_spec,
        out_shape=out_shape,
        scratch_shapes=[pltpu.VMEM((tm, tn), F32)] if nk > 1 else [],
        compiler_params=_cp("parallel", "parallel", "arbitrary"),
    )(*args)


def _rms_fwd(h, g, *, name, tm=512, dep=None):
    t, d = h.shape
    tm = min(tm, t)

    def body(*refs):
        h_ref, g_ref, u_ref = refs[0], refs[1], refs[-1]
        hv = h_ref[...]
        r = lax.rsqrt(jnp.mean(hv * hv, axis=-1, keepdims=True) + EPS)
        u_ref[...] = ((hv * r) * g_ref[...]).astype(BF)

    in_specs = [pl.BlockSpec((tm, d), lambda i: (i, 0)), pl.BlockSpec((1, d), lambda i: (0, 0))]
    args = (h, g)
    if dep is not None:
        in_specs.append(pl.BlockSpec(memory_space=pl.ANY))
        args += (dep,)
    return pl.pallas_call(
        body,
        name=name,
        grid=(t // tm,),
        in_specs=in_specs,
        out_specs=pl.BlockSpec((tm, d), lambda i: (i, 0)),
        out_shape=jax.ShapeDtypeStruct((t, d), BF),
        compiler_params=_cp("parallel"),
    )(*args)


def _rms_bwd(du, h, g, dres, *, name, tm=256):
    t, d = h.shape
    tm = min(tm, t)
    want_dh = dres is not None

    def body(*refs):
        if want_dh:
            du_ref, h_ref, g_ref, dres_ref, dh_ref, dhb_ref, dg_ref = refs
        else:
            du_ref, h_ref, g_ref, dg_ref = refs
        i = pl.program_id(0)
        hv = h_ref[...]
        duv = du_ref[...]
        r = lax.rsqrt(jnp.mean(hv * hv, axis=-1, keepdims=True) + EPS)
        nv = hv * r
        if want_dh:
            gy = duv * g_ref[...]
            dh = dres_ref[...] + r * (gy - nv * jnp.mean(nv * gy, axis=-1, keepdims=True))
            dh_ref[...] = dh
            dhb_ref[...] = dh.astype(BF)

        @pl.when(i == 0)
        def _():
            dg_ref[...] = jnp.zeros_like(dg_ref)

        dg_ref[...] += jnp.sum(duv * nv, axis=0, keepdims=True)

    row = pl.BlockSpec((tm, d), lambda i: (i, 0))
    vec = pl.BlockSpec((1, d), lambda i: (0, 0))
    if want_dh:
        in_specs, args = [row, row, vec, row], (du, h, g, dres)
        out_specs = [row, row, vec]
        out_shape = [jax.ShapeDtypeStruct((t, d), F32), jax.ShapeDtypeStruct((t, d), BF),
                     jax.ShapeDtypeStruct((1, d), F32)]
    else:
        in_specs, args = [row, row, vec], (du, h, g)
        out_specs = [vec]
        out_shape = [jax.ShapeDtypeStruct((1, d), F32)]
    outs = pl.pallas_call(
        body,
        name=name,
        grid=(t // tm,),
        in_specs=in_specs,
        out_specs=out_specs,
        out_shape=out_shape,
        compiler_params=_cp("arbitrary"),
    )(*args)
    return (outs[0], outs[1], outs[2]) if want_dh else (None, None, outs[0])


def _loss_head(h, target, g, *, name, tm=256):
    t, d = h.shape
    tm = min(tm, t)

    def body(h_ref, t_ref, g_ref, dh_ref, dhb_ref, loss_ref, dg_ref):
        i = pl.program_id(0)
        hv = h_ref[...]
        gv = g_ref[...]
        r = lax.rsqrt(jnp.mean(hv * hv, axis=-1, keepdims=True) + EPS)
        nv = hv * r
        e = nv * gv - t_ref[...]
        per_tok = jnp.mean(e * e, axis=-1, keepdims=True)
        lp = 0.5 * jnp.sum(per_tok, axis=0, keepdims=True)
        dy = e * (1.0 / d)
        gy = dy * gv
        dh = r * (gy - nv * jnp.mean(nv * gy, axis=-1, keepdims=True))
        dh_ref[...] = dh
        dhb_ref[...] = dh.astype(BF)

        @pl.when(i == 0)
        def _():
            loss_ref[...] = jnp.zeros_like(loss_ref)
            dg_ref[...] = jnp.zeros_like(dg_ref)

        loss_ref[...] += jnp.broadcast_to(lp, loss_ref.shape)
        dg_ref[...] += jnp.sum(dy * nv, axis=0, keepdims=True)

    row = pl.BlockSpec((tm, d), lambda i: (i, 0))
    vec = pl.BlockSpec((1, d), lambda i: (0, 0))
    return pl.pallas_call(
        body,
        name=name,
        grid=(t // tm,),
        in_specs=[row, row, vec],
        out_specs=[row, row, pl.BlockSpec((SUBLANES, LANES), lambda i: (0, 0)), vec],
        out_shape=[
            jax.ShapeDtypeStruct((t, d), F32),
            jax.ShapeDtypeStruct((t, d), BF),
            jax.ShapeDtypeStruct((SUBLANES, LANES), F32),
            jax.ShapeDtypeStruct((1, d), F32),
        ],
        compiler_params=_cp("arbitrary"),
    )(h, target, g)


def _ffn_in_fwd(u, w, *, name, tm=1024, tn=512, dep=None):
    t, d = u.shape
    f = w.shape[1] // 2
    tm, tn = _fit(t, tm), _fit(f, tn)
    nf = f // tn

    def body(*refs):
        u_ref, wa_ref, wb_ref, hid_ref, act_ref = refs[0], refs[1], refs[2], refs[-2], refs[-1]
        uv = u_ref[...]
        a = jnp.dot(uv, wa_ref[...], preferred_element_type=F32)
        b = jnp.dot(uv, wb_ref[...], preferred_element_type=F32)
        hid_ref[0] = a.astype(BF)
        hid_ref[1] = b.astype(BF)
        act_ref[...] = ((a * _sigmoid(a)) * b).astype(BF)

    in_specs = [
        pl.BlockSpec((tm, d), lambda i, j: (i, 0)),
        pl.BlockSpec((d, tn), lambda i, j: (0, j)),
        pl.BlockSpec((d, tn), lambda i, j: (0, nf + j)),
    ]
    args = (u, w, w)
    if dep is not None:
        in_specs.append(pl.BlockSpec(memory_space=pl.ANY))
        args += (dep,)
    return pl.pallas_call(
        body,
        name=name,
        grid=(t // tm, nf),
        in_specs=in_specs,
        out_specs=[pl.BlockSpec((2, tm, tn), lambda i, j: (0, i, j)), pl.BlockSpec((tm, tn), lambda i, j: (i, j))],
        out_shape=[jax.ShapeDtypeStruct((2, t, f), BF), jax.ShapeDtypeStruct((t, f), BF)],
        compiler_params=_cp("parallel", "parallel"),
    )(*args)


def _ffn_out_bwd(dh, w_out, hid2, *, name, tm=1024, tn=512, dep=None):
    t, d = dh.shape
    f = w_out.shape[0]
    tm, tn = _fit(t, tm), _fit(f, tn)

    def body(*refs):
        dh_ref, w_ref, hid_ref, o_ref = refs[0], refs[1], refs[2], refs[-1]
        dact = lax.dot_general(dh_ref[...], w_ref[...], _DIMS["nt"], preferred_element_type=F32)
        a = hid_ref[0].astype(F32)
        b = hid_ref[1].astype(F32)
        sg = _sigmoid(a)
        o_ref[0] = (dact * b * (sg * (1.0 + a * (1.0 - sg)))).astype(BF)
        o_ref[1] = (dact * (a * sg)).astype(BF)

    pair = pl.BlockSpec((2, tm, tn), lambda i, j: (0, i, j))
    in_specs = [pl.BlockSpec((tm, d), lambda i, j: (i, 0)), pl.BlockSpec((tn, d), lambda i, j: (j, 0)), pair]
    args = (dh, w_out, hid2)
    if dep is not None:
        in_specs.append(pl.BlockSpec(memory_space=pl.ANY))
        args += (dep,)
    return pl.pallas_call(
        body,
        name=name,
        grid=(t // tm, f // tn),
        in_specs=in_specs,
        out_specs=pair,
        out_shape=jax.ShapeDtypeStruct((2, t, f), BF),
        compiler_params=_cp("parallel", "parallel"),
    )(*args)


def _gate_fwd(proj, ya, yc, *, ga0, gc0, name, tm=512, tc=512):
    t, d = ya.shape
    tm, tc = min(tm, t), math.gcd(tc, d, ga0, gc0)
    a0, c0 = ga0 // tc, gc0 // tc

    def body(ga_ref, gc_ref, ya_ref, yc_ref, o_ref):
        o_ref[...] = (_sigmoid(ga_ref[...]) * ya_ref[...] + _sigmoid(gc_ref[...]) * yc_ref[...]).astype(BF)

    blk = pl.BlockSpec((tm, tc), lambda i, j: (i, j))
    return pl.pallas_call(
        body,
        name=name,
        grid=(t // tm, d // tc),
        in_specs=[
            pl.BlockSpec((tm, tc), lambda i, j: (i, a0 + j)),
            pl.BlockSpec((tm, tc), lambda i, j: (i, c0 + j)),
            blk,
            blk,
        ],
        out_specs=blk,
        out_shape=jax.ShapeDtypeStruct((t, d), BF),
        compiler_params=_cp("parallel", "parallel"),
    )(proj, proj, ya, yc)


def _gate_bwd(dm, proj, ya, yc, *, ga0, gc0, name, tm=512, tc=512):
    t, d = ya.shape
    tm, tc = min(tm, t), math.gcd(tc, d, ga0, gc0)
    a0, c0 = ga0 // tc, gc0 // tc

    def body(dm_ref, ga_ref, gc_ref, ya_ref, yc_ref, dya_ref, dyc_ref, dga_ref, dgc_ref):
        dmv = dm_ref[...]
        sa = _sigmoid(ga_ref[...])
        sc = _sigmoid(gc_ref[...])
        dya_ref[...] = (dmv * sa).astype(BF)
        dyc_ref[...] = (dmv * sc).astype(BF)
        dga_ref[...] = (dmv * ya_ref[...] * (sa * (1.0 - sa))).astype(BF)
        dgc_ref[...] = (dmv * yc_ref[...] * (sc * (1.0 - sc))).astype(BF)

    blk = pl.BlockSpec((tm, tc), lambda i, j: (i, j))
    out = jax.ShapeDtypeStruct((t, d), BF)
    return pl.pallas_call(
        body,
        name=name,
        grid=(t // tm, d // tc),
        in_specs=[
            blk,
            pl.BlockSpec((tm, tc), lambda i, j: (i, a0 + j)),
            pl.BlockSpec((tm, tc), lambda i, j: (i, c0 + j)),
            blk,
            blk,
        ],
        out_specs=[blk, blk, blk, blk],
        out_shape=[out, out, out, out],
        compiler_params=_cp("parallel", "parallel"),
    )(dm, proj, proj, ya, yc)


def _conv_taps(cz, czp, i):
    czp = czp * (i > 0).astype(F32)
    h1 = czp[SUBLANES - 1:SUBLANES, :]
    h2 = czp[SUBLANES - 2:SUBLANES - 1, :]
    row = lax.broadcasted_iota(jnp.int32, cz.shape, 0)
    s1 = jnp.where(row == 0, h1, pltpu.roll(cz, 1, 0))
    s2 = jnp.where(row == 0, h2, jnp.where(row == 1, h1, pltpu.roll(cz, 2, 0)))
    return s1, s2


def _conv_fwd(proj, w8, *, z0, gb0, gc0, cw, name, tm=512, tc=512):
    t = proj.shape[0]
    tm, tc = min(tm, t), math.gcd(tc, cw, z0, gb0, gc0)
    zb, bb, cb = z0 // tc, gb0 // tc, gc0 // tc
    rb = tm // SUBLANES

    def body(z_ref, gb_ref, gc_ref, zp_ref, gcp_ref, w_ref, o_ref):
        i = pl.program_id(0)
        cz = gc_ref[...] * z_ref[...]
        s1, s2 = _conv_taps(cz, gcp_ref[...] * zp_ref[...], i)
        w = w_ref[...]
        y = w[0:1, :] * s2 + w[1:2, :] * s1 + w[2:3, :] * cz
        o_ref[...] = (gb_ref[...] * y).astype(BF)

    def cur(b0):
        return pl.BlockSpec((tm, tc), lambda i, j: (i, b0 + j))

    def prev(b0):
        return pl.BlockSpec((SUBLANES, tc), lambda i, j: (jnp.maximum(i * rb - 1, 0), b0 + j))

    return pl.pallas_call(
        body,
        name=name,
        grid=(t // tm, cw // tc),
        in_specs=[cur(zb), cur(bb), cur(cb), prev(zb), prev(cb), pl.BlockSpec((SUBLANES, tc), lambda i, j: (0, j))],
        out_specs=pl.BlockSpec((tm, tc), lambda i, j: (i, j)),
        out_shape=jax.ShapeDtypeStruct((t, cw), BF),
        compiler_params=_cp("parallel", "parallel"),
    )(proj, proj, proj, proj, proj, w8)


def _conv_bwd(proj, w8, dcy, *, z0, gb0, gc0, cw, name, tm=512, tc=512):
    t = proj.shape[0]
    tm, tc = min(tm, t), math.gcd(tc, cw, z0, gb0, gc0)
    zb, bb, cb = z0 // tc, gb0 // tc, gc0 // tc
    rb = tm // SUBLANES
    nt = t // tm

    def body(z_ref, gb_ref, gc_ref, zp_ref, gcp_ref, d_ref, dn_ref, gbn_ref, w_ref, dz_ref, dgb_ref, dgc_ref, dw_ref):
        i = pl.program_id(1)
        z = z_ref[...]
        gc = gc_ref[...]
        gb = gb_ref[...]
        cz = gc * z
        s1, s2 = _conv_taps(cz, gcp_ref[...] * zp_ref[...], i)
        w = w_ref[...]
        w0, w1, w2 = w[0:1, :], w[1:2, :], w[2:3, :]
        yc = w0 * s2 + w1 * s1 + w2 * cz
        dcyv = d_ref[...]
        dgb_ref[...] = (dcyv * yc).astype(BF)
        dyc = dcyv * gb
        dycn = dn_ref[...] * gbn_ref[...] * (i < nt - 1).astype(F32)
        n1, n2 = dycn[0:1, :], dycn[1:2, :]
        row = lax.broadcasted_iota(jnp.int32, cz.shape, 0)
        a1 = jnp.where(row == tm - 1, n1, pltpu.roll(dyc, tm - 1, 0))
        a2 = jnp.where(row == tm - 1, n2, jnp.where(row == tm - 2, n1, pltpu.roll(dyc, tm - 2, 0)))
        dcz = w2 * dyc + w1 * a1 + w0 * a2
        dz_ref[...] = (dcz * gc).astype(BF)
        dgc_ref[...] = (dcz * z).astype(BF)
        dw0 = jnp.sum(dyc * s2, axis=0, keepdims=True)
        dw1 = jnp.sum(dyc * s1, axis=0, keepdims=True)
        dw2 = jnp.sum(dyc * cz, axis=0, keepdims=True)
        r8 = lax.broadcasted_iota(jnp.int32, (SUBLANES, tc), 0)
        upd = jnp.where(r8 == 0, dw0, jnp.where(r8 == 1, dw1, jnp.where(r8 == 2, dw2, 0.0)))

        @pl.when(i == 0)
        def _():
            dw_ref[...] = jnp.zeros_like(dw_ref)

        dw_ref[...] += upd

    def cur(b0):
        return pl.BlockSpec((tm, tc), lambda j, i: (i, b0 + j))

    def prev(b0):
        return pl.BlockSpec((SUBLANES, tc), lambda j, i: (jnp.maximum(i * rb - 1, 0), b0 + j))

    def nxt(b0):
        return pl.BlockSpec((SUBLANES, tc), lambda j, i: (jnp.minimum((i + 1) * rb, t // SUBLANES - 1), b0 + j))

    blk = pl.BlockSpec((tm, tc), lambda j, i: (i, j))
    w_spec = pl.BlockSpec((SUBLANES, tc), lambda j, i: (0, j))
    out = jax.ShapeDtypeStruct((t, cw), BF)
    return pl.pallas_call(
        body,
        name=name,
        grid=(cw // tc, nt),
        in_specs=[cur(zb), cur(bb), cur(cb), prev(zb), prev(cb), blk, nxt(0), nxt(bb), w_spec],
        out_specs=[blk, blk, blk, w_spec],
        out_shape=[out, out, out, jax.ShapeDtypeStruct((SUBLANES, cw), F32)],
        compiler_params=_cp("parallel", "arbitrary"),
    )(proj, proj, proj, proj, proj, dcy, dcy, proj, w8)


def _rot_half(x):
    lane = lax.broadcasted_iota(jnp.int32, x.shape, 1)
    first = (lane % HEAD_DIM) < (HEAD_DIM // 2)
    return jnp.where(first, pltpu.roll(x, LANES - HEAD_DIM // 2, 1), pltpu.roll(x, HEAD_DIM // 2, 1))


def _rope(x, c, s):
    parts = []
    for a in range(x.shape[1] // LANES):
        xa = x[:, a * LANES:(a + 1) * LANES]
        parts.append(xa * c + _rot_half(xa) * s)
    return parts[0] if len(parts) == 1 else jnp.concatenate(parts, axis=1)


def _rope_bwd(dy, c, s):
    parts = []
    for a in range(dy.shape[1] // LANES):
        da = dy[:, a * LANES:(a + 1) * LANES]
        parts.append(da * c + _rot_half(da * s))
    return parts[0] if len(parts) == 1 else jnp.concatenate(parts, axis=1)


def _band_mask(i):
    b = WINDOW
    r = lax.broadcasted_iota(jnp.int32, (b, 2 * b), 0)
    c = lax.broadcasted_iota(jnp.int32, (b, 2 * b), 1)
    no_prev = jnp.where(i > 0, 0, 2 * b)
    return ((c < b) & (c > r + no_prev)) | ((c >= b) & ((c - b) <= r))


def _chunk(x, a):
    return x[:, a * LANES:(a + 1) * LANES]


def _kv_aligned(kp, kc, h):
    band = jnp.concatenate([_chunk(kp, h // 2), _chunk(kc, h // 2)], axis=0).astype(F32)
    swapped = pltpu.roll(band, HEAD_DIM, 1)
    return (band, swapped) if h % 2 == 0 else (swapped, band)


def _swa_fwd(proj, cosf, sinf, sinks, *, nq, name, dep=None):
    t = proj.shape[0]
    nkv = nq // Q_PER_KV
    aw, kw, b = nq * HEAD_DIM, nkv * HEAD_DIM, WINDOW
    nb = t // b
    kblk = aw // kw
    scale = HEAD_DIM ** -0.5

    def body(*refs):
        sink_ref, q_ref, kc_ref, kp_ref, vc_ref, vp_ref, cc_ref, cp_ref, sc_ref, sp_ref = refs[:10]
        o_ref, qr_ref, kr_ref = refs[-3:]
        i = pl.program_id(0)
        cc, sc, cpv, spv = cc_ref[...], sc_ref[...], cp_ref[...], sp_ref[...]
        qr = _rope(q_ref[...], cc, sc)
        kc = _rope(kc_ref[...], cc, sc)
        kp = _rope(kp_ref[...], cpv, spv)
        qr_ref[...] = qr.astype(BF)
        kr_ref[...] = kc.astype(BF)
        vc, vp = vc_ref[...], vp_ref[...]
        valid = _band_mask(i)
        lo = lax.broadcasted_iota(jnp.int32, (b, LANES), 1) < HEAD_DIM
        for a in range(nq // 2):
            h = (2 * a) // Q_PER_KV
            ks = [x.astype(BF) for x in _kv_aligned(kp, kc, h)]
            vs = [x.astype(BF) for x in _kv_aligned(vp, vc, h)]
            qa = _chunk(qr, a)
            o_par = []
            for par in range(2):
                hq = 2 * a + par
                qm = jnp.where(lo if par == 0 else ~lo, qa, 0.0).astype(BF)
                s = lax.dot_general(qm, ks[par], _DIMS["nt"], preferred_element_type=F32) * scale
                s = jnp.where(valid, s, -jnp.inf)
                sink = sink_ref[hq]
                m = jnp.maximum(jnp.max(s, axis=-1, keepdims=True), sink)
                p = jnp.exp(s - m)
                p = p / (jnp.sum(p, axis=-1, keepdims=True) + jnp.exp(sink - m))
                o_par.append(jnp.dot(p.astype(BF), vs[par], preferred_element_type=F32))
            o_ref[:, a * LANES:(a + 1) * LANES] = jnp.where(lo, o_par[0], o_par[1]).astype(BF)

    def prev_i(i):
        return jnp.maximum(i - 1, 0)

    tab_c = pl.BlockSpec((b, LANES), lambda i: (i, 0))
    tab_p = pl.BlockSpec((b, LANES), lambda i: (prev_i(i), 0))
    in_specs = [
        pl.BlockSpec(memory_space=pltpu.SMEM),
        pl.BlockSpec((b, aw), lambda i: (i, 0)),
        pl.BlockSpec((b, kw), lambda i: (i, kblk)),
        pl.BlockSpec((b, kw), lambda i: (prev_i(i), kblk)),
        pl.BlockSpec((b, kw), lambda i: (i, kblk + 1)),
        pl.BlockSpec((b, kw), lambda i: (prev_i(i), kblk + 1)),
        tab_c,
        tab_p,
        tab_c,
        tab_p,
    ]
    args = (sinks, proj, proj, proj, proj, proj, cosf, cosf, sinf, sinf)
    if dep is not None:
        in_specs.append(pl.BlockSpec(memory_space=pl.ANY))
        args += (dep,)
    return pl.pallas_call(
        body,
        name=name,
        grid=(nb,),
        in_specs=in_specs,
        out_specs=[
            pl.BlockSpec((b, aw), lambda i: (i, 0)),
            pl.BlockSpec((b, aw), lambda i: (i, 0)),
            pl.BlockSpec((b, kw), lambda i: (i, 0)),
        ],
        out_shape=[
            jax.ShapeDtypeStruct((t, aw), BF),
            jax.ShapeDtypeStruct((t, aw), BF),
            jax.ShapeDtypeStruct((t, kw), BF),
        ],
        compiler_params=_cp("parallel"),
    )(*args)


def _swa_bwd(qr, kr, proj, do, cosf, sinf, sinks, *, nq, name):
    t = proj.shape[0]
    nkv = nq // Q_PER_KV
    aw, kw, b = nq * HEAD_DIM, nkv * HEAD_DIM, WINDOW
    nb = t // b
    kblk = aw // kw
    scale = HEAD_DIM ** -0.5

    def body(sink_ref, q_ref, kc_ref, kp_ref, vc_ref, vp_ref, do_ref, cc_ref, cp_ref, sc_ref, sp_ref,
             dq_ref, dk_ref, dv_ref, ds_ref, ck_ref, cv_ref, sacc_ref):
        i = pl.program_id(0)

        @pl.when(i == 0)
        def _():
            ck_ref[...] = jnp.zeros_like(ck_ref)
            cv_ref[...] = jnp.zeros_like(cv_ref)
            sacc_ref[...] = jnp.zeros_like(sacc_ref)

        @pl.when(i < nb)
        def _():
            q = q_ref[...]
            kc, kp = kc_ref[...], kp_ref[...]
            vc, vp = vc_ref[...], vp_ref[...]
            dov = do_ref[...]
            valid = _band_mask(i)
            lane = lax.broadcasted_iota(jnp.int32, (b, LANES), 1)
            lo = lane < HEAD_DIM
            cc, sc = cc_ref[...], sc_ref[...]
            nch = kw // LANES
            dk_ch = [jnp.zeros((2 * b, LANES), F32) for _ in range(nch)]
            dv_ch = [jnp.zeros((2 * b, LANES), F32) for _ in range(nch)]
            sacc = jnp.zeros((b, LANES), F32)
            for a in range(nq // 2):
                h = (2 * a) // Q_PER_KV
                ks = [x.astype(BF) for x in _kv_aligned(kp, kc, h)]
                vs = [x.astype(BF) for x in _kv_aligned(vp, vc, h)]
                qa = _chunk(q, a).astype(F32)
                doa = _chunk(dov, a).astype(F32)
                dq_par = []
                for par in range(2):
                    hq = 2 * a + par
                    mine = lo if par == 0 else ~lo
                    qm = jnp.where(mine, qa, 0.0).astype(BF)
                    dom = jnp.where(mine, doa, 0.0).astype(BF)
                    s = lax.dot_general(qm, ks[par], _DIMS["nt"], preferred_element_type=F32) * scale
                    s = jnp.where(valid, s, -jnp.inf)
                    sink = sink_ref[hq]
                    m = jnp.maximum(jnp.max(s, axis=-1, keepdims=True), sink)
                    e = jnp.exp(s - m)
                    es = jnp.exp(sink - m)
                    zinv = 1.0 / (jnp.sum(e, axis=-1, keepdims=True) + es)
                    p = e * zinv
                    dp = lax.dot_general(dom, vs[par], _DIMS["nt"], preferred_element_type=F32)
                    delta = jnp.sum(p * dp, axis=-1, keepdims=True)
                    dsv = (p * (dp - delta) * scale).astype(BF)
                    sacc = sacc + jnp.where(lane == hq, -(es * zinv) * delta, 0.0)
                    dq_par.append(jnp.dot(dsv, ks[par], preferred_element_type=F32))
                    dkh = lax.dot_general(dsv, qm, _DIMS["tn"], preferred_element_type=F32)
                    dvh = lax.dot_general(p.astype(BF), dom, _DIMS["tn"], preferred_element_type=F32)
                    if par != h % 2:
                        dkh = pltpu.roll(dkh, HEAD_DIM, 1)
                        dvh = pltpu.roll(dvh, HEAD_DIM, 1)
                    dk_ch[h // 2] = dk_ch[h // 2] + dkh
                    dv_ch[h // 2] = dv_ch[h // 2] + dvh
                dqa = jnp.where(lo, dq_par[0], dq_par[1])
                dq_ref[:, a * LANES:(a + 1) * LANES] = _rope_bwd(dqa, cc, sc).astype(BF)
            dk = dk_ch[0] if nch == 1 else jnp.concatenate(dk_ch, axis=1)
            dv = dv_ch[0] if nch == 1 else jnp.concatenate(dv_ch, axis=1)
            dk_ref[...] = _rope_bwd(ck_ref[...] + dk[:b, :], cp_ref[...], sp_ref[...]).astype(BF)
            dv_ref[...] = (cv_ref[...] + dv[:b, :]).astype(BF)
            ck_ref[...] = dk[b:, :]
            cv_ref[...] = dv[b:, :]
            sacc_ref[...] += sacc

        @pl.when(i == nb)
        def _():
            dk_ref[...] = _rope_bwd(ck_ref[...], cp_ref[...], sp_ref[...]).astype(BF)
            dv_ref[...] = cv_ref[...].astype(BF)
            ds_ref[...] = jnp.broadcast_to(jnp.sum(sacc_ref[...], axis=0, keepdims=True), ds_ref.shape)

    def cur_i(i):
        return jnp.minimum(i, nb - 1)

    def prev_i(i):
        return jnp.clip(i - 1, 0, nb - 1)

    tab_c = pl.BlockSpec((b, LANES), lambda i: (cur_i(i), 0))
    tab_p = pl.BlockSpec((b, LANES), lambda i: (prev_i(i), 0))
    return pl.pallas_call(
        body,
        name=name,
        grid=(nb + 1,),
        in_specs=[
            pl.BlockSpec(memory_space=pltpu.SMEM),
            pl.BlockSpec((b, aw), lambda i: (cur_i(i), 0)),
            pl.BlockSpec((b, kw), lambda i: (cur_i(i), 0)),
            pl.BlockSpec((b, kw), lambda i: (prev_i(i), 0)),
            pl.BlockSpec((b, kw), lambda i: (cur_i(i), kblk + 1)),
            pl.BlockSpec((b, kw), lambda i: (prev_i(i), kblk + 1)),
            pl.BlockSpec((b, aw), lambda i: (cur_i(i), 0)),
            tab_c,
            tab_p,
            tab_c,
            tab_p,
        ],
        out_specs=[
            pl.BlockSpec((b, aw), lambda i: (cur_i(i), 0)),
            pl.BlockSpec((b, kw), lambda i: (prev_i(i), 0)),
            pl.BlockSpec((b, kw), lambda i: (prev_i(i), 0)),
            pl.BlockSpec((SUBLANES, LANES), lambda i: (0, 0)),
        ],
        out_shape=[
            jax.ShapeDtypeStruct((t, aw), BF),
            jax.ShapeDtypeStruct((t, kw), BF),
            jax.ShapeDtypeStruct((t, kw), BF),
            jax.ShapeDtypeStruct((SUBLANES, LANES), F32),
        ],
        scratch_shapes=[pltpu.VMEM((b, kw), F32), pltpu.VMEM((b, kw), F32), pltpu.VMEM((b, LANES), F32)],
        compiler_params=_cp("arbitrary"),
    )(sinks, qr, kr, kr, proj, proj, do, cosf, cosf, sinf, sinf)


def _xattn_fwd(xq, kv, *, name, tq=512):
    t, xw = xq.shape
    mtok = kv.shape[0]
    tq = min(tq, t)
    nh = xw // X_HEAD_DIM
    scale = X_HEAD_DIM ** -0.5

    def body(q_ref, kv_ref, o_ref):
        q = q_ref[...]
        kvv = kv_ref[...]
        outs = []
        for h in range(nh):
            sl = slice(h * X_HEAD_DIM, (h + 1) * X_HEAD_DIM)
            k = kvv[:, sl]
            v = kvv[:, xw + h * X_HEAD_DIM: xw + (h + 1) * X_HEAD_DIM]
            s = lax.dot_general(q[:, sl], k, _DIMS["nt"], preferred_element_type=F32) * scale
            e = jnp.exp(s - jnp.max(s, axis=-1, keepdims=True))
            p = e / jnp.sum(e, axis=-1, keepdims=True)
            outs.append(jnp.dot(p.astype(BF), v, preferred_element_type=F32))
        o_ref[...] = jnp.concatenate(outs, axis=1).astype(BF)

    return pl.pallas_call(
        body,
        name=name,
        grid=(t // tq,),
        in_specs=[pl.BlockSpec((tq, xw), lambda i: (i, 0)), pl.BlockSpec((mtok, 2 * xw), lambda i: (0, 0))],
        out_specs=pl.BlockSpec((tq, xw), lambda i: (i, 0)),
        out_shape=jax.ShapeDtypeStruct((t, xw), BF),
        compiler_params=_cp("parallel"),
    )(xq, kv)


def _xattn_bwd(xq, kv, do, *, name, tq=512):
    t, xw = xq.shape
    mtok = kv.shape[0]
    tq = min(tq, t)
    nh = xw // X_HEAD_DIM
    scale = X_HEAD_DIM ** -0.5

    def body(q_ref, kv_ref, do_ref, dq_ref, dkv_ref):
        i = pl.program_id(0)
        q = q_ref[...]
        kvv = kv_ref[...]
        dov = do_ref[...]
        dqs, dks, dvs = [], [], []
        for h in range(nh):
            sl = slice(h * X_HEAD_DIM, (h + 1) * X_HEAD_DIM)
            k = kvv[:, sl]
            v = kvv[:, xw + h * X_HEAD_DIM: xw + (h + 1) * X_HEAD_DIM]
            qh, doh = q[:, sl], dov[:, sl]
            s = lax.dot_general(qh, k, _DIMS["nt"], preferred_element_type=F32) * scale
            e = jnp.exp(s - jnp.max(s, axis=-1, keepdims=True))
            p = e / jnp.sum(e, axis=-1, keepdims=True)
            dp = lax.dot_general(doh, v, _DIMS["nt"], preferred_element_type=F32)
            delta = jnp.sum(p * dp, axis=-1, keepdims=True)
            dsv = (p * (dp - delta) * scale).astype(BF)
            dqs.append(jnp.dot(dsv, k, preferred_element_type=F32))
            dks.append(lax.dot_general(dsv, qh, _DIMS["tn"], preferred_element_type=F32))
            dvs.append(lax.dot_general(p.astype(BF), doh, _DIMS["tn"], preferred_element_type=F32))
        dq_ref[...] = jnp.concatenate(dqs, axis=1).astype(BF)

        @pl.when(i == 0)
        def _():
            dkv_ref[...] = jnp.zeros_like(dkv_ref)

        dkv_ref[...] += jnp.concatenate(dks + dvs, axis=1)

    row = pl.BlockSpec((tq, xw), lambda i: (i, 0))
    full = pl.BlockSpec((mtok, 2 * xw), lambda i: (0, 0))
    return pl.pallas_call(
        body,
        name=name,
        grid=(t // tq,),
        in_specs=[row, full, row],
        out_specs=[row, full],
        out_shape=[jax.ShapeDtypeStruct((t, xw), BF), jax.ShapeDtypeStruct((mtok, 2 * xw), F32)],
        compiler_params=_cp("arbitrary"),
    )(xq, kv, do)


def _adam_math(w, g, m, v):
    m = ADAM_B1 * m + (1.0 - ADAM_B1) * g
    v = ADAM_B2 * v + (1.0 - ADAM_B2) * (g * g)
    m_hat = m / (1.0 - ADAM_B1 ** ADAM_STEP)
    v_hat = v / (1.0 - ADAM_B2 ** ADAM_STEP)
    delta = -ADAM_LR * (m_hat / (jnp.sqrt(v_hat) + ADAM_EPS) + ADAM_WD * w)
    return delta, m, v


def _row_tile(r, c, n_arrays, budget=24 * 1024 * 1024):
    step = 2 * SUBLANES
    cap = max(step, budget // (2 * n_arrays * c * 4))
    if r <= cap:
        return r
    best = None
    for tr in range(step, cap + 1, step):
        if r % tr == 0:
            best = tr
    assert best is not None, (r, c)
    return best


def _adamw_sum(parts, own, me, w, m, v, *, name):
    _, r, c = parts.shape
    tr = _row_tile(r, c, 12)

    def body(me_ref, p_ref, own_ref, w_ref, m_ref, v_ref, g_ref, d_ref, nm_ref, nv_ref):
        mine = jnp.full((tr, c), me_ref[0], jnp.int32)
        g = None
        for s in range(N_DEV):
            term = jnp.where(mine == s, own_ref[...], p_ref[s]).astype(F32)
            g = term if g is None else g + term
        g_ref[...] = g
        d_ref[...], nm_ref[...], nv_ref[...] = _adam_math(w_ref[...], g, m_ref[...], v_ref[...])

    blk = pl.BlockSpec((tr, c), lambda i, me_ref: (i, 0))
    out = jax.ShapeDtypeStruct((r, c), F32)
    return pl.pallas_call(
        body,
        name=name,
        grid_spec=pltpu.PrefetchScalarGridSpec(
            num_scalar_prefetch=1,
            grid=(r // tr,),
            in_specs=[
                pl.BlockSpec((N_DEV, tr, c), lambda i, me_ref: (0, i, 0)),
                pl.BlockSpec((None, tr, c), lambda i, me_ref: (me_ref[0], i, 0)),
                blk, blk, blk,
            ],
            out_specs=[blk, blk, blk, blk],
        ),
        out_shape=[out, out, out, out],
        compiler_params=_cp("parallel"),
    )(me, parts, own, w, m, v)


def _adamw_small(w, g, m, v, *, name):
    def body(w_ref, g_ref, m_ref, v_ref, d_ref, nm_ref, nv_ref):
        d_ref[...], nm_ref[...], nv_ref[...] = _adam_math(w_ref[...], g_ref[...], m_ref[...], v_ref[...])

    out = jax.ShapeDtypeStruct(w.shape, F32)
    return pl.pallas_call(body, name=name, out_shape=[out, out, out])(w, g, m, v)


def _mesh_pos():
    x, y, c = lax.axis_index("x"), lax.axis_index("y"), lax.axis_index("c")
    return x, y, c


def _peer(x, y, c, mask):
    px = 1 - x if mask & 4 else x
    py = 1 - y if mask & 2 else y
    pc = 1 - c if mask & 1 else c
    return (px, py, pc), 4 * px + 2 * py + pc


_HBM = pl.BlockSpec(memory_space=pltpu.HBM)
_SEM = pl.BlockSpec(memory_space=pltpu.SEMAPHORE)
_EFFECT = pltpu.SideEffectType.DATAFLOW_SIDE_EFFECTING


def _me():
    return 4 * lax.axis_index("x") + 2 * lax.axis_index("y") + lax.axis_index("c")


def _landing(own, me):
    land = lax.empty((N_DEV,) + own.shape, own.dtype)
    return lax.dynamic_update_slice(land, own[None], (me, 0, 0))


_ALL = tuple(range(1, N_DEV))
_CHIPS = (2, 4, 6)
GATHER_DIRECT = tuple((m, None, 0, m) for m in _ALL)
SCATTER_DIRECT = tuple((m, m, 0, m) for m in _ALL)
GATHER_CHIPS = tuple((m, None, 0, m) for m in (1,) + _CHIPS)
GATHER_SIBLING = tuple((1, m, m, m ^ 1) for m in _CHIPS)


def _copy(src, land, send_sem, recv_sem, sem, x, y, c, entry, arriving):
    to, src_m, dst_m, arr_m = entry
    peer, _ = _peer(x, y, c, to)
    blk = lambda m: _peer(x, y, c, m)[1]
    return pltpu.make_async_remote_copy(
        src_ref=src if src_m is None else src.at[blk(src_m)],
        dst_ref=land.at[blk(arr_m if arriving else dst_m)],
        send_sem=send_sem.at[sem], recv_sem=recv_sem.at[sem], device_id=peer, device_id_type=MESH)


def _exchange_start(groups, plan, *, name, after=None):
    flat = [p for g in groups for p in g]
    from_land = flat[0][0] is None
    n, ng, nc = len(flat), len(groups), len(plan)
    n_buf = n if from_land else 2 * n

    def body(*refs):
        lands = refs[:n] if from_land else refs[n:2 * n]
        srcs = lands if from_land else refs[:n]
        sems = refs[n_buf + (after is not None):n_buf + (after is not None) + 2 * ng]
        token = refs[-1]
        x, y, c = _mesh_pos()
        w = 0
        for gi, g in enumerate(groups):
            for wi in range(len(g)):
                for k, entry in enumerate(plan):
                    _copy(srcs[w], lands[w], sems[2 * gi], sems[2 * gi + 1], wi * nc + k, x, y, c, entry,
                          False).start()
                w += 1
        token[...] = jnp.zeros_like(token)

    sem_shapes = []
    for g in groups:
        sem_shapes += [pltpu.SemaphoreType.DMA((len(g) * nc,))] * 2
    args = [] if from_land else [pltpu.with_memory_space_constraint(s, pltpu.HBM) for s, _ in flat]
    args += [pltpu.with_memory_space_constraint(l, pltpu.HBM) for _, l in flat]
    extra = [] if after is None else [after]
    outs = pl.pallas_call(
        body,
        name=name,
        in_specs=[_HBM] * n_buf + [pl.BlockSpec(memory_space=pl.ANY)] * len(extra),
        out_specs=[_SEM] * (2 * ng) + [_HBM] * n_buf + [pl.BlockSpec(memory_space=pltpu.VMEM)],
        out_shape=sem_shapes + [pltpu.HBM(a.shape, a.dtype) for a in args]
        + [jax.ShapeDtypeStruct((SUBLANES, LANES), F32)],
        input_output_aliases={i: 2 * ng + i for i in range(n_buf)},
        compiler_params=pltpu.CompilerParams(has_side_effects=_EFFECT),
    )(*args, *extra)
    sems, thru, token = outs[:2 * ng], outs[2 * ng:2 * ng + n_buf], outs[-1]
    res, w = [], 0
    for gi, g in enumerate(groups):
        m = len(g)
        srcs = [None] * m if from_land else list(thru[w:w + m])
        lands = list(thru[w:w + m]) if from_land else list(thru[n + w:n + w + m])
        res.append((sems[2 * gi], sems[2 * gi + 1], srcs, lands))
        w += m
    return res, token


def _exchange_wait(group, plan, after, *, name):
    send_sems, recv_sems, srcs_in, lands_in = group
    n, nc = len(lands_in), len(plan)
    from_land = srcs_in[0] is None
    n_buf = n if from_land else 2 * n

    def body(*refs):
        lands = refs[:n] if from_land else refs[n:2 * n]
        srcs = lands if from_land else refs[:n]
        send_sem, recv_sem = refs[n_buf], refs[n_buf + 1]
        x, y, c = _mesh_pos()
        for w in range(n):
            for k, entry in enumerate(plan):
                cp = _copy(srcs[w], lands[w], send_sem, recv_sem, w * nc + k, x, y, c, entry, True)
                cp.wait_send()
                cp.wait_recv()

    bufs = lands_in if from_land else srcs_in + lands_in
    outs = pl.pallas_call(
        body,
        name=name,
        in_specs=[_HBM] * n_buf + [_SEM, _SEM, pl.BlockSpec(memory_space=pl.ANY)],
        out_specs=[_HBM] * n_buf,
        out_shape=[pltpu.HBM(a.shape, a.dtype) for a in bufs],
        input_output_aliases={i: i for i in range(n_buf)},
        compiler_params=pltpu.CompilerParams(has_side_effects=_EFFECT),
    )(*bufs, send_sems, recv_sems, after)
    if from_land:
        return [None] * n, list(outs)
    return list(outs[:n]), list(outs[n:])


def _all_reduce_small(parts, rows, width, *, name):
    n = len(parts)

    def body(*refs):
        ins = refs[:n]
        o_ref, pack_ref, buf_ref, send_sems, recv_sems = refs[n:]
        x, y, c_ = _mesh_pos()
        me = 4 * x + 2 * y + c_
        pack_ref[...] = jnp.zeros_like(pack_ref)
        for ref, (arr, r0, nr) in zip(ins, parts):
            pack_ref[r0:r0 + nr, 0:arr.shape[1]] = ref[0:nr, :]
        sends, recvs = [], []
        for k in range(N_DEV - 1):
            peer, pidx = _peer(x, y, c_, k + 1)
            cp = pltpu.make_async_remote_copy(
                src_ref=pack_ref, dst_ref=buf_ref.at[me], send_sem=send_sems.at[k], recv_sem=recv_sems.at[k],
                device_id=peer, device_id_type=MESH)
            cp.start()
            sends.append(cp)
            recvs.append(pltpu.make_async_remote_copy(
                src_ref=pack_ref, dst_ref=buf_ref.at[pidx], send_sem=send_sems.at[k], recv_sem=recv_sems.at[k],
                device_id=peer, device_id_type=MESH))
        buf_ref[me] = pack_ref[...]
        for rc in recvs:
            rc.wait_recv()
        for cp in sends:
            cp.wait_send()
        acc = buf_ref[0]
        for s in range(1, N_DEV):
            acc = acc + buf_ref[s]
        o_ref[...] = acc

    vmem = pl.BlockSpec(memory_space=pltpu.VMEM)
    return pl.pallas_call(
        body,
        name=name,
        in_specs=[vmem] * n,
        out_specs=vmem,
        out_shape=jax.ShapeDtypeStruct((rows, width), F32),
        scratch_shapes=[
            pltpu.VMEM((rows, width), F32),
            pltpu.VMEM((N_DEV, rows, width), F32),
            pltpu.SemaphoreType.DMA((N_DEV - 1,)),
            pltpu.SemaphoreType.DMA((N_DEV - 1,)),
        ],
    )(*[p[0] for p in parts])


def _rope_tables(t):
    half = HEAD_DIM // 2
    inv_freq = ROPE_THETA ** (-jnp.arange(half, dtype=F32) / half)
    ang = jnp.arange(t, dtype=jnp.int32).astype(F32)[:, None] * inv_freq[None, :]
    cos, sin = jnp.cos(ang), jnp.sin(ang)
    cosf = jnp.concatenate([cos, cos, cos, cos], axis=1)
    sinf = jnp.concatenate([-sin, sin, -sin, sin], axis=1)
    return cosf, sinf


def _local_step(x, mem, target, gains, sinks, aw, cw, pre_w, get_w, put_g, dep0=None):
    t, d = x.shape
    nq = aw // HEAD_DIM
    kw = aw // Q_PER_KV
    z0 = aw + 2 * kw
    gb0, gc0 = z0 + cw, z0 + 2 * cw
    ga0 = z0 + 3 * cw
    gcm0 = ga0 + d
    cosf, sinf = _rope_tables(t)

    u1 = _rms_fwd(x, gains["g_mix"], name="rms_mix", dep=dep0)
    mem_n = _rms_fwd(mem, gains["g_mem"], name="rms_mem", dep=dep0)
    pre_w("w_in", u1)
    w_in = get_w("w_in", u1)
    proj = _mm(u1, w_in, mode="nn", tm=1024, tn=512, tk=2048, out_dtype=F32, name="mm_in")
    o_attn, q_rot, k_rot = _swa_fwd(proj, cosf, sinf, sinks, nq=nq, name="swa_fwd", dep=pre_w("conv_w8", proj))
    conv_w8 = get_w("conv_w8", o_attn)
    w_attn_proj, w_conv_proj, w_mix_out = (get_w(n, o_attn) for n in ("w_attn_proj", "w_conv_proj", "w_mix_out"))
    w_xq, w_xkv, w_xo = (get_w(n, o_attn) for n in ("w_xq", "w_xkv", "w_xo"))
    y_attn = _mm(o_attn, w_attn_proj, mode="nn", tm=1024, tn=1024, tk=1024, out_dtype=F32, name="mm_attn_proj")
    cy = _conv_fwd(proj, conv_w8, z0=z0, gb0=gb0, gc0=gc0, cw=cw, name="conv_fwd")
    y_conv = _mm(cy, w_conv_proj, mode="nn", tm=1024, tn=1024, tk=1024, out_dtype=F32, name="mm_conv_proj")
    merged = _gate_fwd(proj, y_attn, y_conv, ga0=ga0, gc0=gcm0, name="gate_fwd")
    h1 = _mm(merged, w_mix_out, mode="nn", tm=1024, tn=1024, tk=2048, out_dtype=F32, name="mm_mix_out", residual=x)
    u2 = _rms_fwd(h1, gains["g_xattn"], name="rms_xattn", dep=pre_w("w_ffn_in", h1))
    xq = _mm(u2, w_xq, mode="nn", tm=1024, tn=512, tk=2048, out_dtype=BF, name="mm_xq")
    kv = _mm(mem_n, w_xkv, mode="nn", tm=256, tn=1024, tk=2048, out_dtype=BF, name="mm_xkv")
    o_x = _xattn_fwd(xq, kv, name="xattn_fwd")
    h2 = _mm(o_x, w_xo, mode="nn", tm=1024, tn=1024, tk=512, out_dtype=F32, name="mm_xo", residual=h1)
    u3 = _rms_fwd(h2, gains["g_ffn"], name="rms_ffn")
    w_ffn_in = get_w("w_ffn_in", u3)
    hid2, act = _ffn_in_fwd(u3, w_ffn_in, name="mm_ffn_in", dep=pre_w("w_ffn_out", u3))
    w_ffn_out = get_w("w_ffn_out", act)
    h3 = _mm(act, w_ffn_out, mode="nn", tm=512, tn=1024, tk=8192, out_dtype=F32, name="mm_ffn_out", residual=h2)

    tt = 8192
    dh3, dh3b, loss_tile, dg_final = _loss_head(h3, target, gains["g_final"], name="loss_head")
    tok = put_g("w_ffn_out", _mm(act, dh3b, mode="tn", tm=512, tn=1024, tk=tt, out_dtype=BF, name="mm_dw_ffn_out"))
    dhid2 = _ffn_out_bwd(dh3b, w_ffn_out, hid2, name="mm_dact", dep=tok)
    f2 = w_ffn_in.shape[1]
    tok = put_g("w_ffn_in", _mm(u3, dhid2, mode="tn", tm=1024, tn=f2 // N_DEV, tk=tt, out_dtype=BF,
                                name="mm_dw_ffn_in", b_planes=2, stacked=True), stacked=True)
    du3 = _mm(dhid2, w_ffn_in, mode="nt", tm=1024, tn=1024, tk=2816, out_dtype=F32, name="mm_du3", dep=tok,
              a_planes=2)
    dh2, dh2b, dg_ffn = _rms_bwd(du3, h2, gains["g_ffn"], dh3, name="rms_ffn_bwd")
    put_g("w_xo", _mm(o_x, dh2b, mode="tn", tm=512, tn=d // N_DEV, tk=tt, out_dtype=BF, name="mm_dw_xo",
                      stacked=True), stacked=True)
    do_x = _mm(dh2b, w_xo, mode="nt", tm=1024, tn=512, tk=2048, out_dtype=BF, name="mm_do_x")
    dxq, dkv = _xattn_bwd(xq, kv, do_x, name="xattn_bwd")
    put_g("w_xkv", _mm(mem_n, dkv, mode="tn", tm=1024, tn=1024, tk=256, out_dtype=BF, name="mm_dw_xkv"))
    tok = put_g("w_xq", _mm(u2, dxq, mode="tn", tm=1024, tn=512, tk=tt, out_dtype=BF, name="mm_dw_xq"))
    du2 = _mm(dxq, w_xq, mode="nt", tm=1024, tn=1024, tk=512, out_dtype=F32, name="mm_du2", dep=tok)
    dmem_n = _mm(dkv, w_xkv, mode="nt", tm=256, tn=1024, tk=1024, out_dtype=F32, name="mm_dmem")
    _, _, dg_mem = _rms_bwd(dmem_n, mem, gains["g_mem"], None, name="rms_mem_bwd")
    dh1, dh1b, dg_xattn = _rms_bwd(du2, h1, gains["g_xattn"], dh2, name="rms_xattn_bwd")
    put_g("w_mix_out", _mm(merged, dh1b, mode="tn", tm=1024, tn=1024, tk=tt, out_dtype=BF, name="mm_dw_mix_out"))
    dmerged = _mm(dh1b, w_mix_out, mode="nt", tm=1024, tn=1024, tk=2048, out_dtype=F32, name="mm_dmerged")
    dya, dyc, dga, dgc = _gate_bwd(dmerged, proj, y_attn, y_conv, ga0=ga0, gc0=gcm0, name="gate_bwd")
    put_g("w_attn_proj", _mm(o_attn, dya, mode="tn", tm=1024, tn=d // N_DEV, tk=tt, out_dtype=BF,
                             name="mm_dw_attn_proj", stacked=True), stacked=True)
    do_attn = _mm(dya, w_attn_proj, mode="nt", tm=1024, tn=1024, tk=2048, out_dtype=BF, name="mm_do_attn")
    tok = put_g("w_conv_proj", _mm(cy, dyc, mode="tn", tm=1024, tn=d // N_DEV, tk=tt, out_dtype=BF,
                                   name="mm_dw_conv_proj", stacked=True), stacked=True)
    dcy = _mm(dyc, w_conv_proj, mode="nt", tm=1024, tn=1024, tk=2048, out_dtype=F32, name="mm_dcy", dep=tok)
    dz, dgb, dgcv, dconv_w8 = _conv_bwd(proj, conv_w8, dcy, z0=z0, gb0=gb0, gc0=gc0, cw=cw, name="conv_bwd")
    dq, dk, dv, dsink_tile = _swa_bwd(q_rot, k_rot, proj, do_attn, cosf, sinf, sinks, nq=nq, name="swa_bwd")
    dproj = jnp.concatenate([dq, dk, dv, dz, dgb, dgcv, dga, dgc], axis=1)
    tok = put_g("w_in", _mm(u1, dproj, mode="tn", tm=1024, tn=512, tk=tt, out_dtype=BF, name="mm_dw_in"))
    du1 = _mm(dproj, w_in, mode="nt", tm=512, tn=1024, tk=4352, out_dtype=F32, name="mm_du1", dep=tok)
    grad_x, _, dg_mix = _rms_bwd(du1, x, gains["g_mix"], dh1, name="rms_mix_bwd")

    small = {
        "g_mix": dg_mix, "g_xattn": dg_xattn, "g_mem": dg_mem, "g_ffn": dg_ffn, "g_final": dg_final,
        "attn_sinks": dsink_tile, "conv_w8": dconv_w8, "loss": loss_tile,
    }
    return grad_x, small


_COL_SHARDED = ("w_in", "w_attn_proj", "w_conv_proj", "w_xo", "w_ffn_in")
_ROW_SHARDED = ("w_mix_out", "w_xq", "w_xkv", "w_ffn_out")
_BIG = _COL_SHARDED + _ROW_SHARDED
_GAINS = ("g_mix", "g_xattn", "g_mem", "g_ffn", "g_final")
_GATHER_GROUPS = (("w_in",), ("conv_w8", "w_attn_proj", "w_conv_proj", "w_mix_out", "w_xq", "w_xkv", "w_xo"),
                  ("w_ffn_in",), ("w_ffn_out",))
_SCATTER_GROUPS = (("w_ffn_out",), ("w_ffn_in",), ("w_xo", "w_xq", "w_xkv"),
                   ("w_mix_out", "w_attn_proj", "w_conv_proj"), ("w_in",))
_WEIGHTS = ("g_mix", "w_in", "conv_w", "attn_sinks", "w_attn_proj", "w_conv_proj", "w_mix_out", "g_xattn", "g_mem",
            "w_xq", "w_xkv", "w_xo", "g_ffn", "w_ffn_in", "w_ffn_out", "g_final")


def _unstack(g, col_sharded):
    n, r, c = g.shape
    if col_sharded:
        return jnp.transpose(g, (1, 0, 2)).reshape(r, n * c)
    return g.reshape(n * r, c)


def _stack(w, col_sharded):
    r, c = w.shape
    if col_sharded:
        return jnp.transpose(w.reshape(r, N_DEV, c // N_DEV), (1, 0, 2))
    return w.reshape(N_DEV, r // N_DEV, c)


def kernel(x, mem, g_mix, w_in, conv_w, attn_sinks, w_attn_proj, w_conv_proj, w_mix_out, g_xattn, g_mem, w_xq, w_xkv, w_xo, g_ffn, w_ffn_in, w_ffn_out, g_final, loss_target, m_g_mix, m_w_in, m_conv_w, m_attn_sinks, m_w_attn_proj, m_w_conv_proj, m_w_mix_out, m_g_xattn, m_g_mem, m_w_xq, m_w_xkv, m_w_xo, m_g_ffn, m_w_ffn_in, m_w_ffn_out, m_g_final, v_g_mix, v_w_in, v_conv_w, v_attn_sinks, v_w_attn_proj, v_w_conv_proj, v_w_mix_out, v_g_xattn, v_g_mem, v_w_xq, v_w_xkv, v_w_xo, v_g_ffn, v_w_ffn_in, v_w_ffn_out, v_g_final):
    w_ = dict(g_mix=g_mix, w_in=w_in, conv_w=conv_w, attn_sinks=attn_sinks, w_attn_proj=w_attn_proj,
              w_conv_proj=w_conv_proj, w_mix_out=w_mix_out, g_xattn=g_xattn, g_mem=g_mem, w_xq=w_xq, w_xkv=w_xkv,
              w_xo=w_xo, g_ffn=g_ffn, w_ffn_in=w_ffn_in, w_ffn_out=w_ffn_out, g_final=g_final)
    m_ = dict(g_mix=m_g_mix, w_in=m_w_in, conv_w=m_conv_w, attn_sinks=m_attn_sinks, w_attn_proj=m_w_attn_proj,
              w_conv_proj=m_w_conv_proj, w_mix_out=m_w_mix_out, g_xattn=m_g_xattn, g_mem=m_g_mem, w_xq=m_w_xq,
              w_xkv=m_w_xkv, w_xo=m_w_xo, g_ffn=m_g_ffn, w_ffn_in=m_w_ffn_in, w_ffn_out=m_w_ffn_out,
              g_final=m_g_final)
    v_ = dict(g_mix=v_g_mix, w_in=v_w_in, conv_w=v_conv_w, attn_sinks=v_attn_sinks, w_attn_proj=v_w_attn_proj,
              w_conv_proj=v_w_conv_proj, w_mix_out=v_w_mix_out, g_xattn=v_g_xattn, g_mem=v_g_mem, w_xq=v_w_xq,
              w_xkv=v_w_xkv, w_xo=v_w_xo, g_ffn=v_g_ffn, w_ffn_in=v_w_ffn_in, w_ffn_out=v_w_ffn_out,
              g_final=v_g_final)
    t, d = x.shape[1], x.shape[2]
    nq = attn_sinks.shape[-1]
    cw_shard = conv_w.shape[-1]
    cw = cw_shard * N_DEV

    def two_d(a):
        return a.reshape(a.shape[-2], a.shape[-1]) if a.ndim == 3 else a.reshape(1, a.shape[-1])

    me = _me()
    col = set(_COL_SHARDED) | {"conv_w8"}

    shards = {n: two_d(w_[n]).astype(BF) for n in _BIG}
    shards["conv_w8"] = jnp.zeros((SUBLANES, cw_shard), F32).at[:3].set(two_d(conv_w))
    gathers, token = _exchange_start(
        [[(shards[n], _landing(shards[n], me)) for n in g] for g in _GATHER_GROUPS], GATHER_CHIPS,
        name="gather_start")
    passes, full = {}, {}

    def group_of(name):
        return [name in g for g in _GATHER_GROUPS].index(True)

    def pre_w(name, after):
        gi = group_of(name)
        _, lands = _exchange_wait(gathers[gi], GATHER_CHIPS, after, name="gather_wait_%d" % gi)
        started, tok = _exchange_start([[(None, land) for land in lands]], GATHER_SIBLING,
                                       name="gather_pass_%d" % gi)
        passes[gi] = started[0]
        return tok

    def get_w(name, after):
        if name not in full:
            gi = group_of(name)
            _, lands = _exchange_wait(passes[gi], GATHER_SIBLING, after, name="gather_pass_wait_%d" % gi)
            for n, land in zip(_GATHER_GROUPS[gi], lands):
                full[n] = _unstack(land, n in col)
        return full[name]

    pending, scatters = {}, []

    def put_g(name, dw, stacked=False):
        pending[name] = dw if stacked else _stack(dw, name in col)
        gi = [name in g for g in _SCATTER_GROUPS].index(True)
        group = _SCATTER_GROUPS[gi]
        if not all(n in pending for n in group):
            return None
        pairs = [(pending[n], lax.empty(pending[n].shape, pending[n].dtype)) for n in group]
        started, tok = _exchange_start([pairs], SCATTER_DIRECT, name="scatter_start_%d" % gi)
        scatters.append((gi, started[0]))
        return tok

    gains = {n: two_d(w_[n]) for n in _GAINS}
    grad_x, small = _local_step(
        x[0], mem[0], loss_target[0], gains, attn_sinks.reshape(nq), w_attn_proj.shape[-2], cw, pre_w, get_w, put_g,
        dep0=token)

    grads, deltas, new_m, new_v = {}, {}, {}, {}
    after = grad_x
    me1 = me.reshape(1).astype(jnp.int32)
    for gi, started in scatters:
        mine, parts = _exchange_wait(started, SCATTER_DIRECT, after, name="scatter_wait_%d" % gi)
        for n, own, p in zip(_SCATTER_GROUPS[gi], mine, parts):
            shape = w_[n].shape
            g, dl, nm, nv = _adamw_sum(p, own, me1, two_d(w_[n]), two_d(m_[n]), two_d(v_[n]), name="adamw_" + n)
            grads[n], deltas[n], new_m[n], new_v[n] = (a.reshape(shape) for a in (g, dl, nm, nv))
            after = g

    parts = [(small[n], i, 1) for i, n in enumerate(_GAINS)]
    parts += [(small["attn_sinks"], 5, 1), (small["loss"], 6, 1), (small["conv_w8"], 8, 3)]
    red = _all_reduce_small(parts, 2 * SUBLANES, max(d, cw), name="reduce_small")
    loss = red[6, 0]
    small_g = {n: red[i:i + 1, :d] for i, n in enumerate(_GAINS)}
    small_g["attn_sinks"] = red[5:6, :nq]
    small_g["conv_w"] = lax.dynamic_slice(red, (8, me * cw_shard), (3, cw_shard))
    for n in _GAINS + ("attn_sinks", "conv_w"):
        shape = w_[n].shape
        g = small_g[n]
        dl, nm, nv = _adamw_small(two_d(w_[n]), g, two_d(m_[n]), two_d(v_[n]), name="adamw_" + n)
        grads[n], deltas[n], new_m[n], new_v[n] = (a.reshape(shape) for a in (g, dl, nm, nv))

    return (loss, grad_x[None], *[grads[n] for n in _WEIGHTS], *[deltas[n] for n in _WEIGHTS],
            *[new_m[n] for n in _WEIGHTS], *[new_v[n] for n in _WEIGHTS])
```

```python
import functools
import math

import jax
import jax.numpy as jnp
from jax import lax
from jax.experimental import pallas as pl
from jax.experimental.pallas import tpu as pltpu

HEAD_DIM = 64
Q_PER_KV = 4
WINDOW = 128
X_HEAD_DIM = 128
ROPE_THETA = 10000.0
EPS = 1e-6
ADAM_LR = 0.001
ADAM_B1 = 0.9
ADAM_B2 = 0.999
ADAM_EPS = 1e-08
ADAM_WD = 0.01
ADAM_STEP = 10

N_DEV = 8
LANES = 128
SUBLANES = 8
VMEM_LIMIT_BYTES = 56 * 1024 * 1024
BF = jnp.bfloat16
F32 = jnp.float32
MESH = pl.DeviceIdType.MESH


def _cp(*sem):
    return pltpu.CompilerParams(dimension_semantics=sem, vmem_limit_bytes=VMEM_LIMIT_BYTES)


def _sigmoid(x):
    return 1.0 / (1.0 + jnp.exp(-x))


_DIMS = {
    "nn": (((1,), (0,)), ((), ())),
    "nt": (((1,), (1,)), ((), ())),
    "tn": (((0,), (0,)), ((), ())),
}


def _fit(dim, tile):
    if dim <= tile:
        return dim
    for t in range(tile // LANES * LANES, 0, -LANES):
        if dim % t == 0:
            return t
    return dim


def _mm(a, b, *, mode, tm, tn, tk, out_dtype, name, residual=None, dep=None, a_planes=1, b_planes=1,
        stacked=False, b_cols=None):
    if a_planes > 1:
        assert mode == "nt"
        (_, m, kp), (n, k) = a.shape, b.shape
        assert kp * a_planes == k
    elif b_planes > 1:
        assert mode == "tn"
        (k, m), (_, k2, np_) = a.shape, b.shape
        n = np_ * b_planes
        assert k == k2
    elif mode == "nn":
        (m, k), (k2, n) = a.shape, b.shape
        assert k == k2, (name, a.shape, b.shape)
    elif mode == "nt":
        (m, k), (n, k2) = a.shape, b.shape
        assert k == k2, (name, a.shape, b.shape)
    else:
        (k, m), (k2, n) = a.shape, b.shape
        assert k == k2, (name, a.shape, b.shape)
    tm, tn, tk = _fit(m, tm), _fit(n // b_planes, tn), _fit(k // a_planes, tk)
    assert m % tm == 0 and (n // b_planes) % tn == 0 and (k // a_planes) % tk == 0, (name, m, n, k, tm, tn, tk)
    j0 = 0
    if b_cols is not None:
        assert mode == "tn" and b_planes == 1 and b_cols[0] % tn == 0 and b_cols[1] % tn == 0
        j0, n = b_cols[0] // tn, b_cols[1]
    nk = k // tk
    nkp, njp = nk // a_planes, n // tn // b_planes
    if a_planes > 1:
        a_spec = pl.BlockSpec((None, tm, tk), lambda i, j, kk: (kk // nkp, i, kk % nkp))
    elif mode == "tn":
        a_spec = pl.BlockSpec((tk, tm), lambda i, j, kk: (kk, i))
    else:
        a_spec = pl.BlockSpec((tm, tk), lambda i, j, kk: (i, kk))
    if b_planes > 1:
        b_spec = pl.BlockSpec((None, tk, tn), lambda i, j, kk: (j // njp, kk, j % njp))
    elif mode == "nt":
        b_spec = pl.BlockSpec((tn, tk), lambda i, j, kk: (j, kk))
    else:
        b_spec = pl.BlockSpec((tk, tn), lambda i, j, kk: (kk, j + j0))
    if stacked:
        assert residual is None
        o_spec = pl.BlockSpec((None, tm, tn), lambda i, j, kk: (j, i, 0))
        out_shape = jax.ShapeDtypeStruct((n // tn, m, tn), out_dtype)
    else:
        o_spec = pl.BlockSpec((tm, tn), lambda i, j, kk: (i, j))
        out_shape = jax.ShapeDtypeStruct((m, n), out_dtype)
    dims = _DIMS[mode]
    has_res = residual is not None
    n_in = 2 + has_res + (dep is not None)

    def body(*refs):
        a_ref, b_ref, r_ref, o_ref = refs[0], refs[1], refs[2], refs[n_in]
        part = lax.dot_general(a_ref[...].astype(BF), b_ref[...].astype(BF), dims, preferred_element_type=F32)

        def finish(acc):
            if has_res:
                acc = r_ref[...] + acc
            o_ref[...] = acc.astype(out_dtype)

        if nk == 1:
            finish(part)
        else:
            acc_ref = refs[-1]
            kk = pl.program_id(2)

            @pl.when(kk == 0)
            def _():
                acc_ref[...] = part

            @pl.when(kk > 0)
            def _():
                acc_ref[...] += part

            @pl.when(kk == nk - 1)
            def _():
                finish(acc_ref[...])

    in_specs = [a_spec, b_spec] + ([o_spec] if has_res else [])
    args = (a, b) + ((residual,) if has_res else ())
    if dep is not None:
        in_specs.append(pl.BlockSpec(memory_space=pl.ANY))
        args += (dep,)
    return pl.pallas_call(
        body,
        name=name,
        grid=(m // tm, n // tn, nk),
        in_specs=in_specs,
        out_specs=o_spec,
        out_shape=out_shape,
        scratch_shapes=[pltpu.VMEM((tm, tn), F32)] if nk > 1 else [],
        compiler_params=_cp("parallel", "parallel", "arbitrary"),
    )(*args)


def _rms_fwd(h, g, *, name, tm=512, dep=None):
    t, d = h.shape
    tm = min(tm, t)

    def body(*refs):
        h_ref, g_ref, u_ref = refs[0], refs[1], refs[-1]
        hv = h_ref[...]
        r = lax.rsqrt(jnp.mean(hv * hv, axis=-1, keepdims=True) + EPS)
        u_ref[...] = ((hv * r) * g_ref[...]).astype(BF)

    in_specs = [pl.BlockSpec((tm, d), lambda i: (i, 0)), pl.BlockSpec((1, d), lambda i: (0, 0))]
    args = (h, g)
    if dep is not None:
        in_specs.append(pl.BlockSpec(memory_space=pl.ANY))
        args += (dep,)
    return pl.pallas_call(
        body,
        name=name,
        grid=(t // tm,),
        in_specs=in_specs,
        out_specs=pl.BlockSpec((tm, d), lambda i: (i, 0)),
        out_shape=jax.ShapeDtypeStruct((t, d), BF),
        compiler_params=_cp("parallel"),
    )(*args)


def _rms_bwd(du, h, g, dres, *, name, tm=256):
    t, d = h.shape
    tm = min(tm, t)
    want_dh = dres is not None

    def body(*refs):
        if want_dh:
            du_ref, h_ref, g_ref, dres_ref, dh_ref, dhb_ref, dg_ref = refs
        else:
            du_ref, h_ref, g_ref, dg_ref = refs
        i = pl.program_id(0)
        hv = h_ref[...]
        duv = du_ref[...]
        r = lax.rsqrt(jnp.mean(hv * hv, axis=-1, keepdims=True) + EPS)
        nv = hv * r
        if want_dh:
            gy = duv * g_ref[...]
            dh = dres_ref[...] + r * (gy - nv * jnp.mean(nv * gy, axis=-1, keepdims=True))
            dh_ref[...] = dh
            dhb_ref[...] = dh.astype(BF)

        @pl.when(i == 0)
        def _():
            dg_ref[...] = jnp.zeros_like(dg_ref)

        dg_ref[...] += jnp.sum(duv * nv, axis=0, keepdims=True)

    row = pl.BlockSpec((tm, d), lambda i: (i, 0))
    vec = pl.BlockSpec((1, d), lambda i: (0, 0))
    if want_dh:
        in_specs, args = [row, row, vec, row], (du, h, g, dres)
        out_specs = [row, row, vec]
        out_shape = [jax.ShapeDtypeStruct((t, d), F32), jax.ShapeDtypeStruct((t, d), BF),
                     jax.ShapeDtypeStruct((1, d), F32)]
    else:
        in_specs, args = [row, row, vec], (du, h, g)
        out_specs = [vec]
        out_shape = [jax.ShapeDtypeStruct((1, d), F32)]
    outs = pl.pallas_call(
        body,
        name=name,
        grid=(t // tm,),
        in_specs=in_specs,
        out_specs=out_specs,
        out_shape=out_shape,
        compiler_params=_cp("arbitrary"),
    )(*args)
    return (outs[0], outs[1], outs[2]) if want_dh else (None, None, outs[0])


def _loss_head(h, target, g, *, name, tm=256):
    t, d = h.shape
    tm = min(tm, t)

    def body(h_ref, t_ref, g_ref, dh_ref, dhb_ref, loss_ref, dg_ref):
        i = pl.program_id(0)
        hv = h_ref[...]
        gv = g_ref[...]
        r = lax.rsqrt(jnp.mean(hv * hv, axis=-1, keepdims=True) + EPS)
        nv = hv * r
        e = nv * gv - t_ref[...]
        per_tok = jnp.mean(e * e, axis=-1, keepdims=True)
        lp = 0.5 * jnp.sum(per_tok, axis=0, keepdims=True)
        dy = e * (1.0 / d)
        gy = dy * gv
        dh = r * (gy - nv * jnp.mean(nv * gy, axis=-1, keepdims=True))
        dh_ref[...] = dh
        dhb_ref[...] = dh.astype(BF)

        @pl.when(i == 0)
        def _():
            loss_ref[...] = jnp.zeros_like(loss_ref)
            dg_ref[...] = jnp.zeros_like(dg_ref)

        loss_ref[...] += jnp.broadcast_to(lp, loss_ref.shape)
        dg_ref[...] += jnp.sum(dy * nv, axis=0, keepdims=True)

    row = pl.BlockSpec((tm, d), lambda i: (i, 0))
    vec = pl.BlockSpec((1, d), lambda i: (0, 0))
    return pl.pallas_call(
        body,
        name=name,
        grid=(t // tm,),
        in_specs=[row, row, vec],
        out_specs=[row, row, pl.BlockSpec((SUBLANES, LANES), lambda i: (0, 0)), vec],
        out_shape=[
            jax.ShapeDtypeStruct((t, d), F32),
            jax.ShapeDtypeStruct((t, d), BF),
            jax.ShapeDtypeStruct((SUBLANES, LANES), F32),
            jax.ShapeDtypeStruct((1, d), F32),
        ],
        compiler_params=_cp("arbitrary"),
    )(h, target, g)


def _ffn_in_fwd(u, w, *, name, tm=1024, tn=512, dep=None):
    t, d = u.shape
    f = w.shape[1] // 2
    tm, tn = _fit(t, tm), _fit(f, tn)
    nf = f // tn

    def body(*refs):
        u_ref, wa_ref, wb_ref, hid_ref, act_ref = refs[0], refs[1], refs[2], refs[-2], refs[-1]
        uv = u_ref[...]
        a = jnp.dot(uv, wa_ref[...], preferred_element_type=F32)
        b = jnp.dot(uv, wb_ref[...], preferred_element_type=F32)
        hid_ref[0] = a.astype(BF)
        hid_ref[1] = b.astype(BF)
        act_ref[...] = ((a * _sigmoid(a)) * b).astype(BF)

    in_specs = [
        pl.BlockSpec((tm, d), lambda i, j: (i, 0)),
        pl.BlockSpec((d, tn), lambda i, j: (0, j)),
        pl.BlockSpec((d, tn), lambda i, j: (0, nf + j)),
    ]
    args = (u, w, w)
    if dep is not None:
        in_specs.append(pl.BlockSpec(memory_space=pl.ANY))
        args += (dep,)
    return pl.pallas_call(
        body,
        name=name,
        grid=(t // tm, nf),
        in_specs=in_specs,
        out_specs=[pl.BlockSpec((2, tm, tn), lambda i, j: (0, i, j)), pl.BlockSpec((tm, tn), lambda i, j: (i, j))],
        out_shape=[jax.ShapeDtypeStruct((2, t, f), BF), jax.ShapeDtypeStruct((t, f), BF)],
        compiler_params=_cp("parallel", "parallel"),
    )(*args)


def _ffn_out_bwd(dh, w_out, hid2, *, name, tm=1024, tn=512, dep=None):
    t, d = dh.shape
    f = w_out.shape[0]
    tm, tn = _fit(t, tm), _fit(f, tn)

    def body(*refs):
        dh_ref, w_ref, hid_ref, o_ref = refs[0], refs[1], refs[2], refs[-1]
        dact = lax.dot_general(dh_ref[...], w_ref[...], _DIMS["nt"], preferred_element_type=F32)
        a = hid_ref[0].astype(F32)
        b = hid_ref[1].astype(F32)
        sg = _sigmoid(a)
        o_ref[0] = (dact * b * (sg * (1.0 + a * (1.0 - sg)))).astype(BF)
        o_ref[1] = (dact * (a * sg)).astype(BF)

    pair = pl.BlockSpec((2, tm, tn), lambda i, j: (0, i, j))
    in_specs = [pl.BlockSpec((tm, d), lambda i, j: (i, 0)), pl.BlockSpec((tn, d), lambda i, j: (j, 0)), pair]
    args = (dh, w_out, hid2)
    if dep is not None:
        in_specs.append(pl.BlockSpec(memory_space=pl.ANY))
        args += (dep,)
    return pl.pallas_call(
        body,
        name=name,
        grid=(t // tm, f // tn),
        in_specs=in_specs,
        out_specs=pair,
        out_shape=jax.ShapeDtypeStruct((2, t, f), BF),
        compiler_params=_cp("parallel", "parallel"),
    )(*args)


def _gate_fwd(cy, w, proj, ya, *, ga0, gc0, name, tm=1024, tc=512):
    t, d = ya.shape
    kc = cy.shape[1]
    tm, tc = _fit(t, tm), math.gcd(tc, d, ga0, gc0)
    a0, c0 = ga0 // tc, gc0 // tc

    def body(cy_ref, w_ref, ga_ref, gc_ref, ya_ref, yc_ref, o_ref):
        yc = jnp.dot(cy_ref[...], w_ref[...], preferred_element_type=F32)
        yc_ref[...] = yc
        o_ref[...] = (_sigmoid(ga_ref[...]) * ya_ref[...] + _sigmoid(gc_ref[...]) * yc).astype(BF)

    blk = pl.BlockSpec((tm, tc), lambda i, j: (i, j))
    return pl.pallas_call(
        body,
        name=name,
        grid=(t // tm, d // tc),
        in_specs=[
            pl.BlockSpec((tm, kc), lambda i, j: (i, 0)),
            pl.BlockSpec((kc, tc), lambda i, j: (0, j)),
            pl.BlockSpec((tm, tc), lambda i, j: (i, a0 + j)),
            pl.BlockSpec((tm, tc), lambda i, j: (i, c0 + j)),
            blk,
        ],
        out_specs=[blk, blk],
        out_shape=[jax.ShapeDtypeStruct((t, d), F32), jax.ShapeDtypeStruct((t, d), BF)],
        compiler_params=_cp("parallel", "parallel"),
    )(cy, w, proj, proj, ya)


def _gate_bwd(dh, w, proj, ya, yc, *, ga0, gc0, name, tm=1024, tc=512):
    t, d = ya.shape
    tm, tc = _fit(t, tm), math.gcd(tc, d, ga0, gc0)
    a0, c0 = ga0 // tc, gc0 // tc

    def body(dh_ref, w_ref, ga_ref, gc_ref, ya_ref, yc_ref, dya_ref, dyc_ref, dga_ref, dgc_ref):
        dmv = lax.dot_general(dh_ref[...], w_ref[...], _DIMS["nt"], preferred_element_type=F32)
        sa = _sigmoid(ga_ref[...])
        sc = _sigmoid(gc_ref[...])
        dya_ref[...] = (dmv * sa).astype(BF)
        dyc_ref[...] = (dmv * sc).astype(BF)
        dga_ref[...] = (dmv * ya_ref[...] * (sa * (1.0 - sa))).astype(BF)
        dgc_ref[...] = (dmv * yc_ref[...] * (sc * (1.0 - sc))).astype(BF)

    blk = pl.BlockSpec((tm, tc), lambda i, j: (i, j))
    out = jax.ShapeDtypeStruct((t, d), BF)
    return pl.pallas_call(
        body,
        name=name,
        grid=(t // tm, d // tc),
        in_specs=[
            pl.BlockSpec((tm, d), lambda i, j: (i, 0)),
            pl.BlockSpec((tc, d), lambda i, j: (j, 0)),
            pl.BlockSpec((tm, tc), lambda i, j: (i, a0 + j)),
            pl.BlockSpec((tm, tc), lambda i, j: (i, c0 + j)),
            blk,
            blk,
        ],
        out_specs=[blk, blk, blk, blk],
        out_shape=[out, out, out, out],
        compiler_params=_cp("parallel", "parallel"),
    )(dh, w, proj, proj, ya, yc)


def _conv_taps(cz, czp, i):
    czp = czp * (i > 0).astype(F32)
    h1 = czp[SUBLANES - 1:SUBLANES, :]
    h2 = czp[SUBLANES - 2:SUBLANES - 1, :]
    row = lax.broadcasted_iota(jnp.int32, cz.shape, 0)
    s1 = jnp.where(row == 0, h1, pltpu.roll(cz, 1, 0))
    s2 = jnp.where(row == 0, h2, jnp.where(row == 1, h1, pltpu.roll(cz, 2, 0)))
    return s1, s2


def _conv_fwd(proj, w8, *, z0, gb0, gc0, cw, name, tm=512, tc=512):
    t = proj.shape[0]
    tm, tc = min(tm, t), math.gcd(tc, cw, z0, gb0, gc0)
    zb, bb, cb = z0 // tc, gb0 // tc, gc0 // tc
    rb = tm // SUBLANES

    def body(z_ref, gb_ref, gc_ref, zp_ref, gcp_ref, w_ref, o_ref):
        i = pl.program_id(0)
        cz = gc_ref[...] * z_ref[...]
        s1, s2 = _conv_taps(cz, gcp_ref[...] * zp_ref[...], i)
        w = w_ref[...]
        y = w[0:1, :] * s2 + w[1:2, :] * s1 + w[2:3, :] * cz
        o_ref[...] = (gb_ref[...] * y).astype(BF)

    def cur(b0):
        return pl.BlockSpec((tm, tc), lambda i, j: (i, b0 + j))

    def prev(b0):
        return pl.BlockSpec((SUBLANES, tc), lambda i, j: (jnp.maximum(i * rb - 1, 0), b0 + j))

    return pl.pallas_call(
        body,
        name=name,
        grid=(t // tm, cw // tc),
        in_specs=[cur(zb), cur(bb), cur(cb), prev(zb), prev(cb), pl.BlockSpec((SUBLANES, tc), lambda i, j: (0, j))],
        out_specs=pl.BlockSpec((tm, tc), lambda i, j: (i, j)),
        out_shape=jax.ShapeDtypeStruct((t, cw), BF),
        compiler_params=_cp("parallel", "parallel"),
    )(proj, proj, proj, proj, proj, w8)


def _conv_bwd(proj, w8, dcy, *, z0, gb0, gc0, cw, name, tm=512, tc=512):
    t = proj.shape[0]
    tm, tc = min(tm, t), math.gcd(tc, cw, z0, gb0, gc0)
    zb, bb, cb = z0 // tc, gb0 // tc, gc0 // tc
    rb = tm // SUBLANES
    nt = t // tm

    def body(z_ref, gb_ref, gc_ref, zp_ref, gcp_ref, d_ref, dn_ref, gbn_ref, w_ref, dz_ref, dgb_ref, dgc_ref, dw_ref):
        i = pl.program_id(1)
        z = z_ref[...]
        gc = gc_ref[...]
        gb = gb_ref[...]
        cz = gc * z
        s1, s2 = _conv_taps(cz, gcp_ref[...] * zp_ref[...], i)
        w = w_ref[...]
        w0, w1, w2 = w[0:1, :], w[1:2, :], w[2:3, :]
        yc = w0 * s2 + w1 * s1 + w2 * cz
        dcyv = d_ref[...]
        dgb_ref[...] = (dcyv * yc).astype(BF)
        dyc = dcyv * gb
        dycn = dn_ref[...] * gbn_ref[...] * (i < nt - 1).astype(F32)
        n1, n2 = dycn[0:1, :], dycn[1:2, :]
        row = lax.broadcasted_iota(jnp.int32, cz.shape, 0)
        a1 = jnp.where(row == tm - 1, n1, pltpu.roll(dyc, tm - 1, 0))
        a2 = jnp.where(row == tm - 1, n2, jnp.where(row == tm - 2, n1, pltpu.roll(dyc, tm - 2, 0)))
        dcz = w2 * dyc + w1 * a1 + w0 * a2
        dz_ref[...] = (dcz * gc).astype(BF)
        dgc_ref[...] = (dcz * z).astype(BF)
        dw0 = jnp.sum(dyc * s2, axis=0, keepdims=True)
        dw1 = jnp.sum(dyc * s1, axis=0, keepdims=True)
        dw2 = jnp.sum(dyc * cz, axis=0, keepdims=True)
        r8 = lax.broadcasted_iota(jnp.int32, (SUBLANES, tc), 0)
        upd = jnp.where(r8 == 0, dw0, jnp.where(r8 == 1, dw1, jnp.where(r8 == 2, dw2, 0.0)))

        @pl.when(i == 0)
        def _():
            dw_ref[...] = jnp.zeros_like(dw_ref)

        dw_ref[...] += upd

    def cur(b0):
        return pl.BlockSpec((tm, tc), lambda j, i: (i, b0 + j))

    def prev(b0):
        return pl.BlockSpec((SUBLANES, tc), lambda j, i: (jnp.maximum(i * rb - 1, 0), b0 + j))

    def nxt(b0):
        return pl.BlockSpec((SUBLANES, tc), lambda j, i: (jnp.minimum((i + 1) * rb, t // SUBLANES - 1), b0 + j))

    blk = pl.BlockSpec((tm, tc), lambda j, i: (i, j))
    w_spec = pl.BlockSpec((SUBLANES, tc), lambda j, i: (0, j))
    out = jax.ShapeDtypeStruct((t, cw), BF)
    return pl.pallas_call(
        body,
        name=name,
        grid=(cw // tc, nt),
        in_specs=[cur(zb), cur(bb), cur(cb), prev(zb), prev(cb), blk, nxt(0), nxt(bb), w_spec],
        out_specs=[blk, blk, blk, w_spec],
        out_shape=[out, out, out, jax.ShapeDtypeStruct((SUBLANES, cw), F32)],
        compiler_params=_cp("parallel", "arbitrary"),
    )(proj, proj, proj, proj, proj, dcy, dcy, proj, w8)


def _rot_half(x):
    lane = lax.broadcasted_iota(jnp.int32, x.shape, 1)
    first = (lane % HEAD_DIM) < (HEAD_DIM // 2)
    return jnp.where(first, pltpu.roll(x, LANES - HEAD_DIM // 2, 1), pltpu.roll(x, HEAD_DIM // 2, 1))


def _rope(x, c, s):
    parts = []
    for a in range(x.shape[1] // LANES):
        xa = x[:, a * LANES:(a + 1) * LANES]
        parts.append(xa * c + _rot_half(xa) * s)
    return parts[0] if len(parts) == 1 else jnp.concatenate(parts, axis=1)


def _rope_bwd(dy, c, s):
    parts = []
    for a in range(dy.shape[1] // LANES):
        da = dy[:, a * LANES:(a + 1) * LANES]
        parts.append(da * c + _rot_half(da * s))
    return parts[0] if len(parts) == 1 else jnp.concatenate(parts, axis=1)


def _band_mask(i):
    b = WINDOW
    r = lax.broadcasted_iota(jnp.int32, (b, 2 * b), 0)
    c = lax.broadcasted_iota(jnp.int32, (b, 2 * b), 1)
    no_prev = jnp.where(i > 0, 0, 2 * b)
    return ((c < b) & (c > r + no_prev)) | ((c >= b) & ((c - b) <= r))


def _chunk(x, a):
    return x[:, a * LANES:(a + 1) * LANES]


def _kv_aligned(kp, kc, h):
    band = jnp.concatenate([_chunk(kp, h // 2), _chunk(kc, h // 2)], axis=0).astype(F32)
    swapped = pltpu.roll(band, HEAD_DIM, 1)
    return (band, swapped) if h % 2 == 0 else (swapped, band)


def _swa_fwd(proj, cosf, sinf, sinks, *, nq, name, dep=None):
    t = proj.shape[0]
    nkv = nq // Q_PER_KV
    aw, kw, b = nq * HEAD_DIM, nkv * HEAD_DIM, WINDOW
    nb = t // b
    kblk = aw // kw
    scale = HEAD_DIM ** -0.5

    def body(*refs):
        sink_ref, q_ref, kc_ref, kp_ref, vc_ref, vp_ref, cc_ref, cp_ref, sc_ref, sp_ref = refs[:10]
        o_ref, qr_ref, kr_ref = refs[-3:]
        i = pl.program_id(0)
        cc, sc, cpv, spv = cc_ref[...], sc_ref[...], cp_ref[...], sp_ref[...]
        qr = _rope(q_ref[...], cc, sc)
        kc = _rope(kc_ref[...], cc, sc)
        kp = _rope(kp_ref[...], cpv, spv)
        qr_ref[...] = qr.astype(BF)
        kr_ref[...] = kc.astype(BF)
        vc, vp = vc_ref[...], vp_ref[...]
        valid = _band_mask(i)
        lo = lax.broadcasted_iota(jnp.int32, (b, LANES), 1) < HEAD_DIM
        for a in range(nq // 2):
            h = (2 * a) // Q_PER_KV
            ks = [x.astype(BF) for x in _kv_aligned(kp, kc, h)]
            vs = [x.astype(BF) for x in _kv_aligned(vp, vc, h)]
            qa = _chunk(qr, a)
            o_par = []
            for par in range(2):
                hq = 2 * a + par
                qm = jnp.where(lo if par == 0 else ~lo, qa, 0.0).astype(BF)
                s = lax.dot_general(qm, ks[par], _DIMS["nt"], preferred_element_type=F32) * scale
                s = jnp.where(valid, s, -jnp.inf)
                sink = sink_ref[hq]
                m = jnp.maximum(jnp.max(s, axis=-1, keepdims=True), sink)
                p = jnp.exp(s - m)
                p = p / (jnp.sum(p, axis=-1, keepdims=True) + jnp.exp(sink - m))
                o_par.append(jnp.dot(p.astype(BF), vs[par], preferred_element_type=F32))
            o_ref[:, a * LANES:(a + 1) * LANES] = jnp.where(lo, o_par[0], o_par[1]).astype(BF)

    def prev_i(i):
        return jnp.maximum(i - 1, 0)

    tab_c = pl.BlockSpec((b, LANES), lambda i: (i, 0))
    tab_p = pl.BlockSpec((b, LANES), lambda i: (prev_i(i), 0))
    in_specs = [
        pl.BlockSpec(memory_space=pltpu.SMEM),
        pl.BlockSpec((b, aw), lambda i: (i, 0)),
        pl.BlockSpec((b, kw), lambda i: (i, kblk)),
        pl.BlockSpec((b, kw), lambda i: (prev_i(i), kblk)),
        pl.BlockSpec((b, kw), lambda i: (i, kblk + 1)),
        pl.BlockSpec((b, kw), lambda i: (prev_i(i), kblk + 1)),
        tab_c,
        tab_p,
        tab_c,
        tab_p,
    ]
    args = (sinks, proj, proj, proj, proj, proj, cosf, cosf, sinf, sinf)
    if dep is not None:
        in_specs.append(pl.BlockSpec(memory_space=pl.ANY))
        args += (dep,)
    return pl.pallas_call(
        body,
        name=name,
        grid=(nb,),
        in_specs=in_specs,
        out_specs=[
            pl.BlockSpec((b, aw), lambda i: (i, 0)),
            pl.BlockSpec((b, aw), lambda i: (i, 0)),
            pl.BlockSpec((b, kw), lambda i: (i, 0)),
        ],
        out_shape=[
            jax.ShapeDtypeStruct((t, aw), BF),
            jax.ShapeDtypeStruct((t, aw), BF),
            jax.ShapeDtypeStruct((t, kw), BF),
        ],
        compiler_params=_cp("parallel"),
    )(*args)


def _swa_bwd(qr, kr, proj, do, cosf, sinf, sinks, *, nq, name):
    t = proj.shape[0]
    nkv = nq // Q_PER_KV
    aw, kw, b = nq * HEAD_DIM, nkv * HEAD_DIM, WINDOW
    nb = t // b
    kblk = aw // kw
    scale = HEAD_DIM ** -0.5

    def body(sink_ref, q_ref, kc_ref, kp_ref, vc_ref, vp_ref, do_ref, cc_ref, cp_ref, sc_ref, sp_ref,
             dq_ref, dk_ref, dv_ref, ds_ref, ck_ref, cv_ref, sacc_ref):
        i = pl.program_id(0)

        @pl.when(i == 0)
        def _():
            ck_ref[...] = jnp.zeros_like(ck_ref)
            cv_ref[...] = jnp.zeros_like(cv_ref)
            sacc_ref[...] = jnp.zeros_like(sacc_ref)

        @pl.when(i < nb)
        def _():
            q = q_ref[...]
            kc, kp = kc_ref[...], kp_ref[...]
            vc, vp = vc_ref[...], vp_ref[...]
            dov = do_ref[...]
            valid = _band_mask(i)
            lane = lax.broadcasted_iota(jnp.int32, (b, LANES), 1)
            lo = lane < HEAD_DIM
            cc, sc = cc_ref[...], sc_ref[...]
            nch = kw // LANES
            dk_ch = [jnp.zeros((2 * b, LANES), F32) for _ in range(nch)]
            dv_ch = [jnp.zeros((2 * b, LANES), F32) for _ in range(nch)]
            sacc = jnp.zeros((b, LANES), F32)
            for a in range(nq // 2):
                h = (2 * a) // Q_PER_KV
                ks = [x.astype(BF) for x in _kv_aligned(kp, kc, h)]
                vs = [x.astype(BF) for x in _kv_aligned(vp, vc, h)]
                qa = _chunk(q, a).astype(F32)
                doa = _chunk(dov, a).astype(F32)
                dq_par = []
                for par in range(2):
                    hq = 2 * a + par
                    mine = lo if par == 0 else ~lo
                    qm = jnp.where(mine, qa, 0.0).astype(BF)
                    dom = jnp.where(mine, doa, 0.0).astype(BF)
                    s = lax.dot_general(qm, ks[par], _DIMS["nt"], preferred_element_type=F32) * scale
                    s = jnp.where(valid, s, -jnp.inf)
                    sink = sink_ref[hq]
                    m = jnp.maximum(jnp.max(s, axis=-1, keepdims=True), sink)
                    e = jnp.exp(s - m)
                    es = jnp.exp(sink - m)
                    zinv = 1.0 / (jnp.sum(e, axis=-1, keepdims=True) + es)
                    p = e * zinv
                    dp = lax.dot_general(dom, vs[par], _DIMS["nt"], preferred_element_type=F32)
                    delta = jnp.sum(p * dp, axis=-1, keepdims=True)
                    dsv = (p * (dp - delta) * scale).astype(BF)
                    sacc = sacc + jnp.where(lane == hq, -(es * zinv) * delta, 0.0)
                    dq_par.append(jnp.dot(dsv, ks[par], preferred_element_type=F32))
                    dkh = lax.dot_general(dsv, qm, _DIMS["tn"], preferred_element_type=F32)
                    dvh = lax.dot_general(p.astype(BF), dom, _DIMS["tn"], preferred_element_type=F32)
                    if par != h % 2:
                        dkh = pltpu.roll(dkh, HEAD_DIM, 1)
                        dvh = pltpu.roll(dvh, HEAD_DIM, 1)
                    dk_ch[h // 2] = dk_ch[h // 2] + dkh
                    dv_ch[h // 2] = dv_ch[h // 2] + dvh
                dqa = jnp.where(lo, dq_par[0], dq_par[1])
                dq_ref[:, a * LANES:(a + 1) * LANES] = _rope_bwd(dqa, cc, sc).astype(BF)
            dk = dk_ch[0] if nch == 1 else jnp.concatenate(dk_ch, axis=1)
            dv = dv_ch[0] if nch == 1 else jnp.concatenate(dv_ch, axis=1)
            dk_ref[...] = _rope_bwd(ck_ref[...] + dk[:b, :], cp_ref[...], sp_ref[...]).astype(BF)
            dv_ref[...] = (cv_ref[...] + dv[:b, :]).astype(BF)
            ck_ref[...] = dk[b:, :]
            cv_ref[...] = dv[b:, :]
            sacc_ref[...] += sacc

        @pl.when(i == nb)
        def _():
            dk_ref[...] = _rope_bwd(ck_ref[...], cp_ref[...], sp_ref[...]).astype(BF)
            dv_ref[...] = cv_ref[...].astype(BF)
            ds_ref[...] = jnp.broadcast_to(jnp.sum(sacc_ref[...], axis=0, keepdims=True), ds_ref.shape)

    def cur_i(i):
        return jnp.minimum(i, nb - 1)

    def prev_i(i):
        return jnp.clip(i - 1, 0, nb - 1)

    tab_c = pl.BlockSpec((b, LANES), lambda i: (cur_i(i), 0))
    tab_p = pl.BlockSpec((b, LANES), lambda i: (prev_i(i), 0))
    return pl.pallas_call(
        body,
        name=name,
        grid=(nb + 1,),
        in_specs=[
            pl.BlockSpec(memory_space=pltpu.SMEM),
            pl.BlockSpec((b, aw), lambda i: (cur_i(i), 0)),
            pl.BlockSpec((b, kw), lambda i: (cur_i(i), 0)),
            pl.BlockSpec((b, kw), lambda i: (prev_i(i), 0)),
            pl.BlockSpec((b, kw), lambda i: (cur_i(i), kblk + 1)),
            pl.BlockSpec((b, kw), lambda i: (prev_i(i), kblk + 1)),
            pl.BlockSpec((b, aw), lambda i: (cur_i(i), 0)),
            tab_c,
            tab_p,
            tab_c,
            tab_p,
        ],
        out_specs=[
            pl.BlockSpec((b, aw), lambda i: (cur_i(i), 0)),
            pl.BlockSpec((b, kw), lambda i: (prev_i(i), 0)),
            pl.BlockSpec((b, kw), lambda i: (prev_i(i), 0)),
            pl.BlockSpec((SUBLANES, LANES), lambda i: (0, 0)),
        ],
        out_shape=[
            jax.ShapeDtypeStruct((t, aw), BF),
            jax.ShapeDtypeStruct((t, kw), BF),
            jax.ShapeDtypeStruct((t, kw), BF),
            jax.ShapeDtypeStruct((SUBLANES, LANES), F32),
        ],
        scratch_shapes=[pltpu.VMEM((b, kw), F32), pltpu.VMEM((b, kw), F32), pltpu.VMEM((b, LANES), F32)],
        compiler_params=_cp("arbitrary"),
    )(sinks, qr, kr, kr, proj, proj, do, cosf, cosf, sinf, sinf)


def _xattn_fwd(xq, kv, *, name, tq=512):
    t, xw = xq.shape
    mtok = kv.shape[0]
    tq = min(tq, t)
    nh = xw // X_HEAD_DIM
    scale = X_HEAD_DIM ** -0.5

    def body(q_ref, kv_ref, o_ref):
        q = q_ref[...]
        kvv = kv_ref[...]
        outs = []
        for h in range(nh):
            sl = slice(h * X_HEAD_DIM, (h + 1) * X_HEAD_DIM)
            k = kvv[:, sl]
            v = kvv[:, xw + h * X_HEAD_DIM: xw + (h + 1) * X_HEAD_DIM]
            s = lax.dot_general(q[:, sl], k, _DIMS["nt"], preferred_element_type=F32) * scale
            e = jnp.exp(s - jnp.max(s, axis=-1, keepdims=True))
            p = e / jnp.sum(e, axis=-1, keepdims=True)
            outs.append(jnp.dot(p.astype(BF), v, preferred_element_type=F32))
        o_ref[...] = jnp.concatenate(outs, axis=1).astype(BF)

    return pl.pallas_call(
        body,
        name=name,
        grid=(t // tq,),
        in_specs=[pl.BlockSpec((tq, xw), lambda i: (i, 0)), pl.BlockSpec((mtok, 2 * xw), lambda i: (0, 0))],
        out_specs=pl.BlockSpec((tq, xw), lambda i: (i, 0)),
        out_shape=jax.ShapeDtypeStruct((t, xw), BF),
        compiler_params=_cp("parallel"),
    )(xq, kv)


def _xattn_bwd(xq, kv, do, *, name, tq=512):
    t, xw = xq.shape
    mtok = kv.shape[0]
    tq = min(tq, t)
    nh = xw // X_HEAD_DIM
    scale = X_HEAD_DIM ** -0.5

    def body(q_ref, kv_ref, do_ref, dq_ref, dkv_ref):
        i = pl.program_id(0)
        q = q_ref[...]
        kvv = kv_ref[...]
        dov = do_ref[...]
        dqs, dks, dvs = [], [], []
        for h in range(nh):
            sl = slice(h * X_HEAD_DIM, (h + 1) * X_HEAD_DIM)
            k = kvv[:, sl]
            v = kvv[:, xw + h * X_HEAD_DIM: xw + (h + 1) * X_HEAD_DIM]
            qh, doh = q[:, sl], dov[:, sl]
            s = lax.dot_general(qh, k, _DIMS["nt"], preferred_element_type=F32) * scale
            e = jnp.exp(s - jnp.max(s, axis=-1, keepdims=True))
            p = e / jnp.sum(e, axis=-1, keepdims=True)
            dp = lax.dot_general(doh, v, _DIMS["nt"], preferred_element_type=F32)
            delta = jnp.sum(p * dp, axis=-1, keepdims=True)
            dsv = (p * (dp - delta) * scale).astype(BF)
            dqs.append(jnp.dot(dsv, k, preferred_element_type=F32))
            dks.append(lax.dot_general(dsv, qh, _DIMS["tn"], preferred_element_type=F32))
            dvs.append(lax.dot_general(p.astype(BF), doh, _DIMS["tn"], preferred_element_type=F32))
        dq_ref[...] = jnp.concatenate(dqs, axis=1).astype(BF)

        @pl.when(i == 0)
        def _():
            dkv_ref[...] = jnp.zeros_like(dkv_ref)

        dkv_ref[...] += jnp.concatenate(dks + dvs, axis=1)

    row = pl.BlockSpec((tq, xw), lambda i: (i, 0))
    full = pl.BlockSpec((mtok, 2 * xw), lambda i: (0, 0))
    return pl.pallas_call(
        body,
        name=name,
        grid=(t // tq,),
        in_specs=[row, full, row],
        out_specs=[row, full],
        out_shape=[jax.ShapeDtypeStruct((t, xw), BF), jax.ShapeDtypeStruct((mtok, 2 * xw), F32)],
        compiler_params=_cp("arbitrary"),
    )(xq, kv, do)


def _adam_math(w, g, m, v):
    m = ADAM_B1 * m + (1.0 - ADAM_B1) * g
    v = ADAM_B2 * v + (1.0 - ADAM_B2) * (g * g)
    m_hat = m / (1.0 - ADAM_B1 ** ADAM_STEP)
    v_hat = v / (1.0 - ADAM_B2 ** ADAM_STEP)
    delta = -ADAM_LR * (m_hat / (jnp.sqrt(v_hat) + ADAM_EPS) + ADAM_WD * w)
    return delta, m, v


def _row_tile(r, c, n_arrays, budget=24 * 1024 * 1024):
    step = 2 * SUBLANES
    cap = max(step, budget // (2 * n_arrays * c * 4))
    if r <= cap:
        return r
    best = None
    for tr in range(step, cap + 1, step):
        if r % tr == 0:
            best = tr
    assert best is not None, (r, c)
    return best


def _adamw_sum(parts, own, me, w, m, v, *, name):
    _, r, c = parts.shape
    tr = _row_tile(r, c, 12)

    def body(me_ref, p_ref, own_ref, w_ref, m_ref, v_ref, g_ref, d_ref, nm_ref, nv_ref):
        mine = jnp.full((tr, c), me_ref[0], jnp.int32)
        g = None
        for s in range(N_DEV):
            term = jnp.where(mine == s, own_ref[...], p_ref[s]).astype(F32)
            g = term if g is None else g + term
        g_ref[...] = g
        d_ref[...], nm_ref[...], nv_ref[...] = _adam_math(w_ref[...], g, m_ref[...], v_ref[...])

    blk = pl.BlockSpec((tr, c), lambda i, me_ref: (i, 0))
    out = jax.ShapeDtypeStruct((r, c), F32)
    return pl.pallas_call(
        body,
        name=name,
        grid_spec=pltpu.PrefetchScalarGridSpec(
            num_scalar_prefetch=1,
            grid=(r // tr,),
            in_specs=[
                pl.BlockSpec((N_DEV, tr, c), lambda i, me_ref: (0, i, 0)),
                pl.BlockSpec((None, tr, c), lambda i, me_ref: (me_ref[0], i, 0)),
                blk, blk, blk,
            ],
            out_specs=[blk, blk, blk, blk],
        ),
        out_shape=[out, out, out, out],
        compiler_params=_cp("parallel"),
    )(me, parts, own, w, m, v)


def _sum_partials(parts, own, me, *, name):
    _, r, c = parts.shape
    tr = _row_tile(r, c, 6)

    def body(me_ref, p_ref, own_ref, g_ref):
        mine = jnp.full((tr, c), me_ref[0], jnp.int32)
        g = None
        for s in range(N_DEV):
            term = jnp.where(mine == s, own_ref[...], p_ref[s]).astype(F32)
            g = term if g is None else g + term
        g_ref[...] = g

    return pl.pallas_call(
        body,
        name=name,
        grid_spec=pltpu.PrefetchScalarGridSpec(
            num_scalar_prefetch=1,
            grid=(r // tr,),
            in_specs=[
                pl.BlockSpec((N_DEV, tr, c), lambda i, me_ref: (0, i, 0)),
                pl.BlockSpec((None, tr, c), lambda i, me_ref: (me_ref[0], i, 0)),
            ],
            out_specs=pl.BlockSpec((tr, c), lambda i, me_ref: (i, 0)),
        ),
        out_shape=jax.ShapeDtypeStruct((r, c), F32),
        compiler_params=_cp("parallel"),
    )(me, parts, own)


def _adamw_rows(w, g, m, v, *, name):
    r, c = w.shape
    tr = _row_tile(r, c, 7)

    def body(w_ref, g_ref, m_ref, v_ref, d_ref, nm_ref, nv_ref):
        d_ref[...], nm_ref[...], nv_ref[...] = _adam_math(w_ref[...], g_ref[...], m_ref[...], v_ref[...])

    blk = pl.BlockSpec((tr, c), lambda i: (i, 0))
    out = jax.ShapeDtypeStruct((r, c), F32)
    return pl.pallas_call(
        body,
        name=name,
        grid=(r // tr,),
        in_specs=[blk, blk, blk, blk],
        out_specs=[blk, blk, blk],
        out_shape=[out, out, out],
        compiler_params=_cp("parallel"),
    )(w, g, m, v)


def _adamw_small(w, g, m, v, *, name):
    def body(w_ref, g_ref, m_ref, v_ref, d_ref, nm_ref, nv_ref):
        d_ref[...], nm_ref[...], nv_ref[...] = _adam_math(w_ref[...], g_ref[...], m_ref[...], v_ref[...])

    out = jax.ShapeDtypeStruct(w.shape, F32)
    return pl.pallas_call(body, name=name, out_shape=[out, out, out])(w, g, m, v)


def _mesh_pos():
    x, y, c = lax.axis_index("x"), lax.axis_index("y"), lax.axis_index("c")
    return x, y, c


def _peer(x, y, c, mask):
    px = 1 - x if mask & 4 else x
    py = 1 - y if mask & 2 else y
    pc = 1 - c if mask & 1 else c
    return (px, py, pc), 4 * px + 2 * py + pc


_HBM = pl.BlockSpec(memory_space=pltpu.HBM)
_SEM = pl.BlockSpec(memory_space=pltpu.SEMAPHORE)
_EFFECT = pltpu.SideEffectType.DATAFLOW_SIDE_EFFECTING


def _me():
    return 4 * lax.axis_index("x") + 2 * lax.axis_index("y") + lax.axis_index("c")


def _landing(own, me):
    land = lax.empty((N_DEV,) + own.shape, own.dtype)
    return lax.dynamic_update_slice(land, own[None], (me, 0, 0))


_ALL = tuple(range(1, N_DEV))
_CHIPS = (2, 4, 6)
GATHER_DIRECT = tuple((m, None, 0, m) for m in _ALL)
SCATTER_DIRECT = tuple((m, m, 0, m) for m in _ALL)
GATHER_CHIPS = tuple((m, None, 0, m) for m in (1,) + _CHIPS)
GATHER_SIBLING = tuple((1, m, m, m ^ 1) for m in _CHIPS)


def _copy(src, land, send_sem, recv_sem, sem, x, y, c, entry, arriving):
    to, src_m, dst_m, arr_m = entry
    peer, _ = _peer(x, y, c, to)
    blk = lambda m: _peer(x, y, c, m)[1]
    return pltpu.make_async_remote_copy(
        src_ref=src if src_m is None else src.at[blk(src_m)],
        dst_ref=land.at[blk(arr_m if arriving else dst_m)],
        send_sem=send_sem.at[sem], recv_sem=recv_sem.at[sem], device_id=peer, device_id_type=MESH)


def _exchange_start(groups, plan, *, name, after=None):
    flat = [p for g in groups for p in g]
    from_land = flat[0][0] is None
    n, ng, nc = len(flat), len(groups), len(plan)
    n_buf = n if from_land else 2 * n

    def body(*refs):
        lands = refs[:n] if from_land else refs[n:2 * n]
        srcs = lands if from_land else refs[:n]
        sems = refs[n_buf + (after is not None):n_buf + (after is not None) + 2 * ng]
        token = refs[-1]
        x, y, c = _mesh_pos()
        w = 0
        for gi, g in enumerate(groups):
            for wi in range(len(g)):
                for k, entry in enumerate(plan):
                    _copy(srcs[w], lands[w], sems[2 * gi], sems[2 * gi + 1], wi * nc + k, x, y, c, entry,
                          False).start()
                w += 1
        token[...] = jnp.zeros_like(token)

    sem_shapes = []
    for g in groups:
        sem_shapes += [pltpu.SemaphoreType.DMA((len(g) * nc,))] * 2
    args = [] if from_land else [pltpu.with_memory_space_constraint(s, pltpu.HBM) for s, _ in flat]
    args += [pltpu.with_memory_space_constraint(l, pltpu.HBM) for _, l in flat]
    extra = [] if after is None else [after]
    outs = pl.pallas_call(
        body,
        name=name,
        in_specs=[_HBM] * n_buf + [pl.BlockSpec(memory_space=pl.ANY)] * len(extra),
        out_specs=[_SEM] * (2 * ng) + [_HBM] * n_buf + [pl.BlockSpec(memory_space=pltpu.VMEM)],
        out_shape=sem_shapes + [pltpu.HBM(a.shape, a.dtype) for a in args]
        + [jax.ShapeDtypeStruct((SUBLANES, LANES), F32)],
        input_output_aliases={i: 2 * ng + i for i in range(n_buf)},
        compiler_params=pltpu.CompilerParams(has_side_effects=_EFFECT),
    )(*args, *extra)
    sems, thru, token = outs[:2 * ng], outs[2 * ng:2 * ng + n_buf], outs[-1]
    res, w = [], 0
    for gi, g in enumerate(groups):
        m = len(g)
        srcs = [None] * m if from_land else list(thru[w:w + m])
        lands = list(thru[w:w + m]) if from_land else list(thru[n + w:n + w + m])
        res.append((sems[2 * gi], sems[2 * gi + 1], srcs, lands))
        w += m
    return res, token


def _exchange_wait(group, plan, after, *, name):
    send_sems, recv_sems, srcs_in, lands_in = group
    n, nc = len(lands_in), len(plan)
    from_land = srcs_in[0] is None
    n_buf = n if from_land else 2 * n

    def body(*refs):
        lands = refs[:n] if from_land else refs[n:2 * n]
        srcs = lands if from_land else refs[:n]
        send_sem, recv_sem = refs[n_buf], refs[n_buf + 1]
        x, y, c = _mesh_pos()
        for w in range(n):
            for k, entry in enumerate(plan):
                cp = _copy(srcs[w], lands[w], send_sem, recv_sem, w * nc + k, x, y, c, entry, True)
                cp.wait_send()
                cp.wait_recv()

    bufs = lands_in if from_land else srcs_in + lands_in
    outs = pl.pallas_call(
        body,
        name=name,
        in_specs=[_HBM] * n_buf + [_SEM, _SEM, pl.BlockSpec(memory_space=pl.ANY)],
        out_specs=[_HBM] * n_buf,
        out_shape=[pltpu.HBM(a.shape, a.dtype) for a in bufs],
        input_output_aliases={i: i for i in range(n_buf)},
        compiler_params=pltpu.CompilerParams(has_side_effects=_EFFECT),
    )(*bufs, send_sems, recv_sems, after)
    if from_land:
        return [None] * n, list(outs)
    return list(outs[:n]), list(outs[n:])


def _all_reduce_small(parts, rows, width, *, name, dep=None):
    n = len(parts)

    def body(*refs):
        ins = refs[:n]
        o_ref, pack_ref, buf_ref, send_sems, recv_sems = refs[-5:]
        x, y, c_ = _mesh_pos()
        me = 4 * x + 2 * y + c_
        pack_ref[...] = jnp.zeros_like(pack_ref)
        for ref, (arr, r0, nr) in zip(ins, parts):
            pack_ref[r0:r0 + nr, 0:arr.shape[1]] = ref[0:nr, :]
        sends, recvs = [], []
        for k in range(N_DEV - 1):
            peer, pidx = _peer(x, y, c_, k + 1)
            cp = pltpu.make_async_remote_copy(
                src_ref=pack_ref, dst_ref=buf_ref.at[me], send_sem=send_sems.at[k], recv_sem=recv_sems.at[k],
                device_id=peer, device_id_type=MESH)
            cp.start()
            sends.append(cp)
            recvs.append(pltpu.make_async_remote_copy(
                src_ref=pack_ref, dst_ref=buf_ref.at[pidx], send_sem=send_sems.at[k], recv_sem=recv_sems.at[k],
                device_id=peer, device_id_type=MESH))
        buf_ref[me] = pack_ref[...]
        for rc in recvs:
            rc.wait_recv()
        for cp in sends:
            cp.wait_send()
        acc = buf_ref[0]
        for s in range(1, N_DEV):
            acc = acc + buf_ref[s]
        o_ref[...] = acc

    vmem = pl.BlockSpec(memory_space=pltpu.VMEM)
    in_specs = [vmem] * n
    args = [p[0] for p in parts]
    if dep is not None:
        in_specs.append(pl.BlockSpec(memory_space=pl.ANY))
        args.append(dep)
    return pl.pallas_call(
        body,
        name=name,
        in_specs=in_specs,
        out_specs=vmem,
        out_shape=jax.ShapeDtypeStruct((rows, width), F32),
        scratch_shapes=[
            pltpu.VMEM((rows, width), F32),
            pltpu.VMEM((N_DEV, rows, width), F32),
            pltpu.SemaphoreType.DMA((N_DEV - 1,)),
            pltpu.SemaphoreType.DMA((N_DEV - 1,)),
        ],
    )(*args)


def _rope_tables(t):
    half = HEAD_DIM // 2
    inv_freq = ROPE_THETA ** (-jnp.arange(half, dtype=F32) / half)
    ang = jnp.arange(t, dtype=jnp.int32).astype(F32)[:, None] * inv_freq[None, :]
    cos, sin = jnp.cos(ang), jnp.sin(ang)
    cosf = jnp.concatenate([cos, cos, cos, cos], axis=1)
    sinf = jnp.concatenate([-sin, sin, -sin, sin], axis=1)
    return cosf, sinf


def _local_step(x, mem, target, gains, sinks, aw, cw, pre_w, get_w, put_g, dep0=None):
    t, d = x.shape
    nq = aw // HEAD_DIM
    kw = aw // Q_PER_KV
    z0 = aw + 2 * kw
    gb0, gc0 = z0 + cw, z0 + 2 * cw
    ga0 = z0 + 3 * cw
    gcm0 = ga0 + d
    cosf, sinf = _rope_tables(t)

    u1 = _rms_fwd(x, gains["g_mix"], name="rms_mix", dep=dep0)
    mem_n = _rms_fwd(mem, gains["g_mem"], name="rms_mem", dep=dep0)
    pre_w("w_in", u1)
    w_in_t = get_w("w_in", u1)
    proj = _mm(u1, w_in_t, mode="nt", tm=1024, tn=512, tk=2048, out_dtype=F32, name="mm_in")
    o_attn, q_rot, k_rot = _swa_fwd(proj, cosf, sinf, sinks, nq=nq, name="swa_fwd", dep=pre_w("conv_w8", proj))
    conv_w8 = get_w("conv_w8", o_attn)
    w_attn_proj, w_conv_proj, w_mix_out = (get_w(n, o_attn) for n in ("w_attn_proj", "w_conv_proj", "w_mix_out"))
    w_xq, w_xkv, w_xo = (get_w(n, o_attn) for n in ("w_xq", "w_xkv", "w_xo"))
    y_attn = _mm(o_attn, w_attn_proj, mode="nn", tm=1024, tn=1024, tk=1024, out_dtype=F32, name="mm_attn_proj")
    cy = _conv_fwd(proj, conv_w8, z0=z0, gb0=gb0, gc0=gc0, cw=cw, name="conv_fwd")
    y_conv, merged = _gate_fwd(cy, w_conv_proj, proj, y_attn, ga0=ga0, gc0=gcm0, name="mm_conv_proj")
    h1 = _mm(merged, w_mix_out, mode="nn", tm=1024, tn=1024, tk=2048, out_dtype=F32, name="mm_mix_out", residual=x)
    u2 = _rms_fwd(h1, gains["g_xattn"], name="rms_xattn", dep=pre_w("w_ffn_in", h1))
    xq = _mm(u2, w_xq, mode="nn", tm=1024, tn=512, tk=2048, out_dtype=BF, name="mm_xq")
    kv = _mm(mem_n, w_xkv, mode="nn", tm=256, tn=1024, tk=2048, out_dtype=BF, name="mm_xkv")
    o_x = _xattn_fwd(xq, kv, name="xattn_fwd")
    h2 = _mm(o_x, w_xo, mode="nn", tm=1024, tn=1024, tk=512, out_dtype=F32, name="mm_xo", residual=h1)
    u3 = _rms_fwd(h2, gains["g_ffn"], name="rms_ffn")
    w_ffn_in = get_w("w_ffn_in", xq)
    hid2, act = _ffn_in_fwd(u3, w_ffn_in, name="mm_ffn_in", dep=pre_w("w_ffn_out", u3))
    w_ffn_out = get_w("w_ffn_out", act)
    h3 = _mm(act, w_ffn_out, mode="nn", tm=512, tn=1024, tk=8192, out_dtype=F32, name="mm_ffn_out", residual=h2)

    tt = 8192
    dh3, dh3b, loss_tile, dg_final = _loss_head(h3, target, gains["g_final"], name="loss_head")
    tok = put_g("w_ffn_out", _mm(act, dh3b, mode="tn", tm=512, tn=1024, tk=tt, out_dtype=BF, name="mm_dw_ffn_out"))
    dhid2 = _ffn_out_bwd(dh3b, w_ffn_out, hid2, name="mm_dact", dep=tok)
    f2 = w_ffn_in.shape[1]
    tok = put_g("w_ffn_in", _mm(u3, dhid2, mode="tn", tm=1024, tn=f2 // N_DEV, tk=tt, out_dtype=BF,
                                name="mm_dw_ffn_in", b_planes=2, stacked=True), stacked=True)
    du3 = _mm(dhid2, w_ffn_in, mode="nt", tm=1024, tn=1024, tk=2816, out_dtype=F32, name="mm_du3", dep=tok,
              a_planes=2)
    dh2, dh2b, dg_ffn = _rms_bwd(du3, h2, gains["g_ffn"], dh3, name="rms_ffn_bwd")
    put_g("w_xo", _mm(o_x, dh2b, mode="tn", tm=512, tn=d // N_DEV, tk=tt, out_dtype=BF, name="mm_dw_xo",
                      stacked=True), stacked=True)
    do_x = _mm(dh2b, w_xo, mode="nt", tm=1024, tn=512, tk=2048, out_dtype=BF, name="mm_do_x")
    dxq, dkv = _xattn_bwd(xq, kv, do_x, name="xattn_bwd")
    put_g("w_xkv", _mm(mem_n, dkv, mode="tn", tm=1024, tn=1024, tk=256, out_dtype=BF, name="mm_dw_xkv"))
    tok = put_g("w_xq", _mm(u2, dxq, mode="tn", tm=1024, tn=512, tk=tt, out_dtype=BF, name="mm_dw_xq"))
    du2 = _mm(dxq, w_xq, mode="nt", tm=1024, tn=1024, tk=512, out_dtype=F32, name="mm_du2", dep=tok)
    dmem_n = _mm(dkv, w_xkv, mode="nt", tm=256, tn=1024, tk=1024, out_dtype=F32, name="mm_dmem")
    _, _, dg_mem = _rms_bwd(dmem_n, mem, gains["g_mem"], None, name="rms_mem_bwd")
    dh1, dh1b, dg_xattn = _rms_bwd(du2, h1, gains["g_xattn"], dh2, name="rms_xattn_bwd")
    put_g("w_mix_out", _mm(merged, dh1b, mode="tn", tm=1024, tn=1024, tk=tt, out_dtype=BF, name="mm_dw_mix_out"))
    dya, dyc, dga, dgc = _gate_bwd(dh1b, w_mix_out, proj, y_attn, y_conv, ga0=ga0, gc0=gcm0, name="mm_dmerged")
    put_g("w_attn_proj", _mm(o_attn, dya, mode="tn", tm=1024, tn=d // N_DEV, tk=tt, out_dtype=BF,
                             name="mm_dw_attn_proj", stacked=True), stacked=True)
    do_attn = _mm(dya, w_attn_proj, mode="nt", tm=1024, tn=1024, tk=2048, out_dtype=BF, name="mm_do_attn")
    tok = put_g("w_conv_proj", _mm(cy, dyc, mode="tn", tm=1024, tn=d // N_DEV, tk=tt, out_dtype=BF,
                                   name="mm_dw_conv_proj", stacked=True), stacked=True)
    dcy = _mm(dyc, w_conv_proj, mode="nt", tm=1024, tn=1024, tk=2048, out_dtype=F32, name="mm_dcy", dep=tok)
    dz, dgb, dgcv, dconv_w8 = _conv_bwd(proj, conv_w8, dcy, z0=z0, gb0=gb0, gc0=gc0, cw=cw, name="conv_bwd")
    dq, dk, dv, dsink_tile = _swa_bwd(q_rot, k_rot, proj, do_attn, cosf, sinf, sinks, nq=nq, name="swa_bwd")
    dproj = jnp.concatenate([dq, dk, dv, dz, dgb, dgcv, dga, dgc], axis=1)
    for hi in range(2):
        tok = put_g("w_in_%d" % hi, _mm(dproj, u1, mode="tn", tm=512, tn=d // 2, tk=tt, out_dtype=BF,
                                        name="mm_dw_in_%d" % hi, b_cols=(hi * (d // 2), d // 2), dep=tok))
    du1 = _mm(dproj, w_in_t, mode="nn", tm=512, tn=1024, tk=4352, out_dtype=F32, name="mm_du1", dep=tok)
    grad_x, _, dg_mix = _rms_bwd(du1, x, gains["g_mix"], dh1, name="rms_mix_bwd")

    small = {
        "g_mix": dg_mix, "g_xattn": dg_xattn, "g_mem": dg_mem, "g_ffn": dg_ffn, "g_final": dg_final,
        "attn_sinks": dsink_tile, "conv_w8": dconv_w8, "loss": loss_tile,
    }
    return grad_x, small


_COL_SHARDED = ("w_in", "w_attn_proj", "w_conv_proj", "w_xo", "w_ffn_in")
_ROW_SHARDED = ("w_mix_out", "w_xq", "w_xkv", "w_ffn_out")
_BIG = _COL_SHARDED + _ROW_SHARDED
_GAINS = ("g_mix", "g_xattn", "g_mem", "g_ffn", "g_final")
_GATHER_GROUPS = (("w_in",), ("conv_w8", "w_attn_proj", "w_conv_proj", "w_mix_out", "w_xq", "w_xkv", "w_xo"),
                  ("w_ffn_in",), ("w_ffn_out",))
_SCATTER_GROUPS = (("w_ffn_out",), ("w_ffn_in",), ("w_xo", "w_xq", "w_xkv"),
                   ("w_mix_out", "w_attn_proj", "w_conv_proj"), ("w_in_0",), ("w_in_1",))
_WEIGHTS = ("g_mix", "w_in", "conv_w", "attn_sinks", "w_attn_proj", "w_conv_proj", "w_mix_out", "g_xattn", "g_mem",
            "w_xq", "w_xkv", "w_xo", "g_ffn", "w_ffn_in", "w_ffn_out", "g_final")


def _unstack(g, col_sharded):
    n, r, c = g.shape
    if col_sharded:
        return jnp.transpose(g, (1, 0, 2)).reshape(r, n * c)
    return g.reshape(n * r, c)


def _stack(w, col_sharded):
    r, c = w.shape
    if col_sharded:
        return jnp.transpose(w.reshape(r, N_DEV, c // N_DEV), (1, 0, 2))
    return w.reshape(N_DEV, r // N_DEV, c)


def kernel(x, mem, g_mix, w_in, conv_w, attn_sinks, w_attn_proj, w_conv_proj, w_mix_out, g_xattn, g_mem, w_xq, w_xkv, w_xo, g_ffn, w_ffn_in, w_ffn_out, g_final, loss_target, m_g_mix, m_w_in, m_conv_w, m_attn_sinks, m_w_attn_proj, m_w_conv_proj, m_w_mix_out, m_g_xattn, m_g_mem, m_w_xq, m_w_xkv, m_w_xo, m_g_ffn, m_w_ffn_in, m_w_ffn_out, m_g_final, v_g_mix, v_w_in, v_conv_w, v_attn_sinks, v_w_attn_proj, v_w_conv_proj, v_w_mix_out, v_g_xattn, v_g_mem, v_w_xq, v_w_xkv, v_w_xo, v_g_ffn, v_w_ffn_in, v_w_ffn_out, v_g_final):
    w_ = dict(g_mix=g_mix, w_in=w_in, conv_w=conv_w, attn_sinks=attn_sinks, w_attn_proj=w_attn_proj,
              w_conv_proj=w_conv_proj, w_mix_out=w_mix_out, g_xattn=g_xattn, g_mem=g_mem, w_xq=w_xq, w_xkv=w_xkv,
              w_xo=w_xo, g_ffn=g_ffn, w_ffn_in=w_ffn_in, w_ffn_out=w_ffn_out, g_final=g_final)
    m_ = dict(g_mix=m_g_mix, w_in=m_w_in, conv_w=m_conv_w, attn_sinks=m_attn_sinks, w_attn_proj=m_w_attn_proj,
              w_conv_proj=m_w_conv_proj, w_mix_out=m_w_mix_out, g_xattn=m_g_xattn, g_mem=m_g_mem, w_xq=m_w_xq,
              w_xkv=m_w_xkv, w_xo=m_w_xo, g_ffn=m_g_ffn, w_ffn_in=m_w_ffn_in, w_ffn_out=m_w_ffn_out,
              g_final=m_g_final)
    v_ = dict(g_mix=v_g_mix, w_in=v_w_in, conv_w=v_conv_w, attn_sinks=v_attn_sinks, w_attn_proj=v_w_attn_proj,
              w_conv_proj=v_w_conv_proj, w_mix_out=v_w_mix_out, g_xattn=v_g_xattn, g_mem=v_g_mem, w_xq=v_w_xq,
              w_xkv=v_w_xkv, w_xo=v_w_xo, g_ffn=v_g_ffn, w_ffn_in=v_w_ffn_in, w_ffn_out=v_w_ffn_out,
              g_final=v_g_final)
    t, d = x.shape[1], x.shape[2]
    nq = attn_sinks.shape[-1]
    cw_shard = conv_w.shape[-1]
    cw = cw_shard * N_DEV

    def two_d(a):
        return a.reshape(a.shape[-2], a.shape[-1]) if a.ndim == 3 else a.reshape(1, a.shape[-1])

    me = _me()
    col = set(_COL_SHARDED) | {"conv_w8"}

    shards = {n: two_d(w_[n]).astype(BF) for n in _BIG}
    shards["w_in"] = shards["w_in"].T
    col = col - {"w_in"}
    shards["conv_w8"] = jnp.zeros((SUBLANES, cw_shard), F32).at[:3].set(two_d(conv_w))
    first, token = _exchange_start(
        [[(shards[n], _landing(shards[n], me)) for n in _GATHER_GROUPS[0]]], GATHER_CHIPS, name="gather_start_0")
    rest, token = _exchange_start(
        [[(shards[n], _landing(shards[n], me)) for n in g] for g in _GATHER_GROUPS[1:]], GATHER_CHIPS,
        name="gather_start_1", after=token)
    gathers = first + rest
    passes, full = {}, {}

    def group_of(name):
        return [name in g for g in _GATHER_GROUPS].index(True)

    def pre_w(name, after):
        gi = group_of(name)
        _, lands = _exchange_wait(gathers[gi], GATHER_CHIPS, after, name="gather_wait_%d" % gi)
        started, tok = _exchange_start([[(None, land) for land in lands]], GATHER_SIBLING,
                                       name="gather_pass_%d" % gi)
        passes[gi] = started[0]
        return tok

    def get_w(name, after):
        if name not in full:
            gi = group_of(name)
            _, lands = _exchange_wait(passes[gi], GATHER_SIBLING, after, name="gather_pass_wait_%d" % gi)
            for n, land in zip(_GATHER_GROUPS[gi], lands):
                full[n] = _unstack(land, n in col)
        return full[name]

    pending, scatters = {}, []

    def put_g(name, dw, stacked=False):
        pending[name] = dw if stacked else _stack(dw, name in col)
        gi = [name in g for g in _SCATTER_GROUPS].index(True)
        group = _SCATTER_GROUPS[gi]
        if not all(n in pending for n in group):
            return None
        pairs = [(pending[n], lax.empty(pending[n].shape, pending[n].dtype)) for n in group]
        started, tok = _exchange_start([pairs], SCATTER_DIRECT, name="scatter_start_%d" % gi)
        scatters.append((gi, started[0]))
        return tok

    gains = {n: two_d(w_[n]) for n in _GAINS}
    grad_x, small = _local_step(
        x[0], mem[0], loss_target[0], gains, attn_sinks.reshape(nq), w_attn_proj.shape[-2], cw, pre_w, get_w, put_g,
        dep0=token)

    grads, deltas, new_m, new_v = {}, {}, {}, {}
    me1 = me.reshape(1).astype(jnp.int32)
    after, halves = grad_x, []
    for gi, started in scatters:
        mine, parts = _exchange_wait(started, SCATTER_DIRECT, after, name="scatter_wait_%d" % gi)
        for n, own, p in zip(_SCATTER_GROUPS[gi], mine, parts):
            if n.startswith("w_in_"):
                halves.append(_sum_partials(p, own, me1, name="sum_" + n).T)
                after = halves[-1]
                if len(halves) < 2:
                    continue
                n, g = "w_in", jnp.concatenate(halves, axis=0)
                dl, nm, nv = _adamw_rows(two_d(w_[n]), g, two_d(m_[n]), two_d(v_[n]), name="adamw_" + n)
            else:
                g, dl, nm, nv = _adamw_sum(p, own, me1, two_d(w_[n]), two_d(m_[n]), two_d(v_[n]), name="adamw_" + n)
            shape = w_[n].shape
            grads[n], deltas[n], new_m[n], new_v[n] = (a.reshape(shape) for a in (g, dl, nm, nv))
            after = g

    parts = [(small[n], i, 1) for i, n in enumerate(_GAINS)]
    parts += [(small["attn_sinks"], 5, 1), (small["loss"], 6, 1), (small["conv_w8"], 8, 3)]
    red = _all_reduce_small(parts, 2 * SUBLANES, max(d, cw), name="reduce_small", dep=after)
    loss = red[6, 0]
    small_g = {n: red[i:i + 1, :d] for i, n in enumerate(_GAINS)}
    small_g["attn_sinks"] = red[5:6, :nq]
    small_g["conv_w"] = lax.dynamic_slice(red, (8, me * cw_shard), (3, cw_shard))
    for n in _GAINS + ("attn_sinks", "conv_w"):
        shape = w_[n].shape
        g = small_g[n]
        dl, nm, nv = _adamw_small(two_d(w_[n]), g, two_d(m_[n]), two_d(v_[n]), name="adamw_" + n)
        grads[n], deltas[n], new_m[n], new_v[n] = (a.reshape(shape) for a in (g, dl, nm, nv))

    return (loss, grad_x[None], *[grads[n] for n in _WEIGHTS], *[deltas[n] for n in _WEIGHTS],
            *[new_m[n] for n in _WEIGHTS], *[new_v[n] for n in _WEIGHTS])
```

```python
import functools
import math

import jax
import jax.numpy as jnp
from jax import lax
from jax.experimental import pallas as pl
from jax.experimental.pallas import tpu as pltpu

HEAD_DIM = 64
Q_PER_KV = 4
WINDOW = 128
X_HEAD_DIM = 128
ROPE_THETA = 10000.0
EPS = 1e-6
ADAM_LR = 0.001
ADAM_B1 = 0.9
ADAM_B2 = 0.999
ADAM_EPS = 1e-08
ADAM_WD = 0.01
ADAM_STEP = 10

N_DEV = 8
LANES = 128
SUBLANES = 8
VMEM_LIMIT_BYTES = 56 * 1024 * 1024
BF = jnp.bfloat16
F32 = jnp.float32
MESH = pl.DeviceIdType.MESH


def _cp(*sem):
    return pltpu.CompilerParams(dimension_semantics=sem, vmem_limit_bytes=VMEM_LIMIT_BYTES)


def _sigmoid(x):
    return 1.0 / (1.0 + jnp.exp(-x))


_DIMS = {
    "nn": (((1,), (0,)), ((), ())),
    "nt": (((1,), (1,)), ((), ())),
    "tn": (((0,), (0,)), ((), ())),
}


def _fit(dim, tile):
    if dim <= tile:
        return dim
    for t in range(tile // LANES * LANES, 0, -LANES):
        if dim % t == 0:
            return t
    return dim


def _mm(a, b, *, mode, tm, tn, tk, out_dtype, name, residual=None, dep=None, a_planes=1, b_planes=1,
        stacked=False, b_cols=None, a_cols=None):
    if a_planes > 1:
        assert mode == "nt"
        (_, m, kp), (n, k) = a.shape, b.shape
        assert kp * a_planes == k
    elif b_planes > 1:
        assert mode == "tn"
        (k, m), (_, k2, np_) = a.shape, b.shape
        n = np_ * b_planes
        assert k == k2
    elif mode == "nn":
        (m, k), (k2, n) = a.shape, b.shape
        assert k == k2, (name, a.shape, b.shape)
    elif mode == "nt":
        (m, k), (n, k2) = a.shape, b.shape
        if a_cols is not None:
            k = a_cols[1]
        assert k == k2, (name, a.shape, b.shape)
    else:
        (k, m), (k2, n) = a.shape, b.shape
        assert k == k2, (name, a.shape, b.shape)
    tm, tn, tk = _fit(m, tm), _fit(n // b_planes, tn), _fit(k // a_planes, tk)
    assert m % tm == 0 and (n // b_planes) % tn == 0 and (k // a_planes) % tk == 0, (name, m, n, k, tm, tn, tk)
    ka0 = 0
    if a_cols is not None:
        assert mode == "nt" and a_planes == 1 and a_cols[0] % tk == 0
        ka0 = a_cols[0] // tk
    j0 = 0
    if b_cols is not None:
        assert mode == "tn" and b_planes == 1 and b_cols[0] % tn == 0 and b_cols[1] % tn == 0
        j0, n = b_cols[0] // tn, b_cols[1]
    nk = k // tk
    nkp, njp = nk // a_planes, n // tn // b_planes
    if a_planes > 1:
        a_spec = pl.BlockSpec((None, tm, tk), lambda i, j, kk: (kk // nkp, i, kk % nkp))
    elif mode == "tn":
        a_spec = pl.BlockSpec((tk, tm), lambda i, j, kk: (kk, i))
    else:
        a_spec = pl.BlockSpec((tm, tk), lambda i, j, kk: (i, kk + ka0))
    if b_planes > 1:
        b_spec = pl.BlockSpec((None, tk, tn), lambda i, j, kk: (j // njp, kk, j % njp))
    elif mode == "nt":
        b_spec = pl.BlockSpec((tn, tk), lambda i, j, kk: (j, kk))
    else:
        b_spec = pl.BlockSpec((tk, tn), lambda i, j, kk: (kk, j + j0))
    if stacked:
        assert residual is None
        o_spec = pl.BlockSpec((None, tm, tn), lambda i, j, kk: (j, i, 0))
        out_shape = jax.ShapeDtypeStruct((n // tn, m, tn), out_dtype)
    else:
        o_spec = pl.BlockSpec((tm, tn), lambda i, j, kk: (i, j))
        out_shape = jax.ShapeDtypeStruct((m, n), out_dtype)
    dims = _DIMS[mode]
    has_res = residual is not None
    n_in = 2 + has_res + (dep is not None)

    def body(*refs):
        a_ref, b_ref, r_ref, o_ref = refs[0], refs[1], refs[2], refs[n_in]
        part = lax.dot_general(a_ref[...].astype(BF), b_ref[...].astype(BF), dims, preferred_element_type=F32)

        def finish(acc):
            if has_res:
                acc = r_ref[...] + acc
            o_ref[...] = acc.astype(out_dtype)

        if nk == 1:
            finish(part)
        else:
            acc_ref = refs[-1]
            kk = pl.program_id(2)

            @pl.when(kk == 0)
            def _():
                acc_ref[...] = part

            @pl.when(kk > 0)
            def _():
                acc_ref[...] += part

            @pl.when(kk == nk - 1)
            def _():
                finish(acc_ref[...])

    in_specs = [a_spec, b_spec] + ([o_spec] if has_res else [])
    args = (a, b) + ((residual,) if has_res else ())
    if dep is not None:
        in_specs.append(pl.BlockSpec(memory_space=pl.ANY))
        args += (dep,)
    return pl.pallas_call(
        body,
        name=name,
        grid=(m // tm, n // tn, nk),
        in_specs=in_specs,
        out_specs=o_spec,
        out_shape=out_shape,
        scratch_shapes=[pltpu.VMEM((tm, tn), F32)] if nk > 1 else [],
        compiler_params=_cp("parallel", "parallel", "arbitrary"),
    )(*args)


def _rms_fwd(h, g, *, name, tm=512, dep=None):
    t, d = h.shape
    tm = min(tm, t)

    def body(*refs):
        h_ref, g_ref, u_ref = refs[0], refs[1], refs[-1]
        hv = h_ref[...]
        r = lax.rsqrt(jnp.mean(hv * hv, axis=-1, keepdims=True) + EPS)
        u_ref[...] = ((hv * r) * g_ref[...]).astype(BF)

    in_specs = [pl.BlockSpec((tm, d), lambda i: (i, 0)), pl.BlockSpec((1, d), lambda i: (0, 0))]
    args = (h, g)
    if dep is not None:
        in_specs.append(pl.BlockSpec(memory_space=pl.ANY))
        args += (dep,)
    return pl.pallas_call(
        body,
        name=name,
        grid=(t // tm,),
        in_specs=in_specs,
        out_specs=pl.BlockSpec((tm, d), lambda i: (i, 0)),
        out_shape=jax.ShapeDtypeStruct((t, d), BF),
        compiler_params=_cp("parallel"),
    )(*args)


def _rms_bwd(du, h, g, dres, *, name, tm=256):
    t, d = h.shape
    tm = min(tm, t)
    want_dh = dres is not None

    def body(*refs):
        if want_dh:
            du_ref, h_ref, g_ref, dres_ref, dh_ref, dhb_ref, dg_ref = refs
        else:
            du_ref, h_ref, g_ref, dg_ref = refs
        i = pl.program_id(0)
        hv = h_ref[...]
        duv = du_ref[...]
        r = lax.rsqrt(jnp.mean(hv * hv, axis=-1, keepdims=True) + EPS)
        nv = hv * r
        if want_dh:
            gy = duv * g_ref[...]
            dh = dres_ref[...] + r * (gy - nv * jnp.mean(nv * gy, axis=-1, keepdims=True))
            dh_ref[...] = dh
            dhb_ref[...] = dh.astype(BF)

        @pl.when(i == 0)
        def _():
            dg_ref[...] = jnp.zeros_like(dg_ref)

        dg_ref[...] += jnp.sum(duv * nv, axis=0, keepdims=True)

    row = pl.BlockSpec((tm, d), lambda i: (i, 0))
    vec = pl.BlockSpec((1, d), lambda i: (0, 0))
    if want_dh:
        in_specs, args = [row, row, vec, row], (du, h, g, dres)
        out_specs = [row, row, vec]
        out_shape = [jax.ShapeDtypeStruct((t, d), F32), jax.ShapeDtypeStruct((t, d), BF),
                     jax.ShapeDtypeStruct((1, d), F32)]
    else:
        in_specs, args = [row, row, vec], (du, h, g)
        out_specs = [vec]
        out_shape = [jax.ShapeDtypeStruct((1, d), F32)]
    outs = pl.pallas_call(
        body,
        name=name,
        grid=(t // tm,),
        in_specs=in_specs,
        out_specs=out_specs,
        out_shape=out_shape,
        compiler_params=_cp("arbitrary"),
    )(*args)
    return (outs[0], outs[1], outs[2]) if want_dh else (None, None, outs[0])


def _loss_head(h, target, g, *, name, tm=256):
    t, d = h.shape
    tm = min(tm, t)

    def body(h_ref, t_ref, g_ref, dh_ref, dhb_ref, loss_ref, dg_ref):
        i = pl.program_id(0)
        hv = h_ref[...]
        gv = g_ref[...]
        r = lax.rsqrt(jnp.mean(hv * hv, axis=-1, keepdims=True) + EPS)
        nv = hv * r
        e = nv * gv - t_ref[...]
        per_tok = jnp.mean(e * e, axis=-1, keepdims=True)
        lp = 0.5 * jnp.sum(per_tok, axis=0, keepdims=True)
        dy = e * (1.0 / d)
        gy = dy * gv
        dh = r * (gy - nv * jnp.mean(nv * gy, axis=-1, keepdims=True))
        dh_ref[...] = dh
        dhb_ref[...] = dh.astype(BF)

        @pl.when(i == 0)
        def _():
            loss_ref[...] = jnp.zeros_like(loss_ref)
            dg_ref[...] = jnp.zeros_like(dg_ref)

        loss_ref[...] += jnp.broadcast_to(lp, loss_ref.shape)
        dg_ref[...] += jnp.sum(dy * nv, axis=0, keepdims=True)

    row = pl.BlockSpec((tm, d), lambda i: (i, 0))
    vec = pl.BlockSpec((1, d), lambda i: (0, 0))
    return pl.pallas_call(
        body,
        name=name,
        grid=(t // tm,),
        in_specs=[row, row, vec],
        out_specs=[row, row, pl.BlockSpec((SUBLANES, LANES), lambda i: (0, 0)), vec],
        out_shape=[
            jax.ShapeDtypeStruct((t, d), F32),
            jax.ShapeDtypeStruct((t, d), BF),
            jax.ShapeDtypeStruct((SUBLANES, LANES), F32),
            jax.ShapeDtypeStruct((1, d), F32),
        ],
        compiler_params=_cp("arbitrary"),
    )(h, target, g)


def _ffn_in_fwd(u, w, *, name, tm=1024, tn=512, dep=None):
    t, d = u.shape
    f = w.shape[1] // 2
    tm, tn = _fit(t, tm), _fit(f, tn)
    nf = f // tn

    def body(*refs):
        u_ref, wa_ref, wb_ref, hid_ref, act_ref = refs[0], refs[1], refs[2], refs[-2], refs[-1]
        uv = u_ref[...]
        a = jnp.dot(uv, wa_ref[...], preferred_element_type=F32)
        b = jnp.dot(uv, wb_ref[...], preferred_element_type=F32)
        hid_ref[0] = a.astype(BF)
        hid_ref[1] = b.astype(BF)
        act_ref[...] = ((a * _sigmoid(a)) * b).astype(BF)

    in_specs = [
        pl.BlockSpec((tm, d), lambda i, j: (i, 0)),
        pl.BlockSpec((d, tn), lambda i, j: (0, j)),
        pl.BlockSpec((d, tn), lambda i, j: (0, nf + j)),
    ]
    args = (u, w, w)
    if dep is not None:
        in_specs.append(pl.BlockSpec(memory_space=pl.ANY))
        args += (dep,)
    return pl.pallas_call(
        body,
        name=name,
        grid=(t // tm, nf),
        in_specs=in_specs,
        out_specs=[pl.BlockSpec((2, tm, tn), lambda i, j: (0, i, j)), pl.BlockSpec((tm, tn), lambda i, j: (i, j))],
        out_shape=[jax.ShapeDtypeStruct((2, t, f), BF), jax.ShapeDtypeStruct((t, f), BF)],
        compiler_params=_cp("parallel", "parallel"),
    )(*args)


def _ffn_out_bwd(dh, w_out, hid2, *, name, tm=1024, tn=512, dep=None):
    t, d = dh.shape
    f = w_out.shape[0]
    tm, tn = _fit(t, tm), _fit(f, tn)

    def body(*refs):
        dh_ref, w_ref, hid_ref, o_ref = refs[0], refs[1], refs[2], refs[-1]
        dact = lax.dot_general(dh_ref[...], w_ref[...], _DIMS["nt"], preferred_element_type=F32)
        a = hid_ref[0].astype(F32)
        b = hid_ref[1].astype(F32)
        sg = _sigmoid(a)
        o_ref[0] = (dact * b * (sg * (1.0 + a * (1.0 - sg)))).astype(BF)
        o_ref[1] = (dact * (a * sg)).astype(BF)

    pair = pl.BlockSpec((2, tm, tn), lambda i, j: (0, i, j))
    in_specs = [pl.BlockSpec((tm, d), lambda i, j: (i, 0)), pl.BlockSpec((tn, d), lambda i, j: (j, 0)), pair]
    args = (dh, w_out, hid2)
    if dep is not None:
        in_specs.append(pl.BlockSpec(memory_space=pl.ANY))
        args += (dep,)
    return pl.pallas_call(
        body,
        name=name,
        grid=(t // tm, f // tn),
        in_specs=in_specs,
        out_specs=pair,
        out_shape=jax.ShapeDtypeStruct((2, t, f), BF),
        compiler_params=_cp("parallel", "parallel"),
    )(*args)


def _gate_fwd(cy, w, proj, ya, *, ga0, gc0, name, tm=1024, tc=512):
    t, d = ya.shape
    kc = cy.shape[1]
    tm, tc = _fit(t, tm), math.gcd(tc, d, ga0, gc0)
    a0, c0 = ga0 // tc, gc0 // tc

    def body(cy_ref, w_ref, ga_ref, gc_ref, ya_ref, yc_ref, o_ref):
        yc = jnp.dot(cy_ref[...], w_ref[...], preferred_element_type=F32)
        yc_ref[...] = yc
        o_ref[...] = (_sigmoid(ga_ref[...]) * ya_ref[...] + _sigmoid(gc_ref[...]) * yc).astype(BF)

    blk = pl.BlockSpec((tm, tc), lambda i, j: (i, j))
    return pl.pallas_call(
        body,
        name=name,
        grid=(t // tm, d // tc),
        in_specs=[
            pl.BlockSpec((tm, kc), lambda i, j: (i, 0)),
            pl.BlockSpec((kc, tc), lambda i, j: (0, j)),
            pl.BlockSpec((tm, tc), lambda i, j: (i, a0 + j)),
            pl.BlockSpec((tm, tc), lambda i, j: (i, c0 + j)),
            blk,
        ],
        out_specs=[blk, blk],
        out_shape=[jax.ShapeDtypeStruct((t, d), F32), jax.ShapeDtypeStruct((t, d), BF)],
        compiler_params=_cp("parallel", "parallel"),
    )(cy, w, proj, proj, ya)


def _gate_bwd(dh, w, proj, ya, yc, *, ga0, gc0, name, tm=1024, tc=512):
    t, d = ya.shape
    tm, tc = _fit(t, tm), math.gcd(tc, d, ga0, gc0)
    a0, c0 = ga0 // tc, gc0 // tc

    def body(dh_ref, w_ref, ga_ref, gc_ref, ya_ref, yc_ref, dya_ref, dyc_ref, dga_ref, dgc_ref):
        dmv = lax.dot_general(dh_ref[...], w_ref[...], _DIMS["nt"], preferred_element_type=F32)
        sa = _sigmoid(ga_ref[...])
        sc = _sigmoid(gc_ref[...])
        dya_ref[...] = (dmv * sa).astype(BF)
        dyc_ref[...] = (dmv * sc).astype(BF)
        dga_ref[...] = (dmv * ya_ref[...] * (sa * (1.0 - sa))).astype(BF)
        dgc_ref[...] = (dmv * yc_ref[...] * (sc * (1.0 - sc))).astype(BF)

    blk = pl.BlockSpec((tm, tc), lambda i, j: (i, j))
    out = jax.ShapeDtypeStruct((t, d), BF)
    return pl.pallas_call(
        body,
        name=name,
        grid=(t // tm, d // tc),
        in_specs=[
            pl.BlockSpec((tm, d), lambda i, j: (i, 0)),
            pl.BlockSpec((tc, d), lambda i, j: (j, 0)),
            pl.BlockSpec((tm, tc), lambda i, j: (i, a0 + j)),
            pl.BlockSpec((tm, tc), lambda i, j: (i, c0 + j)),
            blk,
            blk,
        ],
        out_specs=[blk, blk, blk, blk],
        out_shape=[out, out, out, out],
        compiler_params=_cp("parallel", "parallel"),
    )(dh, w, proj, proj, ya, yc)


def _conv_taps(cz, czp, i):
    czp = czp * (i > 0).astype(F32)
    h1 = czp[SUBLANES - 1:SUBLANES, :]
    h2 = czp[SUBLANES - 2:SUBLANES - 1, :]
    row = lax.broadcasted_iota(jnp.int32, cz.shape, 0)
    s1 = jnp.where(row == 0, h1, pltpu.roll(cz, 1, 0))
    s2 = jnp.where(row == 0, h2, jnp.where(row == 1, h1, pltpu.roll(cz, 2, 0)))
    return s1, s2


def _conv_fwd(proj, w8, *, z0, gb0, gc0, cw, name, tm=512, tc=512):
    t = proj.shape[0]
    tm, tc = min(tm, t), math.gcd(tc, cw, z0, gb0, gc0)
    zb, bb, cb = z0 // tc, gb0 // tc, gc0 // tc
    rb = tm // SUBLANES

    def body(z_ref, gb_ref, gc_ref, zp_ref, gcp_ref, w_ref, o_ref):
        i = pl.program_id(0)
        cz = gc_ref[...] * z_ref[...]
        s1, s2 = _conv_taps(cz, gcp_ref[...] * zp_ref[...], i)
        w = w_ref[...]
        y = w[0:1, :] * s2 + w[1:2, :] * s1 + w[2:3, :] * cz
        o_ref[...] = (gb_ref[...] * y).astype(BF)

    def cur(b0):
        return pl.BlockSpec((tm, tc), lambda i, j: (i, b0 + j))

    def prev(b0):
        return pl.BlockSpec((SUBLANES, tc), lambda i, j: (jnp.maximum(i * rb - 1, 0), b0 + j))

    return pl.pallas_call(
        body,
        name=name,
        grid=(t // tm, cw // tc),
        in_specs=[cur(zb), cur(bb), cur(cb), prev(zb), prev(cb), pl.BlockSpec((SUBLANES, tc), lambda i, j: (0, j))],
        out_specs=pl.BlockSpec((tm, tc), lambda i, j: (i, j)),
        out_shape=jax.ShapeDtypeStruct((t, cw), BF),
        compiler_params=_cp("parallel", "parallel"),
    )(proj, proj, proj, proj, proj, w8)


def _conv_bwd(proj, w8, dcy, *, z0, gb0, gc0, cw, name, tm=512, tc=512):
    t = proj.shape[0]
    tm, tc = min(tm, t), math.gcd(tc, cw, z0, gb0, gc0)
    zb, bb, cb = z0 // tc, gb0 // tc, gc0 // tc
    rb = tm // SUBLANES
    nt = t // tm

    def body(z_ref, gb_ref, gc_ref, zp_ref, gcp_ref, d_ref, dn_ref, gbn_ref, w_ref, dz_ref, dgb_ref, dgc_ref, dw_ref):
        i = pl.program_id(1)
        z = z_ref[...]
        gc = gc_ref[...]
        gb = gb_ref[...]
        cz = gc * z
        s1, s2 = _conv_taps(cz, gcp_ref[...] * zp_ref[...], i)
        w = w_ref[...]
        w0, w1, w2 = w[0:1, :], w[1:2, :], w[2:3, :]
        yc = w0 * s2 + w1 * s1 + w2 * cz
        dcyv = d_ref[...]
        dgb_ref[...] = (dcyv * yc).astype(BF)
        dyc = dcyv * gb
        dycn = dn_ref[...] * gbn_ref[...] * (i < nt - 1).astype(F32)
        n1, n2 = dycn[0:1, :], dycn[1:2, :]
        row = lax.broadcasted_iota(jnp.int32, cz.shape, 0)
        a1 = jnp.where(row == tm - 1, n1, pltpu.roll(dyc, tm - 1, 0))
        a2 = jnp.where(row == tm - 1, n2, jnp.where(row == tm - 2, n1, pltpu.roll(dyc, tm - 2, 0)))
        dcz = w2 * dyc + w1 * a1 + w0 * a2
        dz_ref[...] = (dcz * gc).astype(BF)
        dgc_ref[...] = (dcz * z).astype(BF)
        dw0 = jnp.sum(dyc * s2, axis=0, keepdims=True)
        dw1 = jnp.sum(dyc * s1, axis=0, keepdims=True)
        dw2 = jnp.sum(dyc * cz, axis=0, keepdims=True)
        r8 = lax.broadcasted_iota(jnp.int32, (SUBLANES, tc), 0)
        upd = jnp.where(r8 == 0, dw0, jnp.where(r8 == 1, dw1, jnp.where(r8 == 2, dw2, 0.0)))

        @pl.when(i == 0)
        def _():
            dw_ref[...] = jnp.zeros_like(dw_ref)

        dw_ref[...] += upd

    def cur(b0):
        return pl.BlockSpec((tm, tc), lambda j, i: (i, b0 + j))

    def prev(b0):
        return pl.BlockSpec((SUBLANES, tc), lambda j, i: (jnp.maximum(i * rb - 1, 0), b0 + j))

    def nxt(b0):
        return pl.BlockSpec((SUBLANES, tc), lambda j, i: (jnp.minimum((i + 1) * rb, t // SUBLANES - 1), b0 + j))

    blk = pl.BlockSpec((tm, tc), lambda j, i: (i, j))
    w_spec = pl.BlockSpec((SUBLANES, tc), lambda j, i: (0, j))
    out = jax.ShapeDtypeStruct((t, cw), BF)
    return pl.pallas_call(
        body,
        name=name,
        grid=(cw // tc, nt),
        in_specs=[cur(zb), cur(bb), cur(cb), prev(zb), prev(cb), blk, nxt(0), nxt(bb), w_spec],
        out_specs=[blk, blk, blk, w_spec],
        out_shape=[out, out, out, jax.ShapeDtypeStruct((SUBLANES, cw), F32)],
        compiler_params=_cp("parallel", "arbitrary"),
    )(proj, proj, proj, proj, proj, dcy, dcy, proj, w8)


def _rot_half(x):
    lane = lax.broadcasted_iota(jnp.int32, x.shape, 1)
    first = (lane % HEAD_DIM) < (HEAD_DIM // 2)
    return jnp.where(first, pltpu.roll(x, LANES - HEAD_DIM // 2, 1), pltpu.roll(x, HEAD_DIM // 2, 1))


def _rope(x, c, s):
    parts = []
    for a in range(x.shape[1] // LANES):
        xa = x[:, a * LANES:(a + 1) * LANES]
        parts.append(xa * c + _rot_half(xa) * s)
    return parts[0] if len(parts) == 1 else jnp.concatenate(parts, axis=1)


def _rope_bwd(dy, c, s):
    parts = []
    for a in range(dy.shape[1] // LANES):
        da = dy[:, a * LANES:(a + 1) * LANES]
        parts.append(da * c + _rot_half(da * s))
    return parts[0] if len(parts) == 1 else jnp.concatenate(parts, axis=1)


def _window(i):
    b = WINDOW
    r = lax.broadcasted_iota(jnp.int32, (b, b), 0)
    c = lax.broadcasted_iota(jnp.int32, (b, b), 1)
    return c <= r, c <= r + jnp.where(i > 0, b, 0)


def _band_pick(x, tri):
    b = tri.shape[0]
    return jnp.where(tri, x[:, b:], x[:, :b])


def _band_spread(y, tri):
    return jnp.concatenate([jnp.where(tri, 0.0, y), jnp.where(tri, y, 0.0)], axis=1)


def _chunk(x, a):
    return x[:, a * LANES:(a + 1) * LANES]


def _kv_aligned(kp, kc, h):
    band = jnp.concatenate([_chunk(kp, h // 2), _chunk(kc, h // 2)], axis=0).astype(F32)
    swapped = pltpu.roll(band, HEAD_DIM, 1)
    return (band, swapped) if h % 2 == 0 else (swapped, band)


def _swa_fwd(proj, cosf, sinf, sinks, *, nq, name, dep=None):
    t = proj.shape[0]
    nkv = nq // Q_PER_KV
    aw, kw, b = nq * HEAD_DIM, nkv * HEAD_DIM, WINDOW
    nb = t // b
    kblk = aw // kw
    scale = HEAD_DIM ** -0.5

    def body(*refs):
        sink_ref, q_ref, kc_ref, kp_ref, vc_ref, vp_ref, cc_ref, cp_ref, sc_ref, sp_ref = refs[:10]
        o_ref, qr_ref, kr_ref = refs[-3:]
        i = pl.program_id(0)
        cc, sc, cpv, spv = cc_ref[...], sc_ref[...], cp_ref[...], sp_ref[...]
        qr = _rope(q_ref[...], cc, sc)
        kc = _rope(kc_ref[...], cc, sc)
        kp = _rope(kp_ref[...], cpv, spv)
        qr_ref[...] = qr.astype(BF)
        kr_ref[...] = kc.astype(BF)
        vc, vp = vc_ref[...], vp_ref[...]
        tri, ok = _window(i)
        lo = lax.broadcasted_iota(jnp.int32, (b, LANES), 1) < HEAD_DIM
        for a in range(nq // 2):
            h = (2 * a) // Q_PER_KV
            ks = [x.astype(BF) for x in _kv_aligned(kp, kc, h)]
            vs = [x.astype(BF) for x in _kv_aligned(vp, vc, h)]
            qa = _chunk(qr, a)
            o_par = []
            for par in range(2):
                hq = 2 * a + par
                qm = jnp.where(lo if par == 0 else ~lo, qa, 0.0).astype(BF)
                s = _band_pick(lax.dot_general(qm, ks[par], _DIMS["nt"], preferred_element_type=F32), tri) * scale
                s = jnp.where(ok, s, -jnp.inf)
                sink = sink_ref[hq]
                m = jnp.maximum(jnp.max(s, axis=-1, keepdims=True), sink)
                p = jnp.exp(s - m)
                p = p / (jnp.sum(p, axis=-1, keepdims=True) + jnp.exp(sink - m))
                o_par.append(jnp.dot(_band_spread(p, tri).astype(BF), vs[par], preferred_element_type=F32))
            o_ref[:, a * LANES:(a + 1) * LANES] = jnp.where(lo, o_par[0], o_par[1]).astype(BF)

    def prev_i(i):
        return jnp.maximum(i - 1, 0)

    tab_c = pl.BlockSpec((b, LANES), lambda i: (i, 0))
    tab_p = pl.BlockSpec((b, LANES), lambda i: (prev_i(i), 0))
    in_specs = [
        pl.BlockSpec(memory_space=pltpu.SMEM),
        pl.BlockSpec((b, aw), lambda i: (i, 0)),
        pl.BlockSpec((b, kw), lambda i: (i, kblk)),
        pl.BlockSpec((b, kw), lambda i: (prev_i(i), kblk)),
        pl.BlockSpec((b, kw), lambda i: (i, kblk + 1)),
        pl.BlockSpec((b, kw), lambda i: (prev_i(i), kblk + 1)),
        tab_c,
        tab_p,
        tab_c,
        tab_p,
    ]
    args = (sinks, proj, proj, proj, proj, proj, cosf, cosf, sinf, sinf)
    if dep is not None:
        in_specs.append(pl.BlockSpec(memory_space=pl.ANY))
        args += (dep,)
    return pl.pallas_call(
        body,
        name=name,
        grid=(nb,),
        in_specs=in_specs,
        out_specs=[
            pl.BlockSpec((b, aw), lambda i: (i, 0)),
            pl.BlockSpec((b, aw), lambda i: (i, 0)),
            pl.BlockSpec((b, kw), lambda i: (i, 0)),
        ],
        out_shape=[
            jax.ShapeDtypeStruct((t, aw), BF),
            jax.ShapeDtypeStruct((t, aw), BF),
            jax.ShapeDtypeStruct((t, kw), BF),
        ],
        compiler_params=_cp("parallel"),
    )(*args)


def _swa_bwd(qr, kr, proj, do, cosf, sinf, sinks, *, nq, name):
    t = proj.shape[0]
    nkv = nq // Q_PER_KV
    aw, kw, b = nq * HEAD_DIM, nkv * HEAD_DIM, WINDOW
    nb = t // b
    kblk = aw // kw
    scale = HEAD_DIM ** -0.5

    def body(sink_ref, q_ref, kc_ref, kp_ref, vc_ref, vp_ref, do_ref, cc_ref, cp_ref, sc_ref, sp_ref,
             dq_ref, dk_ref, dv_ref, ds_ref, ck_ref, cv_ref, sacc_ref):
        i = pl.program_id(0)

        @pl.when(i == 0)
        def _():
            ck_ref[...] = jnp.zeros_like(ck_ref)
            cv_ref[...] = jnp.zeros_like(cv_ref)
            sacc_ref[...] = jnp.zeros_like(sacc_ref)

        @pl.when(i < nb)
        def _():
            q = q_ref[...]
            kc, kp = kc_ref[...], kp_ref[...]
            vc, vp = vc_ref[...], vp_ref[...]
            dov = do_ref[...]
            tri, ok = _window(i)
            lane = lax.broadcasted_iota(jnp.int32, (b, LANES), 1)
            lo = lane < HEAD_DIM
            cc, sc = cc_ref[...], sc_ref[...]
            nch = kw // LANES
            row_lo = lax.broadcasted_iota(jnp.int32, (LANES, b), 0) < HEAD_DIM
            dk_ch = [jnp.zeros((LANES, 2 * b), F32) for _ in range(nch)]
            dv_ch = [jnp.zeros((LANES, 2 * b), F32) for _ in range(nch)]
            sacc = jnp.zeros((b, LANES), F32)
            for a in range(nq // 2):
                h = (2 * a) // Q_PER_KV
                ks = [x.astype(BF) for x in _kv_aligned(kp, kc, h)]
                vs = [x.astype(BF) for x in _kv_aligned(vp, vc, h)]
                qa = _chunk(q, a).astype(F32)
                doa = _chunk(dov, a).astype(F32)
                qa_t, doa_t = qa.T, doa.T
                dq_par = []
                for par in range(2):
                    hq = 2 * a + par
                    mine = lo if par == 0 else ~lo
                    mine_t = row_lo if par == 0 else ~row_lo
                    qm = jnp.where(mine, qa, 0.0).astype(BF)
                    dom = jnp.where(mine, doa, 0.0).astype(BF)
                    qm_t = jnp.where(mine_t, qa_t, 0.0).astype(BF)
                    dom_t = jnp.where(mine_t, doa_t, 0.0).astype(BF)
                    s = _band_pick(lax.dot_general(qm, ks[par], _DIMS["nt"], preferred_element_type=F32), tri)
                    s = jnp.where(ok, s * scale, -jnp.inf)
                    sink = sink_ref[hq]
                    m = jnp.maximum(jnp.max(s, axis=-1, keepdims=True), sink)
                    e = jnp.exp(s - m)
                    es = jnp.exp(sink - m)
                    zinv = 1.0 / (jnp.sum(e, axis=-1, keepdims=True) + es)
                    p = e * zinv
                    dp = _band_pick(lax.dot_general(dom, vs[par], _DIMS["nt"], preferred_element_type=F32), tri)
                    delta = jnp.sum(p * dp, axis=-1, keepdims=True)
                    dsv = _band_spread(p * (dp - delta) * scale, tri).astype(BF)
                    sacc = sacc + jnp.where(lane == hq, -(es * zinv) * delta, 0.0)
                    dq_par.append(jnp.dot(dsv, ks[par], preferred_element_type=F32))
                    dkh = jnp.dot(qm_t, dsv, preferred_element_type=F32)
                    dvh = jnp.dot(dom_t, _band_spread(p, tri).astype(BF), preferred_element_type=F32)
                    if par != h % 2:
                        dkh = pltpu.roll(dkh, HEAD_DIM, 0)
                        dvh = pltpu.roll(dvh, HEAD_DIM, 0)
                    dk_ch[h // 2] = dk_ch[h // 2] + dkh
                    dv_ch[h // 2] = dv_ch[h // 2] + dvh
                dqa = jnp.where(lo, dq_par[0], dq_par[1])
                dq_ref[:, a * LANES:(a + 1) * LANES] = _rope_bwd(dqa, cc, sc).astype(BF)
            dk_ch = [x.T for x in dk_ch]
            dv_ch = [x.T for x in dv_ch]
            dk = dk_ch[0] if nch == 1 else jnp.concatenate(dk_ch, axis=1)
            dv = dv_ch[0] if nch == 1 else jnp.concatenate(dv_ch, axis=1)
            dk_ref[...] = _rope_bwd(ck_ref[...] + dk[:b, :], cp_ref[...], sp_ref[...]).astype(BF)
            dv_ref[...] = (cv_ref[...] + dv[:b, :]).astype(BF)
            ck_ref[...] = dk[b:, :]
            cv_ref[...] = dv[b:, :]
            sacc_ref[...] += sacc

        @pl.when(i == nb)
        def _():
            dk_ref[...] = _rope_bwd(ck_ref[...], cp_ref[...], sp_ref[...]).astype(BF)
            dv_ref[...] = cv_ref[...].astype(BF)
            ds_ref[...] = jnp.broadcast_to(jnp.sum(sacc_ref[...], axis=0, keepdims=True), ds_ref.shape)

    def cur_i(i):
        return jnp.minimum(i, nb - 1)

    def prev_i(i):
        return jnp.clip(i - 1, 0, nb - 1)

    tab_c = pl.BlockSpec((b, LANES), lambda i: (cur_i(i), 0))
    tab_p = pl.BlockSpec((b, LANES), lambda i: (prev_i(i), 0))
    return pl.pallas_call(
        body,
        name=name,
        grid=(nb + 1,),
        in_specs=[
            pl.BlockSpec(memory_space=pltpu.SMEM),
            pl.BlockSpec((b, aw), lambda i: (cur_i(i), 0)),
            pl.BlockSpec((b, kw), lambda i: (cur_i(i), 0)),
            pl.BlockSpec((b, kw), lambda i: (prev_i(i), 0)),
            pl.BlockSpec((b, kw), lambda i: (cur_i(i), kblk + 1)),
            pl.BlockSpec((b, kw), lambda i: (prev_i(i), kblk + 1)),
            pl.BlockSpec((b, aw), lambda i: (cur_i(i), 0)),
            tab_c,
            tab_p,
            tab_c,
            tab_p,
        ],
        out_specs=[
            pl.BlockSpec((b, aw), lambda i: (cur_i(i), 0)),
            pl.BlockSpec((b, kw), lambda i: (prev_i(i), 0)),
            pl.BlockSpec((b, kw), lambda i: (prev_i(i), 0)),
            pl.BlockSpec((SUBLANES, LANES), lambda i: (0, 0)),
        ],
        out_shape=[
            jax.ShapeDtypeStruct((t, aw), BF),
            jax.ShapeDtypeStruct((t, kw), BF),
            jax.ShapeDtypeStruct((t, kw), BF),
            jax.ShapeDtypeStruct((SUBLANES, LANES), F32),
        ],
        scratch_shapes=[pltpu.VMEM((b, kw), F32), pltpu.VMEM((b, kw), F32), pltpu.VMEM((b, LANES), F32)],
        compiler_params=_cp("arbitrary"),
    )(sinks, qr, kr, kr, proj, proj, do, cosf, cosf, sinf, sinf)


def _xattn_fwd(xq, kv, *, name, tq=512):
    t, xw = xq.shape
    mtok = kv.shape[0]
    tq = min(tq, t)
    nh = xw // X_HEAD_DIM
    scale = X_HEAD_DIM ** -0.5

    def body(q_ref, kv_ref, o_ref):
        q = q_ref[...]
        kvv = kv_ref[...]
        outs = []
        for h in range(nh):
            sl = slice(h * X_HEAD_DIM, (h + 1) * X_HEAD_DIM)
            k = kvv[:, sl]
            v = kvv[:, xw + h * X_HEAD_DIM: xw + (h + 1) * X_HEAD_DIM]
            s = lax.dot_general(q[:, sl], k, _DIMS["nt"], preferred_element_type=F32) * scale
            e = jnp.exp(s - jnp.max(s, axis=-1, keepdims=True))
            p = e / jnp.sum(e, axis=-1, keepdims=True)
            outs.append(jnp.dot(p.astype(BF), v, preferred_element_type=F32))
        o_ref[...] = jnp.concatenate(outs, axis=1).astype(BF)

    return pl.pallas_call(
        body,
        name=name,
        grid=(t // tq,),
        in_specs=[pl.BlockSpec((tq, xw), lambda i: (i, 0)), pl.BlockSpec((mtok, 2 * xw), lambda i: (0, 0))],
        out_specs=pl.BlockSpec((tq, xw), lambda i: (i, 0)),
        out_shape=jax.ShapeDtypeStruct((t, xw), BF),
        compiler_params=_cp("parallel"),
    )(xq, kv)


def _xattn_bwd(xq, kv, do, *, name, tq=512):
    t, xw = xq.shape
    mtok = kv.shape[0]
    tq = min(tq, t)
    nh = xw // X_HEAD_DIM
    scale = X_HEAD_DIM ** -0.5

    def body(q_ref, kv_ref, do_ref, dq_ref, dkv_ref):
        i = pl.program_id(0)
        q = q_ref[...]
        kvv = kv_ref[...]
        dov = do_ref[...]
        dqs, dks, dvs = [], [], []
        for h in range(nh):
            sl = slice(h * X_HEAD_DIM, (h + 1) * X_HEAD_DIM)
            k = kvv[:, sl]
            v = kvv[:, xw + h * X_HEAD_DIM: xw + (h + 1) * X_HEAD_DIM]
            qh, doh = q[:, sl], dov[:, sl]
            s = lax.dot_general(qh, k, _DIMS["nt"], preferred_element_type=F32) * scale
            e = jnp.exp(s - jnp.max(s, axis=-1, keepdims=True))
            p = e / jnp.sum(e, axis=-1, keepdims=True)
            dp = lax.dot_general(doh, v, _DIMS["nt"], preferred_element_type=F32)
            delta = jnp.sum(p * dp, axis=-1, keepdims=True)
            dsv = (p * (dp - delta) * scale).astype(BF)
            dqs.append(jnp.dot(dsv, k, preferred_element_type=F32))
            dks.append(lax.dot_general(dsv, qh, _DIMS["tn"], preferred_element_type=F32))
            dvs.append(lax.dot_general(p.astype(BF), doh, _DIMS["tn"], preferred_element_type=F32))
        dq_ref[...] = jnp.concatenate(dqs, axis=1).astype(BF)

        @pl.when(i == 0)
        def _():
            dkv_ref[...] = jnp.zeros_like(dkv_ref)

        dkv_ref[...] += jnp.concatenate(dks + dvs, axis=1)

    row = pl.BlockSpec((tq, xw), lambda i: (i, 0))
    full = pl.BlockSpec((mtok, 2 * xw), lambda i: (0, 0))
    return pl.pallas_call(
        body,
        name=name,
        grid=(t // tq,),
        in_specs=[row, full, row],
        out_specs=[row, full],
        out_shape=[jax.ShapeDtypeStruct((t, xw), BF), jax.ShapeDtypeStruct((mtok, 2 * xw), F32)],
        compiler_params=_cp("arbitrary"),
    )(xq, kv, do)


def _adam_math(w, g, m, v):
    m = ADAM_B1 * m + (1.0 - ADAM_B1) * g
    v = ADAM_B2 * v + (1.0 - ADAM_B2) * (g * g)
    m_hat = m / (1.0 - ADAM_B1 ** ADAM_STEP)
    v_hat = v / (1.0 - ADAM_B2 ** ADAM_STEP)
    delta = -ADAM_LR * (m_hat / (jnp.sqrt(v_hat) + ADAM_EPS) + ADAM_WD * w)
    return delta, m, v


def _row_tile(r, c, n_arrays, budget=24 * 1024 * 1024):
    step = 2 * SUBLANES
    cap = max(step, budget // (2 * n_arrays * c * 4))
    if r <= cap:
        return r
    best = None
    for tr in range(step, cap + 1, step):
        if r % tr == 0:
            best = tr
    assert best is not None, (r, c)
    return best


def _adamw_sum(parts, own, me, w, m, v, *, name):
    _, r, c = parts.shape
    tr = _row_tile(r, c, 12)

    def body(me_ref, p_ref, own_ref, w_ref, m_ref, v_ref, g_ref, d_ref, nm_ref, nv_ref):
        mine = jnp.full((tr, c), me_ref[0], jnp.int32)
        g = None
        for s in range(N_DEV):
            term = jnp.where(mine == s, own_ref[...], p_ref[s]).astype(F32)
            g = term if g is None else g + term
        g_ref[...] = g
        d_ref[...], nm_ref[...], nv_ref[...] = _adam_math(w_ref[...], g, m_ref[...], v_ref[...])

    blk = pl.BlockSpec((tr, c), lambda i, me_ref: (i, 0))
    out = jax.ShapeDtypeStruct((r, c), F32)
    return pl.pallas_call(
        body,
        name=name,
        grid_spec=pltpu.PrefetchScalarGridSpec(
            num_scalar_prefetch=1,
            grid=(r // tr,),
            in_specs=[
                pl.BlockSpec((N_DEV, tr, c), lambda i, me_ref: (0, i, 0)),
                pl.BlockSpec((None, tr, c), lambda i, me_ref: (me_ref[0], i, 0)),
                blk, blk, blk,
            ],
            out_specs=[blk, blk, blk, blk],
        ),
        out_shape=[out, out, out, out],
        compiler_params=_cp("parallel"),
    )(me, parts, own, w, m, v)


def _to_bf16(a, *, name, dep=None):
    r, c = a.shape
    tr = _row_tile(r, c, 2)

    def body(*refs):
        refs[-1][...] = refs[0][...].astype(BF)

    blk = pl.BlockSpec((tr, c), lambda i: (i, 0))
    in_specs, args = [blk], (a,)
    if dep is not None:
        in_specs.append(pl.BlockSpec(memory_space=pl.ANY))
        args += (dep,)
    return pl.pallas_call(
        body,
        name=name,
        grid=(r // tr,),
        in_specs=in_specs,
        out_specs=blk,
        out_shape=jax.ShapeDtypeStruct((r, c), BF),
        compiler_params=_cp("parallel"),
    )(*args)


def _sum_partials(parts, own, me, *, name):
    _, r, c = parts.shape
    tr = _row_tile(r, c, 6)

    def body(me_ref, p_ref, own_ref, g_ref):
        mine = jnp.full((tr, c), me_ref[0], jnp.int32)
        g = None
        for s in range(N_DEV):
            term = jnp.where(mine == s, own_ref[...], p_ref[s]).astype(F32)
            g = term if g is None else g + term
        g_ref[...] = g

    return pl.pallas_call(
        body,
        name=name,
        grid_spec=pltpu.PrefetchScalarGridSpec(
            num_scalar_prefetch=1,
            grid=(r // tr,),
            in_specs=[
                pl.BlockSpec((N_DEV, tr, c), lambda i, me_ref: (0, i, 0)),
                pl.BlockSpec((None, tr, c), lambda i, me_ref: (me_ref[0], i, 0)),
            ],
            out_specs=pl.BlockSpec((tr, c), lambda i, me_ref: (i, 0)),
        ),
        out_shape=jax.ShapeDtypeStruct((r, c), F32),
        compiler_params=_cp("parallel"),
    )(me, parts, own)


def _adamw_rows(w, g, m, v, *, name):
    r, c = w.shape
    tr = _row_tile(r, c, 7)

    def body(w_ref, g_ref, m_ref, v_ref, d_ref, nm_ref, nv_ref):
        d_ref[...], nm_ref[...], nv_ref[...] = _adam_math(w_ref[...], g_ref[...], m_ref[...], v_ref[...])

    blk = pl.BlockSpec((tr, c), lambda i: (i, 0))
    out = jax.ShapeDtypeStruct((r, c), F32)
    return pl.pallas_call(
        body,
        name=name,
        grid=(r // tr,),
        in_specs=[blk, blk, blk, blk],
        out_specs=[blk, blk, blk],
        out_shape=[out, out, out],
        compiler_params=_cp("parallel"),
    )(w, g, m, v)


def _adamw_small(w, g, m, v, *, name):
    def body(w_ref, g_ref, m_ref, v_ref, d_ref, nm_ref, nv_ref):
        d_ref[...], nm_ref[...], nv_ref[...] = _adam_math(w_ref[...], g_ref[...], m_ref[...], v_ref[...])

    out = jax.ShapeDtypeStruct(w.shape, F32)
    return pl.pallas_call(body, name=name, out_shape=[out, out, out])(w, g, m, v)


def _mesh_pos():
    x, y, c = lax.axis_index("x"), lax.axis_index("y"), lax.axis_index("c")
    return x, y, c


def _peer(x, y, c, mask):
    px = 1 - x if mask & 4 else x
    py = 1 - y if mask & 2 else y
    pc = 1 - c if mask & 1 else c
    return (px, py, pc), 4 * px + 2 * py + pc


_HBM = pl.BlockSpec(memory_space=pltpu.HBM)
_SEM = pl.BlockSpec(memory_space=pltpu.SEMAPHORE)
_EFFECT = pltpu.SideEffectType.DATAFLOW_SIDE_EFFECTING


def _me():
    return 4 * lax.axis_index("x") + 2 * lax.axis_index("y") + lax.axis_index("c")


def _landing(own, me):
    land = lax.empty((N_DEV,) + own.shape, own.dtype)
    return lax.dynamic_update_slice(land, own[None], (me, 0, 0))


_ALL = tuple(range(1, N_DEV))
_CHIPS = (2, 4, 6)
GATHER_DIRECT = tuple((m, None, 0, m) for m in _ALL)
SCATTER_DIRECT = tuple((m, m, 0, m) for m in _ALL)
GATHER_CHIPS = tuple((m, None, 0, m) for m in (1,) + _CHIPS)
GATHER_SIBLING = tuple((1, m, m, m ^ 1) for m in _CHIPS)


def _copy(src, land, send_sem, recv_sem, sem, x, y, c, entry, arriving):
    to, src_m, dst_m, arr_m = entry
    peer, _ = _peer(x, y, c, to)
    blk = lambda m: _peer(x, y, c, m)[1]
    return pltpu.make_async_remote_copy(
        src_ref=src if src_m is None else src.at[blk(src_m)],
        dst_ref=land.at[blk(arr_m if arriving else dst_m)],
        send_sem=send_sem.at[sem], recv_sem=recv_sem.at[sem], device_id=peer, device_id_type=MESH)


def _exchange_start(groups, plan, *, name, after=None):
    flat = [p for g in groups for p in g]
    from_land = flat[0][0] is None
    n, ng, nc = len(flat), len(groups), len(plan)
    n_buf = n if from_land else 2 * n

    def body(*refs):
        lands = refs[:n] if from_land else refs[n:2 * n]
        srcs = lands if from_land else refs[:n]
        sems = refs[n_buf + (after is not None):n_buf + (after is not None) + 2 * ng]
        token = refs[-1]
        x, y, c = _mesh_pos()
        w = 0
        for gi, g in enumerate(groups):
            for wi in range(len(g)):
                for k, entry in enumerate(plan):
                    _copy(srcs[w], lands[w], sems[2 * gi], sems[2 * gi + 1], wi * nc + k, x, y, c, entry,
                          False).start()
                w += 1
        token[...] = jnp.zeros_like(token)

    sem_shapes = []
    for g in groups:
        sem_shapes += [pltpu.SemaphoreType.DMA((len(g) * nc,))] * 2
    args = [] if from_land else [pltpu.with_memory_space_constraint(s, pltpu.HBM) for s, _ in flat]
    args += [pltpu.with_memory_space_constraint(l, pltpu.HBM) for _, l in flat]
    extra = [] if after is None else [after]
    outs = pl.pallas_call(
        body,
        name=name,
        in_specs=[_HBM] * n_buf + [pl.BlockSpec(memory_space=pl.ANY)] * len(extra),
        out_specs=[_SEM] * (2 * ng) + [_HBM] * n_buf + [pl.BlockSpec(memory_space=pltpu.VMEM)],
        out_shape=sem_shapes + [pltpu.HBM(a.shape, a.dtype) for a in args]
        + [jax.ShapeDtypeStruct((SUBLANES, LANES), F32)],
        input_output_aliases={i: 2 * ng + i for i in range(n_buf)},
        compiler_params=pltpu.CompilerParams(has_side_effects=_EFFECT),
    )(*args, *extra)
    sems, thru, token = outs[:2 * ng], outs[2 * ng:2 * ng + n_buf], outs[-1]
    res, w = [], 0
    for gi, g in enumerate(groups):
        m = len(g)
        srcs = [None] * m if from_land else list(thru[w:w + m])
        lands = list(thru[w:w + m]) if from_land else list(thru[n + w:n + w + m])
        res.append((sems[2 * gi], sems[2 * gi + 1], srcs, lands))
        w += m
    return res, token


def _exchange_wait(group, plan, after, *, name):
    send_sems, recv_sems, srcs_in, lands_in = group
    n, nc = len(lands_in), len(plan)
    from_land = srcs_in[0] is None
    n_buf = n if from_land else 2 * n

    def body(*refs):
        lands = refs[:n] if from_land else refs[n:2 * n]
        srcs = lands if from_land else refs[:n]
        send_sem, recv_sem = refs[n_buf], refs[n_buf + 1]
        x, y, c = _mesh_pos()
        for w in range(n):
            for k, entry in enumerate(plan):
                cp = _copy(srcs[w], lands[w], send_sem, recv_sem, w * nc + k, x, y, c, entry, True)
                cp.wait_send()
                cp.wait_recv()

    bufs = lands_in if from_land else srcs_in + lands_in
    outs = pl.pallas_call(
        body,
        name=name,
        in_specs=[_HBM] * n_buf + [_SEM, _SEM, pl.BlockSpec(memory_space=pl.ANY)],
        out_specs=[_HBM] * n_buf,
        out_shape=[pltpu.HBM(a.shape, a.dtype) for a in bufs],
        input_output_aliases={i: i for i in range(n_buf)},
        compiler_params=pltpu.CompilerParams(has_side_effects=_EFFECT),
    )(*bufs, send_sems, recv_sems, after)
    if from_land:
        return [None] * n, list(outs)
    return list(outs[:n]), list(outs[n:])


def _all_reduce_small(parts, rows, width, *, name, dep=None):
    n = len(parts)

    def body(*refs):
        ins = refs[:n]
        o_ref, pack_ref, buf_ref, send_sems, recv_sems = refs[-5:]
        x, y, c_ = _mesh_pos()
        me = 4 * x + 2 * y + c_
        pack_ref[...] = jnp.zeros_like(pack_ref)
        for ref, (arr, r0, nr) in zip(ins, parts):
            pack_ref[r0:r0 + nr, 0:arr.shape[1]] = ref[0:nr, :]
        sends, recvs = [], []
        for k in range(N_DEV - 1):
            peer, pidx = _peer(x, y, c_, k + 1)
            cp = pltpu.make_async_remote_copy(
                src_ref=pack_ref, dst_ref=buf_ref.at[me], send_sem=send_sems.at[k], recv_sem=recv_sems.at[k],
                device_id=peer, device_id_type=MESH)
            cp.start()
            sends.append(cp)
            recvs.append(pltpu.make_async_remote_copy(
                src_ref=pack_ref, dst_ref=buf_ref.at[pidx], send_sem=send_sems.at[k], recv_sem=recv_sems.at[k],
                device_id=peer, device_id_type=MESH))
        buf_ref[me] = pack_ref[...]
        for rc in recvs:
            rc.wait_recv()
        for cp in sends:
            cp.wait_send()
        acc = buf_ref[0]
        for s in range(1, N_DEV):
            acc = acc + buf_ref[s]
        o_ref[...] = acc

    vmem = pl.BlockSpec(memory_space=pltpu.VMEM)
    in_specs = [vmem] * n
    args = [p[0] for p in parts]
    if dep is not None:
        in_specs.append(pl.BlockSpec(memory_space=pl.ANY))
        args.append(dep)
    return pl.pallas_call(
        body,
        name=name,
        in_specs=in_specs,
        out_specs=vmem,
        out_shape=jax.ShapeDtypeStruct((rows, width), F32),
        scratch_shapes=[
            pltpu.VMEM((rows, width), F32),
            pltpu.VMEM((N_DEV, rows, width), F32),
            pltpu.SemaphoreType.DMA((N_DEV - 1,)),
            pltpu.SemaphoreType.DMA((N_DEV - 1,)),
        ],
    )(*args)


def _rope_tables(t):
    half = HEAD_DIM // 2
    inv_freq = ROPE_THETA ** (-jnp.arange(half, dtype=F32) / half)
    ang = jnp.arange(t, dtype=jnp.int32).astype(F32)[:, None] * inv_freq[None, :]
    cos, sin = jnp.cos(ang), jnp.sin(ang)
    cosf = jnp.concatenate([cos, cos, cos, cos], axis=1)
    sinf = jnp.concatenate([-sin, sin, -sin, sin], axis=1)
    return cosf, sinf


def _local_step(x, mem, target, gains, sinks, aw, cw, pre_w, get_w, put_g, dep0=None):
    t, d = x.shape
    nq = aw // HEAD_DIM
    kw = aw // Q_PER_KV
    z0 = aw + 2 * kw
    gb0, gc0 = z0 + cw, z0 + 2 * cw
    ga0 = z0 + 3 * cw
    gcm0 = ga0 + d
    cosf, sinf = _rope_tables(t)

    u1 = _rms_fwd(x, gains["g_mix"], name="rms_mix", dep=dep0)
    mem_n = _rms_fwd(mem, gains["g_mem"], name="rms_mem", dep=dep0)
    pre_w("w_in", u1)
    w_in_t = get_w("w_in", u1)
    proj = _mm(u1, w_in_t, mode="nt", tm=1024, tn=512, tk=2048, out_dtype=F32, name="mm_in")
    o_attn, q_rot, k_rot = _swa_fwd(proj, cosf, sinf, sinks, nq=nq, name="swa_fwd", dep=pre_w("conv_w8", proj))
    conv_w8 = get_w("conv_w8", o_attn)
    w_attn_proj, w_conv_proj, w_mix_out = (get_w(n, o_attn) for n in ("w_attn_proj", "w_conv_proj", "w_mix_out"))
    w_xq, w_xkv, w_xo = (get_w(n, o_attn) for n in ("w_xq", "w_xkv", "w_xo"))
    y_attn = _mm(o_attn, w_attn_proj, mode="nn", tm=1024, tn=1024, tk=1024, out_dtype=F32, name="mm_attn_proj")
    cy = _conv_fwd(proj, conv_w8, z0=z0, gb0=gb0, gc0=gc0, cw=cw, name="conv_fwd")
    y_conv, merged = _gate_fwd(cy, w_conv_proj, proj, y_attn, ga0=ga0, gc0=gcm0, name="mm_conv_proj")
    h1 = _mm(merged, w_mix_out, mode="nn", tm=1024, tn=1024, tk=2048, out_dtype=F32, name="mm_mix_out", residual=x)
    u2 = _rms_fwd(h1, gains["g_xattn"], name="rms_xattn", dep=pre_w("w_ffn_in", h1))
    xq = _mm(u2, w_xq, mode="nn", tm=1024, tn=512, tk=2048, out_dtype=BF, name="mm_xq")
    kv = _mm(mem_n, w_xkv, mode="nn", tm=256, tn=1024, tk=2048, out_dtype=BF, name="mm_xkv")
    o_x = _xattn_fwd(xq, kv, name="xattn_fwd")
    h2 = _mm(o_x, w_xo, mode="nn", tm=1024, tn=1024, tk=512, out_dtype=F32, name="mm_xo", residual=h1)
    u3 = _rms_fwd(h2, gains["g_ffn"], name="rms_ffn")
    w_ffn_in = get_w("w_ffn_in", xq)
    hid2, act = _ffn_in_fwd(u3, w_ffn_in, name="mm_ffn_in", dep=pre_w("w_ffn_out", u3))
    w_ffn_out = get_w("w_ffn_out", act)
    h3 = _mm(act, w_ffn_out, mode="nn", tm=512, tn=1024, tk=8192, out_dtype=F32, name="mm_ffn_out", residual=h2)

    tt = 8192
    dh3, dh3b, loss_tile, dg_final = _loss_head(h3, target, gains["g_final"], name="loss_head")
    tok = put_g("w_ffn_out", _mm(act, dh3b, mode="tn", tm=512, tn=1024, tk=tt, out_dtype=BF, name="mm_dw_ffn_out"))
    dhid2 = _ffn_out_bwd(dh3b, w_ffn_out, hid2, name="mm_dact", dep=tok)
    f2 = w_ffn_in.shape[1]
    tok = put_g("w_ffn_in", _mm(u3, dhid2, mode="tn", tm=1024, tn=f2 // N_DEV, tk=tt, out_dtype=BF,
                                name="mm_dw_ffn_in", b_planes=2, stacked=True), stacked=True)
    du3 = _mm(dhid2, w_ffn_in, mode="nt", tm=1024, tn=1024, tk=2816, out_dtype=F32, name="mm_du3", dep=tok,
              a_planes=2)
    dh2, dh2b, dg_ffn = _rms_bwd(du3, h2, gains["g_ffn"], dh3, name="rms_ffn_bwd")
    put_g("w_xo", _mm(o_x, dh2b, mode="tn", tm=512, tn=d // N_DEV, tk=tt, out_dtype=BF, name="mm_dw_xo",
                      stacked=True), stacked=True)
    do_x = _mm(dh2b, w_xo, mode="nt", tm=1024, tn=512, tk=2048, out_dtype=BF, name="mm_do_x")
    dxq, dkv = _xattn_bwd(xq, kv, do_x, name="xattn_bwd")
    put_g("w_xkv", _mm(mem_n, dkv, mode="tn", tm=1024, tn=1024, tk=256, out_dtype=BF, name="mm_dw_xkv"))
    tok = put_g("w_xq", _mm(u2, dxq, mode="tn", tm=1024, tn=512, tk=tt, out_dtype=BF, name="mm_dw_xq"))
    du2 = _mm(dxq, w_xq, mode="nt", tm=1024, tn=1024, tk=512, out_dtype=F32, name="mm_du2", dep=tok)
    dmem_n = _mm(dkv, w_xkv, mode="nt", tm=256, tn=1024, tk=1024, out_dtype=F32, name="mm_dmem")
    _, _, dg_mem = _rms_bwd(dmem_n, mem, gains["g_mem"], None, name="rms_mem_bwd")
    dh1, dh1b, dg_xattn = _rms_bwd(du2, h1, gains["g_xattn"], dh2, name="rms_xattn_bwd")
    put_g("w_mix_out", _mm(merged, dh1b, mode="tn", tm=1024, tn=1024, tk=tt, out_dtype=BF, name="mm_dw_mix_out"))
    dya, dyc, dga, dgc = _gate_bwd(dh1b, w_mix_out, proj, y_attn, y_conv, ga0=ga0, gc0=gcm0, name="mm_dmerged")
    put_g("w_attn_proj", _mm(o_attn, dya, mode="tn", tm=1024, tn=d // N_DEV, tk=tt, out_dtype=BF,
                             name="mm_dw_attn_proj", stacked=True), stacked=True)
    do_attn = _mm(dya, w_attn_proj, mode="nt", tm=1024, tn=1024, tk=2048, out_dtype=BF, name="mm_do_attn")
    tok = put_g("w_conv_proj", _mm(cy, dyc, mode="tn", tm=1024, tn=d // N_DEV, tk=tt, out_dtype=BF,
                                   name="mm_dw_conv_proj", stacked=True), stacked=True)
    dcy = _mm(dyc, w_conv_proj, mode="nt", tm=1024, tn=1024, tk=2048, out_dtype=F32, name="mm_dcy", dep=tok)
    dz, dgb, dgcv, dconv_w8 = _conv_bwd(proj, conv_w8, dcy, z0=z0, gb0=gb0, gc0=gc0, cw=cw, name="conv_bwd")
    dq, dk, dv, dsink_tile = _swa_bwd(q_rot, k_rot, proj, do_attn, cosf, sinf, sinks, nq=nq, name="swa_bwd")
    dproj = jnp.concatenate([dq, dk, dv, dz, dgb, dgcv, dga, dgc], axis=1)
    for hi in range(2):
        tok = put_g("w_in_%d" % hi, _mm(dproj, u1, mode="tn", tm=512, tn=d // 2, tk=tt, out_dtype=BF,
                                        name="mm_dw_in_%d" % hi, b_cols=(hi * (d // 2), d // 2), dep=tok))
    du1 = _mm(dproj, w_in_t, mode="nn", tm=512, tn=1024, tk=4352, out_dtype=F32, name="mm_du1", dep=tok)
    grad_x, _, dg_mix = _rms_bwd(du1, x, gains["g_mix"], dh1, name="rms_mix_bwd")

    small = {
        "g_mix": dg_mix, "g_xattn": dg_xattn, "g_mem": dg_mem, "g_ffn": dg_ffn, "g_final": dg_final,
        "attn_sinks": dsink_tile, "conv_w8": dconv_w8, "loss": loss_tile,
    }
    return grad_x, small


_COL_SHARDED = ("w_in", "w_attn_proj", "w_conv_proj", "w_xo", "w_ffn_in")
_ROW_SHARDED = ("w_mix_out", "w_xq", "w_xkv", "w_ffn_out")
_BIG = _COL_SHARDED + _ROW_SHARDED
_GAINS = ("g_mix", "g_xattn", "g_mem", "g_ffn", "g_final")
_GATHER_GROUPS = (("w_in",), ("conv_w8", "w_attn_proj", "w_conv_proj", "w_mix_out", "w_xq", "w_xkv", "w_xo"),
                  ("w_ffn_in",), ("w_ffn_out",))
_SCATTER_GROUPS = (("w_ffn_out",), ("w_ffn_in",), ("w_xo", "w_xq", "w_xkv"),
                   ("w_mix_out", "w_attn_proj", "w_conv_proj"), ("w_in_0",), ("w_in_1",))
_WEIGHTS = ("g_mix", "w_in", "conv_w", "attn_sinks", "w_attn_proj", "w_conv_proj", "w_mix_out", "g_xattn", "g_mem",
            "w_xq", "w_xkv", "w_xo", "g_ffn", "w_ffn_in", "w_ffn_out", "g_final")


def _unstack(g, col_sharded):
    n, r, c = g.shape
    if col_sharded:
        return jnp.transpose(g, (1, 0, 2)).reshape(r, n * c)
    return g.reshape(n * r, c)


def _stack(w, col_sharded):
    r, c = w.shape
    if col_sharded:
        return jnp.transpose(w.reshape(r, N_DEV, c // N_DEV), (1, 0, 2))
    return w.reshape(N_DEV, r // N_DEV, c)


def kernel(x, mem, g_mix, w_in, conv_w, attn_sinks, w_attn_proj, w_conv_proj, w_mix_out, g_xattn, g_mem, w_xq, w_xkv, w_xo, g_ffn, w_ffn_in, w_ffn_out, g_final, loss_target, m_g_mix, m_w_in, m_conv_w, m_attn_sinks, m_w_attn_proj, m_w_conv_proj, m_w_mix_out, m_g_xattn, m_g_mem, m_w_xq, m_w_xkv, m_w_xo, m_g_ffn, m_w_ffn_in, m_w_ffn_out, m_g_final, v_g_mix, v_w_in, v_conv_w, v_attn_sinks, v_w_attn_proj, v_w_conv_proj, v_w_mix_out, v_g_xattn, v_g_mem, v_w_xq, v_w_xkv, v_w_xo, v_g_ffn, v_w_ffn_in, v_w_ffn_out, v_g_final):
    w_ = dict(g_mix=g_mix, w_in=w_in, conv_w=conv_w, attn_sinks=attn_sinks, w_attn_proj=w_attn_proj,
              w_conv_proj=w_conv_proj, w_mix_out=w_mix_out, g_xattn=g_xattn, g_mem=g_mem, w_xq=w_xq, w_xkv=w_xkv,
              w_xo=w_xo, g_ffn=g_ffn, w_ffn_in=w_ffn_in, w_ffn_out=w_ffn_out, g_final=g_final)
    m_ = dict(g_mix=m_g_mix, w_in=m_w_in, conv_w=m_conv_w, attn_sinks=m_attn_sinks, w_attn_proj=m_w_attn_proj,
              w_conv_proj=m_w_conv_proj, w_mix_out=m_w_mix_out, g_xattn=m_g_xattn, g_mem=m_g_mem, w_xq=m_w_xq,
              w_xkv=m_w_xkv, w_xo=m_w_xo, g_ffn=m_g_ffn, w_ffn_in=m_w_ffn_in, w_ffn_out=m_w_ffn_out,
              g_final=m_g_final)
    v_ = dict(g_mix=v_g_mix, w_in=v_w_in, conv_w=v_conv_w, attn_sinks=v_attn_sinks, w_attn_proj=v_w_attn_proj,
              w_conv_proj=v_w_conv_proj, w_mix_out=v_w_mix_out, g_xattn=v_g_xattn, g_mem=v_g_mem, w_xq=v_w_xq,
              w_xkv=v_w_xkv, w_xo=v_w_xo, g_ffn=v_g_ffn, w_ffn_in=v_w_ffn_in, w_ffn_out=v_w_ffn_out,
              g_final=v_g_final)
    t, d = x.shape[1], x.shape[2]
    nq = attn_sinks.shape[-1]
    cw_shard = conv_w.shape[-1]
    cw = cw_shard * N_DEV

    def two_d(a):
        return a.reshape(a.shape[-2], a.shape[-1]) if a.ndim == 3 else a.reshape(1, a.shape[-1])

    me = _me()
    col = (set(_COL_SHARDED) | {"conv_w8"}) - {"w_in"}

    shards = {"w_in": two_d(w_in).T.astype(BF)}
    first, token = _exchange_start(
        [[(shards[n], _landing(shards[n], me)) for n in g] for g in _GATHER_GROUPS[:1]], GATHER_CHIPS,
        name="gather_start_0")
    for n in _BIG:
        if n != "w_in":
            shards[n] = _to_bf16(two_d(w_[n]), name="cast_" + n, dep=token)
    shards["conv_w8"] = jnp.zeros((SUBLANES, cw_shard), F32).at[:3].set(two_d(conv_w))
    rest, token = _exchange_start(
        [[(shards[n], _landing(shards[n], me)) for n in g] for g in _GATHER_GROUPS[1:]], GATHER_CHIPS,
        name="gather_start_1", after=token)
    gathers = first + rest
    passes, full = {}, {}

    def group_of(name):
        return [name in g for g in _GATHER_GROUPS].index(True)

    def pre_w(name, after):
        gi = group_of(name)
        _, lands = _exchange_wait(gathers[gi], GATHER_CHIPS, after, name="gather_wait_%d" % gi)
        started, tok = _exchange_start([[(None, land) for land in lands]], GATHER_SIBLING,
                                       name="gather_pass_%d" % gi)
        passes[gi] = started[0]
        return tok

    def get_w(name, after):
        if name not in full:
            gi = group_of(name)
            _, lands = _exchange_wait(passes[gi], GATHER_SIBLING, after, name="gather_pass_wait_%d" % gi)
            for n, land in zip(_GATHER_GROUPS[gi], lands):
                full[n] = _unstack(land, n in col)
        return full[name]

    pending, scatters = {}, []

    def put_g(name, dw, stacked=False):
        pending[name] = dw if stacked else _stack(dw, name in col)
        gi = [name in g for g in _SCATTER_GROUPS].index(True)
        group = _SCATTER_GROUPS[gi]
        if not all(n in pending for n in group):
            return None
        pairs = [(pending[n], lax.empty(pending[n].shape, pending[n].dtype)) for n in group]
        started, tok = _exchange_start([pairs], SCATTER_DIRECT, name="scatter_start_%d" % gi)
        scatters.append((gi, started[0]))
        return tok

    gains = {n: two_d(w_[n]) for n in _GAINS}
    grad_x, small = _local_step(
        x[0], mem[0], loss_target[0], gains, attn_sinks.reshape(nq), w_attn_proj.shape[-2], cw, pre_w, get_w, put_g,
        dep0=token)

    grads, deltas, new_m, new_v = {}, {}, {}, {}
    me1 = me.reshape(1).astype(jnp.int32)
    after, halves = grad_x, []
    for gi, started in scatters:
        mine, parts = _exchange_wait(started, SCATTER_DIRECT, after, name="scatter_wait_%d" % gi)
        for n, own, p in zip(_SCATTER_GROUPS[gi], mine, parts):
            if n.startswith("w_in_"):
                halves.append(_sum_partials(p, own, me1, name="sum_" + n).T)
                after = halves[-1]
                if len(halves) < 2:
                    continue
                n, g = "w_in", jnp.concatenate(halves, axis=0)
                dl, nm, nv = _adamw_rows(two_d(w_[n]), g, two_d(m_[n]), two_d(v_[n]), name="adamw_" + n)
            else:
                g, dl, nm, nv = _adamw_sum(p, own, me1, two_d(w_[n]), two_d(m_[n]), two_d(v_[n]), name="adamw_" + n)
            shape = w_[n].shape
            grads[n], deltas[n], new_m[n], new_v[n] = (a.reshape(shape) for a in (g, dl, nm, nv))
            after = g

    parts = [(small[n], i, 1) for i, n in enumerate(_GAINS)]
    parts += [(small["attn_sinks"], 5, 1), (small["loss"], 6, 1), (small["conv_w8"], 8, 3)]
    red = _all_reduce_small(parts, 2 * SUBLANES, max(d, cw), name="reduce_small", dep=after)
    loss = red[6, 0]
    small_g = {n: red[i:i + 1, :d] for i, n in enumerate(_GAINS)}
    small_g["attn_sinks"] = red[5:6, :nq]
    small_g["conv_w"] = lax.dynamic_slice(red, (8, me * cw_shard), (3, cw_shard))
    for n in _GAINS + ("attn_sinks", "conv_w"):
        shape = w_[n].shape
        g = small_g[n]
        dl, nm, nv = _adamw_small(two_d(w_[n]), g, two_d(m_[n]), two_d(v_[n]), name="adamw_" + n)
        grads[n], deltas[n], new_m[n], new_v[n] = (a.reshape(shape) for a in (g, dl, nm, nv))

    return (loss, grad_x[None], *[grads[n] for n in _WEIGHTS], *[deltas[n] for n in _WEIGHTS],
            *[new_m[n] for n in _WEIGHTS], *[new_v[n] for n in _WEIGHTS])
```

```python
import functools
import math

import jax
import jax.numpy as jnp
from jax import lax
from jax.experimental import pallas as pl
from jax.experimental.pallas import tpu as pltpu

HEAD_DIM = 64
Q_PER_KV = 4
WINDOW = 128
X_HEAD_DIM = 128
ROPE_THETA = 10000.0
EPS = 1e-6
ADAM_LR = 0.001
ADAM_B1 = 0.9
ADAM_B2 = 0.999
ADAM_EPS = 1e-08
ADAM_WD = 0.01
ADAM_STEP = 10

N_DEV = 8
LANES = 128
SUBLANES = 8
VMEM_LIMIT_BYTES = 56 * 1024 * 1024
BF = jnp.bfloat16
F32 = jnp.float32
MESH = pl.DeviceIdType.MESH


def _cp(*sem):
    return pltpu.CompilerParams(dimension_semantics=sem, vmem_limit_bytes=VMEM_LIMIT_BYTES)


def _sigmoid(x):
    return 1.0 / (1.0 + jnp.exp(-x))


_DIMS = {
    "nn": (((1,), (0,)), ((), ())),
    "nt": (((1,), (1,)), ((), ())),
    "tn": (((0,), (0,)), ((), ())),
}


def _fit(dim, tile):
    if dim <= tile:
        return dim
    for t in range(tile // LANES * LANES, 0, -LANES):
        if dim % t == 0:
            return t
    return dim


def _mm(a, b, *, mode, tm, tn, tk, out_dtype, name, residual=None, dep=None, a_planes=1, b_planes=1,
        stacked=False, b_cols=None, a_cols=None):
    if a_planes > 1:
        assert mode == "nt"
        (_, m, kp), (n, k) = a.shape, b.shape
        assert kp * a_planes == k
    elif b_planes > 1:
        assert mode == "tn"
        (k, m), (_, k2, np_) = a.shape, b.shape
        n = np_ * b_planes
        assert k == k2
    elif mode == "nn":
        (m, k), (k2, n) = a.shape, b.shape
        assert k == k2, (name, a.shape, b.shape)
    elif mode == "nt":
        (m, k), (n, k2) = a.shape, b.shape
        if a_cols is not None:
            k = a_cols[1]
        assert k == k2, (name, a.shape, b.shape)
    else:
        (k, m), (k2, n) = a.shape, b.shape
        assert k == k2, (name, a.shape, b.shape)
    tm, tn, tk = _fit(m, tm), _fit(n // b_planes, tn), _fit(k // a_planes, tk)
    assert m % tm == 0 and (n // b_planes) % tn == 0 and (k // a_planes) % tk == 0, (name, m, n, k, tm, tn, tk)
    ka0 = 0
    if a_cols is not None:
        assert mode == "nt" and a_planes == 1 and a_cols[0] % tk == 0
        ka0 = a_cols[0] // tk
    j0 = 0
    if b_cols is not None:
        assert mode == "tn" and b_planes == 1 and b_cols[0] % tn == 0 and b_cols[1] % tn == 0
        j0, n = b_cols[0] // tn, b_cols[1]
    nk = k // tk
    nkp, njp = nk // a_planes, n // tn // b_planes
    if a_planes > 1:
        a_spec = pl.BlockSpec((None, tm, tk), lambda i, j, kk: (kk // nkp, i, kk % nkp))
    elif mode == "tn":
        a_spec = pl.BlockSpec((tk, tm), lambda i, j, kk: (kk, i))
    else:
        a_spec = pl.BlockSpec((tm, tk), lambda i, j, kk: (i, kk + ka0))
    if b_planes > 1:
        b_spec = pl.BlockSpec((None, tk, tn), lambda i, j, kk: (j // njp, kk, j % njp))
    elif mode == "nt":
        b_spec = pl.BlockSpec((tn, tk), lambda i, j, kk: (j, kk))
    else:
        b_spec = pl.BlockSpec((tk, tn), lambda i, j, kk: (kk, j + j0))
    if stacked:
        assert residual is None
        o_spec = pl.BlockSpec((None, tm, tn), lambda i, j, kk: (j, i, 0))
        out_shape = jax.ShapeDtypeStruct((n // tn, m, tn), out_dtype)
    else:
        o_spec = pl.BlockSpec((tm, tn), lambda i, j, kk: (i, j))
        out_shape = jax.ShapeDtypeStruct((m, n), out_dtype)
    dims = _DIMS[mode]
    has_res = residual is not None
    n_in = 2 + has_res + (dep is not None)

    def body(*refs):
        a_ref, b_ref, r_ref, o_ref = refs[0], refs[1], refs[2], refs[n_in]
        part = lax.dot_general(a_ref[...].astype(BF), b_ref[...].astype(BF), dims, preferred_element_type=F32)

        def finish(acc):
            if has_res:
                acc = r_ref[...] + acc
            o_ref[...] = acc.astype(out_dtype)

        if nk == 1:
            finish(part)
        else:
            acc_ref = refs[-1]
            kk = pl.program_id(2)

            @pl.when(kk == 0)
            def _():
                acc_ref[...] = part

            @pl.when(kk > 0)
            def _():
                acc_ref[...] += part

            @pl.when(kk == nk - 1)
            def _():
                finish(acc_ref[...])

    in_specs = [a_spec, b_spec] + ([o_spec] if has_res else [])
    args = (a, b) + ((residual,) if has_res else ())
    if dep is not None:
        in_specs.append(pl.BlockSpec(memory_space=pl.ANY))
        args += (dep,)
    return pl.pallas_call(
        body,
        name=name,
        grid=(m // tm, n // tn, nk),
        in_specs=in_specs,
        out_specs=o_spec,
        out_shape=out_shape,
        scratch_shapes=[pltpu.VMEM((tm, tn), F32)] if nk > 1 else [],
        compiler_params=_cp("parallel", "parallel", "arbitrary"),
    )(*args)


def _rms_fwd(h, g, *, name, tm=512, dep=None):
    t, d = h.shape
    tm = min(tm, t)

    def body(*refs):
        h_ref, g_ref, u_ref = refs[0], refs[1], refs[-1]
        hv = h_ref[...]
        r = lax.rsqrt(jnp.mean(hv * hv, axis=-1, keepdims=True) + EPS)
        u_ref[...] = ((hv * r) * g_ref[...]).astype(BF)

    in_specs = [pl.BlockSpec((tm, d), lambda i: (i, 0)), pl.BlockSpec((1, d), lambda i: (0, 0))]
    args = (h, g)
    if dep is not None:
        in_specs.append(pl.BlockSpec(memory_space=pl.ANY))
        args += (dep,)
    return pl.pallas_call(
        body,
        name=name,
        grid=(t // tm,),
        in_specs=in_specs,
        out_specs=pl.BlockSpec((tm, d), lambda i: (i, 0)),
        out_shape=jax.ShapeDtypeStruct((t, d), BF),
        compiler_params=_cp("parallel"),
    )(*args)


def _rms_bwd(du, h, g, dres, *, name, tm=256):
    t, d = h.shape
    tm = min(tm, t)
    want_dh = dres is not None

    def body(*refs):
        if want_dh:
            du_ref, h_ref, g_ref, dres_ref, dh_ref, dhb_ref, dg_ref = refs
        else:
            du_ref, h_ref, g_ref, dg_ref = refs
        i = pl.program_id(0)
        hv = h_ref[...]
        duv = du_ref[...]
        r = lax.rsqrt(jnp.mean(hv * hv, axis=-1, keepdims=True) + EPS)
        nv = hv * r
        if want_dh:
            gy = duv * g_ref[...]
            dh = dres_ref[...] + r * (gy - nv * jnp.mean(nv * gy, axis=-1, keepdims=True))
            dh_ref[...] = dh
            dhb_ref[...] = dh.astype(BF)

        @pl.when(i == 0)
        def _():
            dg_ref[...] = jnp.zeros_like(dg_ref)

        dg_ref[...] += jnp.sum(duv * nv, axis=0, keepdims=True)

    row = pl.BlockSpec((tm, d), lambda i: (i, 0))
    vec = pl.BlockSpec((1, d), lambda i: (0, 0))
    if want_dh:
        in_specs, args = [row, row, vec, row], (du, h, g, dres)
        out_specs = [row, row, vec]
        out_shape = [jax.ShapeDtypeStruct((t, d), F32), jax.ShapeDtypeStruct((t, d), BF),
                     jax.ShapeDtypeStruct((1, d), F32)]
    else:
        in_specs, args = [row, row, vec], (du, h, g)
        out_specs = [vec]
        out_shape = [jax.ShapeDtypeStruct((1, d), F32)]
    outs = pl.pallas_call(
        body,
        name=name,
        grid=(t // tm,),
        in_specs=in_specs,
        out_specs=out_specs,
        out_shape=out_shape,
        compiler_params=_cp("arbitrary"),
    )(*args)
    return (outs[0], outs[1], outs[2]) if want_dh else (None, None, outs[0])


def _loss_head(h, target, g, *, name, tm=256):
    t, d = h.shape
    tm = min(tm, t)

    def body(h_ref, t_ref, g_ref, dh_ref, dhb_ref, loss_ref, dg_ref):
        i = pl.program_id(0)
        hv = h_ref[...]
        gv = g_ref[...]
        r = lax.rsqrt(jnp.mean(hv * hv, axis=-1, keepdims=True) + EPS)
        nv = hv * r
        e = nv * gv - t_ref[...]
        per_tok = jnp.mean(e * e, axis=-1, keepdims=True)
        lp = 0.5 * jnp.sum(per_tok, axis=0, keepdims=True)
        dy = e * (1.0 / d)
        gy = dy * gv
        dh = r * (gy - nv * jnp.mean(nv * gy, axis=-1, keepdims=True))
        dh_ref[...] = dh
        dhb_ref[...] = dh.astype(BF)

        @pl.when(i == 0)
        def _():
            loss_ref[...] = jnp.zeros_like(loss_ref)
            dg_ref[...] = jnp.zeros_like(dg_ref)

        loss_ref[...] += jnp.broadcast_to(lp, loss_ref.shape)
        dg_ref[...] += jnp.sum(dy * nv, axis=0, keepdims=True)

    row = pl.BlockSpec((tm, d), lambda i: (i, 0))
    vec = pl.BlockSpec((1, d), lambda i: (0, 0))
    return pl.pallas_call(
        body,
        name=name,
        grid=(t // tm,),
        in_specs=[row, row, vec],
        out_specs=[row, row, pl.BlockSpec((SUBLANES, LANES), lambda i: (0, 0)), vec],
        out_shape=[
            jax.ShapeDtypeStruct((t, d), F32),
            jax.ShapeDtypeStruct((t, d), BF),
            jax.ShapeDtypeStruct((SUBLANES, LANES), F32),
            jax.ShapeDtypeStruct((1, d), F32),
        ],
        compiler_params=_cp("arbitrary"),
    )(h, target, g)


def _ffn_in_fwd(u, w, *, name, tm=1024, tn=512, dep=None):
    t, d = u.shape
    f = w.shape[1] // 2
    tm, tn = _fit(t, tm), _fit(f, tn)
    nf = f // tn

    def body(*refs):
        u_ref, wa_ref, wb_ref, hid_ref, act_ref = refs[0], refs[1], refs[2], refs[-2], refs[-1]
        uv = u_ref[...]
        a = jnp.dot(uv, wa_ref[...], preferred_element_type=F32)
        b = jnp.dot(uv, wb_ref[...], preferred_element_type=F32)
        hid_ref[0] = a.astype(BF)
        hid_ref[1] = b.astype(BF)
        act_ref[...] = ((a * _sigmoid(a)) * b).astype(BF)

    in_specs = [
        pl.BlockSpec((tm, d), lambda i, j: (i, 0)),
        pl.BlockSpec((d, tn), lambda i, j: (0, j)),
        pl.BlockSpec((d, tn), lambda i, j: (0, nf + j)),
    ]
    args = (u, w, w)
    if dep is not None:
        in_specs.append(pl.BlockSpec(memory_space=pl.ANY))
        args += (dep,)
    return pl.pallas_call(
        body,
        name=name,
        grid=(t // tm, nf),
        in_specs=in_specs,
        out_specs=[pl.BlockSpec((2, tm, tn), lambda i, j: (0, i, j)), pl.BlockSpec((tm, tn), lambda i, j: (i, j))],
        out_shape=[jax.ShapeDtypeStruct((2, t, f), BF), jax.ShapeDtypeStruct((t, f), BF)],
        compiler_params=_cp("parallel", "parallel"),
    )(*args)


def _ffn_out_bwd(dh, w_out, hid2, *, name, tm=1024, tn=512, dep=None):
    t, d = dh.shape
    f = w_out.shape[0]
    tm, tn = _fit(t, tm), _fit(f, tn)

    def body(*refs):
        dh_ref, w_ref, hid_ref, o_ref = refs[0], refs[1], refs[2], refs[-1]
        dact = lax.dot_general(dh_ref[...], w_ref[...], _DIMS["nt"], preferred_element_type=F32)
        a = hid_ref[0].astype(F32)
        b = hid_ref[1].astype(F32)
        sg = _sigmoid(a)
        o_ref[0] = (dact * b * (sg * (1.0 + a * (1.0 - sg)))).astype(BF)
        o_ref[1] = (dact * (a * sg)).astype(BF)

    pair = pl.BlockSpec((2, tm, tn), lambda i, j: (0, i, j))
    in_specs = [pl.BlockSpec((tm, d), lambda i, j: (i, 0)), pl.BlockSpec((tn, d), lambda i, j: (j, 0)), pair]
    args = (dh, w_out, hid2)
    if dep is not None:
        in_specs.append(pl.BlockSpec(memory_space=pl.ANY))
        args += (dep,)
    return pl.pallas_call(
        body,
        name=name,
        grid=(t // tm, f // tn),
        in_specs=in_specs,
        out_specs=pair,
        out_shape=jax.ShapeDtypeStruct((2, t, f), BF),
        compiler_params=_cp("parallel", "parallel"),
    )(*args)


def _gate_fwd(cy, w, proj, ya, *, ga0, gc0, name, tm=1024, tc=512):
    t, d = ya.shape
    kc = cy.shape[1]
    tm, tc = _fit(t, tm), math.gcd(tc, d, ga0, gc0)
    a0, c0 = ga0 // tc, gc0 // tc

    def body(cy_ref, w_ref, ga_ref, gc_ref, ya_ref, yc_ref, o_ref):
        yc = jnp.dot(cy_ref[...], w_ref[...], preferred_element_type=F32)
        yc_ref[...] = yc
        o_ref[...] = (_sigmoid(ga_ref[...]) * ya_ref[...] + _sigmoid(gc_ref[...]) * yc).astype(BF)

    blk = pl.BlockSpec((tm, tc), lambda i, j: (i, j))
    return pl.pallas_call(
        body,
        name=name,
        grid=(t // tm, d // tc),
        in_specs=[
            pl.BlockSpec((tm, kc), lambda i, j: (i, 0)),
            pl.BlockSpec((kc, tc), lambda i, j: (0, j)),
            pl.BlockSpec((tm, tc), lambda i, j: (i, a0 + j)),
            pl.BlockSpec((tm, tc), lambda i, j: (i, c0 + j)),
            blk,
        ],
        out_specs=[blk, blk],
        out_shape=[jax.ShapeDtypeStruct((t, d), F32), jax.ShapeDtypeStruct((t, d), BF)],
        compiler_params=_cp("parallel", "parallel"),
    )(cy, w, proj, proj, ya)


def _gate_bwd(dh, w, proj, ya, yc, *, ga0, gc0, name, tm=1024, tc=512):
    t, d = ya.shape
    tm, tc = _fit(t, tm), math.gcd(tc, d, ga0, gc0)
    a0, c0 = ga0 // tc, gc0 // tc

    def body(dh_ref, w_ref, ga_ref, gc_ref, ya_ref, yc_ref, dya_ref, dyc_ref, dga_ref, dgc_ref):
        dmv = lax.dot_general(dh_ref[...], w_ref[...], _DIMS["nt"], preferred_element_type=F32)
        sa = _sigmoid(ga_ref[...])
        sc = _sigmoid(gc_ref[...])
        dya_ref[...] = (dmv * sa).astype(BF)
        dyc_ref[...] = (dmv * sc).astype(BF)
        dga_ref[...] = (dmv * ya_ref[...] * (sa * (1.0 - sa))).astype(BF)
        dgc_ref[...] = (dmv * yc_ref[...] * (sc * (1.0 - sc))).astype(BF)

    blk = pl.BlockSpec((tm, tc), lambda i, j: (i, j))
    out = jax.ShapeDtypeStruct((t, d), BF)
    return pl.pallas_call(
        body,
        name=name,
        grid=(t // tm, d // tc),
        in_specs=[
            pl.BlockSpec((tm, d), lambda i, j: (i, 0)),
            pl.BlockSpec((tc, d), lambda i, j: (j, 0)),
            pl.BlockSpec((tm, tc), lambda i, j: (i, a0 + j)),
            pl.BlockSpec((tm, tc), lambda i, j: (i, c0 + j)),
            blk,
            blk,
        ],
        out_specs=[blk, blk, blk, blk],
        out_shape=[out, out, out, out],
        compiler_params=_cp("parallel", "parallel"),
    )(dh, w, proj, proj, ya, yc)


def _conv_taps(cz, czp, i):
    czp = czp * (i > 0).astype(F32)
    h1 = czp[SUBLANES - 1:SUBLANES, :]
    h2 = czp[SUBLANES - 2:SUBLANES - 1, :]
    row = lax.broadcasted_iota(jnp.int32, cz.shape, 0)
    s1 = jnp.where(row == 0, h1, pltpu.roll(cz, 1, 0))
    s2 = jnp.where(row == 0, h2, jnp.where(row == 1, h1, pltpu.roll(cz, 2, 0)))
    return s1, s2


def _conv_fwd(proj, w8, *, z0, gb0, gc0, cw, name, tm=512, tc=512):
    t = proj.shape[0]
    tm, tc = min(tm, t), math.gcd(tc, cw, z0, gb0, gc0)
    zb, bb, cb = z0 // tc, gb0 // tc, gc0 // tc
    rb = tm // SUBLANES

    def body(z_ref, gb_ref, gc_ref, zp_ref, gcp_ref, w_ref, o_ref):
        i = pl.program_id(0)
        cz = gc_ref[...] * z_ref[...]
        s1, s2 = _conv_taps(cz, gcp_ref[...] * zp_ref[...], i)
        w = w_ref[...]
        y = w[0:1, :] * s2 + w[1:2, :] * s1 + w[2:3, :] * cz
        o_ref[...] = (gb_ref[...] * y).astype(BF)

    def cur(b0):
        return pl.BlockSpec((tm, tc), lambda i, j: (i, b0 + j))

    def prev(b0):
        return pl.BlockSpec((SUBLANES, tc), lambda i, j: (jnp.maximum(i * rb - 1, 0), b0 + j))

    return pl.pallas_call(
        body,
        name=name,
        grid=(t // tm, cw // tc),
        in_specs=[cur(zb), cur(bb), cur(cb), prev(zb), prev(cb), pl.BlockSpec((SUBLANES, tc), lambda i, j: (0, j))],
        out_specs=pl.BlockSpec((tm, tc), lambda i, j: (i, j)),
        out_shape=jax.ShapeDtypeStruct((t, cw), BF),
        compiler_params=_cp("parallel", "parallel"),
    )(proj, proj, proj, proj, proj, w8)


def _conv_bwd(proj, w8, dcy, *, z0, gb0, gc0, cw, name, tm=512, tc=512):
    t = proj.shape[0]
    tm, tc = min(tm, t), math.gcd(tc, cw, z0, gb0, gc0)
    zb, bb, cb = z0 // tc, gb0 // tc, gc0 // tc
    rb = tm // SUBLANES
    nt = t // tm

    def body(z_ref, gb_ref, gc_ref, zp_ref, gcp_ref, d_ref, dn_ref, gbn_ref, w_ref, dz_ref, dgb_ref, dgc_ref, dw_ref):
        i = pl.program_id(1)
        z = z_ref[...]
        gc = gc_ref[...]
        gb = gb_ref[...]
        cz = gc * z
        s1, s2 = _conv_taps(cz, gcp_ref[...] * zp_ref[...], i)
        w = w_ref[...]
        w0, w1, w2 = w[0:1, :], w[1:2, :], w[2:3, :]
        yc = w0 * s2 + w1 * s1 + w2 * cz
        dcyv = d_ref[...]
        dgb_ref[...] = (dcyv * yc).astype(BF)
        dyc = dcyv * gb
        dycn = dn_ref[...] * gbn_ref[...] * (i < nt - 1).astype(F32)
        n1, n2 = dycn[0:1, :], dycn[1:2, :]
        row = lax.broadcasted_iota(jnp.int32, cz.shape, 0)
        a1 = jnp.where(row == tm - 1, n1, pltpu.roll(dyc, tm - 1, 0))
        a2 = jnp.where(row == tm - 1, n2, jnp.where(row == tm - 2, n1, pltpu.roll(dyc, tm - 2, 0)))
        dcz = w2 * dyc + w1 * a1 + w0 * a2
        dz_ref[...] = (dcz * gc).astype(BF)
        dgc_ref[...] = (dcz * z).astype(BF)
        dw0 = jnp.sum(dyc * s2, axis=0, keepdims=True)
        dw1 = jnp.sum(dyc * s1, axis=0, keepdims=True)
        dw2 = jnp.sum(dyc * cz, axis=0, keepdims=True)
        r8 = lax.broadcasted_iota(jnp.int32, (SUBLANES, tc), 0)
        upd = jnp.where(r8 == 0, dw0, jnp.where(r8 == 1, dw1, jnp.where(r8 == 2, dw2, 0.0)))

        @pl.when(i == 0)
        def _():
            dw_ref[...] = jnp.zeros_like(dw_ref)

        dw_ref[...] += upd

    def cur(b0):
        return pl.BlockSpec((tm, tc), lambda j, i: (i, b0 + j))

    def prev(b0):
        return pl.BlockSpec((SUBLANES, tc), lambda j, i: (jnp.maximum(i * rb - 1, 0), b0 + j))

    def nxt(b0):
        return pl.BlockSpec((SUBLANES, tc), lambda j, i: (jnp.minimum((i + 1) * rb, t // SUBLANES - 1), b0 + j))

    blk = pl.BlockSpec((tm, tc), lambda j, i: (i, j))
    w_spec = pl.BlockSpec((SUBLANES, tc), lambda j, i: (0, j))
    out = jax.ShapeDtypeStruct((t, cw), BF)
    return pl.pallas_call(
        body,
        name=name,
        grid=(cw // tc, nt),
        in_specs=[cur(zb), cur(bb), cur(cb), prev(zb), prev(cb), blk, nxt(0), nxt(bb), w_spec],
        out_specs=[blk, blk, blk, w_spec],
        out_shape=[out, out, out, jax.ShapeDtypeStruct((SUBLANES, cw), F32)],
        compiler_params=_cp("parallel", "arbitrary"),
    )(proj, proj, proj, proj, proj, dcy, dcy, proj, w8)


def _rot_half(x):
    lane = lax.broadcasted_iota(jnp.int32, x.shape, 1)
    first = (lane % HEAD_DIM) < (HEAD_DIM // 2)
    return jnp.where(first, pltpu.roll(x, LANES - HEAD_DIM // 2, 1), pltpu.roll(x, HEAD_DIM // 2, 1))


def _rope(x, c, s):
    parts = []
    for a in range(x.shape[1] // LANES):
        xa = x[:, a * LANES:(a + 1) * LANES]
        parts.append(xa * c + _rot_half(xa) * s)
    return parts[0] if len(parts) == 1 else jnp.concatenate(parts, axis=1)


def _rope_bwd(dy, c, s):
    parts = []
    for a in range(dy.shape[1] // LANES):
        da = dy[:, a * LANES:(a + 1) * LANES]
        parts.append(da * c + _rot_half(da * s))
    return parts[0] if len(parts) == 1 else jnp.concatenate(parts, axis=1)


def _window(i):
    b = WINDOW
    r = lax.broadcasted_iota(jnp.int32, (b, b), 0)
    c = lax.broadcasted_iota(jnp.int32, (b, b), 1)
    return c <= r, c <= r + jnp.where(i > 0, b, 0)


def _band_pick(x, tri):
    b = tri.shape[0]
    return jnp.where(tri, x[:, b:], x[:, :b])


def _band_spread(y, tri):
    return jnp.concatenate([jnp.where(tri, 0.0, y), jnp.where(tri, y, 0.0)], axis=1)


def _chunk(x, a):
    return x[:, a * LANES:(a + 1) * LANES]


def _kv_aligned(kp, kc, h):
    band = jnp.concatenate([_chunk(kp, h // 2), _chunk(kc, h // 2)], axis=0).astype(F32)
    swapped = pltpu.roll(band, HEAD_DIM, 1)
    return (band, swapped) if h % 2 == 0 else (swapped, band)


def _swa_fwd(proj, cosf, sinf, sinks, *, nq, name, dep=None):
    t = proj.shape[0]
    nkv = nq // Q_PER_KV
    aw, kw, b = nq * HEAD_DIM, nkv * HEAD_DIM, WINDOW
    nb = t // b
    kblk = aw // kw
    scale = HEAD_DIM ** -0.5

    def body(*refs):
        sink_ref, q_ref, kc_ref, kp_ref, vc_ref, vp_ref, cc_ref, cp_ref, sc_ref, sp_ref = refs[:10]
        o_ref, qr_ref, kr_ref, s_scr, p_scr = refs[-5:]
        i = pl.program_id(0)
        cc, sc, cpv, spv = cc_ref[...], sc_ref[...], cp_ref[...], sp_ref[...]
        qr = _rope(q_ref[...], cc, sc)
        kc = _rope(kc_ref[...], cc, sc)
        kp = _rope(kp_ref[...], cpv, spv)
        qr_ref[...] = qr.astype(BF)
        kr_ref[...] = kc.astype(BF)
        vc, vp = vc_ref[...], vp_ref[...]
        tri, ok = _window(i)
        lo = lax.broadcasted_iota(jnp.int32, (b, LANES), 1) < HEAD_DIM
        ks = [[x.astype(BF) for x in _kv_aligned(kp, kc, h)] for h in range(nkv)]
        vs = [[x.astype(BF) for x in _kv_aligned(vp, vc, h)] for h in range(nkv)]
        for hq in range(nq):
            a, par = hq // 2, hq % 2
            qm = jnp.where(lo if par == 0 else ~lo, _chunk(qr, a), 0.0).astype(BF)
            s_scr[hq] = _band_pick(
                lax.dot_general(qm, ks[hq // Q_PER_KV][par], _DIMS["nt"], preferred_element_type=F32), tri)
        for hq in range(nq):
            s = jnp.where(ok, s_scr[hq] * scale, -jnp.inf)
            sink = sink_ref[hq]
            m = jnp.maximum(jnp.max(s, axis=-1, keepdims=True), sink)
            p = jnp.exp(s - m)
            p = p / (jnp.sum(p, axis=-1, keepdims=True) + jnp.exp(sink - m))
            p_scr[hq] = _band_spread(p, tri).astype(BF)
        for a in range(nq // 2):
            o_par = [jnp.dot(p_scr[2 * a + par], vs[(2 * a) // Q_PER_KV][par], preferred_element_type=F32)
                     for par in range(2)]
            o_ref[:, a * LANES:(a + 1) * LANES] = jnp.where(lo, o_par[0], o_par[1]).astype(BF)

    def prev_i(i):
        return jnp.maximum(i - 1, 0)

    tab_c = pl.BlockSpec((b, LANES), lambda i: (i, 0))
    tab_p = pl.BlockSpec((b, LANES), lambda i: (prev_i(i), 0))
    in_specs = [
        pl.BlockSpec(memory_space=pltpu.SMEM),
        pl.BlockSpec((b, aw), lambda i: (i, 0)),
        pl.BlockSpec((b, kw), lambda i: (i, kblk)),
        pl.BlockSpec((b, kw), lambda i: (prev_i(i), kblk)),
        pl.BlockSpec((b, kw), lambda i: (i, kblk + 1)),
        pl.BlockSpec((b, kw), lambda i: (prev_i(i), kblk + 1)),
        tab_c,
        tab_p,
        tab_c,
        tab_p,
    ]
    args = (sinks, proj, proj, proj, proj, proj, cosf, cosf, sinf, sinf)
    if dep is not None:
        in_specs.append(pl.BlockSpec(memory_space=pl.ANY))
        args += (dep,)
    return pl.pallas_call(
        body,
        name=name,
        grid=(nb,),
        in_specs=in_specs,
        out_specs=[
            pl.BlockSpec((b, aw), lambda i: (i, 0)),
            pl.BlockSpec((b, aw), lambda i: (i, 0)),
            pl.BlockSpec((b, kw), lambda i: (i, 0)),
        ],
        out_shape=[
            jax.ShapeDtypeStruct((t, aw), BF),
            jax.ShapeDtypeStruct((t, aw), BF),
            jax.ShapeDtypeStruct((t, kw), BF),
        ],
        scratch_shapes=[pltpu.VMEM((nq, b, b), F32), pltpu.VMEM((nq, b, 2 * b), BF)],
        compiler_params=_cp("parallel"),
    )(*args)


def _swa_bwd(qr, kr, proj, do, cosf, sinf, sinks, *, nq, name):
    t = proj.shape[0]
    nkv = nq // Q_PER_KV
    aw, kw, b = nq * HEAD_DIM, nkv * HEAD_DIM, WINDOW
    nb = t // b
    kblk = aw // kw
    scale = HEAD_DIM ** -0.5

    def body(sink_ref, q_ref, kc_ref, kp_ref, vc_ref, vp_ref, do_ref, cc_ref, cp_ref, sc_ref, sp_ref,
             dq_ref, dk_ref, dv_ref, ds_ref, ck_ref, cv_ref, sacc_ref, s_scr, dp_scr, ds_scr, pf_scr):
        i = pl.program_id(0)

        @pl.when(i == 0)
        def _():
            ck_ref[...] = jnp.zeros_like(ck_ref)
            cv_ref[...] = jnp.zeros_like(cv_ref)
            sacc_ref[...] = jnp.zeros_like(sacc_ref)

        @pl.when(i < nb)
        def _():
            q = q_ref[...]
            kc, kp = kc_ref[...], kp_ref[...]
            vc, vp = vc_ref[...], vp_ref[...]
            dov = do_ref[...]
            tri, ok = _window(i)
            lane = lax.broadcasted_iota(jnp.int32, (b, LANES), 1)
            lo = lane < HEAD_DIM
            cc, sc = cc_ref[...], sc_ref[...]
            nch = kw // LANES
            row_lo = lax.broadcasted_iota(jnp.int32, (LANES, b), 0) < HEAD_DIM
            dk_ch = [jnp.zeros((LANES, 2 * b), F32) for _ in range(nch)]
            dv_ch = [jnp.zeros((LANES, 2 * b), F32) for _ in range(nch)]
            sacc = jnp.zeros((b, LANES), F32)
            ks = [[x.astype(BF) for x in _kv_aligned(kp, kc, h)] for h in range(nkv)]
            vs = [[x.astype(BF) for x in _kv_aligned(vp, vc, h)] for h in range(nkv)]
            for hq in range(nq):
                a, par, h = hq // 2, hq % 2, hq // Q_PER_KV
                mine = lo if par == 0 else ~lo
                qm = jnp.where(mine, _chunk(q, a).astype(F32), 0.0).astype(BF)
                dom = jnp.where(mine, _chunk(dov, a).astype(F32), 0.0).astype(BF)
                s_scr[hq] = _band_pick(lax.dot_general(qm, ks[h][par], _DIMS["nt"], preferred_element_type=F32), tri)
                dp_scr[hq] = _band_pick(
                    lax.dot_general(dom, vs[h][par], _DIMS["nt"], preferred_element_type=F32), tri)
            for hq in range(nq):
                s = jnp.where(ok, s_scr[hq] * scale, -jnp.inf)
                sink = sink_ref[hq]
                m = jnp.maximum(jnp.max(s, axis=-1, keepdims=True), sink)
                e = jnp.exp(s - m)
                es = jnp.exp(sink - m)
                zinv = 1.0 / (jnp.sum(e, axis=-1, keepdims=True) + es)
                p = e * zinv
                dp = dp_scr[hq]
                delta = jnp.sum(p * dp, axis=-1, keepdims=True)
                ds_scr[hq] = _band_spread(p * (dp - delta) * scale, tri).astype(BF)
                pf_scr[hq] = _band_spread(p, tri).astype(BF)
                sacc = sacc + jnp.where(lane == hq, -(es * zinv) * delta, 0.0)
            for a in range(nq // 2):
                h = (2 * a) // Q_PER_KV
                qa_t = _chunk(q, a).astype(F32).T
                doa_t = _chunk(dov, a).astype(F32).T
                dq_par = []
                for par in range(2):
                    hq = 2 * a + par
                    mine_t = row_lo if par == 0 else ~row_lo
                    qm_t = jnp.where(mine_t, qa_t, 0.0).astype(BF)
                    dom_t = jnp.where(mine_t, doa_t, 0.0).astype(BF)
                    dsv = ds_scr[hq]
                    dq_par.append(jnp.dot(dsv, ks[h][par], preferred_element_type=F32))
                    dkh = jnp.dot(qm_t, dsv, preferred_element_type=F32)
                    dvh = jnp.dot(dom_t, pf_scr[hq], preferred_element_type=F32)
                    if par != h % 2:
                        dkh = pltpu.roll(dkh, HEAD_DIM, 0)
                        dvh = pltpu.roll(dvh, HEAD_DIM, 0)
                    dk_ch[h // 2] = dk_ch[h // 2] + dkh
                    dv_ch[h // 2] = dv_ch[h // 2] + dvh
                dqa = jnp.where(lo, dq_par[0], dq_par[1])
                dq_ref[:, a * LANES:(a + 1) * LANES] = _rope_bwd(dqa, cc, sc).astype(BF)
            dk_ch = [x.T for x in dk_ch]
            dv_ch = [x.T for x in dv_ch]
            dk = dk_ch[0] if nch == 1 else jnp.concatenate(dk_ch, axis=1)
            dv = dv_ch[0] if nch == 1 else jnp.concatenate(dv_ch, axis=1)
            dk_ref[...] = _rope_bwd(ck_ref[...] + dk[:b, :], cp_ref[...], sp_ref[...]).astype(BF)
            dv_ref[...] = (cv_ref[...] + dv[:b, :]).astype(BF)
            ck_ref[...] = dk[b:, :]
            cv_ref[...] = dv[b:, :]
            sacc_ref[...] += sacc

        @pl.when(i == nb)
        def _():
            dk_ref[...] = _rope_bwd(ck_ref[...], cp_ref[...], sp_ref[...]).astype(BF)
            dv_ref[...] = cv_ref[...].astype(BF)
            ds_ref[...] = jnp.broadcast_to(jnp.sum(sacc_ref[...], axis=0, keepdims=True), ds_ref.shape)

    def cur_i(i):
        return jnp.minimum(i, nb - 1)

    def prev_i(i):
        return jnp.clip(i - 1, 0, nb - 1)

    tab_c = pl.BlockSpec((b, LANES), lambda i: (cur_i(i), 0))
    tab_p = pl.BlockSpec((b, LANES), lambda i: (prev_i(i), 0))
    return pl.pallas_call(
        body,
        name=name,
        grid=(nb + 1,),
        in_specs=[
            pl.BlockSpec(memory_space=pltpu.SMEM),
            pl.BlockSpec((b, aw), lambda i: (cur_i(i), 0)),
            pl.BlockSpec((b, kw), lambda i: (cur_i(i), 0)),
            pl.BlockSpec((b, kw), lambda i: (prev_i(i), 0)),
            pl.BlockSpec((b, kw), lambda i: (cur_i(i), kblk + 1)),
            pl.BlockSpec((b, kw), lambda i: (prev_i(i), kblk + 1)),
            pl.BlockSpec((b, aw), lambda i: (cur_i(i), 0)),
            tab_c,
            tab_p,
            tab_c,
            tab_p,
        ],
        out_specs=[
            pl.BlockSpec((b, aw), lambda i: (cur_i(i), 0)),
            pl.BlockSpec((b, kw), lambda i: (prev_i(i), 0)),
            pl.BlockSpec((b, kw), lambda i: (prev_i(i), 0)),
            pl.BlockSpec((SUBLANES, LANES), lambda i: (0, 0)),
        ],
        out_shape=[
            jax.ShapeDtypeStruct((t, aw), BF),
            jax.ShapeDtypeStruct((t, kw), BF),
            jax.ShapeDtypeStruct((t, kw), BF),
            jax.ShapeDtypeStruct((SUBLANES, LANES), F32),
        ],
        scratch_shapes=[pltpu.VMEM((b, kw), F32), pltpu.VMEM((b, kw), F32), pltpu.VMEM((b, LANES), F32),
                        pltpu.VMEM((nq, b, b), F32), pltpu.VMEM((nq, b, b), F32),
                        pltpu.VMEM((nq, b, 2 * b), BF), pltpu.VMEM((nq, b, 2 * b), BF)],
        compiler_params=_cp("arbitrary"),
    )(sinks, qr, kr, kr, proj, proj, do, cosf, cosf, sinf, sinf)


def _xattn_fwd(xq, kv, *, name, tq=512):
    t, xw = xq.shape
    mtok = kv.shape[0]
    tq = min(tq, t)
    nh = xw // X_HEAD_DIM
    scale = X_HEAD_DIM ** -0.5

    def body(q_ref, kv_ref, o_ref):
        q = q_ref[...]
        kvv = kv_ref[...]
        outs = []
        for h in range(nh):
            sl = slice(h * X_HEAD_DIM, (h + 1) * X_HEAD_DIM)
            k = kvv[:, sl]
            v = kvv[:, xw + h * X_HEAD_DIM: xw + (h + 1) * X_HEAD_DIM]
            s = lax.dot_general(q[:, sl], k, _DIMS["nt"], preferred_element_type=F32) * scale
            e = jnp.exp(s - jnp.max(s, axis=-1, keepdims=True))
            p = e / jnp.sum(e, axis=-1, keepdims=True)
            outs.append(jnp.dot(p.astype(BF), v, preferred_element_type=F32))
        o_ref[...] = jnp.concatenate(outs, axis=1).astype(BF)

    return pl.pallas_call(
        body,
        name=name,
        grid=(t // tq,),
        in_specs=[pl.BlockSpec((tq, xw), lambda i: (i, 0)), pl.BlockSpec((mtok, 2 * xw), lambda i: (0, 0))],
        out_specs=pl.BlockSpec((tq, xw), lambda i: (i, 0)),
        out_shape=jax.ShapeDtypeStruct((t, xw), BF),
        compiler_params=_cp("parallel"),
    )(xq, kv)


def _xattn_bwd(xq, kv, do, *, name, tq=512):
    t, xw = xq.shape
    mtok = kv.shape[0]
    tq = min(tq, t)
    nh = xw // X_HEAD_DIM
    scale = X_HEAD_DIM ** -0.5

    def body(q_ref, kv_ref, do_ref, dq_ref, dkv_ref):
        i = pl.program_id(0)
        q = q_ref[...]
        kvv = kv_ref[...]
        dov = do_ref[...]
        dqs, dks, dvs = [], [], []
        for h in range(nh):
            sl = slice(h * X_HEAD_DIM, (h + 1) * X_HEAD_DIM)
            k = kvv[:, sl]
            v = kvv[:, xw + h * X_HEAD_DIM: xw + (h + 1) * X_HEAD_DIM]
            qh, doh = q[:, sl], dov[:, sl]
            s = lax.dot_general(qh, k, _DIMS["nt"], preferred_element_type=F32) * scale
            e = jnp.exp(s - jnp.max(s, axis=-1, keepdims=True))
            p = e / jnp.sum(e, axis=-1, keepdims=True)
            dp = lax.dot_general(doh, v, _DIMS["nt"], preferred_element_type=F32)
            delta = jnp.sum(p * dp, axis=-1, keepdims=True)
            dsv = (p * (dp - delta) * scale).astype(BF)
            dqs.append(jnp.dot(dsv, k, preferred_element_type=F32))
            dks.append(lax.dot_general(dsv, qh, _DIMS["tn"], preferred_element_type=F32))
            dvs.append(lax.dot_general(p.astype(BF), doh, _DIMS["tn"], preferred_element_type=F32))
        dq_ref[...] = jnp.concatenate(dqs, axis=1).astype(BF)

        @pl.when(i == 0)
        def _():
            dkv_ref[...] = jnp.zeros_like(dkv_ref)

        dkv_ref[...] += jnp.concatenate(dks + dvs, axis=1)

    row = pl.BlockSpec((tq, xw), lambda i: (i, 0))
    full = pl.BlockSpec((mtok, 2 * xw), lambda i: (0, 0))
    return pl.pallas_call(
        body,
        name=name,
        grid=(t // tq,),
        in_specs=[row, full, row],
        out_specs=[row, full],
        out_shape=[jax.ShapeDtypeStruct((t, xw), BF), jax.ShapeDtypeStruct((mtok, 2 * xw), F32)],
        compiler_params=_cp("arbitrary"),
    )(xq, kv, do)


def _adam_math(w, g, m, v):
    m = ADAM_B1 * m + (1.0 - ADAM_B1) * g
    v = ADAM_B2 * v + (1.0 - ADAM_B2) * (g * g)
    m_hat = m / (1.0 - ADAM_B1 ** ADAM_STEP)
    v_hat = v / (1.0 - ADAM_B2 ** ADAM_STEP)
    delta = -ADAM_LR * (m_hat / (jnp.sqrt(v_hat) + ADAM_EPS) + ADAM_WD * w)
    return delta, m, v


def _row_tile(r, c, n_arrays, budget=24 * 1024 * 1024):
    step = 2 * SUBLANES
    cap = max(step, budget // (2 * n_arrays * c * 4))
    if r <= cap:
        return r
    best = None
    for tr in range(step, cap + 1, step):
        if r % tr == 0:
            best = tr
    assert best is not None, (r, c)
    return best


def _adamw_sum(parts, own, me, w, m, v, *, name):
    _, r, c = parts.shape
    tr = _row_tile(r, c, 12)

    def body(me_ref, p_ref, own_ref, w_ref, m_ref, v_ref, g_ref, d_ref, nm_ref, nv_ref):
        mine = jnp.full((tr, c), me_ref[0], jnp.int32)
        g = None
        for s in range(N_DEV):
            term = jnp.where(mine == s, own_ref[...], p_ref[s]).astype(F32)
            g = term if g is None else g + term
        g_ref[...] = g
        d_ref[...], nm_ref[...], nv_ref[...] = _adam_math(w_ref[...], g, m_ref[...], v_ref[...])

    blk = pl.BlockSpec((tr, c), lambda i, me_ref: (i, 0))
    out = jax.ShapeDtypeStruct((r, c), F32)
    return pl.pallas_call(
        body,
        name=name,
        grid_spec=pltpu.PrefetchScalarGridSpec(
            num_scalar_prefetch=1,
            grid=(r // tr,),
            in_specs=[
                pl.BlockSpec((N_DEV, tr, c), lambda i, me_ref: (0, i, 0)),
                pl.BlockSpec((None, tr, c), lambda i, me_ref: (me_ref[0], i, 0)),
                blk, blk, blk,
            ],
            out_specs=[blk, blk, blk, blk],
        ),
        out_shape=[out, out, out, out],
        compiler_params=_cp("parallel"),
    )(me, parts, own, w, m, v)


def _to_bf16(a, *, name, dep=None):
    r, c = a.shape
    tr = _row_tile(r, c, 2)

    def body(*refs):
        refs[-1][...] = refs[0][...].astype(BF)

    blk = pl.BlockSpec((tr, c), lambda i: (i, 0))
    in_specs, args = [blk], (a,)
    if dep is not None:
        in_specs.append(pl.BlockSpec(memory_space=pl.ANY))
        args += (dep,)
    return pl.pallas_call(
        body,
        name=name,
        grid=(r // tr,),
        in_specs=in_specs,
        out_specs=blk,
        out_shape=jax.ShapeDtypeStruct((r, c), BF),
        compiler_params=_cp("parallel"),
    )(*args)


def _sum_partials(parts, own, me, *, name):
    _, r, c = parts.shape
    tr = _row_tile(r, c, 6)

    def body(me_ref, p_ref, own_ref, g_ref):
        mine = jnp.full((tr, c), me_ref[0], jnp.int32)
        g = None
        for s in range(N_DEV):
            term = jnp.where(mine == s, own_ref[...], p_ref[s]).astype(F32)
            g = term if g is None else g + term
        g_ref[...] = g

    return pl.pallas_call(
        body,
        name=name,
        grid_spec=pltpu.PrefetchScalarGridSpec(
            num_scalar_prefetch=1,
            grid=(r // tr,),
            in_specs=[
                pl.BlockSpec((N_DEV, tr, c), lambda i, me_ref: (0, i, 0)),
                pl.BlockSpec((None, tr, c), lambda i, me_ref: (me_ref[0], i, 0)),
            ],
            out_specs=pl.BlockSpec((tr, c), lambda i, me_ref: (i, 0)),
        ),
        out_shape=jax.ShapeDtypeStruct((r, c), F32),
        compiler_params=_cp("parallel"),
    )(me, parts, own)


def _adamw_rows(w, g, m, v, *, name):
    r, c = w.shape
    tr = _row_tile(r, c, 7)

    def body(w_ref, g_ref, m_ref, v_ref, d_ref, nm_ref, nv_ref):
        d_ref[...], nm_ref[...], nv_ref[...] = _adam_math(w_ref[...], g_ref[...], m_ref[...], v_ref[...])

    blk = pl.BlockSpec((tr, c), lambda i: (i, 0))
    out = jax.ShapeDtypeStruct((r, c), F32)
    return pl.pallas_call(
        body,
        name=name,
        grid=(r // tr,),
        in_specs=[blk, blk, blk, blk],
        out_specs=[blk, blk, blk],
        out_shape=[out, out, out],
        compiler_params=_cp("parallel"),
    )(w, g, m, v)


def _adamw_small(w, g, m, v, *, name):
    def body(w_ref, g_ref, m_ref, v_ref, d_ref, nm_ref, nv_ref):
        d_ref[...], nm_ref[...], nv_ref[...] = _adam_math(w_ref[...], g_ref[...], m_ref[...], v_ref[...])

    out = jax.ShapeDtypeStruct(w.shape, F32)
    return pl.pallas_call(body, name=name, out_shape=[out, out, out])(w, g, m, v)


def _mesh_pos():
    x, y, c = lax.axis_index("x"), lax.axis_index("y"), lax.axis_index("c")
    return x, y, c


def _peer(x, y, c, mask):
    px = 1 - x if mask & 4 else x
    py = 1 - y if mask & 2 else y
    pc = 1 - c if mask & 1 else c
    return (px, py, pc), 4 * px + 2 * py + pc


_HBM = pl.BlockSpec(memory_space=pltpu.HBM)
_SEM = pl.BlockSpec(memory_space=pltpu.SEMAPHORE)
_EFFECT = pltpu.SideEffectType.DATAFLOW_SIDE_EFFECTING


def _me():
    return 4 * lax.axis_index("x") + 2 * lax.axis_index("y") + lax.axis_index("c")


def _landing(own, me):
    land = lax.empty((N_DEV,) + own.shape, own.dtype)
    return lax.dynamic_update_slice(land, own[None], (me, 0, 0))


_ALL = tuple(range(1, N_DEV))
_CHIPS = (2, 4, 6)
GATHER_DIRECT = tuple((m, None, 0, m) for m in _ALL)
SCATTER_DIRECT = tuple((m, m, 0, m) for m in _ALL)
GATHER_CHIPS = tuple((m, None, 0, m) for m in (1,) + _CHIPS)
GATHER_SIBLING = tuple((1, m, m, m ^ 1) for m in _CHIPS)


def _copy(src, land, send_sem, recv_sem, sem, x, y, c, entry, arriving):
    to, src_m, dst_m, arr_m = entry
    peer, _ = _peer(x, y, c, to)
    blk = lambda m: _peer(x, y, c, m)[1]
    return pltpu.make_async_remote_copy(
        src_ref=src if src_m is None else src.at[blk(src_m)],
        dst_ref=land.at[blk(arr_m if arriving else dst_m)],
        send_sem=send_sem.at[sem], recv_sem=recv_sem.at[sem], device_id=peer, device_id_type=MESH)


def _exchange_start(groups, plan, *, name, after=None):
    flat = [p for g in groups for p in g]
    from_land = flat[0][0] is None
    n, ng, nc = len(flat), len(groups), len(plan)
    n_buf = n if from_land else 2 * n

    def body(*refs):
        lands = refs[:n] if from_land else refs[n:2 * n]
        srcs = lands if from_land else refs[:n]
        sems = refs[n_buf + (after is not None):n_buf + (after is not None) + 2 * ng]
        token = refs[-1]
        x, y, c = _mesh_pos()
        w = 0
        for gi, g in enumerate(groups):
            for wi in range(len(g)):
                for k, entry in enumerate(plan):
                    _copy(srcs[w], lands[w], sems[2 * gi], sems[2 * gi + 1], wi * nc + k, x, y, c, entry,
                          False).start()
                w += 1
        token[...] = jnp.zeros_like(token)

    sem_shapes = []
    for g in groups:
        sem_shapes += [pltpu.SemaphoreType.DMA((len(g) * nc,))] * 2
    args = [] if from_land else [pltpu.with_memory_space_constraint(s, pltpu.HBM) for s, _ in flat]
    args += [pltpu.with_memory_space_constraint(l, pltpu.HBM) for _, l in flat]
    extra = [] if after is None else [after]
    outs = pl.pallas_call(
        body,
        name=name,
        in_specs=[_HBM] * n_buf + [pl.BlockSpec(memory_space=pl.ANY)] * len(extra),
        out_specs=[_SEM] * (2 * ng) + [_HBM] * n_buf + [pl.BlockSpec(memory_space=pltpu.VMEM)],
        out_shape=sem_shapes + [pltpu.HBM(a.shape, a.dtype) for a in args]
        + [jax.ShapeDtypeStruct((SUBLANES, LANES), F32)],
        input_output_aliases={i: 2 * ng + i for i in range(n_buf)},
        compiler_params=pltpu.CompilerParams(has_side_effects=_EFFECT),
    )(*args, *extra)
    sems, thru, token = outs[:2 * ng], outs[2 * ng:2 * ng + n_buf], outs[-1]
    res, w = [], 0
    for gi, g in enumerate(groups):
        m = len(g)
        srcs = [None] * m if from_land else list(thru[w:w + m])
        lands = list(thru[w:w + m]) if from_land else list(thru[n + w:n + w + m])
        res.append((sems[2 * gi], sems[2 * gi + 1], srcs, lands))
        w += m
    return res, token


def _exchange_wait(group, plan, after, *, name):
    send_sems, recv_sems, srcs_in, lands_in = group
    n, nc = len(lands_in), len(plan)
    from_land = srcs_in[0] is None
    n_buf = n if from_land else 2 * n

    def body(*refs):
        lands = refs[:n] if from_land else refs[n:2 * n]
        srcs = lands if from_land else refs[:n]
        send_sem, recv_sem = refs[n_buf], refs[n_buf + 1]
        x, y, c = _mesh_pos()
        for w in range(n):
            for k, entry in enumerate(plan):
                cp = _copy(srcs[w], lands[w], send_sem, recv_sem, w * nc + k, x, y, c, entry, True)
                cp.wait_send()
                cp.wait_recv()

    bufs = lands_in if from_land else srcs_in + lands_in
    outs = pl.pallas_call(
        body,
        name=name,
        in_specs=[_HBM] * n_buf + [_SEM, _SEM, pl.BlockSpec(memory_space=pl.ANY)],
        out_specs=[_HBM] * n_buf,
        out_shape=[pltpu.HBM(a.shape, a.dtype) for a in bufs],
        input_output_aliases={i: i for i in range(n_buf)},
        compiler_params=pltpu.CompilerParams(has_side_effects=_EFFECT),
    )(*bufs, send_sems, recv_sems, after)
    if from_land:
        return [None] * n, list(outs)
    return list(outs[:n]), list(outs[n:])


def _all_reduce_small(parts, rows, width, *, name, dep=None):
    n = len(parts)

    def body(*refs):
        ins = refs[:n]
        o_ref, pack_ref, buf_ref, send_sems, recv_sems = refs[-5:]
        x, y, c_ = _mesh_pos()
        me = 4 * x + 2 * y + c_
        pack_ref[...] = jnp.zeros_like(pack_ref)
        for ref, (arr, r0, nr) in zip(ins, parts):
            pack_ref[r0:r0 + nr, 0:arr.shape[1]] = ref[0:nr, :]
        sends, recvs = [], []
        for k in range(N_DEV - 1):
            peer, pidx = _peer(x, y, c_, k + 1)
            cp = pltpu.make_async_remote_copy(
                src_ref=pack_ref, dst_ref=buf_ref.at[me], send_sem=send_sems.at[k], recv_sem=recv_sems.at[k],
                device_id=peer, device_id_type=MESH)
            cp.start()
            sends.append(cp)
            recvs.append(pltpu.make_async_remote_copy(
                src_ref=pack_ref, dst_ref=buf_ref.at[pidx], send_sem=send_sems.at[k], recv_sem=recv_sems.at[k],
                device_id=peer, device_id_type=MESH))
        buf_ref[me] = pack_ref[...]
        for rc in recvs:
            rc.wait_recv()
        for cp in sends:
            cp.wait_send()
        acc = buf_ref[0]
        for s in range(1, N_DEV):
            acc = acc + buf_ref[s]
        o_ref[...] = acc

    vmem = pl.BlockSpec(memory_space=pltpu.VMEM)
    in_specs = [vmem] * n
    args = [p[0] for p in parts]
    if dep is not None:
        in_specs.append(pl.BlockSpec(memory_space=pl.ANY))
        args.append(dep)
    return pl.pallas_call(
        body,
        name=name,
        in_specs=in_specs,
        out_specs=vmem,
        out_shape=jax.ShapeDtypeStruct((rows, width), F32),
        scratch_shapes=[
            pltpu.VMEM((rows, width), F32),
            pltpu.VMEM((N_DEV, rows, width), F32),
            pltpu.SemaphoreType.DMA((N_DEV - 1,)),
            pltpu.SemaphoreType.DMA((N_DEV - 1,)),
        ],
    )(*args)


def _rope_tables(t):
    half = HEAD_DIM // 2
    inv_freq = ROPE_THETA ** (-jnp.arange(half, dtype=F32) / half)
    ang = jnp.arange(t, dtype=jnp.int32).astype(F32)[:, None] * inv_freq[None, :]
    cos, sin = jnp.cos(ang), jnp.sin(ang)
    cosf = jnp.concatenate([cos, cos, cos, cos], axis=1)
    sinf = jnp.concatenate([-sin, sin, -sin, sin], axis=1)
    return cosf, sinf


def _local_step(x, mem, target, gains, sinks, aw, cw, pre_w, get_w, put_g, dep0=None):
    t, d = x.shape
    nq = aw // HEAD_DIM
    kw = aw // Q_PER_KV
    z0 = aw + 2 * kw
    gb0, gc0 = z0 + cw, z0 + 2 * cw
    ga0 = z0 + 3 * cw
    gcm0 = ga0 + d
    cosf, sinf = _rope_tables(t)

    u1 = _rms_fwd(x, gains["g_mix"], name="rms_mix", dep=dep0)
    mem_n = _rms_fwd(mem, gains["g_mem"], name="rms_mem", dep=dep0)
    pre_w("w_in", u1)
    w_in_t = get_w("w_in", u1)
    proj = _mm(u1, w_in_t, mode="nt", tm=1024, tn=512, tk=2048, out_dtype=F32, name="mm_in")
    o_attn, q_rot, k_rot = _swa_fwd(proj, cosf, sinf, sinks, nq=nq, name="swa_fwd", dep=pre_w("conv_w8", proj))
    conv_w8 = get_w("conv_w8", o_attn)
    w_attn_proj, w_conv_proj, w_mix_out = (get_w(n, o_attn) for n in ("w_attn_proj", "w_conv_proj", "w_mix_out"))
    w_xq, w_xkv, w_xo = (get_w(n, o_attn) for n in ("w_xq", "w_xkv", "w_xo"))
    y_attn = _mm(o_attn, w_attn_proj, mode="nn", tm=1024, tn=1024, tk=1024, out_dtype=F32, name="mm_attn_proj")
    cy = _conv_fwd(proj, conv_w8, z0=z0, gb0=gb0, gc0=gc0, cw=cw, name="conv_fwd")
    y_conv, merged = _gate_fwd(cy, w_conv_proj, proj, y_attn, ga0=ga0, gc0=gcm0, name="mm_conv_proj")
    h1 = _mm(merged, w_mix_out, mode="nn", tm=1024, tn=1024, tk=2048, out_dtype=F32, name="mm_mix_out", residual=x)
    u2 = _rms_fwd(h1, gains["g_xattn"], name="rms_xattn", dep=pre_w("w_ffn_in", h1))
    xq = _mm(u2, w_xq, mode="nn", tm=1024, tn=512, tk=2048, out_dtype=BF, name="mm_xq")
    kv = _mm(mem_n, w_xkv, mode="nn", tm=256, tn=1024, tk=2048, out_dtype=BF, name="mm_xkv")
    o_x = _xattn_fwd(xq, kv, name="xattn_fwd")
    h2 = _mm(o_x, w_xo, mode="nn", tm=1024, tn=1024, tk=512, out_dtype=F32, name="mm_xo", residual=h1)
    u3 = _rms_fwd(h2, gains["g_ffn"], name="rms_ffn")
    w_ffn_in = get_w("w_ffn_in", xq)
    hid2, act = _ffn_in_fwd(u3, w_ffn_in, name="mm_ffn_in", dep=pre_w("w_ffn_out", u3))
    w_ffn_out = get_w("w_ffn_out", act)
    h3 = _mm(act, w_ffn_out, mode="nn", tm=512, tn=1024, tk=8192, out_dtype=F32, name="mm_ffn_out", residual=h2)

    tt = 8192
    dh3, dh3b, loss_tile, dg_final = _loss_head(h3, target, gains["g_final"], name="loss_head")
    tok = put_g("w_ffn_out", _mm(act, dh3b, mode="tn", tm=512, tn=1024, tk=tt, out_dtype=BF, name="mm_dw_ffn_out"))
    dhid2 = _ffn_out_bwd(dh3b, w_ffn_out, hid2, name="mm_dact", dep=tok)
    f2 = w_ffn_in.shape[1]
    tok = put_g("w_ffn_in", _mm(u3, dhid2, mode="tn", tm=1024, tn=f2 // N_DEV, tk=tt, out_dtype=BF,
                                name="mm_dw_ffn_in", b_planes=2, stacked=True), stacked=True)
    du3 = _mm(dhid2, w_ffn_in, mode="nt", tm=1024, tn=1024, tk=2816, out_dtype=F32, name="mm_du3", dep=tok,
              a_planes=2)
    dh2, dh2b, dg_ffn = _rms_bwd(du3, h2, gains["g_ffn"], dh3, name="rms_ffn_bwd")
    put_g("w_xo", _mm(o_x, dh2b, mode="tn", tm=512, tn=d // N_DEV, tk=tt, out_dtype=BF, name="mm_dw_xo",
                      stacked=True), stacked=True)
    do_x = _mm(dh2b, w_xo, mode="nt", tm=1024, tn=512, tk=2048, out_dtype=BF, name="mm_do_x")
    dxq, dkv = _xattn_bwd(xq, kv, do_x, name="xattn_bwd")
    put_g("w_xkv", _mm(mem_n, dkv, mode="tn", tm=1024, tn=1024, tk=256, out_dtype=BF, name="mm_dw_xkv"))
    tok = put_g("w_xq", _mm(u2, dxq, mode="tn", tm=1024, tn=512, tk=tt, out_dtype=BF, name="mm_dw_xq"))
    du2 = _mm(dxq, w_xq, mode="nt", tm=1024, tn=1024, tk=512, out_dtype=F32, name="mm_du2", dep=tok)
    dmem_n = _mm(dkv, w_xkv, mode="nt", tm=256, tn=1024, tk=1024, out_dtype=F32, name="mm_dmem")
    _, _, dg_mem = _rms_bwd(dmem_n, mem, gains["g_mem"], None, name="rms_mem_bwd")
    dh1, dh1b, dg_xattn = _rms_bwd(du2, h1, gains["g_xattn"], dh2, name="rms_xattn_bwd")
    put_g("w_mix_out", _mm(merged, dh1b, mode="tn", tm=1024, tn=1024, tk=tt, out_dtype=BF, name="mm_dw_mix_out"))
    dya, dyc, dga, dgc = _gate_bwd(dh1b, w_mix_out, proj, y_attn, y_conv, ga0=ga0, gc0=gcm0, name="mm_dmerged")
    put_g("w_attn_proj", _mm(o_attn, dya, mode="tn", tm=1024, tn=d // N_DEV, tk=tt, out_dtype=BF,
                             name="mm_dw_attn_proj", stacked=True), stacked=True)
    do_attn = _mm(dya, w_attn_proj, mode="nt", tm=1024, tn=1024, tk=2048, out_dtype=BF, name="mm_do_attn")
    tok = put_g("w_conv_proj", _mm(cy, dyc, mode="tn", tm=1024, tn=d // N_DEV, tk=tt, out_dtype=BF,
                                   name="mm_dw_conv_proj", stacked=True), stacked=True)
    dcy = _mm(dyc, w_conv_proj, mode="nt", tm=1024, tn=1024, tk=2048, out_dtype=F32, name="mm_dcy", dep=tok)
    dz, dgb, dgcv, dconv_w8 = _conv_bwd(proj, conv_w8, dcy, z0=z0, gb0=gb0, gc0=gc0, cw=cw, name="conv_bwd")
    dq, dk, dv, dsink_tile = _swa_bwd(q_rot, k_rot, proj, do_attn, cosf, sinf, sinks, nq=nq, name="swa_bwd")
    dproj = jnp.concatenate([dq, dk, dv, dz, dgb, dgcv, dga, dgc], axis=1)
    for hi in range(2):
        tok = put_g("w_in_%d" % hi, _mm(dproj, u1, mode="tn", tm=512, tn=d // 2, tk=tt, out_dtype=BF,
                                        name="mm_dw_in_%d" % hi, b_cols=(hi * (d // 2), d // 2), dep=tok))
    du1 = _mm(dproj, w_in_t, mode="nn", tm=512, tn=1024, tk=4352, out_dtype=F32, name="mm_du1", dep=tok)
    grad_x, _, dg_mix = _rms_bwd(du1, x, gains["g_mix"], dh1, name="rms_mix_bwd")

    small = {
        "g_mix": dg_mix, "g_xattn": dg_xattn, "g_mem": dg_mem, "g_ffn": dg_ffn, "g_final": dg_final,
        "attn_sinks": dsink_tile, "conv_w8": dconv_w8, "loss": loss_tile,
    }
    return grad_x, small


_COL_SHARDED = ("w_in", "w_attn_proj", "w_conv_proj", "w_xo", "w_ffn_in")
_ROW_SHARDED = ("w_mix_out", "w_xq", "w_xkv", "w_ffn_out")
_BIG = _COL_SHARDED + _ROW_SHARDED
_GAINS = ("g_mix", "g_xattn", "g_mem", "g_ffn", "g_final")
_GATHER_GROUPS = (("w_in",), ("conv_w8", "w_attn_proj", "w_conv_proj", "w_mix_out", "w_xq", "w_xkv", "w_xo"),
                  ("w_ffn_in",), ("w_ffn_out",))
_SCATTER_GROUPS = (("w_ffn_out",), ("w_ffn_in",), ("w_xo", "w_xq", "w_xkv"),
                   ("w_mix_out", "w_attn_proj", "w_conv_proj"), ("w_in_0",), ("w_in_1",))
_WEIGHTS = ("g_mix", "w_in", "conv_w", "attn_sinks", "w_attn_proj", "w_conv_proj", "w_mix_out", "g_xattn", "g_mem",
            "w_xq", "w_xkv", "w_xo", "g_ffn", "w_ffn_in", "w_ffn_out", "g_final")


def _unstack(g, col_sharded):
    n, r, c = g.shape
    if col_sharded:
        return jnp.transpose(g, (1, 0, 2)).reshape(r, n * c)
    return g.reshape(n * r, c)


def _stack(w, col_sharded):
    r, c = w.shape
    if col_sharded:
        return jnp.transpose(w.reshape(r, N_DEV, c // N_DEV), (1, 0, 2))
    return w.reshape(N_DEV, r // N_DEV, c)


def kernel(x, mem, g_mix, w_in, conv_w, attn_sinks, w_attn_proj, w_conv_proj, w_mix_out, g_xattn, g_mem, w_xq, w_xkv, w_xo, g_ffn, w_ffn_in, w_ffn_out, g_final, loss_target, m_g_mix, m_w_in, m_conv_w, m_attn_sinks, m_w_attn_proj, m_w_conv_proj, m_w_mix_out, m_g_xattn, m_g_mem, m_w_xq, m_w_xkv, m_w_xo, m_g_ffn, m_w_ffn_in, m_w_ffn_out, m_g_final, v_g_mix, v_w_in, v_conv_w, v_attn_sinks, v_w_attn_proj, v_w_conv_proj, v_w_mix_out, v_g_xattn, v_g_mem, v_w_xq, v_w_xkv, v_w_xo, v_g_ffn, v_w_ffn_in, v_w_ffn_out, v_g_final):
    w_ = dict(g_mix=g_mix, w_in=w_in, conv_w=conv_w, attn_sinks=attn_sinks, w_attn_proj=w_attn_proj,
              w_conv_proj=w_conv_proj, w_mix_out=w_mix_out, g_xattn=g_xattn, g_mem=g_mem, w_xq=w_xq, w_xkv=w_xkv,
              w_xo=w_xo, g_ffn=g_ffn, w_ffn_in=w_ffn_in, w_ffn_out=w_ffn_out, g_final=g_final)
    m_ = dict(g_mix=m_g_mix, w_in=m_w_in, conv_w=m_conv_w, attn_sinks=m_attn_sinks, w_attn_proj=m_w_attn_proj,
              w_conv_proj=m_w_conv_proj, w_mix_out=m_w_mix_out, g_xattn=m_g_xattn, g_mem=m_g_mem, w_xq=m_w_xq,
              w_xkv=m_w_xkv, w_xo=m_w_xo, g_ffn=m_g_ffn, w_ffn_in=m_w_ffn_in, w_ffn_out=m_w_ffn_out,
              g_final=m_g_final)
    v_ = dict(g_mix=v_g_mix, w_in=v_w_in, conv_w=v_conv_w, attn_sinks=v_attn_sinks, w_attn_proj=v_w_attn_proj,
              w_conv_proj=v_w_conv_proj, w_mix_out=v_w_mix_out, g_xattn=v_g_xattn, g_mem=v_g_mem, w_xq=v_w_xq,
              w_xkv=v_w_xkv, w_xo=v_w_xo, g_ffn=v_g_ffn, w_ffn_in=v_w_ffn_in, w_ffn_out=v_w_ffn_out,
              g_final=v_g_final)
    t, d = x.shape[1], x.shape[2]
    nq = attn_sinks.shape[-1]
    cw_shard = conv_w.shape[-1]
    cw = cw_shard * N_DEV

    def two_d(a):
        return a.reshape(a.shape[-2], a.shape[-1]) if a.ndim == 3 else a.reshape(1, a.shape[-1])

    me = _me()
    col = (set(_COL_SHARDED) | {"conv_w8"}) - {"w_in"}

    shards = {"w_in": two_d(w_in).T.astype(BF)}
    first, token = _exchange_start(
        [[(shards[n], _landing(shards[n], me)) for n in g] for g in _GATHER_GROUPS[:1]], GATHER_CHIPS,
        name="gather_start_0")
    for n in _BIG:
        if n != "w_in":
            shards[n] = _to_bf16(two_d(w_[n]), name="cast_" + n, dep=token)
    shards["conv_w8"] = jnp.zeros((SUBLANES, cw_shard), F32).at[:3].set(two_d(conv_w))
    rest, token = _exchange_start(
        [[(shards[n], _landing(shards[n], me)) for n in g] for g in _GATHER_GROUPS[1:]], GATHER_CHIPS,
        name="gather_start_1", after=token)
    gathers = first + rest
    passes, full = {}, {}

    def group_of(name):
        return [name in g for g in _GATHER_GROUPS].index(True)

    def pre_w(name, after):
        gi = group_of(name)
        _, lands = _exchange_wait(gathers[gi], GATHER_CHIPS, after, name="gather_wait_%d" % gi)
        started, tok = _exchange_start([[(None, land) for land in lands]], GATHER_SIBLING,
                                       name="gather_pass_%d" % gi)
        passes[gi] = started[0]
        return tok

    def get_w(name, after):
        if name not in full:
            gi = group_of(name)
            _, lands = _exchange_wait(passes[gi], GATHER_SIBLING, after, name="gather_pass_wait_%d" % gi)
            for n, land in zip(_GATHER_GROUPS[gi], lands):
                full[n] = _unstack(land, n in col)
        return full[name]

    pending, scatters = {}, []

    def put_g(name, dw, stacked=False):
        pending[name] = dw if stacked else _stack(dw, name in col)
        gi = [name in g for g in _SCATTER_GROUPS].index(True)
        group = _SCATTER_GROUPS[gi]
        if not all(n in pending for n in group):
            return None
        pairs = [(pending[n], lax.empty(pending[n].shape, pending[n].dtype)) for n in group]
        started, tok = _exchange_start([pairs], SCATTER_DIRECT, name="scatter_start_%d" % gi)
        scatters.append((gi, started[0]))
        return tok

    gains = {n: two_d(w_[n]) for n in _GAINS}
    grad_x, small = _local_step(
        x[0], mem[0], loss_target[0], gains, attn_sinks.reshape(nq), w_attn_proj.shape[-2], cw, pre_w, get_w, put_g,
        dep0=token)

    grads, deltas, new_m, new_v = {}, {}, {}, {}
    me1 = me.reshape(1).astype(jnp.int32)
    after, halves = grad_x, []
    for gi, started in scatters:
        mine, parts = _exchange_wait(started, SCATTER_DIRECT, after, name="scatter_wait_%d" % gi)
        for n, own, p in zip(_SCATTER_GROUPS[gi], mine, parts):
            if n.startswith("w_in_"):
                halves.append(_sum_partials(p, own, me1, name="sum_" + n).T)
                after = halves[-1]
                if len(halves) < 2:
                    continue
                n, g = "w_in", jnp.concatenate(halves, axis=0)
                dl, nm, nv = _adamw_rows(two_d(w_[n]), g, two_d(m_[n]), two_d(v_[n]), name="adamw_" + n)
            else:
                g, dl, nm, nv = _adamw_sum(p, own, me1, two_d(w_[n]), two_d(m_[n]), two_d(v_[n]), name="adamw_" + n)
            shape = w_[n].shape
            grads[n], deltas[n], new_m[n], new_v[n] = (a.reshape(shape) for a in (g, dl, nm, nv))
            after = g

    parts = [(small[n], i, 1) for i, n in enumerate(_GAINS)]
    parts += [(small["attn_sinks"], 5, 1), (small["loss"], 6, 1), (small["conv_w8"], 8, 3)]
    red = _all_reduce_small(parts, 2 * SUBLANES, max(d, cw), name="reduce_small", dep=after)
    loss = red[6, 0]
    small_g = {n: red[i:i + 1, :d] for i, n in enumerate(_GAINS)}
    small_g["attn_sinks"] = red[5:6, :nq]
    small_g["conv_w"] = lax.dynamic_slice(red, (8, me * cw_shard), (3, cw_shard))
    for n in _GAINS + ("attn_sinks", "conv_w"):
        shape = w_[n].shape
        g = small_g[n]
        dl, nm, nv = _adamw_small(two_d(w_[n]), g, two_d(m_[n]), two_d(v_[n]), name="adamw_" + n)
        grads[n], deltas[n], new_m[n], new_v[n] = (a.reshape(shape) for a in (g, dl, nm, nv))

    return (loss, grad_x[None], *[grads[n] for n in _WEIGHTS], *[deltas[n] for n in _WEIGHTS],
            *[new_m[n] for n in _WEIGHTS], *[new_v[n] for n in _WEIGHTS])
```

```python
import functools
import math

import jax
import jax.numpy as jnp
from jax import lax
from jax.experimental import pallas as pl
from jax.experimental.pallas import tpu as pltpu

HEAD_DIM = 64
Q_PER_KV = 4
WINDOW = 128
X_HEAD_DIM = 128
ROPE_THETA = 10000.0
EPS = 1e-6
ADAM_LR = 0.001
ADAM_B1 = 0.9
ADAM_B2 = 0.999
ADAM_EPS = 1e-08
ADAM_WD = 0.01
ADAM_STEP = 10

N_DEV = 8
LANES = 128
SUBLANES = 8
VMEM_LIMIT_BYTES = 56 * 1024 * 1024
BF = jnp.bfloat16
F32 = jnp.float32
MESH = pl.DeviceIdType.MESH


def _cp(*sem):
    return pltpu.CompilerParams(dimension_semantics=sem, vmem_limit_bytes=VMEM_LIMIT_BYTES)


def _sigmoid(x):
    return 1.0 / (1.0 + jnp.exp(-x))


_DIMS = {
    "nn": (((1,), (0,)), ((), ())),
    "nt": (((1,), (1,)), ((), ())),
    "tn": (((0,), (0,)), ((), ())),
}


def _fit(dim, tile):
    if dim <= tile:
        return dim
    for t in range(tile // LANES * LANES, 0, -LANES):
        if dim % t == 0:
            return t
    return dim


def _mm(a, b, *, mode, tm, tn, tk, out_dtype, name, residual=None, dep=None, a_planes=1, b_planes=1,
        stacked=False, b_cols=None, a_cols=None, rms_gain=None):
    if a_planes > 1:
        assert mode == "nt"
        (_, m, kp), (n, k) = a.shape, b.shape
        assert kp * a_planes == k
    elif b_planes > 1:
        assert mode == "tn"
        (k, m), (_, k2, np_) = a.shape, b.shape
        n = np_ * b_planes
        assert k == k2
    elif mode == "nn":
        (m, k), (k2, n) = a.shape, b.shape
        assert k == k2, (name, a.shape, b.shape)
    elif mode == "nt":
        (m, k), (n, k2) = a.shape, b.shape
        if a_cols is not None:
            k = a_cols[1]
        assert k == k2, (name, a.shape, b.shape)
    else:
        (k, m), (k2, n) = a.shape, b.shape
        assert k == k2, (name, a.shape, b.shape)
    tm, tn, tk = _fit(m, tm), _fit(n // b_planes, tn), _fit(k // a_planes, tk)
    assert m % tm == 0 and (n // b_planes) % tn == 0 and (k // a_planes) % tk == 0, (name, m, n, k, tm, tn, tk)
    ka0 = 0
    if a_cols is not None:
        assert mode == "nt" and a_planes == 1 and a_cols[0] % tk == 0
        ka0 = a_cols[0] // tk
    j0 = 0
    if b_cols is not None:
        assert mode == "tn" and b_planes == 1 and b_cols[0] % tn == 0 and b_cols[1] % tn == 0
        j0, n = b_cols[0] // tn, b_cols[1]
    nk = k // tk
    nkp, njp = nk // a_planes, n // tn // b_planes
    if a_planes > 1:
        a_spec = pl.BlockSpec((None, tm, tk), lambda i, j, kk: (kk // nkp, i, kk % nkp))
    elif mode == "tn":
        a_spec = pl.BlockSpec((tk, tm), lambda i, j, kk: (kk, i))
    else:
        a_spec = pl.BlockSpec((tm, tk), lambda i, j, kk: (i, kk + ka0))
    if b_planes > 1:
        b_spec = pl.BlockSpec((None, tk, tn), lambda i, j, kk: (j // njp, kk, j % njp))
    elif mode == "nt":
        b_spec = pl.BlockSpec((tn, tk), lambda i, j, kk: (j, kk))
    else:
        b_spec = pl.BlockSpec((tk, tn), lambda i, j, kk: (kk, j + j0))
    if stacked:
        assert residual is None
        o_spec = pl.BlockSpec((None, tm, tn), lambda i, j, kk: (j, i, 0))
        out_shape = jax.ShapeDtypeStruct((n // tn, m, tn), out_dtype)
    else:
        o_spec = pl.BlockSpec((tm, tn), lambda i, j, kk: (i, j))
        out_shape = jax.ShapeDtypeStruct((m, n), out_dtype)
    dims = _DIMS[mode]
    has_res = residual is not None
    has_rms = rms_gain is not None
    assert not has_rms or (tn == n and not stacked)
    n_in = 2 + has_res + has_rms + (dep is not None)

    def body(*refs):
        a_ref, b_ref, r_ref, o_ref = refs[0], refs[1], refs[2], refs[n_in]
        part = lax.dot_general(a_ref[...].astype(BF), b_ref[...].astype(BF), dims, preferred_element_type=F32)

        def finish(acc):
            if has_res:
                acc = r_ref[...] + acc
            o_ref[...] = acc.astype(out_dtype)
            if has_rms:
                r = lax.rsqrt(jnp.mean(acc * acc, axis=-1, keepdims=True) + EPS)
                refs[n_in + 1][...] = ((acc * r) * refs[2 + has_res][...]).astype(BF)

        if nk == 1:
            finish(part)
        else:
            acc_ref = refs[-1]
            kk = pl.program_id(2)

            @pl.when(kk == 0)
            def _():
                acc_ref[...] = part

            @pl.when(kk > 0)
            def _():
                acc_ref[...] += part

            @pl.when(kk == nk - 1)
            def _():
                finish(acc_ref[...])

    in_specs = [a_spec, b_spec] + ([o_spec] if has_res else [])
    args = (a, b) + ((residual,) if has_res else ())
    out_specs = o_spec
    if has_rms:
        in_specs.append(pl.BlockSpec((1, n), lambda i, j, kk: (0, 0)))
        args += (rms_gain,)
        out_specs = [o_spec, o_spec]
        out_shape = [out_shape, jax.ShapeDtypeStruct((m, n), BF)]
    if dep is not None:
        in_specs.append(pl.BlockSpec(memory_space=pl.ANY))
        args += (dep,)
    return pl.pallas_call(
        body,
        name=name,
        grid=(m // tm, n // tn, nk),
        in_specs=in_specs,
        out_specs=out_specs,
        out_shape=out_shape,
        scratch_shapes=[pltpu.VMEM((tm, tn), F32)] if nk > 1 else [],
        compiler_params=_cp("parallel", "parallel", "arbitrary"),
    )(*args)


def _rms_fwd(h, g, *, name, tm=512, dep=None):
    t, d = h.shape
    tm = min(tm, t)

    def body(*refs):
        h_ref, g_ref, u_ref = refs[0], refs[1], refs[-1]
        hv = h_ref[...]
        r = lax.rsqrt(jnp.mean(hv * hv, axis=-1, keepdims=True) + EPS)
        u_ref[...] = ((hv * r) * g_ref[...]).astype(BF)

    in_specs = [pl.BlockSpec((tm, d), lambda i: (i, 0)), pl.BlockSpec((1, d), lambda i: (0, 0))]
    args = (h, g)
    if dep is not None:
        in_specs.append(pl.BlockSpec(memory_space=pl.ANY))
        args += (dep,)
    return pl.pallas_call(
        body,
        name=name,
        grid=(t // tm,),
        in_specs=in_specs,
        out_specs=pl.BlockSpec((tm, d), lambda i: (i, 0)),
        out_shape=jax.ShapeDtypeStruct((t, d), BF),
        compiler_params=_cp("parallel"),
    )(*args)


def _rms_bwd(du, h, g, dres, *, name, tm=256, du_w=None, dep=None):
    t, d = h.shape
    tm = min(tm, t)
    want_dh = dres is not None

    def body(*refs):
        du_ref, h_ref, g_ref, dg_ref = refs[0], refs[1], refs[2], refs[-1]
        if want_dh:
            dres_ref, dh_ref, dhb_ref = refs[3], refs[-3], refs[-2]
        i = pl.program_id(0)
        hv = h_ref[...]
        duv = du_ref[...]
        if du_w is not None:
            duv = lax.dot_general(duv, refs[3 + want_dh][...], _DIMS["nt"], preferred_element_type=F32)
        r = lax.rsqrt(jnp.mean(hv * hv, axis=-1, keepdims=True) + EPS)
        nv = hv * r
        if want_dh:
            gy = duv * g_ref[...]
            dh = dres_ref[...] + r * (gy - nv * jnp.mean(nv * gy, axis=-1, keepdims=True))
            dh_ref[...] = dh
            dhb_ref[...] = dh.astype(BF)

        @pl.when(i == 0)
        def _():
            dg_ref[...] = jnp.zeros_like(dg_ref)

        dg_ref[...] += jnp.sum(duv * nv, axis=0, keepdims=True)

    row = pl.BlockSpec((tm, d), lambda i: (i, 0))
    vec = pl.BlockSpec((1, d), lambda i: (0, 0))
    du_spec = row if du_w is None else pl.BlockSpec((tm, du.shape[1]), lambda i: (i, 0))
    if want_dh:
        in_specs, args = [du_spec, row, vec, row], (du, h, g, dres)
        out_specs = [row, row, vec]
        out_shape = [jax.ShapeDtypeStruct((t, d), F32), jax.ShapeDtypeStruct((t, d), BF),
                     jax.ShapeDtypeStruct((1, d), F32)]
    else:
        in_specs, args = [du_spec, row, vec], (du, h, g)
        out_specs = [vec]
        out_shape = [jax.ShapeDtypeStruct((1, d), F32)]
    if du_w is not None:
        in_specs.append(pl.BlockSpec(du_w.shape, lambda i: (0, 0)))
        args += (du_w,)
    if dep is not None:
        in_specs.append(pl.BlockSpec(memory_space=pl.ANY))
        args += (dep,)
    outs = pl.pallas_call(
        body,
        name=name,
        grid=(t // tm,),
        in_specs=in_specs,
        out_specs=out_specs,
        out_shape=out_shape,
        compiler_params=_cp("arbitrary"),
    )(*args)
    return (outs[0], outs[1], outs[2]) if want_dh else (None, None, outs[0])


def _loss_head(h, target, g, *, name, tm=256):
    t, d = h.shape
    tm = min(tm, t)

    def body(h_ref, t_ref, g_ref, dh_ref, dhb_ref, loss_ref, dg_ref):
        i = pl.program_id(0)
        hv = h_ref[...]
        gv = g_ref[...]
        r = lax.rsqrt(jnp.mean(hv * hv, axis=-1, keepdims=True) + EPS)
        nv = hv * r
        e = nv * gv - t_ref[...]
        per_tok = jnp.mean(e * e, axis=-1, keepdims=True)
        lp = 0.5 * jnp.sum(per_tok, axis=0, keepdims=True)
        dy = e * (1.0 / d)
        gy = dy * gv
        dh = r * (gy - nv * jnp.mean(nv * gy, axis=-1, keepdims=True))
        dh_ref[...] = dh
        dhb_ref[...] = dh.astype(BF)

        @pl.when(i == 0)
        def _():
            loss_ref[...] = jnp.zeros_like(loss_ref)
            dg_ref[...] = jnp.zeros_like(dg_ref)

        loss_ref[...] += jnp.broadcast_to(lp, loss_ref.shape)
        dg_ref[...] += jnp.sum(dy * nv, axis=0, keepdims=True)

    row = pl.BlockSpec((tm, d), lambda i: (i, 0))
    vec = pl.BlockSpec((1, d), lambda i: (0, 0))
    return pl.pallas_call(
        body,
        name=name,
        grid=(t // tm,),
        in_specs=[row, row, vec],
        out_specs=[row, row, pl.BlockSpec((SUBLANES, LANES), lambda i: (0, 0)), vec],
        out_shape=[
            jax.ShapeDtypeStruct((t, d), F32),
            jax.ShapeDtypeStruct((t, d), BF),
            jax.ShapeDtypeStruct((SUBLANES, LANES), F32),
            jax.ShapeDtypeStruct((1, d), F32),
        ],
        compiler_params=_cp("arbitrary"),
    )(h, target, g)


def _ffn_in_fwd(u, w, *, name, tm=1024, tn=512, dep=None):
    t, d = u.shape
    f = w.shape[1] // 2
    tm, tn = _fit(t, tm), _fit(f, tn)
    nf = f // tn

    def body(*refs):
        u_ref, wa_ref, wb_ref, hid_ref, act_ref = refs[0], refs[1], refs[2], refs[-2], refs[-1]
        uv = u_ref[...]
        a = jnp.dot(uv, wa_ref[...], preferred_element_type=F32)
        b = jnp.dot(uv, wb_ref[...], preferred_element_type=F32)
        hid_ref[0] = a.astype(BF)
        hid_ref[1] = b.astype(BF)
        act_ref[...] = ((a * _sigmoid(a)) * b).astype(BF)

    in_specs = [
        pl.BlockSpec((tm, d), lambda i, j: (i, 0)),
        pl.BlockSpec((d, tn), lambda i, j: (0, j)),
        pl.BlockSpec((d, tn), lambda i, j: (0, nf + j)),
    ]
    args = (u, w, w)
    if dep is not None:
        in_specs.append(pl.BlockSpec(memory_space=pl.ANY))
        args += (dep,)
    return pl.pallas_call(
        body,
        name=name,
        grid=(t // tm, nf),
        in_specs=in_specs,
        out_specs=[pl.BlockSpec((2, tm, tn), lambda i, j: (0, i, j)), pl.BlockSpec((tm, tn), lambda i, j: (i, j))],
        out_shape=[jax.ShapeDtypeStruct((2, t, f), BF), jax.ShapeDtypeStruct((t, f), BF)],
        compiler_params=_cp("parallel", "parallel"),
    )(*args)


def _ffn_out_bwd(dh, w_out, hid2, *, name, tm=1024, tn=512, dep=None):
    t, d = dh.shape
    f = w_out.shape[0]
    tm, tn = _fit(t, tm), _fit(f, tn)

    def body(*refs):
        dh_ref, w_ref, hid_ref, o_ref = refs[0], refs[1], refs[2], refs[-1]
        dact = lax.dot_general(dh_ref[...], w_ref[...], _DIMS["nt"], preferred_element_type=F32)
        a = hid_ref[0].astype(F32)
        b = hid_ref[1].astype(F32)
        sg = _sigmoid(a)
        o_ref[0] = (dact * b * (sg * (1.0 + a * (1.0 - sg)))).astype(BF)
        o_ref[1] = (dact * (a * sg)).astype(BF)

    pair = pl.BlockSpec((2, tm, tn), lambda i, j: (0, i, j))
    in_specs = [pl.BlockSpec((tm, d), lambda i, j: (i, 0)), pl.BlockSpec((tn, d), lambda i, j: (j, 0)), pair]
    args = (dh, w_out, hid2)
    if dep is not None:
        in_specs.append(pl.BlockSpec(memory_space=pl.ANY))
        args += (dep,)
    return pl.pallas_call(
        body,
        name=name,
        grid=(t // tm, f // tn),
        in_specs=in_specs,
        out_specs=pair,
        out_shape=jax.ShapeDtypeStruct((2, t, f), BF),
        compiler_params=_cp("parallel", "parallel"),
    )(*args)


def _gate_fwd(cy, w, proj, ya, *, ga0, gc0, name, tm=1024, tc=512):
    t, d = ya.shape
    kc = cy.shape[1]
    tm, tc = _fit(t, tm), math.gcd(tc, d, ga0, gc0)
    a0, c0 = ga0 // tc, gc0 // tc

    def body(cy_ref, w_ref, ga_ref, gc_ref, ya_ref, yc_ref, o_ref):
        yc = jnp.dot(cy_ref[...], w_ref[...], preferred_element_type=F32)
        yc_ref[...] = yc
        o_ref[...] = (_sigmoid(ga_ref[...]) * ya_ref[...] + _sigmoid(gc_ref[...]) * yc).astype(BF)

    blk = pl.BlockSpec((tm, tc), lambda i, j: (i, j))
    return pl.pallas_call(
        body,
        name=name,
        grid=(t // tm, d // tc),
        in_specs=[
            pl.BlockSpec((tm, kc), lambda i, j: (i, 0)),
            pl.BlockSpec((kc, tc), lambda i, j: (0, j)),
            pl.BlockSpec((tm, tc), lambda i, j: (i, a0 + j)),
            pl.BlockSpec((tm, tc), lambda i, j: (i, c0 + j)),
            blk,
        ],
        out_specs=[blk, blk],
        out_shape=[jax.ShapeDtypeStruct((t, d), F32), jax.ShapeDtypeStruct((t, d), BF)],
        compiler_params=_cp("parallel", "parallel"),
    )(cy, w, proj, proj, ya)


def _gate_bwd(dh, w, proj, ya, yc, *, ga0, gc0, name, tm=1024, tc=512):
    t, d = ya.shape
    tm, tc = _fit(t, tm), math.gcd(tc, d, ga0, gc0)
    a0, c0 = ga0 // tc, gc0 // tc

    def body(dh_ref, w_ref, ga_ref, gc_ref, ya_ref, yc_ref, dya_ref, dyc_ref, dga_ref, dgc_ref):
        dmv = lax.dot_general(dh_ref[...], w_ref[...], _DIMS["nt"], preferred_element_type=F32)
        sa = _sigmoid(ga_ref[...])
        sc = _sigmoid(gc_ref[...])
        dya_ref[...] = (dmv * sa).astype(BF)
        dyc_ref[...] = (dmv * sc).astype(BF)
        dga_ref[...] = (dmv * ya_ref[...] * (sa * (1.0 - sa))).astype(BF)
        dgc_ref[...] = (dmv * yc_ref[...] * (sc * (1.0 - sc))).astype(BF)

    blk = pl.BlockSpec((tm, tc), lambda i, j: (i, j))
    out = jax.ShapeDtypeStruct((t, d), BF)
    return pl.pallas_call(
        body,
        name=name,
        grid=(t // tm, d // tc),
        in_specs=[
            pl.BlockSpec((tm, d), lambda i, j: (i, 0)),
            pl.BlockSpec((tc, d), lambda i, j: (j, 0)),
            pl.BlockSpec((tm, tc), lambda i, j: (i, a0 + j)),
            pl.BlockSpec((tm, tc), lambda i, j: (i, c0 + j)),
            blk,
            blk,
        ],
        out_specs=[blk, blk, blk, blk],
        out_shape=[out, out, out, out],
        compiler_params=_cp("parallel", "parallel"),
    )(dh, w, proj, proj, ya, yc)


def _conv_taps(cz, czp, i):
    czp = czp * (i > 0).astype(F32)
    h1 = czp[SUBLANES - 1:SUBLANES, :]
    h2 = czp[SUBLANES - 2:SUBLANES - 1, :]
    row = lax.broadcasted_iota(jnp.int32, cz.shape, 0)
    s1 = jnp.where(row == 0, h1, pltpu.roll(cz, 1, 0))
    s2 = jnp.where(row == 0, h2, jnp.where(row == 1, h1, pltpu.roll(cz, 2, 0)))
    return s1, s2


def _conv_fwd(proj, w8, *, z0, gb0, gc0, cw, name, tm=512, tc=512):
    t = proj.shape[0]
    tm, tc = min(tm, t), math.gcd(tc, cw, z0, gb0, gc0)
    zb, bb, cb = z0 // tc, gb0 // tc, gc0 // tc
    rb = tm // SUBLANES

    def body(z_ref, gb_ref, gc_ref, zp_ref, gcp_ref, w_ref, o_ref):
        i = pl.program_id(0)
        cz = gc_ref[...] * z_ref[...]
        s1, s2 = _conv_taps(cz, gcp_ref[...] * zp_ref[...], i)
        w = w_ref[...]
        y = w[0:1, :] * s2 + w[1:2, :] * s1 + w[2:3, :] * cz
        o_ref[...] = (gb_ref[...] * y).astype(BF)

    def cur(b0):
        return pl.BlockSpec((tm, tc), lambda i, j: (i, b0 + j))

    def prev(b0):
        return pl.BlockSpec((SUBLANES, tc), lambda i, j: (jnp.maximum(i * rb - 1, 0), b0 + j))

    return pl.pallas_call(
        body,
        name=name,
        grid=(t // tm, cw // tc),
        in_specs=[cur(zb), cur(bb), cur(cb), prev(zb), prev(cb), pl.BlockSpec((SUBLANES, tc), lambda i, j: (0, j))],
        out_specs=pl.BlockSpec((tm, tc), lambda i, j: (i, j)),
        out_shape=jax.ShapeDtypeStruct((t, cw), BF),
        compiler_params=_cp("parallel", "parallel"),
    )(proj, proj, proj, proj, proj, w8)


def _conv_bwd(proj, w8, dcy, *, z0, gb0, gc0, cw, name, tm=512, tc=512):
    t = proj.shape[0]
    tm, tc = min(tm, t), math.gcd(tc, cw, z0, gb0, gc0)
    zb, bb, cb = z0 // tc, gb0 // tc, gc0 // tc
    rb = tm // SUBLANES
    nt = t // tm

    def body(z_ref, gb_ref, gc_ref, zp_ref, gcp_ref, d_ref, dn_ref, gbn_ref, w_ref, dz_ref, dgb_ref, dgc_ref, dw_ref):
        i = pl.program_id(1)
        z = z_ref[...]
        gc = gc_ref[...]
        gb = gb_ref[...]
        cz = gc * z
        s1, s2 = _conv_taps(cz, gcp_ref[...] * zp_ref[...], i)
        w = w_ref[...]
        w0, w1, w2 = w[0:1, :], w[1:2, :], w[2:3, :]
        yc = w0 * s2 + w1 * s1 + w2 * cz
        dcyv = d_ref[...]
        dgb_ref[...] = (dcyv * yc).astype(BF)
        dyc = dcyv * gb
        dycn = dn_ref[...] * gbn_ref[...] * (i < nt - 1).astype(F32)
        n1, n2 = dycn[0:1, :], dycn[1:2, :]
        row = lax.broadcasted_iota(jnp.int32, cz.shape, 0)
        a1 = jnp.where(row == tm - 1, n1, pltpu.roll(dyc, tm - 1, 0))
        a2 = jnp.where(row == tm - 1, n2, jnp.where(row == tm - 2, n1, pltpu.roll(dyc, tm - 2, 0)))
        dcz = w2 * dyc + w1 * a1 + w0 * a2
        dz_ref[...] = (dcz * gc).astype(BF)
        dgc_ref[...] = (dcz * z).astype(BF)
        dw0 = jnp.sum(dyc * s2, axis=0, keepdims=True)
        dw1 = jnp.sum(dyc * s1, axis=0, keepdims=True)
        dw2 = jnp.sum(dyc * cz, axis=0, keepdims=True)
        r8 = lax.broadcasted_iota(jnp.int32, (SUBLANES, tc), 0)
        upd = jnp.where(r8 == 0, dw0, jnp.where(r8 == 1, dw1, jnp.where(r8 == 2, dw2, 0.0)))

        @pl.when(i == 0)
        def _():
            dw_ref[...] = jnp.zeros_like(dw_ref)

        dw_ref[...] += upd

    def cur(b0):
        return pl.BlockSpec((tm, tc), lambda j, i: (i, b0 + j))

    def prev(b0):
        return pl.BlockSpec((SUBLANES, tc), lambda j, i: (jnp.maximum(i * rb - 1, 0), b0 + j))

    def nxt(b0):
        return pl.BlockSpec((SUBLANES, tc), lambda j, i: (jnp.minimum((i + 1) * rb, t // SUBLANES - 1), b0 + j))

    blk = pl.BlockSpec((tm, tc), lambda j, i: (i, j))
    w_spec = pl.BlockSpec((SUBLANES, tc), lambda j, i: (0, j))
    out = jax.ShapeDtypeStruct((t, cw), BF)
    return pl.pallas_call(
        body,
        name=name,
        grid=(cw // tc, nt),
        in_specs=[cur(zb), cur(bb), cur(cb), prev(zb), prev(cb), blk, nxt(0), nxt(bb), w_spec],
        out_specs=[blk, blk, blk, w_spec],
        out_shape=[out, out, out, jax.ShapeDtypeStruct((SUBLANES, cw), F32)],
        compiler_params=_cp("parallel", "arbitrary"),
    )(proj, proj, proj, proj, proj, dcy, dcy, proj, w8)


def _rot_half(x):
    lane = lax.broadcasted_iota(jnp.int32, x.shape, 1)
    first = (lane % HEAD_DIM) < (HEAD_DIM // 2)
    return jnp.where(first, pltpu.roll(x, LANES - HEAD_DIM // 2, 1), pltpu.roll(x, HEAD_DIM // 2, 1))


def _rope(x, c, s):
    parts = []
    for a in range(x.shape[1] // LANES):
        xa = x[:, a * LANES:(a + 1) * LANES]
        parts.append(xa * c + _rot_half(xa) * s)
    return parts[0] if len(parts) == 1 else jnp.concatenate(parts, axis=1)


def _rope_bwd(dy, c, s):
    parts = []
    for a in range(dy.shape[1] // LANES):
        da = dy[:, a * LANES:(a + 1) * LANES]
        parts.append(da * c + _rot_half(da * s))
    return parts[0] if len(parts) == 1 else jnp.concatenate(parts, axis=1)


def _window(i):
    b = WINDOW
    r = lax.broadcasted_iota(jnp.int32, (b, b), 0)
    c = lax.broadcasted_iota(jnp.int32, (b, b), 1)
    return c <= r, c <= r + jnp.where(i > 0, b, 0)


def _band_pick(x, tri):
    b = tri.shape[0]
    return jnp.where(tri, x[:, b:], x[:, :b])


def _band_spread(y, tri):
    return jnp.concatenate([jnp.where(tri, 0.0, y), jnp.where(tri, y, 0.0)], axis=1)


def _chunk(x, a):
    return x[:, a * LANES:(a + 1) * LANES]


def _kv_aligned(kp, kc, h):
    band = jnp.concatenate([_chunk(kp, h // 2), _chunk(kc, h // 2)], axis=0).astype(F32)
    swapped = pltpu.roll(band, HEAD_DIM, 1)
    return (band, swapped) if h % 2 == 0 else (swapped, band)


def _swa_fwd(proj, cosf, sinf, sinks, *, nq, name, dep=None):
    t = proj.shape[0]
    nkv = nq // Q_PER_KV
    aw, kw, b = nq * HEAD_DIM, nkv * HEAD_DIM, WINDOW
    nb = t // b
    kblk = aw // kw
    scale = HEAD_DIM ** -0.5

    def body(*refs):
        sink_ref, q_ref, kc_ref, kp_ref, vc_ref, vp_ref, cc_ref, cp_ref, sc_ref, sp_ref = refs[:10]
        o_ref, qr_ref, kr_ref, s_scr, p_scr = refs[-5:]
        i = pl.program_id(0)
        cc, sc, cpv, spv = cc_ref[...], sc_ref[...], cp_ref[...], sp_ref[...]
        qr = _rope(q_ref[...], cc, sc)
        kc = _rope(kc_ref[...], cc, sc)
        kp = _rope(kp_ref[...], cpv, spv)
        qr_ref[...] = qr.astype(BF)
        kr_ref[...] = kc.astype(BF)
        vc, vp = vc_ref[...], vp_ref[...]
        tri, ok = _window(i)
        lo = lax.broadcasted_iota(jnp.int32, (b, LANES), 1) < HEAD_DIM
        ks = [[x.astype(BF) for x in _kv_aligned(kp, kc, h)] for h in range(nkv)]
        vs = [[x.astype(BF) for x in _kv_aligned(vp, vc, h)] for h in range(nkv)]
        for hq in range(nq):
            a, par = hq // 2, hq % 2
            qm = jnp.where(lo if par == 0 else ~lo, _chunk(qr, a), 0.0).astype(BF)
            s_scr[hq] = _band_pick(
                lax.dot_general(qm, ks[hq // Q_PER_KV][par], _DIMS["nt"], preferred_element_type=F32), tri)
        for hq in range(nq):
            s = jnp.where(ok, s_scr[hq] * scale, -jnp.inf)
            sink = sink_ref[hq]
            m = jnp.maximum(jnp.max(s, axis=-1, keepdims=True), sink)
            p = jnp.exp(s - m)
            p = p / (jnp.sum(p, axis=-1, keepdims=True) + jnp.exp(sink - m))
            p_scr[hq] = _band_spread(p, tri).astype(BF)
        for a in range(nq // 2):
            o_par = [jnp.dot(p_scr[2 * a + par], vs[(2 * a) // Q_PER_KV][par], preferred_element_type=F32)
                     for par in range(2)]
            o_ref[:, a * LANES:(a + 1) * LANES] = jnp.where(lo, o_par[0], o_par[1]).astype(BF)

    def prev_i(i):
        return jnp.maximum(i - 1, 0)

    tab_c = pl.BlockSpec((b, LANES), lambda i: (i, 0))
    tab_p = pl.BlockSpec((b, LANES), lambda i: (prev_i(i), 0))
    in_specs = [
        pl.BlockSpec(memory_space=pltpu.SMEM),
        pl.BlockSpec((b, aw), lambda i: (i, 0)),
        pl.BlockSpec((b, kw), lambda i: (i, kblk)),
        pl.BlockSpec((b, kw), lambda i: (prev_i(i), kblk)),
        pl.BlockSpec((b, kw), lambda i: (i, kblk + 1)),
        pl.BlockSpec((b, kw), lambda i: (prev_i(i), kblk + 1)),
        tab_c,
        tab_p,
        tab_c,
        tab_p,
    ]
    args = (sinks, proj, proj, proj, proj, proj, cosf, cosf, sinf, sinf)
    if dep is not None:
        in_specs.append(pl.BlockSpec(memory_space=pl.ANY))
        args += (dep,)
    return pl.pallas_call(
        body,
        name=name,
        grid=(nb,),
        in_specs=in_specs,
        out_specs=[
            pl.BlockSpec((b, aw), lambda i: (i, 0)),
            pl.BlockSpec((b, aw), lambda i: (i, 0)),
            pl.BlockSpec((b, kw), lambda i: (i, 0)),
        ],
        out_shape=[
            jax.ShapeDtypeStruct((t, aw), BF),
            jax.ShapeDtypeStruct((t, aw), BF),
            jax.ShapeDtypeStruct((t, kw), BF),
        ],
        scratch_shapes=[pltpu.VMEM((nq, b, b), F32), pltpu.VMEM((nq, b, 2 * b), BF)],
        compiler_params=_cp("parallel"),
    )(*args)


def _swa_bwd(qr, kr, proj, do, cosf, sinf, sinks, *, nq, name):
    t = proj.shape[0]
    nkv = nq // Q_PER_KV
    aw, kw, b = nq * HEAD_DIM, nkv * HEAD_DIM, WINDOW
    nb = t // b
    kblk = aw // kw
    scale = HEAD_DIM ** -0.5

    def body(sink_ref, q_ref, kc_ref, kp_ref, vc_ref, vp_ref, do_ref, cc_ref, cp_ref, sc_ref, sp_ref,
             dq_ref, dk_ref, dv_ref, ds_ref, ck_ref, cv_ref, sacc_ref, s_scr, dp_scr, ds_scr, pf_scr):
        i = pl.program_id(0)

        @pl.when(i == 0)
        def _():
            ck_ref[...] = jnp.zeros_like(ck_ref)
            cv_ref[...] = jnp.zeros_like(cv_ref)
            sacc_ref[...] = jnp.zeros_like(sacc_ref)

        @pl.when(i < nb)
        def _():
            q = q_ref[...]
            kc, kp = kc_ref[...], kp_ref[...]
            vc, vp = vc_ref[...], vp_ref[...]
            dov = do_ref[...]
            tri, ok = _window(i)
            lane = lax.broadcasted_iota(jnp.int32, (b, LANES), 1)
            lo = lane < HEAD_DIM
            cc, sc = cc_ref[...], sc_ref[...]
            nch = kw // LANES
            row_lo = lax.broadcasted_iota(jnp.int32, (LANES, b), 0) < HEAD_DIM
            dk_ch = [jnp.zeros((LANES, 2 * b), F32) for _ in range(nch)]
            dv_ch = [jnp.zeros((LANES, 2 * b), F32) for _ in range(nch)]
            sacc = jnp.zeros((b, LANES), F32)
            ks = [[x.astype(BF) for x in _kv_aligned(kp, kc, h)] for h in range(nkv)]
            vs = [[x.astype(BF) for x in _kv_aligned(vp, vc, h)] for h in range(nkv)]
            for hq in range(nq):
                a, par, h = hq // 2, hq % 2, hq // Q_PER_KV
                mine = lo if par == 0 else ~lo
                qm = jnp.where(mine, _chunk(q, a).astype(F32), 0.0).astype(BF)
                dom = jnp.where(mine, _chunk(dov, a).astype(F32), 0.0).astype(BF)
                s_scr[hq] = _band_pick(lax.dot_general(qm, ks[h][par], _DIMS["nt"], preferred_element_type=F32), tri)
                dp_scr[hq] = _band_pick(
                    lax.dot_general(dom, vs[h][par], _DIMS["nt"], preferred_element_type=F32), tri)
            for hq in range(nq):
                s = jnp.where(ok, s_scr[hq] * scale, -jnp.inf)
                sink = sink_ref[hq]
                m = jnp.maximum(jnp.max(s, axis=-1, keepdims=True), sink)
                e = jnp.exp(s - m)
                es = jnp.exp(sink - m)
                zinv = 1.0 / (jnp.sum(e, axis=-1, keepdims=True) + es)
                p = e * zinv
                dp = dp_scr[hq]
                delta = jnp.sum(p * dp, axis=-1, keepdims=True)
                ds_scr[hq] = _band_spread(p * (dp - delta) * scale, tri).astype(BF)
                pf_scr[hq] = _band_spread(p, tri).astype(BF)
                sacc = sacc + jnp.where(lane == hq, -(es * zinv) * delta, 0.0)
            for a in range(nq // 2):
                h = (2 * a) // Q_PER_KV
                qa_t = _chunk(q, a).astype(F32).T
                doa_t = _chunk(dov, a).astype(F32).T
                dq_par = []
                for par in range(2):
                    hq = 2 * a + par
                    mine_t = row_lo if par == 0 else ~row_lo
                    qm_t = jnp.where(mine_t, qa_t, 0.0).astype(BF)
                    dom_t = jnp.where(mine_t, doa_t, 0.0).astype(BF)
                    dsv = ds_scr[hq]
                    dq_par.append(jnp.dot(dsv, ks[h][par], preferred_element_type=F32))
                    dkh = jnp.dot(qm_t, dsv, preferred_element_type=F32)
                    dvh = jnp.dot(dom_t, pf_scr[hq], preferred_element_type=F32)
                    if par != h % 2:
                        dkh = pltpu.roll(dkh, HEAD_DIM, 0)
                        dvh = pltpu.roll(dvh, HEAD_DIM, 0)
                    dk_ch[h // 2] = dk_ch[h // 2] + dkh
                    dv_ch[h // 2] = dv_ch[h // 2] + dvh
                dqa = jnp.where(lo, dq_par[0], dq_par[1])
                dq_ref[:, a * LANES:(a + 1) * LANES] = _rope_bwd(dqa, cc, sc).astype(BF)
            dk_ch = [x.T for x in dk_ch]
            dv_ch = [x.T for x in dv_ch]
            dk = dk_ch[0] if nch == 1 else jnp.concatenate(dk_ch, axis=1)
            dv = dv_ch[0] if nch == 1 else jnp.concatenate(dv_ch, axis=1)
            dk_ref[...] = _rope_bwd(ck_ref[...] + dk[:b, :], cp_ref[...], sp_ref[...]).astype(BF)
            dv_ref[...] = (cv_ref[...] + dv[:b, :]).astype(BF)
            ck_ref[...] = dk[b:, :]
            cv_ref[...] = dv[b:, :]
            sacc_ref[...] += sacc

        @pl.when(i == nb)
        def _():
            dk_ref[...] = _rope_bwd(ck_ref[...], cp_ref[...], sp_ref[...]).astype(BF)
            dv_ref[...] = cv_ref[...].astype(BF)
            ds_ref[...] = jnp.broadcast_to(jnp.sum(sacc_ref[...], axis=0, keepdims=True), ds_ref.shape)

    def cur_i(i):
        return jnp.minimum(i, nb - 1)

    def prev_i(i):
        return jnp.clip(i - 1, 0, nb - 1)

    tab_c = pl.BlockSpec((b, LANES), lambda i: (cur_i(i), 0))
    tab_p = pl.BlockSpec((b, LANES), lambda i: (prev_i(i), 0))
    return pl.pallas_call(
        body,
        name=name,
        grid=(nb + 1,),
        in_specs=[
            pl.BlockSpec(memory_space=pltpu.SMEM),
            pl.BlockSpec((b, aw), lambda i: (cur_i(i), 0)),
            pl.BlockSpec((b, kw), lambda i: (cur_i(i), 0)),
            pl.BlockSpec((b, kw), lambda i: (prev_i(i), 0)),
            pl.BlockSpec((b, kw), lambda i: (cur_i(i), kblk + 1)),
            pl.BlockSpec((b, kw), lambda i: (prev_i(i), kblk + 1)),
            pl.BlockSpec((b, aw), lambda i: (cur_i(i), 0)),
            tab_c,
            tab_p,
            tab_c,
            tab_p,
        ],
        out_specs=[
            pl.BlockSpec((b, aw), lambda i: (cur_i(i), 0)),
            pl.BlockSpec((b, kw), lambda i: (prev_i(i), 0)),
            pl.BlockSpec((b, kw), lambda i: (prev_i(i), 0)),
            pl.BlockSpec((SUBLANES, LANES), lambda i: (0, 0)),
        ],
        out_shape=[
            jax.ShapeDtypeStruct((t, aw), BF),
            jax.ShapeDtypeStruct((t, kw), BF),
            jax.ShapeDtypeStruct((t, kw), BF),
            jax.ShapeDtypeStruct((SUBLANES, LANES), F32),
        ],
        scratch_shapes=[pltpu.VMEM((b, kw), F32), pltpu.VMEM((b, kw), F32), pltpu.VMEM((b, LANES), F32),
                        pltpu.VMEM((nq, b, b), F32), pltpu.VMEM((nq, b, b), F32),
                        pltpu.VMEM((nq, b, 2 * b), BF), pltpu.VMEM((nq, b, 2 * b), BF)],
        compiler_params=_cp("arbitrary"),
    )(sinks, qr, kr, kr, proj, proj, do, cosf, cosf, sinf, sinf)


def _xattn_fwd(xq, kv, *, name, tq=512):
    t, xw = xq.shape
    mtok = kv.shape[0]
    tq = min(tq, t)
    nh = xw // X_HEAD_DIM
    scale = X_HEAD_DIM ** -0.5

    def body(q_ref, kv_ref, o_ref):
        q = q_ref[...]
        kvv = kv_ref[...]
        outs = []
        for h in range(nh):
            sl = slice(h * X_HEAD_DIM, (h + 1) * X_HEAD_DIM)
            k = kvv[:, sl]
            v = kvv[:, xw + h * X_HEAD_DIM: xw + (h + 1) * X_HEAD_DIM]
            s = lax.dot_general(q[:, sl], k, _DIMS["nt"], preferred_element_type=F32) * scale
            e = jnp.exp(s - jnp.max(s, axis=-1, keepdims=True))
            p = e / jnp.sum(e, axis=-1, keepdims=True)
            outs.append(jnp.dot(p.astype(BF), v, preferred_element_type=F32))
        o_ref[...] = jnp.concatenate(outs, axis=1).astype(BF)

    return pl.pallas_call(
        body,
        name=name,
        grid=(t // tq,),
        in_specs=[pl.BlockSpec((tq, xw), lambda i: (i, 0)), pl.BlockSpec((mtok, 2 * xw), lambda i: (0, 0))],
        out_specs=pl.BlockSpec((tq, xw), lambda i: (i, 0)),
        out_shape=jax.ShapeDtypeStruct((t, xw), BF),
        compiler_params=_cp("parallel"),
    )(xq, kv)


def _xattn_bwd(xq, kv, do, *, name, tq=512):
    t, xw = xq.shape
    mtok = kv.shape[0]
    tq = min(tq, t)
    nh = xw // X_HEAD_DIM
    scale = X_HEAD_DIM ** -0.5

    def body(q_ref, kv_ref, do_ref, dq_ref, dkv_ref):
        i = pl.program_id(0)
        q = q_ref[...]
        kvv = kv_ref[...]
        dov = do_ref[...]
        dqs, dks, dvs = [], [], []
        for h in range(nh):
            sl = slice(h * X_HEAD_DIM, (h + 1) * X_HEAD_DIM)
            k = kvv[:, sl]
            v = kvv[:, xw + h * X_HEAD_DIM: xw + (h + 1) * X_HEAD_DIM]
            qh, doh = q[:, sl], dov[:, sl]
            s = lax.dot_general(qh, k, _DIMS["nt"], preferred_element_type=F32) * scale
            e = jnp.exp(s - jnp.max(s, axis=-1, keepdims=True))
            p = e / jnp.sum(e, axis=-1, keepdims=True)
            dp = lax.dot_general(doh, v, _DIMS["nt"], preferred_element_type=F32)
            delta = jnp.sum(p * dp, axis=-1, keepdims=True)
            dsv = (p * (dp - delta) * scale).astype(BF)
            dqs.append(jnp.dot(dsv, k, preferred_element_type=F32))
            dks.append(lax.dot_general(dsv, qh, _DIMS["tn"], preferred_element_type=F32))
            dvs.append(lax.dot_general(p.astype(BF), doh, _DIMS["tn"], preferred_element_type=F32))
        dq_ref[...] = jnp.concatenate(dqs, axis=1).astype(BF)

        @pl.when(i == 0)
        def _():
            dkv_ref[...] = jnp.zeros_like(dkv_ref)

        dkv_ref[...] += jnp.concatenate(dks + dvs, axis=1)

    row = pl.BlockSpec((tq, xw), lambda i: (i, 0))
    full = pl.BlockSpec((mtok, 2 * xw), lambda i: (0, 0))
    return pl.pallas_call(
        body,
        name=name,
        grid=(t // tq,),
        in_specs=[row, full, row],
        out_specs=[row, full],
        out_shape=[jax.ShapeDtypeStruct((t, xw), BF), jax.ShapeDtypeStruct((mtok, 2 * xw), F32)],
        compiler_params=_cp("arbitrary"),
    )(xq, kv, do)


def _adam_math(w, g, m, v):
    m = ADAM_B1 * m + (1.0 - ADAM_B1) * g
    v = ADAM_B2 * v + (1.0 - ADAM_B2) * (g * g)
    m_hat = m / (1.0 - ADAM_B1 ** ADAM_STEP)
    v_hat = v / (1.0 - ADAM_B2 ** ADAM_STEP)
    delta = -ADAM_LR * (m_hat / (jnp.sqrt(v_hat) + ADAM_EPS) + ADAM_WD * w)
    return delta, m, v


def _row_tile(r, c, n_arrays, budget=24 * 1024 * 1024):
    step = 2 * SUBLANES
    cap = max(step, budget // (2 * n_arrays * c * 4))
    if r <= cap:
        return r
    best = None
    for tr in range(step, cap + 1, step):
        if r % tr == 0:
            best = tr
    assert best is not None, (r, c)
    return best


def _adamw_sum(parts, own, me, w, m, v, *, name):
    _, r, c = parts.shape
    tr = _row_tile(r, c, 12)

    def body(me_ref, p_ref, own_ref, w_ref, m_ref, v_ref, g_ref, d_ref, nm_ref, nv_ref):
        mine = jnp.full((tr, c), me_ref[0], jnp.int32)
        g = None
        for s in range(N_DEV):
            term = jnp.where(mine == s, own_ref[...], p_ref[s]).astype(F32)
            g = term if g is None else g + term
        g_ref[...] = g
        d_ref[...], nm_ref[...], nv_ref[...] = _adam_math(w_ref[...], g, m_ref[...], v_ref[...])

    blk = pl.BlockSpec((tr, c), lambda i, me_ref: (i, 0))
    out = jax.ShapeDtypeStruct((r, c), F32)
    return pl.pallas_call(
        body,
        name=name,
        grid_spec=pltpu.PrefetchScalarGridSpec(
            num_scalar_prefetch=1,
            grid=(r // tr,),
            in_specs=[
                pl.BlockSpec((N_DEV, tr, c), lambda i, me_ref: (0, i, 0)),
                pl.BlockSpec((None, tr, c), lambda i, me_ref: (me_ref[0], i, 0)),
                blk, blk, blk,
            ],
            out_specs=[blk, blk, blk, blk],
        ),
        out_shape=[out, out, out, out],
        compiler_params=_cp("parallel"),
    )(me, parts, own, w, m, v)


def _to_bf16(a, *, name, dep=None):
    r, c = a.shape
    tr = _row_tile(r, c, 2)

    def body(*refs):
        refs[-1][...] = refs[0][...].astype(BF)

    blk = pl.BlockSpec((tr, c), lambda i: (i, 0))
    in_specs, args = [blk], (a,)
    if dep is not None:
        in_specs.append(pl.BlockSpec(memory_space=pl.ANY))
        args += (dep,)
    return pl.pallas_call(
        body,
        name=name,
        grid=(r // tr,),
        in_specs=in_specs,
        out_specs=blk,
        out_shape=jax.ShapeDtypeStruct((r, c), BF),
        compiler_params=_cp("parallel"),
    )(*args)


def _sum_partials(parts, own, me, *, name):
    _, r, c = parts.shape
    tr = _row_tile(r, c, 6)

    def body(me_ref, p_ref, own_ref, g_ref):
        mine = jnp.full((tr, c), me_ref[0], jnp.int32)
        g = None
        for s in range(N_DEV):
            term = jnp.where(mine == s, own_ref[...], p_ref[s]).astype(F32)
            g = term if g is None else g + term
        g_ref[...] = g

    return pl.pallas_call(
        body,
        name=name,
        grid_spec=pltpu.PrefetchScalarGridSpec(
            num_scalar_prefetch=1,
            grid=(r // tr,),
            in_specs=[
                pl.BlockSpec((N_DEV, tr, c), lambda i, me_ref: (0, i, 0)),
                pl.BlockSpec((None, tr, c), lambda i, me_ref: (me_ref[0], i, 0)),
            ],
            out_specs=pl.BlockSpec((tr, c), lambda i, me_ref: (i, 0)),
        ),
        out_shape=jax.ShapeDtypeStruct((r, c), F32),
        compiler_params=_cp("parallel"),
    )(me, parts, own)


def _adamw_rows(w, g, m, v, *, name):
    r, c = w.shape
    tr = _row_tile(r, c, 7)

    def body(w_ref, g_ref, m_ref, v_ref, d_ref, nm_ref, nv_ref):
        d_ref[...], nm_ref[...], nv_ref[...] = _adam_math(w_ref[...], g_ref[...], m_ref[...], v_ref[...])

    blk = pl.BlockSpec((tr, c), lambda i: (i, 0))
    out = jax.ShapeDtypeStruct((r, c), F32)
    return pl.pallas_call(
        body,
        name=name,
        grid=(r // tr,),
        in_specs=[blk, blk, blk, blk],
        out_specs=[blk, blk, blk],
        out_shape=[out, out, out],
        compiler_params=_cp("parallel"),
    )(w, g, m, v)


def _adamw_small(w, g, m, v, *, name):
    def body(w_ref, g_ref, m_ref, v_ref, d_ref, nm_ref, nv_ref):
        d_ref[...], nm_ref[...], nv_ref[...] = _adam_math(w_ref[...], g_ref[...], m_ref[...], v_ref[...])

    out = jax.ShapeDtypeStruct(w.shape, F32)
    return pl.pallas_call(body, name=name, out_shape=[out, out, out])(w, g, m, v)


def _mesh_pos():
    x, y, c = lax.axis_index("x"), lax.axis_index("y"), lax.axis_index("c")
    return x, y, c


def _peer(x, y, c, mask):
    px = 1 - x if mask & 4 else x
    py = 1 - y if mask & 2 else y
    pc = 1 - c if mask & 1 else c
    return (px, py, pc), 4 * px + 2 * py + pc


_HBM = pl.BlockSpec(memory_space=pltpu.HBM)
_SEM = pl.BlockSpec(memory_space=pltpu.SEMAPHORE)
_EFFECT = pltpu.SideEffectType.DATAFLOW_SIDE_EFFECTING


def _me():
    return 4 * lax.axis_index("x") + 2 * lax.axis_index("y") + lax.axis_index("c")


def _landing(own, me):
    land = lax.empty((N_DEV,) + own.shape, own.dtype)
    return lax.dynamic_update_slice(land, own[None], (me, 0, 0))


_ALL = tuple(range(1, N_DEV))
_CHIPS = (2, 4, 6)
GATHER_DIRECT = tuple((m, None, 0, m) for m in _ALL)
SCATTER_DIRECT = tuple((m, m, 0, m) for m in _ALL)
GATHER_CHIPS = tuple((m, None, 0, m) for m in (1,) + _CHIPS)
GATHER_SIBLING = tuple((1, m, m, m ^ 1) for m in _CHIPS)


def _copy(src, land, send_sem, recv_sem, sem, x, y, c, entry, arriving):
    to, src_m, dst_m, arr_m = entry
    peer, _ = _peer(x, y, c, to)
    blk = lambda m: _peer(x, y, c, m)[1]
    return pltpu.make_async_remote_copy(
        src_ref=src if src_m is None else src.at[blk(src_m)],
        dst_ref=land.at[blk(arr_m if arriving else dst_m)],
        send_sem=send_sem.at[sem], recv_sem=recv_sem.at[sem], device_id=peer, device_id_type=MESH)


def _exchange_start(groups, plan, *, name, after=None):
    flat = [p for g in groups for p in g]
    from_land = flat[0][0] is None
    n, ng, nc = len(flat), len(groups), len(plan)
    n_buf = n if from_land else 2 * n

    def body(*refs):
        lands = refs[:n] if from_land else refs[n:2 * n]
        srcs = lands if from_land else refs[:n]
        sems = refs[n_buf + (after is not None):n_buf + (after is not None) + 2 * ng]
        token = refs[-1]
        x, y, c = _mesh_pos()
        w = 0
        for gi, g in enumerate(groups):
            for wi in range(len(g)):
                for k, entry in enumerate(plan):
                    _copy(srcs[w], lands[w], sems[2 * gi], sems[2 * gi + 1], wi * nc + k, x, y, c, entry,
                          False).start()
                w += 1
        token[...] = jnp.zeros_like(token)

    sem_shapes = []
    for g in groups:
        sem_shapes += [pltpu.SemaphoreType.DMA((len(g) * nc,))] * 2
    args = [] if from_land else [pltpu.with_memory_space_constraint(s, pltpu.HBM) for s, _ in flat]
    args += [pltpu.with_memory_space_constraint(l, pltpu.HBM) for _, l in flat]
    extra = [] if after is None else [after]
    outs = pl.pallas_call(
        body,
        name=name,
        in_specs=[_HBM] * n_buf + [pl.BlockSpec(memory_space=pl.ANY)] * len(extra),
        out_specs=[_SEM] * (2 * ng) + [_HBM] * n_buf + [pl.BlockSpec(memory_space=pltpu.VMEM)],
        out_shape=sem_shapes + [pltpu.HBM(a.shape, a.dtype) for a in args]
        + [jax.ShapeDtypeStruct((SUBLANES, LANES), F32)],
        input_output_aliases={i: 2 * ng + i for i in range(n_buf)},
        compiler_params=pltpu.CompilerParams(has_side_effects=_EFFECT),
    )(*args, *extra)
    sems, thru, token = outs[:2 * ng], outs[2 * ng:2 * ng + n_buf], outs[-1]
    res, w = [], 0
    for gi, g in enumerate(groups):
        m = len(g)
        srcs = [None] * m if from_land else list(thru[w:w + m])
        lands = list(thru[w:w + m]) if from_land else list(thru[n + w:n + w + m])
        res.append((sems[2 * gi], sems[2 * gi + 1], srcs, lands))
        w += m
    return res, token


def _exchange_wait(group, plan, after, *, name):
    send_sems, recv_sems, srcs_in, lands_in = group
    n, nc = len(lands_in), len(plan)
    from_land = srcs_in[0] is None
    n_buf = n if from_land else 2 * n

    def body(*refs):
        lands = refs[:n] if from_land else refs[n:2 * n]
        srcs = lands if from_land else refs[:n]
        send_sem, recv_sem = refs[n_buf], refs[n_buf + 1]
        x, y, c = _mesh_pos()
        for w in range(n):
            for k, entry in enumerate(plan):
                cp = _copy(srcs[w], lands[w], send_sem, recv_sem, w * nc + k, x, y, c, entry, True)
                cp.wait_send()
                cp.wait_recv()

    bufs = lands_in if from_land else srcs_in + lands_in
    outs = pl.pallas_call(
        body,
        name=name,
        in_specs=[_HBM] * n_buf + [_SEM, _SEM, pl.BlockSpec(memory_space=pl.ANY)],
        out_specs=[_HBM] * n_buf,
        out_shape=[pltpu.HBM(a.shape, a.dtype) for a in bufs],
        input_output_aliases={i: i for i in range(n_buf)},
        compiler_params=pltpu.CompilerParams(has_side_effects=_EFFECT),
    )(*bufs, send_sems, recv_sems, after)
    if from_land:
        return [None] * n, list(outs)
    return list(outs[:n]), list(outs[n:])


def _all_reduce_small(parts, rows, width, *, name, dep=None):
    n = len(parts)

    def body(*refs):
        ins = refs[:n]
        o_ref, pack_ref, buf_ref, send_sems, recv_sems = refs[-5:]
        x, y, c_ = _mesh_pos()
        me = 4 * x + 2 * y + c_
        pack_ref[...] = jnp.zeros_like(pack_ref)
        for ref, (arr, r0, nr) in zip(ins, parts):
            pack_ref[r0:r0 + nr, 0:arr.shape[1]] = ref[0:nr, :]
        sends, recvs = [], []
        for k in range(N_DEV - 1):
            peer, pidx = _peer(x, y, c_, k + 1)
            cp = pltpu.make_async_remote_copy(
                src_ref=pack_ref, dst_ref=buf_ref.at[me], send_sem=send_sems.at[k], recv_sem=recv_sems.at[k],
                device_id=peer, device_id_type=MESH)
            cp.start()
            sends.append(cp)
            recvs.append(pltpu.make_async_remote_copy(
                src_ref=pack_ref, dst_ref=buf_ref.at[pidx], send_sem=send_sems.at[k], recv_sem=recv_sems.at[k],
                device_id=peer, device_id_type=MESH))
        buf_ref[me] = pack_ref[...]
        for rc in recvs:
            rc.wait_recv()
        for cp in sends:
            cp.wait_send()
        acc = buf_ref[0]
        for s in range(1, N_DEV):
            acc = acc + buf_ref[s]
        o_ref[...] = acc

    vmem = pl.BlockSpec(memory_space=pltpu.VMEM)
    in_specs = [vmem] * n
    args = [p[0] for p in parts]
    if dep is not None:
        in_specs.append(pl.BlockSpec(memory_space=pl.ANY))
        args.append(dep)
    return pl.pallas_call(
        body,
        name=name,
        in_specs=in_specs,
        out_specs=vmem,
        out_shape=jax.ShapeDtypeStruct((rows, width), F32),
        scratch_shapes=[
            pltpu.VMEM((rows, width), F32),
            pltpu.VMEM((N_DEV, rows, width), F32),
            pltpu.SemaphoreType.DMA((N_DEV - 1,)),
            pltpu.SemaphoreType.DMA((N_DEV - 1,)),
        ],
    )(*args)


def _rope_tables(t):
    half = HEAD_DIM // 2
    inv_freq = ROPE_THETA ** (-jnp.arange(half, dtype=F32) / half)
    ang = jnp.arange(t, dtype=jnp.int32).astype(F32)[:, None] * inv_freq[None, :]
    cos, sin = jnp.cos(ang), jnp.sin(ang)
    cosf = jnp.concatenate([cos, cos, cos, cos], axis=1)
    sinf = jnp.concatenate([-sin, sin, -sin, sin], axis=1)
    return cosf, sinf


def _local_step(x, mem, target, gains, sinks, aw, cw, pre_w, get_w, put_g, dep0=None):
    t, d = x.shape
    nq = aw // HEAD_DIM
    kw = aw // Q_PER_KV
    z0 = aw + 2 * kw
    gb0, gc0 = z0 + cw, z0 + 2 * cw
    ga0 = z0 + 3 * cw
    gcm0 = ga0 + d
    cosf, sinf = _rope_tables(t)

    u1 = _rms_fwd(x, gains["g_mix"], name="rms_mix", dep=dep0)
    mem_n = _rms_fwd(mem, gains["g_mem"], name="rms_mem", dep=dep0)
    pre_w("w_in", u1)
    w_in_t = get_w("w_in", u1)
    proj = _mm(u1, w_in_t, mode="nt", tm=1024, tn=512, tk=2048, out_dtype=F32, name="mm_in")
    o_attn, q_rot, k_rot = _swa_fwd(proj, cosf, sinf, sinks, nq=nq, name="swa_fwd", dep=pre_w("conv_w8", proj))
    conv_w8 = get_w("conv_w8", o_attn)
    w_attn_proj, w_conv_proj, w_mix_out = (get_w(n, o_attn) for n in ("w_attn_proj", "w_conv_proj", "w_mix_out"))
    w_xq, w_xkv, w_xo = (get_w(n, o_attn) for n in ("w_xq", "w_xkv", "w_xo"))
    y_attn = _mm(o_attn, w_attn_proj, mode="nn", tm=1024, tn=1024, tk=1024, out_dtype=F32, name="mm_attn_proj")
    cy = _conv_fwd(proj, conv_w8, z0=z0, gb0=gb0, gc0=gc0, cw=cw, name="conv_fwd")
    y_conv, merged = _gate_fwd(cy, w_conv_proj, proj, y_attn, ga0=ga0, gc0=gcm0, name="mm_conv_proj")
    h1, u2 = _mm(merged, w_mix_out, mode="nn", tm=512, tn=d, tk=2048, out_dtype=F32, name="mm_mix_out", residual=x,
                 rms_gain=gains["g_xattn"])
    xq = _mm(u2, w_xq, mode="nn", tm=1024, tn=512, tk=2048, out_dtype=BF, name="mm_xq",
             dep=pre_w("w_ffn_in", h1))
    kv = _mm(mem_n, w_xkv, mode="nn", tm=256, tn=1024, tk=2048, out_dtype=BF, name="mm_xkv")
    o_x = _xattn_fwd(xq, kv, name="xattn_fwd")
    h2, u3 = _mm(o_x, w_xo, mode="nn", tm=512, tn=d, tk=512, out_dtype=F32, name="mm_xo", residual=h1,
                 rms_gain=gains["g_ffn"])
    w_ffn_in = get_w("w_ffn_in", xq)
    hid2, act = _ffn_in_fwd(u3, w_ffn_in, name="mm_ffn_in", dep=pre_w("w_ffn_out", u3))
    w_ffn_out = get_w("w_ffn_out", act)
    h3 = _mm(act, w_ffn_out, mode="nn", tm=512, tn=1024, tk=8192, out_dtype=F32, name="mm_ffn_out", residual=h2)

    tt = 8192
    dh3, dh3b, loss_tile, dg_final = _loss_head(h3, target, gains["g_final"], name="loss_head")
    tok = put_g("w_ffn_out", _mm(act, dh3b, mode="tn", tm=512, tn=1024, tk=tt, out_dtype=BF, name="mm_dw_ffn_out"))
    dhid2 = _ffn_out_bwd(dh3b, w_ffn_out, hid2, name="mm_dact", dep=tok)
    f2 = w_ffn_in.shape[1]
    tok = put_g("w_ffn_in", _mm(u3, dhid2, mode="tn", tm=1024, tn=f2 // N_DEV, tk=tt, out_dtype=BF,
                                name="mm_dw_ffn_in", b_planes=2, stacked=True), stacked=True)
    du3 = _mm(dhid2, w_ffn_in, mode="nt", tm=1024, tn=1024, tk=2816, out_dtype=F32, name="mm_du3", dep=tok,
              a_planes=2)
    dh2, dh2b, dg_ffn = _rms_bwd(du3, h2, gains["g_ffn"], dh3, name="rms_ffn_bwd")
    put_g("w_xo", _mm(o_x, dh2b, mode="tn", tm=512, tn=d // N_DEV, tk=tt, out_dtype=BF, name="mm_dw_xo",
                      stacked=True), stacked=True)
    do_x = _mm(dh2b, w_xo, mode="nt", tm=1024, tn=512, tk=2048, out_dtype=BF, name="mm_do_x")
    dxq, dkv = _xattn_bwd(xq, kv, do_x, name="xattn_bwd")
    put_g("w_xkv", _mm(mem_n, dkv, mode="tn", tm=1024, tn=1024, tk=256, out_dtype=BF, name="mm_dw_xkv"))
    tok = put_g("w_xq", _mm(u2, dxq, mode="tn", tm=1024, tn=512, tk=tt, out_dtype=BF, name="mm_dw_xq"))
    tok_xq = tok
    dmem_n = _mm(dkv, w_xkv, mode="nt", tm=256, tn=1024, tk=1024, out_dtype=F32, name="mm_dmem")
    _, _, dg_mem = _rms_bwd(dmem_n, mem, gains["g_mem"], None, name="rms_mem_bwd")
    dh1, dh1b, dg_xattn = _rms_bwd(dxq, h1, gains["g_xattn"], dh2, name="rms_xattn_bwd", du_w=w_xq, dep=tok_xq)
    put_g("w_mix_out", _mm(merged, dh1b, mode="tn", tm=1024, tn=1024, tk=tt, out_dtype=BF, name="mm_dw_mix_out"))
    dya, dyc, dga, dgc = _gate_bwd(dh1b, w_mix_out, proj, y_attn, y_conv, ga0=ga0, gc0=gcm0, name="mm_dmerged")
    put_g("w_attn_proj", _mm(o_attn, dya, mode="tn", tm=1024, tn=d // N_DEV, tk=tt, out_dtype=BF,
                             name="mm_dw_attn_proj", stacked=True), stacked=True)
    do_attn = _mm(dya, w_attn_proj, mode="nt", tm=1024, tn=1024, tk=2048, out_dtype=BF, name="mm_do_attn")
    tok = put_g("w_conv_proj", _mm(cy, dyc, mode="tn", tm=1024, tn=d // N_DEV, tk=tt, out_dtype=BF,
                                   name="mm_dw_conv_proj", stacked=True), stacked=True)
    dcy = _mm(dyc, w_conv_proj, mode="nt", tm=1024, tn=1024, tk=2048, out_dtype=F32, name="mm_dcy", dep=tok)
    dz, dgb, dgcv, dconv_w8 = _conv_bwd(proj, conv_w8, dcy, z0=z0, gb0=gb0, gc0=gc0, cw=cw, name="conv_bwd")
    dq, dk, dv, dsink_tile = _swa_bwd(q_rot, k_rot, proj, do_attn, cosf, sinf, sinks, nq=nq, name="swa_bwd")
    dproj = jnp.concatenate([dq, dk, dv, dz, dgb, dgcv, dga, dgc], axis=1)
    for hi in range(2):
        tok = put_g("w_in_%d" % hi, _mm(dproj, u1, mode="tn", tm=512, tn=d // 2, tk=tt, out_dtype=BF,
                                        name="mm_dw_in_%d" % hi, b_cols=(hi * (d // 2), d // 2), dep=tok))
    du1 = _mm(dproj, w_in_t, mode="nn", tm=512, tn=1024, tk=4352, out_dtype=F32, name="mm_du1", dep=tok)
    grad_x, _, dg_mix = _rms_bwd(du1, x, gains["g_mix"], dh1, name="rms_mix_bwd")

    small = {
        "g_mix": dg_mix, "g_xattn": dg_xattn, "g_mem": dg_mem, "g_ffn": dg_ffn, "g_final": dg_final,
        "attn_sinks": dsink_tile, "conv_w8": dconv_w8, "loss": loss_tile,
    }
    return grad_x, small


_COL_SHARDED = ("w_in", "w_attn_proj", "w_conv_proj", "w_xo", "w_ffn_in")
_ROW_SHARDED = ("w_mix_out", "w_xq", "w_xkv", "w_ffn_out")
_BIG = _COL_SHARDED + _ROW_SHARDED
_GAINS = ("g_mix", "g_xattn", "g_mem", "g_ffn", "g_final")
_GATHER_GROUPS = (("w_in",), ("conv_w8", "w_attn_proj", "w_conv_proj", "w_mix_out", "w_xq", "w_xkv", "w_xo"),
                  ("w_ffn_in",), ("w_ffn_out",))
_SCATTER_GROUPS = (("w_ffn_out",), ("w_ffn_in",), ("w_xo", "w_xq", "w_xkv"),
                   ("w_mix_out", "w_attn_proj", "w_conv_proj"), ("w_in_0",), ("w_in_1",))
_WEIGHTS = ("g_mix", "w_in", "conv_w", "attn_sinks", "w_attn_proj", "w_conv_proj", "w_mix_out", "g_xattn", "g_mem",
            "w_xq", "w_xkv", "w_xo", "g_ffn", "w_ffn_in", "w_ffn_out", "g_final")


def _unstack(g, col_sharded):
    n, r, c = g.shape
    if col_sharded:
        return jnp.transpose(g, (1, 0, 2)).reshape(r, n * c)
    return g.reshape(n * r, c)


def _stack(w, col_sharded):
    r, c = w.shape
    if col_sharded:
        return jnp.transpose(w.reshape(r, N_DEV, c // N_DEV), (1, 0, 2))
    return w.reshape(N_DEV, r // N_DEV, c)


def kernel(x, mem, g_mix, w_in, conv_w, attn_sinks, w_attn_proj, w_conv_proj, w_mix_out, g_xattn, g_mem, w_xq, w_xkv, w_xo, g_ffn, w_ffn_in, w_ffn_out, g_final, loss_target, m_g_mix, m_w_in, m_conv_w, m_attn_sinks, m_w_attn_proj, m_w_conv_proj, m_w_mix_out, m_g_xattn, m_g_mem, m_w_xq, m_w_xkv, m_w_xo, m_g_ffn, m_w_ffn_in, m_w_ffn_out, m_g_final, v_g_mix, v_w_in, v_conv_w, v_attn_sinks, v_w_attn_proj, v_w_conv_proj, v_w_mix_out, v_g_xattn, v_g_mem, v_w_xq, v_w_xkv, v_w_xo, v_g_ffn, v_w_ffn_in, v_w_ffn_out, v_g_final):
    w_ = dict(g_mix=g_mix, w_in=w_in, conv_w=conv_w, attn_sinks=attn_sinks, w_attn_proj=w_attn_proj,
              w_conv_proj=w_conv_proj, w_mix_out=w_mix_out, g_xattn=g_xattn, g_mem=g_mem, w_xq=w_xq, w_xkv=w_xkv,
              w_xo=w_xo, g_ffn=g_ffn, w_ffn_in=w_ffn_in, w_ffn_out=w_ffn_out, g_final=g_final)
    m_ = dict(g_mix=m_g_mix, w_in=m_w_in, conv_w=m_conv_w, attn_sinks=m_attn_sinks, w_attn_proj=m_w_attn_proj,
              w_conv_proj=m_w_conv_proj, w_mix_out=m_w_mix_out, g_xattn=m_g_xattn, g_mem=m_g_mem, w_xq=m_w_xq,
              w_xkv=m_w_xkv, w_xo=m_w_xo, g_ffn=m_g_ffn, w_ffn_in=m_w_ffn_in, w_ffn_out=m_w_ffn_out,
              g_final=m_g_final)
    v_ = dict(g_mix=v_g_mix, w_in=v_w_in, conv_w=v_conv_w, attn_sinks=v_attn_sinks, w_attn_proj=v_w_attn_proj,
              w_conv_proj=v_w_conv_proj, w_mix_out=v_w_mix_out, g_xattn=v_g_xattn, g_mem=v_g_mem, w_xq=v_w_xq,
              w_xkv=v_w_xkv, w_xo=v_w_xo, g_ffn=v_g_ffn, w_ffn_in=v_w_ffn_in, w_ffn_out=v_w_ffn_out,
              g_final=v_g_final)
    t, d = x.shape[1], x.shape[2]
    nq = attn_sinks.shape[-1]
    cw_shard = conv_w.shape[-1]
    cw = cw_shard * N_DEV

    def two_d(a):
        return a.reshape(a.shape[-2], a.shape[-1]) if a.ndim == 3 else a.reshape(1, a.shape[-1])

    me = _me()
    col = (set(_COL_SHARDED) | {"conv_w8"}) - {"w_in"}

    shards = {"w_in": two_d(w_in).T.astype(BF)}
    first, token = _exchange_start(
        [[(shards[n], _landing(shards[n], me)) for n in g] for g in _GATHER_GROUPS[:1]], GATHER_CHIPS,
        name="gather_start_0")
    for n in _BIG:
        if n != "w_in":
            shards[n] = _to_bf16(two_d(w_[n]), name="cast_" + n, dep=token)
    shards["conv_w8"] = jnp.zeros((SUBLANES, cw_shard), F32).at[:3].set(two_d(conv_w))
    rest, token = _exchange_start(
        [[(shards[n], _landing(shards[n], me)) for n in g] for g in _GATHER_GROUPS[1:]], GATHER_CHIPS,
        name="gather_start_1", after=token)
    gathers = first + rest
    passes, full = {}, {}

    def group_of(name):
        return [name in g for g in _GATHER_GROUPS].index(True)

    def pre_w(name, after):
        gi = group_of(name)
        _, lands = _exchange_wait(gathers[gi], GATHER_CHIPS, after, name="gather_wait_%d" % gi)
        started, tok = _exchange_start([[(None, land) for land in lands]], GATHER_SIBLING,
                                       name="gather_pass_%d" % gi)
        passes[gi] = started[0]
        return tok

    def get_w(name, after):
        if name not in full:
            gi = group_of(name)
            _, lands = _exchange_wait(passes[gi], GATHER_SIBLING, after, name="gather_pass_wait_%d" % gi)
            for n, land in zip(_GATHER_GROUPS[gi], lands):
                full[n] = _unstack(land, n in col)
        return full[name]

    pending, scatters = {}, []

    def put_g(name, dw, stacked=False):
        pending[name] = dw if stacked else _stack(dw, name in col)
        gi = [name in g for g in _SCATTER_GROUPS].index(True)
        group = _SCATTER_GROUPS[gi]
        if not all(n in pending for n in group):
            return None
        pairs = [(pending[n], lax.empty(pending[n].shape, pending[n].dtype)) for n in group]
        started, tok = _exchange_start([pairs], SCATTER_DIRECT, name="scatter_start_%d" % gi)
        scatters.append((gi, started[0]))
        return tok

    gains = {n: two_d(w_[n]) for n in _GAINS}
    grad_x, small = _local_step(
        x[0], mem[0], loss_target[0], gains, attn_sinks.reshape(nq), w_attn_proj.shape[-2], cw, pre_w, get_w, put_g,
        dep0=token)

    grads, deltas, new_m, new_v = {}, {}, {}, {}
    me1 = me.reshape(1).astype(jnp.int32)
    after, halves = grad_x, []
    for gi, started in scatters:
        mine, parts = _exchange_wait(started, SCATTER_DIRECT, after, name="scatter_wait_%d" % gi)
        for n, own, p in zip(_SCATTER_GROUPS[gi], mine, parts):
            if n.startswith("w_in_"):
                halves.append(_sum_partials(p, own, me1, name="sum_" + n).T)
                after = halves[-1]
                if len(halves) < 2:
                    continue
                n, g = "w_in", jnp.concatenate(halves, axis=0)
                dl, nm, nv = _adamw_rows(two_d(w_[n]), g, two_d(m_[n]), two_d(v_[n]), name="adamw_" + n)
            else:
                g, dl, nm, nv = _adamw_sum(p, own, me1, two_d(w_[n]), two_d(m_[n]), two_d(v_[n]), name="adamw_" + n)
            shape = w_[n].shape
            grads[n], deltas[n], new_m[n], new_v[n] = (a.reshape(shape) for a in (g, dl, nm, nv))
            after = g

    parts = [(small[n], i, 1) for i, n in enumerate(_GAINS)]
    parts += [(small["attn_sinks"], 5, 1), (small["loss"], 6, 1), (small["conv_w8"], 8, 3)]
    red = _all_reduce_small(parts, 2 * SUBLANES, max(d, cw), name="reduce_small", dep=after)
    loss = red[6, 0]
    small_g = {n: red[i:i + 1, :d] for i, n in enumerate(_GAINS)}
    small_g["attn_sinks"] = red[5:6, :nq]
    small_g["conv_w"] = lax.dynamic_slice(red, (8, me * cw_shard), (3, cw_shard))
    for n in _GAINS + ("attn_sinks", "conv_w"):
        shape = w_[n].shape
        g = small_g[n]
        dl, nm, nv = _adamw_small(two_d(w_[n]), g, two_d(m_[n]), two_d(v_[n]), name="adamw_" + n)
        grads[n], deltas[n], new_m[n], new_v[n] = (a.reshape(shape) for a in (g, dl, nm, nv))

    return (loss, grad_x[None], *[grads[n] for n in _WEIGHTS], *[deltas[n] for n in _WEIGHTS],
            *[new_m[n] for n in _WEIGHTS], *[new_v[n] for n in _WEIGHTS])
```

```python
import functools
import math

import jax
import jax.numpy as jnp
from jax import lax
from jax.experimental import pallas as pl
from jax.experimental.pallas import tpu as pltpu

HEAD_DIM = 64
Q_PER_KV = 4
WINDOW = 128
X_HEAD_DIM = 128
ROPE_THETA = 10000.0
EPS = 1e-6
ADAM_LR = 0.001
ADAM_B1 = 0.9
ADAM_B2 = 0.999
ADAM_EPS = 1e-08
ADAM_WD = 0.01
ADAM_STEP = 10

N_DEV = 8
LANES = 128
SUBLANES = 8
VMEM_LIMIT_BYTES = 56 * 1024 * 1024
BF = jnp.bfloat16
F32 = jnp.float32
MESH = pl.DeviceIdType.MESH


def _cp(*sem):
    return pltpu.CompilerParams(dimension_semantics=sem, vmem_limit_bytes=VMEM_LIMIT_BYTES)


def _sigmoid(x):
    return 1.0 / (1.0 + jnp.exp(-x))


_DIMS = {
    "nn": (((1,), (0,)), ((), ())),
    "nt": (((1,), (1,)), ((), ())),
    "tn": (((0,), (0,)), ((), ())),
}


def _fit(dim, tile):
    if dim <= tile:
        return dim
    for t in range(tile // LANES * LANES, 0, -LANES):
        if dim % t == 0:
            return t
    return dim


def _mm(a, b, *, mode, tm, tn, tk, out_dtype, name, residual=None, dep=None, a_planes=1, b_planes=1,
        stacked=False, b_cols=None, a_cols=None, rms_gain=None):
    if a_planes > 1:
        assert mode == "nt"
        (_, m, kp), (n, k) = a.shape, b.shape
        assert kp * a_planes == k
    elif b_planes > 1:
        assert mode == "tn"
        (k, m), (_, k2, np_) = a.shape, b.shape
        n = np_ * b_planes
        assert k == k2
    elif mode == "nn":
        (m, k), (k2, n) = a.shape, b.shape
        assert k == k2, (name, a.shape, b.shape)
    elif mode == "nt":
        (m, k), (n, k2) = a.shape, b.shape
        if a_cols is not None:
            k = a_cols[1]
        assert k == k2, (name, a.shape, b.shape)
    else:
        (k, m), (k2, n) = a.shape, b.shape
        assert k == k2, (name, a.shape, b.shape)
    tm, tn, tk = _fit(m, tm), _fit(n // b_planes, tn), _fit(k // a_planes, tk)
    assert m % tm == 0 and (n // b_planes) % tn == 0 and (k // a_planes) % tk == 0, (name, m, n, k, tm, tn, tk)
    ka0 = 0
    if a_cols is not None:
        assert mode == "nt" and a_planes == 1 and a_cols[0] % tk == 0
        ka0 = a_cols[0] // tk
    j0 = 0
    if b_cols is not None:
        assert mode == "tn" and b_planes == 1 and b_cols[0] % tn == 0 and b_cols[1] % tn == 0
        j0, n = b_cols[0] // tn, b_cols[1]
    nk = k // tk
    nkp, njp = nk // a_planes, n // tn // b_planes
    if a_planes > 1:
        a_spec = pl.BlockSpec((None, tm, tk), lambda i, j, kk: (kk // nkp, i, kk % nkp))
    elif mode == "tn":
        a_spec = pl.BlockSpec((tk, tm), lambda i, j, kk: (kk, i))
    else:
        a_spec = pl.BlockSpec((tm, tk), lambda i, j, kk: (i, kk + ka0))
    if b_planes > 1:
        b_spec = pl.BlockSpec((None, tk, tn), lambda i, j, kk: (j // njp, kk, j % njp))
    elif mode == "nt":
        b_spec = pl.BlockSpec((tn, tk), lambda i, j, kk: (j, kk))
    else:
        b_spec = pl.BlockSpec((tk, tn), lambda i, j, kk: (kk, j + j0))
    if stacked:
        assert residual is None
        o_spec = pl.BlockSpec((None, tm, tn), lambda i, j, kk: (j, i, 0))
        out_shape = jax.ShapeDtypeStruct((n // tn, m, tn), out_dtype)
    else:
        o_spec = pl.BlockSpec((tm, tn), lambda i, j, kk: (i, j))
        out_shape = jax.ShapeDtypeStruct((m, n), out_dtype)
    dims = _DIMS[mode]
    has_res = residual is not None
    has_rms = rms_gain is not None
    assert not has_rms or (tn == n and not stacked)
    n_in = 2 + has_res + has_rms + (dep is not None)

    def body(*refs):
        a_ref, b_ref, r_ref, o_ref = refs[0], refs[1], refs[2], refs[n_in]
        part = lax.dot_general(a_ref[...].astype(BF), b_ref[...].astype(BF), dims, preferred_element_type=F32)

        def finish(acc):
            if has_res:
                acc = r_ref[...] + acc
            o_ref[...] = acc.astype(out_dtype)
            if has_rms:
                r = lax.rsqrt(jnp.mean(acc * acc, axis=-1, keepdims=True) + EPS)
                refs[n_in + 1][...] = ((acc * r) * refs[2 + has_res][...]).astype(BF)

        if nk == 1:
            finish(part)
        else:
            acc_ref = refs[-1]
            kk = pl.program_id(2)

            @pl.when(kk == 0)
            def _():
                acc_ref[...] = part

            @pl.when(kk > 0)
            def _():
                acc_ref[...] += part

            @pl.when(kk == nk - 1)
            def _():
                finish(acc_ref[...])

    in_specs = [a_spec, b_spec] + ([o_spec] if has_res else [])
    args = (a, b) + ((residual,) if has_res else ())
    out_specs = o_spec
    if has_rms:
        in_specs.append(pl.BlockSpec((1, n), lambda i, j, kk: (0, 0)))
        args += (rms_gain,)
        out_specs = [o_spec, o_spec]
        out_shape = [out_shape, jax.ShapeDtypeStruct((m, n), BF)]
    if dep is not None:
        in_specs.append(pl.BlockSpec(memory_space=pl.ANY))
        args += (dep,)
    return pl.pallas_call(
        body,
        name=name,
        grid=(m // tm, n // tn, nk),
        in_specs=in_specs,
        out_specs=out_specs,
        out_shape=out_shape,
        scratch_shapes=[pltpu.VMEM((tm, tn), F32)] if nk > 1 else [],
        compiler_params=_cp("parallel", "parallel", "arbitrary"),
    )(*args)


def _rms_fwd(h, g, *, name, tm=512, dep=None):
    t, d = h.shape
    tm = min(tm, t)

    def body(*refs):
        h_ref, g_ref, u_ref = refs[0], refs[1], refs[-1]
        hv = h_ref[...]
        r = lax.rsqrt(jnp.mean(hv * hv, axis=-1, keepdims=True) + EPS)
        u_ref[...] = ((hv * r) * g_ref[...]).astype(BF)

    in_specs = [pl.BlockSpec((tm, d), lambda i: (i, 0)), pl.BlockSpec((1, d), lambda i: (0, 0))]
    args = (h, g)
    for one in () if dep is None else (dep if isinstance(dep, tuple) else (dep,)):
        in_specs.append(pl.BlockSpec(memory_space=pl.ANY))
        args += (one,)
    return pl.pallas_call(
        body,
        name=name,
        grid=(t // tm,),
        in_specs=in_specs,
        out_specs=pl.BlockSpec((tm, d), lambda i: (i, 0)),
        out_shape=jax.ShapeDtypeStruct((t, d), BF),
        compiler_params=_cp("parallel"),
    )(*args)


def _rms_bwd(du, h, g, dres, *, name, tm=256, du_w=None, dep=None):
    t, d = h.shape
    tm = min(tm, t)
    want_dh = dres is not None

    def body(*refs):
        du_ref, h_ref, g_ref, dg_ref = refs[0], refs[1], refs[2], refs[-1]
        if want_dh:
            dres_ref, dh_ref, dhb_ref = refs[3], refs[-3], refs[-2]
        i = pl.program_id(0)
        hv = h_ref[...]
        duv = du_ref[...]
        if du_w is not None:
            duv = lax.dot_general(duv, refs[3 + want_dh][...], _DIMS["nt"], preferred_element_type=F32)
        r = lax.rsqrt(jnp.mean(hv * hv, axis=-1, keepdims=True) + EPS)
        nv = hv * r
        if want_dh:
            gy = duv * g_ref[...]
            dh = dres_ref[...] + r * (gy - nv * jnp.mean(nv * gy, axis=-1, keepdims=True))
            dh_ref[...] = dh
            dhb_ref[...] = dh.astype(BF)

        @pl.when(i == 0)
        def _():
            dg_ref[...] = jnp.zeros_like(dg_ref)

        dg_ref[...] += jnp.sum(duv * nv, axis=0, keepdims=True)

    row = pl.BlockSpec((tm, d), lambda i: (i, 0))
    vec = pl.BlockSpec((1, d), lambda i: (0, 0))
    du_spec = row if du_w is None else pl.BlockSpec((tm, du.shape[1]), lambda i: (i, 0))
    if want_dh:
        in_specs, args = [du_spec, row, vec, row], (du, h, g, dres)
        out_specs = [row, row, vec]
        out_shape = [jax.ShapeDtypeStruct((t, d), F32), jax.ShapeDtypeStruct((t, d), BF),
                     jax.ShapeDtypeStruct((1, d), F32)]
    else:
        in_specs, args = [du_spec, row, vec], (du, h, g)
        out_specs = [vec]
        out_shape = [jax.ShapeDtypeStruct((1, d), F32)]
    if du_w is not None:
        in_specs.append(pl.BlockSpec(du_w.shape, lambda i: (0, 0)))
        args += (du_w,)
    if dep is not None:
        in_specs.append(pl.BlockSpec(memory_space=pl.ANY))
        args += (dep,)
    outs = pl.pallas_call(
        body,
        name=name,
        grid=(t // tm,),
        in_specs=in_specs,
        out_specs=out_specs,
        out_shape=out_shape,
        compiler_params=_cp("arbitrary"),
    )(*args)
    return (outs[0], outs[1], outs[2]) if want_dh else (None, None, outs[0])


def _loss_head(h, target, g, *, name, tm=256):
    t, d = h.shape
    tm = min(tm, t)

    def body(h_ref, t_ref, g_ref, dh_ref, dhb_ref, loss_ref, dg_ref):
        i = pl.program_id(0)
        hv = h_ref[...]
        gv = g_ref[...]
        r = lax.rsqrt(jnp.mean(hv * hv, axis=-1, keepdims=True) + EPS)
        nv = hv * r
        e = nv * gv - t_ref[...]
        per_tok = jnp.mean(e * e, axis=-1, keepdims=True)
        lp = 0.5 * jnp.sum(per_tok, axis=0, keepdims=True)
        dy = e * (1.0 / d)
        gy = dy * gv
        dh = r * (gy - nv * jnp.mean(nv * gy, axis=-1, keepdims=True))
        dh_ref[...] = dh
        dhb_ref[...] = dh.astype(BF)

        @pl.when(i == 0)
        def _():
            loss_ref[...] = jnp.zeros_like(loss_ref)
            dg_ref[...] = jnp.zeros_like(dg_ref)

        loss_ref[...] += jnp.broadcast_to(lp, loss_ref.shape)
        dg_ref[...] += jnp.sum(dy * nv, axis=0, keepdims=True)

    row = pl.BlockSpec((tm, d), lambda i: (i, 0))
    vec = pl.BlockSpec((1, d), lambda i: (0, 0))
    return pl.pallas_call(
        body,
        name=name,
        grid=(t // tm,),
        in_specs=[row, row, vec],
        out_specs=[row, row, pl.BlockSpec((SUBLANES, LANES), lambda i: (0, 0)), vec],
        out_shape=[
            jax.ShapeDtypeStruct((t, d), F32),
            jax.ShapeDtypeStruct((t, d), BF),
            jax.ShapeDtypeStruct((SUBLANES, LANES), F32),
            jax.ShapeDtypeStruct((1, d), F32),
        ],
        compiler_params=_cp("arbitrary"),
    )(h, target, g)


def _ffn_in_fwd(u, w, *, name, tm=1024, tn=512, dep=None):
    t, d = u.shape
    f = w.shape[1] // 2
    tm, tn = _fit(t, tm), _fit(f, tn)
    nf = f // tn

    def body(*refs):
        u_ref, wa_ref, wb_ref, hid_ref, act_ref = refs[0], refs[1], refs[2], refs[-2], refs[-1]
        uv = u_ref[...]
        a = jnp.dot(uv, wa_ref[...], preferred_element_type=F32)
        b = jnp.dot(uv, wb_ref[...], preferred_element_type=F32)
        hid_ref[0] = a.astype(BF)
        hid_ref[1] = b.astype(BF)
        act_ref[...] = ((a * _sigmoid(a)) * b).astype(BF)

    in_specs = [
        pl.BlockSpec((tm, d), lambda i, j: (i, 0)),
        pl.BlockSpec((d, tn), lambda i, j: (0, j)),
        pl.BlockSpec((d, tn), lambda i, j: (0, nf + j)),
    ]
    args = (u, w, w)
    if dep is not None:
        in_specs.append(pl.BlockSpec(memory_space=pl.ANY))
        args += (dep,)
    return pl.pallas_call(
        body,
        name=name,
        grid=(t // tm, nf),
        in_specs=in_specs,
        out_specs=[pl.BlockSpec((2, tm, tn), lambda i, j: (0, i, j)), pl.BlockSpec((tm, tn), lambda i, j: (i, j))],
        out_shape=[jax.ShapeDtypeStruct((2, t, f), BF), jax.ShapeDtypeStruct((t, f), BF)],
        compiler_params=_cp("parallel", "parallel"),
    )(*args)


def _ffn_out_bwd(dh, w_out, hid2, *, name, tm=1024, tn=512, dep=None):
    t, d = dh.shape
    f = w_out.shape[0]
    tm, tn = _fit(t, tm), _fit(f, tn)

    def body(*refs):
        dh_ref, w_ref, hid_ref, o_ref = refs[0], refs[1], refs[2], refs[-1]
        dact = lax.dot_general(dh_ref[...], w_ref[...], _DIMS["nt"], preferred_element_type=F32)
        a = hid_ref[0].astype(F32)
        b = hid_ref[1].astype(F32)
        sg = _sigmoid(a)
        o_ref[0] = (dact * b * (sg * (1.0 + a * (1.0 - sg)))).astype(BF)
        o_ref[1] = (dact * (a * sg)).astype(BF)

    pair = pl.BlockSpec((2, tm, tn), lambda i, j: (0, i, j))
    in_specs = [pl.BlockSpec((tm, d), lambda i, j: (i, 0)), pl.BlockSpec((tn, d), lambda i, j: (j, 0)), pair]
    args = (dh, w_out, hid2)
    if dep is not None:
        in_specs.append(pl.BlockSpec(memory_space=pl.ANY))
        args += (dep,)
    return pl.pallas_call(
        body,
        name=name,
        grid=(t // tm, f // tn),
        in_specs=in_specs,
        out_specs=pair,
        out_shape=jax.ShapeDtypeStruct((2, t, f), BF),
        compiler_params=_cp("parallel", "parallel"),
    )(*args)


def _gate_fwd(cy, w, proj, ya, *, ga0, gc0, name, tm=1024, tc=512):
    t, d = ya.shape
    kc = cy.shape[1]
    tm, tc = _fit(t, tm), math.gcd(tc, d, ga0, gc0)
    a0, c0 = ga0 // tc, gc0 // tc

    def body(cy_ref, w_ref, ga_ref, gc_ref, ya_ref, yc_ref, o_ref):
        yc = jnp.dot(cy_ref[...], w_ref[...], preferred_element_type=F32)
        yc_ref[...] = yc
        o_ref[...] = (_sigmoid(ga_ref[...]) * ya_ref[...] + _sigmoid(gc_ref[...]) * yc).astype(BF)

    blk = pl.BlockSpec((tm, tc), lambda i, j: (i, j))
    return pl.pallas_call(
        body,
        name=name,
        grid=(t // tm, d // tc),
        in_specs=[
            pl.BlockSpec((tm, kc), lambda i, j: (i, 0)),
            pl.BlockSpec((kc, tc), lambda i, j: (0, j)),
            pl.BlockSpec((tm, tc), lambda i, j: (i, a0 + j)),
            pl.BlockSpec((tm, tc), lambda i, j: (i, c0 + j)),
            blk,
        ],
        out_specs=[blk, blk],
        out_shape=[jax.ShapeDtypeStruct((t, d), F32), jax.ShapeDtypeStruct((t, d), BF)],
        compiler_params=_cp("parallel", "parallel"),
    )(cy, w, proj, proj, ya)


def _gate_bwd(dh, w, proj, ya, yc, *, ga0, gc0, name, tm=1024, tc=512):
    t, d = ya.shape
    tm, tc = _fit(t, tm), math.gcd(tc, d, ga0, gc0)
    a0, c0 = ga0 // tc, gc0 // tc

    def body(dh_ref, w_ref, ga_ref, gc_ref, ya_ref, yc_ref, dya_ref, dyc_ref, dga_ref, dgc_ref):
        dmv = lax.dot_general(dh_ref[...], w_ref[...], _DIMS["nt"], preferred_element_type=F32)
        sa = _sigmoid(ga_ref[...])
        sc = _sigmoid(gc_ref[...])
        dya_ref[...] = (dmv * sa).astype(BF)
        dyc_ref[...] = (dmv * sc).astype(BF)
        dga_ref[...] = (dmv * ya_ref[...] * (sa * (1.0 - sa))).astype(BF)
        dgc_ref[...] = (dmv * yc_ref[...] * (sc * (1.0 - sc))).astype(BF)

    blk = pl.BlockSpec((tm, tc), lambda i, j: (i, j))
    out = jax.ShapeDtypeStruct((t, d), BF)
    return pl.pallas_call(
        body,
        name=name,
        grid=(t // tm, d // tc),
        in_specs=[
            pl.BlockSpec((tm, d), lambda i, j: (i, 0)),
            pl.BlockSpec((tc, d), lambda i, j: (j, 0)),
            pl.BlockSpec((tm, tc), lambda i, j: (i, a0 + j)),
            pl.BlockSpec((tm, tc), lambda i, j: (i, c0 + j)),
            blk,
            blk,
        ],
        out_specs=[blk, blk, blk, blk],
        out_shape=[out, out, out, out],
        compiler_params=_cp("parallel", "parallel"),
    )(dh, w, proj, proj, ya, yc)


def _conv_taps(cz, czp, i):
    czp = czp * (i > 0).astype(F32)
    h1 = czp[SUBLANES - 1:SUBLANES, :]
    h2 = czp[SUBLANES - 2:SUBLANES - 1, :]
    row = lax.broadcasted_iota(jnp.int32, cz.shape, 0)
    s1 = jnp.where(row == 0, h1, pltpu.roll(cz, 1, 0))
    s2 = jnp.where(row == 0, h2, jnp.where(row == 1, h1, pltpu.roll(cz, 2, 0)))
    return s1, s2


def _conv_fwd(proj, w8, *, z0, gb0, gc0, cw, name, tm=512, tc=512):
    t = proj.shape[0]
    tm, tc = min(tm, t), math.gcd(tc, cw, z0, gb0, gc0)
    zb, bb, cb = z0 // tc, gb0 // tc, gc0 // tc
    rb = tm // SUBLANES

    def body(z_ref, gb_ref, gc_ref, zp_ref, gcp_ref, w_ref, o_ref):
        i = pl.program_id(0)
        cz = gc_ref[...] * z_ref[...]
        s1, s2 = _conv_taps(cz, gcp_ref[...] * zp_ref[...], i)
        w = w_ref[...]
        y = w[0:1, :] * s2 + w[1:2, :] * s1 + w[2:3, :] * cz
        o_ref[...] = (gb_ref[...] * y).astype(BF)

    def cur(b0):
        return pl.BlockSpec((tm, tc), lambda i, j: (i, b0 + j))

    def prev(b0):
        return pl.BlockSpec((SUBLANES, tc), lambda i, j: (jnp.maximum(i * rb - 1, 0), b0 + j))

    return pl.pallas_call(
        body,
        name=name,
        grid=(t // tm, cw // tc),
        in_specs=[cur(zb), cur(bb), cur(cb), prev(zb), prev(cb), pl.BlockSpec((SUBLANES, tc), lambda i, j: (0, j))],
        out_specs=pl.BlockSpec((tm, tc), lambda i, j: (i, j)),
        out_shape=jax.ShapeDtypeStruct((t, cw), BF),
        compiler_params=_cp("parallel", "parallel"),
    )(proj, proj, proj, proj, proj, w8)


def _conv_bwd(proj, w8, dcy, *, z0, gb0, gc0, cw, name, tm=512, tc=512):
    t = proj.shape[0]
    tm, tc = min(tm, t), math.gcd(tc, cw, z0, gb0, gc0)
    zb, bb, cb = z0 // tc, gb0 // tc, gc0 // tc
    rb = tm // SUBLANES
    nt = t // tm

    def body(z_ref, gb_ref, gc_ref, zp_ref, gcp_ref, d_ref, dn_ref, gbn_ref, w_ref, dz_ref, dgb_ref, dgc_ref, dw_ref):
        i = pl.program_id(1)
        z = z_ref[...]
        gc = gc_ref[...]
        gb = gb_ref[...]
        cz = gc * z
        s1, s2 = _conv_taps(cz, gcp_ref[...] * zp_ref[...], i)
        w = w_ref[...]
        w0, w1, w2 = w[0:1, :], w[1:2, :], w[2:3, :]
        yc = w0 * s2 + w1 * s1 + w2 * cz
        dcyv = d_ref[...]
        dgb_ref[...] = (dcyv * yc).astype(BF)
        dyc = dcyv * gb
        dycn = dn_ref[...] * gbn_ref[...] * (i < nt - 1).astype(F32)
        n1, n2 = dycn[0:1, :], dycn[1:2, :]
        row = lax.broadcasted_iota(jnp.int32, cz.shape, 0)
        a1 = jnp.where(row == tm - 1, n1, pltpu.roll(dyc, tm - 1, 0))
        a2 = jnp.where(row == tm - 1, n2, jnp.where(row == tm - 2, n1, pltpu.roll(dyc, tm - 2, 0)))
        dcz = w2 * dyc + w1 * a1 + w0 * a2
        dz_ref[...] = (dcz * gc).astype(BF)
        dgc_ref[...] = (dcz * z).astype(BF)
        dw0 = jnp.sum(dyc * s2, axis=0, keepdims=True)
        dw1 = jnp.sum(dyc * s1, axis=0, keepdims=True)
        dw2 = jnp.sum(dyc * cz, axis=0, keepdims=True)
        r8 = lax.broadcasted_iota(jnp.int32, (SUBLANES, tc), 0)
        upd = jnp.where(r8 == 0, dw0, jnp.where(r8 == 1, dw1, jnp.where(r8 == 2, dw2, 0.0)))

        @pl.when(i == 0)
        def _():
            dw_ref[...] = jnp.zeros_like(dw_ref)

        dw_ref[...] += upd

    def cur(b0):
        return pl.BlockSpec((tm, tc), lambda j, i: (i, b0 + j))

    def prev(b0):
        return pl.BlockSpec((SUBLANES, tc), lambda j, i: (jnp.maximum(i * rb - 1, 0), b0 + j))

    def nxt(b0):
        return pl.BlockSpec((SUBLANES, tc), lambda j, i: (jnp.minimum((i + 1) * rb, t // SUBLANES - 1), b0 + j))

    blk = pl.BlockSpec((tm, tc), lambda j, i: (i, j))
    w_spec = pl.BlockSpec((SUBLANES, tc), lambda j, i: (0, j))
    out = jax.ShapeDtypeStruct((t, cw), BF)
    return pl.pallas_call(
        body,
        name=name,
        grid=(cw // tc, nt),
        in_specs=[cur(zb), cur(bb), cur(cb), prev(zb), prev(cb), blk, nxt(0), nxt(bb), w_spec],
        out_specs=[blk, blk, blk, w_spec],
        out_shape=[out, out, out, jax.ShapeDtypeStruct((SUBLANES, cw), F32)],
        compiler_params=_cp("parallel", "arbitrary"),
    )(proj, proj, proj, proj, proj, dcy, dcy, proj, w8)


def _rot_half(x):
    lane = lax.broadcasted_iota(jnp.int32, x.shape, 1)
    first = (lane % HEAD_DIM) < (HEAD_DIM // 2)
    return jnp.where(first, pltpu.roll(x, LANES - HEAD_DIM // 2, 1), pltpu.roll(x, HEAD_DIM // 2, 1))


def _rope(x, c, s):
    parts = []
    for a in range(x.shape[1] // LANES):
        xa = x[:, a * LANES:(a + 1) * LANES]
        parts.append(xa * c + _rot_half(xa) * s)
    return parts[0] if len(parts) == 1 else jnp.concatenate(parts, axis=1)


def _rope_bwd(dy, c, s):
    parts = []
    for a in range(dy.shape[1] // LANES):
        da = dy[:, a * LANES:(a + 1) * LANES]
        parts.append(da * c + _rot_half(da * s))
    return parts[0] if len(parts) == 1 else jnp.concatenate(parts, axis=1)


def _window(i):
    b = WINDOW
    r = lax.broadcasted_iota(jnp.int32, (b, b), 0)
    c = lax.broadcasted_iota(jnp.int32, (b, b), 1)
    return c <= r, c <= r + jnp.where(i > 0, b, 0)


def _band_pick(x, tri):
    b = tri.shape[0]
    return jnp.where(tri, x[:, b:], x[:, :b])


def _band_spread(y, tri):
    return jnp.concatenate([jnp.where(tri, 0.0, y), jnp.where(tri, y, 0.0)], axis=1)


def _chunk(x, a):
    return x[:, a * LANES:(a + 1) * LANES]


def _kv_aligned(kp, kc, h):
    band = jnp.concatenate([_chunk(kp, h // 2), _chunk(kc, h // 2)], axis=0).astype(F32)
    swapped = pltpu.roll(band, HEAD_DIM, 1)
    return (band, swapped) if h % 2 == 0 else (swapped, band)


def _swa_fwd(proj, cosf, sinf, sinks, *, nq, name, dep=None):
    t = proj.shape[0]
    nkv = nq // Q_PER_KV
    aw, kw, b = nq * HEAD_DIM, nkv * HEAD_DIM, WINDOW
    nb = t // b
    kblk = aw // kw
    scale = HEAD_DIM ** -0.5

    def body(*refs):
        sink_ref, q_ref, kc_ref, kp_ref, vc_ref, vp_ref, cc_ref, cp_ref, sc_ref, sp_ref = refs[:10]
        o_ref, qr_ref, kr_ref, s_scr, p_scr = refs[-5:]
        i = pl.program_id(0)
        cc, sc, cpv, spv = cc_ref[...], sc_ref[...], cp_ref[...], sp_ref[...]
        qr = _rope(q_ref[...], cc, sc)
        kc = _rope(kc_ref[...], cc, sc)
        kp = _rope(kp_ref[...], cpv, spv)
        qr_ref[...] = qr.astype(BF)
        kr_ref[...] = kc.astype(BF)
        vc, vp = vc_ref[...], vp_ref[...]
        tri, ok = _window(i)
        lo = lax.broadcasted_iota(jnp.int32, (b, LANES), 1) < HEAD_DIM
        ks = [[x.astype(BF) for x in _kv_aligned(kp, kc, h)] for h in range(nkv)]
        vs = [[x.astype(BF) for x in _kv_aligned(vp, vc, h)] for h in range(nkv)]
        for hq in range(nq):
            a, par = hq // 2, hq % 2
            qm = jnp.where(lo if par == 0 else ~lo, _chunk(qr, a), 0.0).astype(BF)
            s_scr[hq] = _band_pick(
                lax.dot_general(qm, ks[hq // Q_PER_KV][par], _DIMS["nt"], preferred_element_type=F32), tri)
        for hq in range(nq):
            s = jnp.where(ok, s_scr[hq] * scale, -jnp.inf)
            sink = sink_ref[hq]
            m = jnp.maximum(jnp.max(s, axis=-1, keepdims=True), sink)
            p = jnp.exp(s - m)
            p = p / (jnp.sum(p, axis=-1, keepdims=True) + jnp.exp(sink - m))
            p_scr[hq] = _band_spread(p, tri).astype(BF)
        for a in range(nq // 2):
            o_par = [jnp.dot(p_scr[2 * a + par], vs[(2 * a) // Q_PER_KV][par], preferred_element_type=F32)
                     for par in range(2)]
            o_ref[:, a * LANES:(a + 1) * LANES] = jnp.where(lo, o_par[0], o_par[1]).astype(BF)

    def prev_i(i):
        return jnp.maximum(i - 1, 0)

    tab_c = pl.BlockSpec((b, LANES), lambda i: (i, 0))
    tab_p = pl.BlockSpec((b, LANES), lambda i: (prev_i(i), 0))
    in_specs = [
        pl.BlockSpec(memory_space=pltpu.SMEM),
        pl.BlockSpec((b, aw), lambda i: (i, 0)),
        pl.BlockSpec((b, kw), lambda i: (i, kblk)),
        pl.BlockSpec((b, kw), lambda i: (prev_i(i), kblk)),
        pl.BlockSpec((b, kw), lambda i: (i, kblk + 1)),
        pl.BlockSpec((b, kw), lambda i: (prev_i(i), kblk + 1)),
        tab_c,
        tab_p,
        tab_c,
        tab_p,
    ]
    args = (sinks, proj, proj, proj, proj, proj, cosf, cosf, sinf, sinf)
    if dep is not None:
        in_specs.append(pl.BlockSpec(memory_space=pl.ANY))
        args += (dep,)
    return pl.pallas_call(
        body,
        name=name,
        grid=(nb,),
        in_specs=in_specs,
        out_specs=[
            pl.BlockSpec((b, aw), lambda i: (i, 0)),
            pl.BlockSpec((b, aw), lambda i: (i, 0)),
            pl.BlockSpec((b, kw), lambda i: (i, 0)),
        ],
        out_shape=[
            jax.ShapeDtypeStruct((t, aw), BF),
            jax.ShapeDtypeStruct((t, aw), BF),
            jax.ShapeDtypeStruct((t, kw), BF),
        ],
        scratch_shapes=[pltpu.VMEM((nq, b, b), F32), pltpu.VMEM((nq, b, 2 * b), BF)],
        compiler_params=_cp("parallel"),
    )(*args)


def _swa_bwd(qr, kr, proj, do, cosf, sinf, sinks, *, nq, name):
    t = proj.shape[0]
    nkv = nq // Q_PER_KV
    aw, kw, b = nq * HEAD_DIM, nkv * HEAD_DIM, WINDOW
    nb = t // b
    kblk = aw // kw
    scale = HEAD_DIM ** -0.5

    def body(sink_ref, q_ref, kc_ref, kp_ref, vc_ref, vp_ref, do_ref, cc_ref, cp_ref, sc_ref, sp_ref,
             dq_ref, dk_ref, dv_ref, ds_ref, ck_ref, cv_ref, sacc_ref, s_scr, dp_scr, ds_scr, pf_scr):
        i = pl.program_id(0)

        @pl.when(i == 0)
        def _():
            ck_ref[...] = jnp.zeros_like(ck_ref)
            cv_ref[...] = jnp.zeros_like(cv_ref)
            sacc_ref[...] = jnp.zeros_like(sacc_ref)

        @pl.when(i < nb)
        def _():
            q = q_ref[...]
            kc, kp = kc_ref[...], kp_ref[...]
            vc, vp = vc_ref[...], vp_ref[...]
            dov = do_ref[...]
            tri, ok = _window(i)
            lane = lax.broadcasted_iota(jnp.int32, (b, LANES), 1)
            lo = lane < HEAD_DIM
            cc, sc = cc_ref[...], sc_ref[...]
            nch = kw // LANES
            row_lo = lax.broadcasted_iota(jnp.int32, (LANES, b), 0) < HEAD_DIM
            dk_ch = [jnp.zeros((LANES, 2 * b), F32) for _ in range(nch)]
            dv_ch = [jnp.zeros((LANES, 2 * b), F32) for _ in range(nch)]
            sacc = jnp.zeros((b, LANES), F32)
            ks = [[x.astype(BF) for x in _kv_aligned(kp, kc, h)] for h in range(nkv)]
            vs = [[x.astype(BF) for x in _kv_aligned(vp, vc, h)] for h in range(nkv)]
            for hq in range(nq):
                a, par, h = hq // 2, hq % 2, hq // Q_PER_KV
                mine = lo if par == 0 else ~lo
                qm = jnp.where(mine, _chunk(q, a).astype(F32), 0.0).astype(BF)
                dom = jnp.where(mine, _chunk(dov, a).astype(F32), 0.0).astype(BF)
                s_scr[hq] = _band_pick(lax.dot_general(qm, ks[h][par], _DIMS["nt"], preferred_element_type=F32), tri)
                dp_scr[hq] = _band_pick(
                    lax.dot_general(dom, vs[h][par], _DIMS["nt"], preferred_element_type=F32), tri)
            for hq in range(nq):
                s = jnp.where(ok, s_scr[hq] * scale, -jnp.inf)
                sink = sink_ref[hq]
                m = jnp.maximum(jnp.max(s, axis=-1, keepdims=True), sink)
                e = jnp.exp(s - m)
                es = jnp.exp(sink - m)
                zinv = 1.0 / (jnp.sum(e, axis=-1, keepdims=True) + es)
                p = e * zinv
                dp = dp_scr[hq]
                delta = jnp.sum(p * dp, axis=-1, keepdims=True)
                ds_scr[hq] = _band_spread(p * (dp - delta) * scale, tri).astype(BF)
                pf_scr[hq] = _band_spread(p, tri).astype(BF)
                sacc = sacc + jnp.where(lane == hq, -(es * zinv) * delta, 0.0)
            for a in range(nq // 2):
                h = (2 * a) // Q_PER_KV
                qa_t = _chunk(q, a).astype(F32).T
                doa_t = _chunk(dov, a).astype(F32).T
                dq_par = []
                for par in range(2):
                    hq = 2 * a + par
                    mine_t = row_lo if par == 0 else ~row_lo
                    qm_t = jnp.where(mine_t, qa_t, 0.0).astype(BF)
                    dom_t = jnp.where(mine_t, doa_t, 0.0).astype(BF)
                    dsv = ds_scr[hq]
                    dq_par.append(jnp.dot(dsv, ks[h][par], preferred_element_type=F32))
                    dkh = jnp.dot(qm_t, dsv, preferred_element_type=F32)
                    dvh = jnp.dot(dom_t, pf_scr[hq], preferred_element_type=F32)
                    if par != h % 2:
                        dkh = pltpu.roll(dkh, HEAD_DIM, 0)
                        dvh = pltpu.roll(dvh, HEAD_DIM, 0)
                    dk_ch[h // 2] = dk_ch[h // 2] + dkh
                    dv_ch[h // 2] = dv_ch[h // 2] + dvh
                dqa = jnp.where(lo, dq_par[0], dq_par[1])
                dq_ref[:, a * LANES:(a + 1) * LANES] = _rope_bwd(dqa, cc, sc).astype(BF)
            dk_ch = [x.T for x in dk_ch]
            dv_ch = [x.T for x in dv_ch]
            dk = dk_ch[0] if nch == 1 else jnp.concatenate(dk_ch, axis=1)
            dv = dv_ch[0] if nch == 1 else jnp.concatenate(dv_ch, axis=1)
            dk_ref[...] = _rope_bwd(ck_ref[...] + dk[:b, :], cp_ref[...], sp_ref[...]).astype(BF)
            dv_ref[...] = (cv_ref[...] + dv[:b, :]).astype(BF)
            ck_ref[...] = dk[b:, :]
            cv_ref[...] = dv[b:, :]
            sacc_ref[...] += sacc

        @pl.when(i == nb)
        def _():
            dk_ref[...] = _rope_bwd(ck_ref[...], cp_ref[...], sp_ref[...]).astype(BF)
            dv_ref[...] = cv_ref[...].astype(BF)
            ds_ref[...] = jnp.broadcast_to(jnp.sum(sacc_ref[...], axis=0, keepdims=True), ds_ref.shape)

    def cur_i(i):
        return jnp.minimum(i, nb - 1)

    def prev_i(i):
        return jnp.clip(i - 1, 0, nb - 1)

    tab_c = pl.BlockSpec((b, LANES), lambda i: (cur_i(i), 0))
    tab_p = pl.BlockSpec((b, LANES), lambda i: (prev_i(i), 0))
    return pl.pallas_call(
        body,
        name=name,
        grid=(nb + 1,),
        in_specs=[
            pl.BlockSpec(memory_space=pltpu.SMEM),
            pl.BlockSpec((b, aw), lambda i: (cur_i(i), 0)),
            pl.BlockSpec((b, kw), lambda i: (cur_i(i), 0)),
            pl.BlockSpec((b, kw), lambda i: (prev_i(i), 0)),
            pl.BlockSpec((b, kw), lambda i: (cur_i(i), kblk + 1)),
            pl.BlockSpec((b, kw), lambda i: (prev_i(i), kblk + 1)),
            pl.BlockSpec((b, aw), lambda i: (cur_i(i), 0)),
            tab_c,
            tab_p,
            tab_c,
            tab_p,
        ],
        out_specs=[
            pl.BlockSpec((b, aw), lambda i: (cur_i(i), 0)),
            pl.BlockSpec((b, kw), lambda i: (prev_i(i), 0)),
            pl.BlockSpec((b, kw), lambda i: (prev_i(i), 0)),
            pl.BlockSpec((SUBLANES, LANES), lambda i: (0, 0)),
        ],
        out_shape=[
            jax.ShapeDtypeStruct((t, aw), BF),
            jax.ShapeDtypeStruct((t, kw), BF),
            jax.ShapeDtypeStruct((t, kw), BF),
            jax.ShapeDtypeStruct((SUBLANES, LANES), F32),
        ],
        scratch_shapes=[pltpu.VMEM((b, kw), F32), pltpu.VMEM((b, kw), F32), pltpu.VMEM((b, LANES), F32),
                        pltpu.VMEM((nq, b, b), F32), pltpu.VMEM((nq, b, b), F32),
                        pltpu.VMEM((nq, b, 2 * b), BF), pltpu.VMEM((nq, b, 2 * b), BF)],
        compiler_params=_cp("arbitrary"),
    )(sinks, qr, kr, kr, proj, proj, do, cosf, cosf, sinf, sinf)


def _xattn_fwd(xq, kv, *, name, tq=512):
    t, xw = xq.shape
    mtok = kv.shape[0]
    tq = min(tq, t)
    nh = xw // X_HEAD_DIM
    scale = X_HEAD_DIM ** -0.5

    def body(q_ref, kv_ref, o_ref):
        q = q_ref[...]
        kvv = kv_ref[...]
        outs = []
        for h in range(nh):
            sl = slice(h * X_HEAD_DIM, (h + 1) * X_HEAD_DIM)
            k = kvv[:, sl]
            v = kvv[:, xw + h * X_HEAD_DIM: xw + (h + 1) * X_HEAD_DIM]
            s = lax.dot_general(q[:, sl], k, _DIMS["nt"], preferred_element_type=F32) * scale
            e = jnp.exp(s - jnp.max(s, axis=-1, keepdims=True))
            p = e / jnp.sum(e, axis=-1, keepdims=True)
            outs.append(jnp.dot(p.astype(BF), v, preferred_element_type=F32))
        o_ref[...] = jnp.concatenate(outs, axis=1).astype(BF)

    return pl.pallas_call(
        body,
        name=name,
        grid=(t // tq,),
        in_specs=[pl.BlockSpec((tq, xw), lambda i: (i, 0)), pl.BlockSpec((mtok, 2 * xw), lambda i: (0, 0))],
        out_specs=pl.BlockSpec((tq, xw), lambda i: (i, 0)),
        out_shape=jax.ShapeDtypeStruct((t, xw), BF),
        compiler_params=_cp("parallel"),
    )(xq, kv)


def _xattn_bwd(xq, kv, do, *, name, tq=512):
    t, xw = xq.shape
    mtok = kv.shape[0]
    tq = min(tq, t)
    nh = xw // X_HEAD_DIM
    scale = X_HEAD_DIM ** -0.5

    def body(q_ref, kv_ref, do_ref, dq_ref, dkv_ref):
        i = pl.program_id(0)
        q = q_ref[...]
        kvv = kv_ref[...]
        dov = do_ref[...]
        dqs, dks, dvs = [], [], []
        for h in range(nh):
            sl = slice(h * X_HEAD_DIM, (h + 1) * X_HEAD_DIM)
            k = kvv[:, sl]
            v = kvv[:, xw + h * X_HEAD_DIM: xw + (h + 1) * X_HEAD_DIM]
            qh, doh = q[:, sl], dov[:, sl]
            s = lax.dot_general(qh, k, _DIMS["nt"], preferred_element_type=F32) * scale
            e = jnp.exp(s - jnp.max(s, axis=-1, keepdims=True))
            p = e / jnp.sum(e, axis=-1, keepdims=True)
            dp = lax.dot_general(doh, v, _DIMS["nt"], preferred_element_type=F32)
            delta = jnp.sum(p * dp, axis=-1, keepdims=True)
            dsv = (p * (dp - delta) * scale).astype(BF)
            dqs.append(jnp.dot(dsv, k, preferred_element_type=F32))
            dks.append(lax.dot_general(dsv, qh, _DIMS["tn"], preferred_element_type=F32))
            dvs.append(lax.dot_general(p.astype(BF), doh, _DIMS["tn"], preferred_element_type=F32))
        dq_ref[...] = jnp.concatenate(dqs, axis=1).astype(BF)

        @pl.when(i == 0)
        def _():
            dkv_ref[...] = jnp.zeros_like(dkv_ref)

        dkv_ref[...] += jnp.concatenate(dks + dvs, axis=1)

    row = pl.BlockSpec((tq, xw), lambda i: (i, 0))
    full = pl.BlockSpec((mtok, 2 * xw), lambda i: (0, 0))
    return pl.pallas_call(
        body,
        name=name,
        grid=(t // tq,),
        in_specs=[row, full, row],
        out_specs=[row, full],
        out_shape=[jax.ShapeDtypeStruct((t, xw), BF), jax.ShapeDtypeStruct((mtok, 2 * xw), F32)],
        compiler_params=_cp("arbitrary"),
    )(xq, kv, do)


def _adam_math(w, g, m, v):
    m = ADAM_B1 * m + (1.0 - ADAM_B1) * g
    v = ADAM_B2 * v + (1.0 - ADAM_B2) * (g * g)
    m_hat = m / (1.0 - ADAM_B1 ** ADAM_STEP)
    v_hat = v / (1.0 - ADAM_B2 ** ADAM_STEP)
    delta = -ADAM_LR * (m_hat / (jnp.sqrt(v_hat) + ADAM_EPS) + ADAM_WD * w)
    return delta, m, v


def _row_tile(r, c, n_arrays, budget=24 * 1024 * 1024):
    step = 2 * SUBLANES
    cap = max(step, budget // (2 * n_arrays * c * 4))
    if r <= cap:
        return r
    best = None
    for tr in range(step, cap + 1, step):
        if r % tr == 0:
            best = tr
    assert best is not None, (r, c)
    return best


def _adamw_sum(parts, own, me, w, m, v, *, name):
    _, r, c = parts.shape
    tr = _row_tile(r, c, 12)

    def body(me_ref, p_ref, own_ref, w_ref, m_ref, v_ref, g_ref, d_ref, nm_ref, nv_ref):
        mine = jnp.full((tr, c), me_ref[0], jnp.int32)
        g = None
        for s in range(N_DEV):
            term = jnp.where(mine == s, own_ref[...], p_ref[s]).astype(F32)
            g = term if g is None else g + term
        g_ref[...] = g
        d_ref[...], nm_ref[...], nv_ref[...] = _adam_math(w_ref[...], g, m_ref[...], v_ref[...])

    blk = pl.BlockSpec((tr, c), lambda i, me_ref: (i, 0))
    out = jax.ShapeDtypeStruct((r, c), F32)
    return pl.pallas_call(
        body,
        name=name,
        grid_spec=pltpu.PrefetchScalarGridSpec(
            num_scalar_prefetch=1,
            grid=(r // tr,),
            in_specs=[
                pl.BlockSpec((N_DEV, tr, c), lambda i, me_ref: (0, i, 0)),
                pl.BlockSpec((None, tr, c), lambda i, me_ref: (me_ref[0], i, 0)),
                blk, blk, blk,
            ],
            out_specs=[blk, blk, blk, blk],
        ),
        out_shape=[out, out, out, out],
        compiler_params=_cp("parallel"),
    )(me, parts, own, w, m, v)


def _adamw_sum_pieces(parts, owns, me, w, m, v, *, name):
    r = w.shape[0]
    widths = [p.shape[2] for p in parts]
    c = sum(widths)
    k = len(parts)
    tr = _row_tile(r, c, 12)

    def body(me_ref, *refs):
        p_refs, own_refs = refs[:k], refs[k:2 * k]
        w_ref, m_ref, v_ref, g_ref, d_ref, nm_ref, nv_ref = refs[2 * k:]
        off = 0
        for p_ref, own_ref, ck in zip(p_refs, own_refs, widths):
            mine = jnp.full((tr, ck), me_ref[0], jnp.int32)
            g = None
            for s in range(N_DEV):
                term = jnp.where(mine == s, own_ref[...], p_ref[s]).astype(F32)
                g = term if g is None else g + term
            sl = slice(off, off + ck)
            g_ref[:, sl] = g
            d_ref[:, sl], nm_ref[:, sl], nv_ref[:, sl] = _adam_math(w_ref[:, sl], g, m_ref[:, sl], v_ref[:, sl])
            off += ck

    blk = pl.BlockSpec((tr, c), lambda i, me_ref: (i, 0))
    out = jax.ShapeDtypeStruct((r, c), F32)
    in_specs = [pl.BlockSpec((N_DEV, tr, ck), lambda i, me_ref: (0, i, 0)) for ck in widths]
    in_specs += [pl.BlockSpec((None, tr, ck), lambda i, me_ref: (me_ref[0], i, 0)) for ck in widths]
    return pl.pallas_call(
        body,
        name=name,
        grid_spec=pltpu.PrefetchScalarGridSpec(
            num_scalar_prefetch=1,
            grid=(r // tr,),
            in_specs=in_specs + [blk, blk, blk],
            out_specs=[blk, blk, blk, blk],
        ),
        out_shape=[out, out, out, out],
        compiler_params=_cp("parallel"),
    )(me, *parts, *owns, w, m, v)


def _to_bf16(a, *, name, dep=None):
    r, c = a.shape
    tr = _row_tile(r, c, 2)

    def body(*refs):
        refs[-1][...] = refs[0][...].astype(BF)

    blk = pl.BlockSpec((tr, c), lambda i: (i, 0))
    in_specs, args = [blk], (a,)
    if dep is not None:
        in_specs.append(pl.BlockSpec(memory_space=pl.ANY))
        args += (dep,)
    return pl.pallas_call(
        body,
        name=name,
        grid=(r // tr,),
        in_specs=in_specs,
        out_specs=blk,
        out_shape=jax.ShapeDtypeStruct((r, c), BF),
        compiler_params=_cp("parallel"),
    )(*args)


def _sum_partials(parts, own, me, *, name):
    _, r, c = parts.shape
    tr = _row_tile(r, c, 6)

    def body(me_ref, p_ref, own_ref, g_ref):
        mine = jnp.full((tr, c), me_ref[0], jnp.int32)
        g = None
        for s in range(N_DEV):
            term = jnp.where(mine == s, own_ref[...], p_ref[s]).astype(F32)
            g = term if g is None else g + term
        g_ref[...] = g

    return pl.pallas_call(
        body,
        name=name,
        grid_spec=pltpu.PrefetchScalarGridSpec(
            num_scalar_prefetch=1,
            grid=(r // tr,),
            in_specs=[
                pl.BlockSpec((N_DEV, tr, c), lambda i, me_ref: (0, i, 0)),
                pl.BlockSpec((None, tr, c), lambda i, me_ref: (me_ref[0], i, 0)),
            ],
            out_specs=pl.BlockSpec((tr, c), lambda i, me_ref: (i, 0)),
        ),
        out_shape=jax.ShapeDtypeStruct((r, c), F32),
        compiler_params=_cp("parallel"),
    )(me, parts, own)


def _adamw_rows(w, g, m, v, *, name):
    r, c = w.shape
    tr = _row_tile(r, c, 7)

    def body(w_ref, g_ref, m_ref, v_ref, d_ref, nm_ref, nv_ref):
        d_ref[...], nm_ref[...], nv_ref[...] = _adam_math(w_ref[...], g_ref[...], m_ref[...], v_ref[...])

    blk = pl.BlockSpec((tr, c), lambda i: (i, 0))
    out = jax.ShapeDtypeStruct((r, c), F32)
    return pl.pallas_call(
        body,
        name=name,
        grid=(r // tr,),
        in_specs=[blk, blk, blk, blk],
        out_specs=[blk, blk, blk],
        out_shape=[out, out, out],
        compiler_params=_cp("parallel"),
    )(w, g, m, v)


def _adamw_small(w, g, m, v, *, name):
    def body(w_ref, g_ref, m_ref, v_ref, d_ref, nm_ref, nv_ref):
        d_ref[...], nm_ref[...], nv_ref[...] = _adam_math(w_ref[...], g_ref[...], m_ref[...], v_ref[...])

    out = jax.ShapeDtypeStruct(w.shape, F32)
    return pl.pallas_call(body, name=name, out_shape=[out, out, out])(w, g, m, v)


def _mesh_pos():
    x, y, c = lax.axis_index("x"), lax.axis_index("y"), lax.axis_index("c")
    return x, y, c


def _peer(x, y, c, mask):
    px = 1 - x if mask & 4 else x
    py = 1 - y if mask & 2 else y
    pc = 1 - c if mask & 1 else c
    return (px, py, pc), 4 * px + 2 * py + pc


_HBM = pl.BlockSpec(memory_space=pltpu.HBM)
_SEM = pl.BlockSpec(memory_space=pltpu.SEMAPHORE)
_EFFECT = pltpu.SideEffectType.DATAFLOW_SIDE_EFFECTING


def _me():
    return 4 * lax.axis_index("x") + 2 * lax.axis_index("y") + lax.axis_index("c")


def _landing(own, me):
    land = lax.empty((N_DEV,) + own.shape, own.dtype)
    return lax.dynamic_update_slice(land, own[None], (me, 0, 0))


_ALL = tuple(range(1, N_DEV))
_CHIPS = (2, 4, 6)
GATHER_DIRECT = tuple((m, None, 0, m) for m in _ALL)
SCATTER_DIRECT = tuple((m, m, 0, m) for m in _ALL)
GATHER_CHIPS = tuple((m, None, 0, m) for m in (1,) + _CHIPS)
GATHER_SIBLING = tuple((1, m, m, m ^ 1) for m in _CHIPS)


def _copy(src, land, send_sem, recv_sem, sem, x, y, c, entry, arriving):
    to, src_m, dst_m, arr_m = entry
    peer, _ = _peer(x, y, c, to)
    blk = lambda m: _peer(x, y, c, m)[1]
    return pltpu.make_async_remote_copy(
        src_ref=src if src_m is None else src.at[blk(src_m)],
        dst_ref=land.at[blk(arr_m if arriving else dst_m)],
        send_sem=send_sem.at[sem], recv_sem=recv_sem.at[sem], device_id=peer, device_id_type=MESH)


def _exchange_start(groups, plan, *, name, after=None):
    flat = [p for g in groups for p in g]
    from_land = flat[0][0] is None
    n, ng, nc = len(flat), len(groups), len(plan)
    n_buf = n if from_land else 2 * n

    def body(*refs):
        lands = refs[:n] if from_land else refs[n:2 * n]
        srcs = lands if from_land else refs[:n]
        sems = refs[n_buf + (after is not None):n_buf + (after is not None) + 2 * ng]
        token = refs[-1]
        x, y, c = _mesh_pos()
        w = 0
        for gi, g in enumerate(groups):
            for wi in range(len(g)):
                for k, entry in enumerate(plan):
                    _copy(srcs[w], lands[w], sems[2 * gi], sems[2 * gi + 1], wi * nc + k, x, y, c, entry,
                          False).start()
                w += 1
        token[...] = jnp.zeros_like(token)

    sem_shapes = []
    for g in groups:
        sem_shapes += [pltpu.SemaphoreType.DMA((len(g) * nc,))] * 2
    args = [] if from_land else [pltpu.with_memory_space_constraint(s, pltpu.HBM) for s, _ in flat]
    args += [pltpu.with_memory_space_constraint(l, pltpu.HBM) for _, l in flat]
    extra = [] if after is None else [after]
    outs = pl.pallas_call(
        body,
        name=name,
        in_specs=[_HBM] * n_buf + [pl.BlockSpec(memory_space=pl.ANY)] * len(extra),
        out_specs=[_SEM] * (2 * ng) + [_HBM] * n_buf + [pl.BlockSpec(memory_space=pltpu.VMEM)],
        out_shape=sem_shapes + [pltpu.HBM(a.shape, a.dtype) for a in args]
        + [jax.ShapeDtypeStruct((SUBLANES, LANES), F32)],
        input_output_aliases={i: 2 * ng + i for i in range(n_buf)},
        compiler_params=pltpu.CompilerParams(has_side_effects=_EFFECT),
    )(*args, *extra)
    sems, thru, token = outs[:2 * ng], outs[2 * ng:2 * ng + n_buf], outs[-1]
    res, w = [], 0
    for gi, g in enumerate(groups):
        m = len(g)
        srcs = [None] * m if from_land else list(thru[w:w + m])
        lands = list(thru[w:w + m]) if from_land else list(thru[n + w:n + w + m])
        res.append((sems[2 * gi], sems[2 * gi + 1], srcs, lands))
        w += m
    return res, token


def _exchange_wait(group, plan, after, *, name):
    send_sems, recv_sems, srcs_in, lands_in = group
    n, nc = len(lands_in), len(plan)
    from_land = srcs_in[0] is None
    n_buf = n if from_land else 2 * n

    def body(*refs):
        lands = refs[:n] if from_land else refs[n:2 * n]
        srcs = lands if from_land else refs[:n]
        send_sem, recv_sem = refs[n_buf], refs[n_buf + 1]
        x, y, c = _mesh_pos()
        for w in range(n):
            for k, entry in enumerate(plan):
                cp = _copy(srcs[w], lands[w], send_sem, recv_sem, w * nc + k, x, y, c, entry, True)
                cp.wait_send()
                cp.wait_recv()

    bufs = lands_in if from_land else srcs_in + lands_in
    outs = pl.pallas_call(
        body,
        name=name,
        in_specs=[_HBM] * n_buf + [_SEM, _SEM, pl.BlockSpec(memory_space=pl.ANY)],
        out_specs=[_HBM] * n_buf,
        out_shape=[pltpu.HBM(a.shape, a.dtype) for a in bufs],
        input_output_aliases={i: i for i in range(n_buf)},
        compiler_params=pltpu.CompilerParams(has_side_effects=_EFFECT),
    )(*bufs, send_sems, recv_sems, after)
    if from_land:
        return [None] * n, list(outs)
    return list(outs[:n]), list(outs[n:])


def _all_reduce_small(parts, rows, width, *, name, dep=None):
    n = len(parts)

    def body(*refs):
        ins = refs[:n]
        o_ref, pack_ref, buf_ref, send_sems, recv_sems = refs[-5:]
        x, y, c_ = _mesh_pos()
        me = 4 * x + 2 * y + c_
        pack_ref[...] = jnp.zeros_like(pack_ref)
        for ref, (arr, r0, nr) in zip(ins, parts):
            pack_ref[r0:r0 + nr, 0:arr.shape[1]] = ref[0:nr, :]
        sends, recvs = [], []
        for k in range(N_DEV - 1):
            peer, pidx = _peer(x, y, c_, k + 1)
            cp = pltpu.make_async_remote_copy(
                src_ref=pack_ref, dst_ref=buf_ref.at[me], send_sem=send_sems.at[k], recv_sem=recv_sems.at[k],
                device_id=peer, device_id_type=MESH)
            cp.start()
            sends.append(cp)
            recvs.append(pltpu.make_async_remote_copy(
                src_ref=pack_ref, dst_ref=buf_ref.at[pidx], send_sem=send_sems.at[k], recv_sem=recv_sems.at[k],
                device_id=peer, device_id_type=MESH))
        buf_ref[me] = pack_ref[...]
        for rc in recvs:
            rc.wait_recv()
        for cp in sends:
            cp.wait_send()
        acc = buf_ref[0]
        for s in range(1, N_DEV):
            acc = acc + buf_ref[s]
        o_ref[...] = acc

    vmem = pl.BlockSpec(memory_space=pltpu.VMEM)
    in_specs = [vmem] * n
    args = [p[0] for p in parts]
    if dep is not None:
        in_specs.append(pl.BlockSpec(memory_space=pl.ANY))
        args.append(dep)
    return pl.pallas_call(
        body,
        name=name,
        in_specs=in_specs,
        out_specs=vmem,
        out_shape=jax.ShapeDtypeStruct((rows, width), F32),
        scratch_shapes=[
            pltpu.VMEM((rows, width), F32),
            pltpu.VMEM((N_DEV, rows, width), F32),
            pltpu.SemaphoreType.DMA((N_DEV - 1,)),
            pltpu.SemaphoreType.DMA((N_DEV - 1,)),
        ],
    )(*args)


def _rope_tables(t):
    half = HEAD_DIM // 2
    inv_freq = ROPE_THETA ** (-jnp.arange(half, dtype=F32) / half)
    ang = jnp.arange(t, dtype=jnp.int32).astype(F32)[:, None] * inv_freq[None, :]
    cos, sin = jnp.cos(ang), jnp.sin(ang)
    cosf = jnp.concatenate([cos, cos, cos, cos], axis=1)
    sinf = jnp.concatenate([-sin, sin, -sin, sin], axis=1)
    return cosf, sinf


def _local_step(x, mem, target, gains, sinks, aw, cw, pre_w, get_w, put_g, dep0=None):
    t, d = x.shape
    nq = aw // HEAD_DIM
    kw = aw // Q_PER_KV
    z0 = aw + 2 * kw
    gb0, gc0 = z0 + cw, z0 + 2 * cw
    ga0 = z0 + 3 * cw
    gcm0 = ga0 + d
    cosf, sinf = _rope_tables(t)

    u1 = _rms_fwd(x, gains["g_mix"], name="rms_mix", dep=(cosf, sinf) if dep0 is None else (dep0, cosf, sinf))
    mem_n = _rms_fwd(mem, gains["g_mem"], name="rms_mem", dep=dep0)
    pre_w("w_in", u1)
    w_in_t = get_w("w_in", u1)
    proj = _mm(u1, w_in_t, mode="nt", tm=1024, tn=512, tk=2048, out_dtype=F32, name="mm_in")
    o_attn, q_rot, k_rot = _swa_fwd(proj, cosf, sinf, sinks, nq=nq, name="swa_fwd", dep=pre_w("conv_w8", proj))
    conv_w8 = get_w("conv_w8", o_attn)
    w_attn_proj, w_conv_proj, w_mix_out = (get_w(n, o_attn) for n in ("w_attn_proj", "w_conv_proj", "w_mix_out"))
    w_xq, w_xkv, w_xo = (get_w(n, o_attn) for n in ("w_xq", "w_xkv", "w_xo"))
    y_attn = _mm(o_attn, w_attn_proj, mode="nn", tm=1024, tn=1024, tk=1024, out_dtype=F32, name="mm_attn_proj")
    cy = _conv_fwd(proj, conv_w8, z0=z0, gb0=gb0, gc0=gc0, cw=cw, name="conv_fwd")
    y_conv, merged = _gate_fwd(cy, w_conv_proj, proj, y_attn, ga0=ga0, gc0=gcm0, name="mm_conv_proj")
    h1, u2 = _mm(merged, w_mix_out, mode="nn", tm=512, tn=d, tk=2048, out_dtype=F32, name="mm_mix_out", residual=x,
                 rms_gain=gains["g_xattn"])
    xq = _mm(u2, w_xq, mode="nn", tm=1024, tn=512, tk=2048, out_dtype=BF, name="mm_xq",
             dep=pre_w("w_ffn_in", h1))
    kv = _mm(mem_n, w_xkv, mode="nn", tm=256, tn=1024, tk=2048, out_dtype=BF, name="mm_xkv")
    o_x = _xattn_fwd(xq, kv, name="xattn_fwd")
    h2, u3 = _mm(o_x, w_xo, mode="nn", tm=512, tn=d, tk=512, out_dtype=F32, name="mm_xo", residual=h1,
                 rms_gain=gains["g_ffn"])
    w_ffn_in = get_w("w_ffn_in", xq)
    hid2, act = _ffn_in_fwd(u3, w_ffn_in, name="mm_ffn_in", dep=pre_w("w_ffn_out", u3))
    w_ffn_out = get_w("w_ffn_out", act)
    h3 = _mm(act, w_ffn_out, mode="nn", tm=512, tn=1024, tk=8192, out_dtype=F32, name="mm_ffn_out", residual=h2)

    tt = 8192
    dh3, dh3b, loss_tile, dg_final = _loss_head(h3, target, gains["g_final"], name="loss_head")
    tok = put_g("w_ffn_out", _mm(act, dh3b, mode="tn", tm=512, tn=1024, tk=tt, out_dtype=BF, name="mm_dw_ffn_out"))
    dhid2 = _ffn_out_bwd(dh3b, w_ffn_out, hid2, name="mm_dact", dep=tok)
    f2 = w_ffn_in.shape[1]
    tok = put_g("w_ffn_in", _mm(u3, dhid2, mode="tn", tm=1024, tn=f2 // N_DEV, tk=tt, out_dtype=BF,
                                name="mm_dw_ffn_in", b_planes=2, stacked=True), stacked=True)
    du3 = _mm(dhid2, w_ffn_in, mode="nt", tm=1024, tn=1024, tk=2816, out_dtype=F32, name="mm_du3", dep=tok,
              a_planes=2)
    dh2, dh2b, dg_ffn = _rms_bwd(du3, h2, gains["g_ffn"], dh3, name="rms_ffn_bwd")
    put_g("w_xo", _mm(o_x, dh2b, mode="tn", tm=512, tn=d // N_DEV, tk=tt, out_dtype=BF, name="mm_dw_xo",
                      stacked=True), stacked=True)
    do_x = _mm(dh2b, w_xo, mode="nt", tm=1024, tn=512, tk=2048, out_dtype=BF, name="mm_do_x")
    dxq, dkv = _xattn_bwd(xq, kv, do_x, name="xattn_bwd")
    put_g("w_xkv", _mm(mem_n, dkv, mode="tn", tm=1024, tn=1024, tk=256, out_dtype=BF, name="mm_dw_xkv"))
    tok = put_g("w_xq", _mm(u2, dxq, mode="tn", tm=1024, tn=512, tk=tt, out_dtype=BF, name="mm_dw_xq"))
    tok_xq = tok
    dmem_n = _mm(dkv, w_xkv, mode="nt", tm=256, tn=1024, tk=1024, out_dtype=F32, name="mm_dmem")
    _, _, dg_mem = _rms_bwd(dmem_n, mem, gains["g_mem"], None, name="rms_mem_bwd")
    dh1, dh1b, dg_xattn = _rms_bwd(dxq, h1, gains["g_xattn"], dh2, name="rms_xattn_bwd", du_w=w_xq, dep=tok_xq)
    put_g("w_mix_out", _mm(merged, dh1b, mode="tn", tm=1024, tn=1024, tk=tt, out_dtype=BF, name="mm_dw_mix_out"))
    dya, dyc, dga, dgc = _gate_bwd(dh1b, w_mix_out, proj, y_attn, y_conv, ga0=ga0, gc0=gcm0, name="mm_dmerged")
    put_g("w_attn_proj", _mm(o_attn, dya, mode="tn", tm=1024, tn=d // N_DEV, tk=tt, out_dtype=BF,
                             name="mm_dw_attn_proj", stacked=True), stacked=True)
    do_attn = _mm(dya, w_attn_proj, mode="nt", tm=1024, tn=1024, tk=2048, out_dtype=BF, name="mm_do_attn")
    tok = put_g("w_conv_proj", _mm(cy, dyc, mode="tn", tm=1024, tn=d // N_DEV, tk=tt, out_dtype=BF,
                                   name="mm_dw_conv_proj", stacked=True), stacked=True)
    dcy = _mm(dyc, w_conv_proj, mode="nt", tm=1024, tn=1024, tk=2048, out_dtype=F32, name="mm_dcy", dep=tok)
    dz, dgb, dgcv, dconv_w8 = _conv_bwd(proj, conv_w8, dcy, z0=z0, gb0=gb0, gc0=gc0, cw=cw, name="conv_bwd")
    dq, dk, dv, dsink_tile = _swa_bwd(q_rot, k_rot, proj, do_attn, cosf, sinf, sinks, nq=nq, name="swa_bwd")
    dproj = jnp.concatenate([dq, dk, dv, dz, dgb, dgcv, dga, dgc], axis=1)
    for hi in range(2):
        tok = put_g("w_in_%d" % hi, _mm(dproj, u1, mode="tn", tm=512, tn=d // 2, tk=tt, out_dtype=BF,
                                        name="mm_dw_in_%d" % hi, b_cols=(hi * (d // 2), d // 2), dep=tok))
    du1 = _mm(dproj, w_in_t, mode="nn", tm=512, tn=1024, tk=4352, out_dtype=F32, name="mm_du1", dep=tok)
    grad_x, _, dg_mix = _rms_bwd(du1, x, gains["g_mix"], dh1, name="rms_mix_bwd")

    small = {
        "g_mix": dg_mix, "g_xattn": dg_xattn, "g_mem": dg_mem, "g_ffn": dg_ffn, "g_final": dg_final,
        "attn_sinks": dsink_tile, "conv_w8": dconv_w8, "loss": loss_tile,
    }
    return grad_x, small


_COL_SHARDED = ("w_in", "w_attn_proj", "w_conv_proj", "w_xo", "w_ffn_in")
_ROW_SHARDED = ("w_mix_out", "w_xq", "w_xkv", "w_ffn_out")
_BIG = _COL_SHARDED + _ROW_SHARDED
_GAINS = ("g_mix", "g_xattn", "g_mem", "g_ffn", "g_final")
_GATHER_GROUPS = (("w_in",), ("conv_w8", "w_attn_proj", "w_conv_proj", "w_mix_out", "w_xq", "w_xkv", "w_xo"),
                  ("w_ffn_in",), ("w_ffn_out",))
_SCATTER_GROUPS = (("w_ffn_out",), ("w_ffn_in",), ("w_xo", "w_xq", "w_xkv"),
                   ("w_mix_out", "w_attn_proj", "w_conv_proj"), ("w_in_0",), ("w_in_1",))
_WEIGHTS = ("g_mix", "w_in", "conv_w", "attn_sinks", "w_attn_proj", "w_conv_proj", "w_mix_out", "g_xattn", "g_mem",
            "w_xq", "w_xkv", "w_xo", "g_ffn", "w_ffn_in", "w_ffn_out", "g_final")


def _unstack(g, col_sharded):
    n, r, c = g.shape
    if col_sharded:
        return jnp.transpose(g, (1, 0, 2)).reshape(r, n * c)
    return g.reshape(n * r, c)


def _stack(w, col_sharded):
    r, c = w.shape
    if col_sharded:
        return jnp.transpose(w.reshape(r, N_DEV, c // N_DEV), (1, 0, 2))
    return w.reshape(N_DEV, r // N_DEV, c)


def kernel(x, mem, g_mix, w_in, conv_w, attn_sinks, w_attn_proj, w_conv_proj, w_mix_out, g_xattn, g_mem, w_xq, w_xkv, w_xo, g_ffn, w_ffn_in, w_ffn_out, g_final, loss_target, m_g_mix, m_w_in, m_conv_w, m_attn_sinks, m_w_attn_proj, m_w_conv_proj, m_w_mix_out, m_g_xattn, m_g_mem, m_w_xq, m_w_xkv, m_w_xo, m_g_ffn, m_w_ffn_in, m_w_ffn_out, m_g_final, v_g_mix, v_w_in, v_conv_w, v_attn_sinks, v_w_attn_proj, v_w_conv_proj, v_w_mix_out, v_g_xattn, v_g_mem, v_w_xq, v_w_xkv, v_w_xo, v_g_ffn, v_w_ffn_in, v_w_ffn_out, v_g_final):
    w_ = dict(g_mix=g_mix, w_in=w_in, conv_w=conv_w, attn_sinks=attn_sinks, w_attn_proj=w_attn_proj,
              w_conv_proj=w_conv_proj, w_mix_out=w_mix_out, g_xattn=g_xattn, g_mem=g_mem, w_xq=w_xq, w_xkv=w_xkv,
              w_xo=w_xo, g_ffn=g_ffn, w_ffn_in=w_ffn_in, w_ffn_out=w_ffn_out, g_final=g_final)
    m_ = dict(g_mix=m_g_mix, w_in=m_w_in, conv_w=m_conv_w, attn_sinks=m_attn_sinks, w_attn_proj=m_w_attn_proj,
              w_conv_proj=m_w_conv_proj, w_mix_out=m_w_mix_out, g_xattn=m_g_xattn, g_mem=m_g_mem, w_xq=m_w_xq,
              w_xkv=m_w_xkv, w_xo=m_w_xo, g_ffn=m_g_ffn, w_ffn_in=m_w_ffn_in, w_ffn_out=m_w_ffn_out,
              g_final=m_g_final)
    v_ = dict(g_mix=v_g_mix, w_in=v_w_in, conv_w=v_conv_w, attn_sinks=v_attn_sinks, w_attn_proj=v_w_attn_proj,
              w_conv_proj=v_w_conv_proj, w_mix_out=v_w_mix_out, g_xattn=v_g_xattn, g_mem=v_g_mem, w_xq=v_w_xq,
              w_xkv=v_w_xkv, w_xo=v_w_xo, g_ffn=v_g_ffn, w_ffn_in=v_w_ffn_in, w_ffn_out=v_w_ffn_out,
              g_final=v_g_final)
    t, d = x.shape[1], x.shape[2]
    nq = attn_sinks.shape[-1]
    cw_shard = conv_w.shape[-1]
    cw = cw_shard * N_DEV

    def two_d(a):
        return a.reshape(a.shape[-2], a.shape[-1]) if a.ndim == 3 else a.reshape(1, a.shape[-1])

    me = _me()
    col = (set(_COL_SHARDED) | {"conv_w8"}) - {"w_in"}

    shards = {"w_in": two_d(w_in).T.astype(BF)}
    first, token = _exchange_start(
        [[(shards[n], _landing(shards[n], me)) for n in g] for g in _GATHER_GROUPS[:1]], GATHER_CHIPS,
        name="gather_start_0")
    for n in _BIG:
        if n != "w_in":
            shards[n] = _to_bf16(two_d(w_[n]), name="cast_" + n, dep=token)
    shards["conv_w8"] = jnp.zeros((SUBLANES, cw_shard), F32).at[:3].set(two_d(conv_w))
    rest, token = _exchange_start(
        [[(shards[n], _landing(shards[n], me)) for n in g] for g in _GATHER_GROUPS[1:]], GATHER_CHIPS,
        name="gather_start_1", after=token)
    gathers = first + rest
    passes, full = {}, {}

    def group_of(name):
        return [name in g for g in _GATHER_GROUPS].index(True)

    def pre_w(name, after):
        gi = group_of(name)
        _, lands = _exchange_wait(gathers[gi], GATHER_CHIPS, after, name="gather_wait_%d" % gi)
        started, tok = _exchange_start([[(None, land) for land in lands]], GATHER_SIBLING,
                                       name="gather_pass_%d" % gi)
        passes[gi] = started[0]
        return tok

    def get_w(name, after):
        if name not in full:
            gi = group_of(name)
            _, lands = _exchange_wait(passes[gi], GATHER_SIBLING, after, name="gather_pass_wait_%d" % gi)
            for n, land in zip(_GATHER_GROUPS[gi], lands):
                full[n] = _unstack(land, n in col)
        return full[name]

    pending, scatters = {}, []

    def put_g(name, dw, stacked=False):
        pending[name] = dw if stacked else _stack(dw, name in col)
        gi = [name in g for g in _SCATTER_GROUPS].index(True)
        group = _SCATTER_GROUPS[gi]
        if not all(n in pending for n in group):
            return None
        pairs = [(pending[n], lax.empty(pending[n].shape, pending[n].dtype)) for n in group]
        started, tok = _exchange_start([pairs], SCATTER_DIRECT, name="scatter_start_%d" % gi)
        scatters.append((gi, started[0]))
        return tok

    gains = {n: two_d(w_[n]) for n in _GAINS}
    grad_x, small = _local_step(
        x[0], mem[0], loss_target[0], gains, attn_sinks.reshape(nq), w_attn_proj.shape[-2], cw, pre_w, get_w, put_g,
        dep0=token)

    grads, deltas, new_m, new_v = {}, {}, {}, {}
    me1 = me.reshape(1).astype(jnp.int32)
    after, halves = grad_x, []
    for gi, started in scatters:
        mine, parts = _exchange_wait(started, SCATTER_DIRECT, after, name="scatter_wait_%d" % gi)
        for n, own, p in zip(_SCATTER_GROUPS[gi], mine, parts):
            if n.startswith("w_in_"):
                halves.append((p, own))
                if len(halves) < 2:
                    continue
                n = "w_in"
                g, dl, nm, nv = (a.T for a in _adamw_sum_pieces(
                    [h[0] for h in halves], [h[1] for h in halves], me1, two_d(w_[n]).T, two_d(m_[n]).T,
                    two_d(v_[n]).T, name="adamw_" + n))
            else:
                g, dl, nm, nv = _adamw_sum(p, own, me1, two_d(w_[n]), two_d(m_[n]), two_d(v_[n]), name="adamw_" + n)
            shape = w_[n].shape
            grads[n], deltas[n], new_m[n], new_v[n] = (a.reshape(shape) for a in (g, dl, nm, nv))
            after = g

    parts = [(small[n], i, 1) for i, n in enumerate(_GAINS)]
    parts += [(small["attn_sinks"], 5, 1), (small["loss"], 6, 1), (small["conv_w8"], 8, 3)]
    red = _all_reduce_small(parts, 2 * SUBLANES, max(d, cw), name="reduce_small", dep=after)
    loss = red[6, 0]
    small_g = {n: red[i:i + 1, :d] for i, n in enumerate(_GAINS)}
    small_g["attn_sinks"] = red[5:6, :nq]
    small_g["conv_w"] = lax.dynamic_slice(red, (8, me * cw_shard), (3, cw_shard))
    for n in _GAINS + ("attn_sinks", "conv_w"):
        shape = w_[n].shape
        g = small_g[n]
        dl, nm, nv = _adamw_small(two_d(w_[n]), g, two_d(m_[n]), two_d(v_[n]), name="adamw_" + n)
        grads[n], deltas[n], new_m[n], new_v[n] = (a.reshape(shape) for a in (g, dl, nm, nv))

    return (loss, grad_x[None], *[grads[n] for n in _WEIGHTS], *[deltas[n] for n in _WEIGHTS],
            *[new_m[n] for n in _WEIGHTS], *[new_v[n] for n in _WEIGHTS])
```

```python
import math

import jax
import jax.numpy as jnp
from jax import lax
from jax.experimental import pallas as pl
from jax.experimental.pallas import tpu as pltpu

HEAD_DIM = 64
Q_PER_KV = 4
WINDOW = 128
X_HEAD_DIM = 128
ROPE_THETA = 10000.0
EPS = 1e-6
ADAM_LR = 0.001
ADAM_B1 = 0.9
ADAM_B2 = 0.999
ADAM_EPS = 1e-08
ADAM_WD = 0.01
ADAM_STEP = 10

N_DEV = 8
LANES = 128
SUBLANES = 8
VMEM_LIMIT_BYTES = 56 * 1024 * 1024
BF = jnp.bfloat16
F32 = jnp.float32
MESH = pl.DeviceIdType.MESH


def _cp(*sem):
    return pltpu.CompilerParams(dimension_semantics=sem, vmem_limit_bytes=VMEM_LIMIT_BYTES)


def _sigmoid(x):
    return 1.0 / (1.0 + jnp.exp(-x))


_DIMS = {
    "nn": (((1,), (0,)), ((), ())),
    "nt": (((1,), (1,)), ((), ())),
    "tn": (((0,), (0,)), ((), ())),
}


def _fit(dim, tile):
    if dim <= tile:
        return dim
    for t in range(tile // LANES * LANES, 0, -LANES):
        if dim % t == 0:
            return t
    return dim


def _mm(a, b, *, mode, tm, tn, tk, out_dtype, name, residual=None, dep=None, a_planes=1, b_planes=1,
        stacked=False, b_cols=None, a_cols=None, rms_gain=None):
    if a_planes > 1:
        assert mode == "nt"
        (_, m, kp), (n, k) = a.shape, b.shape
        assert kp * a_planes == k
    elif b_planes > 1:
        assert mode == "tn"
        (k, m), (_, k2, np_) = a.shape, b.shape
        n = np_ * b_planes
        assert k == k2
    elif mode == "nn":
        (m, k), (k2, n) = a.shape, b.shape
        assert k == k2, (name, a.shape, b.shape)
    elif mode == "nt":
        (m, k), (n, k2) = a.shape, b.shape
        if a_cols is not None:
            k = a_cols[1]
        assert k == k2, (name, a.shape, b.shape)
    else:
        (k, m), (k2, n) = a.shape, b.shape
        assert k == k2, (name, a.shape, b.shape)
    tm, tn, tk = _fit(m, tm), _fit(n // b_planes, tn), _fit(k // a_planes, tk)
    assert m % tm == 0 and (n // b_planes) % tn == 0 and (k // a_planes) % tk == 0, (name, m, n, k, tm, tn, tk)
    ka0 = 0
    if a_cols is not None:
        assert mode == "nt" and a_planes == 1 and a_cols[0] % tk == 0
        ka0 = a_cols[0] // tk
    j0 = 0
    if b_cols is not None:
        assert mode == "tn" and b_planes == 1 and b_cols[0] % tn == 0 and b_cols[1] % tn == 0
        j0, n = b_cols[0] // tn, b_cols[1]
    nk = k // tk
    nkp, njp = nk // a_planes, n // tn // b_planes
    if a_planes > 1:
        a_spec = pl.BlockSpec((None, tm, tk), lambda i, j, kk: (kk // nkp, i, kk % nkp))
    elif mode == "tn":
        a_spec = pl.BlockSpec((tk, tm), lambda i, j, kk: (kk, i))
    else:
        a_spec = pl.BlockSpec((tm, tk), lambda i, j, kk: (i, kk + ka0))
    if b_planes > 1:
        b_spec = pl.BlockSpec((None, tk, tn), lambda i, j, kk: (j // njp, kk, j % njp))
    elif mode == "nt":
        b_spec = pl.BlockSpec((tn, tk), lambda i, j, kk: (j, kk))
    else:
        b_spec = pl.BlockSpec((tk, tn), lambda i, j, kk: (kk, j + j0))
    if stacked:
        assert residual is None
        o_spec = pl.BlockSpec((None, tm, tn), lambda i, j, kk: (j, i, 0))
        out_shape = jax.ShapeDtypeStruct((n // tn, m, tn), out_dtype)
    else:
        o_spec = pl.BlockSpec((tm, tn), lambda i, j, kk: (i, j))
        out_shape = jax.ShapeDtypeStruct((m, n), out_dtype)
    dims = _DIMS[mode]
    has_res = residual is not None
    has_rms = rms_gain is not None
    assert not has_rms or (tn == n and not stacked)
    n_in = 2 + has_res + has_rms + (dep is not None)

    def body(*refs):
        a_ref, b_ref, r_ref, o_ref = refs[0], refs[1], refs[2], refs[n_in]
        part = lax.dot_general(a_ref[...].astype(BF), b_ref[...].astype(BF), dims, preferred_element_type=F32)

        def finish(acc):
            if has_res:
                acc = r_ref[...] + acc
            o_ref[...] = acc.astype(out_dtype)
            if has_rms:
                r = lax.rsqrt(jnp.mean(acc * acc, axis=-1, keepdims=True) + EPS)
                refs[n_in + 1][...] = ((acc * r) * refs[2 + has_res][...]).astype(BF)

        if nk == 1:
            finish(part)
        else:
            acc_ref = refs[-1]
            kk = pl.program_id(2)

            @pl.when(kk == 0)
            def _():
                acc_ref[...] = part

            @pl.when(kk > 0)
            def _():
                acc_ref[...] += part

            @pl.when(kk == nk - 1)
            def _():
                finish(acc_ref[...])

    in_specs = [a_spec, b_spec] + ([o_spec] if has_res else [])
    args = (a, b) + ((residual,) if has_res else ())
    out_specs = o_spec
    if has_rms:
        in_specs.append(pl.BlockSpec((1, n), lambda i, j, kk: (0, 0)))
        args += (rms_gain,)
        out_specs = [o_spec, o_spec]
        out_shape = [out_shape, jax.ShapeDtypeStruct((m, n), BF)]
    if dep is not None:
        in_specs.append(pl.BlockSpec(memory_space=pl.ANY))
        args += (dep,)
    return pl.pallas_call(
        body,
        name=name,
        grid=(m // tm, n // tn, nk),
        in_specs=in_specs,
        out_specs=out_specs,
        out_shape=out_shape,
        scratch_shapes=[pltpu.VMEM((tm, tn), F32)] if nk > 1 else [],
        compiler_params=_cp("parallel", "parallel", "arbitrary"),
    )(*args)


def _rms_fwd(h, g, *, name, tm=512, dep=None):
    t, d = h.shape
    tm = min(tm, t)

    def body(*refs):
        h_ref, g_ref, u_ref = refs[0], refs[1], refs[-1]
        hv = h_ref[...]
        r = lax.rsqrt(jnp.mean(hv * hv, axis=-1, keepdims=True) + EPS)
        u_ref[...] = ((hv * r) * g_ref[...]).astype(BF)

    in_specs = [pl.BlockSpec((tm, d), lambda i: (i, 0)), pl.BlockSpec((1, d), lambda i: (0, 0))]
    args = (h, g)
    for one in () if dep is None else (dep if isinstance(dep, tuple) else (dep,)):
        in_specs.append(pl.BlockSpec(memory_space=pl.ANY))
        args += (one,)
    return pl.pallas_call(
        body,
        name=name,
        grid=(t // tm,),
        in_specs=in_specs,
        out_specs=pl.BlockSpec((tm, d), lambda i: (i, 0)),
        out_shape=jax.ShapeDtypeStruct((t, d), BF),
        compiler_params=_cp("parallel"),
    )(*args)


def _rms_bwd(du, h, g, dres, *, name, tm=512, du_w=None, dep=None):
    t, d = h.shape
    tm = min(tm, t)
    want_dh = dres is not None

    def body(*refs):
        du_ref, h_ref, g_ref, dg_ref = refs[0], refs[1], refs[2], refs[-1]
        if want_dh:
            dres_ref, dh_ref, dhb_ref = refs[3], refs[-3], refs[-2]
        i = pl.program_id(0)
        hv = h_ref[...]
        duv = du_ref[...]
        if du_w is not None:
            duv = lax.dot_general(duv, refs[3 + want_dh][...], _DIMS["nt"], preferred_element_type=F32)
        r = lax.rsqrt(jnp.mean(hv * hv, axis=-1, keepdims=True) + EPS)
        nv = hv * r
        if want_dh:
            gy = duv * g_ref[...]
            dh = dres_ref[...] + r * (gy - nv * jnp.mean(nv * gy, axis=-1, keepdims=True))
            dh_ref[...] = dh
            dhb_ref[...] = dh.astype(BF)

        @pl.when(i == 0)
        def _():
            dg_ref[...] = jnp.zeros_like(dg_ref)

        dg_ref[...] += jnp.sum(duv * nv, axis=0, keepdims=True)

    row = pl.BlockSpec((tm, d), lambda i: (i, 0))
    vec = pl.BlockSpec((1, d), lambda i: (0, 0))
    du_spec = row if du_w is None else pl.BlockSpec((tm, du.shape[1]), lambda i: (i, 0))
    if want_dh:
        in_specs, args = [du_spec, row, vec, row], (du, h, g, dres)
        out_specs = [row, row, vec]
        out_shape = [jax.ShapeDtypeStruct((t, d), F32), jax.ShapeDtypeStruct((t, d), BF),
                     jax.ShapeDtypeStruct((1, d), F32)]
    else:
        in_specs, args = [du_spec, row, vec], (du, h, g)
        out_specs = [vec]
        out_shape = [jax.ShapeDtypeStruct((1, d), F32)]
    if du_w is not None:
        in_specs.append(pl.BlockSpec(du_w.shape, lambda i: (0, 0)))
        args += (du_w,)
    if dep is not None:
        in_specs.append(pl.BlockSpec(memory_space=pl.ANY))
        args += (dep,)
    outs = pl.pallas_call(
        body,
        name=name,
        grid=(t // tm,),
        in_specs=in_specs,
        out_specs=out_specs,
        out_shape=out_shape,
        compiler_params=_cp("arbitrary"),
    )(*args)
    return (outs[0], outs[1], outs[2]) if want_dh else (None, None, outs[0])


def _loss_head(h, target, g, *, name, tm=512):
    t, d = h.shape
    tm = min(tm, t)

    def body(h_ref, t_ref, g_ref, dh_ref, dhb_ref, loss_ref, dg_ref):
        i = pl.program_id(0)
        hv = h_ref[...]
        gv = g_ref[...]
        r = lax.rsqrt(jnp.mean(hv * hv, axis=-1, keepdims=True) + EPS)
        nv = hv * r
        e = nv * gv - t_ref[...]
        per_tok = jnp.mean(e * e, axis=-1, keepdims=True)
        lp = 0.5 * jnp.sum(per_tok, axis=0, keepdims=True)
        dy = e * (1.0 / d)
        gy = dy * gv
        dh = r * (gy - nv * jnp.mean(nv * gy, axis=-1, keepdims=True))
        dh_ref[...] = dh
        dhb_ref[...] = dh.astype(BF)

        @pl.when(i == 0)
        def _():
            loss_ref[...] = jnp.zeros_like(loss_ref)
            dg_ref[...] = jnp.zeros_like(dg_ref)

        loss_ref[...] += jnp.broadcast_to(lp, loss_ref.shape)
        dg_ref[...] += jnp.sum(dy * nv, axis=0, keepdims=True)

    row = pl.BlockSpec((tm, d), lambda i: (i, 0))
    vec = pl.BlockSpec((1, d), lambda i: (0, 0))
    return pl.pallas_call(
        body,
        name=name,
        grid=(t // tm,),
        in_specs=[row, row, vec],
        out_specs=[row, row, pl.BlockSpec((SUBLANES, LANES), lambda i: (0, 0)), vec],
        out_shape=[
            jax.ShapeDtypeStruct((t, d), F32),
            jax.ShapeDtypeStruct((t, d), BF),
            jax.ShapeDtypeStruct((SUBLANES, LANES), F32),
            jax.ShapeDtypeStruct((1, d), F32),
        ],
        compiler_params=_cp("arbitrary"),
    )(h, target, g)


def _ffn_in_fwd(u, w, *, name, tm=1024, tn=512, dep=None):
    t, d = u.shape
    f = w.shape[1] // 2
    tm, tn = _fit(t, tm), _fit(f, tn)
    nf = f // tn

    def body(*refs):
        u_ref, wa_ref, wb_ref, hid_ref, act_ref = refs[0], refs[1], refs[2], refs[-2], refs[-1]
        uv = u_ref[...]
        a = jnp.dot(uv, wa_ref[...], preferred_element_type=F32)
        b = jnp.dot(uv, wb_ref[...], preferred_element_type=F32)
        hid_ref[0] = a.astype(BF)
        hid_ref[1] = b.astype(BF)
        act_ref[...] = ((a * _sigmoid(a)) * b).astype(BF)

    in_specs = [
        pl.BlockSpec((tm, d), lambda i, j: (i, 0)),
        pl.BlockSpec((d, tn), lambda i, j: (0, j)),
        pl.BlockSpec((d, tn), lambda i, j: (0, nf + j)),
    ]
    args = (u, w, w)
    if dep is not None:
        in_specs.append(pl.BlockSpec(memory_space=pl.ANY))
        args += (dep,)
    return pl.pallas_call(
        body,
        name=name,
        grid=(t // tm, nf),
        in_specs=in_specs,
        out_specs=[pl.BlockSpec((2, tm, tn), lambda i, j: (0, i, j)), pl.BlockSpec((tm, tn), lambda i, j: (i, j))],
        out_shape=[jax.ShapeDtypeStruct((2, t, f), BF), jax.ShapeDtypeStruct((t, f), BF)],
        compiler_params=_cp("parallel", "parallel"),
    )(*args)


def _ffn_out_bwd(dh, w_out, hid2, *, name, tm=1024, tn=512, dep=None):
    t, d = dh.shape
    f = w_out.shape[0]
    tm, tn = _fit(t, tm), _fit(f, tn)

    def body(*refs):
        dh_ref, w_ref, hid_ref, o_ref = refs[0], refs[1], refs[2], refs[-1]
        dact = lax.dot_general(dh_ref[...], w_ref[...], _DIMS["nt"], preferred_element_type=F32)
        a = hid_ref[0].astype(F32)
        b = hid_ref[1].astype(F32)
        sg = _sigmoid(a)
        o_ref[0] = (dact * b * (sg * (1.0 + a * (1.0 - sg)))).astype(BF)
        o_ref[1] = (dact * (a * sg)).astype(BF)

    pair = pl.BlockSpec((2, tm, tn), lambda i, j: (0, i, j))
    in_specs = [pl.BlockSpec((tm, d), lambda i, j: (i, 0)), pl.BlockSpec((tn, d), lambda i, j: (j, 0)), pair]
    args = (dh, w_out, hid2)
    if dep is not None:
        in_specs.append(pl.BlockSpec(memory_space=pl.ANY))
        args += (dep,)
    return pl.pallas_call(
        body,
        name=name,
        grid=(t // tm, f // tn),
        in_specs=in_specs,
        out_specs=pair,
        out_shape=jax.ShapeDtypeStruct((2, t, f), BF),
        compiler_params=_cp("parallel", "parallel"),
    )(*args)


def _gate_fwd(cy, w, proj, ya, *, ga0, gc0, name, tm=1024, tc=512):
    t, d = ya.shape
    kc = cy.shape[1]
    tm, tc = _fit(t, tm), math.gcd(tc, d, ga0, gc0)
    a0, c0 = ga0 // tc, gc0 // tc

    def body(cy_ref, w_ref, ga_ref, gc_ref, ya_ref, yc_ref, o_ref):
        yc = jnp.dot(cy_ref[...], w_ref[...], preferred_element_type=F32)
        yc_ref[...] = yc
        o_ref[...] = (_sigmoid(ga_ref[...]) * ya_ref[...] + _sigmoid(gc_ref[...]) * yc).astype(BF)

    blk = pl.BlockSpec((tm, tc), lambda i, j: (i, j))
    return pl.pallas_call(
        body,
        name=name,
        grid=(t // tm, d // tc),
        in_specs=[
            pl.BlockSpec((tm, kc), lambda i, j: (i, 0)),
            pl.BlockSpec((kc, tc), lambda i, j: (0, j)),
            pl.BlockSpec((tm, tc), lambda i, j: (i, a0 + j)),
            pl.BlockSpec((tm, tc), lambda i, j: (i, c0 + j)),
            blk,
        ],
        out_specs=[blk, blk],
        out_shape=[jax.ShapeDtypeStruct((t, d), F32), jax.ShapeDtypeStruct((t, d), BF)],
        compiler_params=_cp("parallel", "parallel"),
    )(cy, w, proj, proj, ya)


def _gate_bwd(dh, w, proj, ya, yc, *, ga0, gc0, name, tm=1024, tc=512):
    t, d = ya.shape
    tm, tc = _fit(t, tm), math.gcd(tc, d, ga0, gc0)
    a0, c0 = ga0 // tc, gc0 // tc

    def body(dh_ref, w_ref, ga_ref, gc_ref, ya_ref, yc_ref, dya_ref, dyc_ref, dga_ref, dgc_ref):
        dmv = lax.dot_general(dh_ref[...], w_ref[...], _DIMS["nt"], preferred_element_type=F32)
        sa = _sigmoid(ga_ref[...])
        sc = _sigmoid(gc_ref[...])
        dya_ref[...] = (dmv * sa).astype(BF)
        dyc_ref[...] = (dmv * sc).astype(BF)
        dga_ref[...] = (dmv * ya_ref[...] * (sa * (1.0 - sa))).astype(BF)
        dgc_ref[...] = (dmv * yc_ref[...] * (sc * (1.0 - sc))).astype(BF)

    blk = pl.BlockSpec((tm, tc), lambda i, j: (i, j))
    out = jax.ShapeDtypeStruct((t, d), BF)
    return pl.pallas_call(
        body,
        name=name,
        grid=(t // tm, d // tc),
        in_specs=[
            pl.BlockSpec((tm, d), lambda i, j: (i, 0)),
            pl.BlockSpec((tc, d), lambda i, j: (j, 0)),
            pl.BlockSpec((tm, tc), lambda i, j: (i, a0 + j)),
            pl.BlockSpec((tm, tc), lambda i, j: (i, c0 + j)),
            blk,
            blk,
        ],
        out_specs=[blk, blk, blk, blk],
        out_shape=[out, out, out, out],
        compiler_params=_cp("parallel", "parallel"),
    )(dh, w, proj, proj, ya, yc)


def _conv_taps(cz, czp, i):
    czp = czp * (i > 0).astype(F32)
    h1 = czp[SUBLANES - 1:SUBLANES, :]
    h2 = czp[SUBLANES - 2:SUBLANES - 1, :]
    row = lax.broadcasted_iota(jnp.int32, cz.shape, 0)
    s1 = jnp.where(row == 0, h1, pltpu.roll(cz, 1, 0))
    s2 = jnp.where(row == 0, h2, jnp.where(row == 1, h1, pltpu.roll(cz, 2, 0)))
    return s1, s2


def _conv_fwd(proj, w8, *, z0, gb0, gc0, cw, name, tm=512, tc=512):
    t = proj.shape[0]
    tm, tc = min(tm, t), math.gcd(tc, cw, z0, gb0, gc0)
    zb, bb, cb = z0 // tc, gb0 // tc, gc0 // tc
    rb = tm // SUBLANES

    def body(z_ref, gb_ref, gc_ref, zp_ref, gcp_ref, w_ref, o_ref):
        i = pl.program_id(0)
        cz = gc_ref[...] * z_ref[...]
        s1, s2 = _conv_taps(cz, gcp_ref[...] * zp_ref[...], i)
        w = w_ref[...]
        y = w[0:1, :] * s2 + w[1:2, :] * s1 + w[2:3, :] * cz
        o_ref[...] = (gb_ref[...] * y).astype(BF)

    def cur(b0):
        return pl.BlockSpec((tm, tc), lambda i, j: (i, b0 + j))

    def prev(b0):
        return pl.BlockSpec((SUBLANES, tc), lambda i, j: (jnp.maximum(i * rb - 1, 0), b0 + j))

    return pl.pallas_call(
        body,
        name=name,
        grid=(t // tm, cw // tc),
        in_specs=[cur(zb), cur(bb), cur(cb), prev(zb), prev(cb), pl.BlockSpec((SUBLANES, tc), lambda i, j: (0, j))],
        out_specs=pl.BlockSpec((tm, tc), lambda i, j: (i, j)),
        out_shape=jax.ShapeDtypeStruct((t, cw), BF),
        compiler_params=_cp("parallel", "parallel"),
    )(proj, proj, proj, proj, proj, w8)


def _conv_bwd(proj, w8, dcy, *, z0, gb0, gc0, cw, name, tm=512, tc=512):
    t = proj.shape[0]
    tm, tc = min(tm, t), math.gcd(tc, cw, z0, gb0, gc0)
    zb, bb, cb = z0 // tc, gb0 // tc, gc0 // tc
    rb = tm // SUBLANES
    nt = t // tm

    def body(z_ref, gb_ref, gc_ref, zp_ref, gcp_ref, d_ref, dn_ref, gbn_ref, w_ref, dz_ref, dgb_ref, dgc_ref, dw_ref):
        i = pl.program_id(1)
        z = z_ref[...]
        gc = gc_ref[...]
        gb = gb_ref[...]
        cz = gc * z
        s1, s2 = _conv_taps(cz, gcp_ref[...] * zp_ref[...], i)
        w = w_ref[...]
        w0, w1, w2 = w[0:1, :], w[1:2, :], w[2:3, :]
        yc = w0 * s2 + w1 * s1 + w2 * cz
        dcyv = d_ref[...]
        dgb_ref[...] = (dcyv * yc).astype(BF)
        dyc = dcyv * gb
        dycn = dn_ref[...] * gbn_ref[...] * (i < nt - 1).astype(F32)
        n1, n2 = dycn[0:1, :], dycn[1:2, :]
        row = lax.broadcasted_iota(jnp.int32, cz.shape, 0)
        a1 = jnp.where(row == tm - 1, n1, pltpu.roll(dyc, tm - 1, 0))
        a2 = jnp.where(row == tm - 1, n2, jnp.where(row == tm - 2, n1, pltpu.roll(dyc, tm - 2, 0)))
        dcz = w2 * dyc + w1 * a1 + w0 * a2
        dz_ref[...] = (dcz * gc).astype(BF)
        dgc_ref[...] = (dcz * z).astype(BF)
        dw0 = jnp.sum(dyc * s2, axis=0, keepdims=True)
        dw1 = jnp.sum(dyc * s1, axis=0, keepdims=True)
        dw2 = jnp.sum(dyc * cz, axis=0, keepdims=True)
        r8 = lax.broadcasted_iota(jnp.int32, (SUBLANES, tc), 0)
        upd = jnp.where(r8 == 0, dw0, jnp.where(r8 == 1, dw1, jnp.where(r8 == 2, dw2, 0.0)))

        @pl.when(i == 0)
        def _():
            dw_ref[...] = jnp.zeros_like(dw_ref)

        dw_ref[...] += upd

    def cur(b0):
        return pl.BlockSpec((tm, tc), lambda j, i: (i, b0 + j))

    def prev(b0):
        return pl.BlockSpec((SUBLANES, tc), lambda j, i: (jnp.maximum(i * rb - 1, 0), b0 + j))

    def nxt(b0):
        return pl.BlockSpec((SUBLANES, tc), lambda j, i: (jnp.minimum((i + 1) * rb, t // SUBLANES - 1), b0 + j))

    blk = pl.BlockSpec((tm, tc), lambda j, i: (i, j))
    w_spec = pl.BlockSpec((SUBLANES, tc), lambda j, i: (0, j))
    out = jax.ShapeDtypeStruct((t, cw), BF)
    return pl.pallas_call(
        body,
        name=name,
        grid=(cw // tc, nt),
        in_specs=[cur(zb), cur(bb), cur(cb), prev(zb), prev(cb), blk, nxt(0), nxt(bb), w_spec],
        out_specs=[blk, blk, blk, w_spec],
        out_shape=[out, out, out, jax.ShapeDtypeStruct((SUBLANES, cw), F32)],
        compiler_params=_cp("parallel", "arbitrary"),
    )(proj, proj, proj, proj, proj, dcy, dcy, proj, w8)


def _rot_half(x):
    lane = lax.broadcasted_iota(jnp.int32, x.shape, 1)
    first = (lane % HEAD_DIM) < (HEAD_DIM // 2)
    return jnp.where(first, pltpu.roll(x, LANES - HEAD_DIM // 2, 1), pltpu.roll(x, HEAD_DIM // 2, 1))


def _rope(x, c, s):
    parts = []
    for a in range(x.shape[1] // LANES):
        xa = x[:, a * LANES:(a + 1) * LANES]
        parts.append(xa * c + _rot_half(xa) * s)
    return parts[0] if len(parts) == 1 else jnp.concatenate(parts, axis=1)


def _rope_bwd(dy, c, s):
    parts = []
    for a in range(dy.shape[1] // LANES):
        da = dy[:, a * LANES:(a + 1) * LANES]
        parts.append(da * c + _rot_half(da * s))
    return parts[0] if len(parts) == 1 else jnp.concatenate(parts, axis=1)


def _window(i):
    b = WINDOW
    r = lax.broadcasted_iota(jnp.int32, (b, b), 0)
    c = lax.broadcasted_iota(jnp.int32, (b, b), 1)
    return c <= r, c <= r + jnp.where(i > 0, b, 0)


def _band_pick(x, tri):
    b = tri.shape[0]
    return jnp.where(tri, x[:, b:], x[:, :b])


def _band_spread(y, tri):
    return jnp.concatenate([jnp.where(tri, 0.0, y), jnp.where(tri, y, 0.0)], axis=1)


def _chunk(x, a):
    return x[:, a * LANES:(a + 1) * LANES]


def _kv_aligned(kp, kc, h):
    band = jnp.concatenate([_chunk(kp, h // 2), _chunk(kc, h // 2)], axis=0).astype(F32)
    swapped = pltpu.roll(band, HEAD_DIM, 1)
    return (band, swapped) if h % 2 == 0 else (swapped, band)


def _swa_fwd(proj, cosf, sinf, sinks, *, nq, name, dep=None):
    t = proj.shape[0]
    nkv = nq // Q_PER_KV
    aw, kw, b = nq * HEAD_DIM, nkv * HEAD_DIM, WINDOW
    nb = t // b
    kblk = aw // kw
    scale = HEAD_DIM ** -0.5

    def body(*refs):
        sink_ref, q_ref, kc_ref, kp_ref, vc_ref, vp_ref, cc_ref, cp_ref, sc_ref, sp_ref = refs[:10]
        o_ref, qr_ref, kr_ref, s_scr, p_scr = refs[-5:]
        i = pl.program_id(0)
        cc, sc, cpv, spv = cc_ref[...], sc_ref[...], cp_ref[...], sp_ref[...]
        qr = _rope(q_ref[...], cc, sc)
        kc = _rope(kc_ref[...], cc, sc)
        kp = _rope(kp_ref[...], cpv, spv)
        qr_ref[...] = qr.astype(BF)
        kr_ref[...] = kc.astype(BF)
        vc, vp = vc_ref[...], vp_ref[...]
        tri, ok = _window(i)
        lo = lax.broadcasted_iota(jnp.int32, (b, LANES), 1) < HEAD_DIM
        ks = [[x.astype(BF) for x in _kv_aligned(kp, kc, h)] for h in range(nkv)]
        vs = [[x.astype(BF) for x in _kv_aligned(vp, vc, h)] for h in range(nkv)]
        for hq in range(nq):
            a, par = hq // 2, hq % 2
            qm = jnp.where(lo if par == 0 else ~lo, _chunk(qr, a), 0.0).astype(BF)
            s_scr[hq] = _band_pick(
                lax.dot_general(qm, ks[hq // Q_PER_KV][par], _DIMS["nt"], preferred_element_type=F32), tri)
        for hq in range(nq):
            s = jnp.where(ok, s_scr[hq] * scale, -jnp.inf)
            sink = sink_ref[hq]
            m = jnp.maximum(jnp.max(s, axis=-1, keepdims=True), sink)
            p = jnp.exp(s - m)
            p = p / (jnp.sum(p, axis=-1, keepdims=True) + jnp.exp(sink - m))
            p_scr[hq] = _band_spread(p, tri).astype(BF)
        for a in range(nq // 2):
            o_par = [jnp.dot(p_scr[2 * a + par], vs[(2 * a) // Q_PER_KV][par], preferred_element_type=F32)
                     for par in range(2)]
            o_ref[:, a * LANES:(a + 1) * LANES] = jnp.where(lo, o_par[0], o_par[1]).astype(BF)

    def prev_i(i):
        return jnp.maximum(i - 1, 0)

    tab_c = pl.BlockSpec((b, LANES), lambda i: (i, 0))
    tab_p = pl.BlockSpec((b, LANES), lambda i: (prev_i(i), 0))
    in_specs = [
        pl.BlockSpec(memory_space=pltpu.SMEM),
        pl.BlockSpec((b, aw), lambda i: (i, 0)),
        pl.BlockSpec((b, kw), lambda i: (i, kblk)),
        pl.BlockSpec((b, kw), lambda i: (prev_i(i), kblk)),
        pl.BlockSpec((b, kw), lambda i: (i, kblk + 1)),
        pl.BlockSpec((b, kw), lambda i: (prev_i(i), kblk + 1)),
        tab_c,
        tab_p,
        tab_c,
        tab_p,
    ]
    args = (sinks, proj, proj, proj, proj, proj, cosf, cosf, sinf, sinf)
    if dep is not None:
        in_specs.append(pl.BlockSpec(memory_space=pl.ANY))
        args += (dep,)
    return pl.pallas_call(
        body,
        name=name,
        grid=(nb,),
        in_specs=in_specs,
        out_specs=[
            pl.BlockSpec((b, aw), lambda i: (i, 0)),
            pl.BlockSpec((b, aw), lambda i: (i, 0)),
            pl.BlockSpec((b, kw), lambda i: (i, 0)),
        ],
        out_shape=[
            jax.ShapeDtypeStruct((t, aw), BF),
            jax.ShapeDtypeStruct((t, aw), BF),
            jax.ShapeDtypeStruct((t, kw), BF),
        ],
        scratch_shapes=[pltpu.VMEM((nq, b, b), F32), pltpu.VMEM((nq, b, 2 * b), BF)],
        compiler_params=_cp("parallel"),
    )(*args)


def _swa_bwd(qr, kr, proj, do, cosf, sinf, sinks, *, nq, name):
    t = proj.shape[0]
    nkv = nq // Q_PER_KV
    aw, kw, b = nq * HEAD_DIM, nkv * HEAD_DIM, WINDOW
    nb = t // b
    kblk = aw // kw
    scale = HEAD_DIM ** -0.5

    def body(sink_ref, q_ref, kc_ref, kp_ref, vc_ref, vp_ref, do_ref, cc_ref, cp_ref, sc_ref, sp_ref,
             dq_ref, dk_ref, dv_ref, ds_ref, ck_ref, cv_ref, sacc_ref, s_scr, dp_scr, ds_scr, pf_scr):
        i = pl.program_id(0)

        @pl.when(i == 0)
        def _():
            ck_ref[...] = jnp.zeros_like(ck_ref)
            cv_ref[...] = jnp.zeros_like(cv_ref)
            sacc_ref[...] = jnp.zeros_like(sacc_ref)

        @pl.when(i < nb)
        def _():
            q = q_ref[...]
            kc, kp = kc_ref[...], kp_ref[...]
            vc, vp = vc_ref[...], vp_ref[...]
            dov = do_ref[...]
            tri, ok = _window(i)
            lane = lax.broadcasted_iota(jnp.int32, (b, LANES), 1)
            lo = lane < HEAD_DIM
            cc, sc = cc_ref[...], sc_ref[...]
            nch = kw // LANES
            row_lo = lax.broadcasted_iota(jnp.int32, (LANES, b), 0) < HEAD_DIM
            dk_ch = [jnp.zeros((LANES, 2 * b), F32) for _ in range(nch)]
            dv_ch = [jnp.zeros((LANES, 2 * b), F32) for _ in range(nch)]
            sacc = jnp.zeros((b, LANES), F32)
            ks = [[x.astype(BF) for x in _kv_aligned(kp, kc, h)] for h in range(nkv)]
            vs = [[x.astype(BF) for x in _kv_aligned(vp, vc, h)] for h in range(nkv)]
            for hq in range(nq):
                a, par, h = hq // 2, hq % 2, hq // Q_PER_KV
                mine = lo if par == 0 else ~lo
                qm = jnp.where(mine, _chunk(q, a).astype(F32), 0.0).astype(BF)
                dom = jnp.where(mine, _chunk(dov, a).astype(F32), 0.0).astype(BF)
                s_scr[hq] = _band_pick(lax.dot_general(qm, ks[h][par], _DIMS["nt"], preferred_element_type=F32), tri)
                dp_scr[hq] = _band_pick(
                    lax.dot_general(dom, vs[h][par], _DIMS["nt"], preferred_element_type=F32), tri)
            for hq in range(nq):
                s = jnp.where(ok, s_scr[hq] * scale, -jnp.inf)
                sink = sink_ref[hq]
                m = jnp.maximum(jnp.max(s, axis=-1, keepdims=True), sink)
                e = jnp.exp(s - m)
                es = jnp.exp(sink - m)
                zinv = 1.0 / (jnp.sum(e, axis=-1, keepdims=True) + es)
                p = e * zinv
                dp = dp_scr[hq]
                delta = jnp.sum(p * dp, axis=-1, keepdims=True)
                ds_scr[hq] = _band_spread(p * (dp - delta) * scale, tri).astype(BF)
                pf_scr[hq] = _band_spread(p, tri).astype(BF)
                sacc = sacc + jnp.where(lane == hq, -(es * zinv) * delta, 0.0)
            for a in range(nq // 2):
                h = (2 * a) // Q_PER_KV
                qa_t = _chunk(q, a).astype(F32).T
                doa_t = _chunk(dov, a).astype(F32).T
                dq_par = []
                for par in range(2):
                    hq = 2 * a + par
                    mine_t = row_lo if par == 0 else ~row_lo
                    qm_t = jnp.where(mine_t, qa_t, 0.0).astype(BF)
                    dom_t = jnp.where(mine_t, doa_t, 0.0).astype(BF)
                    dsv = ds_scr[hq]
                    dq_par.append(jnp.dot(dsv, ks[h][par], preferred_element_type=F32))
                    dkh = jnp.dot(qm_t, dsv, preferred_element_type=F32)
                    dvh = jnp.dot(dom_t, pf_scr[hq], preferred_element_type=F32)
                    if par != h % 2:
                        dkh = pltpu.roll(dkh, HEAD_DIM, 0)
                        dvh = pltpu.roll(dvh, HEAD_DIM, 0)
                    dk_ch[h // 2] = dk_ch[h // 2] + dkh
                    dv_ch[h // 2] = dv_ch[h // 2] + dvh
                dqa = jnp.where(lo, dq_par[0], dq_par[1])
                dq_ref[:, a * LANES:(a + 1) * LANES] = _rope_bwd(dqa, cc, sc).astype(BF)
            dk_ch = [x.T for x in dk_ch]
            dv_ch = [x.T for x in dv_ch]
            dk = dk_ch[0] if nch == 1 else jnp.concatenate(dk_ch, axis=1)
            dv = dv_ch[0] if nch == 1 else jnp.concatenate(dv_ch, axis=1)
            dk_ref[...] = _rope_bwd(ck_ref[...] + dk[:b, :], cp_ref[...], sp_ref[...]).astype(BF)
            dv_ref[...] = (cv_ref[...] + dv[:b, :]).astype(BF)
            ck_ref[...] = dk[b:, :]
            cv_ref[...] = dv[b:, :]
            sacc_ref[...] += sacc

        @pl.when(i == nb)
        def _():
            dk_ref[...] = _rope_bwd(ck_ref[...], cp_ref[...], sp_ref[...]).astype(BF)
            dv_ref[...] = cv_ref[...].astype(BF)
            ds_ref[...] = jnp.broadcast_to(jnp.sum(sacc_ref[...], axis=0, keepdims=True), ds_ref.shape)

    def cur_i(i):
        return jnp.minimum(i, nb - 1)

    def prev_i(i):
        return jnp.clip(i - 1, 0, nb - 1)

    tab_c = pl.BlockSpec((b, LANES), lambda i: (cur_i(i), 0))
    tab_p = pl.BlockSpec((b, LANES), lambda i: (prev_i(i), 0))
    return pl.pallas_call(
        body,
        name=name,
        grid=(nb + 1,),
        in_specs=[
            pl.BlockSpec(memory_space=pltpu.SMEM),
            pl.BlockSpec((b, aw), lambda i: (cur_i(i), 0)),
            pl.BlockSpec((b, kw), lambda i: (cur_i(i), 0)),
            pl.BlockSpec((b, kw), lambda i: (prev_i(i), 0)),
            pl.BlockSpec((b, kw), lambda i: (cur_i(i), kblk + 1)),
            pl.BlockSpec((b, kw), lambda i: (prev_i(i), kblk + 1)),
            pl.BlockSpec((b, aw), lambda i: (cur_i(i), 0)),
            tab_c,
            tab_p,
            tab_c,
            tab_p,
        ],
        out_specs=[
            pl.BlockSpec((b, aw), lambda i: (cur_i(i), 0)),
            pl.BlockSpec((b, kw), lambda i: (prev_i(i), 0)),
            pl.BlockSpec((b, kw), lambda i: (prev_i(i), 0)),
            pl.BlockSpec((SUBLANES, LANES), lambda i: (0, 0)),
        ],
        out_shape=[
            jax.ShapeDtypeStruct((t, aw), BF),
            jax.ShapeDtypeStruct((t, kw), BF),
            jax.ShapeDtypeStruct((t, kw), BF),
            jax.ShapeDtypeStruct((SUBLANES, LANES), F32),
        ],
        scratch_shapes=[pltpu.VMEM((b, kw), F32), pltpu.VMEM((b, kw), F32), pltpu.VMEM((b, LANES), F32),
                        pltpu.VMEM((nq, b, b), F32), pltpu.VMEM((nq, b, b), F32),
                        pltpu.VMEM((nq, b, 2 * b), BF), pltpu.VMEM((nq, b, 2 * b), BF)],
        compiler_params=_cp("arbitrary"),
    )(sinks, qr, kr, kr, proj, proj, do, cosf, cosf, sinf, sinf)


def _xattn_fwd(xq, kv, *, name, tq=512):
    t, xw = xq.shape
    mtok = kv.shape[0]
    tq = min(tq, t)
    nh = xw // X_HEAD_DIM
    scale = X_HEAD_DIM ** -0.5

    def body(q_ref, kv_ref, o_ref):
        q = q_ref[...]
        kvv = kv_ref[...]
        outs = []
        for h in range(nh):
            sl = slice(h * X_HEAD_DIM, (h + 1) * X_HEAD_DIM)
            k = kvv[:, sl]
            v = kvv[:, xw + h * X_HEAD_DIM: xw + (h + 1) * X_HEAD_DIM]
            s = lax.dot_general(q[:, sl], k, _DIMS["nt"], preferred_element_type=F32) * scale
            e = jnp.exp(s - jnp.max(s, axis=-1, keepdims=True))
            p = e / jnp.sum(e, axis=-1, keepdims=True)
            outs.append(jnp.dot(p.astype(BF), v, preferred_element_type=F32))
        o_ref[...] = jnp.concatenate(outs, axis=1).astype(BF)

    return pl.pallas_call(
        body,
        name=name,
        grid=(t // tq,),
        in_specs=[pl.BlockSpec((tq, xw), lambda i: (i, 0)), pl.BlockSpec((mtok, 2 * xw), lambda i: (0, 0))],
        out_specs=pl.BlockSpec((tq, xw), lambda i: (i, 0)),
        out_shape=jax.ShapeDtypeStruct((t, xw), BF),
        compiler_params=_cp("parallel"),
    )(xq, kv)


def _xattn_bwd(xq, kv, do, *, name, tq=512):
    t, xw = xq.shape
    mtok = kv.shape[0]
    tq = min(tq, t)
    nh = xw // X_HEAD_DIM
    scale = X_HEAD_DIM ** -0.5

    def body(q_ref, kv_ref, do_ref, dq_ref, dkv_ref):
        i = pl.program_id(0)
        q = q_ref[...]
        kvv = kv_ref[...]
        dov = do_ref[...]
        dqs, dks, dvs = [], [], []
        for h in range(nh):
            sl = slice(h * X_HEAD_DIM, (h + 1) * X_HEAD_DIM)
            k = kvv[:, sl]
            v = kvv[:, xw + h * X_HEAD_DIM: xw + (h + 1) * X_HEAD_DIM]
            qh, doh = q[:, sl], dov[:, sl]
            s = lax.dot_general(qh, k, _DIMS["nt"], preferred_element_type=F32) * scale
            e = jnp.exp(s - jnp.max(s, axis=-1, keepdims=True))
            p = e / jnp.sum(e, axis=-1, keepdims=True)
            dp = lax.dot_general(doh, v, _DIMS["nt"], preferred_element_type=F32)
            delta = jnp.sum(p * dp, axis=-1, keepdims=True)
            dsv = (p * (dp - delta) * scale).astype(BF)
            dqs.append(jnp.dot(dsv, k, preferred_element_type=F32))
            dks.append(lax.dot_general(dsv, qh, _DIMS["tn"], preferred_element_type=F32))
            dvs.append(lax.dot_general(p.astype(BF), doh, _DIMS["tn"], preferred_element_type=F32))
        dq_ref[...] = jnp.concatenate(dqs, axis=1).astype(BF)

        @pl.when(i == 0)
        def _():
            dkv_ref[...] = jnp.zeros_like(dkv_ref)

        dkv_ref[...] += jnp.concatenate(dks + dvs, axis=1)

    row = pl.BlockSpec((tq, xw), lambda i: (i, 0))
    full = pl.BlockSpec((mtok, 2 * xw), lambda i: (0, 0))
    return pl.pallas_call(
        body,
        name=name,
        grid=(t // tq,),
        in_specs=[row, full, row],
        out_specs=[row, full],
        out_shape=[jax.ShapeDtypeStruct((t, xw), BF), jax.ShapeDtypeStruct((mtok, 2 * xw), F32)],
        compiler_params=_cp("arbitrary"),
    )(xq, kv, do)


def _adam_math(w, g, m, v):
    m = ADAM_B1 * m + (1.0 - ADAM_B1) * g
    v = ADAM_B2 * v + (1.0 - ADAM_B2) * (g * g)
    m_hat = m / (1.0 - ADAM_B1 ** ADAM_STEP)
    v_hat = v / (1.0 - ADAM_B2 ** ADAM_STEP)
    delta = -ADAM_LR * (m_hat / (jnp.sqrt(v_hat) + ADAM_EPS) + ADAM_WD * w)
    return delta, m, v


def _row_tile(r, c, n_arrays, budget=24 * 1024 * 1024):
    step = 2 * SUBLANES
    cap = max(step, budget // (2 * n_arrays * c * 4))
    if r <= cap:
        return r
    best = None
    for tr in range(step, cap + 1, step):
        if r % tr == 0:
            best = tr
    assert best is not None, (r, c)
    return best


def _adamw_sum(parts, own, me, w, m, v, *, name):
    _, r, c = parts.shape
    tr = _row_tile(r, c, 12)

    def body(me_ref, p_ref, own_ref, w_ref, m_ref, v_ref, g_ref, d_ref, nm_ref, nv_ref):
        mine = jnp.full((tr, c), me_ref[0], jnp.int32)
        g = None
        for s in range(N_DEV):
            term = jnp.where(mine == s, own_ref[...], p_ref[s]).astype(F32)
            g = term if g is None else g + term
        g_ref[...] = g
        d_ref[...], nm_ref[...], nv_ref[...] = _adam_math(w_ref[...], g, m_ref[...], v_ref[...])

    blk = pl.BlockSpec((tr, c), lambda i, me_ref: (i, 0))
    out = jax.ShapeDtypeStruct((r, c), F32)
    return pl.pallas_call(
        body,
        name=name,
        grid_spec=pltpu.PrefetchScalarGridSpec(
            num_scalar_prefetch=1,
            grid=(r // tr,),
            in_specs=[
                pl.BlockSpec((N_DEV, tr, c), lambda i, me_ref: (0, i, 0)),
                pl.BlockSpec((None, tr, c), lambda i, me_ref: (me_ref[0], i, 0)),
                blk, blk, blk,
            ],
            out_specs=[blk, blk, blk, blk],
        ),
        out_shape=[out, out, out, out],
        compiler_params=_cp("parallel"),
    )(me, parts, own, w, m, v)


def _adamw_sum_pieces(parts, owns, me, w, m, v, *, name):
    r = w.shape[0]
    widths = [p.shape[2] for p in parts]
    c = sum(widths)
    k = len(parts)
    tr = _row_tile(r, c, 12)

    def body(me_ref, *refs):
        p_refs, own_refs = refs[:k], refs[k:2 * k]
        w_ref, m_ref, v_ref, g_ref, d_ref, nm_ref, nv_ref = refs[2 * k:]
        off = 0
        for p_ref, own_ref, ck in zip(p_refs, own_refs, widths):
            mine = jnp.full((tr, ck), me_ref[0], jnp.int32)
            g = None
            for s in range(N_DEV):
                term = jnp.where(mine == s, own_ref[...], p_ref[s]).astype(F32)
                g = term if g is None else g + term
            sl = slice(off, off + ck)
            g_ref[:, sl] = g
            d_ref[:, sl], nm_ref[:, sl], nv_ref[:, sl] = _adam_math(w_ref[:, sl], g, m_ref[:, sl], v_ref[:, sl])
            off += ck

    blk = pl.BlockSpec((tr, c), lambda i, me_ref: (i, 0))
    out = jax.ShapeDtypeStruct((r, c), F32)
    in_specs = [pl.BlockSpec((N_DEV, tr, ck), lambda i, me_ref: (0, i, 0)) for ck in widths]
    in_specs += [pl.BlockSpec((None, tr, ck), lambda i, me_ref: (me_ref[0], i, 0)) for ck in widths]
    return pl.pallas_call(
        body,
        name=name,
        grid_spec=pltpu.PrefetchScalarGridSpec(
            num_scalar_prefetch=1,
            grid=(r // tr,),
            in_specs=in_specs + [blk, blk, blk],
            out_specs=[blk, blk, blk, blk],
        ),
        out_shape=[out, out, out, out],
        compiler_params=_cp("parallel"),
    )(me, *parts, *owns, w, m, v)


def _to_bf16(a, *, name, dep=None):
    r, c = a.shape
    tr = _row_tile(r, c, 2)

    def body(*refs):
        refs[-1][...] = refs[0][...].astype(BF)

    blk = pl.BlockSpec((tr, c), lambda i: (i, 0))
    in_specs, args = [blk], (a,)
    if dep is not None:
        in_specs.append(pl.BlockSpec(memory_space=pl.ANY))
        args += (dep,)
    return pl.pallas_call(
        body,
        name=name,
        grid=(r // tr,),
        in_specs=in_specs,
        out_specs=blk,
        out_shape=jax.ShapeDtypeStruct((r, c), BF),
        compiler_params=_cp("parallel"),
    )(*args)


def _adamw_small(w, g, m, v, *, name):
    def body(w_ref, g_ref, m_ref, v_ref, d_ref, nm_ref, nv_ref):
        d_ref[...], nm_ref[...], nv_ref[...] = _adam_math(w_ref[...], g_ref[...], m_ref[...], v_ref[...])

    out = jax.ShapeDtypeStruct(w.shape, F32)
    return pl.pallas_call(body, name=name, out_shape=[out, out, out])(w, g, m, v)


def _mesh_pos():
    x, y, c = lax.axis_index("x"), lax.axis_index("y"), lax.axis_index("c")
    return x, y, c


def _peer(x, y, c, mask):
    px = 1 - x if mask & 4 else x
    py = 1 - y if mask & 2 else y
    pc = 1 - c if mask & 1 else c
    return (px, py, pc), 4 * px + 2 * py + pc


_HBM = pl.BlockSpec(memory_space=pltpu.HBM)
_SEM = pl.BlockSpec(memory_space=pltpu.SEMAPHORE)
_EFFECT = pltpu.SideEffectType.DATAFLOW_SIDE_EFFECTING


def _me():
    return 4 * lax.axis_index("x") + 2 * lax.axis_index("y") + lax.axis_index("c")


def _landing(own, me):
    land = lax.empty((N_DEV,) + own.shape, own.dtype)
    return lax.dynamic_update_slice(land, own[None], (me, 0, 0))


_ALL = tuple(range(1, N_DEV))
_CHIPS = (2, 4, 6)
SCATTER_DIRECT = tuple((m, m, 0, m) for m in _ALL)
GATHER_CHIPS = tuple((m, None, 0, m) for m in (1,) + _CHIPS)
GATHER_SIBLING = tuple((1, m, m, m ^ 1) for m in _CHIPS)


def _copy(src, land, send_sem, recv_sem, sem, x, y, c, entry, arriving):
    to, src_m, dst_m, arr_m = entry
    peer, _ = _peer(x, y, c, to)
    blk = lambda m: _peer(x, y, c, m)[1]
    return pltpu.make_async_remote_copy(
        src_ref=src if src_m is None else src.at[blk(src_m)],
        dst_ref=land.at[blk(arr_m if arriving else dst_m)],
        send_sem=send_sem.at[sem], recv_sem=recv_sem.at[sem], device_id=peer, device_id_type=MESH)


def _exchange_start(groups, plan, *, name, after=None):
    flat = [p for g in groups for p in g]
    from_land = flat[0][0] is None
    n, ng, nc = len(flat), len(groups), len(plan)
    n_buf = n if from_land else 2 * n

    def body(*refs):
        lands = refs[:n] if from_land else refs[n:2 * n]
        srcs = lands if from_land else refs[:n]
        sems = refs[n_buf + (after is not None):n_buf + (after is not None) + 2 * ng]
        token = refs[-1]
        x, y, c = _mesh_pos()
        w = 0
        for gi, g in enumerate(groups):
            for wi in range(len(g)):
                for k, entry in enumerate(plan):
                    _copy(srcs[w], lands[w], sems[2 * gi], sems[2 * gi + 1], wi * nc + k, x, y, c, entry,
                          False).start()
                w += 1
        token[...] = jnp.zeros_like(token)

    sem_shapes = []
    for g in groups:
        sem_shapes += [pltpu.SemaphoreType.DMA((len(g) * nc,))] * 2
    args = [] if from_land else [pltpu.with_memory_space_constraint(s, pltpu.HBM) for s, _ in flat]
    args += [pltpu.with_memory_space_constraint(l, pltpu.HBM) for _, l in flat]
    extra = [] if after is None else [after]
    outs = pl.pallas_call(
        body,
        name=name,
        in_specs=[_HBM] * n_buf + [pl.BlockSpec(memory_space=pl.ANY)] * len(extra),
        out_specs=[_SEM] * (2 * ng) + [_HBM] * n_buf + [pl.BlockSpec(memory_space=pltpu.VMEM)],
        out_shape=sem_shapes + [pltpu.HBM(a.shape, a.dtype) for a in args]
        + [jax.ShapeDtypeStruct((SUBLANES, LANES), F32)],
        input_output_aliases={i: 2 * ng + i for i in range(n_buf)},
        compiler_params=pltpu.CompilerParams(has_side_effects=_EFFECT),
    )(*args, *extra)
    sems, thru, token = outs[:2 * ng], outs[2 * ng:2 * ng + n_buf], outs[-1]
    res, w = [], 0
    for gi, g in enumerate(groups):
        m = len(g)
        srcs = [None] * m if from_land else list(thru[w:w + m])
        lands = list(thru[w:w + m]) if from_land else list(thru[n + w:n + w + m])
        res.append((sems[2 * gi], sems[2 * gi + 1], srcs, lands))
        w += m
    return res, token


def _exchange_wait(group, plan, after, *, name):
    send_sems, recv_sems, srcs_in, lands_in = group
    n, nc = len(lands_in), len(plan)
    from_land = srcs_in[0] is None
    n_buf = n if from_land else 2 * n

    def body(*refs):
        lands = refs[:n] if from_land else refs[n:2 * n]
        srcs = lands if from_land else refs[:n]
        send_sem, recv_sem = refs[n_buf], refs[n_buf + 1]
        x, y, c = _mesh_pos()
        for w in range(n):
            for k, entry in enumerate(plan):
                cp = _copy(srcs[w], lands[w], send_sem, recv_sem, w * nc + k, x, y, c, entry, True)
                cp.wait_send()
                cp.wait_recv()

    bufs = lands_in if from_land else srcs_in + lands_in
    outs = pl.pallas_call(
        body,
        name=name,
        in_specs=[_HBM] * n_buf + [_SEM, _SEM, pl.BlockSpec(memory_space=pl.ANY)],
        out_specs=[_HBM] * n_buf,
        out_shape=[pltpu.HBM(a.shape, a.dtype) for a in bufs],
        input_output_aliases={i: i for i in range(n_buf)},
        compiler_params=pltpu.CompilerParams(has_side_effects=_EFFECT),
    )(*bufs, send_sems, recv_sems, after)
    if from_land:
        return [None] * n, list(outs)
    return list(outs[:n]), list(outs[n:])


def _all_reduce_small(parts, rows, width, *, name, dep=None):
    n = len(parts)

    def body(*refs):
        ins = refs[:n]
        o_ref, pack_ref, buf_ref, send_sems, recv_sems = refs[-5:]
        x, y, c_ = _mesh_pos()
        me = 4 * x + 2 * y + c_
        pack_ref[...] = jnp.zeros_like(pack_ref)
        for ref, (arr, r0, nr) in zip(ins, parts):
            pack_ref[r0:r0 + nr, 0:arr.shape[1]] = ref[0:nr, :]
        sends, recvs = [], []
        for k in range(N_DEV - 1):
            peer, pidx = _peer(x, y, c_, k + 1)
            cp = pltpu.make_async_remote_copy(
                src_ref=pack_ref, dst_ref=buf_ref.at[me], send_sem=send_sems.at[k], recv_sem=recv_sems.at[k],
                device_id=peer, device_id_type=MESH)
            cp.start()
            sends.append(cp)
            recvs.append(pltpu.make_async_remote_copy(
                src_ref=pack_ref, dst_ref=buf_ref.at[pidx], send_sem=send_sems.at[k], recv_sem=recv_sems.at[k],
                device_id=peer, device_id_type=MESH))
        buf_ref[me] = pack_ref[...]
        for rc in recvs:
            rc.wait_recv()
        for cp in sends:
            cp.wait_send()
        acc = buf_ref[0]
        for s in range(1, N_DEV):
            acc = acc + buf_ref[s]
        o_ref[...] = acc

    vmem = pl.BlockSpec(memory_space=pltpu.VMEM)
    in_specs = [vmem] * n
    args = [p[0] for p in parts]
    if dep is not None:
        in_specs.append(pl.BlockSpec(memory_space=pl.ANY))
        args.append(dep)
    return pl.pallas_call(
        body,
        name=name,
        in_specs=in_specs,
        out_specs=vmem,
        out_shape=jax.ShapeDtypeStruct((rows, width), F32),
        scratch_shapes=[
            pltpu.VMEM((rows, width), F32),
            pltpu.VMEM((N_DEV, rows, width), F32),
            pltpu.SemaphoreType.DMA((N_DEV - 1,)),
            pltpu.SemaphoreType.DMA((N_DEV - 1,)),
        ],
    )(*args)


def _rope_tables(t):
    half = HEAD_DIM // 2
    inv_freq = ROPE_THETA ** (-jnp.arange(half, dtype=F32) / half)
    ang = jnp.arange(t, dtype=jnp.int32).astype(F32)[:, None] * inv_freq[None, :]
    cos, sin = jnp.cos(ang), jnp.sin(ang)
    cosf = jnp.concatenate([cos, cos, cos, cos], axis=1)
    sinf = jnp.concatenate([-sin, sin, -sin, sin], axis=1)
    return cosf, sinf


def _local_step(x, mem, target, gains, sinks, aw, cw, pre_w, get_w, put_g, dep0=None):
    t, d = x.shape
    nq = aw // HEAD_DIM
    kw = aw // Q_PER_KV
    z0 = aw + 2 * kw
    gb0, gc0 = z0 + cw, z0 + 2 * cw
    ga0 = z0 + 3 * cw
    gcm0 = ga0 + d
    cosf, sinf = _rope_tables(t)

    u1 = _rms_fwd(x, gains["g_mix"], name="rms_mix", dep=(cosf, sinf) if dep0 is None else (dep0, cosf, sinf))
    mem_n = _rms_fwd(mem, gains["g_mem"], name="rms_mem", dep=dep0)
    pre_w("w_in", u1)
    w_in_t = get_w("w_in", u1)
    proj = _mm(u1, w_in_t, mode="nt", tm=1024, tn=512, tk=2048, out_dtype=F32, name="mm_in")
    o_attn, q_rot, k_rot = _swa_fwd(proj, cosf, sinf, sinks, nq=nq, name="swa_fwd", dep=pre_w("conv_w8", proj))
    conv_w8 = get_w("conv_w8", o_attn)
    w_attn_proj, w_conv_proj, w_mix_out = (get_w(n, o_attn) for n in ("w_attn_proj", "w_conv_proj", "w_mix_out"))
    w_xq, w_xkv, w_xo = (get_w(n, o_attn) for n in ("w_xq", "w_xkv", "w_xo"))
    y_attn = _mm(o_attn, w_attn_proj, mode="nn", tm=1024, tn=1024, tk=1024, out_dtype=F32, name="mm_attn_proj")
    cy = _conv_fwd(proj, conv_w8, z0=z0, gb0=gb0, gc0=gc0, cw=cw, name="conv_fwd")
    y_conv, merged = _gate_fwd(cy, w_conv_proj, proj, y_attn, ga0=ga0, gc0=gcm0, name="mm_conv_proj")
    h1, u2 = _mm(merged, w_mix_out, mode="nn", tm=512, tn=d, tk=2048, out_dtype=F32, name="mm_mix_out", residual=x,
                 rms_gain=gains["g_xattn"])
    xq = _mm(u2, w_xq, mode="nn", tm=1024, tn=512, tk=2048, out_dtype=BF, name="mm_xq",
             dep=pre_w("w_ffn_in", h1))
    kv = _mm(mem_n, w_xkv, mode="nn", tm=256, tn=1024, tk=2048, out_dtype=BF, name="mm_xkv")
    o_x = _xattn_fwd(xq, kv, name="xattn_fwd")
    h2, u3 = _mm(o_x, w_xo, mode="nn", tm=512, tn=d, tk=512, out_dtype=F32, name="mm_xo", residual=h1,
                 rms_gain=gains["g_ffn"])
    w_ffn_in = get_w("w_ffn_in", xq)
    hid2, act = _ffn_in_fwd(u3, w_ffn_in, name="mm_ffn_in", dep=pre_w("w_ffn_out", u3))
    w_ffn_out = get_w("w_ffn_out", act)
    h3 = _mm(act, w_ffn_out, mode="nn", tm=512, tn=1024, tk=8192, out_dtype=F32, name="mm_ffn_out", residual=h2)

    tt = 8192
    dh3, dh3b, loss_tile, dg_final = _loss_head(h3, target, gains["g_final"], name="loss_head")
    tok = put_g("w_ffn_out", _mm(act, dh3b, mode="tn", tm=512, tn=1024, tk=tt, out_dtype=BF, name="mm_dw_ffn_out"))
    dhid2 = _ffn_out_bwd(dh3b, w_ffn_out, hid2, name="mm_dact", dep=tok)
    f2 = w_ffn_in.shape[1]
    tok = put_g("w_ffn_in", _mm(u3, dhid2, mode="tn", tm=1024, tn=f2 // N_DEV, tk=tt, out_dtype=BF,
                                name="mm_dw_ffn_in", b_planes=2, stacked=True), stacked=True)
    du3 = _mm(dhid2, w_ffn_in, mode="nt", tm=1024, tn=1024, tk=2816, out_dtype=F32, name="mm_du3", dep=tok,
              a_planes=2)
    dh2, dh2b, dg_ffn = _rms_bwd(du3, h2, gains["g_ffn"], dh3, name="rms_ffn_bwd")
    put_g("w_xo", _mm(o_x, dh2b, mode="tn", tm=512, tn=d // N_DEV, tk=tt, out_dtype=BF, name="mm_dw_xo",
                      stacked=True), stacked=True)
    do_x = _mm(dh2b, w_xo, mode="nt", tm=1024, tn=512, tk=2048, out_dtype=BF, name="mm_do_x")
    dxq, dkv = _xattn_bwd(xq, kv, do_x, name="xattn_bwd")
    put_g("w_xkv", _mm(mem_n, dkv, mode="tn", tm=1024, tn=1024, tk=256, out_dtype=BF, name="mm_dw_xkv"))
    tok = put_g("w_xq", _mm(u2, dxq, mode="tn", tm=1024, tn=512, tk=tt, out_dtype=BF, name="mm_dw_xq"))
    tok_xq = tok
    dmem_n = _mm(dkv, w_xkv, mode="nt", tm=256, tn=1024, tk=1024, out_dtype=F32, name="mm_dmem")
    _, _, dg_mem = _rms_bwd(dmem_n, mem, gains["g_mem"], None, name="rms_mem_bwd")
    dh1, dh1b, dg_xattn = _rms_bwd(dxq, h1, gains["g_xattn"], dh2, name="rms_xattn_bwd", du_w=w_xq, dep=tok_xq)
    put_g("w_mix_out", _mm(merged, dh1b, mode="tn", tm=1024, tn=1024, tk=tt, out_dtype=BF, name="mm_dw_mix_out"))
    dya, dyc, dga, dgc = _gate_bwd(dh1b, w_mix_out, proj, y_attn, y_conv, ga0=ga0, gc0=gcm0, name="mm_dmerged")
    put_g("w_attn_proj", _mm(o_attn, dya, mode="tn", tm=1024, tn=d // N_DEV, tk=tt, out_dtype=BF,
                             name="mm_dw_attn_proj", stacked=True), stacked=True)
    do_attn = _mm(dya, w_attn_proj, mode="nt", tm=1024, tn=1024, tk=2048, out_dtype=BF, name="mm_do_attn")
    tok = put_g("w_conv_proj", _mm(cy, dyc, mode="tn", tm=1024, tn=d // N_DEV, tk=tt, out_dtype=BF,
                                   name="mm_dw_conv_proj", stacked=True), stacked=True)
    dcy = _mm(dyc, w_conv_proj, mode="nt", tm=1024, tn=1024, tk=2048, out_dtype=F32, name="mm_dcy", dep=tok)
    dz, dgb, dgcv, dconv_w8 = _conv_bwd(proj, conv_w8, dcy, z0=z0, gb0=gb0, gc0=gc0, cw=cw, name="conv_bwd")
    dq, dk, dv, dsink_tile = _swa_bwd(q_rot, k_rot, proj, do_attn, cosf, sinf, sinks, nq=nq, name="swa_bwd")
    dproj = jnp.concatenate([dq, dk, dv, dz, dgb, dgcv, dga, dgc], axis=1)
    for hi in range(2):
        tok = put_g("w_in_%d" % hi, _mm(dproj, u1, mode="tn", tm=512, tn=d // 2, tk=tt, out_dtype=BF,
                                        name="mm_dw_in_%d" % hi, b_cols=(hi * (d // 2), d // 2), dep=tok))
    du1 = _mm(dproj, w_in_t, mode="nn", tm=512, tn=1024, tk=4352, out_dtype=F32, name="mm_du1", dep=tok)
    grad_x, _, dg_mix = _rms_bwd(du1, x, gains["g_mix"], dh1, name="rms_mix_bwd")

    small = {
        "g_mix": dg_mix, "g_xattn": dg_xattn, "g_mem": dg_mem, "g_ffn": dg_ffn, "g_final": dg_final,
        "attn_sinks": dsink_tile, "conv_w8": dconv_w8, "loss": loss_tile,
    }
    return grad_x, small


_COL_SHARDED = ("w_in", "w_attn_proj", "w_conv_proj", "w_xo", "w_ffn_in")
_ROW_SHARDED = ("w_mix_out", "w_xq", "w_xkv", "w_ffn_out")
_BIG = _COL_SHARDED + _ROW_SHARDED
_GAINS = ("g_mix", "g_xattn", "g_mem", "g_ffn", "g_final")
_GATHER_GROUPS = (("w_in",), ("conv_w8", "w_attn_proj", "w_conv_proj", "w_mix_out", "w_xq", "w_xkv", "w_xo"),
                  ("w_ffn_in",), ("w_ffn_out",))
_SCATTER_GROUPS = (("w_ffn_out",), ("w_ffn_in",), ("w_xo", "w_xq", "w_xkv"),
                   ("w_mix_out", "w_attn_proj", "w_conv_proj"), ("w_in_0",), ("w_in_1",))
_WEIGHTS = ("g_mix", "w_in", "conv_w", "attn_sinks", "w_attn_proj", "w_conv_proj", "w_mix_out", "g_xattn", "g_mem",
            "w_xq", "w_xkv", "w_xo", "g_ffn", "w_ffn_in", "w_ffn_out", "g_final")


def _unstack(g, col_sharded):
    n, r, c = g.shape
    if col_sharded:
        return jnp.transpose(g, (1, 0, 2)).reshape(r, n * c)
    return g.reshape(n * r, c)


def _stack(w, col_sharded):
    r, c = w.shape
    if col_sharded:
        return jnp.transpose(w.reshape(r, N_DEV, c // N_DEV), (1, 0, 2))
    return w.reshape(N_DEV, r // N_DEV, c)


def kernel(x, mem, g_mix, w_in, conv_w, attn_sinks, w_attn_proj, w_conv_proj, w_mix_out, g_xattn, g_mem, w_xq, w_xkv, w_xo, g_ffn, w_ffn_in, w_ffn_out, g_final, loss_target, m_g_mix, m_w_in, m_conv_w, m_attn_sinks, m_w_attn_proj, m_w_conv_proj, m_w_mix_out, m_g_xattn, m_g_mem, m_w_xq, m_w_xkv, m_w_xo, m_g_ffn, m_w_ffn_in, m_w_ffn_out, m_g_final, v_g_mix, v_w_in, v_conv_w, v_attn_sinks, v_w_attn_proj, v_w_conv_proj, v_w_mix_out, v_g_xattn, v_g_mem, v_w_xq, v_w_xkv, v_w_xo, v_g_ffn, v_w_ffn_in, v_w_ffn_out, v_g_final):
    w_ = dict(g_mix=g_mix, w_in=w_in, conv_w=conv_w, attn_sinks=attn_sinks, w_attn_proj=w_attn_proj,
              w_conv_proj=w_conv_proj, w_mix_out=w_mix_out, g_xattn=g_xattn, g_mem=g_mem, w_xq=w_xq, w_xkv=w_xkv,
              w_xo=w_xo, g_ffn=g_ffn, w_ffn_in=w_ffn_in, w_ffn_out=w_ffn_out, g_final=g_final)
    m_ = dict(g_mix=m_g_mix, w_in=m_w_in, conv_w=m_conv_w, attn_sinks=m_attn_sinks, w_attn_proj=m_w_attn_proj,
              w_conv_proj=m_w_conv_proj, w_mix_out=m_w_mix_out, g_xattn=m_g_xattn, g_mem=m_g_mem, w_xq=m_w_xq,
              w_xkv=m_w_xkv, w_xo=m_w_xo, g_ffn=m_g_ffn, w_ffn_in=m_w_ffn_in, w_ffn_out=m_w_ffn_out,
              g_final=m_g_final)
    v_ = dict(g_mix=v_g_mix, w_in=v_w_in, conv_w=v_conv_w, attn_sinks=v_attn_sinks, w_attn_proj=v_w_attn_proj,
              w_conv_proj=v_w_conv_proj, w_mix_out=v_w_mix_out, g_xattn=v_g_xattn, g_mem=v_g_mem, w_xq=v_w_xq,
              w_xkv=v_w_xkv, w_xo=v_w_xo, g_ffn=v_g_ffn, w_ffn_in=v_w_ffn_in, w_ffn_out=v_w_ffn_out,
              g_final=v_g_final)
    t, d = x.shape[1], x.shape[2]
    nq = attn_sinks.shape[-1]
    cw_shard = conv_w.shape[-1]
    cw = cw_shard * N_DEV

    def two_d(a):
        return a.reshape(a.shape[-2], a.shape[-1]) if a.ndim == 3 else a.reshape(1, a.shape[-1])

    me = _me()
    col = (set(_COL_SHARDED) | {"conv_w8"}) - {"w_in"}

    shards = {"w_in": two_d(w_in).T.astype(BF)}
    first, token = _exchange_start(
        [[(shards[n], _landing(shards[n], me)) for n in g] for g in _GATHER_GROUPS[:1]], GATHER_CHIPS,
        name="gather_start_0")
    for n in _BIG:
        if n != "w_in":
            shards[n] = _to_bf16(two_d(w_[n]), name="cast_" + n, dep=token)
    shards["conv_w8"] = jnp.zeros((SUBLANES, cw_shard), F32).at[:3].set(two_d(conv_w))
    rest, token = _exchange_start(
        [[(shards[n], _landing(shards[n], me)) for n in g] for g in _GATHER_GROUPS[1:]], GATHER_CHIPS,
        name="gather_start_1", after=token)
    gathers = first + rest
    passes, full = {}, {}

    def group_of(name):
        return [name in g for g in _GATHER_GROUPS].index(True)

    def pre_w(name, after):
        gi = group_of(name)
        _, lands = _exchange_wait(gathers[gi], GATHER_CHIPS, after, name="gather_wait_%d" % gi)
        started, tok = _exchange_start([[(None, land) for land in lands]], GATHER_SIBLING,
                                       name="gather_pass_%d" % gi)
        passes[gi] = started[0]
        return tok

    def get_w(name, after):
        if name not in full:
            gi = group_of(name)
            _, lands = _exchange_wait(passes[gi], GATHER_SIBLING, after, name="gather_pass_wait_%d" % gi)
            for n, land in zip(_GATHER_GROUPS[gi], lands):
                full[n] = _unstack(land, n in col)
        return full[name]

    pending, scatters = {}, []

    def put_g(name, dw, stacked=False):
        pending[name] = dw if stacked else _stack(dw, name in col)
        gi = [name in g for g in _SCATTER_GROUPS].index(True)
        group = _SCATTER_GROUPS[gi]
        if not all(n in pending for n in group):
            return None
        pairs = [(pending[n], lax.empty(pending[n].shape, pending[n].dtype)) for n in group]
        started, tok = _exchange_start([pairs], SCATTER_DIRECT, name="scatter_start_%d" % gi)
        scatters.append((gi, started[0]))
        return tok

    gains = {n: two_d(w_[n]) for n in _GAINS}
    grad_x, small = _local_step(
        x[0], mem[0], loss_target[0], gains, attn_sinks.reshape(nq), w_attn_proj.shape[-2], cw, pre_w, get_w, put_g,
        dep0=token)

    grads, deltas, new_m, new_v = {}, {}, {}, {}
    me1 = me.reshape(1).astype(jnp.int32)
    after, halves = grad_x, []
    for gi, started in scatters:
        mine, parts = _exchange_wait(started, SCATTER_DIRECT, after, name="scatter_wait_%d" % gi)
        for n, own, p in zip(_SCATTER_GROUPS[gi], mine, parts):
            if n.startswith("w_in_"):
                halves.append((p, own))
                if len(halves) < 2:
                    continue
                n = "w_in"
                g, dl, nm, nv = (a.T for a in _adamw_sum_pieces(
                    [h[0] for h in halves], [h[1] for h in halves], me1, two_d(w_[n]).T, two_d(m_[n]).T,
                    two_d(v_[n]).T, name="adamw_" + n))
            else:
                g, dl, nm, nv = _adamw_sum(p, own, me1, two_d(w_[n]), two_d(m_[n]), two_d(v_[n]), name="adamw_" + n)
            shape = w_[n].shape
            grads[n], deltas[n], new_m[n], new_v[n] = (a.reshape(shape) for a in (g, dl, nm, nv))
            after = g

    parts = [(small[n], i, 1) for i, n in enumerate(_GAINS)]
    parts += [(small["attn_sinks"], 5, 1), (small["loss"], 6, 1), (small["conv_w8"], 8, 3)]
    red = _all_reduce_small(parts, 2 * SUBLANES, max(d, cw), name="reduce_small", dep=after)
    loss = red[6, 0]
    small_g = {n: red[i:i + 1, :d] for i, n in enumerate(_GAINS)}
    small_g["attn_sinks"] = red[5:6, :nq]
    small_g["conv_w"] = lax.dynamic_slice(red, (8, me * cw_shard), (3, cw_shard))
    for n in _GAINS + ("attn_sinks", "conv_w"):
        shape = w_[n].shape
        g = small_g[n]
        dl, nm, nv = _adamw_small(two_d(w_[n]), g, two_d(m_[n]), two_d(v_[n]), name="adamw_" + n)
        grads[n], deltas[n], new_m[n], new_v[n] = (a.reshape(shape) for a in (g, dl, nm, nv))

    return (loss, grad_x[None], *[grads[n] for n in _WEIGHTS], *[deltas[n] for n in _WEIGHTS],
            *[new_m[n] for n in _WEIGHTS], *[new_v[n] for n in _WEIGHTS])
```

```python
import math

import jax
import jax.numpy as jnp
from jax import lax
from jax.experimental import pallas as pl
from jax.experimental.pallas import tpu as pltpu

HEAD_DIM = 64
Q_PER_KV = 4
WINDOW = 128
X_HEAD_DIM = 128
ROPE_THETA = 10000.0
EPS = 1e-6
ADAM_LR = 0.001
ADAM_B1 = 0.9
ADAM_B2 = 0.999
ADAM_EPS = 1e-08
ADAM_WD = 0.01
ADAM_STEP = 10

N_DEV = 8
LANES = 128
SUBLANES = 8
VMEM_LIMIT_BYTES = 56 * 1024 * 1024
BF = jnp.bfloat16
F32 = jnp.float32
MESH = pl.DeviceIdType.MESH


def _cp(*sem):
    return pltpu.CompilerParams(dimension_semantics=sem, vmem_limit_bytes=VMEM_LIMIT_BYTES)


def _sigmoid(x):
    return 1.0 / (1.0 + jnp.exp(-x))


_DIMS = {
    "nn": (((1,), (0,)), ((), ())),
    "nt": (((1,), (1,)), ((), ())),
    "tn": (((0,), (0,)), ((), ())),
}


def _fit(dim, tile):
    if dim <= tile:
        return dim
    for t in range(tile // LANES * LANES, 0, -LANES):
        if dim % t == 0:
            return t
    return dim


def _mm(a, b, *, mode, tm, tn, tk, out_dtype, name, residual=None, dep=None, a_planes=1, b_planes=1,
        stacked=False, b_cols=None, a_cols=None, rms_gain=None):
    if a_planes > 1:
        assert mode == "nt"
        (_, m, kp), (n, k) = a.shape, b.shape
        assert kp * a_planes == k
    elif b_planes > 1:
        assert mode == "tn"
        (k, m), (_, k2, np_) = a.shape, b.shape
        n = np_ * b_planes
        assert k == k2
    elif mode == "nn":
        (m, k), (k2, n) = a.shape, b.shape
        assert k == k2, (name, a.shape, b.shape)
    elif mode == "nt":
        (m, k), (n, k2) = a.shape, b.shape
        if a_cols is not None:
            k = a_cols[1]
        assert k == k2, (name, a.shape, b.shape)
    else:
        (k, m), (k2, n) = a.shape, b.shape
        assert k == k2, (name, a.shape, b.shape)
    tm, tn, tk = _fit(m, tm), _fit(n // b_planes, tn), _fit(k // a_planes, tk)
    assert m % tm == 0 and (n // b_planes) % tn == 0 and (k // a_planes) % tk == 0, (name, m, n, k, tm, tn, tk)
    ka0 = 0
    if a_cols is not None:
        assert mode == "nt" and a_planes == 1 and a_cols[0] % tk == 0
        ka0 = a_cols[0] // tk
    j0 = 0
    if b_cols is not None:
        assert mode == "tn" and b_planes == 1 and b_cols[0] % tn == 0 and b_cols[1] % tn == 0
        j0, n = b_cols[0] // tn, b_cols[1]
    nk = k // tk
    nkp, njp = nk // a_planes, n // tn // b_planes
    if a_planes > 1:
        a_spec = pl.BlockSpec((None, tm, tk), lambda i, j, kk: (kk // nkp, i, kk % nkp))
    elif mode == "tn":
        a_spec = pl.BlockSpec((tk, tm), lambda i, j, kk: (kk, i))
    else:
        a_spec = pl.BlockSpec((tm, tk), lambda i, j, kk: (i, kk + ka0))
    if b_planes > 1:
        b_spec = pl.BlockSpec((None, tk, tn), lambda i, j, kk: (j // njp, kk, j % njp))
    elif mode == "nt":
        b_spec = pl.BlockSpec((tn, tk), lambda i, j, kk: (j, kk))
    else:
        b_spec = pl.BlockSpec((tk, tn), lambda i, j, kk: (kk, j + j0))
    if stacked:
        assert residual is None
        o_spec = pl.BlockSpec((None, tm, tn), lambda i, j, kk: (j, i, 0))
        out_shape = jax.ShapeDtypeStruct((n // tn, m, tn), out_dtype)
    else:
        o_spec = pl.BlockSpec((tm, tn), lambda i, j, kk: (i, j))
        out_shape = jax.ShapeDtypeStruct((m, n), out_dtype)
    dims = _DIMS[mode]
    has_res = residual is not None
    has_rms = rms_gain is not None
    assert not has_rms or (tn == n and not stacked)
    n_in = 2 + has_res + has_rms + (dep is not None)

    def body(*refs):
        a_ref, b_ref, r_ref, o_ref = refs[0], refs[1], refs[2], refs[n_in]
        part = lax.dot_general(a_ref[...].astype(BF), b_ref[...].astype(BF), dims, preferred_element_type=F32)

        def finish(acc):
            if has_res:
                acc = r_ref[...] + acc
            o_ref[...] = acc.astype(out_dtype)
            if has_rms:
                r = lax.rsqrt(jnp.mean(acc * acc, axis=-1, keepdims=True) + EPS)
                refs[n_in + 1][...] = ((acc * r) * refs[2 + has_res][...]).astype(BF)

        if nk == 1:
            finish(part)
        else:
            acc_ref = refs[-1]
            kk = pl.program_id(2)

            @pl.when(kk == 0)
            def _():
                acc_ref[...] = part

            @pl.when(kk > 0)
            def _():
                acc_ref[...] += part

            @pl.when(kk == nk - 1)
            def _():
                finish(acc_ref[...])

    in_specs = [a_spec, b_spec] + ([o_spec] if has_res else [])
    args = (a, b) + ((residual,) if has_res else ())
    out_specs = o_spec
    if has_rms:
        in_specs.append(pl.BlockSpec((1, n), lambda i, j, kk: (0, 0)))
        args += (rms_gain,)
        out_specs = [o_spec, o_spec]
        out_shape = [out_shape, jax.ShapeDtypeStruct((m, n), BF)]
    if dep is not None:
        in_specs.append(pl.BlockSpec(memory_space=pl.ANY))
        args += (dep,)
    return pl.pallas_call(
        body,
        name=name,
        grid=(m // tm, n // tn, nk),
        in_specs=in_specs,
        out_specs=out_specs,
        out_shape=out_shape,
        scratch_shapes=[pltpu.VMEM((tm, tn), F32)] if nk > 1 else [],
        compiler_params=_cp("parallel", "parallel", "arbitrary"),
    )(*args)


def _rms_fwd(h, g, *, name, tm=512, dep=None):
    t, d = h.shape
    tm = min(tm, t)

    def body(*refs):
        h_ref, g_ref, u_ref = refs[0], refs[1], refs[-1]
        hv = h_ref[...]
        r = lax.rsqrt(jnp.mean(hv * hv, axis=-1, keepdims=True) + EPS)
        u_ref[...] = ((hv * r) * g_ref[...]).astype(BF)

    in_specs = [pl.BlockSpec((tm, d), lambda i: (i, 0)), pl.BlockSpec((1, d), lambda i: (0, 0))]
    args = (h, g)
    for one in () if dep is None else (dep if isinstance(dep, tuple) else (dep,)):
        in_specs.append(pl.BlockSpec(memory_space=pl.ANY))
        args += (one,)
    return pl.pallas_call(
        body,
        name=name,
        grid=(t // tm,),
        in_specs=in_specs,
        out_specs=pl.BlockSpec((tm, d), lambda i: (i, 0)),
        out_shape=jax.ShapeDtypeStruct((t, d), BF),
        compiler_params=_cp("parallel"),
    )(*args)


def _rms_bwd(du, h, g, dres, *, name, tm=512, du_w=None, dep=None):
    t, d = h.shape
    tm = min(tm, t)
    want_dh = dres is not None

    def body(*refs):
        du_ref, h_ref, g_ref, dg_ref = refs[0], refs[1], refs[2], refs[-1]
        if want_dh:
            dres_ref, dh_ref, dhb_ref = refs[3], refs[-3], refs[-2]
        i = pl.program_id(0)
        hv = h_ref[...]
        duv = du_ref[...]
        if du_w is not None:
            duv = lax.dot_general(duv, refs[3 + want_dh][...], _DIMS["nt"], preferred_element_type=F32)
        r = lax.rsqrt(jnp.mean(hv * hv, axis=-1, keepdims=True) + EPS)
        nv = hv * r
        if want_dh:
            gy = duv * g_ref[...]
            dh = dres_ref[...] + r * (gy - nv * jnp.mean(nv * gy, axis=-1, keepdims=True))
            dh_ref[...] = dh
            dhb_ref[...] = dh.astype(BF)

        @pl.when(i == 0)
        def _():
            dg_ref[...] = jnp.zeros_like(dg_ref)

        dg_ref[...] += jnp.sum(duv * nv, axis=0, keepdims=True)

    row = pl.BlockSpec((tm, d), lambda i: (i, 0))
    vec = pl.BlockSpec((1, d), lambda i: (0, 0))
    du_spec = row if du_w is None else pl.BlockSpec((tm, du.shape[1]), lambda i: (i, 0))
    if want_dh:
        in_specs, args = [du_spec, row, vec, row], (du, h, g, dres)
        out_specs = [row, row, vec]
        out_shape = [jax.ShapeDtypeStruct((t, d), F32), jax.ShapeDtypeStruct((t, d), BF),
                     jax.ShapeDtypeStruct((1, d), F32)]
    else:
        in_specs, args = [du_spec, row, vec], (du, h, g)
        out_specs = [vec]
        out_shape = [jax.ShapeDtypeStruct((1, d), F32)]
    if du_w is not None:
        in_specs.append(pl.BlockSpec(du_w.shape, lambda i: (0, 0)))
        args += (du_w,)
    if dep is not None:
        in_specs.append(pl.BlockSpec(memory_space=pl.ANY))
        args += (dep,)
    outs = pl.pallas_call(
        body,
        name=name,
        grid=(t // tm,),
        in_specs=in_specs,
        out_specs=out_specs,
        out_shape=out_shape,
        compiler_params=_cp("arbitrary"),
    )(*args)
    return (outs[0], outs[1], outs[2]) if want_dh else (None, None, outs[0])


def _loss_head(h, target, g, *, name, tm=512):
    t, d = h.shape
    tm = min(tm, t)

    def body(h_ref, t_ref, g_ref, dh_ref, dhb_ref, loss_ref, dg_ref):
        i = pl.program_id(0)
        hv = h_ref[...]
        gv = g_ref[...]
        r = lax.rsqrt(jnp.mean(hv * hv, axis=-1, keepdims=True) + EPS)
        nv = hv * r
        e = nv * gv - t_ref[...]
        per_tok = jnp.mean(e * e, axis=-1, keepdims=True)
        lp = 0.5 * jnp.sum(per_tok, axis=0, keepdims=True)
        dy = e * (1.0 / d)
        gy = dy * gv
        dh = r * (gy - nv * jnp.mean(nv * gy, axis=-1, keepdims=True))
        dh_ref[...] = dh
        dhb_ref[...] = dh.astype(BF)

        @pl.when(i == 0)
        def _():
            loss_ref[...] = jnp.zeros_like(loss_ref)
            dg_ref[...] = jnp.zeros_like(dg_ref)

        loss_ref[...] += jnp.broadcast_to(lp, loss_ref.shape)
        dg_ref[...] += jnp.sum(dy * nv, axis=0, keepdims=True)

    row = pl.BlockSpec((tm, d), lambda i: (i, 0))
    vec = pl.BlockSpec((1, d), lambda i: (0, 0))
    return pl.pallas_call(
        body,
        name=name,
        grid=(t // tm,),
        in_specs=[row, row, vec],
        out_specs=[row, row, pl.BlockSpec((SUBLANES, LANES), lambda i: (0, 0)), vec],
        out_shape=[
            jax.ShapeDtypeStruct((t, d), F32),
            jax.ShapeDtypeStruct((t, d), BF),
            jax.ShapeDtypeStruct((SUBLANES, LANES), F32),
            jax.ShapeDtypeStruct((1, d), F32),
        ],
        compiler_params=_cp("arbitrary"),
    )(h, target, g)


def _ffn_in_fwd(u, w, *, name, tm=1024, tn=512, dep=None):
    t, d = u.shape
    f = w.shape[1] // 2
    tm, tn = _fit(t, tm), _fit(f, tn)
    nf = f // tn

    def body(*refs):
        u_ref, wa_ref, wb_ref, hid_ref, act_ref = refs[0], refs[1], refs[2], refs[-2], refs[-1]
        uv = u_ref[...]
        a = jnp.dot(uv, wa_ref[...], preferred_element_type=F32)
        b = jnp.dot(uv, wb_ref[...], preferred_element_type=F32)
        hid_ref[0] = a.astype(BF)
        hid_ref[1] = b.astype(BF)
        act_ref[...] = ((a * _sigmoid(a)) * b).astype(BF)

    in_specs = [
        pl.BlockSpec((tm, d), lambda i, j: (i, 0)),
        pl.BlockSpec((d, tn), lambda i, j: (0, j)),
        pl.BlockSpec((d, tn), lambda i, j: (0, nf + j)),
    ]
    args = (u, w, w)
    if dep is not None:
        in_specs.append(pl.BlockSpec(memory_space=pl.ANY))
        args += (dep,)
    return pl.pallas_call(
        body,
        name=name,
        grid=(t // tm, nf),
        in_specs=in_specs,
        out_specs=[pl.BlockSpec((2, tm, tn), lambda i, j: (0, i, j)), pl.BlockSpec((tm, tn), lambda i, j: (i, j))],
        out_shape=[jax.ShapeDtypeStruct((2, t, f), BF), jax.ShapeDtypeStruct((t, f), BF)],
        compiler_params=_cp("parallel", "parallel"),
    )(*args)


def _ffn_out_bwd(dh, w_out, hid2, *, name, tm=1024, tn=512, dep=None):
    t, d = dh.shape
    f = w_out.shape[0]
    tm, tn = _fit(t, tm), _fit(f, tn)

    def body(*refs):
        dh_ref, w_ref, hid_ref, o_ref = refs[0], refs[1], refs[2], refs[-1]
        dact = lax.dot_general(dh_ref[...], w_ref[...], _DIMS["nt"], preferred_element_type=F32)
        a = hid_ref[0].astype(F32)
        b = hid_ref[1].astype(F32)
        sg = _sigmoid(a)
        o_ref[0] = (dact * b * (sg * (1.0 + a * (1.0 - sg)))).astype(BF)
        o_ref[1] = (dact * (a * sg)).astype(BF)

    pair = pl.BlockSpec((2, tm, tn), lambda i, j: (0, i, j))
    in_specs = [pl.BlockSpec((tm, d), lambda i, j: (i, 0)), pl.BlockSpec((tn, d), lambda i, j: (j, 0)), pair]
    args = (dh, w_out, hid2)
    if dep is not None:
        in_specs.append(pl.BlockSpec(memory_space=pl.ANY))
        args += (dep,)
    return pl.pallas_call(
        body,
        name=name,
        grid=(t // tm, f // tn),
        in_specs=in_specs,
        out_specs=pair,
        out_shape=jax.ShapeDtypeStruct((2, t, f), BF),
        compiler_params=_cp("parallel", "parallel"),
    )(*args)


def _gate_fwd(cy, w, proj, ya, *, ga0, gc0, name, tm=1024, tc=512):
    t, d = ya.shape
    kc = cy.shape[1]
    tm, tc = _fit(t, tm), math.gcd(tc, d, ga0, gc0)
    a0, c0 = ga0 // tc, gc0 // tc

    def body(cy_ref, w_ref, ga_ref, gc_ref, ya_ref, yc_ref, o_ref):
        yc = jnp.dot(cy_ref[...], w_ref[...], preferred_element_type=F32)
        yc_ref[...] = yc
        o_ref[...] = (_sigmoid(ga_ref[...]) * ya_ref[...] + _sigmoid(gc_ref[...]) * yc).astype(BF)

    blk = pl.BlockSpec((tm, tc), lambda i, j: (i, j))
    return pl.pallas_call(
        body,
        name=name,
        grid=(t // tm, d // tc),
        in_specs=[
            pl.BlockSpec((tm, kc), lambda i, j: (i, 0)),
            pl.BlockSpec((kc, tc), lambda i, j: (0, j)),
            pl.BlockSpec((tm, tc), lambda i, j: (i, a0 + j)),
            pl.BlockSpec((tm, tc), lambda i, j: (i, c0 + j)),
            blk,
        ],
        out_specs=[blk, blk],
        out_shape=[jax.ShapeDtypeStruct((t, d), F32), jax.ShapeDtypeStruct((t, d), BF)],
        compiler_params=_cp("parallel", "parallel"),
    )(cy, w, proj, proj, ya)


def _gate_bwd(dh, w, proj, ya, yc, *, ga0, gc0, name, tm=1024, tc=512):
    t, d = ya.shape
    tm, tc = _fit(t, tm), math.gcd(tc, d, ga0, gc0)
    a0, c0 = ga0 // tc, gc0 // tc

    def body(dh_ref, w_ref, ga_ref, gc_ref, ya_ref, yc_ref, dya_ref, dyc_ref, dga_ref, dgc_ref):
        dmv = lax.dot_general(dh_ref[...], w_ref[...], _DIMS["nt"], preferred_element_type=F32)
        sa = _sigmoid(ga_ref[...])
        sc = _sigmoid(gc_ref[...])
        dya_ref[...] = (dmv * sa).astype(BF)
        dyc_ref[...] = (dmv * sc).astype(BF)
        dga_ref[...] = (dmv * ya_ref[...] * (sa * (1.0 - sa))).astype(BF)
        dgc_ref[...] = (dmv * yc_ref[...] * (sc * (1.0 - sc))).astype(BF)

    blk = pl.BlockSpec((tm, tc), lambda i, j: (i, j))
    out = jax.ShapeDtypeStruct((t, d), BF)
    return pl.pallas_call(
        body,
        name=name,
        grid=(t // tm, d // tc),
        in_specs=[
            pl.BlockSpec((tm, d), lambda i, j: (i, 0)),
            pl.BlockSpec((tc, d), lambda i, j: (j, 0)),
            pl.BlockSpec((tm, tc), lambda i, j: (i, a0 + j)),
            pl.BlockSpec((tm, tc), lambda i, j: (i, c0 + j)),
            blk,
            blk,
        ],
        out_specs=[blk, blk, blk, blk],
        out_shape=[out, out, out, out],
        compiler_params=_cp("parallel", "parallel"),
    )(dh, w, proj, proj, ya, yc)


def _conv_taps(cz, czp, i):
    czp = czp * (i > 0).astype(F32)
    h1 = czp[SUBLANES - 1:SUBLANES, :]
    h2 = czp[SUBLANES - 2:SUBLANES - 1, :]
    row = lax.broadcasted_iota(jnp.int32, cz.shape, 0)
    s1 = jnp.where(row == 0, h1, pltpu.roll(cz, 1, 0))
    s2 = jnp.where(row == 0, h2, jnp.where(row == 1, h1, pltpu.roll(cz, 2, 0)))
    return s1, s2


def _conv_fwd(proj, w8, *, z0, gb0, gc0, cw, name, tm=1024, tc=512):
    t = proj.shape[0]
    tm, tc = min(tm, t), math.gcd(tc, cw, z0, gb0, gc0)
    zb, bb, cb = z0 // tc, gb0 // tc, gc0 // tc
    rb = tm // SUBLANES

    def body(z_ref, gb_ref, gc_ref, zp_ref, gcp_ref, w_ref, o_ref):
        i = pl.program_id(0)
        cz = gc_ref[...] * z_ref[...]
        s1, s2 = _conv_taps(cz, gcp_ref[...] * zp_ref[...], i)
        w = w_ref[...]
        y = w[0:1, :] * s2 + w[1:2, :] * s1 + w[2:3, :] * cz
        o_ref[...] = (gb_ref[...] * y).astype(BF)

    def cur(b0):
        return pl.BlockSpec((tm, tc), lambda i, j: (i, b0 + j))

    def prev(b0):
        return pl.BlockSpec((SUBLANES, tc), lambda i, j: (jnp.maximum(i * rb - 1, 0), b0 + j))

    return pl.pallas_call(
        body,
        name=name,
        grid=(t // tm, cw // tc),
        in_specs=[cur(zb), cur(bb), cur(cb), prev(zb), prev(cb), pl.BlockSpec((SUBLANES, tc), lambda i, j: (0, j))],
        out_specs=pl.BlockSpec((tm, tc), lambda i, j: (i, j)),
        out_shape=jax.ShapeDtypeStruct((t, cw), BF),
        compiler_params=_cp("parallel", "parallel"),
    )(proj, proj, proj, proj, proj, w8)


def _conv_bwd(proj, w8, dcy, *, z0, gb0, gc0, cw, name, tm=1024, tc=512):
    t = proj.shape[0]
    tm, tc = min(tm, t), math.gcd(tc, cw, z0, gb0, gc0)
    zb, bb, cb = z0 // tc, gb0 // tc, gc0 // tc
    rb = tm // SUBLANES
    nt = t // tm

    def body(z_ref, gb_ref, gc_ref, zp_ref, gcp_ref, d_ref, dn_ref, gbn_ref, w_ref, dz_ref, dgb_ref, dgc_ref, dw_ref):
        i = pl.program_id(1)
        z = z_ref[...]
        gc = gc_ref[...]
        gb = gb_ref[...]
        cz = gc * z
        s1, s2 = _conv_taps(cz, gcp_ref[...] * zp_ref[...], i)
        w = w_ref[...]
        w0, w1, w2 = w[0:1, :], w[1:2, :], w[2:3, :]
        yc = w0 * s2 + w1 * s1 + w2 * cz
        dcyv = d_ref[...]
        dgb_ref[...] = (dcyv * yc).astype(BF)
        dyc = dcyv * gb
        dycn = dn_ref[...] * gbn_ref[...] * (i < nt - 1).astype(F32)
        n1, n2 = dycn[0:1, :], dycn[1:2, :]
        row = lax.broadcasted_iota(jnp.int32, cz.shape, 0)
        a1 = jnp.where(row == tm - 1, n1, pltpu.roll(dyc, tm - 1, 0))
        a2 = jnp.where(row == tm - 1, n2, jnp.where(row == tm - 2, n1, pltpu.roll(dyc, tm - 2, 0)))
        dcz = w2 * dyc + w1 * a1 + w0 * a2
        dz_ref[...] = (dcz * gc).astype(BF)
        dgc_ref[...] = (dcz * z).astype(BF)
        dw0 = jnp.sum(dyc * s2, axis=0, keepdims=True)
        dw1 = jnp.sum(dyc * s1, axis=0, keepdims=True)
        dw2 = jnp.sum(dyc * cz, axis=0, keepdims=True)
        r8 = lax.broadcasted_iota(jnp.int32, (SUBLANES, tc), 0)
        upd = jnp.where(r8 == 0, dw0, jnp.where(r8 == 1, dw1, jnp.where(r8 == 2, dw2, 0.0)))

        @pl.when(i == 0)
        def _():
            dw_ref[...] = jnp.zeros_like(dw_ref)

        dw_ref[...] += upd

    def cur(b0):
        return pl.BlockSpec((tm, tc), lambda j, i: (i, b0 + j))

    def prev(b0):
        return pl.BlockSpec((SUBLANES, tc), lambda j, i: (jnp.maximum(i * rb - 1, 0), b0 + j))

    def nxt(b0):
        return pl.BlockSpec((SUBLANES, tc), lambda j, i: (jnp.minimum((i + 1) * rb, t // SUBLANES - 1), b0 + j))

    blk = pl.BlockSpec((tm, tc), lambda j, i: (i, j))
    w_spec = pl.BlockSpec((SUBLANES, tc), lambda j, i: (0, j))
    out = jax.ShapeDtypeStruct((t, cw), BF)
    return pl.pallas_call(
        body,
        name=name,
        grid=(cw // tc, nt),
        in_specs=[cur(zb), cur(bb), cur(cb), prev(zb), prev(cb), blk, nxt(0), nxt(bb), w_spec],
        out_specs=[blk, blk, blk, w_spec],
        out_shape=[out, out, out, jax.ShapeDtypeStruct((SUBLANES, cw), F32)],
        compiler_params=_cp("parallel", "arbitrary"),
    )(proj, proj, proj, proj, proj, dcy, dcy, proj, w8)


def _rot_half(x):
    lane = lax.broadcasted_iota(jnp.int32, x.shape, 1)
    first = (lane % HEAD_DIM) < (HEAD_DIM // 2)
    return jnp.where(first, pltpu.roll(x, LANES - HEAD_DIM // 2, 1), pltpu.roll(x, HEAD_DIM // 2, 1))


def _rope(x, c, s):
    parts = []
    for a in range(x.shape[1] // LANES):
        xa = x[:, a * LANES:(a + 1) * LANES]
        parts.append(xa * c + _rot_half(xa) * s)
    return parts[0] if len(parts) == 1 else jnp.concatenate(parts, axis=1)


def _rope_bwd(dy, c, s):
    parts = []
    for a in range(dy.shape[1] // LANES):
        da = dy[:, a * LANES:(a + 1) * LANES]
        parts.append(da * c + _rot_half(da * s))
    return parts[0] if len(parts) == 1 else jnp.concatenate(parts, axis=1)


def _window(i):
    b = WINDOW
    r = lax.broadcasted_iota(jnp.int32, (b, b), 0)
    c = lax.broadcasted_iota(jnp.int32, (b, b), 1)
    return c <= r, c <= r + jnp.where(i > 0, b, 0)


def _band_pick(x, tri):
    b = tri.shape[0]
    return jnp.where(tri, x[:, b:], x[:, :b])


def _band_spread(y, tri):
    return jnp.concatenate([jnp.where(tri, 0.0, y), jnp.where(tri, y, 0.0)], axis=1)


def _chunk(x, a):
    return x[:, a * LANES:(a + 1) * LANES]


def _kv_aligned(kp, kc, h):
    band = jnp.concatenate([_chunk(kp, h // 2), _chunk(kc, h // 2)], axis=0).astype(F32)
    swapped = pltpu.roll(band, HEAD_DIM, 1)
    return (band, swapped) if h % 2 == 0 else (swapped, band)


def _swa_fwd(proj, cosf, sinf, sinks, *, nq, name, dep=None):
    t = proj.shape[0]
    nkv = nq // Q_PER_KV
    aw, kw, b = nq * HEAD_DIM, nkv * HEAD_DIM, WINDOW
    nb = t // b
    kblk = aw // kw
    scale = HEAD_DIM ** -0.5

    def body(*refs):
        sink_ref, q_ref, kc_ref, kp_ref, vc_ref, vp_ref, cc_ref, cp_ref, sc_ref, sp_ref = refs[:10]
        o_ref, qr_ref, kr_ref, s_scr, p_scr = refs[-5:]
        i = pl.program_id(0)
        cc, sc, cpv, spv = cc_ref[...], sc_ref[...], cp_ref[...], sp_ref[...]
        qr = _rope(q_ref[...], cc, sc)
        kc = _rope(kc_ref[...], cc, sc)
        kp = _rope(kp_ref[...], cpv, spv)
        qr_ref[...] = qr.astype(BF)
        kr_ref[...] = kc.astype(BF)
        vc, vp = vc_ref[...], vp_ref[...]
        tri, ok = _window(i)
        lo = lax.broadcasted_iota(jnp.int32, (b, LANES), 1) < HEAD_DIM
        ks = [[x.astype(BF) for x in _kv_aligned(kp, kc, h)] for h in range(nkv)]
        vs = [[x.astype(BF) for x in _kv_aligned(vp, vc, h)] for h in range(nkv)]
        for hq in range(nq):
            a, par = hq // 2, hq % 2
            qm = jnp.where(lo if par == 0 else ~lo, _chunk(qr, a), 0.0).astype(BF)
            s_scr[hq] = _band_pick(
                lax.dot_general(qm, ks[hq // Q_PER_KV][par], _DIMS["nt"], preferred_element_type=F32), tri)
        for hq in range(nq):
            s = jnp.where(ok, s_scr[hq] * scale, -jnp.inf)
            sink = sink_ref[hq]
            m = jnp.maximum(jnp.max(s, axis=-1, keepdims=True), sink)
            p = jnp.exp(s - m)
            p = p / (jnp.sum(p, axis=-1, keepdims=True) + jnp.exp(sink - m))
            p_scr[hq] = _band_spread(p, tri).astype(BF)
        for a in range(nq // 2):
            o_par = [jnp.dot(p_scr[2 * a + par], vs[(2 * a) // Q_PER_KV][par], preferred_element_type=F32)
                     for par in range(2)]
            o_ref[:, a * LANES:(a + 1) * LANES] = jnp.where(lo, o_par[0], o_par[1]).astype(BF)

    def prev_i(i):
        return jnp.maximum(i - 1, 0)

    tab_c = pl.BlockSpec((b, LANES), lambda i: (i, 0))
    tab_p = pl.BlockSpec((b, LANES), lambda i: (prev_i(i), 0))
    in_specs = [
        pl.BlockSpec(memory_space=pltpu.SMEM),
        pl.BlockSpec((b, aw), lambda i: (i, 0)),
        pl.BlockSpec((b, kw), lambda i: (i, kblk)),
        pl.BlockSpec((b, kw), lambda i: (prev_i(i), kblk)),
        pl.BlockSpec((b, kw), lambda i: (i, kblk + 1)),
        pl.BlockSpec((b, kw), lambda i: (prev_i(i), kblk + 1)),
        tab_c,
        tab_p,
        tab_c,
        tab_p,
    ]
    args = (sinks, proj, proj, proj, proj, proj, cosf, cosf, sinf, sinf)
    if dep is not None:
        in_specs.append(pl.BlockSpec(memory_space=pl.ANY))
        args += (dep,)
    return pl.pallas_call(
        body,
        name=name,
        grid=(nb,),
        in_specs=in_specs,
        out_specs=[
            pl.BlockSpec((b, aw), lambda i: (i, 0)),
            pl.BlockSpec((b, aw), lambda i: (i, 0)),
            pl.BlockSpec((b, kw), lambda i: (i, 0)),
        ],
        out_shape=[
            jax.ShapeDtypeStruct((t, aw), BF),
            jax.ShapeDtypeStruct((t, aw), BF),
            jax.ShapeDtypeStruct((t, kw), BF),
        ],
        scratch_shapes=[pltpu.VMEM((nq, b, b), F32), pltpu.VMEM((nq, b, 2 * b), BF)],
        compiler_params=_cp("parallel"),
    )(*args)


def _swa_bwd(qr, kr, proj, do, cosf, sinf, sinks, *, nq, name):
    t = proj.shape[0]
    nkv = nq // Q_PER_KV
    aw, kw, b = nq * HEAD_DIM, nkv * HEAD_DIM, WINDOW
    nb = t // b
    kblk = aw // kw
    scale = HEAD_DIM ** -0.5

    def body(sink_ref, q_ref, kc_ref, kp_ref, vc_ref, vp_ref, do_ref, cc_ref, cp_ref, sc_ref, sp_ref,
             dq_ref, dk_ref, dv_ref, ds_ref, ck_ref, cv_ref, sacc_ref, s_scr, dp_scr, ds_scr, pf_scr):
        i = pl.program_id(0)

        @pl.when(i == 0)
        def _():
            ck_ref[...] = jnp.zeros_like(ck_ref)
            cv_ref[...] = jnp.zeros_like(cv_ref)
            sacc_ref[...] = jnp.zeros_like(sacc_ref)

        @pl.when(i < nb)
        def _():
            q = q_ref[...]
            kc, kp = kc_ref[...], kp_ref[...]
            vc, vp = vc_ref[...], vp_ref[...]
            dov = do_ref[...]
            tri, ok = _window(i)
            lane = lax.broadcasted_iota(jnp.int32, (b, LANES), 1)
            lo = lane < HEAD_DIM
            cc, sc = cc_ref[...], sc_ref[...]
            nch = kw // LANES
            row_lo = lax.broadcasted_iota(jnp.int32, (LANES, b), 0) < HEAD_DIM
            dk_ch = [jnp.zeros((LANES, 2 * b), F32) for _ in range(nch)]
            dv_ch = [jnp.zeros((LANES, 2 * b), F32) for _ in range(nch)]
            sacc = jnp.zeros((b, LANES), F32)
            ks = [[x.astype(BF) for x in _kv_aligned(kp, kc, h)] for h in range(nkv)]
            vs = [[x.astype(BF) for x in _kv_aligned(vp, vc, h)] for h in range(nkv)]
            for hq in range(nq):
                a, par, h = hq // 2, hq % 2, hq // Q_PER_KV
                mine = lo if par == 0 else ~lo
                qm = jnp.where(mine, _chunk(q, a).astype(F32), 0.0).astype(BF)
                dom = jnp.where(mine, _chunk(dov, a).astype(F32), 0.0).astype(BF)
                s_scr[hq] = _band_pick(lax.dot_general(qm, ks[h][par], _DIMS["nt"], preferred_element_type=F32), tri)
                dp_scr[hq] = _band_pick(
                    lax.dot_general(dom, vs[h][par], _DIMS["nt"], preferred_element_type=F32), tri)
            for hq in range(nq):
                s = jnp.where(ok, s_scr[hq] * scale, -jnp.inf)
                sink = sink_ref[hq]
                m = jnp.maximum(jnp.max(s, axis=-1, keepdims=True), sink)
                e = jnp.exp(s - m)
                es = jnp.exp(sink - m)
                zinv = 1.0 / (jnp.sum(e, axis=-1, keepdims=True) + es)
                p = e * zinv
                dp = dp_scr[hq]
                delta = jnp.sum(p * dp, axis=-1, keepdims=True)
                ds_scr[hq] = _band_spread(p * (dp - delta) * scale, tri).astype(BF)
                pf_scr[hq] = _band_spread(p, tri).astype(BF)
                sacc = sacc + jnp.where(lane == hq, -(es * zinv) * delta, 0.0)
            for a in range(nq // 2):
                h = (2 * a) // Q_PER_KV
                qa_t = _chunk(q, a).astype(F32).T
                doa_t = _chunk(dov, a).astype(F32).T
                dq_par = []
                for par in range(2):
                    hq = 2 * a + par
                    mine_t = row_lo if par == 0 else ~row_lo
                    qm_t = jnp.where(mine_t, qa_t, 0.0).astype(BF)
                    dom_t = jnp.where(mine_t, doa_t, 0.0).astype(BF)
                    dsv = ds_scr[hq]
                    dq_par.append(jnp.dot(dsv, ks[h][par], preferred_element_type=F32))
                    dkh = jnp.dot(qm_t, dsv, preferred_element_type=F32)
                    dvh = jnp.dot(dom_t, pf_scr[hq], preferred_element_type=F32)
                    if par != h % 2:
                        dkh = pltpu.roll(dkh, HEAD_DIM, 0)
                        dvh = pltpu.roll(dvh, HEAD_DIM, 0)
                    dk_ch[h // 2] = dk_ch[h // 2] + dkh
                    dv_ch[h // 2] = dv_ch[h // 2] + dvh
                dqa = jnp.where(lo, dq_par[0], dq_par[1])
                dq_ref[:, a * LANES:(a + 1) * LANES] = _rope_bwd(dqa, cc, sc).astype(BF)
            dk_ch = [x.T for x in dk_ch]
            dv_ch = [x.T for x in dv_ch]
            dk = dk_ch[0] if nch == 1 else jnp.concatenate(dk_ch, axis=1)
            dv = dv_ch[0] if nch == 1 else jnp.concatenate(dv_ch, axis=1)
            dk_ref[...] = _rope_bwd(ck_ref[...] + dk[:b, :], cp_ref[...], sp_ref[...]).astype(BF)
            dv_ref[...] = (cv_ref[...] + dv[:b, :]).astype(BF)
            ck_ref[...] = dk[b:, :]
            cv_ref[...] = dv[b:, :]
            sacc_ref[...] += sacc

        @pl.when(i == nb)
        def _():
            dk_ref[...] = _rope_bwd(ck_ref[...], cp_ref[...], sp_ref[...]).astype(BF)
            dv_ref[...] = cv_ref[...].astype(BF)
            ds_ref[...] = jnp.broadcast_to(jnp.sum(sacc_ref[...], axis=0, keepdims=True), ds_ref.shape)

    def cur_i(i):
        return jnp.minimum(i, nb - 1)

    def prev_i(i):
        return jnp.clip(i - 1, 0, nb - 1)

    tab_c = pl.BlockSpec((b, LANES), lambda i: (cur_i(i), 0))
    tab_p = pl.BlockSpec((b, LANES), lambda i: (prev_i(i), 0))
    return pl.pallas_call(
        body,
        name=name,
        grid=(nb + 1,),
        in_specs=[
            pl.BlockSpec(memory_space=pltpu.SMEM),
            pl.BlockSpec((b, aw), lambda i: (cur_i(i), 0)),
            pl.BlockSpec((b, kw), lambda i: (cur_i(i), 0)),
            pl.BlockSpec((b, kw), lambda i: (prev_i(i), 0)),
            pl.BlockSpec((b, kw), lambda i: (cur_i(i), kblk + 1)),
            pl.BlockSpec((b, kw), lambda i: (prev_i(i), kblk + 1)),
            pl.BlockSpec((b, aw), lambda i: (cur_i(i), 0)),
            tab_c,
            tab_p,
            tab_c,
            tab_p,
        ],
        out_specs=[
            pl.BlockSpec((b, aw), lambda i: (cur_i(i), 0)),
            pl.BlockSpec((b, kw), lambda i: (prev_i(i), 0)),
            pl.BlockSpec((b, kw), lambda i: (prev_i(i), 0)),
            pl.BlockSpec((SUBLANES, LANES), lambda i: (0, 0)),
        ],
        out_shape=[
            jax.ShapeDtypeStruct((t, aw), BF),
            jax.ShapeDtypeStruct((t, kw), BF),
            jax.ShapeDtypeStruct((t, kw), BF),
            jax.ShapeDtypeStruct((SUBLANES, LANES), F32),
        ],
        scratch_shapes=[pltpu.VMEM((b, kw), F32), pltpu.VMEM((b, kw), F32), pltpu.VMEM((b, LANES), F32),
                        pltpu.VMEM((nq, b, b), F32), pltpu.VMEM((nq, b, b), F32),
                        pltpu.VMEM((nq, b, 2 * b), BF), pltpu.VMEM((nq, b, 2 * b), BF)],
        compiler_params=_cp("arbitrary"),
    )(sinks, qr, kr, kr, proj, proj, do, cosf, cosf, sinf, sinf)


def _xattn_fwd(xq, kv, *, name, tq=512):
    t, xw = xq.shape
    mtok = kv.shape[0]
    tq = min(tq, t)
    nh = xw // X_HEAD_DIM
    scale = X_HEAD_DIM ** -0.5

    def body(q_ref, kv_ref, o_ref):
        q = q_ref[...]
        kvv = kv_ref[...]
        outs = []
        for h in range(nh):
            sl = slice(h * X_HEAD_DIM, (h + 1) * X_HEAD_DIM)
            k = kvv[:, sl]
            v = kvv[:, xw + h * X_HEAD_DIM: xw + (h + 1) * X_HEAD_DIM]
            s = lax.dot_general(q[:, sl], k, _DIMS["nt"], preferred_element_type=F32) * scale
            e = jnp.exp(s - jnp.max(s, axis=-1, keepdims=True))
            p = e / jnp.sum(e, axis=-1, keepdims=True)
            outs.append(jnp.dot(p.astype(BF), v, preferred_element_type=F32))
        o_ref[...] = jnp.concatenate(outs, axis=1).astype(BF)

    return pl.pallas_call(
        body,
        name=name,
        grid=(t // tq,),
        in_specs=[pl.BlockSpec((tq, xw), lambda i: (i, 0)), pl.BlockSpec((mtok, 2 * xw), lambda i: (0, 0))],
        out_specs=pl.BlockSpec((tq, xw), lambda i: (i, 0)),
        out_shape=jax.ShapeDtypeStruct((t, xw), BF),
        compiler_params=_cp("parallel"),
    )(xq, kv)


def _xattn_bwd(xq, kv, do, *, name, tq=512):
    t, xw = xq.shape
    mtok = kv.shape[0]
    tq = min(tq, t)
    nh = xw // X_HEAD_DIM
    scale = X_HEAD_DIM ** -0.5

    def body(q_ref, kv_ref, do_ref, dq_ref, dkv_ref):
        i = pl.program_id(0)
        q = q_ref[...]
        kvv = kv_ref[...]
        dov = do_ref[...]
        dqs, dks, dvs = [], [], []
        for h in range(nh):
            sl = slice(h * X_HEAD_DIM, (h + 1) * X_HEAD_DIM)
            k = kvv[:, sl]
            v = kvv[:, xw + h * X_HEAD_DIM: xw + (h + 1) * X_HEAD_DIM]
            qh, doh = q[:, sl], dov[:, sl]
            s = lax.dot_general(qh, k, _DIMS["nt"], preferred_element_type=F32) * scale
            e = jnp.exp(s - jnp.max(s, axis=-1, keepdims=True))
            p = e / jnp.sum(e, axis=-1, keepdims=True)
            dp = lax.dot_general(doh, v, _DIMS["nt"], preferred_element_type=F32)
            delta = jnp.sum(p * dp, axis=-1, keepdims=True)
            dsv = (p * (dp - delta) * scale).astype(BF)
            dqs.append(jnp.dot(dsv, k, preferred_element_type=F32))
            dks.append(lax.dot_general(dsv, qh, _DIMS["tn"], preferred_element_type=F32))
            dvs.append(lax.dot_general(p.astype(BF), doh, _DIMS["tn"], preferred_element_type=F32))
        dq_ref[...] = jnp.concatenate(dqs, axis=1).astype(BF)

        @pl.when(i == 0)
        def _():
            dkv_ref[...] = jnp.zeros_like(dkv_ref)

        dkv_ref[...] += jnp.concatenate(dks + dvs, axis=1)

    row = pl.BlockSpec((tq, xw), lambda i: (i, 0))
    full = pl.BlockSpec((mtok, 2 * xw), lambda i: (0, 0))
    return pl.pallas_call(
        body,
        name=name,
        grid=(t // tq,),
        in_specs=[row, full, row],
        out_specs=[row, full],
        out_shape=[jax.ShapeDtypeStruct((t, xw), BF), jax.ShapeDtypeStruct((mtok, 2 * xw), F32)],
        compiler_params=_cp("arbitrary"),
    )(xq, kv, do)


def _adam_math(w, g, m, v):
    m = ADAM_B1 * m + (1.0 - ADAM_B1) * g
    v = ADAM_B2 * v + (1.0 - ADAM_B2) * (g * g)
    m_hat = m / (1.0 - ADAM_B1 ** ADAM_STEP)
    v_hat = v / (1.0 - ADAM_B2 ** ADAM_STEP)
    delta = -ADAM_LR * (m_hat / (jnp.sqrt(v_hat) + ADAM_EPS) + ADAM_WD * w)
    return delta, m, v


def _row_tile(r, c, n_arrays, budget=24 * 1024 * 1024):
    step = 2 * SUBLANES
    cap = max(step, budget // (2 * n_arrays * c * 4))
    if r <= cap:
        return r
    best = None
    for tr in range(step, cap + 1, step):
        if r % tr == 0:
            best = tr
    assert best is not None, (r, c)
    return best


def _adamw_sum(parts, own, me, w, m, v, *, name):
    _, r, c = parts.shape
    tr = _row_tile(r, c, 12)

    def body(me_ref, p_ref, own_ref, w_ref, m_ref, v_ref, g_ref, d_ref, nm_ref, nv_ref):
        mine = jnp.full((tr, c), me_ref[0], jnp.int32)
        g = None
        for s in range(N_DEV):
            term = jnp.where(mine == s, own_ref[...], p_ref[s]).astype(F32)
            g = term if g is None else g + term
        g_ref[...] = g
        d_ref[...], nm_ref[...], nv_ref[...] = _adam_math(w_ref[...], g, m_ref[...], v_ref[...])

    blk = pl.BlockSpec((tr, c), lambda i, me_ref: (i, 0))
    out = jax.ShapeDtypeStruct((r, c), F32)
    return pl.pallas_call(
        body,
        name=name,
        grid_spec=pltpu.PrefetchScalarGridSpec(
            num_scalar_prefetch=1,
            grid=(r // tr,),
            in_specs=[
                pl.BlockSpec((N_DEV, tr, c), lambda i, me_ref: (0, i, 0)),
                pl.BlockSpec((None, tr, c), lambda i, me_ref: (me_ref[0], i, 0)),
                blk, blk, blk,
            ],
            out_specs=[blk, blk, blk, blk],
        ),
        out_shape=[out, out, out, out],
        compiler_params=_cp("parallel"),
    )(me, parts, own, w, m, v)


def _adamw_sum_pieces(parts, owns, me, w, m, v, *, name):
    r = w.shape[0]
    widths = [p.shape[2] for p in parts]
    c = sum(widths)
    k = len(parts)
    tr = _row_tile(r, c, 12)

    def body(me_ref, *refs):
        p_refs, own_refs = refs[:k], refs[k:2 * k]
        w_ref, m_ref, v_ref, g_ref, d_ref, nm_ref, nv_ref = refs[2 * k:]
        off = 0
        for p_ref, own_ref, ck in zip(p_refs, own_refs, widths):
            mine = jnp.full((tr, ck), me_ref[0], jnp.int32)
            g = None
            for s in range(N_DEV):
                term = jnp.where(mine == s, own_ref[...], p_ref[s]).astype(F32)
                g = term if g is None else g + term
            sl = slice(off, off + ck)
            g_ref[:, sl] = g
            d_ref[:, sl], nm_ref[:, sl], nv_ref[:, sl] = _adam_math(w_ref[:, sl], g, m_ref[:, sl], v_ref[:, sl])
            off += ck

    blk = pl.BlockSpec((tr, c), lambda i, me_ref: (i, 0))
    out = jax.ShapeDtypeStruct((r, c), F32)
    in_specs = [pl.BlockSpec((N_DEV, tr, ck), lambda i, me_ref: (0, i, 0)) for ck in widths]
    in_specs += [pl.BlockSpec((None, tr, ck), lambda i, me_ref: (me_ref[0], i, 0)) for ck in widths]
    return pl.pallas_call(
        body,
        name=name,
        grid_spec=pltpu.PrefetchScalarGridSpec(
            num_scalar_prefetch=1,
            grid=(r // tr,),
            in_specs=in_specs + [blk, blk, blk],
            out_specs=[blk, blk, blk, blk],
        ),
        out_shape=[out, out, out, out],
        compiler_params=_cp("parallel"),
    )(me, *parts, *owns, w, m, v)


def _to_bf16(a, *, name, dep=None):
    r, c = a.shape
    tr = _row_tile(r, c, 2)

    def body(*refs):
        refs[-1][...] = refs[0][...].astype(BF)

    blk = pl.BlockSpec((tr, c), lambda i: (i, 0))
    in_specs, args = [blk], (a,)
    if dep is not None:
        in_specs.append(pl.BlockSpec(memory_space=pl.ANY))
        args += (dep,)
    return pl.pallas_call(
        body,
        name=name,
        grid=(r // tr,),
        in_specs=in_specs,
        out_specs=blk,
        out_shape=jax.ShapeDtypeStruct((r, c), BF),
        compiler_params=_cp("parallel"),
    )(*args)


def _adamw_small(w, g, m, v, *, name):
    def body(w_ref, g_ref, m_ref, v_ref, d_ref, nm_ref, nv_ref):
        d_ref[...], nm_ref[...], nv_ref[...] = _adam_math(w_ref[...], g_ref[...], m_ref[...], v_ref[...])

    out = jax.ShapeDtypeStruct(w.shape, F32)
    return pl.pallas_call(body, name=name, out_shape=[out, out, out])(w, g, m, v)


def _mesh_pos():
    x, y, c = lax.axis_index("x"), lax.axis_index("y"), lax.axis_index("c")
    return x, y, c


def _peer(x, y, c, mask):
    px = 1 - x if mask & 4 else x
    py = 1 - y if mask & 2 else y
    pc = 1 - c if mask & 1 else c
    return (px, py, pc), 4 * px + 2 * py + pc


_HBM = pl.BlockSpec(memory_space=pltpu.HBM)
_SEM = pl.BlockSpec(memory_space=pltpu.SEMAPHORE)
_EFFECT = pltpu.SideEffectType.DATAFLOW_SIDE_EFFECTING


def _me():
    return 4 * lax.axis_index("x") + 2 * lax.axis_index("y") + lax.axis_index("c")


def _landing(own, me):
    land = lax.empty((N_DEV,) + own.shape, own.dtype)
    return lax.dynamic_update_slice(land, own[None], (me, 0, 0))


_ALL = tuple(range(1, N_DEV))
_CHIPS = (2, 4, 6)
SCATTER_DIRECT = tuple((m, m, 0, m) for m in _ALL)
GATHER_CHIPS = tuple((m, None, 0, m) for m in (1,) + _CHIPS)
GATHER_SIBLING = tuple((1, m, m, m ^ 1) for m in _CHIPS)


def _copy(src, land, send_sem, recv_sem, sem, x, y, c, entry, arriving):
    to, src_m, dst_m, arr_m = entry
    peer, _ = _peer(x, y, c, to)
    blk = lambda m: _peer(x, y, c, m)[1]
    return pltpu.make_async_remote_copy(
        src_ref=src if src_m is None else src.at[blk(src_m)],
        dst_ref=land.at[blk(arr_m if arriving else dst_m)],
        send_sem=send_sem.at[sem], recv_sem=recv_sem.at[sem], device_id=peer, device_id_type=MESH)


def _exchange_start(groups, plan, *, name, after=None):
    flat = [p for g in groups for p in g]
    from_land = flat[0][0] is None
    n, ng, nc = len(flat), len(groups), len(plan)
    n_buf = n if from_land else 2 * n

    def body(*refs):
        lands = refs[:n] if from_land else refs[n:2 * n]
        srcs = lands if from_land else refs[:n]
        sems = refs[n_buf + (after is not None):n_buf + (after is not None) + 2 * ng]
        token = refs[-1]
        x, y, c = _mesh_pos()
        w = 0
        for gi, g in enumerate(groups):
            for wi in range(len(g)):
                for k, entry in enumerate(plan):
                    _copy(srcs[w], lands[w], sems[2 * gi], sems[2 * gi + 1], wi * nc + k, x, y, c, entry,
                          False).start()
                w += 1
        token[...] = jnp.zeros_like(token)

    sem_shapes = []
    for g in groups:
        sem_shapes += [pltpu.SemaphoreType.DMA((len(g) * nc,))] * 2
    args = [] if from_land else [pltpu.with_memory_space_constraint(s, pltpu.HBM) for s, _ in flat]
    args += [pltpu.with_memory_space_constraint(l, pltpu.HBM) for _, l in flat]
    extra = [] if after is None else [after]
    outs = pl.pallas_call(
        body,
        name=name,
        in_specs=[_HBM] * n_buf + [pl.BlockSpec(memory_space=pl.ANY)] * len(extra),
        out_specs=[_SEM] * (2 * ng) + [_HBM] * n_buf + [pl.BlockSpec(memory_space=pltpu.VMEM)],
        out_shape=sem_shapes + [pltpu.HBM(a.shape, a.dtype) for a in args]
        + [jax.ShapeDtypeStruct((SUBLANES, LANES), F32)],
        input_output_aliases={i: 2 * ng + i for i in range(n_buf)},
        compiler_params=pltpu.CompilerParams(has_side_effects=_EFFECT),
    )(*args, *extra)
    sems, thru, token = outs[:2 * ng], outs[2 * ng:2 * ng + n_buf], outs[-1]
    res, w = [], 0
    for gi, g in enumerate(groups):
        m = len(g)
        srcs = [None] * m if from_land else list(thru[w:w + m])
        lands = list(thru[w:w + m]) if from_land else list(thru[n + w:n + w + m])
        res.append((sems[2 * gi], sems[2 * gi + 1], srcs, lands))
        w += m
    return res, token


def _exchange_wait(group, plan, after, *, name):
    send_sems, recv_sems, srcs_in, lands_in = group
    n, nc = len(lands_in), len(plan)
    from_land = srcs_in[0] is None
    n_buf = n if from_land else 2 * n

    def body(*refs):
        lands = refs[:n] if from_land else refs[n:2 * n]
        srcs = lands if from_land else refs[:n]
        send_sem, recv_sem = refs[n_buf], refs[n_buf + 1]
        x, y, c = _mesh_pos()
        for w in range(n):
            for k, entry in enumerate(plan):
                cp = _copy(srcs[w], lands[w], send_sem, recv_sem, w * nc + k, x, y, c, entry, True)
                cp.wait_send()
                cp.wait_recv()

    bufs = lands_in if from_land else srcs_in + lands_in
    outs = pl.pallas_call(
        body,
        name=name,
        in_specs=[_HBM] * n_buf + [_SEM, _SEM, pl.BlockSpec(memory_space=pl.ANY)],
        out_specs=[_HBM] * n_buf,
        out_shape=[pltpu.HBM(a.shape, a.dtype) for a in bufs],
        input_output_aliases={i: i for i in range(n_buf)},
        compiler_params=pltpu.CompilerParams(has_side_effects=_EFFECT),
    )(*bufs, send_sems, recv_sems, after)
    if from_land:
        return [None] * n, list(outs)
    return list(outs[:n]), list(outs[n:])


def _all_reduce_small(parts, rows, width, *, name, dep=None):
    n = len(parts)

    def body(*refs):
        ins = refs[:n]
        o_ref, pack_ref, buf_ref, send_sems, recv_sems = refs[-5:]
        x, y, c_ = _mesh_pos()
        me = 4 * x + 2 * y + c_
        pack_ref[...] = jnp.zeros_like(pack_ref)
        for ref, (arr, r0, nr) in zip(ins, parts):
            pack_ref[r0:r0 + nr, 0:arr.shape[1]] = ref[0:nr, :]
        sends, recvs = [], []
        for k in range(N_DEV - 1):
            peer, pidx = _peer(x, y, c_, k + 1)
            cp = pltpu.make_async_remote_copy(
                src_ref=pack_ref, dst_ref=buf_ref.at[me], send_sem=send_sems.at[k], recv_sem=recv_sems.at[k],
                device_id=peer, device_id_type=MESH)
            cp.start()
            sends.append(cp)
            recvs.append(pltpu.make_async_remote_copy(
                src_ref=pack_ref, dst_ref=buf_ref.at[pidx], send_sem=send_sems.at[k], recv_sem=recv_sems.at[k],
                device_id=peer, device_id_type=MESH))
        buf_ref[me] = pack_ref[...]
        for rc in recvs:
            rc.wait_recv()
        for cp in sends:
            cp.wait_send()
        acc = buf_ref[0]
        for s in range(1, N_DEV):
            acc = acc + buf_ref[s]
        o_ref[...] = acc

    vmem = pl.BlockSpec(memory_space=pltpu.VMEM)
    in_specs = [vmem] * n
    args = [p[0] for p in parts]
    if dep is not None:
        in_specs.append(pl.BlockSpec(memory_space=pl.ANY))
        args.append(dep)
    return pl.pallas_call(
        body,
        name=name,
        in_specs=in_specs,
        out_specs=vmem,
        out_shape=jax.ShapeDtypeStruct((rows, width), F32),
        scratch_shapes=[
            pltpu.VMEM((rows, width), F32),
            pltpu.VMEM((N_DEV, rows, width), F32),
            pltpu.SemaphoreType.DMA((N_DEV - 1,)),
            pltpu.SemaphoreType.DMA((N_DEV - 1,)),
        ],
    )(*args)


def _rope_tables(t):
    half = HEAD_DIM // 2
    inv_freq = ROPE_THETA ** (-jnp.arange(half, dtype=F32) / half)
    ang = jnp.arange(t, dtype=jnp.int32).astype(F32)[:, None] * inv_freq[None, :]
    cos, sin = jnp.cos(ang), jnp.sin(ang)
    cosf = jnp.concatenate([cos, cos, cos, cos], axis=1)
    sinf = jnp.concatenate([-sin, sin, -sin, sin], axis=1)
    return cosf, sinf


def _local_step(x, mem, target, gains, sinks, aw, cw, pre_w, get_w, put_g, dep0=None):
    t, d = x.shape
    nq = aw // HEAD_DIM
    kw = aw // Q_PER_KV
    z0 = aw + 2 * kw
    gb0, gc0 = z0 + cw, z0 + 2 * cw
    ga0 = z0 + 3 * cw
    gcm0 = ga0 + d
    cosf, sinf = _rope_tables(t)

    u1 = _rms_fwd(x, gains["g_mix"], name="rms_mix", dep=(cosf, sinf) if dep0 is None else (dep0, cosf, sinf))
    mem_n = _rms_fwd(mem, gains["g_mem"], name="rms_mem", dep=dep0)
    pre_w("w_in", u1)
    w_in_t = get_w("w_in", u1)
    proj = _mm(u1, w_in_t, mode="nt", tm=1024, tn=512, tk=2048, out_dtype=F32, name="mm_in")
    o_attn, q_rot, k_rot = _swa_fwd(proj, cosf, sinf, sinks, nq=nq, name="swa_fwd", dep=pre_w("conv_w8", proj))
    conv_w8 = get_w("conv_w8", o_attn)
    w_attn_proj, w_conv_proj, w_mix_out = (get_w(n, o_attn) for n in ("w_attn_proj", "w_conv_proj", "w_mix_out"))
    w_xq, w_xkv, w_xo = (get_w(n, o_attn) for n in ("w_xq", "w_xkv", "w_xo"))
    y_attn = _mm(o_attn, w_attn_proj, mode="nn", tm=1024, tn=1024, tk=1024, out_dtype=F32, name="mm_attn_proj")
    cy = _conv_fwd(proj, conv_w8, z0=z0, gb0=gb0, gc0=gc0, cw=cw, name="conv_fwd")
    y_conv, merged = _gate_fwd(cy, w_conv_proj, proj, y_attn, ga0=ga0, gc0=gcm0, name="mm_conv_proj")
    h1, u2 = _mm(merged, w_mix_out, mode="nn", tm=512, tn=d, tk=2048, out_dtype=F32, name="mm_mix_out", residual=x,
                 rms_gain=gains["g_xattn"])
    xq = _mm(u2, w_xq, mode="nn", tm=1024, tn=512, tk=2048, out_dtype=BF, name="mm_xq",
             dep=pre_w("w_ffn_in", h1))
    kv = _mm(mem_n, w_xkv, mode="nn", tm=256, tn=1024, tk=2048, out_dtype=BF, name="mm_xkv")
    o_x = _xattn_fwd(xq, kv, name="xattn_fwd")
    h2, u3 = _mm(o_x, w_xo, mode="nn", tm=512, tn=d, tk=512, out_dtype=F32, name="mm_xo", residual=h1,
                 rms_gain=gains["g_ffn"])
    w_ffn_in = get_w("w_ffn_in", xq)
    hid2, act = _ffn_in_fwd(u3, w_ffn_in, name="mm_ffn_in", dep=pre_w("w_ffn_out", u3))
    w_ffn_out = get_w("w_ffn_out", act)
    h3 = _mm(act, w_ffn_out, mode="nn", tm=512, tn=1024, tk=8192, out_dtype=F32, name="mm_ffn_out", residual=h2)

    tt = 8192
    dh3, dh3b, loss_tile, dg_final = _loss_head(h3, target, gains["g_final"], name="loss_head")
    tok = put_g("w_ffn_out", _mm(act, dh3b, mode="tn", tm=512, tn=1024, tk=tt, out_dtype=BF, name="mm_dw_ffn_out"))
    dhid2 = _ffn_out_bwd(dh3b, w_ffn_out, hid2, name="mm_dact", dep=tok)
    f2 = w_ffn_in.shape[1]
    tok = put_g("w_ffn_in", _mm(u3, dhid2, mode="tn", tm=1024, tn=f2 // N_DEV, tk=tt, out_dtype=BF,
                                name="mm_dw_ffn_in", b_planes=2, stacked=True), stacked=True)
    du3 = _mm(dhid2, w_ffn_in, mode="nt", tm=1024, tn=1024, tk=2816, out_dtype=F32, name="mm_du3", dep=tok,
              a_planes=2)
    dh2, dh2b, dg_ffn = _rms_bwd(du3, h2, gains["g_ffn"], dh3, name="rms_ffn_bwd")
    put_g("w_xo", _mm(o_x, dh2b, mode="tn", tm=512, tn=d // N_DEV, tk=tt, out_dtype=BF, name="mm_dw_xo",
                      stacked=True), stacked=True)
    do_x = _mm(dh2b, w_xo, mode="nt", tm=1024, tn=512, tk=2048, out_dtype=BF, name="mm_do_x")
    dxq, dkv = _xattn_bwd(xq, kv, do_x, name="xattn_bwd")
    put_g("w_xkv", _mm(mem_n, dkv, mode="tn", tm=1024, tn=1024, tk=256, out_dtype=BF, name="mm_dw_xkv"))
    tok = put_g("w_xq", _mm(u2, dxq, mode="tn", tm=1024, tn=512, tk=tt, out_dtype=BF, name="mm_dw_xq"))
    tok_xq = tok
    dmem_n = _mm(dkv, w_xkv, mode="nt", tm=256, tn=1024, tk=1024, out_dtype=F32, name="mm_dmem")
    _, _, dg_mem = _rms_bwd(dmem_n, mem, gains["g_mem"], None, name="rms_mem_bwd")
    dh1, dh1b, dg_xattn = _rms_bwd(dxq, h1, gains["g_xattn"], dh2, name="rms_xattn_bwd", du_w=w_xq, dep=tok_xq)
    put_g("w_mix_out", _mm(merged, dh1b, mode="tn", tm=1024, tn=1024, tk=tt, out_dtype=BF, name="mm_dw_mix_out"))
    dya, dyc, dga, dgc = _gate_bwd(dh1b, w_mix_out, proj, y_attn, y_conv, ga0=ga0, gc0=gcm0, name="mm_dmerged")
    put_g("w_attn_proj", _mm(o_attn, dya, mode="tn", tm=1024, tn=d // N_DEV, tk=tt, out_dtype=BF,
                             name="mm_dw_attn_proj", stacked=True), stacked=True)
    do_attn = _mm(dya, w_attn_proj, mode="nt", tm=1024, tn=1024, tk=2048, out_dtype=BF, name="mm_do_attn")
    tok = put_g("w_conv_proj", _mm(cy, dyc, mode="tn", tm=1024, tn=d // N_DEV, tk=tt, out_dtype=BF,
                                   name="mm_dw_conv_proj", stacked=True), stacked=True)
    dcy = _mm(dyc, w_conv_proj, mode="nt", tm=1024, tn=1024, tk=2048, out_dtype=F32, name="mm_dcy", dep=tok)
    dz, dgb, dgcv, dconv_w8 = _conv_bwd(proj, conv_w8, dcy, z0=z0, gb0=gb0, gc0=gc0, cw=cw, name="conv_bwd")
    dq, dk, dv, dsink_tile = _swa_bwd(q_rot, k_rot, proj, do_attn, cosf, sinf, sinks, nq=nq, name="swa_bwd")
    dproj = jnp.concatenate([dq, dk, dv, dz, dgb, dgcv, dga, dgc], axis=1)
    for hi in range(2):
        tok = put_g("w_in_%d" % hi, _mm(dproj, u1, mode="tn", tm=512, tn=d // 2, tk=tt, out_dtype=BF,
                                        name="mm_dw_in_%d" % hi, b_cols=(hi * (d // 2), d // 2), dep=tok))
    du1 = _mm(dproj, w_in_t, mode="nn", tm=512, tn=1024, tk=4352, out_dtype=F32, name="mm_du1", dep=tok)
    grad_x, _, dg_mix = _rms_bwd(du1, x, gains["g_mix"], dh1, name="rms_mix_bwd")

    small = {
        "g_mix": dg_mix, "g_xattn": dg_xattn, "g_mem": dg_mem, "g_ffn": dg_ffn, "g_final": dg_final,
        "attn_sinks": dsink_tile, "conv_w8": dconv_w8, "loss": loss_tile,
    }
    return grad_x, small


_COL_SHARDED = ("w_in", "w_attn_proj", "w_conv_proj", "w_xo", "w_ffn_in")
_ROW_SHARDED = ("w_mix_out", "w_xq", "w_xkv", "w_ffn_out")
_BIG = _COL_SHARDED + _ROW_SHARDED
_GAINS = ("g_mix", "g_xattn", "g_mem", "g_ffn", "g_final")
_GATHER_GROUPS = (("w_in",), ("conv_w8", "w_attn_proj", "w_conv_proj", "w_mix_out", "w_xq", "w_xkv", "w_xo"),
                  ("w_ffn_in",), ("w_ffn_out",))
_SCATTER_GROUPS = (("w_ffn_out",), ("w_ffn_in",), ("w_xo", "w_xq", "w_xkv"),
                   ("w_mix_out", "w_attn_proj", "w_conv_proj"), ("w_in_0",), ("w_in_1",))
_WEIGHTS = ("g_mix", "w_in", "conv_w", "attn_sinks", "w_attn_proj", "w_conv_proj", "w_mix_out", "g_xattn", "g_mem",
            "w_xq", "w_xkv", "w_xo", "g_ffn", "w_ffn_in", "w_ffn_out", "g_final")


def _unstack(g, col_sharded):
    n, r, c = g.shape
    if col_sharded:
        return jnp.transpose(g, (1, 0, 2)).reshape(r, n * c)
    return g.reshape(n * r, c)


def _stack(w, col_sharded):
    r, c = w.shape
    if col_sharded:
        return jnp.transpose(w.reshape(r, N_DEV, c // N_DEV), (1, 0, 2))
    return w.reshape(N_DEV, r // N_DEV, c)


def kernel(x, mem, g_mix, w_in, conv_w, attn_sinks, w_attn_proj, w_conv_proj, w_mix_out, g_xattn, g_mem, w_xq, w_xkv, w_xo, g_ffn, w_ffn_in, w_ffn_out, g_final, loss_target, m_g_mix, m_w_in, m_conv_w, m_attn_sinks, m_w_attn_proj, m_w_conv_proj, m_w_mix_out, m_g_xattn, m_g_mem, m_w_xq, m_w_xkv, m_w_xo, m_g_ffn, m_w_ffn_in, m_w_ffn_out, m_g_final, v_g_mix, v_w_in, v_conv_w, v_attn_sinks, v_w_attn_proj, v_w_conv_proj, v_w_mix_out, v_g_xattn, v_g_mem, v_w_xq, v_w_xkv, v_w_xo, v_g_ffn, v_w_ffn_in, v_w_ffn_out, v_g_final):
    w_ = dict(g_mix=g_mix, w_in=w_in, conv_w=conv_w, attn_sinks=attn_sinks, w_attn_proj=w_attn_proj,
              w_conv_proj=w_conv_proj, w_mix_out=w_mix_out, g_xattn=g_xattn, g_mem=g_mem, w_xq=w_xq, w_xkv=w_xkv,
              w_xo=w_xo, g_ffn=g_ffn, w_ffn_in=w_ffn_in, w_ffn_out=w_ffn_out, g_final=g_final)
    m_ = dict(g_mix=m_g_mix, w_in=m_w_in, conv_w=m_conv_w, attn_sinks=m_attn_sinks, w_attn_proj=m_w_attn_proj,
              w_conv_proj=m_w_conv_proj, w_mix_out=m_w_mix_out, g_xattn=m_g_xattn, g_mem=m_g_mem, w_xq=m_w_xq,
              w_xkv=m_w_xkv, w_xo=m_w_xo, g_ffn=m_g_ffn, w_ffn_in=m_w_ffn_in, w_ffn_out=m_w_ffn_out,
              g_final=m_g_final)
    v_ = dict(g_mix=v_g_mix, w_in=v_w_in, conv_w=v_conv_w, attn_sinks=v_attn_sinks, w_attn_proj=v_w_attn_proj,
              w_conv_proj=v_w_conv_proj, w_mix_out=v_w_mix_out, g_xattn=v_g_xattn, g_mem=v_g_mem, w_xq=v_w_xq,
              w_xkv=v_w_xkv, w_xo=v_w_xo, g_ffn=v_g_ffn, w_ffn_in=v_w_ffn_in, w_ffn_out=v_w_ffn_out,
              g_final=v_g_final)
    t, d = x.shape[1], x.shape[2]
    nq = attn_sinks.shape[-1]
    cw_shard = conv_w.shape[-1]
    cw = cw_shard * N_DEV

    def two_d(a):
        return a.reshape(a.shape[-2], a.shape[-1]) if a.ndim == 3 else a.reshape(1, a.shape[-1])

    me = _me()
    col = (set(_COL_SHARDED) | {"conv_w8"}) - {"w_in"}

    shards = {"w_in": two_d(w_in).T.astype(BF)}
    first, token = _exchange_start(
        [[(shards[n], _landing(shards[n], me)) for n in g] for g in _GATHER_GROUPS[:1]], GATHER_CHIPS,
        name="gather_start_0")
    for n in _BIG:
        if n != "w_in":
            shards[n] = _to_bf16(two_d(w_[n]), name="cast_" + n, dep=token)
    shards["conv_w8"] = jnp.zeros((SUBLANES, cw_shard), F32).at[:3].set(two_d(conv_w))
    rest, token = _exchange_start(
        [[(shards[n], _landing(shards[n], me)) for n in g] for g in _GATHER_GROUPS[1:]], GATHER_CHIPS,
        name="gather_start_1", after=token)
    gathers = first + rest
    passes, full = {}, {}

    def group_of(name):
        return [name in g for g in _GATHER_GROUPS].index(True)

    def pre_w(name, after):
        gi = group_of(name)
        _, lands = _exchange_wait(gathers[gi], GATHER_CHIPS, after, name="gather_wait_%d" % gi)
        started, tok = _exchange_start([[(None, land) for land in lands]], GATHER_SIBLING,
                                       name="gather_pass_%d" % gi)
        passes[gi] = started[0]
        return tok

    def get_w(name, after):
        if name not in full:
            gi = group_of(name)
            _, lands = _exchange_wait(passes[gi], GATHER_SIBLING, after, name="gather_pass_wait_%d" % gi)
            for n, land in zip(_GATHER_GROUPS[gi], lands):
                full[n] = _unstack(land, n in col)
        return full[name]

    pending, scatters = {}, []

    def put_g(name, dw, stacked=False):
        pending[name] = dw if stacked else _stack(dw, name in col)
        gi = [name in g for g in _SCATTER_GROUPS].index(True)
        group = _SCATTER_GROUPS[gi]
        if not all(n in pending for n in group):
            return None
        pairs = [(pending[n], lax.empty(pending[n].shape, pending[n].dtype)) for n in group]
        started, tok = _exchange_start([pairs], SCATTER_DIRECT, name="scatter_start_%d" % gi)
        scatters.append((gi, started[0]))
        return tok

    gains = {n: two_d(w_[n]) for n in _GAINS}
    grad_x, small = _local_step(
        x[0], mem[0], loss_target[0], gains, attn_sinks.reshape(nq), w_attn_proj.shape[-2], cw, pre_w, get_w, put_g,
        dep0=token)

    grads, deltas, new_m, new_v = {}, {}, {}, {}
    me1 = me.reshape(1).astype(jnp.int32)
    after, halves = grad_x, []
    for gi, started in scatters:
        mine, parts = _exchange_wait(started, SCATTER_DIRECT, after, name="scatter_wait_%d" % gi)
        for n, own, p in zip(_SCATTER_GROUPS[gi], mine, parts):
            if n.startswith("w_in_"):
                halves.append((p, own))
                if len(halves) < 2:
                    continue
                n = "w_in"
                g, dl, nm, nv = (a.T for a in _adamw_sum_pieces(
                    [h[0] for h in halves], [h[1] for h in halves], me1, two_d(w_[n]).T, two_d(m_[n]).T,
                    two_d(v_[n]).T, name="adamw_" + n))
            else:
                g, dl, nm, nv = _adamw_sum(p, own, me1, two_d(w_[n]), two_d(m_[n]), two_d(v_[n]), name="adamw_" + n)
            shape = w_[n].shape
            grads[n], deltas[n], new_m[n], new_v[n] = (a.reshape(shape) for a in (g, dl, nm, nv))
            after = g

    parts = [(small[n], i, 1) for i, n in enumerate(_GAINS)]
    parts += [(small["attn_sinks"], 5, 1), (small["loss"], 6, 1), (small["conv_w8"], 8, 3)]
    red = _all_reduce_small(parts, 2 * SUBLANES, max(d, cw), name="reduce_small", dep=after)
    loss = red[6, 0]
    small_g = {n: red[i:i + 1, :d] for i, n in enumerate(_GAINS)}
    small_g["attn_sinks"] = red[5:6, :nq]
    small_g["conv_w"] = lax.dynamic_slice(red, (8, me * cw_shard), (3, cw_shard))
    for n in _GAINS + ("attn_sinks", "conv_w"):
        shape = w_[n].shape
        g = small_g[n]
        dl, nm, nv = _adamw_small(two_d(w_[n]), g, two_d(m_[n]), two_d(v_[n]), name="adamw_" + n)
        grads[n], deltas[n], new_m[n], new_v[n] = (a.reshape(shape) for a in (g, dl, nm, nv))

    return (loss, grad_x[None], *[grads[n] for n in _WEIGHTS], *[deltas[n] for n in _WEIGHTS],
            *[new_m[n] for n in _WEIGHTS], *[new_v[n] for n in _WEIGHTS])
```

```python
import math

import jax
import jax.numpy as jnp
from jax import lax
from jax.experimental import pallas as pl
from jax.experimental.pallas import tpu as pltpu

HEAD_DIM = 64
Q_PER_KV = 4
WINDOW = 128
X_HEAD_DIM = 128
ROPE_THETA = 10000.0
EPS = 1e-6
ADAM_LR = 0.001
ADAM_B1 = 0.9
ADAM_B2 = 0.999
ADAM_EPS = 1e-08
ADAM_WD = 0.01
ADAM_STEP = 10

N_DEV = 8
LANES = 128
SUBLANES = 8
VMEM_LIMIT_BYTES = 56 * 1024 * 1024
BF = jnp.bfloat16
F32 = jnp.float32
MESH = pl.DeviceIdType.MESH


def _cp(*sem):
    return pltpu.CompilerParams(dimension_semantics=sem, vmem_limit_bytes=VMEM_LIMIT_BYTES)


def _sigmoid(x):
    return 1.0 / (1.0 + jnp.exp(-x))


_DIMS = {
    "nn": (((1,), (0,)), ((), ())),
    "nt": (((1,), (1,)), ((), ())),
    "tn": (((0,), (0,)), ((), ())),
}


def _fit(dim, tile):
    if dim <= tile:
        return dim
    for t in range(tile // LANES * LANES, 0, -LANES):
        if dim % t == 0:
            return t
    return dim


def _mm(a, b, *, mode, tm, tn, tk, out_dtype, name, residual=None, dep=None, a_planes=1, b_planes=1,
        stacked=False, b_cols=None, a_cols=None, rms_gain=None, fuse_a=False):
    if a_planes > 1:
        assert mode == "nt"
        (_, m, kp), (n, k) = a.shape, b.shape
        assert kp * a_planes == k
    elif b_planes > 1:
        assert mode == "tn"
        (k, m), (_, k2, np_) = a.shape, b.shape
        n = np_ * b_planes
        assert k == k2
    elif mode == "nn":
        (m, k), (k2, n) = a.shape, b.shape
        assert k == k2, (name, a.shape, b.shape)
    elif mode == "nt":
        (m, k), (n, k2) = a.shape, b.shape
        if a_cols is not None:
            k = a_cols[1]
        assert k == k2, (name, a.shape, b.shape)
    else:
        (k, m), (k2, n) = a.shape, b.shape
        assert k == k2, (name, a.shape, b.shape)
    tm, tn, tk = _fit(m, tm), _fit(n // b_planes, tn), _fit(k // a_planes, tk)
    assert m % tm == 0 and (n // b_planes) % tn == 0 and (k // a_planes) % tk == 0, (name, m, n, k, tm, tn, tk)
    ka0 = 0
    if a_cols is not None:
        assert mode == "nt" and a_planes == 1 and a_cols[0] % tk == 0
        ka0 = a_cols[0] // tk
    j0 = 0
    if b_cols is not None:
        assert mode == "tn" and b_planes == 1 and b_cols[0] % tn == 0 and b_cols[1] % tn == 0
        j0, n = b_cols[0] // tn, b_cols[1]
    nk = k // tk
    nkp, njp = nk // a_planes, n // tn // b_planes
    if a_planes > 1:
        a_spec = pl.BlockSpec((None, tm, tk), lambda i, j, kk: (kk // nkp, i, kk % nkp))
    elif mode == "tn":
        a_spec = pl.BlockSpec((tk, tm), lambda i, j, kk: (kk, i))
    else:
        a_spec = pl.BlockSpec((tm, tk), lambda i, j, kk: (i, kk + ka0))
    if b_planes > 1:
        b_spec = pl.BlockSpec((None, tk, tn), lambda i, j, kk: (j // njp, kk, j % njp))
    elif mode == "nt":
        b_spec = pl.BlockSpec((tn, tk), lambda i, j, kk: (j, kk))
    else:
        b_spec = pl.BlockSpec((tk, tn), lambda i, j, kk: (kk, j + j0))
    if stacked:
        assert residual is None
        o_spec = pl.BlockSpec((None, tm, tn), lambda i, j, kk: (j, i, 0))
        out_shape = jax.ShapeDtypeStruct((n // tn, m, tn), out_dtype)
    else:
        o_spec = pl.BlockSpec((tm, tn), lambda i, j, kk: (i, j))
        out_shape = jax.ShapeDtypeStruct((m, n), out_dtype)
    dims = _DIMS[mode]
    has_res = residual is not None
    has_rms = rms_gain is not None
    assert not has_rms or (tn == n and not stacked)
    n_in = 2 + has_res + has_rms + (dep is not None)

    def body(*refs):
        a_ref, b_ref, r_ref, o_ref = refs[0], refs[1], refs[2], refs[n_in]
        part = lax.dot_general(a_ref[...].astype(BF), b_ref[...].astype(BF), dims, preferred_element_type=F32)

        def finish(acc):
            if has_res:
                acc = r_ref[...] + acc
            o_ref[...] = acc.astype(out_dtype)
            if has_rms:
                r = lax.rsqrt(jnp.mean(acc * acc, axis=-1, keepdims=True) + EPS)
                refs[n_in + 1][...] = ((acc * r) * refs[2 + has_res][...]).astype(BF)

        if nk == 1:
            finish(part)
        else:
            acc_ref = refs[-1]
            kk = pl.program_id(2)

            @pl.when(kk == 0)
            def _():
                acc_ref[...] = part

            @pl.when(kk > 0)
            def _():
                acc_ref[...] += part

            @pl.when(kk == nk - 1)
            def _():
                finish(acc_ref[...])

    in_specs = [a_spec, b_spec] + ([o_spec] if has_res else [])
    args = (a, b) + ((residual,) if has_res else ())
    out_specs = o_spec
    if has_rms:
        in_specs.append(pl.BlockSpec((1, n), lambda i, j, kk: (0, 0)))
        args += (rms_gain,)
        out_specs = [o_spec, o_spec]
        out_shape = [out_shape, jax.ShapeDtypeStruct((m, n), BF)]
    if dep is not None:
        in_specs.append(pl.BlockSpec(memory_space=pl.ANY))
        args += (dep,)
    return pl.pallas_call(
        body,
        name=name,
        grid=(m // tm, n // tn, nk),
        in_specs=in_specs,
        out_specs=out_specs,
        out_shape=out_shape,
        scratch_shapes=[pltpu.VMEM((tm, tn), F32)] if nk > 1 else [],
        compiler_params=pltpu.CompilerParams(
            dimension_semantics=("parallel", "parallel", "arbitrary"), vmem_limit_bytes=VMEM_LIMIT_BYTES,
            allow_input_fusion=[True] + [False] * (len(args) - 1) if fuse_a else None),
    )(*args)


def _rms_fwd(h, g, *, name, tm=512, dep=None):
    t, d = h.shape
    tm = min(tm, t)

    def body(*refs):
        h_ref, g_ref, u_ref = refs[0], refs[1], refs[-1]
        hv = h_ref[...]
        r = lax.rsqrt(jnp.mean(hv * hv, axis=-1, keepdims=True) + EPS)
        u_ref[...] = ((hv * r) * g_ref[...]).astype(BF)

    in_specs = [pl.BlockSpec((tm, d), lambda i: (i, 0)), pl.BlockSpec((1, d), lambda i: (0, 0))]
    args = (h, g)
    for one in () if dep is None else (dep if isinstance(dep, tuple) else (dep,)):
        in_specs.append(pl.BlockSpec(memory_space=pl.ANY))
        args += (one,)
    return pl.pallas_call(
        body,
        name=name,
        grid=(t // tm,),
        in_specs=in_specs,
        out_specs=pl.BlockSpec((tm, d), lambda i: (i, 0)),
        out_shape=jax.ShapeDtypeStruct((t, d), BF),
        compiler_params=_cp("parallel"),
    )(*args)


def _rms_bwd(du, h, g, dres, *, name, tm=512, du_w=None, dep=None):
    t, d = h.shape
    tm = min(tm, t)
    want_dh = dres is not None

    def body(*refs):
        du_ref, h_ref, g_ref, dg_ref = refs[0], refs[1], refs[2], refs[-1]
        if want_dh:
            dres_ref, dh_ref, dhb_ref = refs[3], refs[-3], refs[-2]
        i = pl.program_id(0)
        hv = h_ref[...]
        duv = du_ref[...]
        if du_w is not None:
            duv = lax.dot_general(duv, refs[3 + want_dh][...], _DIMS["nt"], preferred_element_type=F32)
        r = lax.rsqrt(jnp.mean(hv * hv, axis=-1, keepdims=True) + EPS)
        nv = hv * r
        if want_dh:
            gy = duv * g_ref[...]
            dh = dres_ref[...] + r * (gy - nv * jnp.mean(nv * gy, axis=-1, keepdims=True))
            dh_ref[...] = dh
            dhb_ref[...] = dh.astype(BF)

        @pl.when(i == 0)
        def _():
            dg_ref[...] = jnp.zeros_like(dg_ref)

        dg_ref[...] += jnp.sum(duv * nv, axis=0, keepdims=True)

    row = pl.BlockSpec((tm, d), lambda i: (i, 0))
    vec = pl.BlockSpec((1, d), lambda i: (0, 0))
    du_spec = row if du_w is None else pl.BlockSpec((tm, du.shape[1]), lambda i: (i, 0))
    if want_dh:
        in_specs, args = [du_spec, row, vec, row], (du, h, g, dres)
        out_specs = [row, row, vec]
        out_shape = [jax.ShapeDtypeStruct((t, d), F32), jax.ShapeDtypeStruct((t, d), BF),
                     jax.ShapeDtypeStruct((1, d), F32)]
    else:
        in_specs, args = [du_spec, row, vec], (du, h, g)
        out_specs = [vec]
        out_shape = [jax.ShapeDtypeStruct((1, d), F32)]
    if du_w is not None:
        in_specs.append(pl.BlockSpec(du_w.shape, lambda i: (0, 0)))
        args += (du_w,)
    if dep is not None:
        in_specs.append(pl.BlockSpec(memory_space=pl.ANY))
        args += (dep,)
    outs = pl.pallas_call(
        body,
        name=name,
        grid=(t // tm,),
        in_specs=in_specs,
        out_specs=out_specs,
        out_shape=out_shape,
        compiler_params=_cp("arbitrary"),
    )(*args)
    return (outs[0], outs[1], outs[2]) if want_dh else (None, None, outs[0])


def _loss_head(h, target, g, *, name, tm=512):
    t, d = h.shape
    tm = min(tm, t)

    def body(h_ref, t_ref, g_ref, dh_ref, dhb_ref, loss_ref, dg_ref):
        i = pl.program_id(0)
        hv = h_ref[...]
        gv = g_ref[...]
        r = lax.rsqrt(jnp.mean(hv * hv, axis=-1, keepdims=True) + EPS)
        nv = hv * r
        e = nv * gv - t_ref[...]
        per_tok = jnp.mean(e * e, axis=-1, keepdims=True)
        lp = 0.5 * jnp.sum(per_tok, axis=0, keepdims=True)
        dy = e * (1.0 / d)
        gy = dy * gv
        dh = r * (gy - nv * jnp.mean(nv * gy, axis=-1, keepdims=True))
        dh_ref[...] = dh
        dhb_ref[...] = dh.astype(BF)

        @pl.when(i == 0)
        def _():
            loss_ref[...] = jnp.zeros_like(loss_ref)
            dg_ref[...] = jnp.zeros_like(dg_ref)

        loss_ref[...] += jnp.broadcast_to(lp, loss_ref.shape)
        dg_ref[...] += jnp.sum(dy * nv, axis=0, keepdims=True)

    row = pl.BlockSpec((tm, d), lambda i: (i, 0))
    vec = pl.BlockSpec((1, d), lambda i: (0, 0))
    return pl.pallas_call(
        body,
        name=name,
        grid=(t // tm,),
        in_specs=[row, row, vec],
        out_specs=[row, row, pl.BlockSpec((SUBLANES, LANES), lambda i: (0, 0)), vec],
        out_shape=[
            jax.ShapeDtypeStruct((t, d), F32),
            jax.ShapeDtypeStruct((t, d), BF),
            jax.ShapeDtypeStruct((SUBLANES, LANES), F32),
            jax.ShapeDtypeStruct((1, d), F32),
        ],
        compiler_params=_cp("arbitrary"),
    )(h, target, g)


def _ffn_in_fwd(u, w, *, name, tm=1024, tn=512, dep=None):
    t, d = u.shape
    f = w.shape[1] // 2
    tm, tn = _fit(t, tm), _fit(f, tn)
    nf = f // tn

    def body(*refs):
        u_ref, wa_ref, wb_ref, hid_ref, act_ref = refs[0], refs[1], refs[2], refs[-2], refs[-1]
        uv = u_ref[...]
        a = jnp.dot(uv, wa_ref[...], preferred_element_type=F32)
        b = jnp.dot(uv, wb_ref[...], preferred_element_type=F32)
        hid_ref[0] = a.astype(BF)
        hid_ref[1] = b.astype(BF)
        act_ref[...] = ((a * _sigmoid(a)) * b).astype(BF)

    in_specs = [
        pl.BlockSpec((tm, d), lambda i, j: (i, 0)),
        pl.BlockSpec((d, tn), lambda i, j: (0, j)),
        pl.BlockSpec((d, tn), lambda i, j: (0, nf + j)),
    ]
    args = (u, w, w)
    if dep is not None:
        in_specs.append(pl.BlockSpec(memory_space=pl.ANY))
        args += (dep,)
    return pl.pallas_call(
        body,
        name=name,
        grid=(t // tm, nf),
        in_specs=in_specs,
        out_specs=[pl.BlockSpec((2, tm, tn), lambda i, j: (0, i, j)), pl.BlockSpec((tm, tn), lambda i, j: (i, j))],
        out_shape=[jax.ShapeDtypeStruct((2, t, f), BF), jax.ShapeDtypeStruct((t, f), BF)],
        compiler_params=_cp("parallel", "parallel"),
    )(*args)


def _ffn_out_bwd(dh, w_out, hid2, *, name, tm=1024, tn=512, dep=None):
    t, d = dh.shape
    f = w_out.shape[0]
    tm, tn = _fit(t, tm), _fit(f, tn)

    def body(*refs):
        dh_ref, w_ref, hid_ref, o_ref = refs[0], refs[1], refs[2], refs[-1]
        dact = lax.dot_general(dh_ref[...], w_ref[...], _DIMS["nt"], preferred_element_type=F32)
        a = hid_ref[0].astype(F32)
        b = hid_ref[1].astype(F32)
        sg = _sigmoid(a)
        o_ref[0] = (dact * b * (sg * (1.0 + a * (1.0 - sg)))).astype(BF)
        o_ref[1] = (dact * (a * sg)).astype(BF)

    pair = pl.BlockSpec((2, tm, tn), lambda i, j: (0, i, j))
    in_specs = [pl.BlockSpec((tm, d), lambda i, j: (i, 0)), pl.BlockSpec((tn, d), lambda i, j: (j, 0)), pair]
    args = (dh, w_out, hid2)
    if dep is not None:
        in_specs.append(pl.BlockSpec(memory_space=pl.ANY))
        args += (dep,)
    return pl.pallas_call(
        body,
        name=name,
        grid=(t // tm, f // tn),
        in_specs=in_specs,
        out_specs=pair,
        out_shape=jax.ShapeDtypeStruct((2, t, f), BF),
        compiler_params=_cp("parallel", "parallel"),
    )(*args)


def _gate_fwd(cy, w, proj, ya, *, ga0, gc0, name, tm=1024, tc=512):
    t, d = ya.shape
    kc = cy.shape[1]
    tm, tc = _fit(t, tm), math.gcd(tc, d, ga0, gc0)
    a0, c0 = ga0 // tc, gc0 // tc

    def body(cy_ref, w_ref, ga_ref, gc_ref, ya_ref, yc_ref, o_ref):
        yc = jnp.dot(cy_ref[...], w_ref[...], preferred_element_type=F32)
        yc_ref[...] = yc
        o_ref[...] = (_sigmoid(ga_ref[...]) * ya_ref[...] + _sigmoid(gc_ref[...]) * yc).astype(BF)

    blk = pl.BlockSpec((tm, tc), lambda i, j: (i, j))
    return pl.pallas_call(
        body,
        name=name,
        grid=(t // tm, d // tc),
        in_specs=[
            pl.BlockSpec((tm, kc), lambda i, j: (i, 0)),
            pl.BlockSpec((kc, tc), lambda i, j: (0, j)),
            pl.BlockSpec((tm, tc), lambda i, j: (i, a0 + j)),
            pl.BlockSpec((tm, tc), lambda i, j: (i, c0 + j)),
            blk,
        ],
        out_specs=[blk, blk],
        out_shape=[jax.ShapeDtypeStruct((t, d), F32), jax.ShapeDtypeStruct((t, d), BF)],
        compiler_params=_cp("parallel", "parallel"),
    )(cy, w, proj, proj, ya)


def _gate_bwd(dh, w, proj, ya, yc, *, ga0, gc0, name, tm=1024, tc=512):
    t, d = ya.shape
    tm, tc = _fit(t, tm), math.gcd(tc, d, ga0, gc0)
    a0, c0 = ga0 // tc, gc0 // tc

    def body(dh_ref, w_ref, ga_ref, gc_ref, ya_ref, yc_ref, dya_ref, dyc_ref, dga_ref, dgc_ref):
        dmv = lax.dot_general(dh_ref[...], w_ref[...], _DIMS["nt"], preferred_element_type=F32)
        sa = _sigmoid(ga_ref[...])
        sc = _sigmoid(gc_ref[...])
        dya_ref[...] = (dmv * sa).astype(BF)
        dyc_ref[...] = (dmv * sc).astype(BF)
        dga_ref[...] = (dmv * ya_ref[...] * (sa * (1.0 - sa))).astype(BF)
        dgc_ref[...] = (dmv * yc_ref[...] * (sc * (1.0 - sc))).astype(BF)

    blk = pl.BlockSpec((tm, tc), lambda i, j: (i, j))
    out = jax.ShapeDtypeStruct((t, d), BF)
    return pl.pallas_call(
        body,
        name=name,
        grid=(t // tm, d // tc),
        in_specs=[
            pl.BlockSpec((tm, d), lambda i, j: (i, 0)),
            pl.BlockSpec((tc, d), lambda i, j: (j, 0)),
            pl.BlockSpec((tm, tc), lambda i, j: (i, a0 + j)),
            pl.BlockSpec((tm, tc), lambda i, j: (i, c0 + j)),
            blk,
            blk,
        ],
        out_specs=[blk, blk, blk, blk],
        out_shape=[out, out, out, out],
        compiler_params=_cp("parallel", "parallel"),
    )(dh, w, proj, proj, ya, yc)


def _conv_taps(cz, czp, i):
    czp = czp * (i > 0).astype(F32)
    h1 = czp[SUBLANES - 1:SUBLANES, :]
    h2 = czp[SUBLANES - 2:SUBLANES - 1, :]
    row = lax.broadcasted_iota(jnp.int32, cz.shape, 0)
    s1 = jnp.where(row == 0, h1, pltpu.roll(cz, 1, 0))
    s2 = jnp.where(row == 0, h2, jnp.where(row == 1, h1, pltpu.roll(cz, 2, 0)))
    return s1, s2


def _conv_fwd(proj, w8, *, z0, gb0, gc0, cw, name, tm=1024, tc=512):
    t = proj.shape[0]
    tm, tc = min(tm, t), math.gcd(tc, cw, z0, gb0, gc0)
    zb, bb, cb = z0 // tc, gb0 // tc, gc0 // tc
    rb = tm // SUBLANES

    def body(z_ref, gb_ref, gc_ref, zp_ref, gcp_ref, w_ref, o_ref):
        i = pl.program_id(0)
        cz = gc_ref[...] * z_ref[...]
        s1, s2 = _conv_taps(cz, gcp_ref[...] * zp_ref[...], i)
        w = w_ref[...]
        y = w[0:1, :] * s2 + w[1:2, :] * s1 + w[2:3, :] * cz
        o_ref[...] = (gb_ref[...] * y).astype(BF)

    def cur(b0):
        return pl.BlockSpec((tm, tc), lambda i, j: (i, b0 + j))

    def prev(b0):
        return pl.BlockSpec((SUBLANES, tc), lambda i, j: (jnp.maximum(i * rb - 1, 0), b0 + j))

    return pl.pallas_call(
        body,
        name=name,
        grid=(t // tm, cw // tc),
        in_specs=[cur(zb), cur(bb), cur(cb), prev(zb), prev(cb), pl.BlockSpec((SUBLANES, tc), lambda i, j: (0, j))],
        out_specs=pl.BlockSpec((tm, tc), lambda i, j: (i, j)),
        out_shape=jax.ShapeDtypeStruct((t, cw), BF),
        compiler_params=_cp("parallel", "parallel"),
    )(proj, proj, proj, proj, proj, w8)


def _conv_bwd(proj, w8, dcy, *, z0, gb0, gc0, cw, name, tm=1024, tc=512):
    t = proj.shape[0]
    tm, tc = min(tm, t), math.gcd(tc, cw, z0, gb0, gc0)
    zb, bb, cb = z0 // tc, gb0 // tc, gc0 // tc
    rb = tm // SUBLANES
    nt = t // tm

    def body(z_ref, gb_ref, gc_ref, zp_ref, gcp_ref, d_ref, dn_ref, gbn_ref, w_ref, dz_ref, dgb_ref, dgc_ref, dw_ref):
        i = pl.program_id(1)
        z = z_ref[...]
        gc = gc_ref[...]
        gb = gb_ref[...]
        cz = gc * z
        s1, s2 = _conv_taps(cz, gcp_ref[...] * zp_ref[...], i)
        w = w_ref[...]
        w0, w1, w2 = w[0:1, :], w[1:2, :], w[2:3, :]
        yc = w0 * s2 + w1 * s1 + w2 * cz
        dcyv = d_ref[...]
        dgb_ref[...] = (dcyv * yc).astype(BF)
        dyc = dcyv * gb
        dycn = dn_ref[...] * gbn_ref[...] * (i < nt - 1).astype(F32)
        n1, n2 = dycn[0:1, :], dycn[1:2, :]
        row = lax.broadcasted_iota(jnp.int32, cz.shape, 0)
        a1 = jnp.where(row == tm - 1, n1, pltpu.roll(dyc, tm - 1, 0))
        a2 = jnp.where(row == tm - 1, n2, jnp.where(row == tm - 2, n1, pltpu.roll(dyc, tm - 2, 0)))
        dcz = w2 * dyc + w1 * a1 + w0 * a2
        dz_ref[...] = (dcz * gc).astype(BF)
        dgc_ref[...] = (dcz * z).astype(BF)
        dw0 = jnp.sum(dyc * s2, axis=0, keepdims=True)
        dw1 = jnp.sum(dyc * s1, axis=0, keepdims=True)
        dw2 = jnp.sum(dyc * cz, axis=0, keepdims=True)
        r8 = lax.broadcasted_iota(jnp.int32, (SUBLANES, tc), 0)
        upd = jnp.where(r8 == 0, dw0, jnp.where(r8 == 1, dw1, jnp.where(r8 == 2, dw2, 0.0)))

        @pl.when(i == 0)
        def _():
            dw_ref[...] = jnp.zeros_like(dw_ref)

        dw_ref[...] += upd

    def cur(b0):
        return pl.BlockSpec((tm, tc), lambda j, i: (i, b0 + j))

    def prev(b0):
        return pl.BlockSpec((SUBLANES, tc), lambda j, i: (jnp.maximum(i * rb - 1, 0), b0 + j))

    def nxt(b0):
        return pl.BlockSpec((SUBLANES, tc), lambda j, i: (jnp.minimum((i + 1) * rb, t // SUBLANES - 1), b0 + j))

    blk = pl.BlockSpec((tm, tc), lambda j, i: (i, j))
    w_spec = pl.BlockSpec((SUBLANES, tc), lambda j, i: (0, j))
    out = jax.ShapeDtypeStruct((t, cw), BF)
    return pl.pallas_call(
        body,
        name=name,
        grid=(cw // tc, nt),
        in_specs=[cur(zb), cur(bb), cur(cb), prev(zb), prev(cb), blk, nxt(0), nxt(bb), w_spec],
        out_specs=[blk, blk, blk, w_spec],
        out_shape=[out, out, out, jax.ShapeDtypeStruct((SUBLANES, cw), F32)],
        compiler_params=_cp("parallel", "arbitrary"),
    )(proj, proj, proj, proj, proj, dcy, dcy, proj, w8)


def _rot_half(x):
    lane = lax.broadcasted_iota(jnp.int32, x.shape, 1)
    first = (lane % HEAD_DIM) < (HEAD_DIM // 2)
    return jnp.where(first, pltpu.roll(x, LANES - HEAD_DIM // 2, 1), pltpu.roll(x, HEAD_DIM // 2, 1))


def _rope(x, c, s):
    parts = []
    for a in range(x.shape[1] // LANES):
        xa = x[:, a * LANES:(a + 1) * LANES]
        parts.append(xa * c + _rot_half(xa) * s)
    return parts[0] if len(parts) == 1 else jnp.concatenate(parts, axis=1)


def _rope_bwd(dy, c, s):
    parts = []
    for a in range(dy.shape[1] // LANES):
        da = dy[:, a * LANES:(a + 1) * LANES]
        parts.append(da * c + _rot_half(da * s))
    return parts[0] if len(parts) == 1 else jnp.concatenate(parts, axis=1)


def _window(i):
    b = WINDOW
    r = lax.broadcasted_iota(jnp.int32, (b, b), 0)
    c = lax.broadcasted_iota(jnp.int32, (b, b), 1)
    return c <= r, c <= r + jnp.where(i > 0, b, 0)


def _band_pick(x, tri):
    b = tri.shape[0]
    return jnp.where(tri, x[:, b:], x[:, :b])


def _band_spread(y, tri):
    return jnp.concatenate([jnp.where(tri, 0.0, y), jnp.where(tri, y, 0.0)], axis=1)


def _chunk(x, a):
    return x[:, a * LANES:(a + 1) * LANES]


def _kv_aligned(kp, kc, h):
    band = jnp.concatenate([_chunk(kp, h // 2), _chunk(kc, h // 2)], axis=0).astype(F32)
    swapped = pltpu.roll(band, HEAD_DIM, 1)
    return (band, swapped) if h % 2 == 0 else (swapped, band)


def _swa_fwd(proj, cosf, sinf, sinks, *, nq, name, dep=None):
    t = proj.shape[0]
    nkv = nq // Q_PER_KV
    aw, kw, b = nq * HEAD_DIM, nkv * HEAD_DIM, WINDOW
    nb = t // b
    kblk = aw // kw
    scale = HEAD_DIM ** -0.5

    def body(*refs):
        sink_ref, q_ref, kc_ref, kp_ref, vc_ref, vp_ref, cc_ref, cp_ref, sc_ref, sp_ref = refs[:10]
        o_ref, qr_ref, kr_ref, s_scr, p_scr = refs[-5:]
        i = pl.program_id(0)
        cc, sc, cpv, spv = cc_ref[...], sc_ref[...], cp_ref[...], sp_ref[...]
        qr = _rope(q_ref[...], cc, sc)
        kc = _rope(kc_ref[...], cc, sc)
        kp = _rope(kp_ref[...], cpv, spv)
        qr_ref[...] = qr.astype(BF)
        kr_ref[...] = kc.astype(BF)
        vc, vp = vc_ref[...], vp_ref[...]
        tri, ok = _window(i)
        lo = lax.broadcasted_iota(jnp.int32, (b, LANES), 1) < HEAD_DIM
        ks = [[x.astype(BF) for x in _kv_aligned(kp, kc, h)] for h in range(nkv)]
        vs = [[x.astype(BF) for x in _kv_aligned(vp, vc, h)] for h in range(nkv)]
        for hq in range(nq):
            a, par = hq // 2, hq % 2
            qm = jnp.where(lo if par == 0 else ~lo, _chunk(qr, a), 0.0).astype(BF)
            s_scr[hq] = _band_pick(
                lax.dot_general(qm, ks[hq // Q_PER_KV][par], _DIMS["nt"], preferred_element_type=F32), tri)
        for hq in range(nq):
            s = jnp.where(ok, s_scr[hq] * scale, -jnp.inf)
            sink = sink_ref[hq]
            m = jnp.maximum(jnp.max(s, axis=-1, keepdims=True), sink)
            p = jnp.exp(s - m)
            p = p / (jnp.sum(p, axis=-1, keepdims=True) + jnp.exp(sink - m))
            p_scr[hq] = _band_spread(p, tri).astype(BF)
        for a in range(nq // 2):
            o_par = [jnp.dot(p_scr[2 * a + par], vs[(2 * a) // Q_PER_KV][par], preferred_element_type=F32)
                     for par in range(2)]
            o_ref[:, a * LANES:(a + 1) * LANES] = jnp.where(lo, o_par[0], o_par[1]).astype(BF)

    def prev_i(i):
        return jnp.maximum(i - 1, 0)

    tab_c = pl.BlockSpec((b, LANES), lambda i: (i, 0))
    tab_p = pl.BlockSpec((b, LANES), lambda i: (prev_i(i), 0))
    in_specs = [
        pl.BlockSpec(memory_space=pltpu.SMEM),
        pl.BlockSpec((b, aw), lambda i: (i, 0)),
        pl.BlockSpec((b, kw), lambda i: (i, kblk)),
        pl.BlockSpec((b, kw), lambda i: (prev_i(i), kblk)),
        pl.BlockSpec((b, kw), lambda i: (i, kblk + 1)),
        pl.BlockSpec((b, kw), lambda i: (prev_i(i), kblk + 1)),
        tab_c,
        tab_p,
        tab_c,
        tab_p,
    ]
    args = (sinks, proj, proj, proj, proj, proj, cosf, cosf, sinf, sinf)
    if dep is not None:
        in_specs.append(pl.BlockSpec(memory_space=pl.ANY))
        args += (dep,)
    return pl.pallas_call(
        body,
        name=name,
        grid=(nb,),
        in_specs=in_specs,
        out_specs=[
            pl.BlockSpec((b, aw), lambda i: (i, 0)),
            pl.BlockSpec((b, aw), lambda i: (i, 0)),
            pl.BlockSpec((b, kw), lambda i: (i, 0)),
        ],
        out_shape=[
            jax.ShapeDtypeStruct((t, aw), BF),
            jax.ShapeDtypeStruct((t, aw), BF),
            jax.ShapeDtypeStruct((t, kw), BF),
        ],
        scratch_shapes=[pltpu.VMEM((nq, b, b), F32), pltpu.VMEM((nq, b, 2 * b), BF)],
        compiler_params=_cp("parallel"),
    )(*args)


def _swa_bwd(qr, kr, proj, do, cosf, sinf, sinks, *, nq, name):
    t = proj.shape[0]
    nkv = nq // Q_PER_KV
    aw, kw, b = nq * HEAD_DIM, nkv * HEAD_DIM, WINDOW
    nb = t // b
    kblk = aw // kw
    scale = HEAD_DIM ** -0.5

    def body(sink_ref, q_ref, kc_ref, kp_ref, vc_ref, vp_ref, do_ref, cc_ref, cp_ref, sc_ref, sp_ref,
             dq_ref, dk_ref, dv_ref, ds_ref, ck_ref, cv_ref, sacc_ref, s_scr, dp_scr, ds_scr, pf_scr):
        i = pl.program_id(0)

        @pl.when(i == 0)
        def _():
            ck_ref[...] = jnp.zeros_like(ck_ref)
            cv_ref[...] = jnp.zeros_like(cv_ref)
            sacc_ref[...] = jnp.zeros_like(sacc_ref)

        @pl.when(i < nb)
        def _():
            q = q_ref[...]
            kc, kp = kc_ref[...], kp_ref[...]
            vc, vp = vc_ref[...], vp_ref[...]
            dov = do_ref[...]
            tri, ok = _window(i)
            lane = lax.broadcasted_iota(jnp.int32, (b, LANES), 1)
            lo = lane < HEAD_DIM
            cc, sc = cc_ref[...], sc_ref[...]
            nch = kw // LANES
            row_lo = lax.broadcasted_iota(jnp.int32, (LANES, b), 0) < HEAD_DIM
            dk_ch = [jnp.zeros((LANES, 2 * b), F32) for _ in range(nch)]
            dv_ch = [jnp.zeros((LANES, 2 * b), F32) for _ in range(nch)]
            sacc = jnp.zeros((b, LANES), F32)
            ks = [[x.astype(BF) for x in _kv_aligned(kp, kc, h)] for h in range(nkv)]
            vs = [[x.astype(BF) for x in _kv_aligned(vp, vc, h)] for h in range(nkv)]
            for hq in range(nq):
                a, par, h = hq // 2, hq % 2, hq // Q_PER_KV
                mine = lo if par == 0 else ~lo
                qm = jnp.where(mine, _chunk(q, a).astype(F32), 0.0).astype(BF)
                dom = jnp.where(mine, _chunk(dov, a).astype(F32), 0.0).astype(BF)
                s_scr[hq] = _band_pick(lax.dot_general(qm, ks[h][par], _DIMS["nt"], preferred_element_type=F32), tri)
                dp_scr[hq] = _band_pick(
                    lax.dot_general(dom, vs[h][par], _DIMS["nt"], preferred_element_type=F32), tri)
            for hq in range(nq):
                s = jnp.where(ok, s_scr[hq] * scale, -jnp.inf)
                sink = sink_ref[hq]
                m = jnp.maximum(jnp.max(s, axis=-1, keepdims=True), sink)
                e = jnp.exp(s - m)
                es = jnp.exp(sink - m)
                zinv = 1.0 / (jnp.sum(e, axis=-1, keepdims=True) + es)
                p = e * zinv
                dp = dp_scr[hq]
                delta = jnp.sum(p * dp, axis=-1, keepdims=True)
                ds_scr[hq] = _band_spread(p * (dp - delta) * scale, tri).astype(BF)
                pf_scr[hq] = _band_spread(p, tri).astype(BF)
                sacc = sacc + jnp.where(lane == hq, -(es * zinv) * delta, 0.0)
            for a in range(nq // 2):
                h = (2 * a) // Q_PER_KV
                qa_t = _chunk(q, a).astype(F32).T
                doa_t = _chunk(dov, a).astype(F32).T
                dq_par = []
                for par in range(2):
                    hq = 2 * a + par
                    mine_t = row_lo if par == 0 else ~row_lo
                    qm_t = jnp.where(mine_t, qa_t, 0.0).astype(BF)
                    dom_t = jnp.where(mine_t, doa_t, 0.0).astype(BF)
                    dsv = ds_scr[hq]
                    dq_par.append(jnp.dot(dsv, ks[h][par], preferred_element_type=F32))
                    dkh = jnp.dot(qm_t, dsv, preferred_element_type=F32)
                    dvh = jnp.dot(dom_t, pf_scr[hq], preferred_element_type=F32)
                    if par != h % 2:
                        dkh = pltpu.roll(dkh, HEAD_DIM, 0)
                        dvh = pltpu.roll(dvh, HEAD_DIM, 0)
                    dk_ch[h // 2] = dk_ch[h // 2] + dkh
                    dv_ch[h // 2] = dv_ch[h // 2] + dvh
                dqa = jnp.where(lo, dq_par[0], dq_par[1])
                dq_ref[:, a * LANES:(a + 1) * LANES] = _rope_bwd(dqa, cc, sc).astype(BF)
            dk_ch = [x.T for x in dk_ch]
            dv_ch = [x.T for x in dv_ch]
            dk = dk_ch[0] if nch == 1 else jnp.concatenate(dk_ch, axis=1)
            dv = dv_ch[0] if nch == 1 else jnp.concatenate(dv_ch, axis=1)
            dk_ref[...] = _rope_bwd(ck_ref[...] + dk[:b, :], cp_ref[...], sp_ref[...]).astype(BF)
            dv_ref[...] = (cv_ref[...] + dv[:b, :]).astype(BF)
            ck_ref[...] = dk[b:, :]
            cv_ref[...] = dv[b:, :]
            sacc_ref[...] += sacc

        @pl.when(i == nb)
        def _():
            dk_ref[...] = _rope_bwd(ck_ref[...], cp_ref[...], sp_ref[...]).astype(BF)
            dv_ref[...] = cv_ref[...].astype(BF)
            ds_ref[...] = jnp.broadcast_to(jnp.sum(sacc_ref[...], axis=0, keepdims=True), ds_ref.shape)

    def cur_i(i):
        return jnp.minimum(i, nb - 1)

    def prev_i(i):
        return jnp.clip(i - 1, 0, nb - 1)

    tab_c = pl.BlockSpec((b, LANES), lambda i: (cur_i(i), 0))
    tab_p = pl.BlockSpec((b, LANES), lambda i: (prev_i(i), 0))
    return pl.pallas_call(
        body,
        name=name,
        grid=(nb + 1,),
        in_specs=[
            pl.BlockSpec(memory_space=pltpu.SMEM),
            pl.BlockSpec((b, aw), lambda i: (cur_i(i), 0)),
            pl.BlockSpec((b, kw), lambda i: (cur_i(i), 0)),
            pl.BlockSpec((b, kw), lambda i: (prev_i(i), 0)),
            pl.BlockSpec((b, kw), lambda i: (cur_i(i), kblk + 1)),
            pl.BlockSpec((b, kw), lambda i: (prev_i(i), kblk + 1)),
            pl.BlockSpec((b, aw), lambda i: (cur_i(i), 0)),
            tab_c,
            tab_p,
            tab_c,
            tab_p,
        ],
        out_specs=[
            pl.BlockSpec((b, aw), lambda i: (cur_i(i), 0)),
            pl.BlockSpec((b, kw), lambda i: (prev_i(i), 0)),
            pl.BlockSpec((b, kw), lambda i: (prev_i(i), 0)),
            pl.BlockSpec((SUBLANES, LANES), lambda i: (0, 0)),
        ],
        out_shape=[
            jax.ShapeDtypeStruct((t, aw), BF),
            jax.ShapeDtypeStruct((t, kw), BF),
            jax.ShapeDtypeStruct((t, kw), BF),
            jax.ShapeDtypeStruct((SUBLANES, LANES), F32),
        ],
        scratch_shapes=[pltpu.VMEM((b, kw), F32), pltpu.VMEM((b, kw), F32), pltpu.VMEM((b, LANES), F32),
                        pltpu.VMEM((nq, b, b), F32), pltpu.VMEM((nq, b, b), F32),
                        pltpu.VMEM((nq, b, 2 * b), BF), pltpu.VMEM((nq, b, 2 * b), BF)],
        compiler_params=_cp("arbitrary"),
    )(sinks, qr, kr, kr, proj, proj, do, cosf, cosf, sinf, sinf)


def _xattn_fwd(xq, kv, *, name, tq=512):
    t, xw = xq.shape
    mtok = kv.shape[0]
    tq = min(tq, t)
    nh = xw // X_HEAD_DIM
    scale = X_HEAD_DIM ** -0.5

    def body(q_ref, kv_ref, o_ref):
        q = q_ref[...]
        kvv = kv_ref[...]
        outs = []
        for h in range(nh):
            sl = slice(h * X_HEAD_DIM, (h + 1) * X_HEAD_DIM)
            k = kvv[:, sl]
            v = kvv[:, xw + h * X_HEAD_DIM: xw + (h + 1) * X_HEAD_DIM]
            s = lax.dot_general(q[:, sl], k, _DIMS["nt"], preferred_element_type=F32) * scale
            e = jnp.exp(s - jnp.max(s, axis=-1, keepdims=True))
            p = e / jnp.sum(e, axis=-1, keepdims=True)
            outs.append(jnp.dot(p.astype(BF), v, preferred_element_type=F32))
        o_ref[...] = jnp.concatenate(outs, axis=1).astype(BF)

    return pl.pallas_call(
        body,
        name=name,
        grid=(t // tq,),
        in_specs=[pl.BlockSpec((tq, xw), lambda i: (i, 0)), pl.BlockSpec((mtok, 2 * xw), lambda i: (0, 0))],
        out_specs=pl.BlockSpec((tq, xw), lambda i: (i, 0)),
        out_shape=jax.ShapeDtypeStruct((t, xw), BF),
        compiler_params=_cp("parallel"),
    )(xq, kv)


def _xattn_bwd(xq, kv, do, *, name, tq=512):
    t, xw = xq.shape
    mtok = kv.shape[0]
    tq = min(tq, t)
    nh = xw // X_HEAD_DIM
    scale = X_HEAD_DIM ** -0.5

    def body(q_ref, kv_ref, do_ref, dq_ref, dkv_ref):
        i = pl.program_id(0)
        q = q_ref[...]
        kvv = kv_ref[...]
        dov = do_ref[...]
        dqs, dks, dvs = [], [], []
        for h in range(nh):
            sl = slice(h * X_HEAD_DIM, (h + 1) * X_HEAD_DIM)
            k = kvv[:, sl]
            v = kvv[:, xw + h * X_HEAD_DIM: xw + (h + 1) * X_HEAD_DIM]
            qh, doh = q[:, sl], dov[:, sl]
            s = lax.dot_general(qh, k, _DIMS["nt"], preferred_element_type=F32) * scale
            e = jnp.exp(s - jnp.max(s, axis=-1, keepdims=True))
            p = e / jnp.sum(e, axis=-1, keepdims=True)
            dp = lax.dot_general(doh, v, _DIMS["nt"], preferred_element_type=F32)
            delta = jnp.sum(p * dp, axis=-1, keepdims=True)
            dsv = (p * (dp - delta) * scale).astype(BF)
            dqs.append(jnp.dot(dsv, k, preferred_element_type=F32))
            dks.append(lax.dot_general(dsv, qh, _DIMS["tn"], preferred_element_type=F32))
            dvs.append(lax.dot_general(p.astype(BF), doh, _DIMS["tn"], preferred_element_type=F32))
        dq_ref[...] = jnp.concatenate(dqs, axis=1).astype(BF)

        @pl.when(i == 0)
        def _():
            dkv_ref[...] = jnp.zeros_like(dkv_ref)

        dkv_ref[...] += jnp.concatenate(dks + dvs, axis=1)

    row = pl.BlockSpec((tq, xw), lambda i: (i, 0))
    full = pl.BlockSpec((mtok, 2 * xw), lambda i: (0, 0))
    return pl.pallas_call(
        body,
        name=name,
        grid=(t // tq,),
        in_specs=[row, full, row],
        out_specs=[row, full],
        out_shape=[jax.ShapeDtypeStruct((t, xw), BF), jax.ShapeDtypeStruct((mtok, 2 * xw), F32)],
        compiler_params=_cp("arbitrary"),
    )(xq, kv, do)


def _adam_math(w, g, m, v):
    m = ADAM_B1 * m + (1.0 - ADAM_B1) * g
    v = ADAM_B2 * v + (1.0 - ADAM_B2) * (g * g)
    m_hat = m / (1.0 - ADAM_B1 ** ADAM_STEP)
    v_hat = v / (1.0 - ADAM_B2 ** ADAM_STEP)
    delta = -ADAM_LR * (m_hat / (jnp.sqrt(v_hat) + ADAM_EPS) + ADAM_WD * w)
    return delta, m, v


def _row_tile(r, c, n_arrays, budget=24 * 1024 * 1024):
    step = 2 * SUBLANES
    cap = max(step, budget // (2 * n_arrays * c * 4))
    if r <= cap:
        return r
    best = None
    for tr in range(step, cap + 1, step):
        if r % tr == 0:
            best = tr
    assert best is not None, (r, c)
    return best


def _adamw_sum(parts, own, me, w, m, v, *, name):
    _, r, c = parts.shape
    tr = _row_tile(r, c, 12)

    def body(me_ref, p_ref, own_ref, w_ref, m_ref, v_ref, g_ref, d_ref, nm_ref, nv_ref):
        mine = jnp.full((tr, c), me_ref[0], jnp.int32)
        g = None
        for s in range(N_DEV):
            term = jnp.where(mine == s, own_ref[...], p_ref[s]).astype(F32)
            g = term if g is None else g + term
        g_ref[...] = g
        d_ref[...], nm_ref[...], nv_ref[...] = _adam_math(w_ref[...], g, m_ref[...], v_ref[...])

    blk = pl.BlockSpec((tr, c), lambda i, me_ref: (i, 0))
    out = jax.ShapeDtypeStruct((r, c), F32)
    return pl.pallas_call(
        body,
        name=name,
        grid_spec=pltpu.PrefetchScalarGridSpec(
            num_scalar_prefetch=1,
            grid=(r // tr,),
            in_specs=[
                pl.BlockSpec((N_DEV, tr, c), lambda i, me_ref: (0, i, 0)),
                pl.BlockSpec((None, tr, c), lambda i, me_ref: (me_ref[0], i, 0)),
                blk, blk, blk,
            ],
            out_specs=[blk, blk, blk, blk],
        ),
        out_shape=[out, out, out, out],
        compiler_params=_cp("parallel"),
    )(me, parts, own, w, m, v)


def _adamw_sum_pieces(parts, owns, me, w, m, v, *, name):
    r = w.shape[0]
    widths = [p.shape[2] for p in parts]
    c = sum(widths)
    k = len(parts)
    tr = _row_tile(r, c, 12)

    def body(me_ref, *refs):
        p_refs, own_refs = refs[:k], refs[k:2 * k]
        w_ref, m_ref, v_ref, g_ref, d_ref, nm_ref, nv_ref = refs[2 * k:]
        off = 0
        for p_ref, own_ref, ck in zip(p_refs, own_refs, widths):
            mine = jnp.full((tr, ck), me_ref[0], jnp.int32)
            g = None
            for s in range(N_DEV):
                term = jnp.where(mine == s, own_ref[...], p_ref[s]).astype(F32)
                g = term if g is None else g + term
            sl = slice(off, off + ck)
            g_ref[:, sl] = g
            d_ref[:, sl], nm_ref[:, sl], nv_ref[:, sl] = _adam_math(w_ref[:, sl], g, m_ref[:, sl], v_ref[:, sl])
            off += ck

    blk = pl.BlockSpec((tr, c), lambda i, me_ref: (i, 0))
    out = jax.ShapeDtypeStruct((r, c), F32)
    in_specs = [pl.BlockSpec((N_DEV, tr, ck), lambda i, me_ref: (0, i, 0)) for ck in widths]
    in_specs += [pl.BlockSpec((None, tr, ck), lambda i, me_ref: (me_ref[0], i, 0)) for ck in widths]
    return pl.pallas_call(
        body,
        name=name,
        grid_spec=pltpu.PrefetchScalarGridSpec(
            num_scalar_prefetch=1,
            grid=(r // tr,),
            in_specs=in_specs + [blk, blk, blk],
            out_specs=[blk, blk, blk, blk],
        ),
        out_shape=[out, out, out, out],
        compiler_params=_cp("parallel"),
    )(me, *parts, *owns, w, m, v)


def _to_bf16(a, *, name, dep=None):
    r, c = a.shape
    tr = _row_tile(r, c, 2)

    def body(*refs):
        refs[-1][...] = refs[0][...].astype(BF)

    blk = pl.BlockSpec((tr, c), lambda i: (i, 0))
    in_specs, args = [blk], (a,)
    if dep is not None:
        in_specs.append(pl.BlockSpec(memory_space=pl.ANY))
        args += (dep,)
    return pl.pallas_call(
        body,
        name=name,
        grid=(r // tr,),
        in_specs=in_specs,
        out_specs=blk,
        out_shape=jax.ShapeDtypeStruct((r, c), BF),
        compiler_params=_cp("parallel"),
    )(*args)


def _adamw_small(w, g, m, v, *, name):
    def body(w_ref, g_ref, m_ref, v_ref, d_ref, nm_ref, nv_ref):
        d_ref[...], nm_ref[...], nv_ref[...] = _adam_math(w_ref[...], g_ref[...], m_ref[...], v_ref[...])

    out = jax.ShapeDtypeStruct(w.shape, F32)
    return pl.pallas_call(body, name=name, out_shape=[out, out, out])(w, g, m, v)


def _mesh_pos():
    x, y, c = lax.axis_index("x"), lax.axis_index("y"), lax.axis_index("c")
    return x, y, c


def _peer(x, y, c, mask):
    px = 1 - x if mask & 4 else x
    py = 1 - y if mask & 2 else y
    pc = 1 - c if mask & 1 else c
    return (px, py, pc), 4 * px + 2 * py + pc


_HBM = pl.BlockSpec(memory_space=pltpu.HBM)
_SEM = pl.BlockSpec(memory_space=pltpu.SEMAPHORE)
_EFFECT = pltpu.SideEffectType.DATAFLOW_SIDE_EFFECTING


def _me():
    return 4 * lax.axis_index("x") + 2 * lax.axis_index("y") + lax.axis_index("c")


def _landing(own, me):
    land = lax.empty((N_DEV,) + own.shape, own.dtype)
    return lax.dynamic_update_slice(land, own[None], (me, 0, 0))


_ALL = tuple(range(1, N_DEV))
_CHIPS = (2, 4, 6)
SCATTER_DIRECT = tuple((m, m, 0, m) for m in _ALL)
GATHER_CHIPS = tuple((m, None, 0, m) for m in (1,) + _CHIPS)
GATHER_SIBLING = tuple((1, m, m, m ^ 1) for m in _CHIPS)


def _copy(src, land, send_sem, recv_sem, sem, x, y, c, entry, arriving):
    to, src_m, dst_m, arr_m = entry
    peer, _ = _peer(x, y, c, to)
    blk = lambda m: _peer(x, y, c, m)[1]
    return pltpu.make_async_remote_copy(
        src_ref=src if src_m is None else src.at[blk(src_m)],
        dst_ref=land.at[blk(arr_m if arriving else dst_m)],
        send_sem=send_sem.at[sem], recv_sem=recv_sem.at[sem], device_id=peer, device_id_type=MESH)


def _exchange_start(groups, plan, *, name, after=None):
    flat = [p for g in groups for p in g]
    from_land = flat[0][0] is None
    n, ng, nc = len(flat), len(groups), len(plan)
    n_buf = n if from_land else 2 * n

    def body(*refs):
        lands = refs[:n] if from_land else refs[n:2 * n]
        srcs = lands if from_land else refs[:n]
        sems = refs[n_buf + (after is not None):n_buf + (after is not None) + 2 * ng]
        token = refs[-1]
        x, y, c = _mesh_pos()
        w = 0
        for gi, g in enumerate(groups):
            for wi in range(len(g)):
                for k, entry in enumerate(plan):
                    _copy(srcs[w], lands[w], sems[2 * gi], sems[2 * gi + 1], wi * nc + k, x, y, c, entry,
                          False).start()
                w += 1
        token[...] = jnp.zeros_like(token)

    sem_shapes = []
    for g in groups:
        sem_shapes += [pltpu.SemaphoreType.DMA((len(g) * nc,))] * 2
    args = [] if from_land else [pltpu.with_memory_space_constraint(s, pltpu.HBM) for s, _ in flat]
    args += [pltpu.with_memory_space_constraint(l, pltpu.HBM) for _, l in flat]
    extra = [] if after is None else [after]
    outs = pl.pallas_call(
        body,
        name=name,
        in_specs=[_HBM] * n_buf + [pl.BlockSpec(memory_space=pl.ANY)] * len(extra),
        out_specs=[_SEM] * (2 * ng) + [_HBM] * n_buf + [pl.BlockSpec(memory_space=pltpu.VMEM)],
        out_shape=sem_shapes + [pltpu.HBM(a.shape, a.dtype) for a in args]
        + [jax.ShapeDtypeStruct((SUBLANES, LANES), F32)],
        input_output_aliases={i: 2 * ng + i for i in range(n_buf)},
        compiler_params=pltpu.CompilerParams(has_side_effects=_EFFECT),
    )(*args, *extra)
    sems, thru, token = outs[:2 * ng], outs[2 * ng:2 * ng + n_buf], outs[-1]
    res, w = [], 0
    for gi, g in enumerate(groups):
        m = len(g)
        srcs = [None] * m if from_land else list(thru[w:w + m])
        lands = list(thru[w:w + m]) if from_land else list(thru[n + w:n + w + m])
        res.append((sems[2 * gi], sems[2 * gi + 1], srcs, lands))
        w += m
    return res, token


def _exchange_wait(group, plan, after, *, name):
    send_sems, recv_sems, srcs_in, lands_in = group
    n, nc = len(lands_in), len(plan)
    from_land = srcs_in[0] is None
    n_buf = n if from_land else 2 * n

    def body(*refs):
        lands = refs[:n] if from_land else refs[n:2 * n]
        srcs = lands if from_land else refs[:n]
        send_sem, recv_sem = refs[n_buf], refs[n_buf + 1]
        x, y, c = _mesh_pos()
        for w in range(n):
            for k, entry in enumerate(plan):
                cp = _copy(srcs[w], lands[w], send_sem, recv_sem, w * nc + k, x, y, c, entry, True)
                cp.wait_send()
                cp.wait_recv()

    bufs = lands_in if from_land else srcs_in + lands_in
    outs = pl.pallas_call(
        body,
        name=name,
        in_specs=[_HBM] * n_buf + [_SEM, _SEM, pl.BlockSpec(memory_space=pl.ANY)],
        out_specs=[_HBM] * n_buf,
        out_shape=[pltpu.HBM(a.shape, a.dtype) for a in bufs],
        input_output_aliases={i: i for i in range(n_buf)},
        compiler_params=pltpu.CompilerParams(has_side_effects=_EFFECT),
    )(*bufs, send_sems, recv_sems, after)
    if from_land:
        return [None] * n, list(outs)
    return list(outs[:n]), list(outs[n:])


def _all_reduce_small(parts, rows, width, *, name, dep=None):
    n = len(parts)

    def body(*refs):
        ins = refs[:n]
        o_ref, pack_ref, buf_ref, send_sems, recv_sems = refs[-5:]
        x, y, c_ = _mesh_pos()
        me = 4 * x + 2 * y + c_
        pack_ref[...] = jnp.zeros_like(pack_ref)
        for ref, (arr, r0, nr) in zip(ins, parts):
            pack_ref[r0:r0 + nr, 0:arr.shape[1]] = ref[0:nr, :]
        sends, recvs = [], []
        for k in range(N_DEV - 1):
            peer, pidx = _peer(x, y, c_, k + 1)
            cp = pltpu.make_async_remote_copy(
                src_ref=pack_ref, dst_ref=buf_ref.at[me], send_sem=send_sems.at[k], recv_sem=recv_sems.at[k],
                device_id=peer, device_id_type=MESH)
            cp.start()
            sends.append(cp)
            recvs.append(pltpu.make_async_remote_copy(
                src_ref=pack_ref, dst_ref=buf_ref.at[pidx], send_sem=send_sems.at[k], recv_sem=recv_sems.at[k],
                device_id=peer, device_id_type=MESH))
        buf_ref[me] = pack_ref[...]
        for rc in recvs:
            rc.wait_recv()
        for cp in sends:
            cp.wait_send()
        acc = buf_ref[0]
        for s in range(1, N_DEV):
            acc = acc + buf_ref[s]
        o_ref[...] = acc

    vmem = pl.BlockSpec(memory_space=pltpu.VMEM)
    in_specs = [vmem] * n
    args = [p[0] for p in parts]
    if dep is not None:
        in_specs.append(pl.BlockSpec(memory_space=pl.ANY))
        args.append(dep)
    return pl.pallas_call(
        body,
        name=name,
        in_specs=in_specs,
        out_specs=vmem,
        out_shape=jax.ShapeDtypeStruct((rows, width), F32),
        scratch_shapes=[
            pltpu.VMEM((rows, width), F32),
            pltpu.VMEM((N_DEV, rows, width), F32),
            pltpu.SemaphoreType.DMA((N_DEV - 1,)),
            pltpu.SemaphoreType.DMA((N_DEV - 1,)),
        ],
    )(*args)


def _rope_tables(t):
    half = HEAD_DIM // 2
    inv_freq = ROPE_THETA ** (-jnp.arange(half, dtype=F32) / half)
    ang = jnp.arange(t, dtype=jnp.int32).astype(F32)[:, None] * inv_freq[None, :]
    cos, sin = jnp.cos(ang), jnp.sin(ang)
    cosf = jnp.concatenate([cos, cos, cos, cos], axis=1)
    sinf = jnp.concatenate([-sin, sin, -sin, sin], axis=1)
    return cosf, sinf


def _local_step(x, mem, target, gains, sinks, aw, cw, pre_w, get_w, put_g, dep0=None):
    t, d = x.shape
    nq = aw // HEAD_DIM
    kw = aw // Q_PER_KV
    z0 = aw + 2 * kw
    gb0, gc0 = z0 + cw, z0 + 2 * cw
    ga0 = z0 + 3 * cw
    gcm0 = ga0 + d
    cosf, sinf = _rope_tables(t)

    u1 = _rms_fwd(x, gains["g_mix"], name="rms_mix", dep=(cosf, sinf) if dep0 is None else (dep0, cosf, sinf))
    mem_n = _rms_fwd(mem, gains["g_mem"], name="rms_mem", dep=dep0)
    pre_w("w_in", u1)
    w_in_t = get_w("w_in", u1)
    proj = _mm(u1, w_in_t, mode="nt", tm=1024, tn=512, tk=2048, out_dtype=F32, name="mm_in")
    o_attn, q_rot, k_rot = _swa_fwd(proj, cosf, sinf, sinks, nq=nq, name="swa_fwd", dep=pre_w("conv_w8", proj))
    conv_w8 = get_w("conv_w8", o_attn)
    w_attn_proj, w_conv_proj, w_mix_out = (get_w(n, o_attn) for n in ("w_attn_proj", "w_conv_proj", "w_mix_out"))
    w_xq, w_xkv, w_xo = (get_w(n, o_attn) for n in ("w_xq", "w_xkv", "w_xo"))
    y_attn = _mm(o_attn, w_attn_proj, mode="nn", tm=1024, tn=1024, tk=1024, out_dtype=F32, name="mm_attn_proj")
    cy = _conv_fwd(proj, conv_w8, z0=z0, gb0=gb0, gc0=gc0, cw=cw, name="conv_fwd")
    y_conv, merged = _gate_fwd(cy, w_conv_proj, proj, y_attn, ga0=ga0, gc0=gcm0, name="mm_conv_proj")
    h1, u2 = _mm(merged, w_mix_out, mode="nn", tm=512, tn=d, tk=2048, out_dtype=F32, name="mm_mix_out", residual=x,
                 rms_gain=gains["g_xattn"])
    xq = _mm(u2, w_xq, mode="nn", tm=1024, tn=512, tk=2048, out_dtype=BF, name="mm_xq",
             dep=pre_w("w_ffn_in", h1))
    kv = _mm(mem_n, w_xkv, mode="nn", tm=256, tn=1024, tk=2048, out_dtype=BF, name="mm_xkv")
    o_x = _xattn_fwd(xq, kv, name="xattn_fwd")
    h2, u3 = _mm(o_x, w_xo, mode="nn", tm=512, tn=d, tk=512, out_dtype=F32, name="mm_xo", residual=h1,
                 rms_gain=gains["g_ffn"])
    w_ffn_in = get_w("w_ffn_in", xq)
    hid2, act = _ffn_in_fwd(u3, w_ffn_in, name="mm_ffn_in", dep=pre_w("w_ffn_out", u3))
    w_ffn_out = get_w("w_ffn_out", act)
    h3 = _mm(act, w_ffn_out, mode="nn", tm=512, tn=1024, tk=8192, out_dtype=F32, name="mm_ffn_out", residual=h2)

    tt = 8192
    dh3, dh3b, loss_tile, dg_final = _loss_head(h3, target, gains["g_final"], name="loss_head")
    tok = put_g("w_ffn_out", _mm(act, dh3b, mode="tn", tm=512, tn=1024, tk=tt, out_dtype=BF, name="mm_dw_ffn_out"))
    dhid2 = _ffn_out_bwd(dh3b, w_ffn_out, hid2, name="mm_dact", dep=tok)
    f2 = w_ffn_in.shape[1]
    tok = put_g("w_ffn_in", _mm(u3, dhid2, mode="tn", tm=1024, tn=f2 // N_DEV, tk=tt, out_dtype=BF,
                                name="mm_dw_ffn_in", b_planes=2, stacked=True), stacked=True)
    du3 = _mm(dhid2, w_ffn_in, mode="nt", tm=1024, tn=1024, tk=2816, out_dtype=F32, name="mm_du3", dep=tok,
              a_planes=2)
    dh2, dh2b, dg_ffn = _rms_bwd(du3, h2, gains["g_ffn"], dh3, name="rms_ffn_bwd")
    put_g("w_xo", _mm(o_x, dh2b, mode="tn", tm=512, tn=d // N_DEV, tk=tt, out_dtype=BF, name="mm_dw_xo",
                      stacked=True), stacked=True)
    do_x = _mm(dh2b, w_xo, mode="nt", tm=1024, tn=512, tk=2048, out_dtype=BF, name="mm_do_x")
    dxq, dkv = _xattn_bwd(xq, kv, do_x, name="xattn_bwd")
    put_g("w_xkv", _mm(mem_n, dkv, mode="tn", tm=1024, tn=1024, tk=256, out_dtype=BF, name="mm_dw_xkv"))
    tok = put_g("w_xq", _mm(u2, dxq, mode="tn", tm=1024, tn=512, tk=tt, out_dtype=BF, name="mm_dw_xq"))
    tok_xq = tok
    dmem_n = _mm(dkv, w_xkv, mode="nt", tm=256, tn=1024, tk=1024, out_dtype=F32, name="mm_dmem")
    _, _, dg_mem = _rms_bwd(dmem_n, mem, gains["g_mem"], None, name="rms_mem_bwd")
    dh1, dh1b, dg_xattn = _rms_bwd(dxq, h1, gains["g_xattn"], dh2, name="rms_xattn_bwd", du_w=w_xq, dep=tok_xq)
    put_g("w_mix_out", _mm(merged, dh1b, mode="tn", tm=1024, tn=1024, tk=tt, out_dtype=BF, name="mm_dw_mix_out"))
    dya, dyc, dga, dgc = _gate_bwd(dh1b, w_mix_out, proj, y_attn, y_conv, ga0=ga0, gc0=gcm0, name="mm_dmerged")
    put_g("w_attn_proj", _mm(o_attn, dya, mode="tn", tm=1024, tn=d // N_DEV, tk=tt, out_dtype=BF,
                             name="mm_dw_attn_proj", stacked=True), stacked=True)
    do_attn = _mm(dya, w_attn_proj, mode="nt", tm=1024, tn=1024, tk=2048, out_dtype=BF, name="mm_do_attn")
    tok = put_g("w_conv_proj", _mm(cy, dyc, mode="tn", tm=1024, tn=d // N_DEV, tk=tt, out_dtype=BF,
                                   name="mm_dw_conv_proj", stacked=True), stacked=True)
    dcy = _mm(dyc, w_conv_proj, mode="nt", tm=1024, tn=1024, tk=2048, out_dtype=F32, name="mm_dcy", dep=tok)
    dz, dgb, dgcv, dconv_w8 = _conv_bwd(proj, conv_w8, dcy, z0=z0, gb0=gb0, gc0=gc0, cw=cw, name="conv_bwd")
    dq, dk, dv, dsink_tile = _swa_bwd(q_rot, k_rot, proj, do_attn, cosf, sinf, sinks, nq=nq, name="swa_bwd")
    dproj = jnp.concatenate([dq, dk, dv, dz, dgb, dgcv, dga, dgc], axis=1)
    for hi in range(2):
        tok = put_g("w_in_%d" % hi, _mm(dproj, u1, mode="tn", tm=512, tn=d // 2, tk=tt, out_dtype=BF,
                                        name="mm_dw_in_%d" % hi, b_cols=(hi * (d // 2), d // 2), dep=tok,
                                        fuse_a=True))
    du1 = _mm(dproj, w_in_t, mode="nn", tm=512, tn=1024, tk=4352, out_dtype=F32, name="mm_du1", dep=tok,
              fuse_a=True)
    grad_x, _, dg_mix = _rms_bwd(du1, x, gains["g_mix"], dh1, name="rms_mix_bwd")

    small = {
        "g_mix": dg_mix, "g_xattn": dg_xattn, "g_mem": dg_mem, "g_ffn": dg_ffn, "g_final": dg_final,
        "attn_sinks": dsink_tile, "conv_w8": dconv_w8, "loss": loss_tile,
    }
    return grad_x, small


_COL_SHARDED = ("w_in", "w_attn_proj", "w_conv_proj", "w_xo", "w_ffn_in")
_ROW_SHARDED = ("w_mix_out", "w_xq", "w_xkv", "w_ffn_out")
_BIG = _COL_SHARDED + _ROW_SHARDED
_GAINS = ("g_mix", "g_xattn", "g_mem", "g_ffn", "g_final")
_GATHER_GROUPS = (("w_in",), ("conv_w8", "w_attn_proj", "w_conv_proj", "w_mix_out", "w_xq", "w_xkv", "w_xo"),
                  ("w_ffn_in",), ("w_ffn_out",))
_SCATTER_GROUPS = (("w_ffn_out",), ("w_ffn_in",), ("w_xo", "w_xq", "w_xkv"),
                   ("w_mix_out", "w_attn_proj", "w_conv_proj"), ("w_in_0",), ("w_in_1",))
_WEIGHTS = ("g_mix", "w_in", "conv_w", "attn_sinks", "w_attn_proj", "w_conv_proj", "w_mix_out", "g_xattn", "g_mem",
            "w_xq", "w_xkv", "w_xo", "g_ffn", "w_ffn_in", "w_ffn_out", "g_final")


def _unstack(g, col_sharded):
    n, r, c = g.shape
    if col_sharded:
        return jnp.transpose(g, (1, 0, 2)).reshape(r, n * c)
    return g.reshape(n * r, c)


def _stack(w, col_sharded):
    r, c = w.shape
    if col_sharded:
        return jnp.transpose(w.reshape(r, N_DEV, c // N_DEV), (1, 0, 2))
    return w.reshape(N_DEV, r // N_DEV, c)


def kernel(x, mem, g_mix, w_in, conv_w, attn_sinks, w_attn_proj, w_conv_proj, w_mix_out, g_xattn, g_mem, w_xq, w_xkv, w_xo, g_ffn, w_ffn_in, w_ffn_out, g_final, loss_target, m_g_mix, m_w_in, m_conv_w, m_attn_sinks, m_w_attn_proj, m_w_conv_proj, m_w_mix_out, m_g_xattn, m_g_mem, m_w_xq, m_w_xkv, m_w_xo, m_g_ffn, m_w_ffn_in, m_w_ffn_out, m_g_final, v_g_mix, v_w_in, v_conv_w, v_attn_sinks, v_w_attn_proj, v_w_conv_proj, v_w_mix_out, v_g_xattn, v_g_mem, v_w_xq, v_w_xkv, v_w_xo, v_g_ffn, v_w_ffn_in, v_w_ffn_out, v_g_final):
    w_ = dict(g_mix=g_mix, w_in=w_in, conv_w=conv_w, attn_sinks=attn_sinks, w_attn_proj=w_attn_proj,
              w_conv_proj=w_conv_proj, w_mix_out=w_mix_out, g_xattn=g_xattn, g_mem=g_mem, w_xq=w_xq, w_xkv=w_xkv,
              w_xo=w_xo, g_ffn=g_ffn, w_ffn_in=w_ffn_in, w_ffn_out=w_ffn_out, g_final=g_final)
    m_ = dict(g_mix=m_g_mix, w_in=m_w_in, conv_w=m_conv_w, attn_sinks=m_attn_sinks, w_attn_proj=m_w_attn_proj,
              w_conv_proj=m_w_conv_proj, w_mix_out=m_w_mix_out, g_xattn=m_g_xattn, g_mem=m_g_mem, w_xq=m_w_xq,
              w_xkv=m_w_xkv, w_xo=m_w_xo, g_ffn=m_g_ffn, w_ffn_in=m_w_ffn_in, w_ffn_out=m_w_ffn_out,
              g_final=m_g_final)
    v_ = dict(g_mix=v_g_mix, w_in=v_w_in, conv_w=v_conv_w, attn_sinks=v_attn_sinks, w_attn_proj=v_w_attn_proj,
              w_conv_proj=v_w_conv_proj, w_mix_out=v_w_mix_out, g_xattn=v_g_xattn, g_mem=v_g_mem, w_xq=v_w_xq,
              w_xkv=v_w_xkv, w_xo=v_w_xo, g_ffn=v_g_ffn, w_ffn_in=v_w_ffn_in, w_ffn_out=v_w_ffn_out,
              g_final=v_g_final)
    t, d = x.shape[1], x.shape[2]
    nq = attn_sinks.shape[-1]
    cw_shard = conv_w.shape[-1]
    cw = cw_shard * N_DEV

    def two_d(a):
        return a.reshape(a.shape[-2], a.shape[-1]) if a.ndim == 3 else a.reshape(1, a.shape[-1])

    me = _me()
    col = (set(_COL_SHARDED) | {"conv_w8"}) - {"w_in"}

    shards = {"w_in": two_d(w_in).T.astype(BF)}
    first, token = _exchange_start(
        [[(shards[n], _landing(shards[n], me)) for n in g] for g in _GATHER_GROUPS[:1]], GATHER_CHIPS,
        name="gather_start_0")
    for n in _BIG:
        if n != "w_in":
            shards[n] = _to_bf16(two_d(w_[n]), name="cast_" + n, dep=token)
    shards["conv_w8"] = jnp.zeros((SUBLANES, cw_shard), F32).at[:3].set(two_d(conv_w))
    rest, token = _exchange_start(
        [[(shards[n], _landing(shards[n], me)) for n in g] for g in _GATHER_GROUPS[1:]], GATHER_CHIPS,
        name="gather_start_1", after=token)
    gathers = first + rest
    passes, full = {}, {}

    def group_of(name):
        return [name in g for g in _GATHER_GROUPS].index(True)

    def pre_w(name, after):
        gi = group_of(name)
        _, lands = _exchange_wait(gathers[gi], GATHER_CHIPS, after, name="gather_wait_%d" % gi)
        started, tok = _exchange_start([[(None, land) for land in lands]], GATHER_SIBLING,
                                       name="gather_pass_%d" % gi)
        passes[gi] = started[0]
        return tok

    def get_w(name, after):
        if name not in full:
            gi = group_of(name)
            _, lands = _exchange_wait(passes[gi], GATHER_SIBLING, after, name="gather_pass_wait_%d" % gi)
            for n, land in zip(_GATHER_GROUPS[gi], lands):
                full[n] = _unstack(land, n in col)
        return full[name]

    pending, scatters = {}, []

    def put_g(name, dw, stacked=False):
        pending[name] = dw if stacked else _stack(dw, name in col)
        gi = [name in g for g in _SCATTER_GROUPS].index(True)
        group = _SCATTER_GROUPS[gi]
        if not all(n in pending for n in group):
            return None
        pairs = [(pending[n], lax.empty(pending[n].shape, pending[n].dtype)) for n in group]
        started, tok = _exchange_start([pairs], SCATTER_DIRECT, name="scatter_start_%d" % gi)
        scatters.append((gi, started[0]))
        return tok

    gains = {n: two_d(w_[n]) for n in _GAINS}
    grad_x, small = _local_step(
        x[0], mem[0], loss_target[0], gains, attn_sinks.reshape(nq), w_attn_proj.shape[-2], cw, pre_w, get_w, put_g,
        dep0=token)

    grads, deltas, new_m, new_v = {}, {}, {}, {}
    me1 = me.reshape(1).astype(jnp.int32)
    after, halves = grad_x, []
    for gi, started in scatters:
        mine, parts = _exchange_wait(started, SCATTER_DIRECT, after, name="scatter_wait_%d" % gi)
        for n, own, p in zip(_SCATTER_GROUPS[gi], mine, parts):
            if n.startswith("w_in_"):
                halves.append((p, own))
                if len(halves) < 2:
                    continue
                n = "w_in"
                g, dl, nm, nv = (a.T for a in _adamw_sum_pieces(
                    [h[0] for h in halves], [h[1] for h in halves], me1, two_d(w_[n]).T, two_d(m_[n]).T,
                    two_d(v_[n]).T, name="adamw_" + n))
            else:
                g, dl, nm, nv = _adamw_sum(p, own, me1, two_d(w_[n]), two_d(m_[n]), two_d(v_[n]), name="adamw_" + n)
            shape = w_[n].shape
            grads[n], deltas[n], new_m[n], new_v[n] = (a.reshape(shape) for a in (g, dl, nm, nv))
            after = g

    parts = [(small[n], i, 1) for i, n in enumerate(_GAINS)]
    parts += [(small["attn_sinks"], 5, 1), (small["loss"], 6, 1), (small["conv_w8"], 8, 3)]
    red = _all_reduce_small(parts, 2 * SUBLANES, max(d, cw), name="reduce_small", dep=after)
    loss = red[6, 0]
    small_g = {n: red[i:i + 1, :d] for i, n in enumerate(_GAINS)}
    small_g["attn_sinks"] = red[5:6, :nq]
    small_g["conv_w"] = lax.dynamic_slice(red, (8, me * cw_shard), (3, cw_shard))
    for n in _GAINS + ("attn_sinks", "conv_w"):
        shape = w_[n].shape
        g = small_g[n]
        dl, nm, nv = _adamw_small(two_d(w_[n]), g, two_d(m_[n]), two_d(v_[n]), name="adamw_" + n)
        grads[n], deltas[n], new_m[n], new_v[n] = (a.reshape(shape) for a in (g, dl, nm, nv))

    return (loss, grad_x[None], *[grads[n] for n in _WEIGHTS], *[deltas[n] for n in _WEIGHTS],
            *[new_m[n] for n in _WEIGHTS], *[new_v[n] for n in _WEIGHTS])
```
